```python
import math
import jax, jax.numpy as jnp
from jax import lax
import numpy as np

D_MODEL = 1024
BATCH = 8
SEQ = 8192
DEPTH = 2

HEAD_DIM = 64
N_HEADS = D_MODEL // HEAD_DIM
HEADS_A = N_HEADS // 2
HEADS_B = N_HEADS - HEADS_A
KV_A = 2
KV_B = 2
G_A = HEADS_A // KV_A
G_B = HEADS_B // KV_B
Q_A = HEADS_A * HEAD_DIM
KVD_A = KV_A * HEAD_DIM
Q_B = HEADS_B * HEAD_DIM
KVD_B = KV_B * HEAD_DIM
IN_COLS = Q_A + 2 * KVD_A + Q_B + 2 * KVD_B
IN_SPLITS = (Q_A, Q_A + KVD_A, Q_A + 2 * KVD_A, Q_A + 2 * KVD_A + Q_B, Q_A + 2 * KVD_A + Q_B + KVD_B)
MIX_WIDTH = Q_A + Q_B

GRID_W = 64
ROPE_THETA = 10000.0
Q_BLOCK = 128
WINDOW = 128
N_BUCKETS = 32
MAX_DISTANCE = 128
D_FF = 2816
CONV_W = 3
ALPHA = (2.0 * DEPTH) ** 0.25
BETA = (8.0 * DEPTH) ** -0.25
RMS_EPS = 1e-6
LN_EPS = 1e-5

kernel_name = "hymba_axial_swa_convglu_deepnorm_encoder"


def rms_norm(x, g):
    xf = x.astype(jnp.float32)
    y = xf * lax.rsqrt(jnp.mean(xf * xf, axis=-1, keepdims=True) + RMS_EPS)
    return (y * g.astype(jnp.float32)).astype(x.dtype)


def layer_norm(x, g, b):
    xf = x.astype(jnp.float32)
    mu = jnp.mean(xf, axis=-1, keepdims=True)
    var = jnp.mean(jnp.square(xf - mu), axis=-1, keepdims=True)
    y = (xf - mu) * lax.rsqrt(var + LN_EPS)
    return (y * g.astype(jnp.float32) + b.astype(jnp.float32)).astype(x.dtype)


def axial_rope_tables(seq_len):
    rows_n = seq_len // GRID_W
    row = jnp.repeat(jnp.arange(rows_n, dtype=jnp.float32), GRID_W)
    col = jnp.tile(jnp.arange(GRID_W, dtype=jnp.float32), rows_n)
    half = HEAD_DIM // 2
    inv_freq = ROPE_THETA ** (-jnp.arange(0, half, 2, dtype=jnp.float32) / half)
    ang = jnp.concatenate([row[:, None] * inv_freq, col[:, None] * inv_freq], axis=-1)
    return jnp.cos(ang), jnp.sin(ang)


def apply_rope(x, cos, sin):
    xf = x.astype(jnp.float32).reshape(x.shape[:-1] + (HEAD_DIM // 2, 2))
    x0, x1 = xf[..., 0], xf[..., 1]
    c = cos[None, :, None, :]
    s = sin[None, :, None, :]
    out = jnp.stack([x0 * c - x1 * s, x0 * s + x1 * c], axis=-1).reshape(x.shape)
    return out.astype(x.dtype)


def global_gqa(q, k, v):
    B, S = q.shape[0], q.shape[1]
    nb = S // Q_BLOCK
    qb = q.reshape(B, nb, Q_BLOCK, KV_A, G_A, HEAD_DIM).transpose(1, 0, 2, 3, 4, 5)
    scale = HEAD_DIM ** -0.5

    def block(qi):
        s = jnp.einsum('bqhgd,bkhd->bhgqk', qi, k, preferred_element_type=jnp.float32) * scale
        p = jax.nn.softmax(s, axis=-1)
        return jnp.einsum('bhgqk,bkhd->bqhgd', p.astype(v.dtype), v)

    o = lax.map(block, qb)
    return o.transpose(1, 0, 2, 3, 4, 5).reshape(B, S, Q_A)


def t5_bucket(rel):
    half = N_BUCKETS // 2
    max_exact = half // 2
    bucket = jnp.where(rel > 0, half, 0)
    rp = jnp.abs(rel)
    rpf = jnp.maximum(rp, 1).astype(jnp.float32)
    large = max_exact + (jnp.log(rpf / max_exact) / math.log(MAX_DISTANCE / max_exact)
                         * (half - max_exact)).astype(jnp.int32)
    large = jnp.minimum(large, half - 1)
    return bucket + jnp.where(rp < max_exact, rp, large)


def window_gqa_sink(q, k, v, rel_bias, sink):
    B, S = q.shape[0], q.shape[1]
    nb = S // Q_BLOCK
    scale = HEAD_DIM ** -0.5
    qb = q.reshape(B, nb, Q_BLOCK, KV_B, G_B, HEAD_DIM)
    pad = ((0, 0), (Q_BLOCK, Q_BLOCK), (0, 0), (0, 0))
    kp = jnp.pad(k, pad).reshape(B, nb + 2, Q_BLOCK, KV_B, HEAD_DIM)
    vp = jnp.pad(v, pad).reshape(B, nb + 2, Q_BLOCK, KV_B, HEAD_DIM)
    kb = jnp.concatenate([kp[:, :-2], kp[:, 1:-1], kp[:, 2:]], axis=2)
    vb = jnp.concatenate([vp[:, :-2], vp[:, 1:-1], vp[:, 2:]], axis=2)
    qpos = jnp.arange(Q_BLOCK, dtype=jnp.int32)
    kpos = jnp.arange(3 * Q_BLOCK, dtype=jnp.int32) - Q_BLOCK
    rel = kpos[None, :] - qpos[:, None]
    bias = rel_bias.astype(jnp.float32)[t5_bucket(rel)]
    bias = bias.transpose(2, 0, 1).reshape(KV_B, G_B, Q_BLOCK, 3 * Q_BLOCK)
    kabs = jnp.arange(nb, dtype=jnp.int32)[:, None] * Q_BLOCK + kpos[None, :]
    valid = (jnp.abs(rel) <= WINDOW)[None] & ((kabs >= 0) & (kabs < S))[:, None, :]
    s = jnp.einsum('bnqhgd,bnkhd->bnhgqk', qb, kb, preferred_element_type=jnp.float32) * scale + bias
    s = jnp.where(valid[None, :, None, None], s, -jnp.inf)
    sink_logit = jnp.broadcast_to(sink.astype(jnp.float32).reshape(KV_B, G_B)[None, None, :, :, None, None],
                                  s.shape[:-1] + (1,))
    p = jax.nn.softmax(jnp.concatenate([s, sink_logit], axis=-1), axis=-1)[..., :-1]
    o = jnp.einsum('bnhgqk,bnkhd->bnqhgd', p.astype(v.dtype), vb)
    return o.reshape(B, S, Q_B)


def token_mixer(x, rel_bias, cos, sin, w_in, q_norm, k_norm, sink, out_norm_a, out_norm_b, w_out):
    B, S, _ = x.shape
    h = jnp.einsum('bsd,de->bse', x, w_in)
    qa, ka, va, qb, kb, vb = jnp.split(h, IN_SPLITS, axis=-1)
    qa = apply_rope(rms_norm(qa.reshape(B, S, HEADS_A, HEAD_DIM), q_norm), cos, sin)
    ka = apply_rope(rms_norm(ka.reshape(B, S, KV_A, HEAD_DIM), k_norm), cos, sin)
    va = va.reshape(B, S, KV_A, HEAD_DIM)
    ya = rms_norm(global_gqa(qa, ka, va), out_norm_a)
    yb = window_gqa_sink(qb.reshape(B, S, HEADS_B, HEAD_DIM), kb.reshape(B, S, KV_B, HEAD_DIM),
                         vb.reshape(B, S, KV_B, HEAD_DIM), rel_bias, sink)
    yb = rms_norm(yb, out_norm_b)
    return jnp.einsum('bse,ed->bsd', jnp.concatenate([ya, yb], axis=-1), w_out)


def conv_glu(x, w_gate, w_up, conv_w, conv_b, w_down):
    S = x.shape[1]
    g = jnp.einsum('bsd,df->bsf', x, w_gate)
    u = jnp.einsum('bsd,df->bsf', x, w_up)
    r = CONV_W // 2
    gp = jnp.pad(g, ((0, 0), (r, r), (0, 0)))
    gc = conv_b
    for j in range(CONV_W):
        gc = gc + gp[:, j:j + S] * conv_w[j]
    return jnp.einsum('bsf,fd->bsd', jax.nn.gelu(gc) * u, w_down)


def _fwd_setup_inputs(seed: int = 0) -> dict:
    key = jax.random.key(seed)
    ks = jax.random.split(key, 20)
    f32 = jnp.float32
    L = DEPTH

    def nrm(k, shape, scale):
        return jax.random.normal(k, shape, f32) * scale

    return {
        "x": nrm(ks[0], (BATCH, SEQ, D_MODEL), 1.0),
        "rel_bias": nrm(ks[1], (N_BUCKETS, HEADS_B), 0.1),
        "w_in": nrm(ks[2], (L, D_MODEL, IN_COLS), D_MODEL ** -0.5),
        "q_norm": 1.0 + nrm(ks[3], (L, HEAD_DIM), 0.05),
        "k_norm": 1.0 + nrm(ks[4], (L, HEAD_DIM), 0.05),
        "sink": nrm(ks[5], (L, HEADS_B), 0.5),
        "out_norm_a": 1.0 + nrm(ks[6], (L, Q_A), 0.05),
        "out_norm_b": 1.0 + nrm(ks[7], (L, Q_B), 0.05),
        "w_out": nrm(ks[8], (L, MIX_WIDTH, D_MODEL), BETA * MIX_WIDTH ** -0.5),
        "ln1_g": 1.0 + nrm(ks[9], (L, D_MODEL), 0.05),
        "ln1_b": nrm(ks[10], (L, D_MODEL), 0.01),
        "w_gate": nrm(ks[11], (L, D_MODEL, D_FF), D_MODEL ** -0.5),
        "w_up": nrm(ks[12], (L, D_MODEL, D_FF), D_MODEL ** -0.5),
        "conv_w": nrm(ks[13], (L, CONV_W, D_FF), CONV_W ** -0.5),
        "conv_b": nrm(ks[14], (L, D_FF), 0.01),
        "w_down": nrm(ks[15], (L, D_FF, D_MODEL), BETA * D_FF ** -0.5),
        "ln2_g": 1.0 + nrm(ks[16], (L, D_MODEL), 0.05),
        "ln2_b": nrm(ks[17], (L, D_MODEL), 0.01),
    }


def _fwd_reference(x, rel_bias, w_in, q_norm, k_norm, sink, out_norm_a, out_norm_b, w_out,
              ln1_g, ln1_b, w_gate, w_up, conv_w, conv_b, w_down, ln2_g, ln2_b):
    cos, sin = axial_rope_tables(x.shape[1])
    for l in range(DEPTH):
        mix = token_mixer(x, rel_bias, cos, sin, w_in[l], q_norm[l], k_norm[l], sink[l],
                          out_norm_a[l], out_norm_b[l], w_out[l])
        x = layer_norm(ALPHA * x + mix, ln1_g[l], ln1_b[l])
        ffn = conv_glu(x, w_gate[l], w_up[l], conv_w[l], conv_b[l], w_down[l])
        x = layer_norm(ALPHA * x + ffn, ln2_g[l], ln2_b[l])
    return x


import jax as _jax
import jax.numpy as _jnp

TWIN_FORMAT = 'train_step'
FWD_PARAMS = ['x', 'rel_bias', 'w_in', 'q_norm', 'k_norm', 'sink', 'out_norm_a', 'out_norm_b', 'w_out', 'ln1_g', 'ln1_b', 'w_gate', 'w_up', 'conv_w', 'conv_b', 'w_down', 'ln2_g', 'ln2_b']
TWIN_WEIGHTS = ['rel_bias', 'w_in', 'q_norm', 'k_norm', 'sink', 'out_norm_a', 'out_norm_b', 'w_out', 'ln1_g', 'ln1_b', 'w_gate', 'w_up', 'conv_w', 'conv_b', 'w_down', 'ln2_g', 'ln2_b']
TWIN_DIFF_INPUT = 'x'
TWIN_INPUTS = ['x', 'rel_bias', 'w_in', 'q_norm', 'k_norm', 'sink', 'out_norm_a', 'out_norm_b', 'w_out', 'ln1_g', 'ln1_b', 'w_gate', 'w_up', 'conv_w', 'conv_b', 'w_down', 'ln2_g', 'ln2_b', 'loss_target', 'm_rel_bias', 'm_w_in', 'm_q_norm', 'm_k_norm', 'm_sink', 'm_out_norm_a', 'm_out_norm_b', 'm_w_out', 'm_ln1_g', 'm_ln1_b', 'm_w_gate', 'm_w_up', 'm_conv_w', 'm_conv_b', 'm_w_down', 'm_ln2_g', 'm_ln2_b', 'v_rel_bias', 'v_w_in', 'v_q_norm', 'v_k_norm', 'v_sink', 'v_out_norm_a', 'v_out_norm_b', 'v_w_out', 'v_ln1_g', 'v_ln1_b', 'v_w_gate', 'v_w_up', 'v_conv_w', 'v_conv_b', 'v_w_down', 'v_ln2_g', 'v_ln2_b']
TWIN_OUTPUTS = ['loss', 'grad_x', 'grad_rel_bias', 'grad_w_in', 'grad_q_norm', 'grad_k_norm', 'grad_sink', 'grad_out_norm_a', 'grad_out_norm_b', 'grad_w_out', 'grad_ln1_g', 'grad_ln1_b', 'grad_w_gate', 'grad_w_up', 'grad_conv_w', 'grad_conv_b', 'grad_w_down', 'grad_ln2_g', 'grad_ln2_b', 'delta_rel_bias', 'delta_w_in', 'delta_q_norm', 'delta_k_norm', 'delta_sink', 'delta_out_norm_a', 'delta_out_norm_b', 'delta_w_out', 'delta_ln1_g', 'delta_ln1_b', 'delta_w_gate', 'delta_w_up', 'delta_conv_w', 'delta_conv_b', 'delta_w_down', 'delta_ln2_g', 'delta_ln2_b', 'new_m_rel_bias', 'new_m_w_in', 'new_m_q_norm', 'new_m_k_norm', 'new_m_sink', 'new_m_out_norm_a', 'new_m_out_norm_b', 'new_m_w_out', 'new_m_ln1_g', 'new_m_ln1_b', 'new_m_w_gate', 'new_m_w_up', 'new_m_conv_w', 'new_m_conv_b', 'new_m_w_down', 'new_m_ln2_g', 'new_m_ln2_b', 'new_v_rel_bias', 'new_v_w_in', 'new_v_q_norm', 'new_v_k_norm', 'new_v_sink', 'new_v_out_norm_a', 'new_v_out_norm_b', 'new_v_w_out', 'new_v_ln1_g', 'new_v_ln1_b', 'new_v_w_gate', 'new_v_w_up', 'new_v_conv_w', 'new_v_conv_b', 'new_v_w_down', 'new_v_ln2_g', 'new_v_ln2_b']
TWIN_LEAF_KINDS = {'loss': 'loss', 'grad_x': 'grad_x', 'grad_rel_bias': 'grad_w', 'grad_w_in': 'grad_w', 'grad_q_norm': 'grad_w', 'grad_k_norm': 'grad_w', 'grad_sink': 'grad_w', 'grad_out_norm_a': 'grad_w', 'grad_out_norm_b': 'grad_w', 'grad_w_out': 'grad_w', 'grad_ln1_g': 'grad_w', 'grad_ln1_b': 'grad_w', 'grad_w_gate': 'grad_w', 'grad_w_up': 'grad_w', 'grad_conv_w': 'grad_w', 'grad_conv_b': 'grad_w', 'grad_w_down': 'grad_w', 'grad_ln2_g': 'grad_w', 'grad_ln2_b': 'grad_w', 'delta_rel_bias': 'delta_w', 'delta_w_in': 'delta_w', 'delta_q_norm': 'delta_w', 'delta_k_norm': 'delta_w', 'delta_sink': 'delta_w', 'delta_out_norm_a': 'delta_w', 'delta_out_norm_b': 'delta_w', 'delta_w_out': 'delta_w', 'delta_ln1_g': 'delta_w', 'delta_ln1_b': 'delta_w', 'delta_w_gate': 'delta_w', 'delta_w_up': 'delta_w', 'delta_conv_w': 'delta_w', 'delta_conv_b': 'delta_w', 'delta_w_down': 'delta_w', 'delta_ln2_g': 'delta_w', 'delta_ln2_b': 'delta_w', 'new_m_rel_bias': 'new_m', 'new_m_w_in': 'new_m', 'new_m_q_norm': 'new_m', 'new_m_k_norm': 'new_m', 'new_m_sink': 'new_m', 'new_m_out_norm_a': 'new_m', 'new_m_out_norm_b': 'new_m', 'new_m_w_out': 'new_m', 'new_m_ln1_g': 'new_m', 'new_m_ln1_b': 'new_m', 'new_m_w_gate': 'new_m', 'new_m_w_up': 'new_m', 'new_m_conv_w': 'new_m', 'new_m_conv_b': 'new_m', 'new_m_w_down': 'new_m', 'new_m_ln2_g': 'new_m', 'new_m_ln2_b': 'new_m', 'new_v_rel_bias': 'new_v', 'new_v_w_in': 'new_v', 'new_v_q_norm': 'new_v', 'new_v_k_norm': 'new_v', 'new_v_sink': 'new_v', 'new_v_out_norm_a': 'new_v', 'new_v_out_norm_b': 'new_v', 'new_v_w_out': 'new_v', 'new_v_ln1_g': 'new_v', 'new_v_ln1_b': 'new_v', 'new_v_w_gate': 'new_v', 'new_v_w_up': 'new_v', 'new_v_conv_w': 'new_v', 'new_v_conv_b': 'new_v', 'new_v_w_down': 'new_v', 'new_v_ln2_g': 'new_v', 'new_v_ln2_b': 'new_v'}


def _forward(args):
    return _fwd_reference(*[args[k] for k in FWD_PARAMS])


def _output_shape():
    def fwd():
        inp = _fwd_setup_inputs(0)
        return _fwd_reference(*[inp[k] for k in FWD_PARAMS])
    out = _jax.eval_shape(fwd)
    return out.shape, out.dtype

N_MICROBATCH = 1
ADAM_LR = 0.001
ADAM_B1 = 0.9
ADAM_B2 = 0.999
ADAM_EPS = 1e-08
ADAM_WD = 0.01
ADAM_STEP = 10
PER_EXAMPLE_BATCH_AXIS = {'x': 0, 'loss_target': 0}
SHARED_INPUTS = []
_WEIGHT_DTYPES = {'rel_bias': _jnp.float32, 'w_in': _jnp.float32, 'q_norm': _jnp.float32, 'k_norm': _jnp.float32, 'sink': _jnp.float32, 'out_norm_a': _jnp.float32, 'out_norm_b': _jnp.float32, 'w_out': _jnp.float32, 'ln1_g': _jnp.float32, 'ln1_b': _jnp.float32, 'w_gate': _jnp.float32, 'w_up': _jnp.float32, 'conv_w': _jnp.float32, 'conv_b': _jnp.float32, 'w_down': _jnp.float32, 'ln2_g': _jnp.float32, 'ln2_b': _jnp.float32}
MOMENT_SCALE = {'rel_bias': 1.306190e-01, 'w_in': 1.460906e-01, 'q_norm': 2.609259e-01, 'k_norm': 2.791400e-01, 'sink': 4.472493e-03, 'out_norm_a': 1.780760e-01, 'out_norm_b': 1.240081e-01, 'w_out': 3.111473e-01, 'ln1_g': 5.407390e+00, 'ln1_b': 1.314286e+00, 'w_gate': 3.625531e-02, 'w_up': 3.661654e-02, 'conv_w': 3.632487e-02, 'conv_b': 3.861160e-02, 'w_down': 1.215507e-01, 'ln2_g': 4.591686e+01, 'ln2_b': 3.699905e+00}


def _to_microbatches(a, axis):
    t = _jnp.moveaxis(a, axis, 0)
    t = t.reshape((N_MICROBATCH, t.shape[0] // N_MICROBATCH) + t.shape[1:])
    return _jnp.moveaxis(t, 1, axis + 1)


def setup_inputs(seed: int = 0) -> dict:
    inp = _fwd_setup_inputs(seed)
    key = _jax.random.fold_in(_jax.random.key(seed), 7919)
    shape, _ = _output_shape()
    out = dict(inp)
    out["loss_target"] = _jax.random.normal(_jax.random.fold_in(key, 0), shape, _jnp.float32)
    for i, name in enumerate(TWIN_WEIGHTS):
        w = inp[name].astype(_jnp.float32)
        if MOMENT_SCALE is None:
            s = _jnp.sqrt(_jnp.mean(_jnp.square(w)) + 1e-30)
        else:
            s = MOMENT_SCALE[name]
        km, kv = _jax.random.split(_jax.random.fold_in(key, i + 1))
        out[name] = w
        out["m_" + name] = s * _jax.random.normal(km, w.shape, _jnp.float32)
        out["v_" + name] = (s * s) * _jax.random.uniform(kv, w.shape, _jnp.float32, 0.5, 1.5)
    if N_MICROBATCH > 1:
        for name, axis in PER_EXAMPLE_BATCH_AXIS.items():
            out[name] = _to_microbatches(out[name], axis)
    return {'x': out['x'], 'rel_bias': out['rel_bias'], 'w_in': out['w_in'], 'q_norm': out['q_norm'], 'k_norm': out['k_norm'], 'sink': out['sink'], 'out_norm_a': out['out_norm_a'], 'out_norm_b': out['out_norm_b'], 'w_out': out['w_out'], 'ln1_g': out['ln1_g'], 'ln1_b': out['ln1_b'], 'w_gate': out['w_gate'], 'w_up': out['w_up'], 'conv_w': out['conv_w'], 'conv_b': out['conv_b'], 'w_down': out['w_down'], 'ln2_g': out['ln2_g'], 'ln2_b': out['ln2_b'], 'loss_target': out['loss_target'], 'm_rel_bias': out['m_rel_bias'], 'm_w_in': out['m_w_in'], 'm_q_norm': out['m_q_norm'], 'm_k_norm': out['m_k_norm'], 'm_sink': out['m_sink'], 'm_out_norm_a': out['m_out_norm_a'], 'm_out_norm_b': out['m_out_norm_b'], 'm_w_out': out['m_w_out'], 'm_ln1_g': out['m_ln1_g'], 'm_ln1_b': out['m_ln1_b'], 'm_w_gate': out['m_w_gate'], 'm_w_up': out['m_w_up'], 'm_conv_w': out['m_conv_w'], 'm_conv_b': out['m_conv_b'], 'm_w_down': out['m_w_down'], 'm_ln2_g': out['m_ln2_g'], 'm_ln2_b': out['m_ln2_b'], 'v_rel_bias': out['v_rel_bias'], 'v_w_in': out['v_w_in'], 'v_q_norm': out['v_q_norm'], 'v_k_norm': out['v_k_norm'], 'v_sink': out['v_sink'], 'v_out_norm_a': out['v_out_norm_a'], 'v_out_norm_b': out['v_out_norm_b'], 'v_w_out': out['v_w_out'], 'v_ln1_g': out['v_ln1_g'], 'v_ln1_b': out['v_ln1_b'], 'v_w_gate': out['v_w_gate'], 'v_w_up': out['v_w_up'], 'v_conv_w': out['v_conv_w'], 'v_conv_b': out['v_conv_b'], 'v_w_down': out['v_w_down'], 'v_ln2_g': out['v_ln2_g'], 'v_ln2_b': out['v_ln2_b']}


def _loss(weights, diff, rest, loss_target):
    with _jax.named_scope("forward"):
        args = {**rest, TWIN_DIFF_INPUT: diff, **{k: w.astype(_WEIGHT_DTYPES[k]) for k, w in weights.items()}}
        y = _forward(args)
    with _jax.named_scope("loss_head"):
        err = _jnp.square(y.astype(_jnp.float32) - loss_target)
        return 0.5 * _jnp.sum(_jnp.mean(err, axis=-1)) if err.ndim else 0.5 * err


def _adamw(w, g, m, v):
    m = ADAM_B1 * m + (1.0 - ADAM_B1) * g
    v = ADAM_B2 * v + (1.0 - ADAM_B2) * _jnp.square(g)
    m_hat = m / (1.0 - ADAM_B1 ** ADAM_STEP)
    v_hat = v / (1.0 - ADAM_B2 ** ADAM_STEP)
    delta = -ADAM_LR * (m_hat / (_jnp.sqrt(v_hat) + ADAM_EPS) + ADAM_WD * w)
    return delta, m, v


def reference(x, rel_bias, w_in, q_norm, k_norm, sink, out_norm_a, out_norm_b, w_out, ln1_g, ln1_b, w_gate, w_up, conv_w, conv_b, w_down, ln2_g, ln2_b, loss_target, m_rel_bias, m_w_in, m_q_norm, m_k_norm, m_sink, m_out_norm_a, m_out_norm_b, m_w_out, m_ln1_g, m_ln1_b, m_w_gate, m_w_up, m_conv_w, m_conv_b, m_w_down, m_ln2_g, m_ln2_b, v_rel_bias, v_w_in, v_q_norm, v_k_norm, v_sink, v_out_norm_a, v_out_norm_b, v_w_out, v_ln1_g, v_ln1_b, v_w_gate, v_w_up, v_conv_w, v_conv_b, v_w_down, v_ln2_g, v_ln2_b):
    given = dict(x=x, rel_bias=rel_bias, w_in=w_in, q_norm=q_norm, k_norm=k_norm, sink=sink, out_norm_a=out_norm_a, out_norm_b=out_norm_b, w_out=w_out, ln1_g=ln1_g, ln1_b=ln1_b, w_gate=w_gate, w_up=w_up, conv_w=conv_w, conv_b=conv_b, w_down=w_down, ln2_g=ln2_g, ln2_b=ln2_b, loss_target=loss_target, m_rel_bias=m_rel_bias, m_w_in=m_w_in, m_q_norm=m_q_norm, m_k_norm=m_k_norm, m_sink=m_sink, m_out_norm_a=m_out_norm_a, m_out_norm_b=m_out_norm_b, m_w_out=m_w_out, m_ln1_g=m_ln1_g, m_ln1_b=m_ln1_b, m_w_gate=m_w_gate, m_w_up=m_w_up, m_conv_w=m_conv_w, m_conv_b=m_conv_b, m_w_down=m_w_down, m_ln2_g=m_ln2_g, m_ln2_b=m_ln2_b, v_rel_bias=v_rel_bias, v_w_in=v_w_in, v_q_norm=v_q_norm, v_k_norm=v_k_norm, v_sink=v_sink, v_out_norm_a=v_out_norm_a, v_out_norm_b=v_out_norm_b, v_w_out=v_w_out, v_ln1_g=v_ln1_g, v_ln1_b=v_ln1_b, v_w_gate=v_w_gate, v_w_up=v_w_up, v_conv_w=v_conv_w, v_conv_b=v_conv_b, v_w_down=v_w_down, v_ln2_g=v_ln2_g, v_ln2_b=v_ln2_b)
    weights = {n: given[n] for n in TWIN_WEIGHTS}
    shared = {n: given[n] for n in SHARED_INPUTS}
    per_example = {n: given[n] for n in ['x']}
    grad_fn = _jax.value_and_grad(_loss, argnums=(0, 1))

    def one_microbatch(ex, loss_target):
        ex = dict(ex)
        diff = ex.pop(TWIN_DIFF_INPUT)
        return grad_fn(weights, diff, {**shared, **ex}, loss_target)

    if N_MICROBATCH == 1:
        loss, (grad_w, grad_x) = one_microbatch(per_example, given["loss_target"])
    else:
        def body(carry, xs):
            loss_sum, grad_sum = carry
            l_k, (gw_k, gx_k) = one_microbatch(xs[0], xs[1])
            with _jax.named_scope("update"):
                return (loss_sum + l_k, _jax.tree.map(_jnp.add, grad_sum, gw_k)), gx_k

        init = (_jnp.zeros((), _jnp.float32), _jax.tree.map(_jnp.zeros_like, weights))
        (loss, grad_w), grad_x = _jax.lax.scan(body, init, (per_example, given["loss_target"]))
    with _jax.named_scope("update"):
        delta_w, new_m, new_v = {}, {}, {}
        for n in TWIN_WEIGHTS:
            delta_w[n], new_m[n], new_v[n] = _adamw(weights[n], grad_w[n], given["m_" + n], given["v_" + n])
    return (loss, grad_x, *[grad_w[n] for n in TWIN_WEIGHTS], *[delta_w[n] for n in TWIN_WEIGHTS],
            *[new_m[n] for n in TWIN_WEIGHTS], *[new_v[n] for n in TWIN_WEIGHTS])
```

```python
import functools
import math

import numpy as np
import jax
import jax.numpy as jnp
from jax import lax
from jax.experimental import pallas as pl
from jax.experimental.pallas import tpu as pltpu

F32 = jnp.float32
BF16 = jnp.bfloat16

D_MODEL = 1024
DEPTH = 2
HEAD_DIM = 64
Q_W = 512
KV_W = 128
IN_COLS = 2 * (Q_W + 2 * KV_W)
N_SHARD = 4
IN_SH = IN_COLS // N_SHARD
OUT_SH = D_MODEL // N_SHARD
D_FF = 2816
FF_SH = D_FF // N_SHARD
Q_BLOCK = 128
WINDOW = 128
N_BUCKETS = 32
MAX_DISTANCE = 128
GRID_W = 64
ROPE_THETA = 10000.0
ALPHA = (2.0 * DEPTH) ** 0.25
RMS_EPS = 1e-6
LN_EPS = 1e-5
NEG = -1e30
LANES = 128
VMEM_LIMIT = 56 * 1024 * 1024

ADAM_LR = 0.001
ADAM_B1 = 0.9
ADAM_B2 = 0.999
ADAM_EPS = 1e-08
ADAM_WD = 0.01
ADAM_STEP = 10

_NN = (((1,), (0,)), ((), ()))
_NT = (((1,), (1,)), ((), ()))
_TN = (((0,), (0,)), ((), ()))


def _dot(a, b, dims):
    return lax.dot_general(a.astype(BF16), b.astype(BF16), dims, preferred_element_type=F32)


def _cparams(sem, vmem=VMEM_LIMIT):
    return pltpu.CompilerParams(dimension_semantics=sem, vmem_limit_bytes=vmem)


def _pair_perm():
    cols = []
    for i in range(4):
        cols += list(range(64 * i, 64 * i + 64)) + list(range(64 * (4 + i), 64 * (4 + i) + 64))
    return np.array(cols, np.int32)


_PQ = _pair_perm()
_IN_PERM = np.concatenate([_PQ, np.arange(512, 768), 768 + _PQ, np.arange(1280, 1536)]).astype(np.int32)
_IN_INV = np.argsort(_IN_PERM).astype(np.int32)
_MIX_PERM = np.concatenate([_PQ, 512 + _PQ]).astype(np.int32)
_MIX_INV = np.argsort(_MIX_PERM).astype(np.int32)
_PQ_INV = np.argsort(_PQ).astype(np.int32)


def _matmul(a, b, *, dims, grid, a_spec, b_spec, o_spec, out_shape, acc_shape, name, res=None,
            res_spec=None, res_scale=1.0):
    nk = grid[-1]
    kax = len(grid) - 1

    def body(*refs):
        if res is None:
            a_ref, b_ref, o_ref, acc = refs
            r_ref = None
        else:
            a_ref, b_ref, r_ref, o_ref, acc = refs
        k = pl.program_id(kax)

        @pl.when(k == 0)
        def _():
            acc[...] = jnp.zeros_like(acc)

        acc[...] += _dot(a_ref[...], b_ref[...], dims)

        @pl.when(k == nk - 1)
        def _():
            o = acc[...]
            if r_ref is not None:
                o = o + res_scale * r_ref[...]
            o_ref[...] = o.astype(o_ref.dtype)

    in_specs = [a_spec, b_spec] + ([res_spec] if res is not None else [])
    args = (a, b) + ((res,) if res is not None else ())
    sem = ("parallel",) * kax + ("arbitrary",)
    return pl.pallas_call(
        body, grid=grid, in_specs=in_specs, out_specs=o_spec, out_shape=out_shape,
        scratch_shapes=[pltpu.VMEM(acc_shape, F32)], compiler_params=_cparams(sem), name=name,
    )(*args)


def _mm_nn(a, b, out_dtype, name, tm=512, res=None, res_scale=1.0):
    m, kd = a.shape
    n = b.shape[1]
    return _matmul(
        a, b, dims=_NN, grid=(m // tm, 1),
        a_spec=pl.BlockSpec((tm, kd), lambda i, k: (i, 0)),
        b_spec=pl.BlockSpec((kd, n), lambda i, k: (0, 0)),
        o_spec=pl.BlockSpec((tm, n), lambda i, k: (i, 0)),
        out_shape=jax.ShapeDtypeStruct((m, n), out_dtype), acc_shape=(tm, n), name=name,
        res=res, res_spec=pl.BlockSpec((tm, n), lambda i, k: (i, 0)), res_scale=res_scale)


def _mm_nt(a, b, out_dtype, name, tm=512, res=None, res_scale=1.0):
    m, kd = a.shape
    n = b.shape[0]
    return _matmul(
        a, b, dims=_NT, grid=(m // tm, 1),
        a_spec=pl.BlockSpec((tm, kd), lambda i, k: (i, 0)),
        b_spec=pl.BlockSpec((n, kd), lambda i, k: (0, 0)),
        o_spec=pl.BlockSpec((tm, n), lambda i, k: (i, 0)),
        out_shape=jax.ShapeDtypeStruct((m, n), out_dtype), acc_shape=(tm, n), name=name,
        res=res, res_spec=pl.BlockSpec((tm, n), lambda i, k: (i, 0)), res_scale=res_scale)


def _mm_tn(a, b, name, tk=512, tn=None):
    t, m = a.shape
    n = b.shape[1]
    tn = n if tn is None else tn
    return _matmul(
        a, b, dims=_TN, grid=(n // tn, t // tk),
        a_spec=pl.BlockSpec((tk, m), lambda j, k: (k, 0)),
        b_spec=pl.BlockSpec((tk, tn), lambda j, k: (k, j)),
        o_spec=pl.BlockSpec((m, tn), lambda j, k: (0, j)),
        out_shape=jax.ShapeDtypeStruct((m, n), F32), acc_shape=(m, tn), name=name)


def _mm_nn_brhs(a, b, out_dtype, name, tm=512):
    m, kd = a.shape
    nb, _, n = b.shape
    return _matmul(
        a, b, dims=_NN, grid=(nb, m // tm, 1),
        a_spec=pl.BlockSpec((tm, kd), lambda j, i, k: (i, 0)),
        b_spec=pl.BlockSpec((None, kd, n), lambda j, i, k: (j, 0, 0)),
        o_spec=pl.BlockSpec((None, tm, n), lambda j, i, k: (j, i, 0)),
        out_shape=jax.ShapeDtypeStruct((nb, m, n), out_dtype), acc_shape=(tm, n), name=name)


def _mm_nt_brhs(a, b, out_dtype, name, tm=512):
    m, kd = a.shape
    nb, n, _ = b.shape
    return _matmul(
        a, b, dims=_NT, grid=(nb, m // tm, 1),
        a_spec=pl.BlockSpec((tm, kd), lambda j, i, k: (i, 0)),
        b_spec=pl.BlockSpec((None, n, kd), lambda j, i, k: (j, 0, 0)),
        o_spec=pl.BlockSpec((None, tm, n), lambda j, i, k: (j, i, 0)),
        out_shape=jax.ShapeDtypeStruct((nb, m, n), out_dtype), acc_shape=(tm, n), name=name)


def _mm_ksum(a, b, dims, out_dtype, name, tm=512, res=None, res_scale=1.0):
    nb, m, kd = a.shape
    n = b.shape[2] if dims == _NN else b.shape[1]
    return _matmul(
        a, b, dims=dims, grid=(m // tm, nb),
        a_spec=pl.BlockSpec((None, tm, kd), lambda i, k: (k, i, 0)),
        b_spec=pl.BlockSpec((None,) + b.shape[1:], lambda i, k: (k, 0, 0)),
        o_spec=pl.BlockSpec((tm, n), lambda i, k: (i, 0)),
        out_shape=jax.ShapeDtypeStruct((m, n), out_dtype), acc_shape=(tm, n), name=name,
        res=res, res_spec=pl.BlockSpec((tm, n), lambda i, k: (i, 0)), res_scale=res_scale)


def _mm_tn_batched(a, b, name, a_batched, b_batched, tk=512):
    nb = a.shape[0] if a_batched else b.shape[0]
    t, m = a.shape[-2:]
    n = b.shape[-1]
    if a_batched:
        a_spec = pl.BlockSpec((None, tk, m), lambda j, k: (j, k, 0))
    else:
        a_spec = pl.BlockSpec((tk, m), lambda j, k: (k, 0))
    if b_batched:
        b_spec = pl.BlockSpec((None, tk, n), lambda j, k: (j, k, 0))
    else:
        b_spec = pl.BlockSpec((tk, n), lambda j, k: (k, 0))
    return _matmul(
        a, b, dims=_TN, grid=(nb, t // tk), a_spec=a_spec, b_spec=b_spec,
        o_spec=pl.BlockSpec((None, m, n), lambda j, k: (j, 0, 0)),
        out_shape=jax.ShapeDtypeStruct((nb, m, n), F32), acc_shape=(m, n), name=name)


def _row_spec(tm, n):
    return pl.BlockSpec((tm, n), lambda i: (i, 0))


def _par_spec(n, rows=1):
    return pl.BlockSpec((rows, n), lambda i: (0, 0))


def _swap_pairs(x):
    lane = lax.broadcasted_iota(jnp.int32, x.shape, 1)
    return jnp.where(lane % 2 == 0, pltpu.roll(x, LANES - 1, 1), pltpu.roll(x, 1, 1))


def _head_sums(v):
    lo = lax.broadcasted_iota(jnp.int32, v.shape, 1) < HEAD_DIM
    s_lo = jnp.sum(jnp.where(lo, v, 0.0), axis=-1, keepdims=True)
    s_hi = jnp.sum(jnp.where(lo, 0.0, v), axis=-1, keepdims=True)
    return jnp.where(lo, s_lo, s_hi)


def _qk_blocks():
    return [(128 * i, True) for i in range(4)] + [(Q_W, False)]


def _prep_fwd(h, cos_t, sin_t, qn, kn, tm=256):
    t = h.shape[0]
    scale = HEAD_DIM ** -0.5

    def body(h_ref, c_ref, s_ref, qn_ref, kn_ref, qa_ref, ka_ref, va_ref, qb_ref, kb_ref, vb_ref):
        c = c_ref[...]
        s = s_ref[...]
        for start, is_q in _qk_blocks():
            x = h_ref[:, start:start + LANES]
            r = lax.rsqrt(_head_sums(x * x) * (1.0 / HEAD_DIM) + RMS_EPS)
            y = x * r * (qn_ref[...] if is_q else kn_ref[...])
            y = y * c + _swap_pairs(y) * s
            if is_q:
                qa_ref[:, start:start + LANES] = (y * scale).astype(BF16)
            else:
                ka_ref[...] = y.astype(BF16)
        va_ref[...] = h_ref[:, 640:768].astype(BF16)
        qb_ref[...] = (h_ref[:, 768:1280] * scale).astype(BF16)
        kb_ref[...] = h_ref[:, 1280:1408].astype(BF16)
        vb_ref[...] = h_ref[:, 1408:1536].astype(BF16)

    sd = jax.ShapeDtypeStruct
    return pl.pallas_call(
        body, grid=(t // tm,),
        in_specs=[_row_spec(tm, IN_COLS), _row_spec(tm, LANES), _row_spec(tm, LANES), _par_spec(LANES), _par_spec(LANES)],
        out_specs=[_row_spec(tm, Q_W), _row_spec(tm, KV_W), _row_spec(tm, KV_W),
                   _row_spec(tm, Q_W), _row_spec(tm, KV_W), _row_spec(tm, KV_W)],
        out_shape=[sd((t, Q_W), BF16), sd((t, KV_W), BF16), sd((t, KV_W), BF16),
                   sd((t, Q_W), BF16), sd((t, KV_W), BF16), sd((t, KV_W), BF16)],
        compiler_params=_cparams(("parallel",)), name="prep_fwd",
    )(h, cos_t, sin_t, qn, kn)


def _prep_bwd(h, cos_t, sin_t, qn, kn, dqa, dka, dva, dqb, dkb, dvb, tm=256):
    t = h.shape[0]
    scale = HEAD_DIM ** -0.5

    def body(h_ref, c_ref, s_ref, qn_ref, kn_ref, dqa_ref, dka_ref, dva_ref, dqb_ref, dkb_ref, dvb_ref,
             dh_ref, dqn_ref, dkn_ref):
        @pl.when(pl.program_id(0) == 0)
        def _():
            dqn_ref[...] = jnp.zeros_like(dqn_ref)
            dkn_ref[...] = jnp.zeros_like(dkn_ref)

        c = c_ref[...]
        s = s_ref[...]
        for start, is_q in _qk_blocks():
            x = h_ref[:, start:start + LANES]
            gain = qn_ref[...] if is_q else kn_ref[...]
            d = dqa_ref[:, start:start + LANES] * scale if is_q else dka_ref[...]
            dy = d * c + _swap_pairs(d * s)
            r = lax.rsqrt(_head_sums(x * x) * (1.0 / HEAD_DIM) + RMS_EPS)
            xr = x * r
            gsum = jnp.sum(dy * xr, axis=0, keepdims=True)
            if is_q:
                dqn_ref[...] += gsum
            else:
                dkn_ref[...] += gsum
            gy = dy * gain
            dx = r * (gy - xr * (_head_sums(xr * gy) * (1.0 / HEAD_DIM)))
            dh_ref[:, start:start + LANES] = dx.astype(BF16)
        dh_ref[:, 640:768] = dva_ref[...].astype(BF16)
        dh_ref[:, 768:1280] = (dqb_ref[...] * scale).astype(BF16)
        dh_ref[:, 1280:1408] = dkb_ref[...].astype(BF16)
        dh_ref[:, 1408:1536] = dvb_ref[...].astype(BF16)

    sd = jax.ShapeDtypeStruct
    return pl.pallas_call(
        body, grid=(t // tm,),
        in_specs=[_row_spec(tm, IN_COLS), _row_spec(tm, LANES), _row_spec(tm, LANES), _par_spec(LANES), _par_spec(LANES),
                  _row_spec(tm, Q_W), _row_spec(tm, KV_W), _row_spec(tm, KV_W),
                  _row_spec(tm, Q_W), _row_spec(tm, KV_W), _row_spec(tm, KV_W)],
        out_specs=[_row_spec(tm, IN_COLS), _par_spec(LANES), _par_spec(LANES)],
        out_shape=[sd((t, IN_COLS), BF16), sd((1, LANES), F32), sd((1, LANES), F32)],
        compiler_params=_cparams(("arbitrary",)), name="prep_bwd",
    )(h, cos_t, sin_t, qn, kn, dqa, dka, dva, dqb, dkb, dvb)


def _outnorm_fwd(oa, ob, ga, gb, tm=512):
    t = oa.shape[0]

    def body(oa_ref, ob_ref, ga_ref, gb_ref, y_ref):
        for o_ref, g_ref, start in ((oa_ref, ga_ref, 0), (ob_ref, gb_ref, Q_W)):
            x = o_ref[...]
            r = lax.rsqrt(jnp.mean(x * x, axis=-1, keepdims=True) + RMS_EPS)
            y_ref[:, start:start + Q_W] = (x * r * g_ref[...]).astype(BF16)

    return pl.pallas_call(
        body, grid=(t // tm,),
        in_specs=[_row_spec(tm, Q_W), _row_spec(tm, Q_W), _par_spec(Q_W), _par_spec(Q_W)],
        out_specs=_row_spec(tm, D_MODEL), out_shape=jax.ShapeDtypeStruct((t, D_MODEL), BF16),
        compiler_params=_cparams(("parallel",)), name="outnorm_fwd",
    )(oa, ob, ga, gb)


def _outnorm_bwd(dy, oa, ob, ga, gb, tm=512):
    t = oa.shape[0]

    def body(dy_ref, oa_ref, ob_ref, ga_ref, gb_ref, doa_ref, dob_ref, dga_ref, dgb_ref):
        @pl.when(pl.program_id(0) == 0)
        def _():
            dga_ref[...] = jnp.zeros_like(dga_ref)
            dgb_ref[...] = jnp.zeros_like(dgb_ref)

        for o_ref, g_ref, do_ref, dg_ref, start in ((oa_ref, ga_ref, doa_ref, dga_ref, 0),
                                                    (ob_ref, gb_ref, dob_ref, dgb_ref, Q_W)):
            x = o_ref[...]
            d = dy_ref[:, start:start + Q_W]
            r = lax.rsqrt(jnp.mean(x * x, axis=-1, keepdims=True) + RMS_EPS)
            xr = x * r
            dg_ref[...] += jnp.sum(d * xr, axis=0, keepdims=True)
            gy = d * g_ref[...]
            do_ref[...] = r * (gy - xr * jnp.mean(xr * gy, axis=-1, keepdims=True))

    sd = jax.ShapeDtypeStruct
    return pl.pallas_call(
        body, grid=(t // tm,),
        in_specs=[_row_spec(tm, D_MODEL), _row_spec(tm, Q_W), _row_spec(tm, Q_W), _par_spec(Q_W), _par_spec(Q_W)],
        out_specs=[_row_spec(tm, Q_W), _row_spec(tm, Q_W), _par_spec(Q_W), _par_spec(Q_W)],
        out_shape=[sd((t, Q_W), F32), sd((t, Q_W), F32), sd((1, Q_W), F32), sd((1, Q_W), F32)],
        compiler_params=_cparams(("arbitrary",)), name="outnorm_bwd",
    )(dy, oa, ob, ga, gb)


def _ln_fwd(z, g, b, tm=512):
    t = z.shape[0]

    def body(z_ref, g_ref, b_ref, x_ref, xb_ref):
        zz = z_ref[...]
        mu = jnp.mean(zz, axis=-1, keepdims=True)
        zc = zz - mu
        r = lax.rsqrt(jnp.mean(zc * zc, axis=-1, keepdims=True) + LN_EPS)
        y = zc * r * g_ref[...] + b_ref[...]
        x_ref[...] = y
        xb_ref[...] = y.astype(BF16)

    sd = jax.ShapeDtypeStruct
    return pl.pallas_call(
        body, grid=(t // tm,),
        in_specs=[_row_spec(tm, D_MODEL), _par_spec(D_MODEL), _par_spec(D_MODEL)],
        out_specs=[_row_spec(tm, D_MODEL), _row_spec(tm, D_MODEL)],
        out_shape=[sd((t, D_MODEL), F32), sd((t, D_MODEL), BF16)],
        compiler_params=_cparams(("parallel",)), name="ln_fwd",
    )(z, g, b)


def _ln_bwd(d, z, g, tm=512):
    t = z.shape[0]

    def body(d_ref, z_ref, g_ref, dz_ref, dzb_ref, dg_ref, db_ref):
        @pl.when(pl.program_id(0) == 0)
        def _():
            dg_ref[...] = jnp.zeros_like(dg_ref)
            db_ref[...] = jnp.zeros_like(db_ref)

        zz = z_ref[...]
        dd = d_ref[...]
        mu = jnp.mean(zz, axis=-1, keepdims=True)
        zc = zz - mu
        r = lax.rsqrt(jnp.mean(zc * zc, axis=-1, keepdims=True) + LN_EPS)
        xh = zc * r
        dg_ref[...] += jnp.sum(dd * xh, axis=0, keepdims=True)
        db_ref[...] += jnp.sum(dd, axis=0, keepdims=True)
        dxh = dd * g_ref[...]
        dz = r * (dxh - jnp.mean(dxh, axis=-1, keepdims=True) - xh * jnp.mean(dxh * xh, axis=-1, keepdims=True))
        dz_ref[...] = dz
        dzb_ref[...] = dz.astype(BF16)

    sd = jax.ShapeDtypeStruct
    return pl.pallas_call(
        body, grid=(t // tm,),
        in_specs=[_row_spec(tm, D_MODEL), _row_spec(tm, D_MODEL), _par_spec(D_MODEL)],
        out_specs=[_row_spec(tm, D_MODEL), _row_spec(tm, D_MODEL), _par_spec(D_MODEL), _par_spec(D_MODEL)],
        out_shape=[sd((t, D_MODEL), F32), sd((t, D_MODEL), BF16), sd((1, D_MODEL), F32), sd((1, D_MODEL), F32)],
        compiler_params=_cparams(("arbitrary",)), name="ln_bwd",
    )(d, z, g)


def _loss_grad(y, tgt, tm=512):
    t = y.shape[0]
    nsteps = t // tm

    def body(y_ref, t_ref, dy_ref, loss_ref, acc):
        i = pl.program_id(0)

        @pl.when(i == 0)
        def _():
            acc[...] = jnp.zeros_like(acc)

        e = y_ref[...] - t_ref[...]
        dy_ref[...] = e * (1.0 / D_MODEL)
        acc[...] += jnp.sum(e * e, axis=0, keepdims=True)

        @pl.when(i == nsteps - 1)
        def _():
            tot = jnp.sum(acc[...], axis=-1, keepdims=True) * (0.5 / D_MODEL)
            loss_ref[...] = jnp.broadcast_to(tot, loss_ref.shape)

    sd = jax.ShapeDtypeStruct
    return pl.pallas_call(
        body, grid=(nsteps,),
        in_specs=[_row_spec(tm, D_MODEL), _row_spec(tm, D_MODEL)],
        out_specs=[_row_spec(tm, D_MODEL), _par_spec(LANES)],
        out_shape=[sd((t, D_MODEL), F32), sd((1, LANES), F32)],
        scratch_shapes=[pltpu.VMEM((1, D_MODEL), F32)],
        compiler_params=_cparams(("arbitrary",)), name="loss_grad",
    )(y, tgt)


_GELU_C = math.sqrt(2.0 / math.pi)
_GELU_K = 0.044715
HALO = 8


def _gelu_parts(x):
    th = jnp.tanh(_GELU_C * (x + _GELU_K * x * x * x))
    return 0.5 * x * (1.0 + th), th


def _halo_specs(tm, t, shift=0):
    last = t // HALO - 1
    cur = pl.BlockSpec((None, tm, FF_SH), lambda j, i: (j + shift, i, 0))
    prev = pl.BlockSpec((None, HALO, FF_SH), lambda j, i: (j + shift, jnp.maximum(i * (tm // HALO) - 1, 0), 0))
    nxt = pl.BlockSpec((None, HALO, FF_SH), lambda j, i: (j + shift, jnp.minimum((i + 1) * (tm // HALO), last), 0))
    return [prev, cur, nxt]


def _ffn_mid_fwd(g, u, cw, tm=512):
    t = g.shape[1]
    nsteps = t // tm

    def body(gp_ref, g_ref, gn_ref, u_ref, cw_ref, h_ref):
        i = pl.program_id(1)
        gg = g_ref[...]
        row = lax.broadcasted_iota(jnp.int32, gg.shape, 0)
        prev = jnp.where(i == 0, 0.0, gp_ref[HALO - 1:HALO, :])
        nxt = jnp.where(i == nsteps - 1, 0.0, gn_ref[0:1, :])
        g_m1 = jnp.where(row == 0, prev, pltpu.roll(gg, 1, 0))
        g_p1 = jnp.where(row == tm - 1, nxt, pltpu.roll(gg, tm - 1, 0))
        gc = cw_ref[3:4, :] + g_m1 * cw_ref[0:1, :] + gg * cw_ref[1:2, :] + g_p1 * cw_ref[2:3, :]
        act, _ = _gelu_parts(gc)
        h_ref[...] = (act * u_ref[...]).astype(BF16)

    return pl.pallas_call(
        body, grid=(N_SHARD, nsteps),
        in_specs=_halo_specs(tm, t) + [pl.BlockSpec((None, tm, FF_SH), lambda j, i: (j, i, 0)),
                                       pl.BlockSpec((None, 8, FF_SH), lambda j, i: (j, 0, 0))],
        out_specs=pl.BlockSpec((None, tm, FF_SH), lambda j, i: (j, i, 0)),
        out_shape=jax.ShapeDtypeStruct((N_SHARD, t, FF_SH), BF16),
        compiler_params=_cparams(("parallel", "parallel")), name="ffn_mid_fwd",
    )(g, g, g, u, cw)


def _ffn_mid_bwd(g, u, dh, cw, tm=512):
    t = g.shape[1]
    nsteps = t // tm
    te = tm + 2 * HALO

    def body(gp_ref, g_ref, gn_ref, up_ref, u_ref, un_ref, dp_ref, d_ref, dn_ref, cw_ref, dg_ref, du_ref, st_ref):
        i = pl.program_id(1)

        @pl.when(i == 0)
        def _():
            st_ref[...] = jnp.zeros_like(st_ref)

        e = lax.broadcasted_iota(jnp.int32, (te, FF_SH), 0)
        tg = i * tm - HALO + e
        valid = (tg >= 0) & (tg < t)

        def ext(p_ref, c_ref, n_ref):
            return jnp.where(valid, jnp.concatenate([p_ref[...], c_ref[...], n_ref[...]], axis=0), 0.0)

        eg = ext(gp_ref, g_ref, gn_ref)
        eu = ext(up_ref, u_ref, un_ref)
        ed = ext(dp_ref, d_ref, dn_ref)
        w0, w1, w2 = cw_ref[0:1, :], cw_ref[1:2, :], cw_ref[2:3, :]
        g_m1 = pltpu.roll(eg, 1, 0)
        g_p1 = pltpu.roll(eg, te - 1, 0)
        gc = cw_ref[3:4, :] + g_m1 * w0 + eg * w1 + g_p1 * w2
        act, th = _gelu_parts(gc)
        dact = 0.5 * (1.0 + th) + 0.5 * gc * (1.0 - th * th) * _GELU_C * (1.0 + 3.0 * _GELU_K * gc * gc)
        dgc = ed * eu * dact
        dg = pltpu.roll(dgc, te - 1, 0) * w0 + dgc * w1 + pltpu.roll(dgc, 1, 0) * w2
        mid = slice(HALO, HALO + tm)
        dg_ref[...] = dg[mid].astype(BF16)
        du_ref[...] = (ed * act)[mid].astype(BF16)
        sel = dgc[mid]
        parts = [jnp.sum(sel, axis=0, keepdims=True),
                 jnp.sum(sel * g_m1[mid], axis=0, keepdims=True),
                 jnp.sum(sel * eg[mid], axis=0, keepdims=True),
                 jnp.sum(sel * g_p1[mid], axis=0, keepdims=True)]
        r8 = lax.broadcasted_iota(jnp.int32, (8, FF_SH), 0)
        upd = jnp.zeros((8, FF_SH), F32)
        for k, p in enumerate(parts):
            upd = upd + jnp.where(r8 == k, p, 0.0)
        st_ref[...] += upd

    blk = pl.BlockSpec((None, tm, FF_SH), lambda j, i: (j, i, 0))
    sd = jax.ShapeDtypeStruct
    return pl.pallas_call(
        body, grid=(N_SHARD, nsteps),
        in_specs=_halo_specs(tm, t) + _halo_specs(tm, t) + _halo_specs(tm, t)
        + [pl.BlockSpec((None, 8, FF_SH), lambda j, i: (j, 0, 0))],
        out_specs=[blk, blk, pl.BlockSpec((None, 8, FF_SH), lambda j, i: (j, 0, 0))],
        out_shape=[sd((N_SHARD, t, FF_SH), BF16), sd((N_SHARD, t, FF_SH), BF16), sd((N_SHARD, 8, FF_SH), F32)],
        compiler_params=_cparams(("parallel", "arbitrary")), name="ffn_mid_bwd",
    )(g, g, g, u, u, u, dh, dh, dh, cw)


def _lo_mask(rows):
    return lax.broadcasted_iota(jnp.int32, (rows, LANES), 1) < HEAD_DIM


def _stack_heads(src_ref, dst_ref, tq):
    lo = _lo_mask(tq)
    for i in range(4):
        blk = src_ref[:, LANES * i:LANES * (i + 1)].astype(dst_ref.dtype)
        zero = jnp.zeros_like(blk)
        dst_ref[tq * i:tq * (i + 1), :] = jnp.where(lo, blk, zero)
        dst_ref[tq * (4 + i):tq * (5 + i), :] = jnp.where(lo, zero, blk)


def _unstack_heads(st, dst_ref, tq):
    lo = _lo_mask(tq)
    for i in range(4):
        dst_ref[:, LANES * i:LANES * (i + 1)] = jnp.where(
            lo, st[tq * i:tq * (i + 1)], st[tq * (4 + i):tq * (5 + i)]).astype(dst_ref.dtype)


def _stacked_delta(do_ref, o_ref, tq):
    lo = _lo_mask(tq)
    los, his = [], []
    for i in range(4):
        pr = do_ref[:, LANES * i:LANES * (i + 1)] * o_ref[:, LANES * i:LANES * (i + 1)]
        los.append(jnp.sum(jnp.where(lo, pr, 0.0), axis=-1, keepdims=True))
        his.append(jnp.sum(jnp.where(lo, 0.0, pr), axis=-1, keepdims=True))
    return jnp.concatenate(los + his, axis=0)


def _gattn_fwd(q, k, v, tq=128, tk=512):
    t = q.shape[0]
    tk = min(tk, t)
    nq, nk, r = t // tq, t // tk, 8 * tq

    def body(q_ref, k_ref, v_ref, o_ref, lse_ref, qs, m_s, l_s, acc):
        _stack_heads(q_ref, qs, tq)
        m_s[...] = jnp.full_like(m_s, NEG)
        l_s[...] = jnp.zeros_like(l_s)
        acc[...] = jnp.zeros_like(acc)

        def step(j, carry):
            off = pl.multiple_of(j * tk, tk)
            kc = k_ref[pl.ds(off, tk), :]
            vc = v_ref[pl.ds(off, tk), :]
            s = _dot(qs[...], kc, _NT)
            m_prev = m_s[...]
            m_new = jnp.maximum(m_prev, jnp.max(s, axis=-1, keepdims=True))
            alpha = jnp.exp(m_prev - m_new)
            p = jnp.exp(s - m_new)
            l_s[...] = alpha * l_s[...] + jnp.sum(p, axis=-1, keepdims=True)
            acc[...] = alpha * acc[...] + _dot(p, vc, _NN)
            m_s[...] = m_new
            return carry

        lax.fori_loop(0, nk, step, 0)
        _unstack_heads(acc[...] / l_s[...], o_ref, tq)
        lse_ref[...] = m_s[...] + jnp.log(l_s[...])

    sd = jax.ShapeDtypeStruct
    return pl.pallas_call(
        body, grid=(nq,),
        in_specs=[_row_spec(tq, Q_W), _par_spec(KV_W, t), _par_spec(KV_W, t)],
        out_specs=[_row_spec(tq, Q_W), _row_spec(r, 1)],
        out_shape=[sd((t, Q_W), F32), sd((nq * r, 1), F32)],
        scratch_shapes=[pltpu.VMEM((r, LANES), BF16), pltpu.VMEM((r, 1), F32), pltpu.VMEM((r, 1), F32),
                        pltpu.VMEM((r, LANES), F32)],
        compiler_params=_cparams(("parallel",)), name="gattn_fwd",
    )(q, k, v)


def _gattn_bwd(q, k, v, o, do, lse, tq=128, tk=512):
    t = q.shape[0]
    tk = min(tk, t)
    nq, nk, r = t // tq, t // tk, 8 * tq

    def body(q_ref, k_ref, v_ref, o_ref, do_ref, lse_ref, dq_ref, dk_ref, dv_ref, qs, dos, dqa):
        @pl.when(pl.program_id(0) == 0)
        def _():
            dk_ref[...] = jnp.zeros_like(dk_ref)
            dv_ref[...] = jnp.zeros_like(dv_ref)

        _stack_heads(q_ref, qs, tq)
        _stack_heads(do_ref, dos, tq)
        delta = _stacked_delta(do_ref, o_ref, tq)
        lse_v = lse_ref[...]
        dqa[...] = jnp.zeros_like(dqa)

        def step(j, carry):
            off = pl.multiple_of(j * tk, tk)
            kc = k_ref[pl.ds(off, tk), :]
            vc = v_ref[pl.ds(off, tk), :]
            p = jnp.exp(_dot(qs[...], kc, _NT) - lse_v)
            dp = _dot(dos[...], vc, _NT)
            ds = (p * (dp - delta)).astype(BF16)
            dqa[...] += _dot(ds, kc, _NN)
            dk_ref[pl.ds(off, tk), :] += _dot(ds, qs[...], _TN)
            dv_ref[pl.ds(off, tk), :] += _dot(p, dos[...], _TN)
            return carry

        lax.fori_loop(0, nk, step, 0)
        _unstack_heads(dqa[...], dq_ref, tq)

    sd = jax.ShapeDtypeStruct
    return pl.pallas_call(
        body, grid=(nq,),
        in_specs=[_row_spec(tq, Q_W), _par_spec(KV_W, t), _par_spec(KV_W, t), _row_spec(tq, Q_W), _row_spec(tq, Q_W),
                  _row_spec(r, 1)],
        out_specs=[_row_spec(tq, Q_W), _par_spec(KV_W, t), _par_spec(KV_W, t)],
        out_shape=[sd((t, Q_W), F32), sd((t, KV_W), F32), sd((t, KV_W), F32)],
        scratch_shapes=[pltpu.VMEM((r, LANES), BF16), pltpu.VMEM((r, LANES), BF16), pltpu.VMEM((r, LANES), F32)],
        compiler_params=_cparams(("arbitrary",)), name="gattn_bwd",
    )(q, k, v, o, do, lse)


_WQ = Q_BLOCK
_WK = 3 * Q_BLOCK
_WR = 8 * _WQ


def _wattn_scores(qs, kw, bias_ref, n, t):
    col = lax.broadcasted_iota(jnp.int32, (1, _WK), 1)
    kabs = (n - 1) * _WQ + col
    s = _dot(qs[...], kw, _NT) + bias_ref[...]
    return jnp.where((kabs >= 0) & (kabs < t), s, NEG)


def _wattn_fwd(q, kp, vp, bias, sink):
    t = q.shape[0]
    nq = t // _WQ

    def body(q_ref, k_ref, v_ref, b_ref, sk_ref, o_ref, lse_ref, qs):
        n = pl.program_id(0)
        _stack_heads(q_ref, qs, _WQ)
        off = pl.multiple_of(n * _WQ, _WQ)
        kw = k_ref[pl.ds(off, _WK), :]
        vw = v_ref[pl.ds(off, _WK), :]
        s = _wattn_scores(qs, kw, b_ref, n, t)
        sk = sk_ref[...]
        m = jnp.maximum(jnp.max(s, axis=-1, keepdims=True), sk)
        p = jnp.exp(s - m)
        l = jnp.sum(p, axis=-1, keepdims=True) + jnp.exp(sk - m)
        _unstack_heads(_dot(p, vw, _NN) / l, o_ref, _WQ)
        lse_ref[...] = m + jnp.log(l)

    sd = jax.ShapeDtypeStruct
    return pl.pallas_call(
        body, grid=(nq,),
        in_specs=[_row_spec(_WQ, Q_W), _par_spec(KV_W, t + 2 * _WQ), _par_spec(KV_W, t + 2 * _WQ),
                  _par_spec(_WK, _WR), _par_spec(1, _WR)],
        out_specs=[_row_spec(_WQ, Q_W), _row_spec(_WR, 1)],
        out_shape=[sd((t, Q_W), F32), sd((nq * _WR, 1), F32)],
        scratch_shapes=[pltpu.VMEM((_WR, LANES), BF16)],
        compiler_params=_cparams(("parallel",)), name="wattn_fwd",
    )(q, kp, vp, bias, sink)


def _wattn_bwd(q, kp, vp, bias, sink, o, do, lse):
    t = q.shape[0]
    nq = t // _WQ

    def body(q_ref, k_ref, v_ref, b_ref, sk_ref, o_ref, do_ref, lse_ref, dq_ref, dk_ref, dv_ref, db_ref, dsk_ref, qs, dos):
        n = pl.program_id(0)

        @pl.when(n == 0)
        def _():
            dk_ref[...] = jnp.zeros_like(dk_ref)
            dv_ref[...] = jnp.zeros_like(dv_ref)
            db_ref[...] = jnp.zeros_like(db_ref)
            dsk_ref[...] = jnp.zeros_like(dsk_ref)

        _stack_heads(q_ref, qs, _WQ)
        _stack_heads(do_ref, dos, _WQ)
        delta = _stacked_delta(do_ref, o_ref, _WQ)
        off = pl.multiple_of(n * _WQ, _WQ)
        kw = k_ref[pl.ds(off, _WK), :]
        vw = v_ref[pl.ds(off, _WK), :]
        lse_v = lse_ref[...]
        p = jnp.exp(_wattn_scores(qs, kw, b_ref, n, t) - lse_v)
        dp = _dot(dos[...], vw, _NT)
        ds = p * (dp - delta)
        db_ref[...] += ds
        dsk_ref[...] -= jnp.exp(sk_ref[...] - lse_v) * delta
        dsb = ds.astype(BF16)
        _unstack_heads(_dot(dsb, kw, _NN), dq_ref, _WQ)
        dk_ref[pl.ds(off, _WK), :] += _dot(dsb, qs[...], _TN)
        dv_ref[pl.ds(off, _WK), :] += _dot(p, dos[...], _TN)

    sd = jax.ShapeDtypeStruct
    tp = t + 2 * _WQ
    return pl.pallas_call(
        body, grid=(nq,),
        in_specs=[_row_spec(_WQ, Q_W), _par_spec(KV_W, tp), _par_spec(KV_W, tp), _par_spec(_WK, _WR), _par_spec(1, _WR),
                  _row_spec(_WQ, Q_W), _row_spec(_WQ, Q_W), _row_spec(_WR, 1)],
        out_specs=[_row_spec(_WQ, Q_W), _par_spec(KV_W, tp), _par_spec(KV_W, tp), _par_spec(_WK, _WR), _par_spec(1, _WR)],
        out_shape=[sd((t, Q_W), F32), sd((tp, KV_W), F32), sd((tp, KV_W), F32), sd((_WR, _WK), F32), sd((_WR, 1), F32)],
        scratch_shapes=[pltpu.VMEM((_WR, LANES), BF16), pltpu.VMEM((_WR, LANES), BF16)],
        compiler_params=_cparams(("arbitrary",)), name="wattn_bwd",
    )(q, kp, vp, bias, sink, o, do, lse)


def _bias_bucket_reduce(db0, db1, bucket):
    def body(a_ref, b_ref, bk_ref, o_ref):
        d = a_ref[...] + b_ref[...]
        bk = bk_ref[...]
        lane = lax.broadcasted_iota(jnp.int32, (1, LANES), 1)
        out = jnp.zeros((1, LANES), F32)
        for b in range(N_BUCKETS):
            tot = jnp.sum(jnp.sum(jnp.where(bk == b, d, 0.0), axis=-1, keepdims=True), axis=0, keepdims=True)
            out = out + jnp.where(lane == b, tot, 0.0)
        o_ref[...] = out

    hb = pl.BlockSpec((None, _WQ, _WK), lambda h: (h, 0, 0))
    return pl.pallas_call(
        body, grid=(8,), in_specs=[hb, hb, pl.BlockSpec((_WQ, _WK), lambda h: (0, 0))],
        out_specs=pl.BlockSpec((None, 1, LANES), lambda h: (h, 0, 0)),
        out_shape=jax.ShapeDtypeStruct((8, 1, LANES), F32),
        compiler_params=_cparams(("parallel",)), name="bias_bucket_reduce",
    )(db0.reshape(8, _WQ, _WK), db1.reshape(8, _WQ, _WK), bucket)


def _rope_tables(t):
    rows_n = t // GRID_W
    row = jnp.repeat(jnp.arange(rows_n, dtype=F32), GRID_W)
    col = jnp.tile(jnp.arange(GRID_W, dtype=F32), rows_n)
    half = HEAD_DIM // 2
    inv_freq = ROPE_THETA ** (-jnp.arange(0, half, 2, dtype=F32) / half)
    ang = jnp.concatenate([row[:, None] * inv_freq, col[:, None] * inv_freq], axis=-1)
    cos, sin = jnp.cos(ang), jnp.sin(ang)
    c64 = jnp.repeat(cos, 2, axis=-1)
    s64 = jnp.stack([-sin, sin], axis=-1).reshape(t, HEAD_DIM)
    return jnp.tile(c64, (1, 2)), jnp.tile(s64, (1, 2))


def _t5_bucket(rel):
    half = N_BUCKETS // 2
    max_exact = half // 2
    bucket = jnp.where(rel > 0, half, 0)
    rp = jnp.abs(rel)
    rpf = jnp.maximum(rp, 1).astype(jnp.float32)
    large = max_exact + (jnp.log(rpf / max_exact) / math.log(MAX_DISTANCE / max_exact)
                         * (half - max_exact)).astype(jnp.int32)
    large = jnp.minimum(large, half - 1)
    return bucket + jnp.where(rp < max_exact, rp, large)


def _window_tables(rel_bias):
    qpos = jnp.arange(_WQ, dtype=jnp.int32)
    kpos = jnp.arange(_WK, dtype=jnp.int32) - _WQ
    rel = kpos[None, :] - qpos[:, None]
    bucket = _t5_bucket(rel)
    bias = rel_bias[bucket].transpose(2, 0, 1)
    bias = jnp.where((jnp.abs(rel) <= WINDOW)[None], bias, NEG)
    return bias.reshape(_WR, _WK), bucket


def _pad_rows(a):
    return jnp.pad(a, ((_WQ, _WQ), (0, 0)))


def _layer_fwd(x, p, tabs):
    cos_t, sin_t, bias = tabs
    h = _mm_nn(x, p["win"], F32, "in_proj")
    qa, ka, va, qb, kb, vb = _prep_fwd(h, cos_t, sin_t, p["qn"], p["kn"])
    oa, lse_a = _gattn_fwd(qa, ka, va)
    kbp, vbp = _pad_rows(kb), _pad_rows(vb)
    ob, lse_b = _wattn_fwd(qb, kbp, vbp, bias, p["sink"])
    y = _outnorm_fwd(oa, ob, p["ga"], p["gb"])
    z1 = _mm_nn(y, p["wout"], F32, "out_proj", res=x, res_scale=ALPHA)
    x1, x1b = _ln_fwd(z1, p["ln1g"], p["ln1b"])
    g = _mm_nn_brhs(x1b, p["wg"], F32, "gate_proj")
    u = _mm_nn_brhs(x1b, p["wu"], F32, "up_proj")
    hdn = _ffn_mid_fwd(g, u, p["cw"])
    z2 = _mm_ksum(hdn, p["wd"], _NN, F32, "down_proj", res=x1, res_scale=ALPHA)
    x2, _ = _ln_fwd(z2, p["ln2g"], p["ln2b"])
    saved = dict(x=x, h=h, qa=qa, ka=ka, va=va, qb=qb, kbp=kbp, vbp=vbp, oa=oa, ob=ob, lse_a=lse_a, lse_b=lse_b,
                 y=y, z1=z1, x1b=x1b, g=g, u=u, hdn=hdn, z2=z2)
    return x2, saved


def _layer_bwd(dx2, p, s, tabs):
    cos_t, sin_t, bias = tabs
    t = dx2.shape[0]
    dz2, dz2b, dln2g, dln2b = _ln_bwd(dx2, s["z2"], p["ln2g"])
    dhdn = _mm_nt_brhs(dz2b, p["wd"], F32, "down_dx")
    dwd = _mm_tn_batched(s["hdn"], dz2b, "down_dw", a_batched=True, b_batched=False)
    dg, du, stats = _ffn_mid_bwd(s["g"], s["u"], dhdn, p["cw"])
    dx1 = _mm_ksum(dg, p["wg"], _NT, F32, "gate_dx", res=dz2, res_scale=ALPHA)
    dx1 = _mm_ksum(du, p["wu"], _NT, F32, "up_dx", res=dx1, res_scale=1.0)
    dwg = _mm_tn_batched(s["x1b"], dg, "gate_dw", a_batched=False, b_batched=True)
    dwu = _mm_tn_batched(s["x1b"], du, "up_dw", a_batched=False, b_batched=True)
    dz1, dz1b, dln1g, dln1b = _ln_bwd(dx1, s["z1"], p["ln1g"])
    dy = _mm_nt(dz1b, p["wout"], F32, "out_dx")
    dwout = _mm_tn(s["y"], dz1b, "out_dw")
    doa, dob, dga, dgb = _outnorm_bwd(dy, s["oa"], s["ob"], p["ga"], p["gb"])
    dqa, dka, dva = _gattn_bwd(s["qa"], s["ka"], s["va"], s["oa"], doa, s["lse_a"])
    dqb, dkbp, dvbp, dbias, dsink = _wattn_bwd(s["qb"], s["kbp"], s["vbp"], bias, p["sink"], s["ob"], dob, s["lse_b"])
    dkb = lax.slice_in_dim(dkbp, _WQ, _WQ + t, axis=0)
    dvb = lax.slice_in_dim(dvbp, _WQ, _WQ + t, axis=0)
    dh, dqn, dkn = _prep_bwd(s["h"], cos_t, sin_t, p["qn"], p["kn"], dqa, dka, dva, dqb, dkb, dvb)
    dx = _mm_nt(dh, p["win"], F32, "in_dx", res=dz1, res_scale=ALPHA)
    dwin = _mm_tn(s["x"], dh, "in_dw")
    grads = dict(win=dwin, wout=dwout, wg=dwg, wu=dwu, wd=dwd, stats=stats, qn=dqn, kn=dkn, ga=dga, gb=dgb,
                 ln1g=dln1g, ln1b=dln1b, ln2g=dln2g, ln2b=dln2b, bias=dbias, sink=dsink)
    return dx, grads


def _prep_layer_params(l, win, wout, wg, wu, wd, cw, q_norm, k_norm, sink, out_norm_a, out_norm_b, conv_b,
                       ln1_g, ln1_b, ln2_g, ln2_b):
    win_full = win.transpose(1, 0, 2).reshape(D_MODEL, IN_COLS)
    row = lambda v: v.reshape(1, -1)
    return dict(
        win=jnp.take(win_full, _IN_PERM, axis=1),
        wout=jnp.take(wout.reshape(D_MODEL, D_MODEL), _MIX_PERM, axis=0),
        wg=wg, wu=wu, wd=wd,
        cw=jnp.pad(cw, ((0, 0), (0, 5), (0, 0))) + jnp.pad(conv_b[l].reshape(N_SHARD, 1, FF_SH), ((0, 0), (3, 4), (0, 0))),
        qn=row(jnp.tile(q_norm[l], 2)), kn=row(jnp.tile(k_norm[l], 2)),
        ga=row(out_norm_a[l][_PQ]), gb=row(out_norm_b[l][_PQ]),
        ln1g=row(ln1_g[l]), ln1b=row(ln1_b[l]), ln2g=row(ln2_g[l]), ln2b=row(ln2_b[l]),
        sink=jnp.repeat(sink[l], _WQ).reshape(_WR, 1))


def _local_step(x, tgt, params, rel_bias):
    t = x.shape[0]
    cos_t, sin_t = _rope_tables(t)
    bias, bucket = _window_tables(rel_bias)
    tabs = (cos_t, sin_t, bias)
    saved = []
    for l in range(DEPTH):
        x, s = _layer_fwd(x, params[l], tabs)
        saved.append(s)
    dx, loss = _loss_grad(x, tgt)
    grads = [None] * DEPTH
    for l in reversed(range(DEPTH)):
        dx, grads[l] = _layer_bwd(dx, params[l], saved[l], tabs)
    dbucket = _bias_bucket_reduce(grads[0]["bias"], grads[1]["bias"], bucket)
    return loss, dx, grads, dbucket


_ANY = pl.BlockSpec(memory_space=pl.ANY)
_MESH = pl.DeviceIdType.MESH


def _mesh_pos():
    return lax.axis_index("x"), lax.axis_index("y"), lax.axis_index("c")


def _other_chips(x, y):
    return [(1 - x, y), (x, 1 - y), (1 - x, 1 - y)]


def _gather_shards(shards):
    n = len(shards)

    def body(*refs):
        ins, outs = refs[:n], refs[n:2 * n]
        send, recv, loc = refs[2 * n:]
        x, y, c = _mesh_pos()
        me = 2 * x + y
        chips = _other_chips(x, y)
        local = [pltpu.make_async_copy(ins[i], outs[i].at[me], loc.at[i]) for i in range(n)]
        for cp in local:
            cp.start()

        def remote(i, k, block):
            px, py = chips[k]
            return pltpu.make_async_remote_copy(ins[i], outs[i].at[block], send.at[i, k], recv.at[i, k],
                                                device_id=(px, py, c), device_id_type=_MESH)

        sends = [remote(i, k, me) for i in range(n) for k in range(3)]
        for cp in sends:
            cp.start()
        for i in range(n):
            for k, (px, py) in enumerate(chips):
                remote(i, k, 2 * px + py).wait_recv()
        for cp in sends:
            cp.wait_send()
        for cp in local:
            cp.wait()

    return pl.pallas_call(
        body, in_specs=[_ANY] * n, out_specs=[_ANY] * n,
        out_shape=[jax.ShapeDtypeStruct((N_SHARD,) + s.shape, s.dtype) for s in shards],
        scratch_shapes=[pltpu.SemaphoreType.DMA((n, 3)), pltpu.SemaphoreType.DMA((n, 3)), pltpu.SemaphoreType.DMA((n,))],
        name="gather_weights",
    )(*shards)


def _scatter_grads(grads):
    n = len(grads)

    def body(*refs):
        ins = [refs[DEPTH * i:DEPTH * (i + 1)] for i in range(n)]
        outs = refs[DEPTH * n:DEPTH * n + n]
        send, recv, loc = refs[DEPTH * n + n:]
        x, y, c = _mesh_pos()
        me = 2 * x + y
        chips = _other_chips(x, y)
        local = [pltpu.make_async_copy(ins[i][l].at[me], outs[i].at[3, l], loc.at[i, l])
                 for i in range(n) for l in range(DEPTH)]
        for cp in local:
            cp.start()

        def remote(i, l, k):
            px, py = chips[k]
            return pltpu.make_async_remote_copy(ins[i][l].at[2 * px + py], outs[i].at[k, l], send.at[i, l, k],
                                                recv.at[i, l, k], device_id=(px, py, c), device_id_type=_MESH)

        sends = [remote(i, l, k) for i in range(n) for l in range(DEPTH) for k in range(3)]
        for cp in sends:
            cp.start()
        for cp in sends:
            cp.wait_recv()
        for cp in sends:
            cp.wait_send()
        for cp in local:
            cp.wait()

    flat = [g for pair in grads for g in pair]
    return pl.pallas_call(
        body, in_specs=[_ANY] * len(flat), out_specs=[_ANY] * n,
        out_shape=[jax.ShapeDtypeStruct((N_SHARD, DEPTH) + g[0].shape[1:], F32) for g in grads],
        scratch_shapes=[pltpu.SemaphoreType.DMA((n, DEPTH, 3)), pltpu.SemaphoreType.DMA((n, DEPTH, 3)),
                        pltpu.SemaphoreType.DMA((n, DEPTH))],
        name="scatter_grads",
    )(*flat)


def _swap_with_sibling(parts):
    n = len(parts)

    def body(*refs):
        ins, outs = refs[:n], refs[n:2 * n]
        send, recv = refs[2 * n:]
        x, y, c = _mesh_pos()
        copies = [pltpu.make_async_remote_copy(ins[i], outs[i], send.at[i], recv.at[i], device_id=(x, y, 1 - c),
                                               device_id_type=_MESH) for i in range(n)]
        for cp in copies:
            cp.start()
        for cp in copies:
            cp.wait_recv()
        for cp in copies:
            cp.wait_send()

    return pl.pallas_call(
        body, in_specs=[_ANY] * n, out_specs=[_ANY] * n,
        out_shape=[jax.ShapeDtypeStruct(p.shape, p.dtype) for p in parts],
        scratch_shapes=[pltpu.SemaphoreType.DMA((n,)), pltpu.SemaphoreType.DMA((n,))],
        name="swap_sibling",
    )(*parts)


N_DEV = 8


def _allreduce_small(packed):
    rows = packed.shape[0]

    def body(in_ref, out_ref, buf, send, recv, loc):
        x, y, c = _mesh_pos()
        me = 4 * x + 2 * y + c
        own = pltpu.make_async_copy(in_ref, buf.at[me], loc)
        own.start()

        def remote(m, block):
            peer = (x ^ (m >> 2), y ^ ((m >> 1) & 1), c ^ (m & 1))
            return pltpu.make_async_remote_copy(in_ref, buf.at[block], send.at[m - 1], recv.at[m - 1],
                                                device_id=peer, device_id_type=_MESH)

        sends = [remote(m, me) for m in range(1, N_DEV)]
        for cp in sends:
            cp.start()
        for m in range(1, N_DEV):
            remote(m, me ^ m).wait_recv()
        for cp in sends:
            cp.wait_send()
        own.wait()
        tot = buf[0]
        for d in range(1, N_DEV):
            tot = tot + buf[d]
        out_ref[...] = tot

    vm = pl.BlockSpec(memory_space=pltpu.VMEM)
    return pl.pallas_call(
        body, in_specs=[vm], out_specs=vm, out_shape=jax.ShapeDtypeStruct((rows, LANES), F32),
        scratch_shapes=[pltpu.VMEM((N_DEV, rows, LANES), F32), pltpu.SemaphoreType.DMA((N_DEV - 1,)),
                        pltpu.SemaphoreType.DMA((N_DEV - 1,)), pltpu.SemaphoreType.DMA(())],
        name="allreduce_small",
    )(packed)


def _shard_rows(r):
    return r // 2 if r % 16 == 0 else r


def _sum_slots(slots):
    _, _, r, cdim = slots.shape
    tr = _shard_rows(r)

    def body(a_ref, b_ref, c_ref, d_ref, o_ref):
        o_ref[...] = ((d_ref[...] + a_ref[...]) + b_ref[...]) + c_ref[...]

    def spec(k):
        return pl.BlockSpec((None, None, tr, cdim), lambda l, i: (k, l, i, 0))

    return pl.pallas_call(
        body, grid=(DEPTH, r // tr), in_specs=[spec(0), spec(1), spec(2), spec(3)],
        out_specs=pl.BlockSpec((None, tr, cdim), lambda l, i: (l, i, 0)),
        out_shape=jax.ShapeDtypeStruct((DEPTH, r, cdim), F32),
        compiler_params=_cparams(("parallel", "parallel")), name="sum_slots",
    )(slots, slots, slots, slots)


def _adamw_math(w, g, m, v):
    m = ADAM_B1 * m + (1.0 - ADAM_B1) * g
    v = ADAM_B2 * v + (1.0 - ADAM_B2) * (g * g)
    m_hat = m / (1.0 - ADAM_B1 ** ADAM_STEP)
    v_hat = v / (1.0 - ADAM_B2 ** ADAM_STEP)
    delta = -ADAM_LR * (m_hat / (jnp.sqrt(v_hat) + ADAM_EPS) + ADAM_WD * w)
    return delta, m, v


def _adamw_big(ga, gb, w, m, v):
    _, r, cdim = w.shape
    tr = _shard_rows(r)

    def body(ga_ref, gb_ref, w_ref, m_ref, v_ref, g_out, d_out, m_out, v_out):
        g = ga_ref[...] + gb_ref[...]
        d, mn, vn = _adamw_math(w_ref[...], g, m_ref[...], v_ref[...])
        g_out[...] = g
        d_out[...] = d
        m_out[...] = mn
        v_out[...] = vn

    spec = pl.BlockSpec((None, tr, cdim), lambda l, i: (l, i, 0))
    shp = jax.ShapeDtypeStruct(w.shape, F32)
    return pl.pallas_call(
        body, grid=(DEPTH, r // tr), in_specs=[spec] * 5, out_specs=[spec] * 4, out_shape=[shp] * 4,
        compiler_params=_cparams(("parallel", "parallel")), name="adamw_big",
    )(ga, gb, w, m, v)


def _adamw_small(ws, gs, ms, vs):
    n = len(ws)

    def body(*refs):
        w_r, g_r, m_r, v_r = (refs[k * n:(k + 1) * n] for k in range(4))
        d_o, m_o, v_o = (refs[(4 + k) * n:(5 + k) * n] for k in range(3))
        for i in range(n):
            d, mn, vn = _adamw_math(w_r[i][...], g_r[i][...], m_r[i][...], v_r[i][...])
            d_o[i][...] = d
            m_o[i][...] = mn
            v_o[i][...] = vn

    vm = pl.BlockSpec(memory_space=pltpu.VMEM)
    shp = [jax.ShapeDtypeStruct(w.shape, F32) for w in ws]
    outs = pl.pallas_call(
        body, in_specs=[vm] * (4 * n), out_specs=[vm] * (3 * n), out_shape=shp * 3, name="adamw_small",
    )(*ws, *gs, *ms, *vs)
    return outs[:n], outs[n:2 * n], outs[2 * n:]


def _tile_rows(a):
    a = a.reshape(-1, LANES)
    pad = (-a.shape[0]) % 8
    return jnp.pad(a, ((0, pad), (0, 0))) if pad else a


_SMALL_LAYER_PARTS = (("qn", 8), ("kn", 8), ("sink", 8), ("ga", 8), ("gb", 8), ("ln1g", 8), ("ln1b", 8),
                      ("ln2g", 8), ("ln2b", 8), ("stats", N_SHARD * 8 * FF_SH // LANES))
_SMALL_HEAD_ROWS = 16
_SMALL_LAYER_ROWS = sum(r for _, r in _SMALL_LAYER_PARTS)


def _pack_small(loss, dbucket, grads):
    parts = [_tile_rows(loss), _tile_rows(dbucket)]
    for l in range(DEPTH):
        parts += [_tile_rows(grads[l][name]) for name, _ in _SMALL_LAYER_PARTS]
    return jnp.concatenate(parts, axis=0)


def _unpack_small(tot, chip):
    out = dict(loss=tot[0, 0], rel_bias=tot[8:16, :N_BUCKETS].T)
    per = {name: [] for name, _ in _SMALL_LAYER_PARTS}
    for l in range(DEPTH):
        base = _SMALL_HEAD_ROWS + l * _SMALL_LAYER_ROWS
        for name, rows in _SMALL_LAYER_PARTS:
            per[name].append(tot[base:base + rows])
            base += rows
    fold = lambda v: v[0, :HEAD_DIM] + v[0, HEAD_DIM:]
    out["q_norm"] = jnp.stack([fold(v) for v in per["qn"]])
    out["k_norm"] = jnp.stack([fold(v) for v in per["kn"]])
    out["sink"] = jnp.stack([jnp.sum(v, axis=1) for v in per["sink"]])
    out["out_norm_a"] = jnp.stack([v[:4].reshape(Q_W)[_PQ_INV] for v in per["ga"]])
    out["out_norm_b"] = jnp.stack([v[:4].reshape(Q_W)[_PQ_INV] for v in per["gb"]])
    for name, key in (("ln1_g", "ln1g"), ("ln1_b", "ln1b"), ("ln2_g", "ln2g"), ("ln2_b", "ln2b")):
        out[name] = jnp.stack([v.reshape(D_MODEL) for v in per[key]])
    stats = [v.reshape(N_SHARD, 8, FF_SH) for v in per["stats"]]
    out["conv_b"] = jnp.stack([s[:, 0, :].reshape(D_FF) for s in stats])
    out["conv_w"] = jnp.stack([lax.dynamic_index_in_dim(s, chip, 0, keepdims=False)[1:4] for s in stats])
    return out


_WEIGHTS = ("rel_bias", "w_in", "q_norm", "k_norm", "sink", "out_norm_a", "out_norm_b", "w_out", "ln1_g", "ln1_b",
            "w_gate", "w_up", "conv_w", "conv_b", "w_down", "ln2_g", "ln2_b")
_BIG = ("w_in", "w_out", "w_gate", "w_up", "w_down")
_SMALL = tuple(n for n in _WEIGHTS if n not in _BIG)


def _col_blocks(g, n):
    return g.reshape(g.shape[0], N_SHARD, n).transpose(1, 0, 2)


def kernel(x, rel_bias, w_in, q_norm, k_norm, sink, out_norm_a, out_norm_b, w_out, ln1_g, ln1_b, w_gate, w_up, conv_w, conv_b, w_down, ln2_g, ln2_b, loss_target, m_rel_bias, m_w_in, m_q_norm, m_k_norm, m_sink, m_out_norm_a, m_out_norm_b, m_w_out, m_ln1_g, m_ln1_b, m_w_gate, m_w_up, m_conv_w, m_conv_b, m_w_down, m_ln2_g, m_ln2_b, v_rel_bias, v_w_in, v_q_norm, v_k_norm, v_sink, v_out_norm_a, v_out_norm_b, v_w_out, v_ln1_g, v_ln1_b, v_w_gate, v_w_up, v_conv_w, v_conv_b, v_w_down, v_ln2_g, v_ln2_b):
    w = dict(rel_bias=rel_bias, w_in=w_in, q_norm=q_norm, k_norm=k_norm, sink=sink, out_norm_a=out_norm_a,
             out_norm_b=out_norm_b, w_out=w_out, ln1_g=ln1_g, ln1_b=ln1_b, w_gate=w_gate, w_up=w_up, conv_w=conv_w,
             conv_b=conv_b, w_down=w_down, ln2_g=ln2_g, ln2_b=ln2_b)
    m = dict(rel_bias=m_rel_bias, w_in=m_w_in, q_norm=m_q_norm, k_norm=m_k_norm, sink=m_sink, out_norm_a=m_out_norm_a,
             out_norm_b=m_out_norm_b, w_out=m_w_out, ln1_g=m_ln1_g, ln1_b=m_ln1_b, w_gate=m_w_gate, w_up=m_w_up,
             conv_w=m_conv_w, conv_b=m_conv_b, w_down=m_w_down, ln2_g=m_ln2_g, ln2_b=m_ln2_b)
    v = dict(rel_bias=v_rel_bias, w_in=v_w_in, q_norm=v_q_norm, k_norm=v_k_norm, sink=v_sink, out_norm_a=v_out_norm_a,
             out_norm_b=v_out_norm_b, w_out=v_w_out, ln1_g=v_ln1_g, ln1_b=v_ln1_b, w_gate=v_w_gate, w_up=v_w_up,
             conv_w=v_conv_w, conv_b=v_conv_b, w_down=v_w_down, ln2_g=v_ln2_g, ln2_b=v_ln2_b)
    chip = 2 * lax.axis_index("x") + lax.axis_index("y")

    shards = [w[name][l].astype(BF16) for l in range(DEPTH) for name in _BIG] + [conv_w]
    gathered = _gather_shards(shards)
    cw_all = gathered[-1]
    params = []
    for l in range(DEPTH):
        win, wout, wg, wu, wd = gathered[len(_BIG) * l:len(_BIG) * (l + 1)]
        params.append(_prep_layer_params(l, win, wout, wg, wu, wd, cw_all[:, l], q_norm, k_norm, sink, out_norm_a,
                                         out_norm_b, conv_b, ln1_g, ln1_b, ln2_g, ln2_b))

    loss, dx, grads, dbucket = _local_step(x[0], loss_target[0], params, rel_bias)

    small = _unpack_small(_allreduce_small(_pack_small(loss, dbucket, grads)), chip)

    blocked = []
    for l in range(DEPTH):
        g = grads[l]
        blocked.append(dict(
            w_in=_col_blocks(jnp.take(g["win"], _IN_INV, axis=1), IN_SH),
            w_out=jnp.take(g["wout"], _MIX_INV, axis=0).reshape(N_SHARD, OUT_SH, D_MODEL),
            w_gate=g["wg"], w_up=g["wu"], w_down=g["wd"]))
    slots = _scatter_grads([[blocked[l][name] for l in range(DEPTH)] for name in _BIG])
    partial = [_sum_slots(s) for s in slots]
    other = _swap_with_sibling(partial)

    grad, delta, new_m, new_v = {}, {}, {}, {}
    for i, name in enumerate(_BIG):
        grad[name], delta[name], new_m[name], new_v[name] = _adamw_big(partial[i], other[i], w[name], m[name], v[name])
    flat2 = lambda a: a.reshape(-1, a.shape[-1])
    ds, ms, vs = _adamw_small([flat2(w[n]) for n in _SMALL], [flat2(small[n]) for n in _SMALL],
                              [flat2(m[n]) for n in _SMALL], [flat2(v[n]) for n in _SMALL])
    for i, name in enumerate(_SMALL):
        grad[name] = small[name]
        delta[name] = ds[i].reshape(w[name].shape)
        new_m[name] = ms[i].reshape(w[name].shape)
        new_v[name] = vs[i].reshape(w[name].shape)

    return (small["loss"], dx[None], *[grad[n] for n in _WEIGHTS], *[delta[n] for n in _WEIGHTS],
            *[new_m[n] for n in _WEIGHTS], *[new_v[n] for n in _WEIGHTS])
```

```python
import functools
import math

import numpy as np
import jax
import jax.numpy as jnp
from jax import lax
from jax.experimental import pallas as pl
from jax.experimental.pallas import tpu as pltpu

F32 = jnp.float32
BF16 = jnp.bfloat16

D_MODEL = 1024
DEPTH = 2
HEAD_DIM = 64
Q_W = 512
KV_W = 128
IN_COLS = 2 * (Q_W + 2 * KV_W)
N_SHARD = 4
IN_SH = IN_COLS // N_SHARD
OUT_SH = D_MODEL // N_SHARD
D_FF = 2816
FF_SH = D_FF // N_SHARD
Q_BLOCK = 128
WINDOW = 128
N_BUCKETS = 32
MAX_DISTANCE = 128
GRID_W = 64
ROPE_THETA = 10000.0
ALPHA = (2.0 * DEPTH) ** 0.25
RMS_EPS = 1e-6
LN_EPS = 1e-5
NEG = -1e30
LANES = 128
VMEM_LIMIT = 56 * 1024 * 1024

ADAM_LR = 0.001
ADAM_B1 = 0.9
ADAM_B2 = 0.999
ADAM_EPS = 1e-08
ADAM_WD = 0.01
ADAM_STEP = 10

_NN = (((1,), (0,)), ((), ()))
_NT = (((1,), (1,)), ((), ()))
_TN = (((0,), (0,)), ((), ()))


def _dot(a, b, dims):
    return lax.dot_general(a.astype(BF16), b.astype(BF16), dims, preferred_element_type=F32)


def _cparams(sem, vmem=VMEM_LIMIT):
    return pltpu.CompilerParams(dimension_semantics=sem, vmem_limit_bytes=vmem)


def _regroup(a, axis, n_outer, n_inner):
    shp = a.shape
    a = a.reshape(shp[:axis] + (n_outer, n_inner, HEAD_DIM) + shp[axis + 1:])
    return jnp.swapaxes(a, axis, axis + 1).reshape(shp)


def _to_pairs(a, axis):
    return _regroup(a, axis, 2, 4)


def _from_pairs(a, axis):
    return _regroup(a, axis, 4, 2)


def _in_cols_to_pairs(w, fn=_to_pairs):
    return jnp.concatenate([fn(w[..., :Q_W], w.ndim - 1), w[..., Q_W:Q_W + 2 * KV_W],
                            fn(w[..., Q_W + 2 * KV_W:2 * Q_W + 2 * KV_W], w.ndim - 1),
                            w[..., 2 * Q_W + 2 * KV_W:]], axis=-1)


def _mix_rows_to_pairs(w, fn=_to_pairs):
    return fn(w.reshape(2, Q_W, w.shape[-1]), 1).reshape(w.shape)


def _matmul(a, b, *, dims, grid, a_spec, b_spec, o_spec, out_shape, acc_shape, name, res=None,
            res_spec=None, res_scale=1.0):
    nk = grid[-1]
    kax = len(grid) - 1

    def body(*refs):
        if res is None:
            a_ref, b_ref, o_ref, acc = refs
            r_ref = None
        else:
            a_ref, b_ref, r_ref, o_ref, acc = refs
        k = pl.program_id(kax)

        @pl.when(k == 0)
        def _():
            acc[...] = jnp.zeros_like(acc)

        acc[...] += _dot(a_ref[...], b_ref[...], dims)

        @pl.when(k == nk - 1)
        def _():
            o = acc[...]
            if r_ref is not None:
                o = o + res_scale * r_ref[...]
            o_ref[...] = o.astype(o_ref.dtype)

    in_specs = [a_spec, b_spec] + ([res_spec] if res is not None else [])
    args = (a, b) + ((res,) if res is not None else ())
    sem = ("parallel",) * kax + ("arbitrary",)
    return pl.pallas_call(
        body, grid=grid, in_specs=in_specs, out_specs=o_spec, out_shape=out_shape,
        scratch_shapes=[pltpu.VMEM(acc_shape, F32)], compiler_params=_cparams(sem), name=name,
    )(*args)


def _mm_nn(a, b, out_dtype, name, tm=512, res=None, res_scale=1.0):
    m, kd = a.shape
    n = b.shape[1]
    return _matmul(
        a, b, dims=_NN, grid=(m // tm, 1),
        a_spec=pl.BlockSpec((tm, kd), lambda i, k: (i, 0)),
        b_spec=pl.BlockSpec((kd, n), lambda i, k: (0, 0)),
        o_spec=pl.BlockSpec((tm, n), lambda i, k: (i, 0)),
        out_shape=jax.ShapeDtypeStruct((m, n), out_dtype), acc_shape=(tm, n), name=name,
        res=res, res_spec=pl.BlockSpec((tm, n), lambda i, k: (i, 0)), res_scale=res_scale)


def _mm_nt(a, b, out_dtype, name, tm=512, res=None, res_scale=1.0):
    m, kd = a.shape
    n = b.shape[0]
    return _matmul(
        a, b, dims=_NT, grid=(m // tm, 1),
        a_spec=pl.BlockSpec((tm, kd), lambda i, k: (i, 0)),
        b_spec=pl.BlockSpec((n, kd), lambda i, k: (0, 0)),
        o_spec=pl.BlockSpec((tm, n), lambda i, k: (i, 0)),
        out_shape=jax.ShapeDtypeStruct((m, n), out_dtype), acc_shape=(tm, n), name=name,
        res=res, res_spec=pl.BlockSpec((tm, n), lambda i, k: (i, 0)), res_scale=res_scale)


def _mm_tn(a, b, name, tk=512, tn=None, out_dtype=BF16):
    t, m = a.shape
    n = b.shape[1]
    tn = n if tn is None else tn
    return _matmul(
        a, b, dims=_TN, grid=(n // tn, t // tk),
        a_spec=pl.BlockSpec((tk, m), lambda j, k: (k, 0)),
        b_spec=pl.BlockSpec((tk, tn), lambda j, k: (k, j)),
        o_spec=pl.BlockSpec((m, tn), lambda j, k: (0, j)),
        out_shape=jax.ShapeDtypeStruct((m, n), out_dtype), acc_shape=(m, tn), name=name)


def _mm_nn_brhs(a, b, out_dtype, name, tm=512):
    m, kd = a.shape
    nb, _, n = b.shape
    return _matmul(
        a, b, dims=_NN, grid=(nb, m // tm, 1),
        a_spec=pl.BlockSpec((tm, kd), lambda j, i, k: (i, 0)),
        b_spec=pl.BlockSpec((None, kd, n), lambda j, i, k: (j, 0, 0)),
        o_spec=pl.BlockSpec((None, tm, n), lambda j, i, k: (j, i, 0)),
        out_shape=jax.ShapeDtypeStruct((nb, m, n), out_dtype), acc_shape=(tm, n), name=name)


def _mm_nt_brhs(a, b, out_dtype, name, tm=512):
    m, kd = a.shape
    nb, n, _ = b.shape
    return _matmul(
        a, b, dims=_NT, grid=(nb, m // tm, 1),
        a_spec=pl.BlockSpec((tm, kd), lambda j, i, k: (i, 0)),
        b_spec=pl.BlockSpec((None, n, kd), lambda j, i, k: (j, 0, 0)),
        o_spec=pl.BlockSpec((None, tm, n), lambda j, i, k: (j, i, 0)),
        out_shape=jax.ShapeDtypeStruct((nb, m, n), out_dtype), acc_shape=(tm, n), name=name)


def _mm_ksum(a, b, dims, out_dtype, name, tm=512, res=None, res_scale=1.0):
    nb, m, kd = a.shape
    n = b.shape[2] if dims == _NN else b.shape[1]
    return _matmul(
        a, b, dims=dims, grid=(m // tm, nb),
        a_spec=pl.BlockSpec((None, tm, kd), lambda i, k: (k, i, 0)),
        b_spec=pl.BlockSpec((None,) + b.shape[1:], lambda i, k: (k, 0, 0)),
        o_spec=pl.BlockSpec((tm, n), lambda i, k: (i, 0)),
        out_shape=jax.ShapeDtypeStruct((m, n), out_dtype), acc_shape=(tm, n), name=name,
        res=res, res_spec=pl.BlockSpec((tm, n), lambda i, k: (i, 0)), res_scale=res_scale)


def _mm_tn_batched(a, b, name, a_batched, b_batched, tk=512, out_dtype=BF16):
    nb = a.shape[0] if a_batched else b.shape[0]
    t, m = a.shape[-2:]
    n = b.shape[-1]
    if a_batched:
        a_spec = pl.BlockSpec((None, tk, m), lambda j, k: (j, k, 0))
    else:
        a_spec = pl.BlockSpec((tk, m), lambda j, k: (k, 0))
    if b_batched:
        b_spec = pl.BlockSpec((None, tk, n), lambda j, k: (j, k, 0))
    else:
        b_spec = pl.BlockSpec((tk, n), lambda j, k: (k, 0))
    return _matmul(
        a, b, dims=_TN, grid=(nb, t // tk), a_spec=a_spec, b_spec=b_spec,
        o_spec=pl.BlockSpec((None, m, n), lambda j, k: (j, 0, 0)),
        out_shape=jax.ShapeDtypeStruct((nb, m, n), out_dtype), acc_shape=(m, n), name=name)


def _row_spec(tm, n):
    return pl.BlockSpec((tm, n), lambda i: (i, 0))


def _par_spec(n, rows=1):
    return pl.BlockSpec((rows, n), lambda i: (0, 0))


def _swap_pairs(x):
    lane = lax.broadcasted_iota(jnp.int32, x.shape, 1)
    return jnp.where(lane % 2 == 0, pltpu.roll(x, LANES - 1, 1), pltpu.roll(x, 1, 1))


def _head_sums(v):
    lo = lax.broadcasted_iota(jnp.int32, v.shape, 1) < HEAD_DIM
    s_lo = jnp.sum(jnp.where(lo, v, 0.0), axis=-1, keepdims=True)
    s_hi = jnp.sum(jnp.where(lo, 0.0, v), axis=-1, keepdims=True)
    return jnp.where(lo, s_lo, s_hi)


def _qk_blocks():
    return [(128 * i, True) for i in range(4)] + [(Q_W, False)]


def _prep_fwd(h, cos_t, sin_t, qn, kn, tm=256):
    t = h.shape[0]
    scale = HEAD_DIM ** -0.5

    def body(h_ref, c_ref, s_ref, qn_ref, kn_ref, qa_ref, ka_ref, va_ref, qb_ref, kb_ref, vb_ref):
        c = c_ref[...]
        s = s_ref[...]
        for start, is_q in _qk_blocks():
            x = h_ref[:, start:start + LANES]
            r = lax.rsqrt(_head_sums(x * x) * (1.0 / HEAD_DIM) + RMS_EPS)
            y = x * r * (qn_ref[...] if is_q else kn_ref[...])
            y = y * c + _swap_pairs(y) * s
            if is_q:
                qa_ref[:, start:start + LANES] = (y * scale).astype(BF16)
            else:
                ka_ref[...] = y.astype(BF16)
        va_ref[...] = h_ref[:, 640:768].astype(BF16)
        qb_ref[...] = (h_ref[:, 768:1280] * scale).astype(BF16)
        kb_ref[...] = h_ref[:, 1280:1408].astype(BF16)
        vb_ref[...] = h_ref[:, 1408:1536].astype(BF16)

    sd = jax.ShapeDtypeStruct
    return pl.pallas_call(
        body, grid=(t // tm,),
        in_specs=[_row_spec(tm, IN_COLS), _row_spec(tm, LANES), _row_spec(tm, LANES), _par_spec(LANES), _par_spec(LANES)],
        out_specs=[_row_spec(tm, Q_W), _row_spec(tm, KV_W), _row_spec(tm, KV_W),
                   _row_spec(tm, Q_W), _row_spec(tm, KV_W), _row_spec(tm, KV_W)],
        out_shape=[sd((t, Q_W), BF16), sd((t, KV_W), BF16), sd((t, KV_W), BF16),
                   sd((t, Q_W), BF16), sd((t, KV_W), BF16), sd((t, KV_W), BF16)],
        compiler_params=_cparams(("parallel",)), name="prep_fwd",
    )(h, cos_t, sin_t, qn, kn)


def _prep_bwd(h, cos_t, sin_t, qn, kn, dqa, dka, dva, dqb, dkb, dvb, tm=256):
    t = h.shape[0]
    scale = HEAD_DIM ** -0.5

    def body(h_ref, c_ref, s_ref, qn_ref, kn_ref, dqa_ref, dka_ref, dva_ref, dqb_ref, dkb_ref, dvb_ref,
             dh_ref, dqn_ref, dkn_ref):
        @pl.when(pl.program_id(0) == 0)
        def _():
            dqn_ref[...] = jnp.zeros_like(dqn_ref)
            dkn_ref[...] = jnp.zeros_like(dkn_ref)

        c = c_ref[...]
        s = s_ref[...]
        for start, is_q in _qk_blocks():
            x = h_ref[:, start:start + LANES]
            gain = qn_ref[...] if is_q else kn_ref[...]
            d = dqa_ref[:, start:start + LANES] * scale if is_q else dka_ref[...]
            dy = d * c + _swap_pairs(d * s)
            r = lax.rsqrt(_head_sums(x * x) * (1.0 / HEAD_DIM) + RMS_EPS)
            xr = x * r
            gsum = jnp.sum(dy * xr, axis=0, keepdims=True)
            if is_q:
                dqn_ref[...] += gsum
            else:
                dkn_ref[...] += gsum
            gy = dy * gain
            dx = r * (gy - xr * (_head_sums(xr * gy) * (1.0 / HEAD_DIM)))
            dh_ref[:, start:start + LANES] = dx.astype(BF16)
        dh_ref[:, 640:768] = dva_ref[...].astype(BF16)
        dh_ref[:, 768:1280] = (dqb_ref[...] * scale).astype(BF16)
        dh_ref[:, 1280:1408] = dkb_ref[...].astype(BF16)
        dh_ref[:, 1408:1536] = dvb_ref[...].astype(BF16)

    sd = jax.ShapeDtypeStruct
    return pl.pallas_call(
        body, grid=(t // tm,),
        in_specs=[_row_spec(tm, IN_COLS), _row_spec(tm, LANES), _row_spec(tm, LANES), _par_spec(LANES), _par_spec(LANES),
                  _row_spec(tm, Q_W), _row_spec(tm, KV_W), _row_spec(tm, KV_W),
                  _row_spec(tm, Q_W), _row_spec(tm, KV_W), _row_spec(tm, KV_W)],
        out_specs=[_row_spec(tm, IN_COLS), _par_spec(LANES), _par_spec(LANES)],
        out_shape=[sd((t, IN_COLS), BF16), sd((1, LANES), F32), sd((1, LANES), F32)],
        compiler_params=_cparams(("arbitrary",)), name="prep_bwd",
    )(h, cos_t, sin_t, qn, kn, dqa, dka, dva, dqb, dkb, dvb)


def _outnorm_fwd(oa, ob, ga, gb, tm=512):
    t = oa.shape[0]

    def body(oa_ref, ob_ref, ga_ref, gb_ref, y_ref):
        for o_ref, g_ref, start in ((oa_ref, ga_ref, 0), (ob_ref, gb_ref, Q_W)):
            x = o_ref[...]
            r = lax.rsqrt(jnp.mean(x * x, axis=-1, keepdims=True) + RMS_EPS)
            y_ref[:, start:start + Q_W] = (x * r * g_ref[...]).astype(BF16)

    return pl.pallas_call(
        body, grid=(t // tm,),
        in_specs=[_row_spec(tm, Q_W), _row_spec(tm, Q_W), _par_spec(Q_W), _par_spec(Q_W)],
        out_specs=_row_spec(tm, D_MODEL), out_shape=jax.ShapeDtypeStruct((t, D_MODEL), BF16),
        compiler_params=_cparams(("parallel",)), name="outnorm_fwd",
    )(oa, ob, ga, gb)


def _outnorm_bwd(dy, oa, ob, ga, gb, tm=512):
    t = oa.shape[0]

    def body(dy_ref, oa_ref, ob_ref, ga_ref, gb_ref, doa_ref, dob_ref, dga_ref, dgb_ref):
        @pl.when(pl.program_id(0) == 0)
        def _():
            dga_ref[...] = jnp.zeros_like(dga_ref)
            dgb_ref[...] = jnp.zeros_like(dgb_ref)

        for o_ref, g_ref, do_ref, dg_ref, start in ((oa_ref, ga_ref, doa_ref, dga_ref, 0),
                                                    (ob_ref, gb_ref, dob_ref, dgb_ref, Q_W)):
            x = o_ref[...]
            d = dy_ref[:, start:start + Q_W]
            r = lax.rsqrt(jnp.mean(x * x, axis=-1, keepdims=True) + RMS_EPS)
            xr = x * r
            dg_ref[...] += jnp.sum(d * xr, axis=0, keepdims=True)
            gy = d * g_ref[...]
            do_ref[...] = r * (gy - xr * jnp.mean(xr * gy, axis=-1, keepdims=True))

    sd = jax.ShapeDtypeStruct
    return pl.pallas_call(
        body, grid=(t // tm,),
        in_specs=[_row_spec(tm, D_MODEL), _row_spec(tm, Q_W), _row_spec(tm, Q_W), _par_spec(Q_W), _par_spec(Q_W)],
        out_specs=[_row_spec(tm, Q_W), _row_spec(tm, Q_W), _par_spec(Q_W), _par_spec(Q_W)],
        out_shape=[sd((t, Q_W), F32), sd((t, Q_W), F32), sd((1, Q_W), F32), sd((1, Q_W), F32)],
        compiler_params=_cparams(("arbitrary",)), name="outnorm_bwd",
    )(dy, oa, ob, ga, gb)


def _ln_fwd(z, g, b, tm=512):
    t = z.shape[0]

    def body(z_ref, g_ref, b_ref, x_ref, xb_ref):
        zz = z_ref[...]
        mu = jnp.mean(zz, axis=-1, keepdims=True)
        zc = zz - mu
        r = lax.rsqrt(jnp.mean(zc * zc, axis=-1, keepdims=True) + LN_EPS)
        y = zc * r * g_ref[...] + b_ref[...]
        x_ref[...] = y
        xb_ref[...] = y.astype(BF16)

    sd = jax.ShapeDtypeStruct
    return pl.pallas_call(
        body, grid=(t // tm,),
        in_specs=[_row_spec(tm, D_MODEL), _par_spec(D_MODEL), _par_spec(D_MODEL)],
        out_specs=[_row_spec(tm, D_MODEL), _row_spec(tm, D_MODEL)],
        out_shape=[sd((t, D_MODEL), F32), sd((t, D_MODEL), BF16)],
        compiler_params=_cparams(("parallel",)), name="ln_fwd",
    )(z, g, b)


def _ln_bwd(d, z, g, tm=512):
    t = z.shape[0]

    def body(d_ref, z_ref, g_ref, dz_ref, dzb_ref, dg_ref, db_ref):
        @pl.when(pl.program_id(0) == 0)
        def _():
            dg_ref[...] = jnp.zeros_like(dg_ref)
            db_ref[...] = jnp.zeros_like(db_ref)

        zz = z_ref[...]
        dd = d_ref[...]
        mu = jnp.mean(zz, axis=-1, keepdims=True)
        zc = zz - mu
        r = lax.rsqrt(jnp.mean(zc * zc, axis=-1, keepdims=True) + LN_EPS)
        xh = zc * r
        dg_ref[...] += jnp.sum(dd * xh, axis=0, keepdims=True)
        db_ref[...] += jnp.sum(dd, axis=0, keepdims=True)
        dxh = dd * g_ref[...]
        dz = r * (dxh - jnp.mean(dxh, axis=-1, keepdims=True) - xh * jnp.mean(dxh * xh, axis=-1, keepdims=True))
        dz_ref[...] = dz
        dzb_ref[...] = dz.astype(BF16)

    sd = jax.ShapeDtypeStruct
    return pl.pallas_call(
        body, grid=(t // tm,),
        in_specs=[_row_spec(tm, D_MODEL), _row_spec(tm, D_MODEL), _par_spec(D_MODEL)],
        out_specs=[_row_spec(tm, D_MODEL), _row_spec(tm, D_MODEL), _par_spec(D_MODEL), _par_spec(D_MODEL)],
        out_shape=[sd((t, D_MODEL), F32), sd((t, D_MODEL), BF16), sd((1, D_MODEL), F32), sd((1, D_MODEL), F32)],
        compiler_params=_cparams(("arbitrary",)), name="ln_bwd",
    )(d, z, g)


def _loss_grad(y, tgt, tm=512):
    t = y.shape[0]
    nsteps = t // tm

    def body(y_ref, t_ref, dy_ref, loss_ref, acc):
        i = pl.program_id(0)

        @pl.when(i == 0)
        def _():
            acc[...] = jnp.zeros_like(acc)

        e = y_ref[...] - t_ref[...]
        dy_ref[...] = e * (1.0 / D_MODEL)
        acc[...] += jnp.sum(e * e, axis=0, keepdims=True)

        @pl.when(i == nsteps - 1)
        def _():
            tot = jnp.sum(acc[...], axis=-1, keepdims=True) * (0.5 / D_MODEL)
            loss_ref[...] = jnp.broadcast_to(tot, loss_ref.shape)

    sd = jax.ShapeDtypeStruct
    return pl.pallas_call(
        body, grid=(nsteps,),
        in_specs=[_row_spec(tm, D_MODEL), _row_spec(tm, D_MODEL)],
        out_specs=[_row_spec(tm, D_MODEL), _par_spec(LANES)],
        out_shape=[sd((t, D_MODEL), F32), sd((1, LANES), F32)],
        scratch_shapes=[pltpu.VMEM((1, D_MODEL), F32)],
        compiler_params=_cparams(("arbitrary",)), name="loss_grad",
    )(y, tgt)


_GELU_C = math.sqrt(2.0 / math.pi)
_GELU_K = 0.044715
HALO = 8


def _gelu_parts(x):
    th = jnp.tanh(_GELU_C * (x + _GELU_K * x * x * x))
    return 0.5 * x * (1.0 + th), th


def _halo_specs(tm, t, shift=0):
    last = t // HALO - 1
    cur = pl.BlockSpec((None, tm, FF_SH), lambda j, i: (j + shift, i, 0))
    prev = pl.BlockSpec((None, HALO, FF_SH), lambda j, i: (j + shift, jnp.maximum(i * (tm // HALO) - 1, 0), 0))
    nxt = pl.BlockSpec((None, HALO, FF_SH), lambda j, i: (j + shift, jnp.minimum((i + 1) * (tm // HALO), last), 0))
    return [prev, cur, nxt]


def _ffn_mid_fwd(g, u, cw, tm=512):
    t = g.shape[1]
    nsteps = t // tm

    def body(gp_ref, g_ref, gn_ref, u_ref, cw_ref, h_ref):
        i = pl.program_id(1)
        gg = g_ref[...]
        row = lax.broadcasted_iota(jnp.int32, gg.shape, 0)
        prev = jnp.where(i == 0, 0.0, gp_ref[HALO - 1:HALO, :])
        nxt = jnp.where(i == nsteps - 1, 0.0, gn_ref[0:1, :])
        g_m1 = jnp.where(row == 0, prev, pltpu.roll(gg, 1, 0))
        g_p1 = jnp.where(row == tm - 1, nxt, pltpu.roll(gg, tm - 1, 0))
        gc = cw_ref[3:4, :] + g_m1 * cw_ref[0:1, :] + gg * cw_ref[1:2, :] + g_p1 * cw_ref[2:3, :]
        act, _ = _gelu_parts(gc)
        h_ref[...] = (act * u_ref[...]).astype(BF16)

    return pl.pallas_call(
        body, grid=(N_SHARD, nsteps),
        in_specs=_halo_specs(tm, t) + [pl.BlockSpec((None, tm, FF_SH), lambda j, i: (j, i, 0)),
                                       pl.BlockSpec((None, 8, FF_SH), lambda j, i: (j, 0, 0))],
        out_specs=pl.BlockSpec((None, tm, FF_SH), lambda j, i: (j, i, 0)),
        out_shape=jax.ShapeDtypeStruct((N_SHARD, t, FF_SH), BF16),
        compiler_params=_cparams(("parallel", "parallel")), name="ffn_mid_fwd",
    )(g, g, g, u, cw)


def _ffn_mid_bwd(g, u, dh, cw, tm=512):
    t = g.shape[1]
    nsteps = t // tm
    te = tm + 2 * HALO

    def body(gp_ref, g_ref, gn_ref, up_ref, u_ref, un_ref, dp_ref, d_ref, dn_ref, cw_ref, dg_ref, du_ref, st_ref):
        i = pl.program_id(1)

        @pl.when(i == 0)
        def _():
            st_ref[...] = jnp.zeros_like(st_ref)

        e = lax.broadcasted_iota(jnp.int32, (te, FF_SH), 0)
        tg = i * tm - HALO + e
        valid = (tg >= 0) & (tg < t)

        def ext(p_ref, c_ref, n_ref):
            return jnp.where(valid, jnp.concatenate([p_ref[...], c_ref[...], n_ref[...]], axis=0), 0.0)

        eg = ext(gp_ref, g_ref, gn_ref)
        eu = ext(up_ref, u_ref, un_ref)
        ed = ext(dp_ref, d_ref, dn_ref)
        w0, w1, w2 = cw_ref[0:1, :], cw_ref[1:2, :], cw_ref[2:3, :]
        g_m1 = pltpu.roll(eg, 1, 0)
        g_p1 = pltpu.roll(eg, te - 1, 0)
        gc = cw_ref[3:4, :] + g_m1 * w0 + eg * w1 + g_p1 * w2
        act, th = _gelu_parts(gc)
        dact = 0.5 * (1.0 + th) + 0.5 * gc * (1.0 - th * th) * _GELU_C * (1.0 + 3.0 * _GELU_K * gc * gc)
        dgc = ed * eu * dact
        dg = pltpu.roll(dgc, te - 1, 0) * w0 + dgc * w1 + pltpu.roll(dgc, 1, 0) * w2
        mid = slice(HALO, HALO + tm)
        dg_ref[...] = dg[mid].astype(BF16)
        du_ref[...] = (ed * act)[mid].astype(BF16)
        sel = dgc[mid]
        parts = [jnp.sum(sel, axis=0, keepdims=True),
                 jnp.sum(sel * g_m1[mid], axis=0, keepdims=True),
                 jnp.sum(sel * eg[mid], axis=0, keepdims=True),
                 jnp.sum(sel * g_p1[mid], axis=0, keepdims=True)]
        r8 = lax.broadcasted_iota(jnp.int32, (8, FF_SH), 0)
        upd = jnp.zeros((8, FF_SH), F32)
        for k, p in enumerate(parts):
            upd = upd + jnp.where(r8 == k, p, 0.0)
        st_ref[...] += upd

    blk = pl.BlockSpec((None, tm, FF_SH), lambda j, i: (j, i, 0))
    sd = jax.ShapeDtypeStruct
    return pl.pallas_call(
        body, grid=(N_SHARD, nsteps),
        in_specs=_halo_specs(tm, t) + _halo_specs(tm, t) + _halo_specs(tm, t)
        + [pl.BlockSpec((None, 8, FF_SH), lambda j, i: (j, 0, 0))],
        out_specs=[blk, blk, pl.BlockSpec((None, 8, FF_SH), lambda j, i: (j, 0, 0))],
        out_shape=[sd((N_SHARD, t, FF_SH), BF16), sd((N_SHARD, t, FF_SH), BF16), sd((N_SHARD, 8, FF_SH), F32)],
        compiler_params=_cparams(("parallel", "arbitrary")), name="ffn_mid_bwd",
    )(g, g, g, u, u, u, dh, dh, dh, cw)


def _lo_mask(rows):
    return lax.broadcasted_iota(jnp.int32, (rows, LANES), 1) < HEAD_DIM


def _stack_heads(src_ref, dst_ref, tq):
    lo = _lo_mask(tq)
    for i in range(4):
        blk = src_ref[:, LANES * i:LANES * (i + 1)].astype(dst_ref.dtype)
        zero = jnp.zeros_like(blk)
        dst_ref[tq * i:tq * (i + 1), :] = jnp.where(lo, blk, zero)
        dst_ref[tq * (4 + i):tq * (5 + i), :] = jnp.where(lo, zero, blk)


def _unstack_heads(st, dst_ref, tq):
    lo = _lo_mask(tq)
    for i in range(4):
        dst_ref[:, LANES * i:LANES * (i + 1)] = jnp.where(
            lo, st[tq * i:tq * (i + 1)], st[tq * (4 + i):tq * (5 + i)]).astype(dst_ref.dtype)


def _stacked_delta(do_ref, o_ref, tq):
    lo = _lo_mask(tq)
    los, his = [], []
    for i in range(4):
        pr = do_ref[:, LANES * i:LANES * (i + 1)] * o_ref[:, LANES * i:LANES * (i + 1)]
        los.append(jnp.sum(jnp.where(lo, pr, 0.0), axis=-1, keepdims=True))
        his.append(jnp.sum(jnp.where(lo, 0.0, pr), axis=-1, keepdims=True))
    return jnp.concatenate(los + his, axis=0)


def _lane_chunks(a):
    return [a[:, LANES * c:LANES * (c + 1)] for c in range(a.shape[1] // LANES)]


def _gattn_fwd(q, k, v, tq=128, tk=512):
    t = q.shape[0]
    tk = min(tk, t)
    nq, nk, r = t // tq, t // tk, 8 * tq

    def body(q_ref, k_ref, v_ref, o_ref, lse_ref, qs, m_s, l_s, acc):
        _stack_heads(q_ref, qs, tq)
        m_s[...] = jnp.full_like(m_s, NEG)

        def max_step(j, carry):
            off = pl.multiple_of(j * tk, tk)
            s = _dot(qs[...], k_ref[pl.ds(off, tk), :], _NT)
            mp = m_s[...]
            for sc in _lane_chunks(s):
                mp = jnp.maximum(mp, sc)
            m_s[...] = mp
            return carry

        lax.fori_loop(0, nk, max_step, 0)
        m_row = jnp.max(m_s[...], axis=-1, keepdims=True)
        m_s[...] = jnp.broadcast_to(m_row, m_s.shape)
        l_s[...] = jnp.zeros_like(l_s)
        acc[...] = jnp.zeros_like(acc)

        def sum_step(j, carry):
            off = pl.multiple_of(j * tk, tk)
            s = _dot(qs[...], k_ref[pl.ds(off, tk), :], _NT)
            m_rep = m_s[...]
            ps = [jnp.exp(sc - m_rep) for sc in _lane_chunks(s)]
            lp = l_s[...]
            for pc in ps:
                lp = lp + pc
            l_s[...] = lp
            p = jnp.concatenate([pc.astype(BF16) for pc in ps], axis=1)
            acc[...] += _dot(p, v_ref[pl.ds(off, tk), :], _NN)
            return carry

        lax.fori_loop(0, nk, sum_step, 0)
        l_row = jnp.sum(l_s[...], axis=-1, keepdims=True)
        _unstack_heads(acc[...] / l_row, o_ref, tq)
        lse_ref[...] = m_row + jnp.log(l_row)

    sd = jax.ShapeDtypeStruct
    return pl.pallas_call(
        body, grid=(nq,),
        in_specs=[_row_spec(tq, Q_W), _par_spec(KV_W, t), _par_spec(KV_W, t)],
        out_specs=[_row_spec(tq, Q_W), _row_spec(r, 1)],
        out_shape=[sd((t, Q_W), F32), sd((nq * r, 1), F32)],
        scratch_shapes=[pltpu.VMEM((r, LANES), BF16), pltpu.VMEM((r, LANES), F32), pltpu.VMEM((r, LANES), F32),
                        pltpu.VMEM((r, LANES), F32)],
        compiler_params=_cparams(("parallel",)), name="gattn_fwd",
    )(q, k, v)


def _gattn_bwd(q, k, v, o, do, lse, tq=128, tk=512):
    t = q.shape[0]
    tk = min(tk, t)
    nq, nk, r = t // tq, t // tk, 8 * tq

    def body(q_ref, k_ref, v_ref, o_ref, do_ref, lse_ref, dq_ref, dk_ref, dv_ref, qs, dos, dqa):
        @pl.when(pl.program_id(0) == 0)
        def _():
            dk_ref[...] = jnp.zeros_like(dk_ref)
            dv_ref[...] = jnp.zeros_like(dv_ref)

        _stack_heads(q_ref, qs, tq)
        _stack_heads(do_ref, dos, tq)
        delta = _stacked_delta(do_ref, o_ref, tq)
        lse_v = lse_ref[...]
        dqa[...] = jnp.zeros_like(dqa)

        def step(j, carry):
            off = pl.multiple_of(j * tk, tk)
            kc = k_ref[pl.ds(off, tk), :]
            vc = v_ref[pl.ds(off, tk), :]
            p = jnp.exp(_dot(qs[...], kc, _NT) - lse_v)
            dp = _dot(dos[...], vc, _NT)
            ds = (p * (dp - delta)).astype(BF16)
            dqa[...] += _dot(ds, kc, _NN)
            dk_ref[pl.ds(off, tk), :] += _dot(ds, qs[...], _TN)
            dv_ref[pl.ds(off, tk), :] += _dot(p, dos[...], _TN)
            return carry

        lax.fori_loop(0, nk, step, 0)
        _unstack_heads(dqa[...], dq_ref, tq)

    sd = jax.ShapeDtypeStruct
    return pl.pallas_call(
        body, grid=(nq,),
        in_specs=[_row_spec(tq, Q_W), _par_spec(KV_W, t), _par_spec(KV_W, t), _row_spec(tq, Q_W), _row_spec(tq, Q_W),
                  _row_spec(r, 1)],
        out_specs=[_row_spec(tq, Q_W), _par_spec(KV_W, t), _par_spec(KV_W, t)],
        out_shape=[sd((t, Q_W), F32), sd((t, KV_W), F32), sd((t, KV_W), F32)],
        scratch_shapes=[pltpu.VMEM((r, LANES), BF16), pltpu.VMEM((r, LANES), BF16), pltpu.VMEM((r, LANES), F32)],
        compiler_params=_cparams(("arbitrary",)), name="gattn_bwd",
    )(q, k, v, o, do, lse)


_WQ = Q_BLOCK
_WK = 3 * Q_BLOCK
_WR = 8 * _WQ


def _wattn_scores(qs, kw, bias_ref, n, t):
    col = lax.broadcasted_iota(jnp.int32, (1, _WK), 1)
    kabs = (n - 1) * _WQ + col
    s = _dot(qs[...], kw, _NT) + bias_ref[...]
    return jnp.where((kabs >= 0) & (kabs < t), s, NEG)


def _wattn_fwd(q, kp, vp, bias, sink):
    t = q.shape[0]
    nq = t // _WQ

    def body(q_ref, k_ref, v_ref, b_ref, sk_ref, o_ref, lse_ref, qs):
        n = pl.program_id(0)
        _stack_heads(q_ref, qs, _WQ)
        off = pl.multiple_of(n * _WQ, _WQ)
        kw = k_ref[pl.ds(off, _WK), :]
        vw = v_ref[pl.ds(off, _WK), :]
        s = _wattn_scores(qs, kw, b_ref, n, t)
        sk = sk_ref[...]
        m = jnp.maximum(jnp.max(s, axis=-1, keepdims=True), sk)
        p = jnp.exp(s - m)
        l = jnp.sum(p, axis=-1, keepdims=True) + jnp.exp(sk - m)
        _unstack_heads(_dot(p, vw, _NN) / l, o_ref, _WQ)
        lse_ref[...] = m + jnp.log(l)

    sd = jax.ShapeDtypeStruct
    return pl.pallas_call(
        body, grid=(nq,),
        in_specs=[_row_spec(_WQ, Q_W), _par_spec(KV_W, t + 2 * _WQ), _par_spec(KV_W, t + 2 * _WQ),
                  _par_spec(_WK, _WR), _par_spec(1, _WR)],
        out_specs=[_row_spec(_WQ, Q_W), _row_spec(_WR, 1)],
        out_shape=[sd((t, Q_W), F32), sd((nq * _WR, 1), F32)],
        scratch_shapes=[pltpu.VMEM((_WR, LANES), BF16)],
        compiler_params=_cparams(("parallel",)), name="wattn_fwd",
    )(q, kp, vp, bias, sink)


def _wattn_bwd(q, kp, vp, bias, sink, o, do, lse):
    t = q.shape[0]
    nq = t // _WQ

    def body(q_ref, k_ref, v_ref, b_ref, sk_ref, o_ref, do_ref, lse_ref, dq_ref, dk_ref, dv_ref, db_ref, dsk_ref, qs, dos):
        n = pl.program_id(0)

        @pl.when(n == 0)
        def _():
            dk_ref[...] = jnp.zeros_like(dk_ref)
            dv_ref[...] = jnp.zeros_like(dv_ref)
            db_ref[...] = jnp.zeros_like(db_ref)
            dsk_ref[...] = jnp.zeros_like(dsk_ref)

        _stack_heads(q_ref, qs, _WQ)
        _stack_heads(do_ref, dos, _WQ)
        delta = _stacked_delta(do_ref, o_ref, _WQ)
        off = pl.multiple_of(n * _WQ, _WQ)
        kw = k_ref[pl.ds(off, _WK), :]
        vw = v_ref[pl.ds(off, _WK), :]
        lse_v = lse_ref[...]
        p = jnp.exp(_wattn_scores(qs, kw, b_ref, n, t) - lse_v)
        dp = _dot(dos[...], vw, _NT)
        ds = p * (dp - delta)
        db_ref[...] += ds
        dsk_ref[...] -= jnp.exp(sk_ref[...] - lse_v) * delta
        dsb = ds.astype(BF16)
        _unstack_heads(_dot(dsb, kw, _NN), dq_ref, _WQ)
        dk_ref[pl.ds(off, _WK), :] += _dot(dsb, qs[...], _TN)
        dv_ref[pl.ds(off, _WK), :] += _dot(p, dos[...], _TN)

    sd = jax.ShapeDtypeStruct
    tp = t + 2 * _WQ
    return pl.pallas_call(
        body, grid=(nq,),
        in_specs=[_row_spec(_WQ, Q_W), _par_spec(KV_W, tp), _par_spec(KV_W, tp), _par_spec(_WK, _WR), _par_spec(1, _WR),
                  _row_spec(_WQ, Q_W), _row_spec(_WQ, Q_W), _row_spec(_WR, 1)],
        out_specs=[_row_spec(_WQ, Q_W), _par_spec(KV_W, tp), _par_spec(KV_W, tp), _par_spec(_WK, _WR), _par_spec(1, _WR)],
        out_shape=[sd((t, Q_W), F32), sd((tp, KV_W), F32), sd((tp, KV_W), F32), sd((_WR, _WK), F32), sd((_WR, 1), F32)],
        scratch_shapes=[pltpu.VMEM((_WR, LANES), BF16), pltpu.VMEM((_WR, LANES), BF16)],
        compiler_params=_cparams(("arbitrary",)), name="wattn_bwd",
    )(q, kp, vp, bias, sink, o, do, lse)


def _bias_bucket_reduce(db0, db1, bucket):
    def body(a_ref, b_ref, bk_ref, o_ref):
        d = a_ref[...] + b_ref[...]
        bk = bk_ref[...]
        lane = lax.broadcasted_iota(jnp.int32, (1, LANES), 1)
        out = jnp.zeros((1, LANES), F32)
        for b in range(N_BUCKETS):
            tot = jnp.sum(jnp.sum(jnp.where(bk == b, d, 0.0), axis=-1, keepdims=True), axis=0, keepdims=True)
            out = out + jnp.where(lane == b, tot, 0.0)
        o_ref[...] = out

    hb = pl.BlockSpec((None, _WQ, _WK), lambda h: (h, 0, 0))
    return pl.pallas_call(
        body, grid=(8,), in_specs=[hb, hb, pl.BlockSpec((_WQ, _WK), lambda h: (0, 0))],
        out_specs=pl.BlockSpec((None, 1, LANES), lambda h: (h, 0, 0)),
        out_shape=jax.ShapeDtypeStruct((8, 1, LANES), F32),
        compiler_params=_cparams(("parallel",)), name="bias_bucket_reduce",
    )(db0.reshape(8, _WQ, _WK), db1.reshape(8, _WQ, _WK), bucket)


def _rope_tables(t):
    rows_n = t // GRID_W
    row = jnp.repeat(jnp.arange(rows_n, dtype=F32), GRID_W)
    col = jnp.tile(jnp.arange(GRID_W, dtype=F32), rows_n)
    half = HEAD_DIM // 2
    inv_freq = ROPE_THETA ** (-jnp.arange(0, half, 2, dtype=F32) / half)
    ang = jnp.concatenate([row[:, None] * inv_freq, col[:, None] * inv_freq], axis=-1)
    cos, sin = jnp.cos(ang), jnp.sin(ang)
    c64 = jnp.repeat(cos, 2, axis=-1)
    s64 = jnp.stack([-sin, sin], axis=-1).reshape(t, HEAD_DIM)
    return jnp.tile(c64, (1, 2)), jnp.tile(s64, (1, 2))


def _t5_bucket(rel):
    half = N_BUCKETS // 2
    max_exact = half // 2
    bucket = jnp.where(rel > 0, half, 0)
    rp = jnp.abs(rel)
    rpf = jnp.maximum(rp, 1).astype(jnp.float32)
    large = max_exact + (jnp.log(rpf / max_exact) / math.log(MAX_DISTANCE / max_exact)
                         * (half - max_exact)).astype(jnp.int32)
    large = jnp.minimum(large, half - 1)
    return bucket + jnp.where(rp < max_exact, rp, large)


def _window_tables(rel_bias):
    qpos = jnp.arange(_WQ, dtype=jnp.int32)
    kpos = jnp.arange(_WK, dtype=jnp.int32) - _WQ
    rel = kpos[None, :] - qpos[:, None]
    bucket = _t5_bucket(rel)
    bias = rel_bias[bucket].transpose(2, 0, 1)
    bias = jnp.where((jnp.abs(rel) <= WINDOW)[None], bias, NEG)
    return bias.reshape(_WR, _WK), bucket


def _pad_rows(a):
    return jnp.pad(a, ((_WQ, _WQ), (0, 0)))


def _layer_fwd(x, p, tabs):
    cos_t, sin_t, bias = tabs
    h = _mm_nn(x, p["win"], F32, "in_proj")
    qa, ka, va, qb, kb, vb = _prep_fwd(h, cos_t, sin_t, p["qn"], p["kn"])
    oa, lse_a = _gattn_fwd(qa, ka, va)
    kbp, vbp = _pad_rows(kb), _pad_rows(vb)
    ob, lse_b = _wattn_fwd(qb, kbp, vbp, bias, p["sink"])
    y = _outnorm_fwd(oa, ob, p["ga"], p["gb"])
    z1 = _mm_nn(y, p["wout"], F32, "out_proj", res=x, res_scale=ALPHA)
    x1, x1b = _ln_fwd(z1, p["ln1g"], p["ln1b"])
    g = _mm_nn_brhs(x1b, p["wg"], F32, "gate_proj")
    u = _mm_nn_brhs(x1b, p["wu"], F32, "up_proj")
    hdn = _ffn_mid_fwd(g, u, p["cw"])
    z2 = _mm_ksum(hdn, p["wd"], _NN, F32, "down_proj", res=x1, res_scale=ALPHA)
    x2, _ = _ln_fwd(z2, p["ln2g"], p["ln2b"])
    saved = dict(x=x, h=h, qa=qa, ka=ka, va=va, qb=qb, kbp=kbp, vbp=vbp, oa=oa, ob=ob, lse_a=lse_a, lse_b=lse_b,
                 y=y, z1=z1, x1b=x1b, g=g, u=u, hdn=hdn, z2=z2)
    return x2, saved


def _layer_bwd(dx2, p, s, tabs):
    cos_t, sin_t, bias = tabs
    t = dx2.shape[0]
    dz2, dz2b, dln2g, dln2b = _ln_bwd(dx2, s["z2"], p["ln2g"])
    dhdn = _mm_nt_brhs(dz2b, p["wd"], F32, "down_dx")
    dwd = _mm_tn_batched(s["hdn"], dz2b, "down_dw", a_batched=True, b_batched=False)
    dg, du, stats = _ffn_mid_bwd(s["g"], s["u"], dhdn, p["cw"])
    dx1 = _mm_ksum(dg, p["wg"], _NT, F32, "gate_dx", res=dz2, res_scale=ALPHA)
    dx1 = _mm_ksum(du, p["wu"], _NT, F32, "up_dx", res=dx1, res_scale=1.0)
    dwg = _mm_tn_batched(s["x1b"], dg, "gate_dw", a_batched=False, b_batched=True)
    dwu = _mm_tn_batched(s["x1b"], du, "up_dw", a_batched=False, b_batched=True)
    dz1, dz1b, dln1g, dln1b = _ln_bwd(dx1, s["z1"], p["ln1g"])
    dy = _mm_nt(dz1b, p["wout"], F32, "out_dx")
    dwout = _mm_tn(s["y"], dz1b, "out_dw")
    doa, dob, dga, dgb = _outnorm_bwd(dy, s["oa"], s["ob"], p["ga"], p["gb"])
    dqa, dka, dva = _gattn_bwd(s["qa"], s["ka"], s["va"], s["oa"], doa, s["lse_a"])
    dqb, dkbp, dvbp, dbias, dsink = _wattn_bwd(s["qb"], s["kbp"], s["vbp"], bias, p["sink"], s["ob"], dob, s["lse_b"])
    dkb = lax.slice_in_dim(dkbp, _WQ, _WQ + t, axis=0)
    dvb = lax.slice_in_dim(dvbp, _WQ, _WQ + t, axis=0)
    dh, dqn, dkn = _prep_bwd(s["h"], cos_t, sin_t, p["qn"], p["kn"], dqa, dka, dva, dqb, dkb, dvb)
    dx = _mm_nt(dh, p["win"], F32, "in_dx", res=dz1, res_scale=ALPHA)
    dwin = _mm_tn(s["x"], dh, "in_dw")
    grads = dict(win=dwin, wout=dwout, wg=dwg, wu=dwu, wd=dwd, stats=stats, qn=dqn, kn=dkn, ga=dga, gb=dgb,
                 ln1g=dln1g, ln1b=dln1b, ln2g=dln2g, ln2b=dln2b, bias=dbias, sink=dsink)
    return dx, grads


def _prep_layer_params(l, win, wout, wg, wu, wd, cw, q_norm, k_norm, sink, out_norm_a, out_norm_b, conv_b,
                       ln1_g, ln1_b, ln2_g, ln2_b):
    win_full = win.transpose(1, 0, 2).reshape(D_MODEL, IN_COLS)
    row = lambda v: v.reshape(1, -1)
    return dict(
        win=_in_cols_to_pairs(win_full),
        wout=_mix_rows_to_pairs(wout.reshape(D_MODEL, D_MODEL)),
        wg=wg, wu=wu, wd=wd,
        cw=jnp.pad(cw, ((0, 0), (0, 5), (0, 0))) + jnp.pad(conv_b[l].reshape(N_SHARD, 1, FF_SH), ((0, 0), (3, 4), (0, 0))),
        qn=row(jnp.tile(q_norm[l], 2)), kn=row(jnp.tile(k_norm[l], 2)),
        ga=row(_to_pairs(out_norm_a[l], 0)), gb=row(_to_pairs(out_norm_b[l], 0)),
        ln1g=row(ln1_g[l]), ln1b=row(ln1_b[l]), ln2g=row(ln2_g[l]), ln2b=row(ln2_b[l]),
        sink=jnp.repeat(sink[l], _WQ).reshape(_WR, 1))


def _local_step(x, tgt, params, rel_bias):
    t = x.shape[0]
    cos_t, sin_t = _rope_tables(t)
    bias, bucket = _window_tables(rel_bias)
    tabs = (cos_t, sin_t, bias)
    saved = []
    for l in range(DEPTH):
        x, s = _layer_fwd(x, params[l], tabs)
        saved.append(s)
    dx, loss = _loss_grad(x, tgt)
    grads = [None] * DEPTH
    for l in reversed(range(DEPTH)):
        dx, grads[l] = _layer_bwd(dx, params[l], saved[l], tabs)
    dbucket = _bias_bucket_reduce(grads[0]["bias"], grads[1]["bias"], bucket)
    return loss, dx, grads, dbucket


_ANY = pl.BlockSpec(memory_space=pl.ANY)
_MESH = pl.DeviceIdType.MESH


def _mesh_pos():
    return lax.axis_index("x"), lax.axis_index("y"), lax.axis_index("c")


def _other_chips(x, y):
    return [(1 - x, y), (x, 1 - y), (1 - x, 1 - y)]


def _gather_shards(shards):
    n = len(shards)

    def body(*refs):
        ins, outs = refs[:n], refs[n:2 * n]
        send, recv, loc = refs[2 * n:]
        x, y, c = _mesh_pos()
        me = 2 * x + y
        chips = _other_chips(x, y)
        local = [pltpu.make_async_copy(ins[i], outs[i].at[me], loc.at[i]) for i in range(n)]
        for cp in local:
            cp.start()

        def remote(i, k, block):
            px, py = chips[k]
            return pltpu.make_async_remote_copy(ins[i], outs[i].at[block], send.at[i, k], recv.at[i, k],
                                                device_id=(px, py, c), device_id_type=_MESH)

        sends = [remote(i, k, me) for i in range(n) for k in range(3)]
        for cp in sends:
            cp.start()
        for i in range(n):
            for k, (px, py) in enumerate(chips):
                remote(i, k, 2 * px + py).wait_recv()
        for cp in sends:
            cp.wait_send()
        for cp in local:
            cp.wait()

    return pl.pallas_call(
        body, in_specs=[_ANY] * n, out_specs=[_ANY] * n,
        out_shape=[jax.ShapeDtypeStruct((N_SHARD,) + s.shape, s.dtype) for s in shards],
        scratch_shapes=[pltpu.SemaphoreType.DMA((n, 3)), pltpu.SemaphoreType.DMA((n, 3)), pltpu.SemaphoreType.DMA((n,))],
        name="gather_weights",
    )(*shards)


def _scatter_grads(grads):
    n = len(grads)

    def body(*refs):
        ins = [refs[DEPTH * i:DEPTH * (i + 1)] for i in range(n)]
        outs = refs[DEPTH * n:DEPTH * n + n]
        send, recv, loc = refs[DEPTH * n + n:]
        x, y, c = _mesh_pos()
        me = 2 * x + y
        chips = _other_chips(x, y)
        local = [pltpu.make_async_copy(ins[i][l].at[me], outs[i].at[3, l], loc.at[i, l])
                 for i in range(n) for l in range(DEPTH)]
        for cp in local:
            cp.start()

        def remote(i, l, k):
            px, py = chips[k]
            return pltpu.make_async_remote_copy(ins[i][l].at[2 * px + py], outs[i].at[k, l], send.at[i, l, k],
                                                recv.at[i, l, k], device_id=(px, py, c), device_id_type=_MESH)

        sends = [remote(i, l, k) for i in range(n) for l in range(DEPTH) for k in range(3)]
        for cp in sends:
            cp.start()
        for cp in sends:
            cp.wait_recv()
        for cp in sends:
            cp.wait_send()
        for cp in local:
            cp.wait()

    flat = [g for pair in grads for g in pair]
    return pl.pallas_call(
        body, in_specs=[_ANY] * len(flat), out_specs=[_ANY] * n,
        out_shape=[jax.ShapeDtypeStruct((N_SHARD, DEPTH) + g[0].shape[1:], g[0].dtype) for g in grads],
        scratch_shapes=[pltpu.SemaphoreType.DMA((n, DEPTH, 3)), pltpu.SemaphoreType.DMA((n, DEPTH, 3)),
                        pltpu.SemaphoreType.DMA((n, DEPTH))],
        name="scatter_grads",
    )(*flat)


def _swap_with_sibling(parts):
    n = len(parts)

    def body(*refs):
        ins, outs = refs[:n], refs[n:2 * n]
        send, recv = refs[2 * n:]
        x, y, c = _mesh_pos()
        copies = [pltpu.make_async_remote_copy(ins[i], outs[i], send.at[i], recv.at[i], device_id=(x, y, 1 - c),
                                               device_id_type=_MESH) for i in range(n)]
        for cp in copies:
            cp.start()
        for cp in copies:
            cp.wait_recv()
        for cp in copies:
            cp.wait_send()

    return pl.pallas_call(
        body, in_specs=[_ANY] * n, out_specs=[_ANY] * n,
        out_shape=[jax.ShapeDtypeStruct(p.shape, p.dtype) for p in parts],
        scratch_shapes=[pltpu.SemaphoreType.DMA((n,)), pltpu.SemaphoreType.DMA((n,))],
        name="swap_sibling",
    )(*parts)


N_DEV = 8


def _allreduce_small(packed):
    rows = packed.shape[0]

    def body(in_ref, out_ref, buf, send, recv, loc):
        x, y, c = _mesh_pos()
        me = 4 * x + 2 * y + c
        own = pltpu.make_async_copy(in_ref, buf.at[me], loc)
        own.start()

        def remote(m, block):
            peer = (x ^ (m >> 2), y ^ ((m >> 1) & 1), c ^ (m & 1))
            return pltpu.make_async_remote_copy(in_ref, buf.at[block], send.at[m - 1], recv.at[m - 1],
                                                device_id=peer, device_id_type=_MESH)

        sends = [remote(m, me) for m in range(1, N_DEV)]
        for cp in sends:
            cp.start()
        for m in range(1, N_DEV):
            remote(m, me ^ m).wait_recv()
        for cp in sends:
            cp.wait_send()
        own.wait()
        tot = buf[0]
        for d in range(1, N_DEV):
            tot = tot + buf[d]
        out_ref[...] = tot

    vm = pl.BlockSpec(memory_space=pltpu.VMEM)
    return pl.pallas_call(
        body, in_specs=[vm], out_specs=vm, out_shape=jax.ShapeDtypeStruct((rows, LANES), F32),
        scratch_shapes=[pltpu.VMEM((N_DEV, rows, LANES), F32), pltpu.SemaphoreType.DMA((N_DEV - 1,)),
                        pltpu.SemaphoreType.DMA((N_DEV - 1,)), pltpu.SemaphoreType.DMA(())],
        name="allreduce_small",
    )(packed)


def _shard_rows(r):
    return r // 2 if r % 32 == 0 else r


def _sum_slots(slots):
    _, _, r, cdim = slots.shape
    tr = _shard_rows(r)

    def body(a_ref, b_ref, c_ref, d_ref, o_ref):
        up = lambda ref: ref[...].astype(F32)
        o_ref[...] = ((up(d_ref) + up(a_ref)) + up(b_ref)) + up(c_ref)

    def spec(k):
        return pl.BlockSpec((None, None, tr, cdim), lambda l, i: (k, l, i, 0))

    return pl.pallas_call(
        body, grid=(DEPTH, r // tr), in_specs=[spec(0), spec(1), spec(2), spec(3)],
        out_specs=pl.BlockSpec((None, tr, cdim), lambda l, i: (l, i, 0)),
        out_shape=jax.ShapeDtypeStruct((DEPTH, r, cdim), F32),
        compiler_params=_cparams(("parallel", "parallel")), name="sum_slots",
    )(slots, slots, slots, slots)


def _adamw_math(w, g, m, v):
    m = ADAM_B1 * m + (1.0 - ADAM_B1) * g
    v = ADAM_B2 * v + (1.0 - ADAM_B2) * (g * g)
    m_hat = m / (1.0 - ADAM_B1 ** ADAM_STEP)
    v_hat = v / (1.0 - ADAM_B2 ** ADAM_STEP)
    delta = -ADAM_LR * (m_hat / (jnp.sqrt(v_hat) + ADAM_EPS) + ADAM_WD * w)
    return delta, m, v


def _adamw_big(ga, gb, w, m, v):
    _, r, cdim = w.shape
    tr = _shard_rows(r)

    def body(ga_ref, gb_ref, w_ref, m_ref, v_ref, g_out, d_out, m_out, v_out):
        g = ga_ref[...] + gb_ref[...]
        d, mn, vn = _adamw_math(w_ref[...], g, m_ref[...], v_ref[...])
        g_out[...] = g
        d_out[...] = d
        m_out[...] = mn
        v_out[...] = vn

    spec = pl.BlockSpec((None, tr, cdim), lambda l, i: (l, i, 0))
    shp = jax.ShapeDtypeStruct(w.shape, F32)
    return pl.pallas_call(
        body, grid=(DEPTH, r // tr), in_specs=[spec] * 5, out_specs=[spec] * 4, out_shape=[shp] * 4,
        compiler_params=_cparams(("parallel", "parallel")), name="adamw_big",
    )(ga, gb, w, m, v)


def _adamw_small(ws, gs, ms, vs):
    n = len(ws)

    def body(*refs):
        w_r, g_r, m_r, v_r = (refs[k * n:(k + 1) * n] for k in range(4))
        d_o, m_o, v_o = (refs[(4 + k) * n:(5 + k) * n] for k in range(3))
        for i in range(n):
            d, mn, vn = _adamw_math(w_r[i][...], g_r[i][...], m_r[i][...], v_r[i][...])
            d_o[i][...] = d
            m_o[i][...] = mn
            v_o[i][...] = vn

    vm = pl.BlockSpec(memory_space=pltpu.VMEM)
    shp = [jax.ShapeDtypeStruct(w.shape, F32) for w in ws]
    outs = pl.pallas_call(
        body, in_specs=[vm] * (4 * n), out_specs=[vm] * (3 * n), out_shape=shp * 3, name="adamw_small",
    )(*ws, *gs, *ms, *vs)
    return outs[:n], outs[n:2 * n], outs[2 * n:]


def _tile_rows(a):
    a = a.reshape(-1, LANES)
    pad = (-a.shape[0]) % 8
    return jnp.pad(a, ((0, pad), (0, 0))) if pad else a


_SMALL_LAYER_PARTS = (("qn", 8), ("kn", 8), ("sink", 8), ("ga", 8), ("gb", 8), ("ln1g", 8), ("ln1b", 8),
                      ("ln2g", 8), ("ln2b", 8), ("stats", N_SHARD * 8 * FF_SH // LANES))
_SMALL_HEAD_ROWS = 16
_SMALL_LAYER_ROWS = sum(r for _, r in _SMALL_LAYER_PARTS)


def _pack_small(loss, dbucket, grads):
    parts = [_tile_rows(loss), _tile_rows(dbucket)]
    for l in range(DEPTH):
        parts += [_tile_rows(grads[l][name]) for name, _ in _SMALL_LAYER_PARTS]
    return jnp.concatenate(parts, axis=0)


def _unpack_small(tot, chip):
    out = dict(loss=tot[0, 0], rel_bias=tot[8:16, :N_BUCKETS].T)
    per = {name: [] for name, _ in _SMALL_LAYER_PARTS}
    for l in range(DEPTH):
        base = _SMALL_HEAD_ROWS + l * _SMALL_LAYER_ROWS
        for name, rows in _SMALL_LAYER_PARTS:
            per[name].append(tot[base:base + rows])
            base += rows
    fold = lambda v: v[0, :HEAD_DIM] + v[0, HEAD_DIM:]
    out["q_norm"] = jnp.stack([fold(v) for v in per["qn"]])
    out["k_norm"] = jnp.stack([fold(v) for v in per["kn"]])
    out["sink"] = jnp.stack([jnp.sum(v, axis=1) for v in per["sink"]])
    out["out_norm_a"] = jnp.stack([_from_pairs(v[:4].reshape(Q_W), 0) for v in per["ga"]])
    out["out_norm_b"] = jnp.stack([_from_pairs(v[:4].reshape(Q_W), 0) for v in per["gb"]])
    for name, key in (("ln1_g", "ln1g"), ("ln1_b", "ln1b"), ("ln2_g", "ln2g"), ("ln2_b", "ln2b")):
        out[name] = jnp.stack([v.reshape(D_MODEL) for v in per[key]])
    stats = [v.reshape(N_SHARD, 8, FF_SH) for v in per["stats"]]
    out["conv_b"] = jnp.stack([s[:, 0, :].reshape(D_FF) for s in stats])
    out["conv_w"] = jnp.stack([lax.dynamic_index_in_dim(s, chip, 0, keepdims=False)[1:4] for s in stats])
    return out


_WEIGHTS = ("rel_bias", "w_in", "q_norm", "k_norm", "sink", "out_norm_a", "out_norm_b", "w_out", "ln1_g", "ln1_b",
            "w_gate", "w_up", "conv_w", "conv_b", "w_down", "ln2_g", "ln2_b")
_BIG = ("w_in", "w_out", "w_gate", "w_up", "w_down")
_SMALL = tuple(n for n in _WEIGHTS if n not in _BIG)


def _col_blocks(g, n):
    return g.reshape(g.shape[0], N_SHARD, n).transpose(1, 0, 2)


def kernel(x, rel_bias, w_in, q_norm, k_norm, sink, out_norm_a, out_norm_b, w_out, ln1_g, ln1_b, w_gate, w_up, conv_w, conv_b, w_down, ln2_g, ln2_b, loss_target, m_rel_bias, m_w_in, m_q_norm, m_k_norm, m_sink, m_out_norm_a, m_out_norm_b, m_w_out, m_ln1_g, m_ln1_b, m_w_gate, m_w_up, m_conv_w, m_conv_b, m_w_down, m_ln2_g, m_ln2_b, v_rel_bias, v_w_in, v_q_norm, v_k_norm, v_sink, v_out_norm_a, v_out_norm_b, v_w_out, v_ln1_g, v_ln1_b, v_w_gate, v_w_up, v_conv_w, v_conv_b, v_w_down, v_ln2_g, v_ln2_b):
    w = dict(rel_bias=rel_bias, w_in=w_in, q_norm=q_norm, k_norm=k_norm, sink=sink, out_norm_a=out_norm_a,
             out_norm_b=out_norm_b, w_out=w_out, ln1_g=ln1_g, ln1_b=ln1_b, w_gate=w_gate, w_up=w_up, conv_w=conv_w,
             conv_b=conv_b, w_down=w_down, ln2_g=ln2_g, ln2_b=ln2_b)
    m = dict(rel_bias=m_rel_bias, w_in=m_w_in, q_norm=m_q_norm, k_norm=m_k_norm, sink=m_sink, out_norm_a=m_out_norm_a,
             out_norm_b=m_out_norm_b, w_out=m_w_out, ln1_g=m_ln1_g, ln1_b=m_ln1_b, w_gate=m_w_gate, w_up=m_w_up,
             conv_w=m_conv_w, conv_b=m_conv_b, w_down=m_w_down, ln2_g=m_ln2_g, ln2_b=m_ln2_b)
    v = dict(rel_bias=v_rel_bias, w_in=v_w_in, q_norm=v_q_norm, k_norm=v_k_norm, sink=v_sink, out_norm_a=v_out_norm_a,
             out_norm_b=v_out_norm_b, w_out=v_w_out, ln1_g=v_ln1_g, ln1_b=v_ln1_b, w_gate=v_w_gate, w_up=v_w_up,
             conv_w=v_conv_w, conv_b=v_conv_b, w_down=v_w_down, ln2_g=v_ln2_g, ln2_b=v_ln2_b)
    chip = 2 * lax.axis_index("x") + lax.axis_index("y")

    shards = [w[name][l].astype(BF16) for l in range(DEPTH) for name in _BIG] + [conv_w]
    gathered = _gather_shards(shards)
    cw_all = gathered[-1]
    params = []
    for l in range(DEPTH):
        win, wout, wg, wu, wd = gathered[len(_BIG) * l:len(_BIG) * (l + 1)]
        params.append(_prep_layer_params(l, win, wout, wg, wu, wd, cw_all[:, l], q_norm, k_norm, sink, out_norm_a,
                                         out_norm_b, conv_b, ln1_g, ln1_b, ln2_g, ln2_b))

    loss, dx, grads, dbucket = _local_step(x[0], loss_target[0], params, rel_bias)

    small = _unpack_small(_allreduce_small(_pack_small(loss, dbucket, grads)), chip)

    blocked = []
    for l in range(DEPTH):
        g = grads[l]
        blocked.append(dict(
            w_in=_col_blocks(_in_cols_to_pairs(g["win"], _from_pairs), IN_SH),
            w_out=_mix_rows_to_pairs(g["wout"], _from_pairs).reshape(N_SHARD, OUT_SH, D_MODEL),
            w_gate=g["wg"], w_up=g["wu"], w_down=g["wd"]))
    slots = _scatter_grads([[blocked[l][name] for l in range(DEPTH)] for name in _BIG])
    partial = [_sum_slots(s) for s in slots]
    other = _swap_with_sibling(partial)

    grad, delta, new_m, new_v = {}, {}, {}, {}
    for i, name in enumerate(_BIG):
        grad[name], delta[name], new_m[name], new_v[name] = _adamw_big(partial[i], other[i], w[name], m[name], v[name])
    flat2 = lambda a: a.reshape(-1, a.shape[-1])
    ds, ms, vs = _adamw_small([flat2(w[n]) for n in _SMALL], [flat2(small[n]) for n in _SMALL],
                              [flat2(m[n]) for n in _SMALL], [flat2(v[n]) for n in _SMALL])
    for i, name in enumerate(_SMALL):
        grad[name] = small[name]
        delta[name] = ds[i].reshape(w[name].shape)
        new_m[name] = ms[i].reshape(w[name].shape)
        new_v[name] = vs[i].reshape(w[name].shape)

    return (small["loss"], dx[None], *[grad[n] for n in _WEIGHTS], *[delta[n] for n in _WEIGHTS],
            *[new_m[n] for n in _WEIGHTS], *[new_v[n] for n in _WEIGHTS])
```

```python
import functools
import math

import numpy as np
import jax
import jax.numpy as jnp
from jax import lax
from jax.experimental import pallas as pl
from jax.experimental.pallas import tpu as pltpu

F32 = jnp.float32
BF16 = jnp.bfloat16

D_MODEL = 1024
DEPTH = 2
HEAD_DIM = 64
Q_W = 512
KV_W = 128
IN_COLS = 2 * (Q_W + 2 * KV_W)
N_SHARD = 4
IN_SH = IN_COLS // N_SHARD
OUT_SH = D_MODEL // N_SHARD
D_FF = 2816
FF_SH = D_FF // N_SHARD
Q_BLOCK = 128
WINDOW = 128
N_BUCKETS = 32
MAX_DISTANCE = 128
GRID_W = 64
ROPE_THETA = 10000.0
ALPHA = (2.0 * DEPTH) ** 0.25
RMS_EPS = 1e-6
LN_EPS = 1e-5
NEG = -1e30
LANES = 128
VMEM_LIMIT = 56 * 1024 * 1024

ADAM_LR = 0.001
ADAM_B1 = 0.9
ADAM_B2 = 0.999
ADAM_EPS = 1e-08
ADAM_WD = 0.01
ADAM_STEP = 10

_NN = (((1,), (0,)), ((), ()))
_NT = (((1,), (1,)), ((), ()))
_TN = (((0,), (0,)), ((), ()))


def _dot(a, b, dims):
    return lax.dot_general(a.astype(BF16), b.astype(BF16), dims, preferred_element_type=F32)


def _cparams(sem, vmem=VMEM_LIMIT):
    return pltpu.CompilerParams(dimension_semantics=sem, vmem_limit_bytes=vmem)


def _regroup(a, axis, n_outer, n_inner):
    shp = a.shape
    a = a.reshape(shp[:axis] + (n_outer, n_inner, HEAD_DIM) + shp[axis + 1:])
    return jnp.swapaxes(a, axis, axis + 1).reshape(shp)


def _to_pairs(a, axis):
    return _regroup(a, axis, 2, 4)


def _from_pairs(a, axis):
    return _regroup(a, axis, 4, 2)


def _in_cols_to_pairs(w, fn=_to_pairs):
    return jnp.concatenate([fn(w[..., :Q_W], w.ndim - 1), w[..., Q_W:Q_W + 2 * KV_W],
                            fn(w[..., Q_W + 2 * KV_W:2 * Q_W + 2 * KV_W], w.ndim - 1),
                            w[..., 2 * Q_W + 2 * KV_W:]], axis=-1)


def _mix_rows_to_pairs(w, fn=_to_pairs):
    return fn(w.reshape(2, Q_W, w.shape[-1]), 1).reshape(w.shape)


def _matmul(a, b, *, dims, grid, a_spec, b_spec, o_spec, out_shape, acc_shape, name, res=None,
            res_spec=None, res_scale=1.0):
    nk = grid[-1]
    kax = len(grid) - 1

    def body(*refs):
        if res is None:
            a_ref, b_ref, o_ref, acc = refs
            r_ref = None
        else:
            a_ref, b_ref, r_ref, o_ref, acc = refs
        k = pl.program_id(kax)

        @pl.when(k == 0)
        def _():
            acc[...] = jnp.zeros_like(acc)

        acc[...] += _dot(a_ref[...], b_ref[...], dims)

        @pl.when(k == nk - 1)
        def _():
            o = acc[...]
            if r_ref is not None:
                o = o + res_scale * r_ref[...]
            o_ref[...] = o.astype(o_ref.dtype)

    in_specs = [a_spec, b_spec] + ([res_spec] if res is not None else [])
    args = (a, b) + ((res,) if res is not None else ())
    sem = ("parallel",) * kax + ("arbitrary",)
    return pl.pallas_call(
        body, grid=grid, in_specs=in_specs, out_specs=o_spec, out_shape=out_shape,
        scratch_shapes=[pltpu.VMEM(acc_shape, F32)], compiler_params=_cparams(sem), name=name,
    )(*args)


def _mm_nn(a, b, out_dtype, name, tm=512, res=None, res_scale=1.0):
    m, kd = a.shape
    n = b.shape[1]
    return _matmul(
        a, b, dims=_NN, grid=(m // tm, 1),
        a_spec=pl.BlockSpec((tm, kd), lambda i, k: (i, 0)),
        b_spec=pl.BlockSpec((kd, n), lambda i, k: (0, 0)),
        o_spec=pl.BlockSpec((tm, n), lambda i, k: (i, 0)),
        out_shape=jax.ShapeDtypeStruct((m, n), out_dtype), acc_shape=(tm, n), name=name,
        res=res, res_spec=pl.BlockSpec((tm, n), lambda i, k: (i, 0)), res_scale=res_scale)


def _mm_nt(a, b, out_dtype, name, tm=512, res=None, res_scale=1.0):
    m, kd = a.shape
    n = b.shape[0]
    return _matmul(
        a, b, dims=_NT, grid=(m // tm, 1),
        a_spec=pl.BlockSpec((tm, kd), lambda i, k: (i, 0)),
        b_spec=pl.BlockSpec((n, kd), lambda i, k: (0, 0)),
        o_spec=pl.BlockSpec((tm, n), lambda i, k: (i, 0)),
        out_shape=jax.ShapeDtypeStruct((m, n), out_dtype), acc_shape=(tm, n), name=name,
        res=res, res_spec=pl.BlockSpec((tm, n), lambda i, k: (i, 0)), res_scale=res_scale)


def _mm_tn(a, b, name, tk=512, tn=None, out_dtype=BF16):
    t, m = a.shape
    n = b.shape[1]
    tn = n if tn is None else tn
    return _matmul(
        a, b, dims=_TN, grid=(n // tn, t // tk),
        a_spec=pl.BlockSpec((tk, m), lambda j, k: (k, 0)),
        b_spec=pl.BlockSpec((tk, tn), lambda j, k: (k, j)),
        o_spec=pl.BlockSpec((m, tn), lambda j, k: (0, j)),
        out_shape=jax.ShapeDtypeStruct((m, n), out_dtype), acc_shape=(m, tn), name=name)


def _mm_nn_brhs(a, b, out_dtype, name, tm=512):
    m, kd = a.shape
    nb, _, n = b.shape
    return _matmul(
        a, b, dims=_NN, grid=(nb, m // tm, 1),
        a_spec=pl.BlockSpec((tm, kd), lambda j, i, k: (i, 0)),
        b_spec=pl.BlockSpec((None, kd, n), lambda j, i, k: (j, 0, 0)),
        o_spec=pl.BlockSpec((None, tm, n), lambda j, i, k: (j, i, 0)),
        out_shape=jax.ShapeDtypeStruct((nb, m, n), out_dtype), acc_shape=(tm, n), name=name)


def _mm_nt_brhs(a, b, out_dtype, name, tm=512):
    m, kd = a.shape
    nb, n, _ = b.shape
    return _matmul(
        a, b, dims=_NT, grid=(nb, m // tm, 1),
        a_spec=pl.BlockSpec((tm, kd), lambda j, i, k: (i, 0)),
        b_spec=pl.BlockSpec((None, n, kd), lambda j, i, k: (j, 0, 0)),
        o_spec=pl.BlockSpec((None, tm, n), lambda j, i, k: (j, i, 0)),
        out_shape=jax.ShapeDtypeStruct((nb, m, n), out_dtype), acc_shape=(tm, n), name=name)


def _mm_ksum(a, b, dims, out_dtype, name, tm=512, res=None, res_scale=1.0):
    nb, m, kd = a.shape
    n = b.shape[2] if dims == _NN else b.shape[1]
    return _matmul(
        a, b, dims=dims, grid=(m // tm, nb),
        a_spec=pl.BlockSpec((None, tm, kd), lambda i, k: (k, i, 0)),
        b_spec=pl.BlockSpec((None,) + b.shape[1:], lambda i, k: (k, 0, 0)),
        o_spec=pl.BlockSpec((tm, n), lambda i, k: (i, 0)),
        out_shape=jax.ShapeDtypeStruct((m, n), out_dtype), acc_shape=(tm, n), name=name,
        res=res, res_spec=pl.BlockSpec((tm, n), lambda i, k: (i, 0)), res_scale=res_scale)


def _mm_tn_batched(a, b, name, a_batched, b_batched, tk=512, out_dtype=BF16):
    nb = a.shape[0] if a_batched else b.shape[0]
    t, m = a.shape[-2:]
    n = b.shape[-1]
    if a_batched:
        a_spec = pl.BlockSpec((None, tk, m), lambda j, k: (j, k, 0))
    else:
        a_spec = pl.BlockSpec((tk, m), lambda j, k: (k, 0))
    if b_batched:
        b_spec = pl.BlockSpec((None, tk, n), lambda j, k: (j, k, 0))
    else:
        b_spec = pl.BlockSpec((tk, n), lambda j, k: (k, 0))
    return _matmul(
        a, b, dims=_TN, grid=(nb, t // tk), a_spec=a_spec, b_spec=b_spec,
        o_spec=pl.BlockSpec((None, m, n), lambda j, k: (j, 0, 0)),
        out_shape=jax.ShapeDtypeStruct((nb, m, n), out_dtype), acc_shape=(m, n), name=name)


def _row_spec(tm, n):
    return pl.BlockSpec((tm, n), lambda i: (i, 0))


def _par_spec(n, rows=1):
    return pl.BlockSpec((rows, n), lambda i: (0, 0))


def _swap_pairs(x):
    lane = lax.broadcasted_iota(jnp.int32, x.shape, 1)
    return jnp.where(lane % 2 == 0, pltpu.roll(x, LANES - 1, 1), pltpu.roll(x, 1, 1))


def _head_sums(v):
    lo = lax.broadcasted_iota(jnp.int32, v.shape, 1) < HEAD_DIM
    s_lo = jnp.sum(jnp.where(lo, v, 0.0), axis=-1, keepdims=True)
    s_hi = jnp.sum(jnp.where(lo, 0.0, v), axis=-1, keepdims=True)
    return jnp.where(lo, s_lo, s_hi)


def _qk_blocks():
    return [(128 * i, True) for i in range(4)] + [(Q_W, False)]


def _prep_fwd(h, cos_t, sin_t, qn, kn, tm=256):
    t = h.shape[0]
    scale = HEAD_DIM ** -0.5

    def body(h_ref, c_ref, s_ref, qn_ref, kn_ref, qa_ref, ka_ref, va_ref, qb_ref, kb_ref, vb_ref):
        c = c_ref[...]
        s = s_ref[...]
        for start, is_q in _qk_blocks():
            x = h_ref[:, start:start + LANES]
            r = lax.rsqrt(_head_sums(x * x) * (1.0 / HEAD_DIM) + RMS_EPS)
            y = x * r * (qn_ref[...] if is_q else kn_ref[...])
            y = y * c + _swap_pairs(y) * s
            if is_q:
                qa_ref[:, start:start + LANES] = (y * scale).astype(BF16)
            else:
                ka_ref[...] = y.astype(BF16)
        va_ref[...] = h_ref[:, 640:768].astype(BF16)
        qb_ref[...] = (h_ref[:, 768:1280] * scale).astype(BF16)
        kb_ref[...] = h_ref[:, 1280:1408].astype(BF16)
        vb_ref[...] = h_ref[:, 1408:1536].astype(BF16)

    sd = jax.ShapeDtypeStruct
    return pl.pallas_call(
        body, grid=(t // tm,),
        in_specs=[_row_spec(tm, IN_COLS), _row_spec(tm, LANES), _row_spec(tm, LANES), _par_spec(LANES), _par_spec(LANES)],
        out_specs=[_row_spec(tm, Q_W), _row_spec(tm, KV_W), _row_spec(tm, KV_W),
                   _row_spec(tm, Q_W), _row_spec(tm, KV_W), _row_spec(tm, KV_W)],
        out_shape=[sd((t, Q_W), BF16), sd((t, KV_W), BF16), sd((t, KV_W), BF16),
                   sd((t, Q_W), BF16), sd((t, KV_W), BF16), sd((t, KV_W), BF16)],
        compiler_params=_cparams(("parallel",)), name="prep_fwd",
    )(h, cos_t, sin_t, qn, kn)


def _prep_bwd(h, cos_t, sin_t, qn, kn, dqa, dka, dva, dqb, dkb, dvb, tm=256):
    t = h.shape[0]
    scale = HEAD_DIM ** -0.5

    def body(h_ref, c_ref, s_ref, qn_ref, kn_ref, dqa_ref, dka_ref, dva_ref, dqb_ref, dkb_ref, dvb_ref,
             dh_ref, dqn_ref, dkn_ref):
        @pl.when(pl.program_id(0) == 0)
        def _():
            dqn_ref[...] = jnp.zeros_like(dqn_ref)
            dkn_ref[...] = jnp.zeros_like(dkn_ref)

        c = c_ref[...]
        s = s_ref[...]
        for start, is_q in _qk_blocks():
            x = h_ref[:, start:start + LANES]
            gain = qn_ref[...] if is_q else kn_ref[...]
            d = dqa_ref[:, start:start + LANES] * scale if is_q else dka_ref[...]
            dy = d * c + _swap_pairs(d * s)
            r = lax.rsqrt(_head_sums(x * x) * (1.0 / HEAD_DIM) + RMS_EPS)
            xr = x * r
            gsum = jnp.sum(dy * xr, axis=0, keepdims=True)
            if is_q:
                dqn_ref[...] += gsum
            else:
                dkn_ref[...] += gsum
            gy = dy * gain
            dx = r * (gy - xr * (_head_sums(xr * gy) * (1.0 / HEAD_DIM)))
            dh_ref[:, start:start + LANES] = dx.astype(BF16)
        dh_ref[:, 640:768] = dva_ref[...].astype(BF16)
        dh_ref[:, 768:1280] = (dqb_ref[...] * scale).astype(BF16)
        dh_ref[:, 1280:1408] = dkb_ref[...].astype(BF16)
        dh_ref[:, 1408:1536] = dvb_ref[...].astype(BF16)

    sd = jax.ShapeDtypeStruct
    return pl.pallas_call(
        body, grid=(t // tm,),
        in_specs=[_row_spec(tm, IN_COLS), _row_spec(tm, LANES), _row_spec(tm, LANES), _par_spec(LANES), _par_spec(LANES),
                  _row_spec(tm, Q_W), _row_spec(tm, KV_W), _row_spec(tm, KV_W),
                  _row_spec(tm, Q_W), _row_spec(tm, KV_W), _row_spec(tm, KV_W)],
        out_specs=[_row_spec(tm, IN_COLS), _par_spec(LANES), _par_spec(LANES)],
        out_shape=[sd((t, IN_COLS), BF16), sd((1, LANES), F32), sd((1, LANES), F32)],
        compiler_params=_cparams(("arbitrary",)), name="prep_bwd",
    )(h, cos_t, sin_t, qn, kn, dqa, dka, dva, dqb, dkb, dvb)


def _outnorm_fwd(oa, ob, ga, gb, tm=512):
    t = oa.shape[0]

    def body(oa_ref, ob_ref, ga_ref, gb_ref, y_ref):
        for o_ref, g_ref, start in ((oa_ref, ga_ref, 0), (ob_ref, gb_ref, Q_W)):
            x = o_ref[...]
            r = lax.rsqrt(jnp.mean(x * x, axis=-1, keepdims=True) + RMS_EPS)
            y_ref[:, start:start + Q_W] = (x * r * g_ref[...]).astype(BF16)

    return pl.pallas_call(
        body, grid=(t // tm,),
        in_specs=[_row_spec(tm, Q_W), _row_spec(tm, Q_W), _par_spec(Q_W), _par_spec(Q_W)],
        out_specs=_row_spec(tm, D_MODEL), out_shape=jax.ShapeDtypeStruct((t, D_MODEL), BF16),
        compiler_params=_cparams(("parallel",)), name="outnorm_fwd",
    )(oa, ob, ga, gb)


def _outnorm_bwd(dy, oa, ob, ga, gb, tm=512):
    t = oa.shape[0]

    def body(dy_ref, oa_ref, ob_ref, ga_ref, gb_ref, doa_ref, dob_ref, dga_ref, dgb_ref):
        @pl.when(pl.program_id(0) == 0)
        def _():
            dga_ref[...] = jnp.zeros_like(dga_ref)
            dgb_ref[...] = jnp.zeros_like(dgb_ref)

        for o_ref, g_ref, do_ref, dg_ref, start in ((oa_ref, ga_ref, doa_ref, dga_ref, 0),
                                                    (ob_ref, gb_ref, dob_ref, dgb_ref, Q_W)):
            x = o_ref[...]
            d = dy_ref[:, start:start + Q_W]
            r = lax.rsqrt(jnp.mean(x * x, axis=-1, keepdims=True) + RMS_EPS)
            xr = x * r
            dg_ref[...] += jnp.sum(d * xr, axis=0, keepdims=True)
            gy = d * g_ref[...]
            do_ref[...] = r * (gy - xr * jnp.mean(xr * gy, axis=-1, keepdims=True))

    sd = jax.ShapeDtypeStruct
    return pl.pallas_call(
        body, grid=(t // tm,),
        in_specs=[_row_spec(tm, D_MODEL), _row_spec(tm, Q_W), _row_spec(tm, Q_W), _par_spec(Q_W), _par_spec(Q_W)],
        out_specs=[_row_spec(tm, Q_W), _row_spec(tm, Q_W), _par_spec(Q_W), _par_spec(Q_W)],
        out_shape=[sd((t, Q_W), F32), sd((t, Q_W), F32), sd((1, Q_W), F32), sd((1, Q_W), F32)],
        compiler_params=_cparams(("arbitrary",)), name="outnorm_bwd",
    )(dy, oa, ob, ga, gb)


def _ln_fwd(z, g, b, tm=512):
    t = z.shape[0]

    def body(z_ref, g_ref, b_ref, x_ref, xb_ref):
        zz = z_ref[...]
        mu = jnp.mean(zz, axis=-1, keepdims=True)
        zc = zz - mu
        r = lax.rsqrt(jnp.mean(zc * zc, axis=-1, keepdims=True) + LN_EPS)
        y = zc * r * g_ref[...] + b_ref[...]
        x_ref[...] = y
        xb_ref[...] = y.astype(BF16)

    sd = jax.ShapeDtypeStruct
    return pl.pallas_call(
        body, grid=(t // tm,),
        in_specs=[_row_spec(tm, D_MODEL), _par_spec(D_MODEL), _par_spec(D_MODEL)],
        out_specs=[_row_spec(tm, D_MODEL), _row_spec(tm, D_MODEL)],
        out_shape=[sd((t, D_MODEL), F32), sd((t, D_MODEL), BF16)],
        compiler_params=_cparams(("parallel",)), name="ln_fwd",
    )(z, g, b)


def _ln_bwd(d, z, g, tm=512):
    t = z.shape[0]

    def body(d_ref, z_ref, g_ref, dz_ref, dzb_ref, dg_ref, db_ref):
        @pl.when(pl.program_id(0) == 0)
        def _():
            dg_ref[...] = jnp.zeros_like(dg_ref)
            db_ref[...] = jnp.zeros_like(db_ref)

        zz = z_ref[...]
        dd = d_ref[...]
        mu = jnp.mean(zz, axis=-1, keepdims=True)
        zc = zz - mu
        r = lax.rsqrt(jnp.mean(zc * zc, axis=-1, keepdims=True) + LN_EPS)
        xh = zc * r
        dg_ref[...] += jnp.sum(dd * xh, axis=0, keepdims=True)
        db_ref[...] += jnp.sum(dd, axis=0, keepdims=True)
        dxh = dd * g_ref[...]
        dz = r * (dxh - jnp.mean(dxh, axis=-1, keepdims=True) - xh * jnp.mean(dxh * xh, axis=-1, keepdims=True))
        dz_ref[...] = dz
        dzb_ref[...] = dz.astype(BF16)

    sd = jax.ShapeDtypeStruct
    return pl.pallas_call(
        body, grid=(t // tm,),
        in_specs=[_row_spec(tm, D_MODEL), _row_spec(tm, D_MODEL), _par_spec(D_MODEL)],
        out_specs=[_row_spec(tm, D_MODEL), _row_spec(tm, D_MODEL), _par_spec(D_MODEL), _par_spec(D_MODEL)],
        out_shape=[sd((t, D_MODEL), F32), sd((t, D_MODEL), BF16), sd((1, D_MODEL), F32), sd((1, D_MODEL), F32)],
        compiler_params=_cparams(("arbitrary",)), name="ln_bwd",
    )(d, z, g)


def _loss_grad(y, tgt, tm=512):
    t = y.shape[0]
    nsteps = t // tm

    def body(y_ref, t_ref, dy_ref, loss_ref, acc):
        i = pl.program_id(0)

        @pl.when(i == 0)
        def _():
            acc[...] = jnp.zeros_like(acc)

        e = y_ref[...] - t_ref[...]
        dy_ref[...] = e * (1.0 / D_MODEL)
        acc[...] += jnp.sum(e * e, axis=0, keepdims=True)

        @pl.when(i == nsteps - 1)
        def _():
            tot = jnp.sum(acc[...], axis=-1, keepdims=True) * (0.5 / D_MODEL)
            loss_ref[...] = jnp.broadcast_to(tot, loss_ref.shape)

    sd = jax.ShapeDtypeStruct
    return pl.pallas_call(
        body, grid=(nsteps,),
        in_specs=[_row_spec(tm, D_MODEL), _row_spec(tm, D_MODEL)],
        out_specs=[_row_spec(tm, D_MODEL), _par_spec(LANES)],
        out_shape=[sd((t, D_MODEL), F32), sd((1, LANES), F32)],
        scratch_shapes=[pltpu.VMEM((1, D_MODEL), F32)],
        compiler_params=_cparams(("arbitrary",)), name="loss_grad",
    )(y, tgt)


_GELU_C = math.sqrt(2.0 / math.pi)
_GELU_K = 0.044715
HALO = 8


def _gelu_parts(x):
    th = jnp.tanh(_GELU_C * (x + _GELU_K * x * x * x))
    return 0.5 * x * (1.0 + th), th


def _halo_specs(tm, t, shift=0):
    last = t // HALO - 1
    cur = pl.BlockSpec((None, tm, FF_SH), lambda j, i: (j + shift, i, 0))
    prev = pl.BlockSpec((None, HALO, FF_SH), lambda j, i: (j + shift, jnp.maximum(i * (tm // HALO) - 1, 0), 0))
    nxt = pl.BlockSpec((None, HALO, FF_SH), lambda j, i: (j + shift, jnp.minimum((i + 1) * (tm // HALO), last), 0))
    return [prev, cur, nxt]


def _ffn_mid_fwd(g, u, cw, tm=512):
    t = g.shape[1]
    nsteps = t // tm

    def body(gp_ref, g_ref, gn_ref, u_ref, cw_ref, h_ref):
        i = pl.program_id(1)
        gg = g_ref[...]
        row = lax.broadcasted_iota(jnp.int32, gg.shape, 0)
        prev = jnp.where(i == 0, 0.0, gp_ref[HALO - 1:HALO, :])
        nxt = jnp.where(i == nsteps - 1, 0.0, gn_ref[0:1, :])
        g_m1 = jnp.where(row == 0, prev, pltpu.roll(gg, 1, 0))
        g_p1 = jnp.where(row == tm - 1, nxt, pltpu.roll(gg, tm - 1, 0))
        gc = cw_ref[3:4, :] + g_m1 * cw_ref[0:1, :] + gg * cw_ref[1:2, :] + g_p1 * cw_ref[2:3, :]
        act, _ = _gelu_parts(gc)
        h_ref[...] = (act * u_ref[...]).astype(BF16)

    return pl.pallas_call(
        body, grid=(N_SHARD, nsteps),
        in_specs=_halo_specs(tm, t) + [pl.BlockSpec((None, tm, FF_SH), lambda j, i: (j, i, 0)),
                                       pl.BlockSpec((None, 8, FF_SH), lambda j, i: (j, 0, 0))],
        out_specs=pl.BlockSpec((None, tm, FF_SH), lambda j, i: (j, i, 0)),
        out_shape=jax.ShapeDtypeStruct((N_SHARD, t, FF_SH), BF16),
        compiler_params=_cparams(("parallel", "parallel")), name="ffn_mid_fwd",
    )(g, g, g, u, cw)


def _ffn_mid_bwd(g, u, dh, cw, tm=512):
    t = g.shape[1]
    nsteps = t // tm
    te = tm + 2 * HALO

    def body(gp_ref, g_ref, gn_ref, up_ref, u_ref, un_ref, dp_ref, d_ref, dn_ref, cw_ref, dg_ref, du_ref, st_ref):
        i = pl.program_id(1)

        @pl.when(i == 0)
        def _():
            st_ref[...] = jnp.zeros_like(st_ref)

        e = lax.broadcasted_iota(jnp.int32, (te, FF_SH), 0)
        tg = i * tm - HALO + e
        valid = (tg >= 0) & (tg < t)

        def ext(p_ref, c_ref, n_ref):
            return jnp.where(valid, jnp.concatenate([p_ref[...], c_ref[...], n_ref[...]], axis=0), 0.0)

        eg = ext(gp_ref, g_ref, gn_ref)
        eu = ext(up_ref, u_ref, un_ref)
        ed = ext(dp_ref, d_ref, dn_ref)
        w0, w1, w2 = cw_ref[0:1, :], cw_ref[1:2, :], cw_ref[2:3, :]
        g_m1 = pltpu.roll(eg, 1, 0)
        g_p1 = pltpu.roll(eg, te - 1, 0)
        gc = cw_ref[3:4, :] + g_m1 * w0 + eg * w1 + g_p1 * w2
        act, th = _gelu_parts(gc)
        dact = 0.5 * (1.0 + th) + 0.5 * gc * (1.0 - th * th) * _GELU_C * (1.0 + 3.0 * _GELU_K * gc * gc)
        dgc = ed * eu * dact
        dg = pltpu.roll(dgc, te - 1, 0) * w0 + dgc * w1 + pltpu.roll(dgc, 1, 0) * w2
        mid = slice(HALO, HALO + tm)
        dg_ref[...] = dg[mid].astype(BF16)
        du_ref[...] = (ed * act)[mid].astype(BF16)
        sel = dgc[mid]
        parts = [jnp.sum(sel, axis=0, keepdims=True),
                 jnp.sum(sel * g_m1[mid], axis=0, keepdims=True),
                 jnp.sum(sel * eg[mid], axis=0, keepdims=True),
                 jnp.sum(sel * g_p1[mid], axis=0, keepdims=True)]
        r8 = lax.broadcasted_iota(jnp.int32, (8, FF_SH), 0)
        upd = jnp.zeros((8, FF_SH), F32)
        for k, p in enumerate(parts):
            upd = upd + jnp.where(r8 == k, p, 0.0)
        st_ref[...] += upd

    blk = pl.BlockSpec((None, tm, FF_SH), lambda j, i: (j, i, 0))
    sd = jax.ShapeDtypeStruct
    return pl.pallas_call(
        body, grid=(N_SHARD, nsteps),
        in_specs=_halo_specs(tm, t) + _halo_specs(tm, t) + _halo_specs(tm, t)
        + [pl.BlockSpec((None, 8, FF_SH), lambda j, i: (j, 0, 0))],
        out_specs=[blk, blk, pl.BlockSpec((None, 8, FF_SH), lambda j, i: (j, 0, 0))],
        out_shape=[sd((N_SHARD, t, FF_SH), BF16), sd((N_SHARD, t, FF_SH), BF16), sd((N_SHARD, 8, FF_SH), F32)],
        compiler_params=_cparams(("parallel", "arbitrary")), name="ffn_mid_bwd",
    )(g, g, g, u, u, u, dh, dh, dh, cw)


def _lo_mask(rows):
    return lax.broadcasted_iota(jnp.int32, (rows, LANES), 1) < HEAD_DIM


def _stack_heads(src_ref, dst_ref, tq):
    lo = _lo_mask(tq)
    for i in range(4):
        blk = src_ref[:, LANES * i:LANES * (i + 1)].astype(dst_ref.dtype)
        zero = jnp.zeros_like(blk)
        dst_ref[tq * i:tq * (i + 1), :] = jnp.where(lo, blk, zero)
        dst_ref[tq * (4 + i):tq * (5 + i), :] = jnp.where(lo, zero, blk)


def _unstack_heads(st, dst_ref, tq):
    lo = _lo_mask(tq)
    for i in range(4):
        dst_ref[:, LANES * i:LANES * (i + 1)] = jnp.where(
            lo, st[tq * i:tq * (i + 1)], st[tq * (4 + i):tq * (5 + i)]).astype(dst_ref.dtype)


def _stacked_delta(do_ref, o_ref, tq):
    lo = _lo_mask(tq)
    los, his = [], []
    for i in range(4):
        pr = do_ref[:, LANES * i:LANES * (i + 1)] * o_ref[:, LANES * i:LANES * (i + 1)]
        los.append(jnp.sum(jnp.where(lo, pr, 0.0), axis=-1, keepdims=True))
        his.append(jnp.sum(jnp.where(lo, 0.0, pr), axis=-1, keepdims=True))
    return jnp.concatenate(los + his, axis=0)


def _lane_chunks(a):
    return [a[:, LANES * c:LANES * (c + 1)] for c in range(a.shape[1] // LANES)]


def _gattn_fwd(q, k, v, tq=128, tk=2048):
    t = q.shape[0]
    tk = min(tk, t)
    nq, nk, r = t // tq, t // tk, 8 * tq

    def body(q_ref, k_ref, v_ref, o_ref, lse_ref, qs, m_s, l_s, acc):
        _stack_heads(q_ref, qs, tq)
        m_s[...] = jnp.full_like(m_s, NEG)

        def max_step(j, carry):
            off = pl.multiple_of(j * tk, tk)
            s = _dot(qs[...], k_ref[pl.ds(off, tk), :], _NT)
            mp = m_s[...]
            for sc in _lane_chunks(s):
                mp = jnp.maximum(mp, sc)
            m_s[...] = mp
            return carry

        lax.fori_loop(0, nk, max_step, 0)
        m_row = jnp.max(m_s[...], axis=-1, keepdims=True)
        m_s[...] = jnp.broadcast_to(m_row, m_s.shape)
        l_s[...] = jnp.zeros_like(l_s)
        acc[...] = jnp.zeros_like(acc)

        def sum_step(j, carry):
            off = pl.multiple_of(j * tk, tk)
            s = _dot(qs[...], k_ref[pl.ds(off, tk), :], _NT)
            m_rep = m_s[...]
            ps = [jnp.exp(sc - m_rep) for sc in _lane_chunks(s)]
            lp = l_s[...]
            for pc in ps:
                lp = lp + pc
            l_s[...] = lp
            p = jnp.concatenate([pc.astype(BF16) for pc in ps], axis=1)
            acc[...] += _dot(p, v_ref[pl.ds(off, tk), :], _NN)
            return carry

        lax.fori_loop(0, nk, sum_step, 0)
        l_row = jnp.sum(l_s[...], axis=-1, keepdims=True)
        _unstack_heads(acc[...] / l_row, o_ref, tq)
        lse_ref[...] = m_row + jnp.log(l_row)

    sd = jax.ShapeDtypeStruct
    return pl.pallas_call(
        body, grid=(nq,),
        in_specs=[_row_spec(tq, Q_W), _par_spec(KV_W, t), _par_spec(KV_W, t)],
        out_specs=[_row_spec(tq, Q_W), _row_spec(r, 1)],
        out_shape=[sd((t, Q_W), F32), sd((nq * r, 1), F32)],
        scratch_shapes=[pltpu.VMEM((r, LANES), BF16), pltpu.VMEM((r, LANES), F32), pltpu.VMEM((r, LANES), F32),
                        pltpu.VMEM((r, LANES), F32)],
        compiler_params=_cparams(("parallel",)), name="gattn_fwd",
    )(q, k, v)


def _gattn_bwd(q, k, v, o, do, lse, tq=128, tk=512):
    t = q.shape[0]
    tk = min(tk, t)
    nq, nk, r = t // tq, t // tk, 8 * tq

    def body(q_ref, k_ref, v_ref, o_ref, do_ref, lse_ref, dq_ref, dk_ref, dv_ref, qs, dos, dqa):
        @pl.when(pl.program_id(0) == 0)
        def _():
            dk_ref[...] = jnp.zeros_like(dk_ref)
            dv_ref[...] = jnp.zeros_like(dv_ref)

        _stack_heads(q_ref, qs, tq)
        _stack_heads(do_ref, dos, tq)
        delta = _stacked_delta(do_ref, o_ref, tq)
        lse_v = lse_ref[...]
        dqa[...] = jnp.zeros_like(dqa)

        def step(j, carry):
            off = pl.multiple_of(j * tk, tk)
            kc = k_ref[pl.ds(off, tk), :]
            vc = v_ref[pl.ds(off, tk), :]
            p = jnp.exp(_dot(qs[...], kc, _NT) - lse_v)
            dp = _dot(dos[...], vc, _NT)
            ds = (p * (dp - delta)).astype(BF16)
            dqa[...] += _dot(ds, kc, _NN)
            dk_ref[pl.ds(off, tk), :] += _dot(ds, qs[...], _TN)
            dv_ref[pl.ds(off, tk), :] += _dot(p, dos[...], _TN)
            return carry

        lax.fori_loop(0, nk, step, 0)
        _unstack_heads(dqa[...], dq_ref, tq)

    sd = jax.ShapeDtypeStruct
    return pl.pallas_call(
        body, grid=(nq,),
        in_specs=[_row_spec(tq, Q_W), _par_spec(KV_W, t), _par_spec(KV_W, t), _row_spec(tq, Q_W), _row_spec(tq, Q_W),
                  _row_spec(r, 1)],
        out_specs=[_row_spec(tq, Q_W), _par_spec(KV_W, t), _par_spec(KV_W, t)],
        out_shape=[sd((t, Q_W), F32), sd((t, KV_W), F32), sd((t, KV_W), F32)],
        scratch_shapes=[pltpu.VMEM((r, LANES), BF16), pltpu.VMEM((r, LANES), BF16), pltpu.VMEM((r, LANES), F32)],
        compiler_params=_cparams(("arbitrary",)), name="gattn_bwd",
    )(q, k, v, o, do, lse)


_WQ = Q_BLOCK
_WK = 3 * Q_BLOCK
_WR = 8 * _WQ
_WNB = 2


def _wattn_scores(qs, kw, bias_ref, n, t):
    col = lax.broadcasted_iota(jnp.int32, (1, _WK), 1)
    kabs = (n - 1) * _WQ + col
    s = _dot(qs[...], kw, _NT) + bias_ref[...]
    return jnp.where((kabs >= 0) & (kabs < t), s, NEG)


def _wattn_fwd(q, kp, vp, bias, sink):
    t = q.shape[0]
    nq = t // _WQ

    def body(q_ref, k_ref, v_ref, b_ref, sk_ref, o_ref, lse_ref, qs):
        sk = sk_ref[...]
        for b in range(_WNB):
            n = pl.program_id(0) * _WNB + b
            rows = pl.ds(b * _WQ, _WQ)
            _stack_heads(q_ref.at[rows, :], qs.at[b], _WQ)
            off = pl.multiple_of(n * _WQ, _WQ)
            kw = k_ref[pl.ds(off, _WK), :]
            vw = v_ref[pl.ds(off, _WK), :]
            s = _wattn_scores(qs.at[b], kw, b_ref, n, t)
            m = jnp.maximum(jnp.max(s, axis=-1, keepdims=True), sk)
            p = jnp.exp(s - m)
            l = jnp.sum(p, axis=-1, keepdims=True) + jnp.exp(sk - m)
            _unstack_heads(_dot(p, vw, _NN) / l, o_ref.at[rows, :], _WQ)
            lse_ref[pl.ds(b * _WR, _WR), :] = m + jnp.log(l)

    sd = jax.ShapeDtypeStruct
    return pl.pallas_call(
        body, grid=(nq // _WNB,),
        in_specs=[_row_spec(_WNB * _WQ, Q_W), _par_spec(KV_W, t + 2 * _WQ), _par_spec(KV_W, t + 2 * _WQ),
                  _par_spec(_WK, _WR), _par_spec(1, _WR)],
        out_specs=[_row_spec(_WNB * _WQ, Q_W), _row_spec(_WNB * _WR, 1)],
        out_shape=[sd((t, Q_W), F32), sd((nq * _WR, 1), F32)],
        scratch_shapes=[pltpu.VMEM((_WNB, _WR, LANES), BF16)],
        compiler_params=_cparams(("parallel",)), name="wattn_fwd",
    )(q, kp, vp, bias, sink)


def _wattn_bwd(q, kp, vp, bias, sink, o, do, lse):
    t = q.shape[0]
    nq = t // _WQ

    def body(q_ref, k_ref, v_ref, b_ref, sk_ref, o_ref, do_ref, lse_ref, dq_ref, dk_ref, dv_ref, db_ref, dsk_ref, qs, dos):
        @pl.when(pl.program_id(0) == 0)
        def _():
            dk_ref[...] = jnp.zeros_like(dk_ref)
            dv_ref[...] = jnp.zeros_like(dv_ref)
            db_ref[...] = jnp.zeros_like(db_ref)
            dsk_ref[...] = jnp.zeros_like(dsk_ref)

        dbias = jnp.zeros((_WR, _WK), F32)
        dsink = jnp.zeros((_WR, 1), F32)
        parts = []
        for b in range(_WNB):
            n = pl.program_id(0) * _WNB + b
            rows = pl.ds(b * _WQ, _WQ)
            _stack_heads(q_ref.at[rows, :], qs.at[b], _WQ)
            _stack_heads(do_ref.at[rows, :], dos.at[b], _WQ)
            delta = _stacked_delta(do_ref.at[rows, :], o_ref.at[rows, :], _WQ)
            off = pl.multiple_of(n * _WQ, _WQ)
            kw = k_ref[pl.ds(off, _WK), :]
            vw = v_ref[pl.ds(off, _WK), :]
            lse_v = lse_ref[pl.ds(b * _WR, _WR), :]
            p = jnp.exp(_wattn_scores(qs.at[b], kw, b_ref, n, t) - lse_v)
            dp = _dot(dos[b], vw, _NT)
            ds = p * (dp - delta)
            dbias = dbias + ds
            dsink = dsink - jnp.exp(sk_ref[...] - lse_v) * delta
            dsb = ds.astype(BF16)
            _unstack_heads(_dot(dsb, kw, _NN), dq_ref.at[rows, :], _WQ)
            parts.append((off, _dot(dsb, qs[b], _TN), _dot(p, dos[b], _TN)))
        db_ref[...] += dbias
        dsk_ref[...] += dsink
        for off, dkw, dvw in parts:
            dk_ref[pl.ds(off, _WK), :] += dkw
            dv_ref[pl.ds(off, _WK), :] += dvw

    sd = jax.ShapeDtypeStruct
    tp = t + 2 * _WQ
    qb = _row_spec(_WNB * _WQ, Q_W)
    return pl.pallas_call(
        body, grid=(nq // _WNB,),
        in_specs=[qb, _par_spec(KV_W, tp), _par_spec(KV_W, tp), _par_spec(_WK, _WR), _par_spec(1, _WR),
                  qb, qb, _row_spec(_WNB * _WR, 1)],
        out_specs=[qb, _par_spec(KV_W, tp), _par_spec(KV_W, tp), _par_spec(_WK, _WR), _par_spec(1, _WR)],
        out_shape=[sd((t, Q_W), F32), sd((tp, KV_W), F32), sd((tp, KV_W), F32), sd((_WR, _WK), F32), sd((_WR, 1), F32)],
        scratch_shapes=[pltpu.VMEM((_WNB, _WR, LANES), BF16), pltpu.VMEM((_WNB, _WR, LANES), BF16)],
        compiler_params=_cparams(("arbitrary",)), name="wattn_bwd",
    )(q, kp, vp, bias, sink, o, do, lse)


def _bias_bucket_reduce(db0, db1, bucket):
    def body(a_ref, b_ref, bk_ref, o_ref):
        d = a_ref[...] + b_ref[...]
        bk = bk_ref[...]
        lane = lax.broadcasted_iota(jnp.int32, (1, LANES), 1)
        out = jnp.zeros((1, LANES), F32)
        for b in range(N_BUCKETS):
            tot = jnp.sum(jnp.sum(jnp.where(bk == b, d, 0.0), axis=-1, keepdims=True), axis=0, keepdims=True)
            out = out + jnp.where(lane == b, tot, 0.0)
        o_ref[...] = out

    hb = pl.BlockSpec((None, _WQ, _WK), lambda h: (h, 0, 0))
    return pl.pallas_call(
        body, grid=(8,), in_specs=[hb, hb, pl.BlockSpec((_WQ, _WK), lambda h: (0, 0))],
        out_specs=pl.BlockSpec((None, 1, LANES), lambda h: (h, 0, 0)),
        out_shape=jax.ShapeDtypeStruct((8, 1, LANES), F32),
        compiler_params=_cparams(("parallel",)), name="bias_bucket_reduce",
    )(db0.reshape(8, _WQ, _WK), db1.reshape(8, _WQ, _WK), bucket)


def _rope_tables(t):
    rows_n = t // GRID_W
    row = jnp.repeat(jnp.arange(rows_n, dtype=F32), GRID_W)
    col = jnp.tile(jnp.arange(GRID_W, dtype=F32), rows_n)
    half = HEAD_DIM // 2
    inv_freq = ROPE_THETA ** (-jnp.arange(0, half, 2, dtype=F32) / half)
    ang = jnp.concatenate([row[:, None] * inv_freq, col[:, None] * inv_freq], axis=-1)
    cos, sin = jnp.cos(ang), jnp.sin(ang)
    c64 = jnp.repeat(cos, 2, axis=-1)
    s64 = jnp.stack([-sin, sin], axis=-1).reshape(t, HEAD_DIM)
    return jnp.tile(c64, (1, 2)), jnp.tile(s64, (1, 2))


def _t5_bucket(rel):
    half = N_BUCKETS // 2
    max_exact = half // 2
    bucket = jnp.where(rel > 0, half, 0)
    rp = jnp.abs(rel)
    rpf = jnp.maximum(rp, 1).astype(jnp.float32)
    large = max_exact + (jnp.log(rpf / max_exact) / math.log(MAX_DISTANCE / max_exact)
                         * (half - max_exact)).astype(jnp.int32)
    large = jnp.minimum(large, half - 1)
    return bucket + jnp.where(rp < max_exact, rp, large)


def _window_tables(rel_bias):
    qpos = jnp.arange(_WQ, dtype=jnp.int32)
    kpos = jnp.arange(_WK, dtype=jnp.int32) - _WQ
    rel = kpos[None, :] - qpos[:, None]
    bucket = _t5_bucket(rel)
    bias = jnp.zeros((8, _WQ, _WK), F32)
    for b in range(N_BUCKETS):
        bias = jnp.where((bucket == b)[None], rel_bias[b][:, None, None], bias)
    bias = jnp.where((jnp.abs(rel) <= WINDOW)[None], bias, NEG)
    return bias.reshape(_WR, _WK), bucket


def _pad_rows(a):
    return jnp.pad(a, ((_WQ, _WQ), (0, 0)))


def _layer_fwd(x, p, tabs):
    cos_t, sin_t, bias = tabs
    h = _mm_nn(x, p["win"], F32, "in_proj")
    qa, ka, va, qb, kb, vb = _prep_fwd(h, cos_t, sin_t, p["qn"], p["kn"])
    oa, lse_a = _gattn_fwd(qa, ka, va)
    kbp, vbp = _pad_rows(kb), _pad_rows(vb)
    ob, lse_b = _wattn_fwd(qb, kbp, vbp, bias, p["sink"])
    y = _outnorm_fwd(oa, ob, p["ga"], p["gb"])
    z1 = _mm_nn(y, p["wout"], F32, "out_proj", res=x, res_scale=ALPHA)
    x1, x1b = _ln_fwd(z1, p["ln1g"], p["ln1b"])
    g = _mm_nn_brhs(x1b, p["wg"], F32, "gate_proj")
    u = _mm_nn_brhs(x1b, p["wu"], F32, "up_proj")
    hdn = _ffn_mid_fwd(g, u, p["cw"])
    z2 = _mm_ksum(hdn, p["wd"], _NN, F32, "down_proj", res=x1, res_scale=ALPHA)
    x2, _ = _ln_fwd(z2, p["ln2g"], p["ln2b"])
    saved = dict(x=x, h=h, qa=qa, ka=ka, va=va, qb=qb, kbp=kbp, vbp=vbp, oa=oa, ob=ob, lse_a=lse_a, lse_b=lse_b,
                 y=y, z1=z1, x1b=x1b, g=g, u=u, hdn=hdn, z2=z2)
    return x2, saved


def _layer_bwd(dx2, p, s, tabs):
    cos_t, sin_t, bias = tabs
    t = dx2.shape[0]
    dz2, dz2b, dln2g, dln2b = _ln_bwd(dx2, s["z2"], p["ln2g"])
    dhdn = _mm_nt_brhs(dz2b, p["wd"], F32, "down_dx")
    dwd = _mm_tn_batched(s["hdn"], dz2b, "down_dw", a_batched=True, b_batched=False)
    dg, du, stats = _ffn_mid_bwd(s["g"], s["u"], dhdn, p["cw"])
    dx1 = _mm_ksum(dg, p["wg"], _NT, F32, "gate_dx", res=dz2, res_scale=ALPHA)
    dx1 = _mm_ksum(du, p["wu"], _NT, F32, "up_dx", res=dx1, res_scale=1.0)
    dwg = _mm_tn_batched(s["x1b"], dg, "gate_dw", a_batched=False, b_batched=True)
    dwu = _mm_tn_batched(s["x1b"], du, "up_dw", a_batched=False, b_batched=True)
    dz1, dz1b, dln1g, dln1b = _ln_bwd(dx1, s["z1"], p["ln1g"])
    dy = _mm_nt(dz1b, p["wout"], F32, "out_dx")
    dwout = _mm_tn(s["y"], dz1b, "out_dw")
    doa, dob, dga, dgb = _outnorm_bwd(dy, s["oa"], s["ob"], p["ga"], p["gb"])
    dqa, dka, dva = _gattn_bwd(s["qa"], s["ka"], s["va"], s["oa"], doa, s["lse_a"])
    dqb, dkbp, dvbp, dbias, dsink = _wattn_bwd(s["qb"], s["kbp"], s["vbp"], bias, p["sink"], s["ob"], dob, s["lse_b"])
    dkb = lax.slice_in_dim(dkbp, _WQ, _WQ + t, axis=0)
    dvb = lax.slice_in_dim(dvbp, _WQ, _WQ + t, axis=0)
    dh, dqn, dkn = _prep_bwd(s["h"], cos_t, sin_t, p["qn"], p["kn"], dqa, dka, dva, dqb, dkb, dvb)
    dx = _mm_nt(dh, p["win"], F32, "in_dx", res=dz1, res_scale=ALPHA)
    dwin = _mm_tn(s["x"], dh, "in_dw")
    grads = dict(win=dwin, wout=dwout, wg=dwg, wu=dwu, wd=dwd, stats=stats, qn=dqn, kn=dkn, ga=dga, gb=dgb,
                 ln1g=dln1g, ln1b=dln1b, ln2g=dln2g, ln2b=dln2b, bias=dbias, sink=dsink)
    return dx, grads


def _prep_layer_params(l, win, wout, wg, wu, wd, cw, q_norm, k_norm, sink, out_norm_a, out_norm_b, conv_b,
                       ln1_g, ln1_b, ln2_g, ln2_b):
    win_full = win.transpose(1, 0, 2).reshape(D_MODEL, IN_COLS)
    row = lambda v: v.reshape(1, -1)
    return dict(
        win=_in_cols_to_pairs(win_full),
        wout=_mix_rows_to_pairs(wout.reshape(D_MODEL, D_MODEL)),
        wg=wg, wu=wu, wd=wd,
        cw=jnp.pad(cw, ((0, 0), (0, 5), (0, 0))) + jnp.pad(conv_b[l].reshape(N_SHARD, 1, FF_SH), ((0, 0), (3, 4), (0, 0))),
        qn=row(jnp.tile(q_norm[l], 2)), kn=row(jnp.tile(k_norm[l], 2)),
        ga=row(_to_pairs(out_norm_a[l], 0)), gb=row(_to_pairs(out_norm_b[l], 0)),
        ln1g=row(ln1_g[l]), ln1b=row(ln1_b[l]), ln2g=row(ln2_g[l]), ln2b=row(ln2_b[l]),
        sink=jnp.repeat(sink[l], _WQ).reshape(_WR, 1))


def _local_step(x, tgt, params, rel_bias):
    t = x.shape[0]
    cos_t, sin_t = _rope_tables(t)
    bias, bucket = _window_tables(rel_bias)
    tabs = (cos_t, sin_t, bias)
    saved = []
    for l in range(DEPTH):
        x, s = _layer_fwd(x, params[l], tabs)
        saved.append(s)
    dx, loss = _loss_grad(x, tgt)
    grads = [None] * DEPTH
    for l in reversed(range(DEPTH)):
        dx, grads[l] = _layer_bwd(dx, params[l], saved[l], tabs)
    dbucket = _bias_bucket_reduce(grads[0]["bias"], grads[1]["bias"], bucket)
    return loss, dx, grads, dbucket


_ANY = pl.BlockSpec(memory_space=pl.ANY)
_MESH = pl.DeviceIdType.MESH


def _mesh_pos():
    return lax.axis_index("x"), lax.axis_index("y"), lax.axis_index("c")


def _other_chips(x, y):
    return [(1 - x, y), (x, 1 - y), (1 - x, 1 - y)]


def _gather_shards(shards):
    n = len(shards)

    def body(*refs):
        ins, outs = refs[:n], refs[n:2 * n]
        send, recv, loc = refs[2 * n:]
        x, y, c = _mesh_pos()
        me = 2 * x + y
        chips = _other_chips(x, y)
        local = [pltpu.make_async_copy(ins[i], outs[i].at[me], loc.at[i]) for i in range(n)]
        for cp in local:
            cp.start()

        def remote(i, k, block):
            px, py = chips[k]
            return pltpu.make_async_remote_copy(ins[i], outs[i].at[block], send.at[i, k], recv.at[i, k],
                                                device_id=(px, py, c), device_id_type=_MESH)

        sends = [remote(i, k, me) for i in range(n) for k in range(3)]
        for cp in sends:
            cp.start()
        for i in range(n):
            for k, (px, py) in enumerate(chips):
                remote(i, k, 2 * px + py).wait_recv()
        for cp in sends:
            cp.wait_send()
        for cp in local:
            cp.wait()

    return pl.pallas_call(
        body, in_specs=[_ANY] * n, out_specs=[_ANY] * n,
        out_shape=[jax.ShapeDtypeStruct((N_SHARD,) + s.shape, s.dtype) for s in shards],
        scratch_shapes=[pltpu.SemaphoreType.DMA((n, 3)), pltpu.SemaphoreType.DMA((n, 3)), pltpu.SemaphoreType.DMA((n,))],
        name="gather_weights",
    )(*shards)


def _scatter_grads(grads):
    n = len(grads)

    def body(*refs):
        ins = [refs[DEPTH * i:DEPTH * (i + 1)] for i in range(n)]
        outs = refs[DEPTH * n:DEPTH * n + n]
        send, recv, loc = refs[DEPTH * n + n:]
        x, y, c = _mesh_pos()
        me = 2 * x + y
        chips = _other_chips(x, y)
        local = [pltpu.make_async_copy(ins[i][l].at[me], outs[i].at[3, l], loc.at[i, l])
                 for i in range(n) for l in range(DEPTH)]
        for cp in local:
            cp.start()

        def remote(i, l, k):
            px, py = chips[k]
            return pltpu.make_async_remote_copy(ins[i][l].at[2 * px + py], outs[i].at[k, l], send.at[i, l, k],
                                                recv.at[i, l, k], device_id=(px, py, c), device_id_type=_MESH)

        sends = [remote(i, l, k) for i in range(n) for l in range(DEPTH) for k in range(3)]
        for cp in sends:
            cp.start()
        for cp in sends:
            cp.wait_recv()
        for cp in sends:
            cp.wait_send()
        for cp in local:
            cp.wait()

    flat = [g for pair in grads for g in pair]
    return pl.pallas_call(
        body, in_specs=[_ANY] * len(flat), out_specs=[_ANY] * n,
        out_shape=[jax.ShapeDtypeStruct((N_SHARD, DEPTH) + g[0].shape[1:], g[0].dtype) for g in grads],
        scratch_shapes=[pltpu.SemaphoreType.DMA((n, DEPTH, 3)), pltpu.SemaphoreType.DMA((n, DEPTH, 3)),
                        pltpu.SemaphoreType.DMA((n, DEPTH))],
        name="scatter_grads",
    )(*flat)


def _swap_with_sibling(parts):
    n = len(parts)

    def body(*refs):
        ins, outs = refs[:n], refs[n:2 * n]
        send, recv = refs[2 * n:]
        x, y, c = _mesh_pos()
        copies = [pltpu.make_async_remote_copy(ins[i], outs[i], send.at[i], recv.at[i], device_id=(x, y, 1 - c),
                                               device_id_type=_MESH) for i in range(n)]
        for cp in copies:
            cp.start()
        for cp in copies:
            cp.wait_recv()
        for cp in copies:
            cp.wait_send()

    return pl.pallas_call(
        body, in_specs=[_ANY] * n, out_specs=[_ANY] * n,
        out_shape=[jax.ShapeDtypeStruct(p.shape, p.dtype) for p in parts],
        scratch_shapes=[pltpu.SemaphoreType.DMA((n,)), pltpu.SemaphoreType.DMA((n,))],
        name="swap_sibling",
    )(*parts)


N_DEV = 8


def _allreduce_small(packed):
    rows = packed.shape[0]

    def body(in_ref, out_ref, buf, send, recv, loc):
        x, y, c = _mesh_pos()
        me = 4 * x + 2 * y + c
        own = pltpu.make_async_copy(in_ref, buf.at[me], loc)
        own.start()

        def remote(m, block):
            peer = (x ^ (m >> 2), y ^ ((m >> 1) & 1), c ^ (m & 1))
            return pltpu.make_async_remote_copy(in_ref, buf.at[block], send.at[m - 1], recv.at[m - 1],
                                                device_id=peer, device_id_type=_MESH)

        sends = [remote(m, me) for m in range(1, N_DEV)]
        for cp in sends:
            cp.start()
        for m in range(1, N_DEV):
            remote(m, me ^ m).wait_recv()
        for cp in sends:
            cp.wait_send()
        own.wait()
        tot = buf[0]
        for d in range(1, N_DEV):
            tot = tot + buf[d]
        out_ref[...] = tot

    vm = pl.BlockSpec(memory_space=pltpu.VMEM)
    return pl.pallas_call(
        body, in_specs=[vm], out_specs=vm, out_shape=jax.ShapeDtypeStruct((rows, LANES), F32),
        scratch_shapes=[pltpu.VMEM((N_DEV, rows, LANES), F32), pltpu.SemaphoreType.DMA((N_DEV - 1,)),
                        pltpu.SemaphoreType.DMA((N_DEV - 1,)), pltpu.SemaphoreType.DMA(())],
        name="allreduce_small",
    )(packed)


def _shard_rows(r):
    return r // 2 if r % 32 == 0 else r


def _sum_slots(slots):
    _, _, r, cdim = slots.shape
    tr = _shard_rows(r)

    def body(a_ref, b_ref, c_ref, d_ref, o_ref):
        up = lambda ref: ref[...].astype(F32)
        o_ref[...] = ((up(d_ref) + up(a_ref)) + up(b_ref)) + up(c_ref)

    def spec(k):
        return pl.BlockSpec((None, None, tr, cdim), lambda l, i: (k, l, i, 0))

    return pl.pallas_call(
        body, grid=(DEPTH, r // tr), in_specs=[spec(0), spec(1), spec(2), spec(3)],
        out_specs=pl.BlockSpec((None, tr, cdim), lambda l, i: (l, i, 0)),
        out_shape=jax.ShapeDtypeStruct((DEPTH, r, cdim), F32),
        compiler_params=_cparams(("parallel", "parallel")), name="sum_slots",
    )(slots, slots, slots, slots)


def _adamw_math(w, g, m, v):
    m = ADAM_B1 * m + (1.0 - ADAM_B1) * g
    v = ADAM_B2 * v + (1.0 - ADAM_B2) * (g * g)
    m_hat = m / (1.0 - ADAM_B1 ** ADAM_STEP)
    v_hat = v / (1.0 - ADAM_B2 ** ADAM_STEP)
    delta = -ADAM_LR * (m_hat / (jnp.sqrt(v_hat) + ADAM_EPS) + ADAM_WD * w)
    return delta, m, v


def _adamw_big(ga, gb, w, m, v):
    _, r, cdim = w.shape
    tr = _shard_rows(r)

    def body(ga_ref, gb_ref, w_ref, m_ref, v_ref, g_out, d_out, m_out, v_out):
        g = ga_ref[...] + gb_ref[...]
        d, mn, vn = _adamw_math(w_ref[...], g, m_ref[...], v_ref[...])
        g_out[...] = g
        d_out[...] = d
        m_out[...] = mn
        v_out[...] = vn

    spec = pl.BlockSpec((None, tr, cdim), lambda l, i: (l, i, 0))
    shp = jax.ShapeDtypeStruct(w.shape, F32)
    return pl.pallas_call(
        body, grid=(DEPTH, r // tr), in_specs=[spec] * 5, out_specs=[spec] * 4, out_shape=[shp] * 4,
        compiler_params=_cparams(("parallel", "parallel")), name="adamw_big",
    )(ga, gb, w, m, v)


def _adamw_small(ws, gs, ms, vs):
    n = len(ws)

    def body(*refs):
        w_r, g_r, m_r, v_r = (refs[k * n:(k + 1) * n] for k in range(4))
        d_o, m_o, v_o = (refs[(4 + k) * n:(5 + k) * n] for k in range(3))
        for i in range(n):
            d, mn, vn = _adamw_math(w_r[i][...], g_r[i][...], m_r[i][...], v_r[i][...])
            d_o[i][...] = d
            m_o[i][...] = mn
            v_o[i][...] = vn

    vm = pl.BlockSpec(memory_space=pltpu.VMEM)
    shp = [jax.ShapeDtypeStruct(w.shape, F32) for w in ws]
    outs = pl.pallas_call(
        body, in_specs=[vm] * (4 * n), out_specs=[vm] * (3 * n), out_shape=shp * 3, name="adamw_small",
    )(*ws, *gs, *ms, *vs)
    return outs[:n], outs[n:2 * n], outs[2 * n:]


def _tile_rows(a):
    a = a.reshape(-1, LANES)
    pad = (-a.shape[0]) % 8
    return jnp.pad(a, ((0, pad), (0, 0))) if pad else a


_SMALL_LAYER_PARTS = (("qn", 8), ("kn", 8), ("sink", 8), ("ga", 8), ("gb", 8), ("ln1g", 8), ("ln1b", 8),
                      ("ln2g", 8), ("ln2b", 8), ("stats", N_SHARD * 8 * FF_SH // LANES))
_SMALL_HEAD_ROWS = 16
_SMALL_LAYER_ROWS = sum(r for _, r in _SMALL_LAYER_PARTS)


def _pack_small(loss, dbucket, grads):
    parts = [_tile_rows(loss), _tile_rows(dbucket)]
    for l in range(DEPTH):
        parts += [_tile_rows(grads[l][name]) for name, _ in _SMALL_LAYER_PARTS]
    return jnp.concatenate(parts, axis=0)


def _unpack_small(tot, chip):
    out = dict(loss=tot[0, 0], rel_bias=tot[8:16, :N_BUCKETS].T)
    per = {name: [] for name, _ in _SMALL_LAYER_PARTS}
    for l in range(DEPTH):
        base = _SMALL_HEAD_ROWS + l * _SMALL_LAYER_ROWS
        for name, rows in _SMALL_LAYER_PARTS:
            per[name].append(tot[base:base + rows])
            base += rows
    fold = lambda v: v[0, :HEAD_DIM] + v[0, HEAD_DIM:]
    out["q_norm"] = jnp.stack([fold(v) for v in per["qn"]])
    out["k_norm"] = jnp.stack([fold(v) for v in per["kn"]])
    out["sink"] = jnp.stack([jnp.sum(v, axis=1) for v in per["sink"]])
    out["out_norm_a"] = jnp.stack([_from_pairs(v[:4].reshape(Q_W), 0) for v in per["ga"]])
    out["out_norm_b"] = jnp.stack([_from_pairs(v[:4].reshape(Q_W), 0) for v in per["gb"]])
    for name, key in (("ln1_g", "ln1g"), ("ln1_b", "ln1b"), ("ln2_g", "ln2g"), ("ln2_b", "ln2b")):
        out[name] = jnp.stack([v.reshape(D_MODEL) for v in per[key]])
    stats = [v.reshape(N_SHARD, 8, FF_SH) for v in per["stats"]]
    out["conv_b"] = jnp.stack([s[:, 0, :].reshape(D_FF) for s in stats])
    out["conv_w"] = jnp.stack([lax.dynamic_index_in_dim(s, chip, 0, keepdims=False)[1:4] for s in stats])
    return out


_WEIGHTS = ("rel_bias", "w_in", "q_norm", "k_norm", "sink", "out_norm_a", "out_norm_b", "w_out", "ln1_g", "ln1_b",
            "w_gate", "w_up", "conv_w", "conv_b", "w_down", "ln2_g", "ln2_b")
_BIG = ("w_in", "w_out", "w_gate", "w_up", "w_down")
_SMALL = tuple(n for n in _WEIGHTS if n not in _BIG)


def _col_blocks(g, n):
    return g.reshape(g.shape[0], N_SHARD, n).transpose(1, 0, 2)


def kernel(x, rel_bias, w_in, q_norm, k_norm, sink, out_norm_a, out_norm_b, w_out, ln1_g, ln1_b, w_gate, w_up, conv_w, conv_b, w_down, ln2_g, ln2_b, loss_target, m_rel_bias, m_w_in, m_q_norm, m_k_norm, m_sink, m_out_norm_a, m_out_norm_b, m_w_out, m_ln1_g, m_ln1_b, m_w_gate, m_w_up, m_conv_w, m_conv_b, m_w_down, m_ln2_g, m_ln2_b, v_rel_bias, v_w_in, v_q_norm, v_k_norm, v_sink, v_out_norm_a, v_out_norm_b, v_w_out, v_ln1_g, v_ln1_b, v_w_gate, v_w_up, v_conv_w, v_conv_b, v_w_down, v_ln2_g, v_ln2_b):
    w = dict(rel_bias=rel_bias, w_in=w_in, q_norm=q_norm, k_norm=k_norm, sink=sink, out_norm_a=out_norm_a,
             out_norm_b=out_norm_b, w_out=w_out, ln1_g=ln1_g, ln1_b=ln1_b, w_gate=w_gate, w_up=w_up, conv_w=conv_w,
             conv_b=conv_b, w_down=w_down, ln2_g=ln2_g, ln2_b=ln2_b)
    m = dict(rel_bias=m_rel_bias, w_in=m_w_in, q_norm=m_q_norm, k_norm=m_k_norm, sink=m_sink, out_norm_a=m_out_norm_a,
             out_norm_b=m_out_norm_b, w_out=m_w_out, ln1_g=m_ln1_g, ln1_b=m_ln1_b, w_gate=m_w_gate, w_up=m_w_up,
             conv_w=m_conv_w, conv_b=m_conv_b, w_down=m_w_down, ln2_g=m_ln2_g, ln2_b=m_ln2_b)
    v = dict(rel_bias=v_rel_bias, w_in=v_w_in, q_norm=v_q_norm, k_norm=v_k_norm, sink=v_sink, out_norm_a=v_out_norm_a,
             out_norm_b=v_out_norm_b, w_out=v_w_out, ln1_g=v_ln1_g, ln1_b=v_ln1_b, w_gate=v_w_gate, w_up=v_w_up,
             conv_w=v_conv_w, conv_b=v_conv_b, w_down=v_w_down, ln2_g=v_ln2_g, ln2_b=v_ln2_b)
    chip = 2 * lax.axis_index("x") + lax.axis_index("y")

    shards = [w[name][l].astype(BF16) for l in range(DEPTH) for name in _BIG] + [conv_w]
    gathered = _gather_shards(shards)
    cw_all = gathered[-1]
    params = []
    for l in range(DEPTH):
        win, wout, wg, wu, wd = gathered[len(_BIG) * l:len(_BIG) * (l + 1)]
        params.append(_prep_layer_params(l, win, wout, wg, wu, wd, cw_all[:, l], q_norm, k_norm, sink, out_norm_a,
                                         out_norm_b, conv_b, ln1_g, ln1_b, ln2_g, ln2_b))

    loss, dx, grads, dbucket = _local_step(x[0], loss_target[0], params, rel_bias)

    small = _unpack_small(_allreduce_small(_pack_small(loss, dbucket, grads)), chip)

    blocked = []
    for l in range(DEPTH):
        g = grads[l]
        blocked.append(dict(
            w_in=_col_blocks(_in_cols_to_pairs(g["win"], _from_pairs), IN_SH),
            w_out=_mix_rows_to_pairs(g["wout"], _from_pairs).reshape(N_SHARD, OUT_SH, D_MODEL),
            w_gate=g["wg"], w_up=g["wu"], w_down=g["wd"]))
    slots = _scatter_grads([[blocked[l][name] for l in range(DEPTH)] for name in _BIG])
    partial = [_sum_slots(s) for s in slots]
    other = _swap_with_sibling(partial)

    grad, delta, new_m, new_v = {}, {}, {}, {}
    for i, name in enumerate(_BIG):
        grad[name], delta[name], new_m[name], new_v[name] = _adamw_big(partial[i], other[i], w[name], m[name], v[name])
    flat2 = lambda a: a.reshape(-1, a.shape[-1])
    ds, ms, vs = _adamw_small([flat2(w[n]) for n in _SMALL], [flat2(small[n]) for n in _SMALL],
                              [flat2(m[n]) for n in _SMALL], [flat2(v[n]) for n in _SMALL])
    for i, name in enumerate(_SMALL):
        grad[name] = small[name]
        delta[name] = ds[i].reshape(w[name].shape)
        new_m[name] = ms[i].reshape(w[name].shape)
        new_v[name] = vs[i].reshape(w[name].shape)

    return (small["loss"], dx[None], *[grad[n] for n in _WEIGHTS], *[delta[n] for n in _WEIGHTS],
            *[new_m[n] for n in _WEIGHTS], *[new_v[n] for n in _WEIGHTS])
```

```python
import functools
import math

import numpy as np
import jax
import jax.numpy as jnp
from jax import lax
from jax.experimental import pallas as pl
from jax.experimental.pallas import tpu as pltpu

F32 = jnp.float32
BF16 = jnp.bfloat16

D_MODEL = 1024
DEPTH = 2
HEAD_DIM = 64
Q_W = 512
KV_W = 128
IN_COLS = 2 * (Q_W + 2 * KV_W)
N_SHARD = 4
IN_SH = IN_COLS // N_SHARD
OUT_SH = D_MODEL // N_SHARD
D_FF = 2816
FF_SH = D_FF // N_SHARD
Q_BLOCK = 128
WINDOW = 128
N_BUCKETS = 32
MAX_DISTANCE = 128
GRID_W = 64
ROPE_THETA = 10000.0
ALPHA = (2.0 * DEPTH) ** 0.25
RMS_EPS = 1e-6
LN_EPS = 1e-5
NEG = -1e30
LANES = 128
VMEM_LIMIT = 56 * 1024 * 1024

ADAM_LR = 0.001
ADAM_B1 = 0.9
ADAM_B2 = 0.999
ADAM_EPS = 1e-08
ADAM_WD = 0.01
ADAM_STEP = 10

_NN = (((1,), (0,)), ((), ()))
_NT = (((1,), (1,)), ((), ()))
_TN = (((0,), (0,)), ((), ()))


def _dot(a, b, dims):
    return lax.dot_general(a.astype(BF16), b.astype(BF16), dims, preferred_element_type=F32)


def _cparams(sem, vmem=VMEM_LIMIT):
    return pltpu.CompilerParams(dimension_semantics=sem, vmem_limit_bytes=vmem)


def _regroup(a, axis, n_outer, n_inner):
    shp = a.shape
    a = a.reshape(shp[:axis] + (n_outer, n_inner, HEAD_DIM) + shp[axis + 1:])
    return jnp.swapaxes(a, axis, axis + 1).reshape(shp)


def _to_pairs(a, axis):
    return _regroup(a, axis, 2, 4)


def _from_pairs(a, axis):
    return _regroup(a, axis, 4, 2)


def _in_cols_to_pairs(w, fn=_to_pairs):
    return jnp.concatenate([fn(w[..., :Q_W], w.ndim - 1), w[..., Q_W:Q_W + 2 * KV_W],
                            fn(w[..., Q_W + 2 * KV_W:2 * Q_W + 2 * KV_W], w.ndim - 1),
                            w[..., 2 * Q_W + 2 * KV_W:]], axis=-1)


def _mix_rows_to_pairs(w, fn=_to_pairs):
    return fn(w.reshape(2, Q_W, w.shape[-1]), 1).reshape(w.shape)


def _matmul(a, b, *, dims, grid, a_spec, b_spec, o_spec, out_shape, acc_shape, name, res=None,
            res_spec=None, res_scale=1.0):
    nk = grid[-1]
    kax = len(grid) - 1

    def body(*refs):
        if res is None:
            a_ref, b_ref, o_ref, acc = refs
            r_ref = None
        else:
            a_ref, b_ref, r_ref, o_ref, acc = refs
        k = pl.program_id(kax)

        @pl.when(k == 0)
        def _():
            acc[...] = jnp.zeros_like(acc)

        acc[...] += _dot(a_ref[...], b_ref[...], dims)

        @pl.when(k == nk - 1)
        def _():
            o = acc[...]
            if r_ref is not None:
                o = o + res_scale * r_ref[...]
            o_ref[...] = o.astype(o_ref.dtype)

    in_specs = [a_spec, b_spec] + ([res_spec] if res is not None else [])
    args = (a, b) + ((res,) if res is not None else ())
    sem = ("parallel",) * kax + ("arbitrary",)
    return pl.pallas_call(
        body, grid=grid, in_specs=in_specs, out_specs=o_spec, out_shape=out_shape,
        scratch_shapes=[pltpu.VMEM(acc_shape, F32)], compiler_params=_cparams(sem), name=name,
    )(*args)


def _mm_nn(a, b, out_dtype, name, tm=512, res=None, res_scale=1.0):
    m, kd = a.shape
    n = b.shape[1]
    return _matmul(
        a, b, dims=_NN, grid=(m // tm, 1),
        a_spec=pl.BlockSpec((tm, kd), lambda i, k: (i, 0)),
        b_spec=pl.BlockSpec((kd, n), lambda i, k: (0, 0)),
        o_spec=pl.BlockSpec((tm, n), lambda i, k: (i, 0)),
        out_shape=jax.ShapeDtypeStruct((m, n), out_dtype), acc_shape=(tm, n), name=name,
        res=res, res_spec=pl.BlockSpec((tm, n), lambda i, k: (i, 0)), res_scale=res_scale)


def _mm_nt(a, b, out_dtype, name, tm=512, res=None, res_scale=1.0):
    m, kd = a.shape
    n = b.shape[0]
    return _matmul(
        a, b, dims=_NT, grid=(m // tm, 1),
        a_spec=pl.BlockSpec((tm, kd), lambda i, k: (i, 0)),
        b_spec=pl.BlockSpec((n, kd), lambda i, k: (0, 0)),
        o_spec=pl.BlockSpec((tm, n), lambda i, k: (i, 0)),
        out_shape=jax.ShapeDtypeStruct((m, n), out_dtype), acc_shape=(tm, n), name=name,
        res=res, res_spec=pl.BlockSpec((tm, n), lambda i, k: (i, 0)), res_scale=res_scale)


def _mm_tn(a, b, name, tk=512, tn=None, out_dtype=BF16):
    t, m = a.shape
    n = b.shape[1]
    tn = n if tn is None else tn
    return _matmul(
        a, b, dims=_TN, grid=(n // tn, t // tk),
        a_spec=pl.BlockSpec((tk, m), lambda j, k: (k, 0)),
        b_spec=pl.BlockSpec((tk, tn), lambda j, k: (k, j)),
        o_spec=pl.BlockSpec((m, tn), lambda j, k: (0, j)),
        out_shape=jax.ShapeDtypeStruct((m, n), out_dtype), acc_shape=(m, tn), name=name)


def _mm_nn_brhs(a, b, out_dtype, name, tm=512):
    m, kd = a.shape
    nb, _, n = b.shape
    return _matmul(
        a, b, dims=_NN, grid=(nb, m // tm, 1),
        a_spec=pl.BlockSpec((tm, kd), lambda j, i, k: (i, 0)),
        b_spec=pl.BlockSpec((None, kd, n), lambda j, i, k: (j, 0, 0)),
        o_spec=pl.BlockSpec((None, tm, n), lambda j, i, k: (j, i, 0)),
        out_shape=jax.ShapeDtypeStruct((nb, m, n), out_dtype), acc_shape=(tm, n), name=name)


def _mm_nt_brhs(a, b, out_dtype, name, tm=512):
    m, kd = a.shape
    nb, n, _ = b.shape
    return _matmul(
        a, b, dims=_NT, grid=(nb, m // tm, 1),
        a_spec=pl.BlockSpec((tm, kd), lambda j, i, k: (i, 0)),
        b_spec=pl.BlockSpec((None, n, kd), lambda j, i, k: (j, 0, 0)),
        o_spec=pl.BlockSpec((None, tm, n), lambda j, i, k: (j, i, 0)),
        out_shape=jax.ShapeDtypeStruct((nb, m, n), out_dtype), acc_shape=(tm, n), name=name)


def _mm_ksum(a, b, dims, out_dtype, name, tm=512, res=None, res_scale=1.0):
    nb, m, kd = a.shape
    n = b.shape[2] if dims == _NN else b.shape[1]
    return _matmul(
        a, b, dims=dims, grid=(m // tm, nb),
        a_spec=pl.BlockSpec((None, tm, kd), lambda i, k: (k, i, 0)),
        b_spec=pl.BlockSpec((None,) + b.shape[1:], lambda i, k: (k, 0, 0)),
        o_spec=pl.BlockSpec((tm, n), lambda i, k: (i, 0)),
        out_shape=jax.ShapeDtypeStruct((m, n), out_dtype), acc_shape=(tm, n), name=name,
        res=res, res_spec=pl.BlockSpec((tm, n), lambda i, k: (i, 0)), res_scale=res_scale)


def _mm_tn_batched(a, b, name, a_batched, b_batched, tk=512, out_dtype=BF16):
    nb = a.shape[0] if a_batched else b.shape[0]
    t, m = a.shape[-2:]
    n = b.shape[-1]
    if a_batched:
        a_spec = pl.BlockSpec((None, tk, m), lambda j, k: (j, k, 0))
    else:
        a_spec = pl.BlockSpec((tk, m), lambda j, k: (k, 0))
    if b_batched:
        b_spec = pl.BlockSpec((None, tk, n), lambda j, k: (j, k, 0))
    else:
        b_spec = pl.BlockSpec((tk, n), lambda j, k: (k, 0))
    return _matmul(
        a, b, dims=_TN, grid=(nb, t // tk), a_spec=a_spec, b_spec=b_spec,
        o_spec=pl.BlockSpec((None, m, n), lambda j, k: (j, 0, 0)),
        out_shape=jax.ShapeDtypeStruct((nb, m, n), out_dtype), acc_shape=(m, n), name=name)


def _row_spec(tm, n):
    return pl.BlockSpec((tm, n), lambda i: (i, 0))


def _par_spec(n, rows=1):
    return pl.BlockSpec((rows, n), lambda i: (0, 0))


def _swap_pairs(x):
    lane = lax.broadcasted_iota(jnp.int32, x.shape, 1)
    return jnp.where(lane % 2 == 0, pltpu.roll(x, LANES - 1, 1), pltpu.roll(x, 1, 1))


def _head_sums(v):
    lo = lax.broadcasted_iota(jnp.int32, v.shape, 1) < HEAD_DIM
    s_lo = jnp.sum(jnp.where(lo, v, 0.0), axis=-1, keepdims=True)
    s_hi = jnp.sum(jnp.where(lo, 0.0, v), axis=-1, keepdims=True)
    return jnp.where(lo, s_lo, s_hi)


def _qk_blocks():
    return [(128 * i, True) for i in range(4)] + [(Q_W, False)]


def _prep_fwd(h, cos_t, sin_t, qn, kn, tm=256):
    t = h.shape[0]
    scale = HEAD_DIM ** -0.5

    def body(h_ref, c_ref, s_ref, qn_ref, kn_ref, qa_ref, ka_ref, va_ref, qb_ref, kb_ref, vb_ref):
        c = c_ref[...]
        s = s_ref[...]
        for start, is_q in _qk_blocks():
            x = h_ref[:, start:start + LANES]
            r = lax.rsqrt(_head_sums(x * x) * (1.0 / HEAD_DIM) + RMS_EPS)
            y = x * r * (qn_ref[...] if is_q else kn_ref[...])
            y = y * c + _swap_pairs(y) * s
            if is_q:
                qa_ref[:, start:start + LANES] = (y * scale).astype(BF16)
            else:
                ka_ref[...] = y.astype(BF16)
        va_ref[...] = h_ref[:, 640:768].astype(BF16)
        qb_ref[...] = (h_ref[:, 768:1280] * scale).astype(BF16)
        kb_ref[...] = h_ref[:, 1280:1408].astype(BF16)
        vb_ref[...] = h_ref[:, 1408:1536].astype(BF16)

    sd = jax.ShapeDtypeStruct
    return pl.pallas_call(
        body, grid=(t // tm,),
        in_specs=[_row_spec(tm, IN_COLS), _row_spec(tm, LANES), _row_spec(tm, LANES), _par_spec(LANES), _par_spec(LANES)],
        out_specs=[_row_spec(tm, Q_W), _row_spec(tm, KV_W), _row_spec(tm, KV_W),
                   _row_spec(tm, Q_W), _row_spec(tm, KV_W), _row_spec(tm, KV_W)],
        out_shape=[sd((t, Q_W), BF16), sd((t, KV_W), BF16), sd((t, KV_W), BF16),
                   sd((t, Q_W), BF16), sd((t, KV_W), BF16), sd((t, KV_W), BF16)],
        compiler_params=_cparams(("parallel",)), name="prep_fwd",
    )(h, cos_t, sin_t, qn, kn)


def _prep_bwd(h, cos_t, sin_t, qn, kn, dqa, dka, dva, dqb, dkb, dvb, tm=256):
    t = h.shape[0]
    scale = HEAD_DIM ** -0.5

    def body(h_ref, c_ref, s_ref, qn_ref, kn_ref, dqa_ref, dka_ref, dva_ref, dqb_ref, dkb_ref, dvb_ref,
             dh_ref, dqn_ref, dkn_ref):
        @pl.when(pl.program_id(0) == 0)
        def _():
            dqn_ref[...] = jnp.zeros_like(dqn_ref)
            dkn_ref[...] = jnp.zeros_like(dkn_ref)

        c = c_ref[...]
        s = s_ref[...]
        for start, is_q in _qk_blocks():
            x = h_ref[:, start:start + LANES]
            gain = qn_ref[...] if is_q else kn_ref[...]
            d = dqa_ref[:, start:start + LANES] * scale if is_q else dka_ref[...]
            dy = d * c + _swap_pairs(d * s)
            r = lax.rsqrt(_head_sums(x * x) * (1.0 / HEAD_DIM) + RMS_EPS)
            xr = x * r
            gsum = jnp.sum(dy * xr, axis=0, keepdims=True)
            if is_q:
                dqn_ref[...] += gsum
            else:
                dkn_ref[...] += gsum
            gy = dy * gain
            dx = r * (gy - xr * (_head_sums(xr * gy) * (1.0 / HEAD_DIM)))
            dh_ref[:, start:start + LANES] = dx.astype(BF16)
        dh_ref[:, 640:768] = dva_ref[...].astype(BF16)
        dh_ref[:, 768:1280] = (dqb_ref[...] * scale).astype(BF16)
        dh_ref[:, 1280:1408] = dkb_ref[...].astype(BF16)
        dh_ref[:, 1408:1536] = dvb_ref[...].astype(BF16)

    sd = jax.ShapeDtypeStruct
    return pl.pallas_call(
        body, grid=(t // tm,),
        in_specs=[_row_spec(tm, IN_COLS), _row_spec(tm, LANES), _row_spec(tm, LANES), _par_spec(LANES), _par_spec(LANES),
                  _row_spec(tm, Q_W), _row_spec(tm, KV_W), _row_spec(tm, KV_W),
                  _row_spec(tm, Q_W), _row_spec(tm, KV_W), _row_spec(tm, KV_W)],
        out_specs=[_row_spec(tm, IN_COLS), _par_spec(LANES), _par_spec(LANES)],
        out_shape=[sd((t, IN_COLS), BF16), sd((1, LANES), F32), sd((1, LANES), F32)],
        compiler_params=_cparams(("arbitrary",)), name="prep_bwd",
    )(h, cos_t, sin_t, qn, kn, dqa, dka, dva, dqb, dkb, dvb)


def _outnorm_fwd(oa, ob, ga, gb, tm=512):
    t = oa.shape[0]

    def body(oa_ref, ob_ref, ga_ref, gb_ref, y_ref):
        for o_ref, g_ref, start in ((oa_ref, ga_ref, 0), (ob_ref, gb_ref, Q_W)):
            x = o_ref[...]
            r = lax.rsqrt(jnp.mean(x * x, axis=-1, keepdims=True) + RMS_EPS)
            y_ref[:, start:start + Q_W] = (x * r * g_ref[...]).astype(BF16)

    return pl.pallas_call(
        body, grid=(t // tm,),
        in_specs=[_row_spec(tm, Q_W), _row_spec(tm, Q_W), _par_spec(Q_W), _par_spec(Q_W)],
        out_specs=_row_spec(tm, D_MODEL), out_shape=jax.ShapeDtypeStruct((t, D_MODEL), BF16),
        compiler_params=_cparams(("parallel",)), name="outnorm_fwd",
    )(oa, ob, ga, gb)


def _outnorm_bwd(dy, oa, ob, ga, gb, tm=512):
    t = oa.shape[0]

    def body(dy_ref, oa_ref, ob_ref, ga_ref, gb_ref, doa_ref, dob_ref, dga_ref, dgb_ref):
        @pl.when(pl.program_id(0) == 0)
        def _():
            dga_ref[...] = jnp.zeros_like(dga_ref)
            dgb_ref[...] = jnp.zeros_like(dgb_ref)

        for o_ref, g_ref, do_ref, dg_ref, start in ((oa_ref, ga_ref, doa_ref, dga_ref, 0),
                                                    (ob_ref, gb_ref, dob_ref, dgb_ref, Q_W)):
            x = o_ref[...]
            d = dy_ref[:, start:start + Q_W]
            r = lax.rsqrt(jnp.mean(x * x, axis=-1, keepdims=True) + RMS_EPS)
            xr = x * r
            dg_ref[...] += jnp.sum(d * xr, axis=0, keepdims=True)
            gy = d * g_ref[...]
            do_ref[...] = r * (gy - xr * jnp.mean(xr * gy, axis=-1, keepdims=True))

    sd = jax.ShapeDtypeStruct
    return pl.pallas_call(
        body, grid=(t // tm,),
        in_specs=[_row_spec(tm, D_MODEL), _row_spec(tm, Q_W), _row_spec(tm, Q_W), _par_spec(Q_W), _par_spec(Q_W)],
        out_specs=[_row_spec(tm, Q_W), _row_spec(tm, Q_W), _par_spec(Q_W), _par_spec(Q_W)],
        out_shape=[sd((t, Q_W), F32), sd((t, Q_W), F32), sd((1, Q_W), F32), sd((1, Q_W), F32)],
        compiler_params=_cparams(("arbitrary",)), name="outnorm_bwd",
    )(dy, oa, ob, ga, gb)


def _ln_fwd(z, g, b, tm=512):
    t = z.shape[0]

    def body(z_ref, g_ref, b_ref, x_ref, xb_ref):
        zz = z_ref[...]
        mu = jnp.mean(zz, axis=-1, keepdims=True)
        zc = zz - mu
        r = lax.rsqrt(jnp.mean(zc * zc, axis=-1, keepdims=True) + LN_EPS)
        y = zc * r * g_ref[...] + b_ref[...]
        x_ref[...] = y
        xb_ref[...] = y.astype(BF16)

    sd = jax.ShapeDtypeStruct
    return pl.pallas_call(
        body, grid=(t // tm,),
        in_specs=[_row_spec(tm, D_MODEL), _par_spec(D_MODEL), _par_spec(D_MODEL)],
        out_specs=[_row_spec(tm, D_MODEL), _row_spec(tm, D_MODEL)],
        out_shape=[sd((t, D_MODEL), F32), sd((t, D_MODEL), BF16)],
        compiler_params=_cparams(("parallel",)), name="ln_fwd",
    )(z, g, b)


def _ln_bwd(d, z, g, tm=512):
    t = z.shape[0]

    def body(d_ref, z_ref, g_ref, dz_ref, dzb_ref, dg_ref, db_ref):
        @pl.when(pl.program_id(0) == 0)
        def _():
            dg_ref[...] = jnp.zeros_like(dg_ref)
            db_ref[...] = jnp.zeros_like(db_ref)

        zz = z_ref[...]
        dd = d_ref[...]
        mu = jnp.mean(zz, axis=-1, keepdims=True)
        zc = zz - mu
        r = lax.rsqrt(jnp.mean(zc * zc, axis=-1, keepdims=True) + LN_EPS)
        xh = zc * r
        dg_ref[...] += jnp.sum(dd * xh, axis=0, keepdims=True)
        db_ref[...] += jnp.sum(dd, axis=0, keepdims=True)
        dxh = dd * g_ref[...]
        dz = r * (dxh - jnp.mean(dxh, axis=-1, keepdims=True) - xh * jnp.mean(dxh * xh, axis=-1, keepdims=True))
        dz_ref[...] = dz
        dzb_ref[...] = dz.astype(BF16)

    sd = jax.ShapeDtypeStruct
    return pl.pallas_call(
        body, grid=(t // tm,),
        in_specs=[_row_spec(tm, D_MODEL), _row_spec(tm, D_MODEL), _par_spec(D_MODEL)],
        out_specs=[_row_spec(tm, D_MODEL), _row_spec(tm, D_MODEL), _par_spec(D_MODEL), _par_spec(D_MODEL)],
        out_shape=[sd((t, D_MODEL), F32), sd((t, D_MODEL), BF16), sd((1, D_MODEL), F32), sd((1, D_MODEL), F32)],
        compiler_params=_cparams(("arbitrary",)), name="ln_bwd",
    )(d, z, g)


def _loss_grad(y, tgt, tm=512):
    t = y.shape[0]
    nsteps = t // tm

    def body(y_ref, t_ref, dy_ref, loss_ref, acc):
        i = pl.program_id(0)

        @pl.when(i == 0)
        def _():
            acc[...] = jnp.zeros_like(acc)

        e = y_ref[...] - t_ref[...]
        dy_ref[...] = e * (1.0 / D_MODEL)
        acc[...] += jnp.sum(e * e, axis=0, keepdims=True)

        @pl.when(i == nsteps - 1)
        def _():
            tot = jnp.sum(acc[...], axis=-1, keepdims=True) * (0.5 / D_MODEL)
            loss_ref[...] = jnp.broadcast_to(tot, loss_ref.shape)

    sd = jax.ShapeDtypeStruct
    return pl.pallas_call(
        body, grid=(nsteps,),
        in_specs=[_row_spec(tm, D_MODEL), _row_spec(tm, D_MODEL)],
        out_specs=[_row_spec(tm, D_MODEL), _par_spec(LANES)],
        out_shape=[sd((t, D_MODEL), F32), sd((1, LANES), F32)],
        scratch_shapes=[pltpu.VMEM((1, D_MODEL), F32)],
        compiler_params=_cparams(("arbitrary",)), name="loss_grad",
    )(y, tgt)


_GELU_C = math.sqrt(2.0 / math.pi)
_GELU_K = 0.044715
HALO = 8


def _gelu_parts(x):
    th = jnp.tanh(_GELU_C * (x + _GELU_K * x * x * x))
    return 0.5 * x * (1.0 + th), th


def _halo_specs(tm, t, shift=0):
    last = t // HALO - 1
    cur = pl.BlockSpec((None, tm, FF_SH), lambda j, i: (j + shift, i, 0))
    prev = pl.BlockSpec((None, HALO, FF_SH), lambda j, i: (j + shift, jnp.maximum(i * (tm // HALO) - 1, 0), 0))
    nxt = pl.BlockSpec((None, HALO, FF_SH), lambda j, i: (j + shift, jnp.minimum((i + 1) * (tm // HALO), last), 0))
    return [prev, cur, nxt]


def _ffn_mid_fwd(g, u, cw, tm=512):
    t = g.shape[1]
    nsteps = t // tm

    def body(gp_ref, g_ref, gn_ref, u_ref, cw_ref, h_ref):
        i = pl.program_id(1)
        gg = g_ref[...]
        row = lax.broadcasted_iota(jnp.int32, gg.shape, 0)
        prev = jnp.where(i == 0, 0.0, gp_ref[HALO - 1:HALO, :])
        nxt = jnp.where(i == nsteps - 1, 0.0, gn_ref[0:1, :])
        g_m1 = jnp.where(row == 0, prev, pltpu.roll(gg, 1, 0))
        g_p1 = jnp.where(row == tm - 1, nxt, pltpu.roll(gg, tm - 1, 0))
        gc = cw_ref[3:4, :] + g_m1 * cw_ref[0:1, :] + gg * cw_ref[1:2, :] + g_p1 * cw_ref[2:3, :]
        act, _ = _gelu_parts(gc)
        h_ref[...] = (act * u_ref[...]).astype(BF16)

    return pl.pallas_call(
        body, grid=(N_SHARD, nsteps),
        in_specs=_halo_specs(tm, t) + [pl.BlockSpec((None, tm, FF_SH), lambda j, i: (j, i, 0)),
                                       pl.BlockSpec((None, 8, FF_SH), lambda j, i: (j, 0, 0))],
        out_specs=pl.BlockSpec((None, tm, FF_SH), lambda j, i: (j, i, 0)),
        out_shape=jax.ShapeDtypeStruct((N_SHARD, t, FF_SH), BF16),
        compiler_params=_cparams(("parallel", "parallel")), name="ffn_mid_fwd",
    )(g, g, g, u, cw)


def _ffn_mid_bwd(g, u, dh, cw, tm=512):
    t = g.shape[1]
    nsteps = t // tm
    te = tm + 2 * HALO

    def body(gp_ref, g_ref, gn_ref, up_ref, u_ref, un_ref, dp_ref, d_ref, dn_ref, cw_ref, dg_ref, du_ref, st_ref):
        i = pl.program_id(1)

        @pl.when(i == 0)
        def _():
            st_ref[...] = jnp.zeros_like(st_ref)

        e = lax.broadcasted_iota(jnp.int32, (te, FF_SH), 0)
        tg = i * tm - HALO + e
        valid = (tg >= 0) & (tg < t)

        def ext(p_ref, c_ref, n_ref):
            return jnp.where(valid, jnp.concatenate([p_ref[...], c_ref[...], n_ref[...]], axis=0), 0.0)

        eg = ext(gp_ref, g_ref, gn_ref)
        eu = ext(up_ref, u_ref, un_ref)
        ed = ext(dp_ref, d_ref, dn_ref)
        w0, w1, w2 = cw_ref[0:1, :], cw_ref[1:2, :], cw_ref[2:3, :]
        g_m1 = pltpu.roll(eg, 1, 0)
        g_p1 = pltpu.roll(eg, te - 1, 0)
        gc = cw_ref[3:4, :] + g_m1 * w0 + eg * w1 + g_p1 * w2
        act, th = _gelu_parts(gc)
        dact = 0.5 * (1.0 + th) + 0.5 * gc * (1.0 - th * th) * _GELU_C * (1.0 + 3.0 * _GELU_K * gc * gc)
        dgc = ed * eu * dact
        dg = pltpu.roll(dgc, te - 1, 0) * w0 + dgc * w1 + pltpu.roll(dgc, 1, 0) * w2
        mid = slice(HALO, HALO + tm)
        dg_ref[...] = dg[mid].astype(BF16)
        du_ref[...] = (ed * act)[mid].astype(BF16)
        sel = dgc[mid]
        parts = [jnp.sum(sel, axis=0, keepdims=True),
                 jnp.sum(sel * g_m1[mid], axis=0, keepdims=True),
                 jnp.sum(sel * eg[mid], axis=0, keepdims=True),
                 jnp.sum(sel * g_p1[mid], axis=0, keepdims=True)]
        r8 = lax.broadcasted_iota(jnp.int32, (8, FF_SH), 0)
        upd = jnp.zeros((8, FF_SH), F32)
        for k, p in enumerate(parts):
            upd = upd + jnp.where(r8 == k, p, 0.0)
        st_ref[...] += upd

    blk = pl.BlockSpec((None, tm, FF_SH), lambda j, i: (j, i, 0))
    sd = jax.ShapeDtypeStruct
    return pl.pallas_call(
        body, grid=(N_SHARD, nsteps),
        in_specs=_halo_specs(tm, t) + _halo_specs(tm, t) + _halo_specs(tm, t)
        + [pl.BlockSpec((None, 8, FF_SH), lambda j, i: (j, 0, 0))],
        out_specs=[blk, blk, pl.BlockSpec((None, 8, FF_SH), lambda j, i: (j, 0, 0))],
        out_shape=[sd((N_SHARD, t, FF_SH), BF16), sd((N_SHARD, t, FF_SH), BF16), sd((N_SHARD, 8, FF_SH), F32)],
        compiler_params=_cparams(("parallel", "arbitrary")), name="ffn_mid_bwd",
    )(g, g, g, u, u, u, dh, dh, dh, cw)


def _lo_mask(rows):
    return lax.broadcasted_iota(jnp.int32, (rows, LANES), 1) < HEAD_DIM


def _stack_heads(src_ref, dst_ref, tq):
    lo = _lo_mask(tq)
    for i in range(4):
        blk = src_ref[:, LANES * i:LANES * (i + 1)].astype(dst_ref.dtype)
        zero = jnp.zeros_like(blk)
        dst_ref[tq * i:tq * (i + 1), :] = jnp.where(lo, blk, zero)
        dst_ref[tq * (4 + i):tq * (5 + i), :] = jnp.where(lo, zero, blk)


def _unstack_heads(st, dst_ref, tq):
    lo = _lo_mask(tq)
    for i in range(4):
        dst_ref[:, LANES * i:LANES * (i + 1)] = jnp.where(
            lo, st[tq * i:tq * (i + 1)], st[tq * (4 + i):tq * (5 + i)]).astype(dst_ref.dtype)


def _stacked_delta(do_ref, o_ref, tq):
    lo = _lo_mask(tq)
    los, his = [], []
    for i in range(4):
        pr = do_ref[:, LANES * i:LANES * (i + 1)] * o_ref[:, LANES * i:LANES * (i + 1)]
        los.append(jnp.sum(jnp.where(lo, pr, 0.0), axis=-1, keepdims=True))
        his.append(jnp.sum(jnp.where(lo, 0.0, pr), axis=-1, keepdims=True))
    return jnp.concatenate(los + his, axis=0)


def _lane_chunks(a):
    return [a[:, LANES * c:LANES * (c + 1)] for c in range(a.shape[1] // LANES)]


def _gattn_fwd(q, k, v, gather=(), tq=128, tk=2048):
    t = q.shape[0]
    tk = min(tk, t)
    nq, nk, r = t // tq, t // tk, 8 * tq
    ng = len(gather)

    def body(*refs):
        q_ref, k_ref, v_ref = refs[:3]
        o_ref, lse_ref = refs[3 + ng:5 + ng]
        qs, m_s, l_s, acc = refs[5 + 2 * ng:9 + 2 * ng]
        if ng:
            ex = _gather_exchange(refs[3:3 + ng], refs[5 + ng:5 + 2 * ng], *refs[9 + 2 * ng:])
            pl.when(pl.program_id(0) == 0)(ex.start)
        _stack_heads(q_ref, qs, tq)
        m_s[...] = jnp.full_like(m_s, NEG)

        def max_step(j, carry):
            off = pl.multiple_of(j * tk, tk)
            s = _dot(qs[...], k_ref[pl.ds(off, tk), :], _NT)
            mp = m_s[...]
            for sc in _lane_chunks(s):
                mp = jnp.maximum(mp, sc)
            m_s[...] = mp
            return carry

        lax.fori_loop(0, nk, max_step, 0)
        m_row = jnp.max(m_s[...], axis=-1, keepdims=True)
        m_s[...] = jnp.broadcast_to(m_row, m_s.shape)
        l_s[...] = jnp.zeros_like(l_s)
        acc[...] = jnp.zeros_like(acc)

        def sum_step(j, carry):
            off = pl.multiple_of(j * tk, tk)
            s = _dot(qs[...], k_ref[pl.ds(off, tk), :], _NT)
            m_rep = m_s[...]
            ps = [jnp.exp(sc - m_rep) for sc in _lane_chunks(s)]
            lp = l_s[...]
            for pc in ps:
                lp = lp + pc
            l_s[...] = lp
            p = jnp.concatenate([pc.astype(BF16) for pc in ps], axis=1)
            acc[...] += _dot(p, v_ref[pl.ds(off, tk), :], _NN)
            return carry

        lax.fori_loop(0, nk, sum_step, 0)
        l_row = jnp.sum(l_s[...], axis=-1, keepdims=True)
        _unstack_heads(acc[...] / l_row, o_ref, tq)
        lse_ref[...] = m_row + jnp.log(l_row)
        if ng:
            pl.when(pl.program_id(0) == nq - 1)(ex.wait)

    sd = jax.ShapeDtypeStruct
    return pl.pallas_call(
        body, grid=(nq,),
        in_specs=[_row_spec(tq, Q_W), _par_spec(KV_W, t), _par_spec(KV_W, t)] + [_ANY] * ng,
        out_specs=[_row_spec(tq, Q_W), _row_spec(r, 1)] + [_ANY] * ng,
        out_shape=[sd((t, Q_W), F32), sd((nq * r, 1), F32)] + _gathered_shapes(gather),
        scratch_shapes=[pltpu.VMEM((r, LANES), BF16), pltpu.VMEM((r, LANES), F32), pltpu.VMEM((r, LANES), F32),
                        pltpu.VMEM((r, LANES), F32)] + (_exchange_sems(ng) if ng else []),
        compiler_params=_cparams(("arbitrary",) if ng else ("parallel",)),
        name="gattn_fwd_gather" if ng else "gattn_fwd",
    )(q, k, v, *gather)


def _gattn_bwd(q, k, v, o, do, lse, scatter=None, tq=128, tk=512):
    t = q.shape[0]
    tk = min(tk, t)
    nq, nk, r = t // tq, t // tk, 8 * tq
    items, sgrads = scatter if scatter else ((), ())
    ns = len(sgrads)
    slot_shapes = []
    for j, (o_idx, _) in enumerate(items):
        if o_idx == len(slot_shapes):
            slot_shapes += _slot_shapes([sgrads[j]])
    nslots = len(slot_shapes)

    def body(*refs):
        q_ref, k_ref, v_ref, o_ref, do_ref, lse_ref = refs[:6]
        dq_ref, dk_ref, dv_ref = refs[6 + ns:9 + ns]
        qs, dos, dqa = refs[9 + ns + nslots:12 + ns + nslots]
        if ns:
            ex = _scatter_exchange(items, refs[6:6 + ns], refs[9 + ns:9 + ns + nslots], *refs[12 + ns + nslots:])
            pl.when(pl.program_id(0) == 0)(ex.start)

        @pl.when(pl.program_id(0) == 0)
        def _():
            dk_ref[...] = jnp.zeros_like(dk_ref)
            dv_ref[...] = jnp.zeros_like(dv_ref)

        _stack_heads(q_ref, qs, tq)
        _stack_heads(do_ref, dos, tq)
        delta = _stacked_delta(do_ref, o_ref, tq)
        lse_v = lse_ref[...]
        dqa[...] = jnp.zeros_like(dqa)

        def step(j, carry):
            off = pl.multiple_of(j * tk, tk)
            kc = k_ref[pl.ds(off, tk), :]
            vc = v_ref[pl.ds(off, tk), :]
            p = jnp.exp(_dot(qs[...], kc, _NT) - lse_v)
            dp = _dot(dos[...], vc, _NT)
            ds = (p * (dp - delta)).astype(BF16)
            dqa[...] += _dot(ds, kc, _NN)
            dk_ref[pl.ds(off, tk), :] += _dot(ds, qs[...], _TN)
            dv_ref[pl.ds(off, tk), :] += _dot(p, dos[...], _TN)
            return carry

        lax.fori_loop(0, nk, step, 0)
        _unstack_heads(dqa[...], dq_ref, tq)
        if ns:
            pl.when(pl.program_id(0) == nq - 1)(ex.wait)

    sd = jax.ShapeDtypeStruct
    return pl.pallas_call(
        body, grid=(nq,),
        in_specs=[_row_spec(tq, Q_W), _par_spec(KV_W, t), _par_spec(KV_W, t), _row_spec(tq, Q_W), _row_spec(tq, Q_W),
                  _row_spec(r, 1)] + [_ANY] * ns,
        out_specs=[_row_spec(tq, Q_W), _par_spec(KV_W, t), _par_spec(KV_W, t)] + [_ANY] * nslots,
        out_shape=[sd((t, Q_W), F32), sd((t, KV_W), F32), sd((t, KV_W), F32)] + slot_shapes,
        scratch_shapes=[pltpu.VMEM((r, LANES), BF16), pltpu.VMEM((r, LANES), BF16), pltpu.VMEM((r, LANES), F32)]
        + (_exchange_sems(ns) if ns else []),
        compiler_params=_cparams(("arbitrary",)), name="gattn_bwd_scatter" if ns else "gattn_bwd",
    )(q, k, v, o, do, lse, *sgrads)


_WQ = Q_BLOCK
_WK = 3 * Q_BLOCK
_WR = 8 * _WQ
_WNB = 2


def _wattn_scores(qs, kw, bias_ref, n, t):
    col = lax.broadcasted_iota(jnp.int32, (1, _WK), 1)
    kabs = (n - 1) * _WQ + col
    s = _dot(qs[...], kw, _NT) + bias_ref[...]
    return jnp.where((kabs >= 0) & (kabs < t), s, NEG)


def _wattn_fwd(q, kp, vp, bias, sink):
    t = q.shape[0]
    nq = t // _WQ

    def body(q_ref, k_ref, v_ref, b_ref, sk_ref, o_ref, lse_ref, qs):
        sk = sk_ref[...]
        for b in range(_WNB):
            n = pl.program_id(0) * _WNB + b
            rows = pl.ds(b * _WQ, _WQ)
            _stack_heads(q_ref.at[rows, :], qs.at[b], _WQ)
            off = pl.multiple_of(n * _WQ, _WQ)
            kw = k_ref[pl.ds(off, _WK), :]
            vw = v_ref[pl.ds(off, _WK), :]
            s = _wattn_scores(qs.at[b], kw, b_ref, n, t)
            m = jnp.maximum(jnp.max(s, axis=-1, keepdims=True), sk)
            p = jnp.exp(s - m)
            l = jnp.sum(p, axis=-1, keepdims=True) + jnp.exp(sk - m)
            _unstack_heads(_dot(p, vw, _NN) / l, o_ref.at[rows, :], _WQ)
            lse_ref[pl.ds(b * _WR, _WR), :] = m + jnp.log(l)

    sd = jax.ShapeDtypeStruct
    return pl.pallas_call(
        body, grid=(nq // _WNB,),
        in_specs=[_row_spec(_WNB * _WQ, Q_W), _par_spec(KV_W, t + 2 * _WQ), _par_spec(KV_W, t + 2 * _WQ),
                  _par_spec(_WK, _WR), _par_spec(1, _WR)],
        out_specs=[_row_spec(_WNB * _WQ, Q_W), _row_spec(_WNB * _WR, 1)],
        out_shape=[sd((t, Q_W), F32), sd((nq * _WR, 1), F32)],
        scratch_shapes=[pltpu.VMEM((_WNB, _WR, LANES), BF16)],
        compiler_params=_cparams(("parallel",)), name="wattn_fwd",
    )(q, kp, vp, bias, sink)


def _wattn_bwd(q, kp, vp, bias, sink, o, do, lse):
    t = q.shape[0]
    nq = t // _WQ

    def body(q_ref, k_ref, v_ref, b_ref, sk_ref, o_ref, do_ref, lse_ref, dq_ref, dk_ref, dv_ref, db_ref, dsk_ref, qs, dos):
        @pl.when(pl.program_id(0) == 0)
        def _():
            dk_ref[...] = jnp.zeros_like(dk_ref)
            dv_ref[...] = jnp.zeros_like(dv_ref)
            db_ref[...] = jnp.zeros_like(db_ref)
            dsk_ref[...] = jnp.zeros_like(dsk_ref)

        dbias = jnp.zeros((_WR, _WK), F32)
        dsink = jnp.zeros((_WR, 1), F32)
        parts = []
        for b in range(_WNB):
            n = pl.program_id(0) * _WNB + b
            rows = pl.ds(b * _WQ, _WQ)
            _stack_heads(q_ref.at[rows, :], qs.at[b], _WQ)
            _stack_heads(do_ref.at[rows, :], dos.at[b], _WQ)
            delta = _stacked_delta(do_ref.at[rows, :], o_ref.at[rows, :], _WQ)
            off = pl.multiple_of(n * _WQ, _WQ)
            kw = k_ref[pl.ds(off, _WK), :]
            vw = v_ref[pl.ds(off, _WK), :]
            lse_v = lse_ref[pl.ds(b * _WR, _WR), :]
            p = jnp.exp(_wattn_scores(qs.at[b], kw, b_ref, n, t) - lse_v)
            dp = _dot(dos[b], vw, _NT)
            ds = p * (dp - delta)
            dbias = dbias + ds
            dsink = dsink - jnp.exp(sk_ref[...] - lse_v) * delta
            dsb = ds.astype(BF16)
            _unstack_heads(_dot(dsb, kw, _NN), dq_ref.at[rows, :], _WQ)
            parts.append((off, _dot(dsb, qs[b], _TN), _dot(p, dos[b], _TN)))
        db_ref[...] += dbias
        dsk_ref[...] += dsink
        for off, dkw, dvw in parts:
            dk_ref[pl.ds(off, _WK), :] += dkw
            dv_ref[pl.ds(off, _WK), :] += dvw

    sd = jax.ShapeDtypeStruct
    tp = t + 2 * _WQ
    qb = _row_spec(_WNB * _WQ, Q_W)
    return pl.pallas_call(
        body, grid=(nq // _WNB,),
        in_specs=[qb, _par_spec(KV_W, tp), _par_spec(KV_W, tp), _par_spec(_WK, _WR), _par_spec(1, _WR),
                  qb, qb, _row_spec(_WNB * _WR, 1)],
        out_specs=[qb, _par_spec(KV_W, tp), _par_spec(KV_W, tp), _par_spec(_WK, _WR), _par_spec(1, _WR)],
        out_shape=[sd((t, Q_W), F32), sd((tp, KV_W), F32), sd((tp, KV_W), F32), sd((_WR, _WK), F32), sd((_WR, 1), F32)],
        scratch_shapes=[pltpu.VMEM((_WNB, _WR, LANES), BF16), pltpu.VMEM((_WNB, _WR, LANES), BF16)],
        compiler_params=_cparams(("arbitrary",)), name="wattn_bwd",
    )(q, kp, vp, bias, sink, o, do, lse)


def _bias_bucket_reduce(db0, db1, bucket):
    def body(a_ref, b_ref, bk_ref, o_ref):
        d = a_ref[...] + b_ref[...]
        bk = bk_ref[...]
        lane = lax.broadcasted_iota(jnp.int32, (1, LANES), 1)
        out = jnp.zeros((1, LANES), F32)
        for b in range(N_BUCKETS):
            tot = jnp.sum(jnp.sum(jnp.where(bk == b, d, 0.0), axis=-1, keepdims=True), axis=0, keepdims=True)
            out = out + jnp.where(lane == b, tot, 0.0)
        o_ref[...] = out

    hb = pl.BlockSpec((None, _WQ, _WK), lambda h: (h, 0, 0))
    return pl.pallas_call(
        body, grid=(8,), in_specs=[hb, hb, pl.BlockSpec((_WQ, _WK), lambda h: (0, 0))],
        out_specs=pl.BlockSpec((None, 1, LANES), lambda h: (h, 0, 0)),
        out_shape=jax.ShapeDtypeStruct((8, 1, LANES), F32),
        compiler_params=_cparams(("parallel",)), name="bias_bucket_reduce",
    )(db0.reshape(8, _WQ, _WK), db1.reshape(8, _WQ, _WK), bucket)


def _rope_tables(t):
    rows_n = t // GRID_W
    row = jnp.repeat(jnp.arange(rows_n, dtype=F32), GRID_W)
    col = jnp.tile(jnp.arange(GRID_W, dtype=F32), rows_n)
    half = HEAD_DIM // 2
    inv_freq = ROPE_THETA ** (-jnp.arange(0, half, 2, dtype=F32) / half)
    ang = jnp.concatenate([row[:, None] * inv_freq, col[:, None] * inv_freq], axis=-1)
    cos, sin = jnp.cos(ang), jnp.sin(ang)
    c64 = jnp.repeat(cos, 2, axis=-1)
    s64 = jnp.stack([-sin, sin], axis=-1).reshape(t, HEAD_DIM)
    return jnp.tile(c64, (1, 2)), jnp.tile(s64, (1, 2))


def _t5_bucket(rel):
    half = N_BUCKETS // 2
    max_exact = half // 2
    bucket = jnp.where(rel > 0, half, 0)
    rp = jnp.abs(rel)
    rpf = jnp.maximum(rp, 1).astype(jnp.float32)
    large = max_exact + (jnp.log(rpf / max_exact) / math.log(MAX_DISTANCE / max_exact)
                         * (half - max_exact)).astype(jnp.int32)
    large = jnp.minimum(large, half - 1)
    return bucket + jnp.where(rp < max_exact, rp, large)


def _window_tables(rel_bias):
    qpos = jnp.arange(_WQ, dtype=jnp.int32)
    kpos = jnp.arange(_WK, dtype=jnp.int32) - _WQ
    rel = kpos[None, :] - qpos[:, None]
    bucket = _t5_bucket(rel)
    bias = jnp.zeros((8, _WQ, _WK), F32)
    for b in range(N_BUCKETS):
        bias = jnp.where((bucket == b)[None], rel_bias[b][:, None, None], bias)
    bias = jnp.where((jnp.abs(rel) <= WINDOW)[None], bias, NEG)
    return bias.reshape(_WR, _WK), bucket


def _pad_rows(a):
    return jnp.pad(a, ((_WQ, _WQ), (0, 0)))


def _layer_fwd(x, p, tabs, gather=None):
    cos_t, sin_t, bias = tabs
    h = _mm_nn(x, p["win"], F32, "in_proj")
    qa, ka, va, qb, kb, vb = _prep_fwd(h, cos_t, sin_t, p["qn"], p["kn"])
    if gather is None:
        oa, lse_a = _gattn_fwd(qa, ka, va)
    else:
        oa, lse_a, *gathered = _gattn_fwd(qa, ka, va, gather=gather[0])
        p = gather[1](gathered)
    kbp, vbp = _pad_rows(kb), _pad_rows(vb)
    ob, lse_b = _wattn_fwd(qb, kbp, vbp, bias, p["sink"])
    y = _outnorm_fwd(oa, ob, p["ga"], p["gb"])
    z1 = _mm_nn(y, p["wout"], F32, "out_proj", res=x, res_scale=ALPHA)
    x1, x1b = _ln_fwd(z1, p["ln1g"], p["ln1b"])
    g = _mm_nn_brhs(x1b, p["wg"], F32, "gate_proj")
    u = _mm_nn_brhs(x1b, p["wu"], F32, "up_proj")
    hdn = _ffn_mid_fwd(g, u, p["cw"])
    z2 = _mm_ksum(hdn, p["wd"], _NN, F32, "down_proj", res=x1, res_scale=ALPHA)
    x2, _ = _ln_fwd(z2, p["ln2g"], p["ln2b"])
    saved = dict(x=x, h=h, qa=qa, ka=ka, va=va, qb=qb, kbp=kbp, vbp=vbp, oa=oa, ob=ob, lse_a=lse_a, lse_b=lse_b,
                 y=y, z1=z1, x1b=x1b, g=g, u=u, hdn=hdn, z2=z2)
    return x2, saved


def _block_grads(g, names=("w_in", "w_out", "w_gate", "w_up", "w_down")):
    make = dict(
        w_in=lambda: _col_blocks(_in_cols_to_pairs(g["win"], _from_pairs), IN_SH),
        w_out=lambda: _mix_rows_to_pairs(g["wout"], _from_pairs).reshape(N_SHARD, OUT_SH, D_MODEL),
        w_gate=lambda: g["wg"], w_up=lambda: g["wu"], w_down=lambda: g["wd"])
    return [make[n]() for n in names]


def _layer_bwd(dx2, p, s, tabs, layer=0, pending=None):
    cos_t, sin_t, bias = tabs
    t = dx2.shape[0]
    dz2, dz2b, dln2g, dln2b = _ln_bwd(dx2, s["z2"], p["ln2g"])
    dhdn = _mm_nt_brhs(dz2b, p["wd"], F32, "down_dx")
    dwd = _mm_tn_batched(s["hdn"], dz2b, "down_dw", a_batched=True, b_batched=False)
    dg, du, stats = _ffn_mid_bwd(s["g"], s["u"], dhdn, p["cw"])
    dx1 = _mm_ksum(dg, p["wg"], _NT, F32, "gate_dx", res=dz2, res_scale=ALPHA)
    dx1 = _mm_ksum(du, p["wu"], _NT, F32, "up_dx", res=dx1, res_scale=1.0)
    dwg = _mm_tn_batched(s["x1b"], dg, "gate_dw", a_batched=False, b_batched=True)
    dwu = _mm_tn_batched(s["x1b"], du, "up_dw", a_batched=False, b_batched=True)
    dz1, dz1b, dln1g, dln1b = _ln_bwd(dx1, s["z1"], p["ln1g"])
    dy = _mm_nt(dz1b, p["wout"], F32, "out_dx")
    dwout = _mm_tn(s["y"], dz1b, "out_dw")
    doa, dob, dga, dgb = _outnorm_bwd(dy, s["oa"], s["ob"], p["ga"], p["gb"])
    slots = None
    if pending is None:
        dqa, dka, dva = _gattn_bwd(s["qa"], s["ka"], s["va"], s["oa"], doa, s["lse_a"])
    else:
        mine = _block_grads(dict(wout=dwout, wg=dwg, wu=dwu, wd=dwd), ("w_out", "w_gate", "w_up", "w_down"))
        todo = list(pending) + [(o + 1, layer, g) for o, g in enumerate(mine)]
        dqa, dka, dva, *slots = _gattn_bwd(s["qa"], s["ka"], s["va"], s["oa"], doa, s["lse_a"],
                                           scatter=([(o, l) for o, l, _ in todo], [g for _, _, g in todo]))
    dqb, dkbp, dvbp, dbias, dsink = _wattn_bwd(s["qb"], s["kbp"], s["vbp"], bias, p["sink"], s["ob"], dob, s["lse_b"])
    dkb = lax.slice_in_dim(dkbp, _WQ, _WQ + t, axis=0)
    dvb = lax.slice_in_dim(dvbp, _WQ, _WQ + t, axis=0)
    dh, dqn, dkn = _prep_bwd(s["h"], cos_t, sin_t, p["qn"], p["kn"], dqa, dka, dva, dqb, dkb, dvb)
    dx = _mm_nt(dh, p["win"], F32, "in_dx", res=dz1, res_scale=ALPHA)
    dwin = _mm_tn(s["x"], dh, "in_dw")
    grads = dict(win=dwin, wout=dwout, wg=dwg, wu=dwu, wd=dwd, stats=stats, qn=dqn, kn=dkn, ga=dga, gb=dgb,
                 ln1g=dln1g, ln1b=dln1b, ln2g=dln2g, ln2b=dln2b, bias=dbias, sink=dsink, slots=slots)
    return dx, grads


def _prep_layer_params(l, win, wout, wg, wu, wd, cw, q_norm, k_norm, sink, out_norm_a, out_norm_b, conv_b,
                       ln1_g, ln1_b, ln2_g, ln2_b):
    win_full = win.transpose(1, 0, 2).reshape(D_MODEL, IN_COLS)
    row = lambda v: v.reshape(1, -1)
    late = {}
    if wout is not None:
        late = dict(
            wout=_mix_rows_to_pairs(wout.reshape(D_MODEL, D_MODEL)), wg=wg, wu=wu, wd=wd,
            cw=jnp.pad(cw, ((0, 0), (0, 5), (0, 0)))
            + jnp.pad(conv_b[l].reshape(N_SHARD, 1, FF_SH), ((0, 0), (3, 4), (0, 0))))
    return dict(
        late, win=_in_cols_to_pairs(win_full),
        qn=row(jnp.tile(q_norm[l], 2)), kn=row(jnp.tile(k_norm[l], 2)),
        ga=row(_to_pairs(out_norm_a[l], 0)), gb=row(_to_pairs(out_norm_b[l], 0)),
        ln1g=row(ln1_g[l]), ln1b=row(ln1_b[l]), ln2g=row(ln2_g[l]), ln2b=row(ln2_b[l]),
        sink=jnp.repeat(sink[l], _WQ).reshape(_WR, 1))


def _local_step(x, tgt, params, rel_bias, gather=None, scatter=False):
    t = x.shape[0]
    cos_t, sin_t = _rope_tables(t)
    bias, bucket = _window_tables(rel_bias)
    tabs = (cos_t, sin_t, bias)
    saved = []
    for l in range(DEPTH):
        x, s = _layer_fwd(x, params[l], tabs, gather if l == 0 else None)
        saved.append(s)
    dx, loss = _loss_grad(x, tgt)
    grads = [None] * DEPTH
    for l in reversed(range(DEPTH)):
        pending = None
        if scatter and l == 0:
            pending = [(o, 1, g) for o, g in enumerate(_block_grads(grads[1]))]
        dx, grads[l] = _layer_bwd(dx, params[l], saved[l], tabs, l, pending)
    dbucket = _bias_bucket_reduce(grads[0]["bias"], grads[1]["bias"], bucket)
    return loss, dx, grads, dbucket


_ANY = pl.BlockSpec(memory_space=pl.ANY)
_MESH = pl.DeviceIdType.MESH


def _mesh_pos():
    return lax.axis_index("x"), lax.axis_index("y"), lax.axis_index("c")


def _other_chips(x, y):
    return [(1 - x, y), (x, 1 - y), (1 - x, 1 - y)]


class _Exchange:
    def __init__(self, local, sends, recvs):
        self.local, self.sends, self.recvs = local, sends, recvs

    def start(self):
        for cp in self.local + self.sends:
            cp.start()

    def wait(self):
        for cp in self.recvs:
            cp.wait_recv()
        for cp in self.sends:
            cp.wait_send()
        for cp in self.local:
            cp.wait()


def _exchange_sems(n):
    return [pltpu.SemaphoreType.DMA((n, 3)), pltpu.SemaphoreType.DMA((n, 3)), pltpu.SemaphoreType.DMA((n,))]


def _gather_exchange(ins, outs, send, recv, loc):
    x, y, c = _mesh_pos()
    me = 2 * x + y
    chips = _other_chips(x, y)

    def remote(i, k, block):
        px, py = chips[k]
        return pltpu.make_async_remote_copy(ins[i], outs[i].at[block], send.at[i, k], recv.at[i, k],
                                            device_id=(px, py, c), device_id_type=_MESH)

    n = len(ins)
    local = [pltpu.make_async_copy(ins[i], outs[i].at[me], loc.at[i]) for i in range(n)]
    sends = [remote(i, k, me) for i in range(n) for k in range(3)]
    recvs = [remote(i, k, 2 * chips[k][0] + chips[k][1]) for i in range(n) for k in range(3)]
    return _Exchange(local, sends, recvs)


def _scatter_exchange(items, ins, outs, send, recv, loc):
    x, y, c = _mesh_pos()
    me = 2 * x + y
    chips = _other_chips(x, y)

    def remote(j, k):
        o, l = items[j]
        px, py = chips[k]
        return pltpu.make_async_remote_copy(ins[j].at[2 * px + py], outs[o].at[k, l], send.at[j, k], recv.at[j, k],
                                            device_id=(px, py, c), device_id_type=_MESH)

    local = [pltpu.make_async_copy(ins[j].at[me], outs[o].at[3, l], loc.at[j]) for j, (o, l) in enumerate(items)]
    sends = [remote(j, k) for j in range(len(items)) for k in range(3)]
    return _Exchange(local, sends, sends)


def _gathered_shapes(shards):
    return [jax.ShapeDtypeStruct((N_SHARD,) + s.shape, s.dtype) for s in shards]


def _slot_shapes(blocks):
    return [jax.ShapeDtypeStruct((N_SHARD, DEPTH) + g.shape[1:], g.dtype) for g in blocks]


def _gather_shards(shards):
    n = len(shards)

    def body(*refs):
        ex = _gather_exchange(refs[:n], refs[n:2 * n], *refs[2 * n:])
        ex.start()
        ex.wait()

    return pl.pallas_call(
        body, in_specs=[_ANY] * n, out_specs=[_ANY] * n, out_shape=_gathered_shapes(shards),
        scratch_shapes=_exchange_sems(n), name="gather_weights",
    )(*shards)


def _scatter_into(items, grads, slots):
    n, ns = len(grads), len(slots)

    def body(*refs):
        ex = _scatter_exchange(items, refs[:n], refs[n + ns:n + 2 * ns], *refs[n + 2 * ns:])
        ex.start()
        ex.wait()

    return pl.pallas_call(
        body, in_specs=[_ANY] * (n + ns), out_specs=[_ANY] * ns,
        out_shape=[jax.ShapeDtypeStruct(s.shape, s.dtype) for s in slots],
        input_output_aliases={n + i: i for i in range(ns)},
        scratch_shapes=_exchange_sems(n), name="scatter_grads",
    )(*grads, *slots)


def _swap_with_sibling(parts):
    n = len(parts)

    def body(*refs):
        ins, outs = refs[:n], refs[n:2 * n]
        send, recv = refs[2 * n:]
        x, y, c = _mesh_pos()
        copies = [pltpu.make_async_remote_copy(ins[i], outs[i], send.at[i], recv.at[i], device_id=(x, y, 1 - c),
                                               device_id_type=_MESH) for i in range(n)]
        for cp in copies:
            cp.start()
        for cp in copies:
            cp.wait_recv()
        for cp in copies:
            cp.wait_send()

    return pl.pallas_call(
        body, in_specs=[_ANY] * n, out_specs=[_ANY] * n,
        out_shape=[jax.ShapeDtypeStruct(p.shape, p.dtype) for p in parts],
        scratch_shapes=[pltpu.SemaphoreType.DMA((n,)), pltpu.SemaphoreType.DMA((n,))],
        name="swap_sibling",
    )(*parts)


N_DEV = 8


def _allreduce_small(packed):
    rows = packed.shape[0]

    def body(in_ref, out_ref, buf, send, recv, loc):
        x, y, c = _mesh_pos()
        me = 4 * x + 2 * y + c
        own = pltpu.make_async_copy(in_ref, buf.at[me], loc)
        own.start()

        def remote(m, block):
            peer = (x ^ (m >> 2), y ^ ((m >> 1) & 1), c ^ (m & 1))
            return pltpu.make_async_remote_copy(in_ref, buf.at[block], send.at[m - 1], recv.at[m - 1],
                                                device_id=peer, device_id_type=_MESH)

        sends = [remote(m, me) for m in range(1, N_DEV)]
        for cp in sends:
            cp.start()
        for m in range(1, N_DEV):
            remote(m, me ^ m).wait_recv()
        for cp in sends:
            cp.wait_send()
        own.wait()
        tot = buf[0]
        for d in range(1, N_DEV):
            tot = tot + buf[d]
        out_ref[...] = tot

    vm = pl.BlockSpec(memory_space=pltpu.VMEM)
    return pl.pallas_call(
        body, in_specs=[vm], out_specs=vm, out_shape=jax.ShapeDtypeStruct((rows, LANES), F32),
        scratch_shapes=[pltpu.VMEM((N_DEV, rows, LANES), F32), pltpu.SemaphoreType.DMA((N_DEV - 1,)),
                        pltpu.SemaphoreType.DMA((N_DEV - 1,)), pltpu.SemaphoreType.DMA(())],
        name="allreduce_small",
    )(packed)


def _shard_rows(r):
    return r // 2 if r % 32 == 0 else r


def _sum_slots(slots):
    _, _, r, cdim = slots.shape
    tr = _shard_rows(r)

    def body(a_ref, b_ref, c_ref, d_ref, o_ref):
        up = lambda ref: ref[...].astype(F32)
        o_ref[...] = ((up(d_ref) + up(a_ref)) + up(b_ref)) + up(c_ref)

    def spec(k):
        return pl.BlockSpec((None, None, tr, cdim), lambda l, i: (k, l, i, 0))

    return pl.pallas_call(
        body, grid=(DEPTH, r // tr), in_specs=[spec(0), spec(1), spec(2), spec(3)],
        out_specs=pl.BlockSpec((None, tr, cdim), lambda l, i: (l, i, 0)),
        out_shape=jax.ShapeDtypeStruct((DEPTH, r, cdim), F32),
        compiler_params=_cparams(("parallel", "parallel")), name="sum_slots",
    )(slots, slots, slots, slots)


def _adamw_math(w, g, m, v):
    m = ADAM_B1 * m + (1.0 - ADAM_B1) * g
    v = ADAM_B2 * v + (1.0 - ADAM_B2) * (g * g)
    m_hat = m / (1.0 - ADAM_B1 ** ADAM_STEP)
    v_hat = v / (1.0 - ADAM_B2 ** ADAM_STEP)
    delta = -ADAM_LR * (m_hat / (jnp.sqrt(v_hat) + ADAM_EPS) + ADAM_WD * w)
    return delta, m, v


def _adamw_big(ga, gb, w, m, v):
    _, r, cdim = w.shape
    tr = _shard_rows(r)

    def body(ga_ref, gb_ref, w_ref, m_ref, v_ref, g_out, d_out, m_out, v_out):
        g = ga_ref[...] + gb_ref[...]
        d, mn, vn = _adamw_math(w_ref[...], g, m_ref[...], v_ref[...])
        g_out[...] = g
        d_out[...] = d
        m_out[...] = mn
        v_out[...] = vn

    spec = pl.BlockSpec((None, tr, cdim), lambda l, i: (l, i, 0))
    shp = jax.ShapeDtypeStruct(w.shape, F32)
    return pl.pallas_call(
        body, grid=(DEPTH, r // tr), in_specs=[spec] * 5, out_specs=[spec] * 4, out_shape=[shp] * 4,
        compiler_params=_cparams(("parallel", "parallel")), name="adamw_big",
    )(ga, gb, w, m, v)


def _adamw_small(ws, gs, ms, vs):
    n = len(ws)

    def body(*refs):
        w_r, g_r, m_r, v_r = (refs[k * n:(k + 1) * n] for k in range(4))
        d_o, m_o, v_o = (refs[(4 + k) * n:(5 + k) * n] for k in range(3))
        for i in range(n):
            d, mn, vn = _adamw_math(w_r[i][...], g_r[i][...], m_r[i][...], v_r[i][...])
            d_o[i][...] = d
            m_o[i][...] = mn
            v_o[i][...] = vn

    vm = pl.BlockSpec(memory_space=pltpu.VMEM)
    shp = [jax.ShapeDtypeStruct(w.shape, F32) for w in ws]
    outs = pl.pallas_call(
        body, in_specs=[vm] * (4 * n), out_specs=[vm] * (3 * n), out_shape=shp * 3, name="adamw_small",
    )(*ws, *gs, *ms, *vs)
    return outs[:n], outs[n:2 * n], outs[2 * n:]


def _tile_rows(a):
    a = a.reshape(-1, LANES)
    pad = (-a.shape[0]) % 8
    return jnp.pad(a, ((0, pad), (0, 0))) if pad else a


_SMALL_LAYER_PARTS = (("qn", 8), ("kn", 8), ("sink", 8), ("ga", 8), ("gb", 8), ("ln1g", 8), ("ln1b", 8),
                      ("ln2g", 8), ("ln2b", 8), ("stats", N_SHARD * 8 * FF_SH // LANES))
_SMALL_HEAD_ROWS = 16
_SMALL_LAYER_ROWS = sum(r for _, r in _SMALL_LAYER_PARTS)


def _pack_small(loss, dbucket, grads):
    parts = [_tile_rows(loss), _tile_rows(dbucket)]
    for l in range(DEPTH):
        parts += [_tile_rows(grads[l][name]) for name, _ in _SMALL_LAYER_PARTS]
    return jnp.concatenate(parts, axis=0)


def _unpack_small(tot, chip):
    out = dict(loss=tot[0, 0], rel_bias=tot[8:16, :N_BUCKETS].T)
    per = {name: [] for name, _ in _SMALL_LAYER_PARTS}
    for l in range(DEPTH):
        base = _SMALL_HEAD_ROWS + l * _SMALL_LAYER_ROWS
        for name, rows in _SMALL_LAYER_PARTS:
            per[name].append(tot[base:base + rows])
            base += rows
    fold = lambda v: v[0, :HEAD_DIM] + v[0, HEAD_DIM:]
    out["q_norm"] = jnp.stack([fold(v) for v in per["qn"]])
    out["k_norm"] = jnp.stack([fold(v) for v in per["kn"]])
    out["sink"] = jnp.stack([jnp.sum(v, axis=1) for v in per["sink"]])
    out["out_norm_a"] = jnp.stack([_from_pairs(v[:4].reshape(Q_W), 0) for v in per["ga"]])
    out["out_norm_b"] = jnp.stack([_from_pairs(v[:4].reshape(Q_W), 0) for v in per["gb"]])
    for name, key in (("ln1_g", "ln1g"), ("ln1_b", "ln1b"), ("ln2_g", "ln2g"), ("ln2_b", "ln2b")):
        out[name] = jnp.stack([v.reshape(D_MODEL) for v in per[key]])
    stats = [v.reshape(N_SHARD, 8, FF_SH) for v in per["stats"]]
    out["conv_b"] = jnp.stack([s[:, 0, :].reshape(D_FF) for s in stats])
    out["conv_w"] = jnp.stack([lax.dynamic_index_in_dim(s, chip, 0, keepdims=False)[1:4] for s in stats])
    return out


_WEIGHTS = ("rel_bias", "w_in", "q_norm", "k_norm", "sink", "out_norm_a", "out_norm_b", "w_out", "ln1_g", "ln1_b",
            "w_gate", "w_up", "conv_w", "conv_b", "w_down", "ln2_g", "ln2_b")
_BIG = ("w_in", "w_out", "w_gate", "w_up", "w_down")
_SMALL = tuple(n for n in _WEIGHTS if n not in _BIG)


def _col_blocks(g, n):
    return g.reshape(g.shape[0], N_SHARD, n).transpose(1, 0, 2)


def kernel(x, rel_bias, w_in, q_norm, k_norm, sink, out_norm_a, out_norm_b, w_out, ln1_g, ln1_b, w_gate, w_up, conv_w, conv_b, w_down, ln2_g, ln2_b, loss_target, m_rel_bias, m_w_in, m_q_norm, m_k_norm, m_sink, m_out_norm_a, m_out_norm_b, m_w_out, m_ln1_g, m_ln1_b, m_w_gate, m_w_up, m_conv_w, m_conv_b, m_w_down, m_ln2_g, m_ln2_b, v_rel_bias, v_w_in, v_q_norm, v_k_norm, v_sink, v_out_norm_a, v_out_norm_b, v_w_out, v_ln1_g, v_ln1_b, v_w_gate, v_w_up, v_conv_w, v_conv_b, v_w_down, v_ln2_g, v_ln2_b):
    w = dict(rel_bias=rel_bias, w_in=w_in, q_norm=q_norm, k_norm=k_norm, sink=sink, out_norm_a=out_norm_a,
             out_norm_b=out_norm_b, w_out=w_out, ln1_g=ln1_g, ln1_b=ln1_b, w_gate=w_gate, w_up=w_up, conv_w=conv_w,
             conv_b=conv_b, w_down=w_down, ln2_g=ln2_g, ln2_b=ln2_b)
    m = dict(rel_bias=m_rel_bias, w_in=m_w_in, q_norm=m_q_norm, k_norm=m_k_norm, sink=m_sink, out_norm_a=m_out_norm_a,
             out_norm_b=m_out_norm_b, w_out=m_w_out, ln1_g=m_ln1_g, ln1_b=m_ln1_b, w_gate=m_w_gate, w_up=m_w_up,
             conv_w=m_conv_w, conv_b=m_conv_b, w_down=m_w_down, ln2_g=m_ln2_g, ln2_b=m_ln2_b)
    v = dict(rel_bias=v_rel_bias, w_in=v_w_in, q_norm=v_q_norm, k_norm=v_k_norm, sink=v_sink, out_norm_a=v_out_norm_a,
             out_norm_b=v_out_norm_b, w_out=v_w_out, ln1_g=v_ln1_g, ln1_b=v_ln1_b, w_gate=v_w_gate, w_up=v_w_up,
             conv_w=v_conv_w, conv_b=v_conv_b, w_down=v_w_down, ln2_g=v_ln2_g, ln2_b=v_ln2_b)
    chip = 2 * lax.axis_index("x") + lax.axis_index("y")

    small_w = (q_norm, k_norm, sink, out_norm_a, out_norm_b, conv_b, ln1_g, ln1_b, ln2_g, ln2_b)
    (win0,) = _gather_shards([w_in[0].astype(BF16)])
    later = ([w[name][0].astype(BF16) for name in _BIG[1:]] + [w[name][1].astype(BF16) for name in _BIG] + [conv_w])
    params = [_prep_layer_params(0, win0, None, None, None, None, None, *small_w), None]

    def finish(g):
        wout0, wg0, wu0, wd0, win1, wout1, wg1, wu1, wd1, cw_all = g
        params[0] = _prep_layer_params(0, win0, wout0, wg0, wu0, wd0, cw_all[:, 0], *small_w)
        params[1] = _prep_layer_params(1, win1, wout1, wg1, wu1, wd1, cw_all[:, 1], *small_w)
        return params[0]

    loss, dx, grads, dbucket = _local_step(x[0], loss_target[0], params, rel_bias, gather=(later, finish),
                                           scatter=True)

    small = _unpack_small(_allreduce_small(_pack_small(loss, dbucket, grads)), chip)

    slots = list(grads[0]["slots"])
    slots[0] = _scatter_into([(0, 0)], _block_grads(grads[0], ("w_in",)), [slots[0]])[0]
    partial = [_sum_slots(s) for s in slots]
    other = _swap_with_sibling(partial)

    grad, delta, new_m, new_v = {}, {}, {}, {}
    for i, name in enumerate(_BIG):
        grad[name], delta[name], new_m[name], new_v[name] = _adamw_big(partial[i], other[i], w[name], m[name], v[name])
    flat2 = lambda a: a.reshape(-1, a.shape[-1])
    ds, ms, vs = _adamw_small([flat2(w[n]) for n in _SMALL], [flat2(small[n]) for n in _SMALL],
                              [flat2(m[n]) for n in _SMALL], [flat2(v[n]) for n in _SMALL])
    for i, name in enumerate(_SMALL):
        grad[name] = small[name]
        delta[name] = ds[i].reshape(w[name].shape)
        new_m[name] = ms[i].reshape(w[name].shape)
        new_v[name] = vs[i].reshape(w[name].shape)

    return (small["loss"], dx[None], *[grad[n] for n in _WEIGHTS], *[delta[n] for n in _WEIGHTS],
            *[new_m[n] for n in _WEIGHTS], *[new_v[n] for n in _WEIGHTS])
```

```python
import functools
import math

import numpy as np
import jax
import jax.numpy as jnp
from jax import lax
from jax.experimental import pallas as pl
from jax.experimental.pallas import tpu as pltpu

F32 = jnp.float32
BF16 = jnp.bfloat16

D_MODEL = 1024
DEPTH = 2
HEAD_DIM = 64
Q_W = 512
KV_W = 128
IN_COLS = 2 * (Q_W + 2 * KV_W)
N_SHARD = 4
IN_SH = IN_COLS // N_SHARD
OUT_SH = D_MODEL // N_SHARD
D_FF = 2816
FF_SH = D_FF // N_SHARD
Q_BLOCK = 128
WINDOW = 128
N_BUCKETS = 32
MAX_DISTANCE = 128
GRID_W = 64
ROPE_THETA = 10000.0
ALPHA = (2.0 * DEPTH) ** 0.25
RMS_EPS = 1e-6
LN_EPS = 1e-5
NEG = -1e30
LANES = 128
VMEM_LIMIT = 56 * 1024 * 1024

ADAM_LR = 0.001
ADAM_B1 = 0.9
ADAM_B2 = 0.999
ADAM_EPS = 1e-08
ADAM_WD = 0.01
ADAM_STEP = 10

_NN = (((1,), (0,)), ((), ()))
_NT = (((1,), (1,)), ((), ()))
_TN = (((0,), (0,)), ((), ()))


def _dot(a, b, dims):
    return lax.dot_general(a.astype(BF16), b.astype(BF16), dims, preferred_element_type=F32)


def _cparams(sem, vmem=VMEM_LIMIT):
    return pltpu.CompilerParams(dimension_semantics=sem, vmem_limit_bytes=vmem)


def _regroup(a, axis, n_outer, n_inner):
    shp = a.shape
    a = a.reshape(shp[:axis] + (n_outer, n_inner, HEAD_DIM) + shp[axis + 1:])
    return jnp.swapaxes(a, axis, axis + 1).reshape(shp)


def _to_pairs(a, axis):
    return _regroup(a, axis, 2, 4)


def _from_pairs(a, axis):
    return _regroup(a, axis, 4, 2)


def _in_cols_to_pairs(w, fn=_to_pairs):
    return jnp.concatenate([fn(w[..., :Q_W], w.ndim - 1), w[..., Q_W:Q_W + 2 * KV_W],
                            fn(w[..., Q_W + 2 * KV_W:2 * Q_W + 2 * KV_W], w.ndim - 1),
                            w[..., 2 * Q_W + 2 * KV_W:]], axis=-1)


def _mix_rows_to_pairs(w, fn=_to_pairs):
    return fn(w.reshape(2, Q_W, w.shape[-1]), 1).reshape(w.shape)


def _matmul(a, b, *, dims, grid, a_spec, b_spec, o_spec, out_shape, acc_shape, name, res=None,
            res_spec=None, res_scale=1.0):
    nk = grid[-1]
    kax = len(grid) - 1

    def body(*refs):
        if res is None:
            a_ref, b_ref, o_ref, acc = refs
            r_ref = None
        else:
            a_ref, b_ref, r_ref, o_ref, acc = refs
        k = pl.program_id(kax)

        @pl.when(k == 0)
        def _():
            acc[...] = jnp.zeros_like(acc)

        acc[...] += _dot(a_ref[...], b_ref[...], dims)

        @pl.when(k == nk - 1)
        def _():
            o = acc[...]
            if r_ref is not None:
                o = o + res_scale * r_ref[...]
            o_ref[...] = o.astype(o_ref.dtype)

    in_specs = [a_spec, b_spec] + ([res_spec] if res is not None else [])
    args = (a, b) + ((res,) if res is not None else ())
    sem = ("parallel",) * kax + ("arbitrary",)
    return pl.pallas_call(
        body, grid=grid, in_specs=in_specs, out_specs=o_spec, out_shape=out_shape,
        scratch_shapes=[pltpu.VMEM(acc_shape, F32)], compiler_params=_cparams(sem), name=name,
    )(*args)


def _mm_nn(a, b, out_dtype, name, tm=512, res=None, res_scale=1.0):
    m, kd = a.shape
    n = b.shape[1]
    return _matmul(
        a, b, dims=_NN, grid=(m // tm, 1),
        a_spec=pl.BlockSpec((tm, kd), lambda i, k: (i, 0)),
        b_spec=pl.BlockSpec((kd, n), lambda i, k: (0, 0)),
        o_spec=pl.BlockSpec((tm, n), lambda i, k: (i, 0)),
        out_shape=jax.ShapeDtypeStruct((m, n), out_dtype), acc_shape=(tm, n), name=name,
        res=res, res_spec=pl.BlockSpec((tm, n), lambda i, k: (i, 0)), res_scale=res_scale)


def _mm_nt(a, b, out_dtype, name, tm=512, res=None, res_scale=1.0):
    m, kd = a.shape
    n = b.shape[0]
    return _matmul(
        a, b, dims=_NT, grid=(m // tm, 1),
        a_spec=pl.BlockSpec((tm, kd), lambda i, k: (i, 0)),
        b_spec=pl.BlockSpec((n, kd), lambda i, k: (0, 0)),
        o_spec=pl.BlockSpec((tm, n), lambda i, k: (i, 0)),
        out_shape=jax.ShapeDtypeStruct((m, n), out_dtype), acc_shape=(tm, n), name=name,
        res=res, res_spec=pl.BlockSpec((tm, n), lambda i, k: (i, 0)), res_scale=res_scale)


def _mm_tn(a, b, name, tk=512, tn=None, out_dtype=BF16):
    t, m = a.shape
    n = b.shape[1]
    tn = n if tn is None else tn
    return _matmul(
        a, b, dims=_TN, grid=(n // tn, t // tk),
        a_spec=pl.BlockSpec((tk, m), lambda j, k: (k, 0)),
        b_spec=pl.BlockSpec((tk, tn), lambda j, k: (k, j)),
        o_spec=pl.BlockSpec((m, tn), lambda j, k: (0, j)),
        out_shape=jax.ShapeDtypeStruct((m, n), out_dtype), acc_shape=(m, tn), name=name)


def _blocked_n(w, dims):
    return w.shape[2] if dims == _NN else w.shape[1]


def _mm_expand(a, w, dims, out_dtype, name, tm=512):
    m, kd = a.shape
    nb, n = w.shape[0], _blocked_n(w, dims)

    def body(a_ref, w_ref, o_ref):
        av = a_ref[...]
        for j in range(nb):
            o_ref[j] = _dot(av, w_ref[j], dims).astype(o_ref.dtype)

    return pl.pallas_call(
        body, grid=(m // tm,),
        in_specs=[pl.BlockSpec((tm, kd), lambda i: (i, 0)), pl.BlockSpec(w.shape, lambda i: (0, 0, 0))],
        out_specs=pl.BlockSpec((nb, tm, n), lambda i: (0, i, 0)),
        out_shape=jax.ShapeDtypeStruct((nb, m, n), out_dtype),
        compiler_params=_cparams(("parallel",)), name=name,
    )(a, w)


def _mm_reduce(a, w, dims, out_dtype, name, tm=512, res=None, res_scale=1.0, ln=None):
    nb, m, kd = a.shape
    n = _blocked_n(w, dims)
    n_in = 2 + (res is not None) + (2 if ln else 0)

    def body(*refs):
        a_ref, w_ref = refs[0], refs[1]
        acc = _dot(a_ref[0], w_ref[0], dims)
        for j in range(1, nb):
            acc = acc + _dot(a_ref[j], w_ref[j], dims)
        if res is not None:
            acc = acc + res_scale * refs[2][...]
        refs[n_in][...] = acc.astype(out_dtype)
        if ln:
            g_ref, b_ref = refs[n_in - 2], refs[n_in - 1]
            zc = acc - jnp.mean(acc, axis=-1, keepdims=True)
            r = lax.rsqrt(jnp.mean(zc * zc, axis=-1, keepdims=True) + LN_EPS)
            y = zc * r * g_ref[...] + b_ref[...]
            refs[n_in + 1][...] = y
            refs[n_in + 2][...] = y.astype(BF16)

    row = pl.BlockSpec((tm, n), lambda i: (i, 0))
    par = pl.BlockSpec((1, n), lambda i: (0, 0))
    sd = jax.ShapeDtypeStruct
    out = pl.pallas_call(
        body, grid=(m // tm,),
        in_specs=[pl.BlockSpec((nb, tm, kd), lambda i: (0, i, 0)), pl.BlockSpec(w.shape, lambda i: (0, 0, 0))]
        + ([row] if res is not None else []) + ([par, par] if ln else []),
        out_specs=[row] * (3 if ln else 1),
        out_shape=[sd((m, n), out_dtype)] + ([sd((m, n), F32), sd((m, n), BF16)] if ln else []),
        compiler_params=_cparams(("parallel",)), name=name,
    )(a, w, *((res,) if res is not None else ()), *(ln or ()))
    return out if ln else out[0]


def _mm_tn_blocks(a, b, name, blk=0, nb=N_SHARD, tk=512, out_dtype=BF16):
    a3, b3 = a.ndim == 3, b.ndim == 3
    t, m, n = a.shape[-2], a.shape[-1], b.shape[-1]
    nsteps = t // tk

    def spec(blocked, width):
        if blocked:
            return pl.BlockSpec((nb, tk, width), lambda k: (blk, k, 0))
        return pl.BlockSpec((tk, width), lambda k: (k, 0))

    def body(a_ref, b_ref, o_ref, acc):
        k = pl.program_id(0)

        @pl.when(k == 0)
        def _():
            acc[...] = jnp.zeros_like(acc)

        for j in range(nb):
            acc[j] += _dot(a_ref[j] if a3 else a_ref[...], b_ref[j] if b3 else b_ref[...], _TN)

        @pl.when(k == nsteps - 1)
        def _():
            o_ref[...] = acc[...].astype(o_ref.dtype)

    return pl.pallas_call(
        body, grid=(nsteps,), in_specs=[spec(a3, m), spec(b3, n)],
        out_specs=pl.BlockSpec((nb, m, n), lambda k: (0, 0, 0)),
        out_shape=jax.ShapeDtypeStruct((nb, m, n), out_dtype),
        scratch_shapes=[pltpu.VMEM((nb, m, n), F32)],
        compiler_params=_cparams(("arbitrary",)), name=name,
    )(a, b)


def _row_spec(tm, n):
    return pl.BlockSpec((tm, n), lambda i: (i, 0))


def _par_spec(n, rows=1):
    return pl.BlockSpec((rows, n), lambda i: (0, 0))


def _swap_pairs(x):
    lane = lax.broadcasted_iota(jnp.int32, x.shape, 1)
    return jnp.where(lane % 2 == 0, pltpu.roll(x, LANES - 1, 1), pltpu.roll(x, 1, 1))


def _head_sums(v):
    lo = lax.broadcasted_iota(jnp.int32, v.shape, 1) < HEAD_DIM
    s_lo = jnp.sum(jnp.where(lo, v, 0.0), axis=-1, keepdims=True)
    s_hi = jnp.sum(jnp.where(lo, 0.0, v), axis=-1, keepdims=True)
    return jnp.where(lo, s_lo, s_hi)


def _qk_blocks():
    return [(128 * i, True) for i in range(4)] + [(Q_W, False)]


def _prep_fwd(h, cos_t, sin_t, qn, kn, tm=256):
    t = h.shape[0]
    scale = HEAD_DIM ** -0.5

    def body(h_ref, c_ref, s_ref, qn_ref, kn_ref, qa_ref, ka_ref, va_ref, qb_ref, kb_ref, vb_ref):
        c = c_ref[...]
        s = s_ref[...]
        for start, is_q in _qk_blocks():
            x = h_ref[:, start:start + LANES]
            r = lax.rsqrt(_head_sums(x * x) * (1.0 / HEAD_DIM) + RMS_EPS)
            y = x * r * (qn_ref[...] if is_q else kn_ref[...])
            y = y * c + _swap_pairs(y) * s
            if is_q:
                qa_ref[:, start:start + LANES] = (y * scale).astype(BF16)
            else:
                ka_ref[...] = y.astype(BF16)
        va_ref[...] = h_ref[:, 640:768].astype(BF16)
        qb_ref[...] = (h_ref[:, 768:1280] * scale).astype(BF16)
        kb_ref[...] = h_ref[:, 1280:1408].astype(BF16)
        vb_ref[...] = h_ref[:, 1408:1536].astype(BF16)

    sd = jax.ShapeDtypeStruct
    return pl.pallas_call(
        body, grid=(t // tm,),
        in_specs=[_row_spec(tm, IN_COLS), _row_spec(tm, LANES), _row_spec(tm, LANES), _par_spec(LANES), _par_spec(LANES)],
        out_specs=[_row_spec(tm, Q_W), _row_spec(tm, KV_W), _row_spec(tm, KV_W),
                   _row_spec(tm, Q_W), _row_spec(tm, KV_W), _row_spec(tm, KV_W)],
        out_shape=[sd((t, Q_W), BF16), sd((t, KV_W), BF16), sd((t, KV_W), BF16),
                   sd((t, Q_W), BF16), sd((t, KV_W), BF16), sd((t, KV_W), BF16)],
        compiler_params=_cparams(("parallel",)), name="prep_fwd",
    )(h, cos_t, sin_t, qn, kn)


def _prep_bwd(h, cos_t, sin_t, qn, kn, dqa, dka, dva, dqb, dkb, dvb, tm=256):
    t = h.shape[0]
    scale = HEAD_DIM ** -0.5

    def body(h_ref, c_ref, s_ref, qn_ref, kn_ref, dqa_ref, dka_ref, dva_ref, dqb_ref, dkb_ref, dvb_ref,
             dh_ref, dqn_ref, dkn_ref):
        @pl.when(pl.program_id(0) == 0)
        def _():
            dqn_ref[...] = jnp.zeros_like(dqn_ref)
            dkn_ref[...] = jnp.zeros_like(dkn_ref)

        c = c_ref[...]
        s = s_ref[...]
        for start, is_q in _qk_blocks():
            x = h_ref[:, start:start + LANES]
            gain = qn_ref[...] if is_q else kn_ref[...]
            d = dqa_ref[:, start:start + LANES] * scale if is_q else dka_ref[...]
            dy = d * c + _swap_pairs(d * s)
            r = lax.rsqrt(_head_sums(x * x) * (1.0 / HEAD_DIM) + RMS_EPS)
            xr = x * r
            gsum = jnp.sum(dy * xr, axis=0, keepdims=True)
            if is_q:
                dqn_ref[...] += gsum
            else:
                dkn_ref[...] += gsum
            gy = dy * gain
            dx = r * (gy - xr * (_head_sums(xr * gy) * (1.0 / HEAD_DIM)))
            dh_ref[:, start:start + LANES] = dx.astype(BF16)
        dh_ref[:, 640:768] = dva_ref[...].astype(BF16)
        dh_ref[:, 768:1280] = (dqb_ref[...] * scale).astype(BF16)
        dh_ref[:, 1280:1408] = dkb_ref[...].astype(BF16)
        dh_ref[:, 1408:1536] = dvb_ref[...].astype(BF16)

    sd = jax.ShapeDtypeStruct
    return pl.pallas_call(
        body, grid=(t // tm,),
        in_specs=[_row_spec(tm, IN_COLS), _row_spec(tm, LANES), _row_spec(tm, LANES), _par_spec(LANES), _par_spec(LANES),
                  _row_spec(tm, Q_W), _row_spec(tm, KV_W), _row_spec(tm, KV_W),
                  _row_spec(tm, Q_W), _row_spec(tm, KV_W), _row_spec(tm, KV_W)],
        out_specs=[_row_spec(tm, IN_COLS), _par_spec(LANES), _par_spec(LANES)],
        out_shape=[sd((t, IN_COLS), BF16), sd((1, LANES), F32), sd((1, LANES), F32)],
        compiler_params=_cparams(("arbitrary",)), name="prep_bwd",
    )(h, cos_t, sin_t, qn, kn, dqa, dka, dva, dqb, dkb, dvb)


def _outnorm_fwd(oa, ob, ga, gb, tm=512):
    t = oa.shape[0]

    def body(oa_ref, ob_ref, ga_ref, gb_ref, y_ref):
        for o_ref, g_ref, start in ((oa_ref, ga_ref, 0), (ob_ref, gb_ref, Q_W)):
            x = o_ref[...]
            r = lax.rsqrt(jnp.mean(x * x, axis=-1, keepdims=True) + RMS_EPS)
            y_ref[:, start:start + Q_W] = (x * r * g_ref[...]).astype(BF16)

    return pl.pallas_call(
        body, grid=(t // tm,),
        in_specs=[_row_spec(tm, Q_W), _row_spec(tm, Q_W), _par_spec(Q_W), _par_spec(Q_W)],
        out_specs=_row_spec(tm, D_MODEL), out_shape=jax.ShapeDtypeStruct((t, D_MODEL), BF16),
        compiler_params=_cparams(("parallel",)), name="outnorm_fwd",
    )(oa, ob, ga, gb)


def _outnorm_bwd(dy, oa, ob, ga, gb, tm=512):
    t = oa.shape[0]

    def body(dy_ref, oa_ref, ob_ref, ga_ref, gb_ref, doa_ref, dob_ref, dga_ref, dgb_ref):
        @pl.when(pl.program_id(0) == 0)
        def _():
            dga_ref[...] = jnp.zeros_like(dga_ref)
            dgb_ref[...] = jnp.zeros_like(dgb_ref)

        for o_ref, g_ref, do_ref, dg_ref, start in ((oa_ref, ga_ref, doa_ref, dga_ref, 0),
                                                    (ob_ref, gb_ref, dob_ref, dgb_ref, Q_W)):
            x = o_ref[...]
            d = dy_ref[:, start:start + Q_W]
            r = lax.rsqrt(jnp.mean(x * x, axis=-1, keepdims=True) + RMS_EPS)
            xr = x * r
            dg_ref[...] += jnp.sum(d * xr, axis=0, keepdims=True)
            gy = d * g_ref[...]
            do_ref[...] = r * (gy - xr * jnp.mean(xr * gy, axis=-1, keepdims=True))

    sd = jax.ShapeDtypeStruct
    return pl.pallas_call(
        body, grid=(t // tm,),
        in_specs=[_row_spec(tm, D_MODEL), _row_spec(tm, Q_W), _row_spec(tm, Q_W), _par_spec(Q_W), _par_spec(Q_W)],
        out_specs=[_row_spec(tm, Q_W), _row_spec(tm, Q_W), _par_spec(Q_W), _par_spec(Q_W)],
        out_shape=[sd((t, Q_W), F32), sd((t, Q_W), F32), sd((1, Q_W), F32), sd((1, Q_W), F32)],
        compiler_params=_cparams(("arbitrary",)), name="outnorm_bwd",
    )(dy, oa, ob, ga, gb)


def _ln_bwd(d, z, g, tm=512):
    t = z.shape[0]

    def body(d_ref, z_ref, g_ref, dz_ref, dzb_ref, dg_ref, db_ref):
        @pl.when(pl.program_id(0) == 0)
        def _():
            dg_ref[...] = jnp.zeros_like(dg_ref)
            db_ref[...] = jnp.zeros_like(db_ref)

        zz = z_ref[...]
        dd = d_ref[...]
        mu = jnp.mean(zz, axis=-1, keepdims=True)
        zc = zz - mu
        r = lax.rsqrt(jnp.mean(zc * zc, axis=-1, keepdims=True) + LN_EPS)
        xh = zc * r
        dg_ref[...] += jnp.sum(dd * xh, axis=0, keepdims=True)
        db_ref[...] += jnp.sum(dd, axis=0, keepdims=True)
        dxh = dd * g_ref[...]
        dz = r * (dxh - jnp.mean(dxh, axis=-1, keepdims=True) - xh * jnp.mean(dxh * xh, axis=-1, keepdims=True))
        dz_ref[...] = dz
        dzb_ref[...] = dz.astype(BF16)

    sd = jax.ShapeDtypeStruct
    return pl.pallas_call(
        body, grid=(t // tm,),
        in_specs=[_row_spec(tm, D_MODEL), _row_spec(tm, D_MODEL), _par_spec(D_MODEL)],
        out_specs=[_row_spec(tm, D_MODEL), _row_spec(tm, D_MODEL), _par_spec(D_MODEL), _par_spec(D_MODEL)],
        out_shape=[sd((t, D_MODEL), F32), sd((t, D_MODEL), BF16), sd((1, D_MODEL), F32), sd((1, D_MODEL), F32)],
        compiler_params=_cparams(("arbitrary",)), name="ln_bwd",
    )(d, z, g)


def _loss_grad(y, tgt, tm=512):
    t = y.shape[0]
    nsteps = t // tm

    def body(y_ref, t_ref, dy_ref, loss_ref, acc):
        i = pl.program_id(0)

        @pl.when(i == 0)
        def _():
            acc[...] = jnp.zeros_like(acc)

        e = y_ref[...] - t_ref[...]
        dy_ref[...] = e * (1.0 / D_MODEL)
        acc[...] += jnp.sum(e * e, axis=0, keepdims=True)

        @pl.when(i == nsteps - 1)
        def _():
            tot = jnp.sum(acc[...], axis=-1, keepdims=True) * (0.5 / D_MODEL)
            loss_ref[...] = jnp.broadcast_to(tot, loss_ref.shape)

    sd = jax.ShapeDtypeStruct
    return pl.pallas_call(
        body, grid=(nsteps,),
        in_specs=[_row_spec(tm, D_MODEL), _row_spec(tm, D_MODEL)],
        out_specs=[_row_spec(tm, D_MODEL), _par_spec(LANES)],
        out_shape=[sd((t, D_MODEL), F32), sd((1, LANES), F32)],
        scratch_shapes=[pltpu.VMEM((1, D_MODEL), F32)],
        compiler_params=_cparams(("arbitrary",)), name="loss_grad",
    )(y, tgt)


_GELU_C = math.sqrt(2.0 / math.pi)
_GELU_K = 0.044715
HALO = 16


def _gelu_parts(x):
    th = jnp.tanh(_GELU_C * (x + _GELU_K * x * x * x))
    return 0.5 * x * (1.0 + th), th


def _halo_specs(tm, t, shift=0):
    last = t // HALO - 1
    cur = pl.BlockSpec((None, tm, FF_SH), lambda j, i: (j + shift, i, 0))
    prev = pl.BlockSpec((None, HALO, FF_SH), lambda j, i: (j + shift, jnp.maximum(i * (tm // HALO) - 1, 0), 0))
    nxt = pl.BlockSpec((None, HALO, FF_SH), lambda j, i: (j + shift, jnp.minimum((i + 1) * (tm // HALO), last), 0))
    return [prev, cur, nxt]


def _ffn_mid_fwd(gu, cw, tm=512):
    t = gu.shape[1]
    nsteps = t // tm

    def body(gp_ref, g_ref, gn_ref, u_ref, cw_ref, h_ref):
        i = pl.program_id(1)
        gg = g_ref[...].astype(F32)
        row = lax.broadcasted_iota(jnp.int32, gg.shape, 0)
        prev = jnp.where(i == 0, 0.0, gp_ref[...].astype(F32)[HALO - 1:HALO, :])
        nxt = jnp.where(i == nsteps - 1, 0.0, gn_ref[...].astype(F32)[0:1, :])
        g_m1 = jnp.where(row == 0, prev, pltpu.roll(gg, 1, 0))
        g_p1 = jnp.where(row == tm - 1, nxt, pltpu.roll(gg, tm - 1, 0))
        gc = cw_ref[3:4, :] + g_m1 * cw_ref[0:1, :] + gg * cw_ref[1:2, :] + g_p1 * cw_ref[2:3, :]
        act, _ = _gelu_parts(gc)
        h_ref[...] = (act * u_ref[...].astype(F32)).astype(BF16)

    return pl.pallas_call(
        body, grid=(N_SHARD, nsteps),
        in_specs=_halo_specs(tm, t) + [pl.BlockSpec((None, tm, FF_SH), lambda j, i: (j + N_SHARD, i, 0)),
                                       pl.BlockSpec((None, 8, FF_SH), lambda j, i: (j, 0, 0))],
        out_specs=pl.BlockSpec((None, tm, FF_SH), lambda j, i: (j, i, 0)),
        out_shape=jax.ShapeDtypeStruct((N_SHARD, t, FF_SH), BF16),
        compiler_params=_cparams(("parallel", "parallel")), name="ffn_mid_fwd",
    )(gu, gu, gu, gu, cw)


def _ffn_mid_bwd(gu, dh, cw, tm=512):
    t = gu.shape[1]
    nsteps = t // tm
    te = tm + 2 * HALO

    def body(gp_ref, g_ref, gn_ref, up_ref, u_ref, un_ref, dp_ref, d_ref, dn_ref, cw_ref, dgu_ref, st_ref):
        i = pl.program_id(1)

        @pl.when(i == 0)
        def _():
            st_ref[...] = jnp.zeros_like(st_ref)

        e = lax.broadcasted_iota(jnp.int32, (te, FF_SH), 0)
        tg = i * tm - HALO + e
        valid = (tg >= 0) & (tg < t)

        def ext(p_ref, c_ref, n_ref):
            whole = jnp.concatenate([p_ref[...], c_ref[...], n_ref[...]], axis=0).astype(F32)
            return jnp.where(valid, whole, 0.0)

        eg = ext(gp_ref, g_ref, gn_ref)
        eu = ext(up_ref, u_ref, un_ref)
        ed = ext(dp_ref, d_ref, dn_ref)
        w0, w1, w2 = cw_ref[0:1, :], cw_ref[1:2, :], cw_ref[2:3, :]
        g_m1 = pltpu.roll(eg, 1, 0)
        g_p1 = pltpu.roll(eg, te - 1, 0)
        gc = cw_ref[3:4, :] + g_m1 * w0 + eg * w1 + g_p1 * w2
        act, th = _gelu_parts(gc)
        dact = 0.5 * (1.0 + th) + 0.5 * gc * (1.0 - th * th) * _GELU_C * (1.0 + 3.0 * _GELU_K * gc * gc)
        dgc = ed * eu * dact
        dg = pltpu.roll(dgc, te - 1, 0) * w0 + dgc * w1 + pltpu.roll(dgc, 1, 0) * w2
        mid = slice(HALO, HALO + tm)
        dgu_ref[0] = dg[mid].astype(BF16)
        dgu_ref[1] = (ed * act)[mid].astype(BF16)
        sel = dgc[mid]
        parts = [jnp.sum(sel, axis=0, keepdims=True),
                 jnp.sum(sel * g_m1[mid], axis=0, keepdims=True),
                 jnp.sum(sel * eg[mid], axis=0, keepdims=True),
                 jnp.sum(sel * g_p1[mid], axis=0, keepdims=True)]
        r8 = lax.broadcasted_iota(jnp.int32, (8, FF_SH), 0)
        upd = jnp.zeros((8, FF_SH), F32)
        for k, p in enumerate(parts):
            upd = upd + jnp.where(r8 == k, p, 0.0)
        st_ref[...] += upd

    sd = jax.ShapeDtypeStruct
    return pl.pallas_call(
        body, grid=(N_SHARD, nsteps),
        in_specs=_halo_specs(tm, t) + _halo_specs(tm, t, N_SHARD) + _halo_specs(tm, t)
        + [pl.BlockSpec((None, 8, FF_SH), lambda j, i: (j, 0, 0))],
        out_specs=[pl.BlockSpec((2, None, tm, FF_SH), lambda j, i: (0, j, i, 0)),
                   pl.BlockSpec((None, 8, FF_SH), lambda j, i: (j, 0, 0))],
        out_shape=[sd((2, N_SHARD, t, FF_SH), BF16), sd((N_SHARD, 8, FF_SH), F32)],
        compiler_params=_cparams(("parallel", "arbitrary")), name="ffn_mid_bwd",
    )(gu, gu, gu, gu, gu, gu, dh, dh, dh, cw)


def _lo_mask(rows):
    return lax.broadcasted_iota(jnp.int32, (rows, LANES), 1) < HEAD_DIM


def _stack_heads(src_ref, dst_ref, tq):
    lo = _lo_mask(tq)
    for i in range(4):
        blk = src_ref[:, LANES * i:LANES * (i + 1)].astype(dst_ref.dtype)
        zero = jnp.zeros_like(blk)
        dst_ref[tq * i:tq * (i + 1), :] = jnp.where(lo, blk, zero)
        dst_ref[tq * (4 + i):tq * (5 + i), :] = jnp.where(lo, zero, blk)


def _unstack_heads(st, dst_ref, tq):
    lo = _lo_mask(tq)
    for i in range(4):
        dst_ref[:, LANES * i:LANES * (i + 1)] = jnp.where(
            lo, st[tq * i:tq * (i + 1)], st[tq * (4 + i):tq * (5 + i)]).astype(dst_ref.dtype)


def _stacked_delta(do_ref, o_ref, tq):
    lo = _lo_mask(tq)
    los, his = [], []
    for i in range(4):
        pr = do_ref[:, LANES * i:LANES * (i + 1)] * o_ref[:, LANES * i:LANES * (i + 1)]
        los.append(jnp.sum(jnp.where(lo, pr, 0.0), axis=-1, keepdims=True))
        his.append(jnp.sum(jnp.where(lo, 0.0, pr), axis=-1, keepdims=True))
    return jnp.concatenate(los + his, axis=0)


def _lane_chunks(a):
    return [a[:, LANES * c:LANES * (c + 1)] for c in range(a.shape[1] // LANES)]


def _gattn_fwd(q, k, v, gather=(), tq=128, tk=2048):
    t = q.shape[0]
    tk = min(tk, t)
    nq, nk, r = t // tq, t // tk, 8 * tq
    ng = len(gather)

    def body(*refs):
        q_ref, k_ref, v_ref = refs[:3]
        o_ref, lse_ref = refs[3 + ng:5 + ng]
        qs, m_s, l_s, acc = refs[5 + 2 * ng:9 + 2 * ng]
        if ng:
            ex = _gather_exchange(refs[3:3 + ng], refs[5 + ng:5 + 2 * ng], *refs[9 + 2 * ng:])
            pl.when(pl.program_id(0) == 0)(ex.start)
        _stack_heads(q_ref, qs, tq)
        m_s[...] = jnp.full_like(m_s, NEG)

        def max_step(j, carry):
            off = pl.multiple_of(j * tk, tk)
            s = _dot(qs[...], k_ref[pl.ds(off, tk), :], _NT)
            mp = m_s[...]
            for sc in _lane_chunks(s):
                mp = jnp.maximum(mp, sc)
            m_s[...] = mp
            return carry

        lax.fori_loop(0, nk, max_step, 0)
        m_row = jnp.max(m_s[...], axis=-1, keepdims=True)
        m_s[...] = jnp.broadcast_to(m_row, m_s.shape)
        l_s[...] = jnp.zeros_like(l_s)
        acc[...] = jnp.zeros_like(acc)

        def sum_step(j, carry):
            off = pl.multiple_of(j * tk, tk)
            s = _dot(qs[...], k_ref[pl.ds(off, tk), :], _NT)
            m_rep = m_s[...]
            ps = [jnp.exp(sc - m_rep) for sc in _lane_chunks(s)]
            lp = l_s[...]
            for pc in ps:
                lp = lp + pc
            l_s[...] = lp
            p = jnp.concatenate([pc.astype(BF16) for pc in ps], axis=1)
            acc[...] += _dot(p, v_ref[pl.ds(off, tk), :], _NN)
            return carry

        lax.fori_loop(0, nk, sum_step, 0)
        l_row = jnp.sum(l_s[...], axis=-1, keepdims=True)
        _unstack_heads(acc[...] / l_row, o_ref, tq)
        lse_ref[...] = m_row + jnp.log(l_row)
        if ng:
            pl.when(pl.program_id(0) == nq - 1)(ex.wait)

    sd = jax.ShapeDtypeStruct
    return pl.pallas_call(
        body, grid=(nq,),
        in_specs=[_row_spec(tq, Q_W), _par_spec(KV_W, t), _par_spec(KV_W, t)] + [_ANY] * ng,
        out_specs=[_row_spec(tq, Q_W), _row_spec(r, 1)] + [_ANY] * ng,
        out_shape=[sd((t, Q_W), F32), sd((nq * r, 1), F32)] + _gathered_shapes(gather),
        scratch_shapes=[pltpu.VMEM((r, LANES), BF16), pltpu.VMEM((r, LANES), F32), pltpu.VMEM((r, LANES), F32),
                        pltpu.VMEM((r, LANES), F32)] + (_exchange_sems(ng) if ng else []),
        compiler_params=_cparams(("arbitrary",) if ng else ("parallel",)),
        name="gattn_fwd_gather" if ng else "gattn_fwd",
    )(q, k, v, *gather)


def _gattn_bwd(q, k, v, o, do, lse, scatter=None, tq=128, tk=512):
    t = q.shape[0]
    tk = min(tk, t)
    nq, nk, r = t // tq, t // tk, 8 * tq
    items, sgrads = scatter if scatter else ((), ())
    ns = len(sgrads)
    slot_shapes = []
    for j, (o_idx, _) in enumerate(items):
        if o_idx == len(slot_shapes):
            slot_shapes += _slot_shapes([sgrads[j]])
    nslots = len(slot_shapes)

    def body(*refs):
        q_ref, k_ref, v_ref, o_ref, do_ref, lse_ref = refs[:6]
        dq_ref, dk_ref, dv_ref = refs[6 + ns:9 + ns]
        qs, dos, dqa = refs[9 + ns + nslots:12 + ns + nslots]
        if ns:
            ex = _scatter_exchange(items, refs[6:6 + ns], refs[9 + ns:9 + ns + nslots], *refs[12 + ns + nslots:])
            pl.when(pl.program_id(0) == 0)(ex.start)

        @pl.when(pl.program_id(0) == 0)
        def _():
            dk_ref[...] = jnp.zeros_like(dk_ref)
            dv_ref[...] = jnp.zeros_like(dv_ref)

        _stack_heads(q_ref, qs, tq)
        _stack_heads(do_ref, dos, tq)
        delta = _stacked_delta(do_ref, o_ref, tq)
        lse_v = lse_ref[...]
        dqa[...] = jnp.zeros_like(dqa)

        def step(j, carry):
            off = pl.multiple_of(j * tk, tk)
            kc = k_ref[pl.ds(off, tk), :]
            vc = v_ref[pl.ds(off, tk), :]
            p = jnp.exp(_dot(qs[...], kc, _NT) - lse_v)
            dp = _dot(dos[...], vc, _NT)
            ds = (p * (dp - delta)).astype(BF16)
            dqa[...] += _dot(ds, kc, _NN)
            dk_ref[pl.ds(off, tk), :] += _dot(ds, qs[...], _TN)
            dv_ref[pl.ds(off, tk), :] += _dot(p, dos[...], _TN)
            return carry

        lax.fori_loop(0, nk, step, 0)
        _unstack_heads(dqa[...], dq_ref, tq)
        if ns:
            pl.when(pl.program_id(0) == nq - 1)(ex.wait)

    sd = jax.ShapeDtypeStruct
    return pl.pallas_call(
        body, grid=(nq,),
        in_specs=[_row_spec(tq, Q_W), _par_spec(KV_W, t), _par_spec(KV_W, t), _row_spec(tq, Q_W), _row_spec(tq, Q_W),
                  _row_spec(r, 1)] + [_ANY] * ns,
        out_specs=[_row_spec(tq, Q_W), _par_spec(KV_W, t), _par_spec(KV_W, t)] + [_ANY] * nslots,
        out_shape=[sd((t, Q_W), F32), sd((t, KV_W), F32), sd((t, KV_W), F32)] + slot_shapes,
        scratch_shapes=[pltpu.VMEM((r, LANES), BF16), pltpu.VMEM((r, LANES), BF16), pltpu.VMEM((r, LANES), F32)]
        + (_exchange_sems(ns) if ns else []),
        compiler_params=_cparams(("arbitrary",)), name="gattn_bwd_scatter" if ns else "gattn_bwd",
    )(q, k, v, o, do, lse, *sgrads)


_WQ = Q_BLOCK
_WK = 3 * Q_BLOCK
_WR = 8 * _WQ
_WNB = 2


def _wattn_scores(qs, kw, bias_ref, n, t):
    col = lax.broadcasted_iota(jnp.int32, (1, _WK), 1)
    kabs = (n - 1) * _WQ + col
    s = _dot(qs[...], kw, _NT) + bias_ref[...]
    return jnp.where((kabs >= 0) & (kabs < t), s, NEG)


def _wattn_fwd(q, kp, vp, bias, sink):
    t = q.shape[0]
    nq = t // _WQ

    def body(q_ref, k_ref, v_ref, b_ref, sk_ref, o_ref, lse_ref, qs):
        sk = sk_ref[...]
        for b in range(_WNB):
            n = pl.program_id(0) * _WNB + b
            rows = pl.ds(b * _WQ, _WQ)
            _stack_heads(q_ref.at[rows, :], qs.at[b], _WQ)
            off = pl.multiple_of(n * _WQ, _WQ)
            kw = k_ref[pl.ds(off, _WK), :]
            vw = v_ref[pl.ds(off, _WK), :]
            s = _wattn_scores(qs.at[b], kw, b_ref, n, t)
            m = jnp.maximum(jnp.max(s, axis=-1, keepdims=True), sk)
            p = jnp.exp(s - m)
            l = jnp.sum(p, axis=-1, keepdims=True) + jnp.exp(sk - m)
            _unstack_heads(_dot(p, vw, _NN) / l, o_ref.at[rows, :], _WQ)
            lse_ref[pl.ds(b * _WR, _WR), :] = m + jnp.log(l)

    sd = jax.ShapeDtypeStruct
    return pl.pallas_call(
        body, grid=(nq // _WNB,),
        in_specs=[_row_spec(_WNB * _WQ, Q_W), _par_spec(KV_W, t + 2 * _WQ), _par_spec(KV_W, t + 2 * _WQ),
                  _par_spec(_WK, _WR), _par_spec(1, _WR)],
        out_specs=[_row_spec(_WNB * _WQ, Q_W), _row_spec(_WNB * _WR, 1)],
        out_shape=[sd((t, Q_W), F32), sd((nq * _WR, 1), F32)],
        scratch_shapes=[pltpu.VMEM((_WNB, _WR, LANES), BF16)],
        compiler_params=_cparams(("parallel",)), name="wattn_fwd",
    )(q, kp, vp, bias, sink)


def _wattn_bwd(q, kp, vp, bias, sink, o, do, lse):
    t = q.shape[0]
    nq = t // _WQ

    def body(q_ref, k_ref, v_ref, b_ref, sk_ref, o_ref, do_ref, lse_ref, dq_ref, dk_ref, dv_ref, db_ref, dsk_ref, qs, dos):
        @pl.when(pl.program_id(0) == 0)
        def _():
            dk_ref[...] = jnp.zeros_like(dk_ref)
            dv_ref[...] = jnp.zeros_like(dv_ref)
            db_ref[...] = jnp.zeros_like(db_ref)
            dsk_ref[...] = jnp.zeros_like(dsk_ref)

        dbias = jnp.zeros((_WR, _WK), F32)
        dsink = jnp.zeros((_WR, 1), F32)
        parts = []
        for b in range(_WNB):
            n = pl.program_id(0) * _WNB + b
            rows = pl.ds(b * _WQ, _WQ)
            _stack_heads(q_ref.at[rows, :], qs.at[b], _WQ)
            _stack_heads(do_ref.at[rows, :], dos.at[b], _WQ)
            delta = _stacked_delta(do_ref.at[rows, :], o_ref.at[rows, :], _WQ)
            off = pl.multiple_of(n * _WQ, _WQ)
            kw = k_ref[pl.ds(off, _WK), :]
            vw = v_ref[pl.ds(off, _WK), :]
            lse_v = lse_ref[pl.ds(b * _WR, _WR), :]
            p = jnp.exp(_wattn_scores(qs.at[b], kw, b_ref, n, t) - lse_v)
            dp = _dot(dos[b], vw, _NT)
            ds = p * (dp - delta)
            dbias = dbias + ds
            dsink = dsink - jnp.exp(sk_ref[...] - lse_v) * delta
            dsb = ds.astype(BF16)
            _unstack_heads(_dot(dsb, kw, _NN), dq_ref.at[rows, :], _WQ)
            parts.append((off, _dot(dsb, qs[b], _TN), _dot(p, dos[b], _TN)))
        db_ref[...] += dbias
        dsk_ref[...] += dsink
        for off, dkw, dvw in parts:
            dk_ref[pl.ds(off, _WK), :] += dkw
            dv_ref[pl.ds(off, _WK), :] += dvw

    sd = jax.ShapeDtypeStruct
    tp = t + 2 * _WQ
    qb = _row_spec(_WNB * _WQ, Q_W)
    return pl.pallas_call(
        body, grid=(nq // _WNB,),
        in_specs=[qb, _par_spec(KV_W, tp), _par_spec(KV_W, tp), _par_spec(_WK, _WR), _par_spec(1, _WR),
                  qb, qb, _row_spec(_WNB * _WR, 1)],
        out_specs=[qb, _par_spec(KV_W, tp), _par_spec(KV_W, tp), _par_spec(_WK, _WR), _par_spec(1, _WR)],
        out_shape=[sd((t, Q_W), F32), sd((tp, KV_W), F32), sd((tp, KV_W), F32), sd((_WR, _WK), F32), sd((_WR, 1), F32)],
        scratch_shapes=[pltpu.VMEM((_WNB, _WR, LANES), BF16), pltpu.VMEM((_WNB, _WR, LANES), BF16)],
        compiler_params=_cparams(("arbitrary",)), name="wattn_bwd",
    )(q, kp, vp, bias, sink, o, do, lse)


def _bias_bucket_reduce(db0, db1, bucket):
    def body(a_ref, b_ref, bk_ref, o_ref):
        d = a_ref[...] + b_ref[...]
        bk = bk_ref[...]
        lane = lax.broadcasted_iota(jnp.int32, (1, LANES), 1)
        out = jnp.zeros((1, LANES), F32)
        for b in range(N_BUCKETS):
            tot = jnp.sum(jnp.sum(jnp.where(bk == b, d, 0.0), axis=-1, keepdims=True), axis=0, keepdims=True)
            out = out + jnp.where(lane == b, tot, 0.0)
        o_ref[...] = out

    hb = pl.BlockSpec((None, _WQ, _WK), lambda h: (h, 0, 0))
    return pl.pallas_call(
        body, grid=(8,), in_specs=[hb, hb, pl.BlockSpec((_WQ, _WK), lambda h: (0, 0))],
        out_specs=pl.BlockSpec((None, 1, LANES), lambda h: (h, 0, 0)),
        out_shape=jax.ShapeDtypeStruct((8, 1, LANES), F32),
        compiler_params=_cparams(("parallel",)), name="bias_bucket_reduce",
    )(db0.reshape(8, _WQ, _WK), db1.reshape(8, _WQ, _WK), bucket)


def _rope_tables(t):
    rows_n = t // GRID_W
    row = jnp.repeat(jnp.arange(rows_n, dtype=F32), GRID_W)
    col = jnp.tile(jnp.arange(GRID_W, dtype=F32), rows_n)
    half = HEAD_DIM // 2
    inv_freq = ROPE_THETA ** (-jnp.arange(0, half, 2, dtype=F32) / half)
    ang = jnp.concatenate([row[:, None] * inv_freq, col[:, None] * inv_freq], axis=-1)
    cos, sin = jnp.cos(ang), jnp.sin(ang)
    c64 = jnp.repeat(cos, 2, axis=-1)
    s64 = jnp.stack([-sin, sin], axis=-1).reshape(t, HEAD_DIM)
    return jnp.tile(c64, (1, 2)), jnp.tile(s64, (1, 2))


def _t5_bucket(rel):
    half = N_BUCKETS // 2
    max_exact = half // 2
    bucket = jnp.where(rel > 0, half, 0)
    rp = jnp.abs(rel)
    rpf = jnp.maximum(rp, 1).astype(jnp.float32)
    large = max_exact + (jnp.log(rpf / max_exact) / math.log(MAX_DISTANCE / max_exact)
                         * (half - max_exact)).astype(jnp.int32)
    large = jnp.minimum(large, half - 1)
    return bucket + jnp.where(rp < max_exact, rp, large)


def _window_tables(rel_bias):
    qpos = jnp.arange(_WQ, dtype=jnp.int32)
    kpos = jnp.arange(_WK, dtype=jnp.int32) - _WQ
    rel = kpos[None, :] - qpos[:, None]
    bucket = _t5_bucket(rel)
    bias = jnp.zeros((8, _WQ, _WK), F32)
    for b in range(N_BUCKETS):
        bias = jnp.where((bucket == b)[None], rel_bias[b][:, None, None], bias)
    bias = jnp.where((jnp.abs(rel) <= WINDOW)[None], bias, NEG)
    return bias.reshape(_WR, _WK), bucket


def _pad_rows(a):
    return jnp.pad(a, ((_WQ, _WQ), (0, 0)))


def _layer_fwd(x, p, tabs, gather=None):
    cos_t, sin_t, bias = tabs
    h = _mm_nn(x, p["win"], F32, "in_proj")
    qa, ka, va, qb, kb, vb = _prep_fwd(h, cos_t, sin_t, p["qn"], p["kn"])
    if gather is None:
        oa, lse_a = _gattn_fwd(qa, ka, va)
    else:
        oa, lse_a, *gathered = _gattn_fwd(qa, ka, va, gather=gather[0])
        p = gather[1](gathered)
    kbp, vbp = _pad_rows(kb), _pad_rows(vb)
    ob, lse_b = _wattn_fwd(qb, kbp, vbp, bias, p["sink"])
    y = _outnorm_fwd(oa, ob, p["ga"], p["gb"])
    z1, x1, x1b = _mm_reduce(y[None], p["wout"][None], _NN, F32, "out_proj", res=x, res_scale=ALPHA,
                             ln=(p["ln1g"], p["ln1b"]))
    gu = _mm_expand(x1b, p["wgu"], _NN, BF16, "gate_up_proj")
    hdn = _ffn_mid_fwd(gu, p["cw"])
    z2, x2, _ = _mm_reduce(hdn, p["wd"], _NN, F32, "down_proj", res=x1, res_scale=ALPHA, ln=(p["ln2g"], p["ln2b"]))
    saved = dict(x=x, h=h, qa=qa, ka=ka, va=va, qb=qb, kbp=kbp, vbp=vbp, oa=oa, ob=ob, lse_a=lse_a, lse_b=lse_b,
                 y=y, z1=z1, x1b=x1b, gu=gu, hdn=hdn, z2=z2)
    return x2, saved


def _block_grads(g, names=("w_in", "w_out", "w_gate", "w_up", "w_down")):
    make = dict(
        w_in=lambda: _col_blocks(_in_cols_to_pairs(g["win"], _from_pairs), IN_SH),
        w_out=lambda: _mix_rows_to_pairs(g["wout"], _from_pairs).reshape(N_SHARD, OUT_SH, D_MODEL),
        w_gate=lambda: g["wg"], w_up=lambda: g["wu"], w_down=lambda: g["wd"])
    return [make[n]() for n in names]


def _layer_bwd(dx2, p, s, tabs, layer=0, pending=None):
    cos_t, sin_t, bias = tabs
    t = dx2.shape[0]
    dz2, dz2b, dln2g, dln2b = _ln_bwd(dx2, s["z2"], p["ln2g"])
    dhdn = _mm_expand(dz2b, p["wd"], _NT, BF16, "down_dx")
    dwd = _mm_tn_blocks(s["hdn"], dz2b, "down_dw")
    dgu, stats = _ffn_mid_bwd(s["gu"], dhdn, p["cw"])
    dgu = dgu.reshape(2 * N_SHARD, t, FF_SH)
    dx1 = _mm_reduce(dgu, p["wgu"], _NT, F32, "gate_up_dx", res=dz2, res_scale=ALPHA)
    dwg = _mm_tn_blocks(s["x1b"], dgu, "gate_dw", blk=0)
    dwu = _mm_tn_blocks(s["x1b"], dgu, "up_dw", blk=1)
    dz1, dz1b, dln1g, dln1b = _ln_bwd(dx1, s["z1"], p["ln1g"])
    dy = _mm_nt(dz1b, p["wout"], F32, "out_dx")
    dwout = _mm_tn(s["y"], dz1b, "out_dw")
    doa, dob, dga, dgb = _outnorm_bwd(dy, s["oa"], s["ob"], p["ga"], p["gb"])
    slots = None
    if pending is None:
        dqa, dka, dva = _gattn_bwd(s["qa"], s["ka"], s["va"], s["oa"], doa, s["lse_a"])
    else:
        mine = _block_grads(dict(wout=dwout, wg=dwg, wu=dwu, wd=dwd), ("w_out", "w_gate", "w_up", "w_down"))
        todo = list(pending) + [(o + 1, layer, g) for o, g in enumerate(mine)]
        dqa, dka, dva, *slots = _gattn_bwd(s["qa"], s["ka"], s["va"], s["oa"], doa, s["lse_a"],
                                           scatter=([(o, l) for o, l, _ in todo], [g for _, _, g in todo]))
    dqb, dkbp, dvbp, dbias, dsink = _wattn_bwd(s["qb"], s["kbp"], s["vbp"], bias, p["sink"], s["ob"], dob, s["lse_b"])
    dkb = lax.slice_in_dim(dkbp, _WQ, _WQ + t, axis=0)
    dvb = lax.slice_in_dim(dvbp, _WQ, _WQ + t, axis=0)
    dh, dqn, dkn = _prep_bwd(s["h"], cos_t, sin_t, p["qn"], p["kn"], dqa, dka, dva, dqb, dkb, dvb)
    dx = _mm_nt(dh, p["win"], F32, "in_dx", res=dz1, res_scale=ALPHA)
    dwin = _mm_tn(s["x"], dh, "in_dw")
    grads = dict(win=dwin, wout=dwout, wg=dwg, wu=dwu, wd=dwd, stats=stats, qn=dqn, kn=dkn, ga=dga, gb=dgb,
                 ln1g=dln1g, ln1b=dln1b, ln2g=dln2g, ln2b=dln2b, bias=dbias, sink=dsink, slots=slots)
    return dx, grads


def _prep_layer_params(l, win, wout, wg, wu, wd, cw, q_norm, k_norm, sink, out_norm_a, out_norm_b, conv_b,
                       ln1_g, ln1_b, ln2_g, ln2_b):
    win_full = win.transpose(1, 0, 2).reshape(D_MODEL, IN_COLS)
    row = lambda v: v.reshape(1, -1)
    late = {}
    if wout is not None:
        late = dict(
            wout=_mix_rows_to_pairs(wout.reshape(D_MODEL, D_MODEL)), wgu=jnp.concatenate([wg, wu], axis=0), wd=wd,
            cw=jnp.pad(cw, ((0, 0), (0, 5), (0, 0)))
            + jnp.pad(conv_b[l].reshape(N_SHARD, 1, FF_SH), ((0, 0), (3, 4), (0, 0))))
    return dict(
        late, win=_in_cols_to_pairs(win_full),
        qn=row(jnp.tile(q_norm[l], 2)), kn=row(jnp.tile(k_norm[l], 2)),
        ga=row(_to_pairs(out_norm_a[l], 0)), gb=row(_to_pairs(out_norm_b[l], 0)),
        ln1g=row(ln1_g[l]), ln1b=row(ln1_b[l]), ln2g=row(ln2_g[l]), ln2b=row(ln2_b[l]),
        sink=jnp.repeat(sink[l], _WQ).reshape(_WR, 1))


def _local_step(x, tgt, params, rel_bias, gather=None, scatter=False):
    t = x.shape[0]
    cos_t, sin_t = _rope_tables(t)
    bias, bucket = _window_tables(rel_bias)
    tabs = (cos_t, sin_t, bias)
    saved = []
    for l in range(DEPTH):
        x, s = _layer_fwd(x, params[l], tabs, gather if l == 0 else None)
        saved.append(s)
    dx, loss = _loss_grad(x, tgt)
    grads = [None] * DEPTH
    for l in reversed(range(DEPTH)):
        pending = None
        if scatter and l == 0:
            pending = [(o, 1, g) for o, g in enumerate(_block_grads(grads[1]))]
        dx, grads[l] = _layer_bwd(dx, params[l], saved[l], tabs, l, pending)
    dbucket = _bias_bucket_reduce(grads[0]["bias"], grads[1]["bias"], bucket)
    return loss, dx, grads, dbucket


_ANY = pl.BlockSpec(memory_space=pl.ANY)
_MESH = pl.DeviceIdType.MESH


def _mesh_pos():
    return lax.axis_index("x"), lax.axis_index("y"), lax.axis_index("c")


def _other_chips(x, y):
    return [(1 - x, y), (x, 1 - y), (1 - x, 1 - y)]


class _Exchange:
    def __init__(self, local, sends, recvs):
        self.local, self.sends, self.recvs = local, sends, recvs

    def start(self):
        for cp in self.local + self.sends:
            cp.start()

    def wait(self):
        for cp in self.recvs:
            cp.wait_recv()
        for cp in self.sends:
            cp.wait_send()
        for cp in self.local:
            cp.wait()


def _exchange_sems(n):
    return [pltpu.SemaphoreType.DMA((n, 3)), pltpu.SemaphoreType.DMA((n, 3)), pltpu.SemaphoreType.DMA((n,))]


def _gather_exchange(ins, outs, send, recv, loc):
    x, y, c = _mesh_pos()
    me = 2 * x + y
    chips = _other_chips(x, y)

    def remote(i, k, block):
        px, py = chips[k]
        return pltpu.make_async_remote_copy(ins[i], outs[i].at[block], send.at[i, k], recv.at[i, k],
                                            device_id=(px, py, c), device_id_type=_MESH)

    n = len(ins)
    local = [pltpu.make_async_copy(ins[i], outs[i].at[me], loc.at[i]) for i in range(n)]
    sends = [remote(i, k, me) for i in range(n) for k in range(3)]
    recvs = [remote(i, k, 2 * chips[k][0] + chips[k][1]) for i in range(n) for k in range(3)]
    return _Exchange(local, sends, recvs)


def _scatter_exchange(items, ins, outs, send, recv, loc):
    x, y, c = _mesh_pos()
    me = 2 * x + y
    chips = _other_chips(x, y)

    def remote(j, k):
        o, l = items[j]
        px, py = chips[k]
        return pltpu.make_async_remote_copy(ins[j].at[2 * px + py], outs[o].at[k, l], send.at[j, k], recv.at[j, k],
                                            device_id=(px, py, c), device_id_type=_MESH)

    local = [pltpu.make_async_copy(ins[j].at[me], outs[o].at[3, l], loc.at[j]) for j, (o, l) in enumerate(items)]
    sends = [remote(j, k) for j in range(len(items)) for k in range(3)]
    return _Exchange(local, sends, sends)


def _gathered_shapes(shards):
    return [jax.ShapeDtypeStruct((N_SHARD,) + s.shape, s.dtype) for s in shards]


def _slot_shapes(blocks):
    return [jax.ShapeDtypeStruct((N_SHARD, DEPTH) + g.shape[1:], g.dtype) for g in blocks]


def _gather_shards(shards):
    n = len(shards)

    def body(*refs):
        ex = _gather_exchange(refs[:n], refs[n:2 * n], *refs[2 * n:])
        ex.start()
        ex.wait()

    return pl.pallas_call(
        body, in_specs=[_ANY] * n, out_specs=[_ANY] * n, out_shape=_gathered_shapes(shards),
        scratch_shapes=_exchange_sems(n), name="gather_weights",
    )(*shards)


def _scatter_into(items, grads, slots):
    n, ns = len(grads), len(slots)

    def body(*refs):
        ex = _scatter_exchange(items, refs[:n], refs[n + ns:n + 2 * ns], *refs[n + 2 * ns:])
        ex.start()
        ex.wait()

    return pl.pallas_call(
        body, in_specs=[_ANY] * (n + ns), out_specs=[_ANY] * ns,
        out_shape=[jax.ShapeDtypeStruct(s.shape, s.dtype) for s in slots],
        input_output_aliases={n + i: i for i in range(ns)},
        scratch_shapes=_exchange_sems(n), name="scatter_grads",
    )(*grads, *slots)


def _swap_with_sibling(parts):
    n = len(parts)

    def body(*refs):
        ins, outs = refs[:n], refs[n:2 * n]
        send, recv = refs[2 * n:]
        x, y, c = _mesh_pos()
        copies = [pltpu.make_async_remote_copy(ins[i], outs[i], send.at[i], recv.at[i], device_id=(x, y, 1 - c),
                                               device_id_type=_MESH) for i in range(n)]
        for cp in copies:
            cp.start()
        for cp in copies:
            cp.wait_recv()
        for cp in copies:
            cp.wait_send()

    return pl.pallas_call(
        body, in_specs=[_ANY] * n, out_specs=[_ANY] * n,
        out_shape=[jax.ShapeDtypeStruct(p.shape, p.dtype) for p in parts],
        scratch_shapes=[pltpu.SemaphoreType.DMA((n,)), pltpu.SemaphoreType.DMA((n,))],
        name="swap_sibling",
    )(*parts)


N_DEV = 8


def _allreduce_small(packed):
    rows = packed.shape[0]

    def body(in_ref, out_ref, buf, send, recv, loc):
        x, y, c = _mesh_pos()
        me = 4 * x + 2 * y + c
        own = pltpu.make_async_copy(in_ref, buf.at[me], loc)
        own.start()

        def remote(m, block):
            peer = (x ^ (m >> 2), y ^ ((m >> 1) & 1), c ^ (m & 1))
            return pltpu.make_async_remote_copy(in_ref, buf.at[block], send.at[m - 1], recv.at[m - 1],
                                                device_id=peer, device_id_type=_MESH)

        sends = [remote(m, me) for m in range(1, N_DEV)]
        for cp in sends:
            cp.start()
        for m in range(1, N_DEV):
            remote(m, me ^ m).wait_recv()
        for cp in sends:
            cp.wait_send()
        own.wait()
        tot = buf[0]
        for d in range(1, N_DEV):
            tot = tot + buf[d]
        out_ref[...] = tot

    vm = pl.BlockSpec(memory_space=pltpu.VMEM)
    return pl.pallas_call(
        body, in_specs=[vm], out_specs=vm, out_shape=jax.ShapeDtypeStruct((rows, LANES), F32),
        scratch_shapes=[pltpu.VMEM((N_DEV, rows, LANES), F32), pltpu.SemaphoreType.DMA((N_DEV - 1,)),
                        pltpu.SemaphoreType.DMA((N_DEV - 1,)), pltpu.SemaphoreType.DMA(())],
        name="allreduce_small",
    )(packed)


def _shard_rows(r):
    return r // 2 if r % 32 == 0 else r


def _sum_slots(slots):
    _, _, r, cdim = slots.shape
    tr = _shard_rows(r)

    def body(a_ref, b_ref, c_ref, d_ref, o_ref):
        up = lambda ref: ref[...].astype(F32)
        o_ref[...] = ((up(d_ref) + up(a_ref)) + up(b_ref)) + up(c_ref)

    def spec(k):
        return pl.BlockSpec((None, None, tr, cdim), lambda l, i: (k, l, i, 0))

    return pl.pallas_call(
        body, grid=(DEPTH, r // tr), in_specs=[spec(0), spec(1), spec(2), spec(3)],
        out_specs=pl.BlockSpec((None, tr, cdim), lambda l, i: (l, i, 0)),
        out_shape=jax.ShapeDtypeStruct((DEPTH, r, cdim), F32),
        compiler_params=_cparams(("parallel", "parallel")), name="sum_slots",
    )(slots, slots, slots, slots)


def _adamw_math(w, g, m, v):
    m = ADAM_B1 * m + (1.0 - ADAM_B1) * g
    v = ADAM_B2 * v + (1.0 - ADAM_B2) * (g * g)
    m_hat = m / (1.0 - ADAM_B1 ** ADAM_STEP)
    v_hat = v / (1.0 - ADAM_B2 ** ADAM_STEP)
    delta = -ADAM_LR * (m_hat / (jnp.sqrt(v_hat) + ADAM_EPS) + ADAM_WD * w)
    return delta, m, v


def _adamw_big(ga, gb, w, m, v):
    _, r, cdim = w.shape
    tr = _shard_rows(r)

    def body(ga_ref, gb_ref, w_ref, m_ref, v_ref, g_out, d_out, m_out, v_out):
        g = ga_ref[...] + gb_ref[...]
        d, mn, vn = _adamw_math(w_ref[...], g, m_ref[...], v_ref[...])
        g_out[...] = g
        d_out[...] = d
        m_out[...] = mn
        v_out[...] = vn

    spec = pl.BlockSpec((None, tr, cdim), lambda l, i: (l, i, 0))
    shp = jax.ShapeDtypeStruct(w.shape, F32)
    return pl.pallas_call(
        body, grid=(DEPTH, r // tr), in_specs=[spec] * 5, out_specs=[spec] * 4, out_shape=[shp] * 4,
        compiler_params=_cparams(("parallel", "parallel")), name="adamw_big",
    )(ga, gb, w, m, v)


def _adamw_small(ws, gs, ms, vs):
    n = len(ws)

    def body(*refs):
        w_r, g_r, m_r, v_r = (refs[k * n:(k + 1) * n] for k in range(4))
        d_o, m_o, v_o = (refs[(4 + k) * n:(5 + k) * n] for k in range(3))
        for i in range(n):
            d, mn, vn = _adamw_math(w_r[i][...], g_r[i][...], m_r[i][...], v_r[i][...])
            d_o[i][...] = d
            m_o[i][...] = mn
            v_o[i][...] = vn

    vm = pl.BlockSpec(memory_space=pltpu.VMEM)
    shp = [jax.ShapeDtypeStruct(w.shape, F32) for w in ws]
    outs = pl.pallas_call(
        body, in_specs=[vm] * (4 * n), out_specs=[vm] * (3 * n), out_shape=shp * 3, name="adamw_small",
    )(*ws, *gs, *ms, *vs)
    return outs[:n], outs[n:2 * n], outs[2 * n:]


def _tile_rows(a):
    a = a.reshape(-1, LANES)
    pad = (-a.shape[0]) % 8
    return jnp.pad(a, ((0, pad), (0, 0))) if pad else a


_SMALL_LAYER_PARTS = (("qn", 8), ("kn", 8), ("sink", 8), ("ga", 8), ("gb", 8), ("ln1g", 8), ("ln1b", 8),
                      ("ln2g", 8), ("ln2b", 8), ("stats", N_SHARD * 8 * FF_SH // LANES))
_SMALL_HEAD_ROWS = 16
_SMALL_LAYER_ROWS = sum(r for _, r in _SMALL_LAYER_PARTS)


def _pack_small(loss, dbucket, grads):
    parts = [_tile_rows(loss), _tile_rows(dbucket)]
    for l in range(DEPTH):
        parts += [_tile_rows(grads[l][name]) for name, _ in _SMALL_LAYER_PARTS]
    return jnp.concatenate(parts, axis=0)


def _unpack_small(tot, chip):
    out = dict(loss=tot[0, 0], rel_bias=tot[8:16, :N_BUCKETS].T)
    per = {name: [] for name, _ in _SMALL_LAYER_PARTS}
    for l in range(DEPTH):
        base = _SMALL_HEAD_ROWS + l * _SMALL_LAYER_ROWS
        for name, rows in _SMALL_LAYER_PARTS:
            per[name].append(tot[base:base + rows])
            base += rows
    fold = lambda v: v[0, :HEAD_DIM] + v[0, HEAD_DIM:]
    out["q_norm"] = jnp.stack([fold(v) for v in per["qn"]])
    out["k_norm"] = jnp.stack([fold(v) for v in per["kn"]])
    out["sink"] = jnp.stack([jnp.sum(v, axis=1) for v in per["sink"]])
    out["out_norm_a"] = jnp.stack([_from_pairs(v[:4].reshape(Q_W), 0) for v in per["ga"]])
    out["out_norm_b"] = jnp.stack([_from_pairs(v[:4].reshape(Q_W), 0) for v in per["gb"]])
    for name, key in (("ln1_g", "ln1g"), ("ln1_b", "ln1b"), ("ln2_g", "ln2g"), ("ln2_b", "ln2b")):
        out[name] = jnp.stack([v.reshape(D_MODEL) for v in per[key]])
    stats = [v.reshape(N_SHARD, 8, FF_SH) for v in per["stats"]]
    out["conv_b"] = jnp.stack([s[:, 0, :].reshape(D_FF) for s in stats])
    out["conv_w"] = jnp.stack([lax.dynamic_index_in_dim(s, chip, 0, keepdims=False)[1:4] for s in stats])
    return out


_WEIGHTS = ("rel_bias", "w_in", "q_norm", "k_norm", "sink", "out_norm_a", "out_norm_b", "w_out", "ln1_g", "ln1_b",
            "w_gate", "w_up", "conv_w", "conv_b", "w_down", "ln2_g", "ln2_b")
_BIG = ("w_in", "w_out", "w_gate", "w_up", "w_down")
_SMALL = tuple(n for n in _WEIGHTS if n not in _BIG)


def _col_blocks(g, n):
    return g.reshape(g.shape[0], N_SHARD, n).transpose(1, 0, 2)


def kernel(x, rel_bias, w_in, q_norm, k_norm, sink, out_norm_a, out_norm_b, w_out, ln1_g, ln1_b, w_gate, w_up, conv_w, conv_b, w_down, ln2_g, ln2_b, loss_target, m_rel_bias, m_w_in, m_q_norm, m_k_norm, m_sink, m_out_norm_a, m_out_norm_b, m_w_out, m_ln1_g, m_ln1_b, m_w_gate, m_w_up, m_conv_w, m_conv_b, m_w_down, m_ln2_g, m_ln2_b, v_rel_bias, v_w_in, v_q_norm, v_k_norm, v_sink, v_out_norm_a, v_out_norm_b, v_w_out, v_ln1_g, v_ln1_b, v_w_gate, v_w_up, v_conv_w, v_conv_b, v_w_down, v_ln2_g, v_ln2_b):
    w = dict(rel_bias=rel_bias, w_in=w_in, q_norm=q_norm, k_norm=k_norm, sink=sink, out_norm_a=out_norm_a,
             out_norm_b=out_norm_b, w_out=w_out, ln1_g=ln1_g, ln1_b=ln1_b, w_gate=w_gate, w_up=w_up, conv_w=conv_w,
             conv_b=conv_b, w_down=w_down, ln2_g=ln2_g, ln2_b=ln2_b)
    m = dict(rel_bias=m_rel_bias, w_in=m_w_in, q_norm=m_q_norm, k_norm=m_k_norm, sink=m_sink, out_norm_a=m_out_norm_a,
             out_norm_b=m_out_norm_b, w_out=m_w_out, ln1_g=m_ln1_g, ln1_b=m_ln1_b, w_gate=m_w_gate, w_up=m_w_up,
             conv_w=m_conv_w, conv_b=m_conv_b, w_down=m_w_down, ln2_g=m_ln2_g, ln2_b=m_ln2_b)
    v = dict(rel_bias=v_rel_bias, w_in=v_w_in, q_norm=v_q_norm, k_norm=v_k_norm, sink=v_sink, out_norm_a=v_out_norm_a,
             out_norm_b=v_out_norm_b, w_out=v_w_out, ln1_g=v_ln1_g, ln1_b=v_ln1_b, w_gate=v_w_gate, w_up=v_w_up,
             conv_w=v_conv_w, conv_b=v_conv_b, w_down=v_w_down, ln2_g=v_ln2_g, ln2_b=v_ln2_b)
    chip = 2 * lax.axis_index("x") + lax.axis_index("y")

    small_w = (q_norm, k_norm, sink, out_norm_a, out_norm_b, conv_b, ln1_g, ln1_b, ln2_g, ln2_b)
    (win0,) = _gather_shards([w_in[0].astype(BF16)])
    later = ([w[name][0].astype(BF16) for name in _BIG[1:]] + [w[name][1].astype(BF16) for name in _BIG] + [conv_w])
    params = [_prep_layer_params(0, win0, None, None, None, None, None, *small_w), None]

    def finish(g):
        wout0, wg0, wu0, wd0, win1, wout1, wg1, wu1, wd1, cw_all = g
        params[0] = _prep_layer_params(0, win0, wout0, wg0, wu0, wd0, cw_all[:, 0], *small_w)
        params[1] = _prep_layer_params(1, win1, wout1, wg1, wu1, wd1, cw_all[:, 1], *small_w)
        return params[0]

    loss, dx, grads, dbucket = _local_step(x[0], loss_target[0], params, rel_bias, gather=(later, finish),
                                           scatter=True)

    small = _unpack_small(_allreduce_small(_pack_small(loss, dbucket, grads)), chip)

    slots = list(grads[0]["slots"])
    slots[0] = _scatter_into([(0, 0)], _block_grads(grads[0], ("w_in",)), [slots[0]])[0]
    partial = [_sum_slots(s) for s in slots]
    other = _swap_with_sibling(partial)

    grad, delta, new_m, new_v = {}, {}, {}, {}
    for i, name in enumerate(_BIG):
        grad[name], delta[name], new_m[name], new_v[name] = _adamw_big(partial[i], other[i], w[name], m[name], v[name])
    flat2 = lambda a: a.reshape(-1, a.shape[-1])
    ds, ms, vs = _adamw_small([flat2(w[n]) for n in _SMALL], [flat2(small[n]) for n in _SMALL],
                              [flat2(m[n]) for n in _SMALL], [flat2(v[n]) for n in _SMALL])
    for i, name in enumerate(_SMALL):
        grad[name] = small[name]
        delta[name] = ds[i].reshape(w[name].shape)
        new_m[name] = ms[i].reshape(w[name].shape)
        new_v[name] = vs[i].reshape(w[name].shape)

    return (small["loss"], dx[None], *[grad[n] for n in _WEIGHTS], *[delta[n] for n in _WEIGHTS],
            *[new_m[n] for n in _WEIGHTS], *[new_v[n] for n in _WEIGHTS])
```

```python
import functools
import math

import numpy as np
import jax
import jax.numpy as jnp
from jax import lax
from jax.experimental import pallas as pl
from jax.experimental.pallas import tpu as pltpu

F32 = jnp.float32
BF16 = jnp.bfloat16

D_MODEL = 1024
DEPTH = 2
HEAD_DIM = 64
Q_W = 512
KV_W = 128
IN_COLS = 2 * (Q_W + 2 * KV_W)
N_SHARD = 4
IN_SH = IN_COLS // N_SHARD
OUT_SH = D_MODEL // N_SHARD
D_FF = 2816
FF_SH = D_FF // N_SHARD
Q_BLOCK = 128
WINDOW = 128
N_BUCKETS = 32
MAX_DISTANCE = 128
GRID_W = 64
ROPE_THETA = 10000.0
ALPHA = (2.0 * DEPTH) ** 0.25
RMS_EPS = 1e-6
LN_EPS = 1e-5
NEG = -1e30
LANES = 128
VMEM_LIMIT = 56 * 1024 * 1024

ADAM_LR = 0.001
ADAM_B1 = 0.9
ADAM_B2 = 0.999
ADAM_EPS = 1e-08
ADAM_WD = 0.01
ADAM_STEP = 10

_NN = (((1,), (0,)), ((), ()))
_NT = (((1,), (1,)), ((), ()))
_TN = (((0,), (0,)), ((), ()))


def _dot(a, b, dims):
    return lax.dot_general(a.astype(BF16), b.astype(BF16), dims, preferred_element_type=F32)


def _cparams(sem, vmem=VMEM_LIMIT):
    return pltpu.CompilerParams(dimension_semantics=sem, vmem_limit_bytes=vmem)


def _regroup(a, axis, n_outer, n_inner):
    shp = a.shape
    a = a.reshape(shp[:axis] + (n_outer, n_inner, HEAD_DIM) + shp[axis + 1:])
    return jnp.swapaxes(a, axis, axis + 1).reshape(shp)


def _to_pairs(a, axis):
    return _regroup(a, axis, 2, 4)


def _from_pairs(a, axis):
    return _regroup(a, axis, 4, 2)


def _in_cols_to_pairs(w, fn=_to_pairs):
    return jnp.concatenate([fn(w[..., :Q_W], w.ndim - 1), w[..., Q_W:Q_W + 2 * KV_W],
                            fn(w[..., Q_W + 2 * KV_W:2 * Q_W + 2 * KV_W], w.ndim - 1),
                            w[..., 2 * Q_W + 2 * KV_W:]], axis=-1)


def _mix_rows_to_pairs(w, fn=_to_pairs):
    return fn(w.reshape(2, Q_W, w.shape[-1]), 1).reshape(w.shape)


def _matmul(a, b, *, dims, grid, a_spec, b_spec, o_spec, out_shape, acc_shape, name, res=None,
            res_spec=None, res_scale=1.0):
    nk = grid[-1]
    kax = len(grid) - 1

    def body(*refs):
        if res is None:
            a_ref, b_ref, o_ref, acc = refs
            r_ref = None
        else:
            a_ref, b_ref, r_ref, o_ref, acc = refs
        k = pl.program_id(kax)

        @pl.when(k == 0)
        def _():
            acc[...] = jnp.zeros_like(acc)

        acc[...] += _dot(a_ref[...], b_ref[...], dims)

        @pl.when(k == nk - 1)
        def _():
            o = acc[...]
            if r_ref is not None:
                o = o + res_scale * r_ref[...]
            o_ref[...] = o.astype(o_ref.dtype)

    in_specs = [a_spec, b_spec] + ([res_spec] if res is not None else [])
    args = (a, b) + ((res,) if res is not None else ())
    sem = ("parallel",) * kax + ("arbitrary",)
    return pl.pallas_call(
        body, grid=grid, in_specs=in_specs, out_specs=o_spec, out_shape=out_shape,
        scratch_shapes=[pltpu.VMEM(acc_shape, F32)], compiler_params=_cparams(sem), name=name,
    )(*args)


def _mm_nn(a, b, out_dtype, name, tm=512, res=None, res_scale=1.0):
    m, kd = a.shape
    n = b.shape[1]
    return _matmul(
        a, b, dims=_NN, grid=(m // tm, 1),
        a_spec=pl.BlockSpec((tm, kd), lambda i, k: (i, 0)),
        b_spec=pl.BlockSpec((kd, n), lambda i, k: (0, 0)),
        o_spec=pl.BlockSpec((tm, n), lambda i, k: (i, 0)),
        out_shape=jax.ShapeDtypeStruct((m, n), out_dtype), acc_shape=(tm, n), name=name,
        res=res, res_spec=pl.BlockSpec((tm, n), lambda i, k: (i, 0)), res_scale=res_scale)


def _mm_nt(a, b, out_dtype, name, tm=512, res=None, res_scale=1.0):
    m, kd = a.shape
    n = b.shape[0]
    return _matmul(
        a, b, dims=_NT, grid=(m // tm, 1),
        a_spec=pl.BlockSpec((tm, kd), lambda i, k: (i, 0)),
        b_spec=pl.BlockSpec((n, kd), lambda i, k: (0, 0)),
        o_spec=pl.BlockSpec((tm, n), lambda i, k: (i, 0)),
        out_shape=jax.ShapeDtypeStruct((m, n), out_dtype), acc_shape=(tm, n), name=name,
        res=res, res_spec=pl.BlockSpec((tm, n), lambda i, k: (i, 0)), res_scale=res_scale)


def _mm_tn(a, b, name, tk=512, tn=None, out_dtype=BF16):
    t, m = a.shape
    n = b.shape[1]
    tn = n if tn is None else tn
    return _matmul(
        a, b, dims=_TN, grid=(n // tn, t // tk),
        a_spec=pl.BlockSpec((tk, m), lambda j, k: (k, 0)),
        b_spec=pl.BlockSpec((tk, tn), lambda j, k: (k, j)),
        o_spec=pl.BlockSpec((m, tn), lambda j, k: (0, j)),
        out_shape=jax.ShapeDtypeStruct((m, n), out_dtype), acc_shape=(m, tn), name=name)


def _blocked_n(w, dims):
    return w.shape[2] if dims == _NN else w.shape[1]


def _mm_expand(a, w, dims, out_dtype, name, tm=512):
    m, kd = a.shape
    nb, n = w.shape[0], _blocked_n(w, dims)

    def body(a_ref, w_ref, o_ref):
        av = a_ref[...]
        for j in range(nb):
            o_ref[j] = _dot(av, w_ref[j], dims).astype(o_ref.dtype)

    return pl.pallas_call(
        body, grid=(m // tm,),
        in_specs=[pl.BlockSpec((tm, kd), lambda i: (i, 0)), pl.BlockSpec(w.shape, lambda i: (0, 0, 0))],
        out_specs=pl.BlockSpec((nb, tm, n), lambda i: (0, i, 0)),
        out_shape=jax.ShapeDtypeStruct((nb, m, n), out_dtype),
        compiler_params=_cparams(("parallel",)), name=name,
    )(a, w)


def _mm_reduce(a, w, dims, out_dtype, name, tm=512, res=None, res_scale=1.0, ln=None):
    nb, m, kd = a.shape
    n = _blocked_n(w, dims)
    n_in = 2 + (res is not None) + (2 if ln else 0)

    def body(*refs):
        a_ref, w_ref = refs[0], refs[1]
        acc = _dot(a_ref[0], w_ref[0], dims)
        for j in range(1, nb):
            acc = acc + _dot(a_ref[j], w_ref[j], dims)
        if res is not None:
            acc = acc + res_scale * refs[2][...]
        refs[n_in][...] = acc.astype(out_dtype)
        if ln:
            g_ref, b_ref = refs[n_in - 2], refs[n_in - 1]
            zc = acc - jnp.mean(acc, axis=-1, keepdims=True)
            r = lax.rsqrt(jnp.mean(zc * zc, axis=-1, keepdims=True) + LN_EPS)
            y = zc * r * g_ref[...] + b_ref[...]
            refs[n_in + 1][...] = y
            refs[n_in + 2][...] = y.astype(BF16)

    row = pl.BlockSpec((tm, n), lambda i: (i, 0))
    par = pl.BlockSpec((1, n), lambda i: (0, 0))
    sd = jax.ShapeDtypeStruct
    out = pl.pallas_call(
        body, grid=(m // tm,),
        in_specs=[pl.BlockSpec((nb, tm, kd), lambda i: (0, i, 0)), pl.BlockSpec(w.shape, lambda i: (0, 0, 0))]
        + ([row] if res is not None else []) + ([par, par] if ln else []),
        out_specs=[row] * (3 if ln else 1),
        out_shape=[sd((m, n), out_dtype)] + ([sd((m, n), F32), sd((m, n), BF16)] if ln else []),
        compiler_params=_cparams(("parallel",)), name=name,
    )(a, w, *((res,) if res is not None else ()), *(ln or ()))
    return out if ln else out[0]


def _mm_tn_blocks(a, b, name, blk=0, nb=N_SHARD, tk=512, out_dtype=BF16):
    a3, b3 = a.ndim == 3, b.ndim == 3
    t, m, n = a.shape[-2], a.shape[-1], b.shape[-1]
    nsteps = t // tk

    def spec(blocked, width):
        if blocked:
            return pl.BlockSpec((nb, tk, width), lambda k: (blk, k, 0))
        return pl.BlockSpec((tk, width), lambda k: (k, 0))

    def body(a_ref, b_ref, o_ref, acc):
        k = pl.program_id(0)

        @pl.when(k == 0)
        def _():
            acc[...] = jnp.zeros_like(acc)

        for j in range(nb):
            acc[j] += _dot(a_ref[j] if a3 else a_ref[...], b_ref[j] if b3 else b_ref[...], _TN)

        @pl.when(k == nsteps - 1)
        def _():
            o_ref[...] = acc[...].astype(o_ref.dtype)

    return pl.pallas_call(
        body, grid=(nsteps,), in_specs=[spec(a3, m), spec(b3, n)],
        out_specs=pl.BlockSpec((nb, m, n), lambda k: (0, 0, 0)),
        out_shape=jax.ShapeDtypeStruct((nb, m, n), out_dtype),
        scratch_shapes=[pltpu.VMEM((nb, m, n), F32)],
        compiler_params=_cparams(("arbitrary",)), name=name,
    )(a, b)


def _row_spec(tm, n):
    return pl.BlockSpec((tm, n), lambda i: (i, 0))


def _par_spec(n, rows=1):
    return pl.BlockSpec((rows, n), lambda i: (0, 0))


def _swap_pairs(x):
    lane = lax.broadcasted_iota(jnp.int32, x.shape, 1)
    return jnp.where(lane % 2 == 0, pltpu.roll(x, LANES - 1, 1), pltpu.roll(x, 1, 1))


def _head_sums(v):
    lo = lax.broadcasted_iota(jnp.int32, v.shape, 1) < HEAD_DIM
    s_lo = jnp.sum(jnp.where(lo, v, 0.0), axis=-1, keepdims=True)
    s_hi = jnp.sum(jnp.where(lo, 0.0, v), axis=-1, keepdims=True)
    return jnp.where(lo, s_lo, s_hi)


def _qk_blocks():
    return [(128 * i, True) for i in range(4)] + [(Q_W, False)]


def _prep_fwd(h, cos_t, sin_t, qn, kn, tm=256):
    t = h.shape[0]
    scale = HEAD_DIM ** -0.5

    def body(h_ref, c_ref, s_ref, qn_ref, kn_ref, qa_ref, ka_ref, va_ref, qb_ref, kb_ref, vb_ref):
        c = c_ref[...]
        s = s_ref[...]
        for start, is_q in _qk_blocks():
            x = h_ref[:, start:start + LANES]
            r = lax.rsqrt(_head_sums(x * x) * (1.0 / HEAD_DIM) + RMS_EPS)
            y = x * r * (qn_ref[...] if is_q else kn_ref[...])
            y = y * c + _swap_pairs(y) * s
            if is_q:
                qa_ref[:, start:start + LANES] = (y * scale).astype(BF16)
            else:
                ka_ref[...] = y.astype(BF16)
        va_ref[...] = h_ref[:, 640:768].astype(BF16)
        qb_ref[...] = (h_ref[:, 768:1280] * scale).astype(BF16)
        kb_ref[...] = h_ref[:, 1280:1408].astype(BF16)
        vb_ref[...] = h_ref[:, 1408:1536].astype(BF16)

    sd = jax.ShapeDtypeStruct
    return pl.pallas_call(
        body, grid=(t // tm,),
        in_specs=[_row_spec(tm, IN_COLS), _row_spec(tm, LANES), _row_spec(tm, LANES), _par_spec(LANES), _par_spec(LANES)],
        out_specs=[_row_spec(tm, Q_W), _row_spec(tm, KV_W), _row_spec(tm, KV_W),
                   _row_spec(tm, Q_W), _row_spec(tm, KV_W), _row_spec(tm, KV_W)],
        out_shape=[sd((t, Q_W), BF16), sd((t, KV_W), BF16), sd((t, KV_W), BF16),
                   sd((t, Q_W), BF16), sd((t, KV_W), BF16), sd((t, KV_W), BF16)],
        compiler_params=_cparams(("parallel",)), name="prep_fwd",
    )(h, cos_t, sin_t, qn, kn)


def _prep_bwd(h, cos_t, sin_t, qn, kn, dqa, dka, dva, dqb, dkb, dvb, tm=256):
    t = h.shape[0]
    scale = HEAD_DIM ** -0.5

    def body(h_ref, c_ref, s_ref, qn_ref, kn_ref, dqa_ref, dka_ref, dva_ref, dqb_ref, dkb_ref, dvb_ref,
             dh_ref, dqn_ref, dkn_ref):
        @pl.when(pl.program_id(0) == 0)
        def _():
            dqn_ref[...] = jnp.zeros_like(dqn_ref)
            dkn_ref[...] = jnp.zeros_like(dkn_ref)

        c = c_ref[...]
        s = s_ref[...]
        for start, is_q in _qk_blocks():
            x = h_ref[:, start:start + LANES]
            gain = qn_ref[...] if is_q else kn_ref[...]
            d = dqa_ref[:, start:start + LANES] * scale if is_q else dka_ref[...]
            dy = d * c + _swap_pairs(d * s)
            r = lax.rsqrt(_head_sums(x * x) * (1.0 / HEAD_DIM) + RMS_EPS)
            xr = x * r
            gsum = jnp.sum(dy * xr, axis=0, keepdims=True)
            if is_q:
                dqn_ref[...] += gsum
            else:
                dkn_ref[...] += gsum
            gy = dy * gain
            dx = r * (gy - xr * (_head_sums(xr * gy) * (1.0 / HEAD_DIM)))
            dh_ref[:, start:start + LANES] = dx.astype(BF16)
        dh_ref[:, 640:768] = dva_ref[...].astype(BF16)
        dh_ref[:, 768:1280] = (dqb_ref[...] * scale).astype(BF16)
        dh_ref[:, 1280:1408] = dkb_ref[...].astype(BF16)
        dh_ref[:, 1408:1536] = dvb_ref[...].astype(BF16)

    sd = jax.ShapeDtypeStruct
    return pl.pallas_call(
        body, grid=(t // tm,),
        in_specs=[_row_spec(tm, IN_COLS), _row_spec(tm, LANES), _row_spec(tm, LANES), _par_spec(LANES), _par_spec(LANES),
                  _row_spec(tm, Q_W), _row_spec(tm, KV_W), _row_spec(tm, KV_W),
                  _row_spec(tm, Q_W), _row_spec(tm, KV_W), _row_spec(tm, KV_W)],
        out_specs=[_row_spec(tm, IN_COLS), _par_spec(LANES), _par_spec(LANES)],
        out_shape=[sd((t, IN_COLS), BF16), sd((1, LANES), F32), sd((1, LANES), F32)],
        compiler_params=_cparams(("arbitrary",)), name="prep_bwd",
    )(h, cos_t, sin_t, qn, kn, dqa, dka, dva, dqb, dkb, dvb)


def _outnorm_fwd(oa, ob, ga, gb, tm=512):
    t = oa.shape[0]

    def body(oa_ref, ob_ref, ga_ref, gb_ref, y_ref):
        for o_ref, g_ref, start in ((oa_ref, ga_ref, 0), (ob_ref, gb_ref, Q_W)):
            x = o_ref[...]
            r = lax.rsqrt(jnp.mean(x * x, axis=-1, keepdims=True) + RMS_EPS)
            y_ref[:, start:start + Q_W] = (x * r * g_ref[...]).astype(BF16)

    return pl.pallas_call(
        body, grid=(t // tm,),
        in_specs=[_row_spec(tm, Q_W), _row_spec(tm, Q_W), _par_spec(Q_W), _par_spec(Q_W)],
        out_specs=_row_spec(tm, D_MODEL), out_shape=jax.ShapeDtypeStruct((t, D_MODEL), BF16),
        compiler_params=_cparams(("parallel",)), name="outnorm_fwd",
    )(oa, ob, ga, gb)


def _outnorm_bwd(dy, oa, ob, ga, gb, tm=512):
    t = oa.shape[0]

    def body(dy_ref, oa_ref, ob_ref, ga_ref, gb_ref, doa_ref, dob_ref, dga_ref, dgb_ref):
        @pl.when(pl.program_id(0) == 0)
        def _():
            dga_ref[...] = jnp.zeros_like(dga_ref)
            dgb_ref[...] = jnp.zeros_like(dgb_ref)

        for o_ref, g_ref, do_ref, dg_ref, start in ((oa_ref, ga_ref, doa_ref, dga_ref, 0),
                                                    (ob_ref, gb_ref, dob_ref, dgb_ref, Q_W)):
            x = o_ref[...]
            d = dy_ref[:, start:start + Q_W]
            r = lax.rsqrt(jnp.mean(x * x, axis=-1, keepdims=True) + RMS_EPS)
            xr = x * r
            dg_ref[...] += jnp.sum(d * xr, axis=0, keepdims=True)
            gy = d * g_ref[...]
            do_ref[...] = r * (gy - xr * jnp.mean(xr * gy, axis=-1, keepdims=True))

    sd = jax.ShapeDtypeStruct
    return pl.pallas_call(
        body, grid=(t // tm,),
        in_specs=[_row_spec(tm, D_MODEL), _row_spec(tm, Q_W), _row_spec(tm, Q_W), _par_spec(Q_W), _par_spec(Q_W)],
        out_specs=[_row_spec(tm, Q_W), _row_spec(tm, Q_W), _par_spec(Q_W), _par_spec(Q_W)],
        out_shape=[sd((t, Q_W), F32), sd((t, Q_W), F32), sd((1, Q_W), F32), sd((1, Q_W), F32)],
        compiler_params=_cparams(("arbitrary",)), name="outnorm_bwd",
    )(dy, oa, ob, ga, gb)


def _ln_bwd(d, z, g, tm=512):
    t = z.shape[0]

    def body(d_ref, z_ref, g_ref, dz_ref, dzb_ref, dg_ref, db_ref):
        @pl.when(pl.program_id(0) == 0)
        def _():
            dg_ref[...] = jnp.zeros_like(dg_ref)
            db_ref[...] = jnp.zeros_like(db_ref)

        zz = z_ref[...]
        dd = d_ref[...]
        mu = jnp.mean(zz, axis=-1, keepdims=True)
        zc = zz - mu
        r = lax.rsqrt(jnp.mean(zc * zc, axis=-1, keepdims=True) + LN_EPS)
        xh = zc * r
        dg_ref[...] += jnp.sum(dd * xh, axis=0, keepdims=True)
        db_ref[...] += jnp.sum(dd, axis=0, keepdims=True)
        dxh = dd * g_ref[...]
        dz = r * (dxh - jnp.mean(dxh, axis=-1, keepdims=True) - xh * jnp.mean(dxh * xh, axis=-1, keepdims=True))
        dz_ref[...] = dz
        dzb_ref[...] = dz.astype(BF16)

    sd = jax.ShapeDtypeStruct
    return pl.pallas_call(
        body, grid=(t // tm,),
        in_specs=[_row_spec(tm, D_MODEL), _row_spec(tm, D_MODEL), _par_spec(D_MODEL)],
        out_specs=[_row_spec(tm, D_MODEL), _row_spec(tm, D_MODEL), _par_spec(D_MODEL), _par_spec(D_MODEL)],
        out_shape=[sd((t, D_MODEL), F32), sd((t, D_MODEL), BF16), sd((1, D_MODEL), F32), sd((1, D_MODEL), F32)],
        compiler_params=_cparams(("arbitrary",)), name="ln_bwd",
    )(d, z, g)


def _loss_grad(y, tgt, tm=512):
    t = y.shape[0]
    nsteps = t // tm

    def body(y_ref, t_ref, dy_ref, loss_ref, acc):
        i = pl.program_id(0)

        @pl.when(i == 0)
        def _():
            acc[...] = jnp.zeros_like(acc)

        e = y_ref[...] - t_ref[...]
        dy_ref[...] = e * (1.0 / D_MODEL)
        acc[...] += jnp.sum(e * e, axis=0, keepdims=True)

        @pl.when(i == nsteps - 1)
        def _():
            tot = jnp.sum(acc[...], axis=-1, keepdims=True) * (0.5 / D_MODEL)
            loss_ref[...] = jnp.broadcast_to(tot, loss_ref.shape)

    sd = jax.ShapeDtypeStruct
    return pl.pallas_call(
        body, grid=(nsteps,),
        in_specs=[_row_spec(tm, D_MODEL), _row_spec(tm, D_MODEL)],
        out_specs=[_row_spec(tm, D_MODEL), _par_spec(LANES)],
        out_shape=[sd((t, D_MODEL), F32), sd((1, LANES), F32)],
        scratch_shapes=[pltpu.VMEM((1, D_MODEL), F32)],
        compiler_params=_cparams(("arbitrary",)), name="loss_grad",
    )(y, tgt)


_GELU_C = math.sqrt(2.0 / math.pi)
_GELU_K = 0.044715
HALO = 16


def _gelu_parts(x):
    th = jnp.tanh(_GELU_C * (x + _GELU_K * x * x * x))
    return 0.5 * x * (1.0 + th), th


def _halo_specs(tm, t, shift=0):
    last = t // HALO - 1
    cur = pl.BlockSpec((None, tm, FF_SH), lambda j, i: (j + shift, i, 0))
    prev = pl.BlockSpec((None, HALO, FF_SH), lambda j, i: (j + shift, jnp.maximum(i * (tm // HALO) - 1, 0), 0))
    nxt = pl.BlockSpec((None, HALO, FF_SH), lambda j, i: (j + shift, jnp.minimum((i + 1) * (tm // HALO), last), 0))
    return [prev, cur, nxt]


def _ffn_mid_fwd(gu, cw, tm=512):
    t = gu.shape[1]
    nsteps = t // tm

    def body(gp_ref, g_ref, gn_ref, u_ref, cw_ref, h_ref):
        i = pl.program_id(1)
        gg = g_ref[...].astype(F32)
        row = lax.broadcasted_iota(jnp.int32, gg.shape, 0)
        prev = jnp.where(i == 0, 0.0, gp_ref[...].astype(F32)[HALO - 1:HALO, :])
        nxt = jnp.where(i == nsteps - 1, 0.0, gn_ref[...].astype(F32)[0:1, :])
        g_m1 = jnp.where(row == 0, prev, pltpu.roll(gg, 1, 0))
        g_p1 = jnp.where(row == tm - 1, nxt, pltpu.roll(gg, tm - 1, 0))
        gc = cw_ref[3:4, :] + g_m1 * cw_ref[0:1, :] + gg * cw_ref[1:2, :] + g_p1 * cw_ref[2:3, :]
        act, _ = _gelu_parts(gc)
        h_ref[...] = (act * u_ref[...].astype(F32)).astype(BF16)

    return pl.pallas_call(
        body, grid=(N_SHARD, nsteps),
        in_specs=_halo_specs(tm, t) + [pl.BlockSpec((None, tm, FF_SH), lambda j, i: (j + N_SHARD, i, 0)),
                                       pl.BlockSpec((None, 8, FF_SH), lambda j, i: (j, 0, 0))],
        out_specs=pl.BlockSpec((None, tm, FF_SH), lambda j, i: (j, i, 0)),
        out_shape=jax.ShapeDtypeStruct((N_SHARD, t, FF_SH), BF16),
        compiler_params=_cparams(("parallel", "parallel")), name="ffn_mid_fwd",
    )(gu, gu, gu, gu, cw)


def _ffn_mid_bwd(gu, dh, cw, tm=512):
    t = gu.shape[1]
    nsteps = t // tm
    te = tm + 2 * HALO

    def body(gp_ref, g_ref, gn_ref, up_ref, u_ref, un_ref, dp_ref, d_ref, dn_ref, cw_ref, dgu_ref, st_ref):
        i = pl.program_id(1)

        @pl.when(i == 0)
        def _():
            st_ref[...] = jnp.zeros_like(st_ref)

        e = lax.broadcasted_iota(jnp.int32, (te, FF_SH), 0)
        tg = i * tm - HALO + e
        valid = (tg >= 0) & (tg < t)

        def ext(p_ref, c_ref, n_ref):
            whole = jnp.concatenate([p_ref[...], c_ref[...], n_ref[...]], axis=0).astype(F32)
            return jnp.where(valid, whole, 0.0)

        eg = ext(gp_ref, g_ref, gn_ref)
        eu = ext(up_ref, u_ref, un_ref)
        ed = ext(dp_ref, d_ref, dn_ref)
        w0, w1, w2 = cw_ref[0:1, :], cw_ref[1:2, :], cw_ref[2:3, :]
        g_m1 = pltpu.roll(eg, 1, 0)
        g_p1 = pltpu.roll(eg, te - 1, 0)
        gc = cw_ref[3:4, :] + g_m1 * w0 + eg * w1 + g_p1 * w2
        act, th = _gelu_parts(gc)
        dact = 0.5 * (1.0 + th) + 0.5 * gc * (1.0 - th * th) * _GELU_C * (1.0 + 3.0 * _GELU_K * gc * gc)
        dgc = ed * eu * dact
        dg = pltpu.roll(dgc, te - 1, 0) * w0 + dgc * w1 + pltpu.roll(dgc, 1, 0) * w2
        mid = slice(HALO, HALO + tm)
        dgu_ref[0] = dg[mid].astype(BF16)
        dgu_ref[1] = (ed * act)[mid].astype(BF16)
        sel = dgc[mid]
        parts = [jnp.sum(sel, axis=0, keepdims=True),
                 jnp.sum(sel * g_m1[mid], axis=0, keepdims=True),
                 jnp.sum(sel * eg[mid], axis=0, keepdims=True),
                 jnp.sum(sel * g_p1[mid], axis=0, keepdims=True)]
        r8 = lax.broadcasted_iota(jnp.int32, (8, FF_SH), 0)
        upd = jnp.zeros((8, FF_SH), F32)
        for k, p in enumerate(parts):
            upd = upd + jnp.where(r8 == k, p, 0.0)
        st_ref[...] += upd

    sd = jax.ShapeDtypeStruct
    return pl.pallas_call(
        body, grid=(N_SHARD, nsteps),
        in_specs=_halo_specs(tm, t) + _halo_specs(tm, t, N_SHARD) + _halo_specs(tm, t)
        + [pl.BlockSpec((None, 8, FF_SH), lambda j, i: (j, 0, 0))],
        out_specs=[pl.BlockSpec((2, None, tm, FF_SH), lambda j, i: (0, j, i, 0)),
                   pl.BlockSpec((None, 8, FF_SH), lambda j, i: (j, 0, 0))],
        out_shape=[sd((2, N_SHARD, t, FF_SH), BF16), sd((N_SHARD, 8, FF_SH), F32)],
        compiler_params=_cparams(("parallel", "arbitrary")), name="ffn_mid_bwd",
    )(gu, gu, gu, gu, gu, gu, dh, dh, dh, cw)


def _lo_mask(rows):
    return lax.broadcasted_iota(jnp.int32, (rows, LANES), 1) < HEAD_DIM


def _stack_heads(src_ref, dst_ref, tq):
    lo = _lo_mask(tq)
    for i in range(4):
        blk = src_ref[:, LANES * i:LANES * (i + 1)].astype(dst_ref.dtype)
        zero = jnp.zeros_like(blk)
        dst_ref[tq * i:tq * (i + 1), :] = jnp.where(lo, blk, zero)
        dst_ref[tq * (4 + i):tq * (5 + i), :] = jnp.where(lo, zero, blk)


def _unstack_heads(st, dst_ref, tq):
    lo = _lo_mask(tq)
    for i in range(4):
        dst_ref[:, LANES * i:LANES * (i + 1)] = jnp.where(
            lo, st[tq * i:tq * (i + 1)], st[tq * (4 + i):tq * (5 + i)]).astype(dst_ref.dtype)


def _stacked_delta(do_ref, o_ref, tq):
    lo = _lo_mask(tq)
    los, his = [], []
    for i in range(4):
        pr = do_ref[:, LANES * i:LANES * (i + 1)] * o_ref[:, LANES * i:LANES * (i + 1)]
        los.append(jnp.sum(jnp.where(lo, pr, 0.0), axis=-1, keepdims=True))
        his.append(jnp.sum(jnp.where(lo, 0.0, pr), axis=-1, keepdims=True))
    return jnp.concatenate(los + his, axis=0)


def _lane_chunks(a):
    return [a[:, LANES * c:LANES * (c + 1)] for c in range(a.shape[1] // LANES)]


def _gattn_fwd(q, k, v, gather=(), tq=128, tk=2048):
    t = q.shape[0]
    tk = min(tk, t)
    nq, nk, r = t // tq, t // tk, 8 * tq
    ng = len(gather)

    def body(*refs):
        q_ref, k_ref, v_ref = refs[:3]
        o_ref, lse_ref = refs[3 + ng:5 + ng]
        qst, m_s, l_s, acct = refs[5 + 2 * ng:9 + 2 * ng]
        if ng:
            ex = _gather_exchange(refs[3:3 + ng], refs[5 + ng:5 + 2 * ng], *refs[9 + 2 * ng:])
            pl.when(pl.program_id(0) == 0)(ex.start)
        lo_rows = lax.broadcasted_iota(jnp.int32, (LANES, tq), 0) < HEAD_DIM
        for i in range(4):
            bt = q_ref[:, LANES * i:LANES * (i + 1)].astype(F32).T
            qst[:, tq * i:tq * (i + 1)] = jnp.where(lo_rows, bt, 0.0).astype(BF16)
            qst[:, tq * (4 + i):tq * (5 + i)] = jnp.where(lo_rows, 0.0, bt).astype(BF16)
        m_s[...] = jnp.full_like(m_s, NEG)

        def max_step(j, carry):
            off = pl.multiple_of(j * tk, tk)
            st = _dot(k_ref[pl.ds(off, tk), :], qst[...], _NN)
            m_s[...] = jnp.maximum(m_s[...], jnp.max(st.reshape(tk // 8, 8, r), axis=0))
            return carry

        lax.fori_loop(0, nk, max_step, 0)
        m_row = jnp.max(m_s[...], axis=0, keepdims=True)
        l_s[...] = jnp.zeros_like(l_s)
        acct[...] = jnp.zeros_like(acct)

        def sum_step(j, carry):
            off = pl.multiple_of(j * tk, tk)
            st = _dot(k_ref[pl.ds(off, tk), :], qst[...], _NN)
            pt = jnp.exp(st - m_row)
            l_s[...] += jnp.sum(pt.reshape(tk // 8, 8, r), axis=0)
            acct[...] += _dot(v_ref[j], pt, _NN)
            return carry

        lax.fori_loop(0, nk, sum_step, 0)
        l_row = jnp.sum(l_s[...], axis=0, keepdims=True)
        ot = acct[...] / l_row
        for i in range(4):
            pair_t = jnp.where(lo_rows, ot[:, tq * i:tq * (i + 1)], ot[:, tq * (4 + i):tq * (5 + i)])
            o_ref[:, LANES * i:LANES * (i + 1)] = pair_t.T
        lse_ref[...] = m_row + jnp.log(l_row)
        if ng:
            pl.when(pl.program_id(0) == nq - 1)(ex.wait)

    sd = jax.ShapeDtypeStruct
    vt3 = v.reshape(nk, tk, KV_W).transpose(0, 2, 1)
    return pl.pallas_call(
        body, grid=(nq,),
        in_specs=[_row_spec(tq, Q_W), _par_spec(KV_W, t), pl.BlockSpec((nk, KV_W, tk), lambda i: (0, 0, 0))]
        + [_ANY] * ng,
        out_specs=[_row_spec(tq, Q_W), pl.BlockSpec((None, 1, r), lambda i: (i, 0, 0))] + [_ANY] * ng,
        out_shape=[sd((t, Q_W), F32), sd((nq, 1, r), F32)] + _gathered_shapes(gather),
        scratch_shapes=[pltpu.VMEM((LANES, r), BF16), pltpu.VMEM((8, r), F32), pltpu.VMEM((8, r), F32),
                        pltpu.VMEM((LANES, r), F32)] + (_exchange_sems(ng) if ng else []),
        compiler_params=_cparams(("arbitrary",) if ng else ("parallel",)),
        name="gattn_fwd_gather" if ng else "gattn_fwd",
    )(q, k, vt3, *gather)


def _gattn_bwd(q, k, v, o, do, lse, scatter=None, tq=128, tk=1024):
    t = q.shape[0]
    tk = min(tk, t)
    nq, nk, r = t // tq, t // tk, 8 * tq
    items, sgrads = scatter if scatter else ((), ())
    ns = len(sgrads)
    slot_shapes = []
    for j, (o_idx, _) in enumerate(items):
        if o_idx == len(slot_shapes):
            slot_shapes += _slot_shapes([sgrads[j]])
    nslots = len(slot_shapes)

    n_in, n_scr = 7, 6
    kt3 = k.reshape(nk, tk, KV_W).transpose(0, 2, 1)

    def body(*refs):
        q_ref, k_ref, v_ref, kt_ref, o_ref, do_ref, lse_ref = refs[:n_in]
        dq_ref, dk_ref, dv_ref = refs[n_in + ns:n_in + 3 + ns]
        scr = n_in + 3 + ns + nslots
        qs, dos, qst, dost, dlt_row, dqt = refs[scr:scr + n_scr]
        if ns:
            ex = _scatter_exchange(items, refs[n_in:n_in + ns], refs[n_in + 3 + ns:scr], *refs[scr + n_scr:])
            pl.when(pl.program_id(0) == 0)(ex.start)

        @pl.when(pl.program_id(0) == 0)
        def _():
            dk_ref[...] = jnp.zeros_like(dk_ref)
            dv_ref[...] = jnp.zeros_like(dv_ref)

        _stack_heads(q_ref, qs, tq)
        _stack_heads(do_ref, dos, tq)
        lo_rows = lax.broadcasted_iota(jnp.int32, (LANES, tq), 0) < HEAD_DIM
        for i in range(4):
            lo, hi = slice(tq * i, tq * (i + 1)), slice(tq * (4 + i), tq * (5 + i))
            cols = slice(LANES * i, LANES * (i + 1))
            for src, dst in ((q_ref, qst), (do_ref, dost)):
                bt = src[:, cols].astype(F32).T
                dst[:, lo] = jnp.where(lo_rows, bt, 0.0).astype(BF16)
                dst[:, hi] = jnp.where(lo_rows, 0.0, bt).astype(BF16)
            prod_t = (do_ref[:, cols] * o_ref[:, cols]).T
            dlt_row[:, lo] = jnp.sum(prod_t[:HEAD_DIM], axis=0, keepdims=True)
            dlt_row[:, hi] = jnp.sum(prod_t[HEAD_DIM:], axis=0, keepdims=True)
        lse_row = lse_ref[...]
        dqt[...] = jnp.zeros_like(dqt)

        def step(j, carry):
            off = pl.multiple_of(j * tk, tk)
            kc = k_ref[pl.ds(off, tk), :]
            vc = v_ref[pl.ds(off, tk), :]
            p = jnp.exp(_dot(kc, qst[...], _NN) - lse_row)
            dp = _dot(vc, dost[...], _NN)
            ds = (p * (dp - dlt_row[...])).astype(BF16)
            dk_ref[pl.ds(off, tk), :] += _dot(ds, qs[...], _NN)
            dv_ref[pl.ds(off, tk), :] += _dot(p, dos[...], _NN)
            dqt[...] += _dot(kt_ref[j], ds, _NN)
            return carry

        lax.fori_loop(0, nk, step, 0)
        for i in range(4):
            pair_t = jnp.where(lo_rows, dqt[:, tq * i:tq * (i + 1)], dqt[:, tq * (4 + i):tq * (5 + i)])
            dq_ref[:, LANES * i:LANES * (i + 1)] = pair_t.T
        if ns:
            pl.when(pl.program_id(0) == nq - 1)(ex.wait)

    sd = jax.ShapeDtypeStruct
    return pl.pallas_call(
        body, grid=(nq,),
        in_specs=[_row_spec(tq, Q_W), _par_spec(KV_W, t), _par_spec(KV_W, t),
                  pl.BlockSpec((nk, KV_W, tk), lambda i: (0, 0, 0)), _row_spec(tq, Q_W), _row_spec(tq, Q_W),
                  pl.BlockSpec((None, 1, r), lambda i: (i, 0, 0))] + [_ANY] * ns,
        out_specs=[_row_spec(tq, Q_W), _par_spec(KV_W, t), _par_spec(KV_W, t)] + [_ANY] * nslots,
        out_shape=[sd((t, Q_W), F32), sd((t, KV_W), F32), sd((t, KV_W), F32)] + slot_shapes,
        scratch_shapes=[pltpu.VMEM((r, LANES), BF16), pltpu.VMEM((r, LANES), BF16), pltpu.VMEM((LANES, r), BF16),
                        pltpu.VMEM((LANES, r), BF16), pltpu.VMEM((1, r), F32),
                        pltpu.VMEM((LANES, r), F32)] + (_exchange_sems(ns) if ns else []),
        compiler_params=_cparams(("arbitrary",)), name="gattn_bwd_scatter" if ns else "gattn_bwd",
    )(q, k, v, kt3, o, do, lse, *sgrads)


_WQ = Q_BLOCK
_WK = 3 * Q_BLOCK
_WR = 8 * _WQ
_WNB = 2


def _wattn_scores(qs, kw, bias_ref, n, t):
    col = lax.broadcasted_iota(jnp.int32, (1, _WK), 1)
    kabs = (n - 1) * _WQ + col
    s = _dot(qs[...], kw, _NT) + bias_ref[...]
    return jnp.where((kabs >= 0) & (kabs < t), s, NEG)


def _wattn_fwd(q, kp, vp, bias, sink):
    t = q.shape[0]
    nq = t // _WQ

    def body(q_ref, k_ref, v_ref, b_ref, sk_ref, o_ref, lse_ref, qs):
        sk = sk_ref[...]
        for b in range(_WNB):
            n = pl.program_id(0) * _WNB + b
            rows = pl.ds(b * _WQ, _WQ)
            _stack_heads(q_ref.at[rows, :], qs.at[b], _WQ)
            off = pl.multiple_of(n * _WQ, _WQ)
            kw = k_ref[pl.ds(off, _WK), :]
            vw = v_ref[pl.ds(off, _WK), :]
            s = _wattn_scores(qs.at[b], kw, b_ref, n, t)
            m = jnp.maximum(jnp.max(s, axis=-1, keepdims=True), sk)
            p = jnp.exp(s - m)
            l = jnp.sum(p, axis=-1, keepdims=True) + jnp.exp(sk - m)
            _unstack_heads(_dot(p, vw, _NN) / l, o_ref.at[rows, :], _WQ)
            lse_ref[pl.ds(b * _WR, _WR), :] = m + jnp.log(l)

    sd = jax.ShapeDtypeStruct
    return pl.pallas_call(
        body, grid=(nq // _WNB,),
        in_specs=[_row_spec(_WNB * _WQ, Q_W), _par_spec(KV_W, t + 2 * _WQ), _par_spec(KV_W, t + 2 * _WQ),
                  _par_spec(_WK, _WR), _par_spec(1, _WR)],
        out_specs=[_row_spec(_WNB * _WQ, Q_W), _row_spec(_WNB * _WR, 1)],
        out_shape=[sd((t, Q_W), F32), sd((nq * _WR, 1), F32)],
        scratch_shapes=[pltpu.VMEM((_WNB, _WR, LANES), BF16)],
        compiler_params=_cparams(("parallel",)), name="wattn_fwd",
    )(q, kp, vp, bias, sink)


def _wattn_bwd(q, kp, vp, bias, sink, o, do, lse):
    t = q.shape[0]
    nq = t // _WQ

    def body(q_ref, k_ref, v_ref, b_ref, sk_ref, o_ref, do_ref, lse_ref, dq_ref, dk_ref, dv_ref, db_ref, dsk_ref, qs, dos):
        @pl.when(pl.program_id(0) == 0)
        def _():
            dk_ref[...] = jnp.zeros_like(dk_ref)
            dv_ref[...] = jnp.zeros_like(dv_ref)
            db_ref[...] = jnp.zeros_like(db_ref)
            dsk_ref[...] = jnp.zeros_like(dsk_ref)

        dbias = jnp.zeros((_WR, _WK), F32)
        dsink = jnp.zeros((_WR, 1), F32)
        parts = []
        for b in range(_WNB):
            n = pl.program_id(0) * _WNB + b
            rows = pl.ds(b * _WQ, _WQ)
            _stack_heads(q_ref.at[rows, :], qs.at[b], _WQ)
            _stack_heads(do_ref.at[rows, :], dos.at[b], _WQ)
            delta = _stacked_delta(do_ref.at[rows, :], o_ref.at[rows, :], _WQ)
            off = pl.multiple_of(n * _WQ, _WQ)
            kw = k_ref[pl.ds(off, _WK), :]
            vw = v_ref[pl.ds(off, _WK), :]
            lse_v = lse_ref[pl.ds(b * _WR, _WR), :]
            p = jnp.exp(_wattn_scores(qs.at[b], kw, b_ref, n, t) - lse_v)
            dp = _dot(dos[b], vw, _NT)
            ds = p * (dp - delta)
            dbias = dbias + ds
            dsink = dsink - jnp.exp(sk_ref[...] - lse_v) * delta
            dsb = ds.astype(BF16)
            _unstack_heads(_dot(dsb, kw, _NN), dq_ref.at[rows, :], _WQ)
            parts.append((off, _dot(dsb, qs[b], _TN), _dot(p, dos[b], _TN)))
        db_ref[...] += dbias
        dsk_ref[...] += dsink
        for off, dkw, dvw in parts:
            dk_ref[pl.ds(off, _WK), :] += dkw
            dv_ref[pl.ds(off, _WK), :] += dvw

    sd = jax.ShapeDtypeStruct
    tp = t + 2 * _WQ
    qb = _row_spec(_WNB * _WQ, Q_W)
    return pl.pallas_call(
        body, grid=(nq // _WNB,),
        in_specs=[qb, _par_spec(KV_W, tp), _par_spec(KV_W, tp), _par_spec(_WK, _WR), _par_spec(1, _WR),
                  qb, qb, _row_spec(_WNB * _WR, 1)],
        out_specs=[qb, _par_spec(KV_W, tp), _par_spec(KV_W, tp), _par_spec(_WK, _WR), _par_spec(1, _WR)],
        out_shape=[sd((t, Q_W), F32), sd((tp, KV_W), F32), sd((tp, KV_W), F32), sd((_WR, _WK), F32), sd((_WR, 1), F32)],
        scratch_shapes=[pltpu.VMEM((_WNB, _WR, LANES), BF16), pltpu.VMEM((_WNB, _WR, LANES), BF16)],
        compiler_params=_cparams(("arbitrary",)), name="wattn_bwd",
    )(q, kp, vp, bias, sink, o, do, lse)


def _bias_bucket_reduce(db0, db1, bucket):
    def body(a_ref, b_ref, bk_ref, o_ref):
        d = a_ref[...] + b_ref[...]
        bk = bk_ref[...]
        lane = lax.broadcasted_iota(jnp.int32, (1, LANES), 1)
        out = jnp.zeros((1, LANES), F32)
        for b in range(N_BUCKETS):
            tot = jnp.sum(jnp.sum(jnp.where(bk == b, d, 0.0), axis=-1, keepdims=True), axis=0, keepdims=True)
            out = out + jnp.where(lane == b, tot, 0.0)
        o_ref[...] = out

    hb = pl.BlockSpec((None, _WQ, _WK), lambda h: (h, 0, 0))
    return pl.pallas_call(
        body, grid=(8,), in_specs=[hb, hb, pl.BlockSpec((_WQ, _WK), lambda h: (0, 0))],
        out_specs=pl.BlockSpec((None, 1, LANES), lambda h: (h, 0, 0)),
        out_shape=jax.ShapeDtypeStruct((8, 1, LANES), F32),
        compiler_params=_cparams(("parallel",)), name="bias_bucket_reduce",
    )(db0.reshape(8, _WQ, _WK), db1.reshape(8, _WQ, _WK), bucket)


def _rope_tables(t):
    rows_n = t // GRID_W
    row = jnp.repeat(jnp.arange(rows_n, dtype=F32), GRID_W)
    col = jnp.tile(jnp.arange(GRID_W, dtype=F32), rows_n)
    half = HEAD_DIM // 2
    inv_freq = ROPE_THETA ** (-jnp.arange(0, half, 2, dtype=F32) / half)
    ang = jnp.concatenate([row[:, None] * inv_freq, col[:, None] * inv_freq], axis=-1)
    cos, sin = jnp.cos(ang), jnp.sin(ang)
    c64 = jnp.repeat(cos, 2, axis=-1)
    s64 = jnp.stack([-sin, sin], axis=-1).reshape(t, HEAD_DIM)
    return jnp.tile(c64, (1, 2)), jnp.tile(s64, (1, 2))


def _t5_bucket(rel):
    half = N_BUCKETS // 2
    max_exact = half // 2
    bucket = jnp.where(rel > 0, half, 0)
    rp = jnp.abs(rel)
    rpf = jnp.maximum(rp, 1).astype(jnp.float32)
    large = max_exact + (jnp.log(rpf / max_exact) / math.log(MAX_DISTANCE / max_exact)
                         * (half - max_exact)).astype(jnp.int32)
    large = jnp.minimum(large, half - 1)
    return bucket + jnp.where(rp < max_exact, rp, large)


def _window_tables(rel_bias):
    qpos = jnp.arange(_WQ, dtype=jnp.int32)
    kpos = jnp.arange(_WK, dtype=jnp.int32) - _WQ
    rel = kpos[None, :] - qpos[:, None]
    bucket = _t5_bucket(rel)
    bias = jnp.zeros((8, _WQ, _WK), F32)
    for b in range(N_BUCKETS):
        bias = jnp.where((bucket == b)[None], rel_bias[b][:, None, None], bias)
    bias = jnp.where((jnp.abs(rel) <= WINDOW)[None], bias, NEG)
    return bias.reshape(_WR, _WK), bucket


def _pad_rows(a):
    return jnp.pad(a, ((_WQ, _WQ), (0, 0)))


def _layer_fwd(x, p, tabs, gather=None):
    cos_t, sin_t, bias = tabs
    h = _mm_nn(x, p["win"], F32, "in_proj")
    qa, ka, va, qb, kb, vb = _prep_fwd(h, cos_t, sin_t, p["qn"], p["kn"])
    if gather is None:
        oa, lse_a = _gattn_fwd(qa, ka, va)
    else:
        oa, lse_a, *gathered = _gattn_fwd(qa, ka, va, gather=gather[0])
        p = gather[1](gathered)
    kbp, vbp = _pad_rows(kb), _pad_rows(vb)
    ob, lse_b = _wattn_fwd(qb, kbp, vbp, bias, p["sink"])
    y = _outnorm_fwd(oa, ob, p["ga"], p["gb"])
    z1, x1, x1b = _mm_reduce(y[None], p["wout"][None], _NN, F32, "out_proj", res=x, res_scale=ALPHA,
                             ln=(p["ln1g"], p["ln1b"]))
    gu = _mm_expand(x1b, p["wgu"], _NN, BF16, "gate_up_proj")
    hdn = _ffn_mid_fwd(gu, p["cw"])
    z2, x2, _ = _mm_reduce(hdn, p["wd"], _NN, F32, "down_proj", res=x1, res_scale=ALPHA, ln=(p["ln2g"], p["ln2b"]))
    saved = dict(x=x, h=h, qa=qa, ka=ka, va=va, qb=qb, kbp=kbp, vbp=vbp, oa=oa, ob=ob, lse_a=lse_a, lse_b=lse_b,
                 y=y, z1=z1, x1b=x1b, gu=gu, hdn=hdn, z2=z2)
    return x2, saved


def _block_grads(g, names=("w_in", "w_out", "w_gate", "w_up", "w_down")):
    make = dict(
        w_in=lambda: _col_blocks(_in_cols_to_pairs(g["win"], _from_pairs), IN_SH),
        w_out=lambda: _mix_rows_to_pairs(g["wout"], _from_pairs).reshape(N_SHARD, OUT_SH, D_MODEL),
        w_gate=lambda: g["wg"], w_up=lambda: g["wu"], w_down=lambda: g["wd"])
    return [make[n]() for n in names]


def _layer_bwd(dx2, p, s, tabs, layer=0, pending=None):
    cos_t, sin_t, bias = tabs
    t = dx2.shape[0]
    dz2, dz2b, dln2g, dln2b = _ln_bwd(dx2, s["z2"], p["ln2g"])
    dhdn = _mm_expand(dz2b, p["wd"], _NT, BF16, "down_dx")
    dwd = _mm_tn_blocks(s["hdn"], dz2b, "down_dw")
    dgu, stats = _ffn_mid_bwd(s["gu"], dhdn, p["cw"])
    dgu = dgu.reshape(2 * N_SHARD, t, FF_SH)
    dx1 = _mm_reduce(dgu, p["wgu"], _NT, F32, "gate_up_dx", res=dz2, res_scale=ALPHA)
    dwg = _mm_tn_blocks(s["x1b"], dgu, "gate_dw", blk=0)
    dwu = _mm_tn_blocks(s["x1b"], dgu, "up_dw", blk=1)
    dz1, dz1b, dln1g, dln1b = _ln_bwd(dx1, s["z1"], p["ln1g"])
    dy = _mm_nt(dz1b, p["wout"], F32, "out_dx")
    dwout = _mm_tn(s["y"], dz1b, "out_dw")
    doa, dob, dga, dgb = _outnorm_bwd(dy, s["oa"], s["ob"], p["ga"], p["gb"])
    slots = None
    if pending is None:
        dqa, dka, dva = _gattn_bwd(s["qa"], s["ka"], s["va"], s["oa"], doa, s["lse_a"])
    else:
        mine = _block_grads(dict(wout=dwout, wg=dwg, wu=dwu, wd=dwd), ("w_out", "w_gate", "w_up", "w_down"))
        todo = list(pending) + [(o + 1, layer, g) for o, g in enumerate(mine)]
        dqa, dka, dva, *slots = _gattn_bwd(s["qa"], s["ka"], s["va"], s["oa"], doa, s["lse_a"],
                                           scatter=([(o, l) for o, l, _ in todo], [g for _, _, g in todo]))
    dqb, dkbp, dvbp, dbias, dsink = _wattn_bwd(s["qb"], s["kbp"], s["vbp"], bias, p["sink"], s["ob"], dob, s["lse_b"])
    dkb = lax.slice_in_dim(dkbp, _WQ, _WQ + t, axis=0)
    dvb = lax.slice_in_dim(dvbp, _WQ, _WQ + t, axis=0)
    dh, dqn, dkn = _prep_bwd(s["h"], cos_t, sin_t, p["qn"], p["kn"], dqa, dka, dva, dqb, dkb, dvb)
    dx = _mm_nt(dh, p["win"], F32, "in_dx", res=dz1, res_scale=ALPHA)
    dwin = _mm_tn(s["x"], dh, "in_dw")
    grads = dict(win=dwin, wout=dwout, wg=dwg, wu=dwu, wd=dwd, stats=stats, qn=dqn, kn=dkn, ga=dga, gb=dgb,
                 ln1g=dln1g, ln1b=dln1b, ln2g=dln2g, ln2b=dln2b, bias=dbias, sink=dsink, slots=slots)
    return dx, grads


def _prep_layer_params(l, win, wout, wg, wu, wd, cw, q_norm, k_norm, sink, out_norm_a, out_norm_b, conv_b,
                       ln1_g, ln1_b, ln2_g, ln2_b):
    win_full = win.transpose(1, 0, 2).reshape(D_MODEL, IN_COLS)
    row = lambda v: v.reshape(1, -1)
    late = {}
    if wout is not None:
        late = dict(
            wout=_mix_rows_to_pairs(wout.reshape(D_MODEL, D_MODEL)), wgu=jnp.concatenate([wg, wu], axis=0), wd=wd,
            cw=jnp.pad(cw, ((0, 0), (0, 5), (0, 0)))
            + jnp.pad(conv_b[l].reshape(N_SHARD, 1, FF_SH), ((0, 0), (3, 4), (0, 0))))
    return dict(
        late, win=_in_cols_to_pairs(win_full),
        qn=row(jnp.tile(q_norm[l], 2)), kn=row(jnp.tile(k_norm[l], 2)),
        ga=row(_to_pairs(out_norm_a[l], 0)), gb=row(_to_pairs(out_norm_b[l], 0)),
        ln1g=row(ln1_g[l]), ln1b=row(ln1_b[l]), ln2g=row(ln2_g[l]), ln2b=row(ln2_b[l]),
        sink=jnp.repeat(sink[l], _WQ).reshape(_WR, 1))


def _local_step(x, tgt, params, rel_bias, gather=None, scatter=False):
    t = x.shape[0]
    cos_t, sin_t = _rope_tables(t)
    bias, bucket = _window_tables(rel_bias)
    tabs = (cos_t, sin_t, bias)
    saved = []
    for l in range(DEPTH):
        x, s = _layer_fwd(x, params[l], tabs, gather if l == 0 else None)
        saved.append(s)
    dx, loss = _loss_grad(x, tgt)
    grads = [None] * DEPTH
    for l in reversed(range(DEPTH)):
        pending = None
        if scatter and l == 0:
            pending = [(o, 1, g) for o, g in enumerate(_block_grads(grads[1]))]
        dx, grads[l] = _layer_bwd(dx, params[l], saved[l], tabs, l, pending)
    dbucket = _bias_bucket_reduce(grads[0]["bias"], grads[1]["bias"], bucket)
    return loss, dx, grads, dbucket


_ANY = pl.BlockSpec(memory_space=pl.ANY)
_MESH = pl.DeviceIdType.MESH


def _mesh_pos():
    return lax.axis_index("x"), lax.axis_index("y"), lax.axis_index("c")


def _other_chips(x, y):
    return [(1 - x, y), (x, 1 - y), (1 - x, 1 - y)]


class _Exchange:
    def __init__(self, local, sends, recvs):
        self.local, self.sends, self.recvs = local, sends, recvs

    def start(self):
        for cp in self.local + self.sends:
            cp.start()

    def wait(self):
        for cp in self.recvs:
            cp.wait_recv()
        for cp in self.sends:
            cp.wait_send()
        for cp in self.local:
            cp.wait()


def _exchange_sems(n):
    return [pltpu.SemaphoreType.DMA((n, 3)), pltpu.SemaphoreType.DMA((n, 3)), pltpu.SemaphoreType.DMA((n,))]


def _gather_exchange(ins, outs, send, recv, loc):
    x, y, c = _mesh_pos()
    me = 2 * x + y
    chips = _other_chips(x, y)

    def remote(i, k, block):
        px, py = chips[k]
        return pltpu.make_async_remote_copy(ins[i], outs[i].at[block], send.at[i, k], recv.at[i, k],
                                            device_id=(px, py, c), device_id_type=_MESH)

    n = len(ins)
    local = [pltpu.make_async_copy(ins[i], outs[i].at[me], loc.at[i]) for i in range(n)]
    sends = [remote(i, k, me) for i in range(n) for k in range(3)]
    recvs = [remote(i, k, 2 * chips[k][0] + chips[k][1]) for i in range(n) for k in range(3)]
    return _Exchange(local, sends, recvs)


def _scatter_exchange(items, ins, outs, send, recv, loc):
    x, y, c = _mesh_pos()
    me = 2 * x + y
    chips = _other_chips(x, y)

    def remote(j, k):
        o, l = items[j]
        px, py = chips[k]
        return pltpu.make_async_remote_copy(ins[j].at[2 * px + py], outs[o].at[k, l], send.at[j, k], recv.at[j, k],
                                            device_id=(px, py, c), device_id_type=_MESH)

    local = [pltpu.make_async_copy(ins[j].at[me], outs[o].at[3, l], loc.at[j]) for j, (o, l) in enumerate(items)]
    sends = [remote(j, k) for j in range(len(items)) for k in range(3)]
    return _Exchange(local, sends, sends)


def _gathered_shapes(shards):
    return [jax.ShapeDtypeStruct((N_SHARD,) + s.shape, s.dtype) for s in shards]


def _slot_shapes(blocks):
    return [jax.ShapeDtypeStruct((N_SHARD, DEPTH) + g.shape[1:], g.dtype) for g in blocks]


def _gather_shards(shards):
    n = len(shards)

    def body(*refs):
        ex = _gather_exchange(refs[:n], refs[n:2 * n], *refs[2 * n:])
        ex.start()
        ex.wait()

    return pl.pallas_call(
        body, in_specs=[_ANY] * n, out_specs=[_ANY] * n, out_shape=_gathered_shapes(shards),
        scratch_shapes=_exchange_sems(n), name="gather_weights",
    )(*shards)


def _scatter_into(items, grads, slots):
    n, ns = len(grads), len(slots)

    def body(*refs):
        ex = _scatter_exchange(items, refs[:n], refs[n + ns:n + 2 * ns], *refs[n + 2 * ns:])
        ex.start()
        ex.wait()

    return pl.pallas_call(
        body, in_specs=[_ANY] * (n + ns), out_specs=[_ANY] * ns,
        out_shape=[jax.ShapeDtypeStruct(s.shape, s.dtype) for s in slots],
        input_output_aliases={n + i: i for i in range(ns)},
        scratch_shapes=_exchange_sems(n), name="scatter_grads",
    )(*grads, *slots)


def _swap_with_sibling(parts):
    n = len(parts)

    def body(*refs):
        ins, outs = refs[:n], refs[n:2 * n]
        send, recv = refs[2 * n:]
        x, y, c = _mesh_pos()
        copies = [pltpu.make_async_remote_copy(ins[i], outs[i], send.at[i], recv.at[i], device_id=(x, y, 1 - c),
                                               device_id_type=_MESH) for i in range(n)]
        for cp in copies:
            cp.start()
        for cp in copies:
            cp.wait_recv()
        for cp in copies:
            cp.wait_send()

    return pl.pallas_call(
        body, in_specs=[_ANY] * n, out_specs=[_ANY] * n,
        out_shape=[jax.ShapeDtypeStruct(p.shape, p.dtype) for p in parts],
        scratch_shapes=[pltpu.SemaphoreType.DMA((n,)), pltpu.SemaphoreType.DMA((n,))],
        name="swap_sibling",
    )(*parts)


N_DEV = 8


def _allreduce_small(packed):
    rows = packed.shape[0]

    def body(in_ref, out_ref, buf, send, recv, loc):
        x, y, c = _mesh_pos()
        me = 4 * x + 2 * y + c
        own = pltpu.make_async_copy(in_ref, buf.at[me], loc)
        own.start()

        def remote(m, block):
            peer = (x ^ (m >> 2), y ^ ((m >> 1) & 1), c ^ (m & 1))
            return pltpu.make_async_remote_copy(in_ref, buf.at[block], send.at[m - 1], recv.at[m - 1],
                                                device_id=peer, device_id_type=_MESH)

        sends = [remote(m, me) for m in range(1, N_DEV)]
        for cp in sends:
            cp.start()
        for m in range(1, N_DEV):
            remote(m, me ^ m).wait_recv()
        for cp in sends:
            cp.wait_send()
        own.wait()
        tot = buf[0]
        for d in range(1, N_DEV):
            tot = tot + buf[d]
        out_ref[...] = tot

    vm = pl.BlockSpec(memory_space=pltpu.VMEM)
    return pl.pallas_call(
        body, in_specs=[vm], out_specs=vm, out_shape=jax.ShapeDtypeStruct((rows, LANES), F32),
        scratch_shapes=[pltpu.VMEM((N_DEV, rows, LANES), F32), pltpu.SemaphoreType.DMA((N_DEV - 1,)),
                        pltpu.SemaphoreType.DMA((N_DEV - 1,)), pltpu.SemaphoreType.DMA(())],
        name="allreduce_small",
    )(packed)


def _shard_rows(r):
    return r // 2 if r % 32 == 0 else r


def _sum_slots(slots):
    _, _, r, cdim = slots.shape
    tr = _shard_rows(r)

    def body(a_ref, b_ref, c_ref, d_ref, o_ref):
        up = lambda ref: ref[...].astype(F32)
        o_ref[...] = ((up(d_ref) + up(a_ref)) + up(b_ref)) + up(c_ref)

    def spec(k):
        return pl.BlockSpec((None, None, tr, cdim), lambda l, i: (k, l, i, 0))

    return pl.pallas_call(
        body, grid=(DEPTH, r // tr), in_specs=[spec(0), spec(1), spec(2), spec(3)],
        out_specs=pl.BlockSpec((None, tr, cdim), lambda l, i: (l, i, 0)),
        out_shape=jax.ShapeDtypeStruct((DEPTH, r, cdim), F32),
        compiler_params=_cparams(("parallel", "parallel")), name="sum_slots",
    )(slots, slots, slots, slots)


def _adamw_math(w, g, m, v):
    m = ADAM_B1 * m + (1.0 - ADAM_B1) * g
    v = ADAM_B2 * v + (1.0 - ADAM_B2) * (g * g)
    m_hat = m / (1.0 - ADAM_B1 ** ADAM_STEP)
    v_hat = v / (1.0 - ADAM_B2 ** ADAM_STEP)
    delta = -ADAM_LR * (m_hat / (jnp.sqrt(v_hat) + ADAM_EPS) + ADAM_WD * w)
    return delta, m, v


def _adamw_big(ga, gb, w, m, v):
    _, r, cdim = w.shape
    tr = _shard_rows(r)

    def body(ga_ref, gb_ref, w_ref, m_ref, v_ref, g_out, d_out, m_out, v_out):
        g = ga_ref[...] + gb_ref[...]
        d, mn, vn = _adamw_math(w_ref[...], g, m_ref[...], v_ref[...])
        g_out[...] = g
        d_out[...] = d
        m_out[...] = mn
        v_out[...] = vn

    spec = pl.BlockSpec((None, tr, cdim), lambda l, i: (l, i, 0))
    shp = jax.ShapeDtypeStruct(w.shape, F32)
    return pl.pallas_call(
        body, grid=(DEPTH, r // tr), in_specs=[spec] * 5, out_specs=[spec] * 4, out_shape=[shp] * 4,
        compiler_params=_cparams(("parallel", "parallel")), name="adamw_big",
    )(ga, gb, w, m, v)


def _adamw_small(ws, gs, ms, vs):
    n = len(ws)

    def body(*refs):
        w_r, g_r, m_r, v_r = (refs[k * n:(k + 1) * n] for k in range(4))
        d_o, m_o, v_o = (refs[(4 + k) * n:(5 + k) * n] for k in range(3))
        for i in range(n):
            d, mn, vn = _adamw_math(w_r[i][...], g_r[i][...], m_r[i][...], v_r[i][...])
            d_o[i][...] = d
            m_o[i][...] = mn
            v_o[i][...] = vn

    vm = pl.BlockSpec(memory_space=pltpu.VMEM)
    shp = [jax.ShapeDtypeStruct(w.shape, F32) for w in ws]
    outs = pl.pallas_call(
        body, in_specs=[vm] * (4 * n), out_specs=[vm] * (3 * n), out_shape=shp * 3, name="adamw_small",
    )(*ws, *gs, *ms, *vs)
    return outs[:n], outs[n:2 * n], outs[2 * n:]


def _tile_rows(a):
    a = a.reshape(-1, LANES)
    pad = (-a.shape[0]) % 8
    return jnp.pad(a, ((0, pad), (0, 0))) if pad else a


_SMALL_LAYER_PARTS = (("qn", 8), ("kn", 8), ("sink", 8), ("ga", 8), ("gb", 8), ("ln1g", 8), ("ln1b", 8),
                      ("ln2g", 8), ("ln2b", 8), ("stats", N_SHARD * 8 * FF_SH // LANES))
_SMALL_HEAD_ROWS = 16
_SMALL_LAYER_ROWS = sum(r for _, r in _SMALL_LAYER_PARTS)


def _pack_small(loss, dbucket, grads):
    parts = [_tile_rows(loss), _tile_rows(dbucket)]
    for l in range(DEPTH):
        parts += [_tile_rows(grads[l][name]) for name, _ in _SMALL_LAYER_PARTS]
    return jnp.concatenate(parts, axis=0)


def _unpack_small(tot, chip):
    out = dict(loss=tot[0, 0], rel_bias=tot[8:16, :N_BUCKETS].T)
    per = {name: [] for name, _ in _SMALL_LAYER_PARTS}
    for l in range(DEPTH):
        base = _SMALL_HEAD_ROWS + l * _SMALL_LAYER_ROWS
        for name, rows in _SMALL_LAYER_PARTS:
            per[name].append(tot[base:base + rows])
            base += rows
    fold = lambda v: v[0, :HEAD_DIM] + v[0, HEAD_DIM:]
    out["q_norm"] = jnp.stack([fold(v) for v in per["qn"]])
    out["k_norm"] = jnp.stack([fold(v) for v in per["kn"]])
    out["sink"] = jnp.stack([jnp.sum(v, axis=1) for v in per["sink"]])
    out["out_norm_a"] = jnp.stack([_from_pairs(v[:4].reshape(Q_W), 0) for v in per["ga"]])
    out["out_norm_b"] = jnp.stack([_from_pairs(v[:4].reshape(Q_W), 0) for v in per["gb"]])
    for name, key in (("ln1_g", "ln1g"), ("ln1_b", "ln1b"), ("ln2_g", "ln2g"), ("ln2_b", "ln2b")):
        out[name] = jnp.stack([v.reshape(D_MODEL) for v in per[key]])
    stats = [v.reshape(N_SHARD, 8, FF_SH) for v in per["stats"]]
    out["conv_b"] = jnp.stack([s[:, 0, :].reshape(D_FF) for s in stats])
    out["conv_w"] = jnp.stack([lax.dynamic_index_in_dim(s, chip, 0, keepdims=False)[1:4] for s in stats])
    return out


_WEIGHTS = ("rel_bias", "w_in", "q_norm", "k_norm", "sink", "out_norm_a", "out_norm_b", "w_out", "ln1_g", "ln1_b",
            "w_gate", "w_up", "conv_w", "conv_b", "w_down", "ln2_g", "ln2_b")
_BIG = ("w_in", "w_out", "w_gate", "w_up", "w_down")
_SMALL = tuple(n for n in _WEIGHTS if n not in _BIG)


def _col_blocks(g, n):
    return g.reshape(g.shape[0], N_SHARD, n).transpose(1, 0, 2)


def kernel(x, rel_bias, w_in, q_norm, k_norm, sink, out_norm_a, out_norm_b, w_out, ln1_g, ln1_b, w_gate, w_up, conv_w, conv_b, w_down, ln2_g, ln2_b, loss_target, m_rel_bias, m_w_in, m_q_norm, m_k_norm, m_sink, m_out_norm_a, m_out_norm_b, m_w_out, m_ln1_g, m_ln1_b, m_w_gate, m_w_up, m_conv_w, m_conv_b, m_w_down, m_ln2_g, m_ln2_b, v_rel_bias, v_w_in, v_q_norm, v_k_norm, v_sink, v_out_norm_a, v_out_norm_b, v_w_out, v_ln1_g, v_ln1_b, v_w_gate, v_w_up, v_conv_w, v_conv_b, v_w_down, v_ln2_g, v_ln2_b):
    w = dict(rel_bias=rel_bias, w_in=w_in, q_norm=q_norm, k_norm=k_norm, sink=sink, out_norm_a=out_norm_a,
             out_norm_b=out_norm_b, w_out=w_out, ln1_g=ln1_g, ln1_b=ln1_b, w_gate=w_gate, w_up=w_up, conv_w=conv_w,
             conv_b=conv_b, w_down=w_down, ln2_g=ln2_g, ln2_b=ln2_b)
    m = dict(rel_bias=m_rel_bias, w_in=m_w_in, q_norm=m_q_norm, k_norm=m_k_norm, sink=m_sink, out_norm_a=m_out_norm_a,
             out_norm_b=m_out_norm_b, w_out=m_w_out, ln1_g=m_ln1_g, ln1_b=m_ln1_b, w_gate=m_w_gate, w_up=m_w_up,
             conv_w=m_conv_w, conv_b=m_conv_b, w_down=m_w_down, ln2_g=m_ln2_g, ln2_b=m_ln2_b)
    v = dict(rel_bias=v_rel_bias, w_in=v_w_in, q_norm=v_q_norm, k_norm=v_k_norm, sink=v_sink, out_norm_a=v_out_norm_a,
             out_norm_b=v_out_norm_b, w_out=v_w_out, ln1_g=v_ln1_g, ln1_b=v_ln1_b, w_gate=v_w_gate, w_up=v_w_up,
             conv_w=v_conv_w, conv_b=v_conv_b, w_down=v_w_down, ln2_g=v_ln2_g, ln2_b=v_ln2_b)
    chip = 2 * lax.axis_index("x") + lax.axis_index("y")

    small_w = (q_norm, k_norm, sink, out_norm_a, out_norm_b, conv_b, ln1_g, ln1_b, ln2_g, ln2_b)
    (win0,) = _gather_shards([w_in[0].astype(BF16)])
    later = ([w[name][0].astype(BF16) for name in _BIG[1:]] + [w[name][1].astype(BF16) for name in _BIG] + [conv_w])
    params = [_prep_layer_params(0, win0, None, None, None, None, None, *small_w), None]

    def finish(g):
        wout0, wg0, wu0, wd0, win1, wout1, wg1, wu1, wd1, cw_all = g
        params[0] = _prep_layer_params(0, win0, wout0, wg0, wu0, wd0, cw_all[:, 0], *small_w)
        params[1] = _prep_layer_params(1, win1, wout1, wg1, wu1, wd1, cw_all[:, 1], *small_w)
        return params[0]

    loss, dx, grads, dbucket = _local_step(x[0], loss_target[0], params, rel_bias, gather=(later, finish),
                                           scatter=True)

    small = _unpack_small(_allreduce_small(_pack_small(loss, dbucket, grads)), chip)

    slots = list(grads[0]["slots"])
    slots[0] = _scatter_into([(0, 0)], _block_grads(grads[0], ("w_in",)), [slots[0]])[0]
    partial = [_sum_slots(s) for s in slots]
    other = _swap_with_sibling(partial)

    grad, delta, new_m, new_v = {}, {}, {}, {}
    for i, name in enumerate(_BIG):
        grad[name], delta[name], new_m[name], new_v[name] = _adamw_big(partial[i], other[i], w[name], m[name], v[name])
    flat2 = lambda a: a.reshape(-1, a.shape[-1])
    ds, ms, vs = _adamw_small([flat2(w[n]) for n in _SMALL], [flat2(small[n]) for n in _SMALL],
                              [flat2(m[n]) for n in _SMALL], [flat2(v[n]) for n in _SMALL])
    for i, name in enumerate(_SMALL):
        grad[name] = small[name]
        delta[name] = ds[i].reshape(w[name].shape)
        new_m[name] = ms[i].reshape(w[name].shape)
        new_v[name] = vs[i].reshape(w[name].shape)

    return (small["loss"], dx[None], *[grad[n] for n in _WEIGHTS], *[delta[n] for n in _WEIGHTS],
            *[new_m[n] for n in _WEIGHTS], *[new_v[n] for n in _WEIGHTS])
```

```python
import functools
import math

import numpy as np
import jax
import jax.numpy as jnp
from jax import lax
from jax.experimental import pallas as pl
from jax.experimental.pallas import tpu as pltpu

F32 = jnp.float32
BF16 = jnp.bfloat16

D_MODEL = 1024
DEPTH = 2
HEAD_DIM = 64
Q_W = 512
KV_W = 128
IN_COLS = 2 * (Q_W + 2 * KV_W)
N_SHARD = 4
IN_SH = IN_COLS // N_SHARD
OUT_SH = D_MODEL // N_SHARD
D_FF = 2816
FF_SH = D_FF // N_SHARD
Q_BLOCK = 128
WINDOW = 128
N_BUCKETS = 32
MAX_DISTANCE = 128
GRID_W = 64
ROPE_THETA = 10000.0
ALPHA = (2.0 * DEPTH) ** 0.25
RMS_EPS = 1e-6
LN_EPS = 1e-5
NEG = -1e30
LANES = 128
VMEM_LIMIT = 56 * 1024 * 1024

ADAM_LR = 0.001
ADAM_B1 = 0.9
ADAM_B2 = 0.999
ADAM_EPS = 1e-08
ADAM_WD = 0.01
ADAM_STEP = 10

_NN = (((1,), (0,)), ((), ()))
_NT = (((1,), (1,)), ((), ()))
_TN = (((0,), (0,)), ((), ()))


def _dot(a, b, dims):
    return lax.dot_general(a.astype(BF16), b.astype(BF16), dims, preferred_element_type=F32)


def _cparams(sem, vmem=VMEM_LIMIT):
    return pltpu.CompilerParams(dimension_semantics=sem, vmem_limit_bytes=vmem)


def _regroup(a, axis, n_outer, n_inner):
    shp = a.shape
    a = a.reshape(shp[:axis] + (n_outer, n_inner, HEAD_DIM) + shp[axis + 1:])
    return jnp.swapaxes(a, axis, axis + 1).reshape(shp)


def _to_pairs(a, axis):
    return _regroup(a, axis, 2, 4)


def _from_pairs(a, axis):
    return _regroup(a, axis, 4, 2)


def _in_cols_to_pairs(w, fn=_to_pairs):
    return jnp.concatenate([fn(w[..., :Q_W], w.ndim - 1), w[..., Q_W:Q_W + 2 * KV_W],
                            fn(w[..., Q_W + 2 * KV_W:2 * Q_W + 2 * KV_W], w.ndim - 1),
                            w[..., 2 * Q_W + 2 * KV_W:]], axis=-1)


def _mix_rows_to_pairs(w, fn=_to_pairs):
    return fn(w.reshape(2, Q_W, w.shape[-1]), 1).reshape(w.shape)


def _matmul(a, b, *, dims, grid, a_spec, b_spec, o_spec, out_shape, acc_shape, name, res=None,
            res_spec=None, res_scale=1.0):
    nk = grid[-1]
    kax = len(grid) - 1

    def body(*refs):
        if res is None:
            a_ref, b_ref, o_ref, acc = refs
            r_ref = None
        else:
            a_ref, b_ref, r_ref, o_ref, acc = refs
        k = pl.program_id(kax)

        @pl.when(k == 0)
        def _():
            acc[...] = jnp.zeros_like(acc)

        acc[...] += _dot(a_ref[...], b_ref[...], dims)

        @pl.when(k == nk - 1)
        def _():
            o = acc[...]
            if r_ref is not None:
                o = o + res_scale * r_ref[...]
            o_ref[...] = o.astype(o_ref.dtype)

    in_specs = [a_spec, b_spec] + ([res_spec] if res is not None else [])
    args = (a, b) + ((res,) if res is not None else ())
    sem = ("parallel",) * kax + ("arbitrary",)
    return pl.pallas_call(
        body, grid=grid, in_specs=in_specs, out_specs=o_spec, out_shape=out_shape,
        scratch_shapes=[pltpu.VMEM(acc_shape, F32)], compiler_params=_cparams(sem), name=name,
    )(*args)


def _mm_nn(a, b, out_dtype, name, tm=512, res=None, res_scale=1.0):
    m, kd = a.shape
    n = b.shape[1]
    return _matmul(
        a, b, dims=_NN, grid=(m // tm, 1),
        a_spec=pl.BlockSpec((tm, kd), lambda i, k: (i, 0)),
        b_spec=pl.BlockSpec((kd, n), lambda i, k: (0, 0)),
        o_spec=pl.BlockSpec((tm, n), lambda i, k: (i, 0)),
        out_shape=jax.ShapeDtypeStruct((m, n), out_dtype), acc_shape=(tm, n), name=name,
        res=res, res_spec=pl.BlockSpec((tm, n), lambda i, k: (i, 0)), res_scale=res_scale)


def _mm_nt(a, b, out_dtype, name, tm=512, res=None, res_scale=1.0):
    m, kd = a.shape
    n = b.shape[0]
    return _matmul(
        a, b, dims=_NT, grid=(m // tm, 1),
        a_spec=pl.BlockSpec((tm, kd), lambda i, k: (i, 0)),
        b_spec=pl.BlockSpec((n, kd), lambda i, k: (0, 0)),
        o_spec=pl.BlockSpec((tm, n), lambda i, k: (i, 0)),
        out_shape=jax.ShapeDtypeStruct((m, n), out_dtype), acc_shape=(tm, n), name=name,
        res=res, res_spec=pl.BlockSpec((tm, n), lambda i, k: (i, 0)), res_scale=res_scale)


def _mm_tn(a, b, name, tk=512, tn=None, out_dtype=BF16):
    t, m = a.shape
    n = b.shape[1]
    tn = n if tn is None else tn
    return _matmul(
        a, b, dims=_TN, grid=(n // tn, t // tk),
        a_spec=pl.BlockSpec((tk, m), lambda j, k: (k, 0)),
        b_spec=pl.BlockSpec((tk, tn), lambda j, k: (k, j)),
        o_spec=pl.BlockSpec((m, tn), lambda j, k: (0, j)),
        out_shape=jax.ShapeDtypeStruct((m, n), out_dtype), acc_shape=(m, tn), name=name)


def _blocked_n(w, dims):
    return w.shape[2] if dims == _NN else w.shape[1]


def _mm_expand(a, w, dims, out_dtype, name, tm=512):
    m, kd = a.shape
    nb, n = w.shape[0], _blocked_n(w, dims)

    def body(a_ref, w_ref, o_ref):
        av = a_ref[...]
        for j in range(nb):
            o_ref[j] = _dot(av, w_ref[j], dims).astype(o_ref.dtype)

    return pl.pallas_call(
        body, grid=(m // tm,),
        in_specs=[pl.BlockSpec((tm, kd), lambda i: (i, 0)), pl.BlockSpec(w.shape, lambda i: (0, 0, 0))],
        out_specs=pl.BlockSpec((nb, tm, n), lambda i: (0, i, 0)),
        out_shape=jax.ShapeDtypeStruct((nb, m, n), out_dtype),
        compiler_params=_cparams(("parallel",)), name=name,
    )(a, w)


def _mm_reduce(a, w, dims, out_dtype, name, tm=512, res=None, res_scale=1.0, ln=None):
    nb, m, kd = a.shape
    n = _blocked_n(w, dims)
    n_in = 2 + (res is not None) + (2 if ln else 0)

    def body(*refs):
        a_ref, w_ref = refs[0], refs[1]
        acc = _dot(a_ref[0], w_ref[0], dims)
        for j in range(1, nb):
            acc = acc + _dot(a_ref[j], w_ref[j], dims)
        if res is not None:
            acc = acc + res_scale * refs[2][...]
        refs[n_in][...] = acc.astype(out_dtype)
        if ln:
            g_ref, b_ref = refs[n_in - 2], refs[n_in - 1]
            zc = acc - jnp.mean(acc, axis=-1, keepdims=True)
            r = lax.rsqrt(jnp.mean(zc * zc, axis=-1, keepdims=True) + LN_EPS)
            y = zc * r * g_ref[...] + b_ref[...]
            refs[n_in + 1][...] = y
            refs[n_in + 2][...] = y.astype(BF16)

    row = pl.BlockSpec((tm, n), lambda i: (i, 0))
    par = pl.BlockSpec((1, n), lambda i: (0, 0))
    sd = jax.ShapeDtypeStruct
    out = pl.pallas_call(
        body, grid=(m // tm,),
        in_specs=[pl.BlockSpec((nb, tm, kd), lambda i: (0, i, 0)), pl.BlockSpec(w.shape, lambda i: (0, 0, 0))]
        + ([row] if res is not None else []) + ([par, par] if ln else []),
        out_specs=[row] * (3 if ln else 1),
        out_shape=[sd((m, n), out_dtype)] + ([sd((m, n), F32), sd((m, n), BF16)] if ln else []),
        compiler_params=_cparams(("parallel",)), name=name,
    )(a, w, *((res,) if res is not None else ()), *(ln or ()))
    return out if ln else out[0]


def _mm_tn_blocks(a, b, name, blk=0, nb=N_SHARD, tk=512, out_dtype=BF16):
    a3, b3 = a.ndim == 3, b.ndim == 3
    t, m, n = a.shape[-2], a.shape[-1], b.shape[-1]
    nsteps = t // tk

    def spec(blocked, width):
        if blocked:
            return pl.BlockSpec((nb, tk, width), lambda k: (blk, k, 0))
        return pl.BlockSpec((tk, width), lambda k: (k, 0))

    def body(a_ref, b_ref, o_ref, acc):
        k = pl.program_id(0)

        @pl.when(k == 0)
        def _():
            acc[...] = jnp.zeros_like(acc)

        for j in range(nb):
            acc[j] += _dot(a_ref[j] if a3 else a_ref[...], b_ref[j] if b3 else b_ref[...], _TN)

        @pl.when(k == nsteps - 1)
        def _():
            o_ref[...] = acc[...].astype(o_ref.dtype)

    return pl.pallas_call(
        body, grid=(nsteps,), in_specs=[spec(a3, m), spec(b3, n)],
        out_specs=pl.BlockSpec((nb, m, n), lambda k: (0, 0, 0)),
        out_shape=jax.ShapeDtypeStruct((nb, m, n), out_dtype),
        scratch_shapes=[pltpu.VMEM((nb, m, n), F32)],
        compiler_params=_cparams(("arbitrary",)), name=name,
    )(a, b)


def _row_spec(tm, n):
    return pl.BlockSpec((tm, n), lambda i: (i, 0))


def _par_spec(n, rows=1):
    return pl.BlockSpec((rows, n), lambda i: (0, 0))


def _swap_pairs(x):
    lane = lax.broadcasted_iota(jnp.int32, x.shape, 1)
    return jnp.where(lane % 2 == 0, pltpu.roll(x, LANES - 1, 1), pltpu.roll(x, 1, 1))


def _head_sums(v):
    lo = lax.broadcasted_iota(jnp.int32, v.shape, 1) < HEAD_DIM
    s_lo = jnp.sum(jnp.where(lo, v, 0.0), axis=-1, keepdims=True)
    s_hi = jnp.sum(jnp.where(lo, 0.0, v), axis=-1, keepdims=True)
    return jnp.where(lo, s_lo, s_hi)


def _qk_blocks():
    return [(128 * i, True) for i in range(4)] + [(Q_W, False)]


def _prep_fwd(h, cos_t, sin_t, qn, kn, tm=256):
    t = h.shape[0]
    scale = HEAD_DIM ** -0.5

    def body(h_ref, c_ref, s_ref, qn_ref, kn_ref, qa_ref, ka_ref, va_ref, qb_ref, kb_ref, vb_ref):
        c = c_ref[...]
        s = s_ref[...]
        for start, is_q in _qk_blocks():
            x = h_ref[:, start:start + LANES]
            r = lax.rsqrt(_head_sums(x * x) * (1.0 / HEAD_DIM) + RMS_EPS)
            y = x * r * (qn_ref[...] if is_q else kn_ref[...])
            y = y * c + _swap_pairs(y) * s
            if is_q:
                qa_ref[:, start:start + LANES] = (y * scale).astype(BF16)
            else:
                ka_ref[...] = y.astype(BF16)
        va_ref[...] = h_ref[:, 640:768].astype(BF16)
        qb_ref[...] = (h_ref[:, 768:1280] * scale).astype(BF16)
        kb_ref[...] = h_ref[:, 1280:1408].astype(BF16)
        vb_ref[...] = h_ref[:, 1408:1536].astype(BF16)

    sd = jax.ShapeDtypeStruct
    return pl.pallas_call(
        body, grid=(t // tm,),
        in_specs=[_row_spec(tm, IN_COLS), _row_spec(tm, LANES), _row_spec(tm, LANES), _par_spec(LANES), _par_spec(LANES)],
        out_specs=[_row_spec(tm, Q_W), _row_spec(tm, KV_W), _row_spec(tm, KV_W),
                   _row_spec(tm, Q_W), _row_spec(tm, KV_W), _row_spec(tm, KV_W)],
        out_shape=[sd((t, Q_W), BF16), sd((t, KV_W), BF16), sd((t, KV_W), BF16),
                   sd((t, Q_W), BF16), sd((t, KV_W), BF16), sd((t, KV_W), BF16)],
        compiler_params=_cparams(("parallel",)), name="prep_fwd",
    )(h, cos_t, sin_t, qn, kn)


def _prep_bwd(h, cos_t, sin_t, qn, kn, dqa, dka, dva, dqb, dkb, dvb, tm=256):
    t = h.shape[0]
    scale = HEAD_DIM ** -0.5

    def body(h_ref, c_ref, s_ref, qn_ref, kn_ref, dqa_ref, dka_ref, dva_ref, dqb_ref, dkb_ref, dvb_ref,
             dh_ref, dqn_ref, dkn_ref):
        @pl.when(pl.program_id(0) == 0)
        def _():
            dqn_ref[...] = jnp.zeros_like(dqn_ref)
            dkn_ref[...] = jnp.zeros_like(dkn_ref)

        c = c_ref[...]
        s = s_ref[...]
        for start, is_q in _qk_blocks():
            x = h_ref[:, start:start + LANES]
            gain = qn_ref[...] if is_q else kn_ref[...]
            d = dqa_ref[:, start:start + LANES] * scale if is_q else dka_ref[...]
            dy = d * c + _swap_pairs(d * s)
            r = lax.rsqrt(_head_sums(x * x) * (1.0 / HEAD_DIM) + RMS_EPS)
            xr = x * r
            gsum = jnp.sum(dy * xr, axis=0, keepdims=True)
            if is_q:
                dqn_ref[...] += gsum
            else:
                dkn_ref[...] += gsum
            gy = dy * gain
            dx = r * (gy - xr * (_head_sums(xr * gy) * (1.0 / HEAD_DIM)))
            dh_ref[:, start:start + LANES] = dx.astype(BF16)
        dh_ref[:, 640:768] = dva_ref[...].astype(BF16)
        dh_ref[:, 768:1280] = (dqb_ref[...] * scale).astype(BF16)
        dh_ref[:, 1280:1408] = dkb_ref[...].astype(BF16)
        dh_ref[:, 1408:1536] = dvb_ref[...].astype(BF16)

    sd = jax.ShapeDtypeStruct
    return pl.pallas_call(
        body, grid=(t // tm,),
        in_specs=[_row_spec(tm, IN_COLS), _row_spec(tm, LANES), _row_spec(tm, LANES), _par_spec(LANES), _par_spec(LANES),
                  _row_spec(tm, Q_W), _row_spec(tm, KV_W), _row_spec(tm, KV_W),
                  _row_spec(tm, Q_W), _row_spec(tm, KV_W), _row_spec(tm, KV_W)],
        out_specs=[_row_spec(tm, IN_COLS), _par_spec(LANES), _par_spec(LANES)],
        out_shape=[sd((t, IN_COLS), BF16), sd((1, LANES), F32), sd((1, LANES), F32)],
        compiler_params=_cparams(("arbitrary",)), name="prep_bwd",
    )(h, cos_t, sin_t, qn, kn, dqa, dka, dva, dqb, dkb, dvb)


def _outnorm_fwd(oa, ob, ga, gb, tm=512):
    t = oa.shape[0]

    def body(oa_ref, ob_ref, ga_ref, gb_ref, y_ref):
        for o_ref, g_ref, start in ((oa_ref, ga_ref, 0), (ob_ref, gb_ref, Q_W)):
            x = o_ref[...]
            r = lax.rsqrt(jnp.mean(x * x, axis=-1, keepdims=True) + RMS_EPS)
            y_ref[:, start:start + Q_W] = (x * r * g_ref[...]).astype(BF16)

    return pl.pallas_call(
        body, grid=(t // tm,),
        in_specs=[_row_spec(tm, Q_W), _row_spec(tm, Q_W), _par_spec(Q_W), _par_spec(Q_W)],
        out_specs=_row_spec(tm, D_MODEL), out_shape=jax.ShapeDtypeStruct((t, D_MODEL), BF16),
        compiler_params=_cparams(("parallel",)), name="outnorm_fwd",
    )(oa, ob, ga, gb)


def _outnorm_bwd(dy, oa, ob, ga, gb, tm=512):
    t = oa.shape[0]

    def body(dy_ref, oa_ref, ob_ref, ga_ref, gb_ref, doa_ref, dob_ref, dga_ref, dgb_ref):
        @pl.when(pl.program_id(0) == 0)
        def _():
            dga_ref[...] = jnp.zeros_like(dga_ref)
            dgb_ref[...] = jnp.zeros_like(dgb_ref)

        for o_ref, g_ref, do_ref, dg_ref, start in ((oa_ref, ga_ref, doa_ref, dga_ref, 0),
                                                    (ob_ref, gb_ref, dob_ref, dgb_ref, Q_W)):
            x = o_ref[...]
            d = dy_ref[:, start:start + Q_W]
            r = lax.rsqrt(jnp.mean(x * x, axis=-1, keepdims=True) + RMS_EPS)
            xr = x * r
            dg_ref[...] += jnp.sum(d * xr, axis=0, keepdims=True)
            gy = d * g_ref[...]
            do_ref[...] = r * (gy - xr * jnp.mean(xr * gy, axis=-1, keepdims=True))

    sd = jax.ShapeDtypeStruct
    return pl.pallas_call(
        body, grid=(t // tm,),
        in_specs=[_row_spec(tm, D_MODEL), _row_spec(tm, Q_W), _row_spec(tm, Q_W), _par_spec(Q_W), _par_spec(Q_W)],
        out_specs=[_row_spec(tm, Q_W), _row_spec(tm, Q_W), _par_spec(Q_W), _par_spec(Q_W)],
        out_shape=[sd((t, Q_W), F32), sd((t, Q_W), F32), sd((1, Q_W), F32), sd((1, Q_W), F32)],
        compiler_params=_cparams(("arbitrary",)), name="outnorm_bwd",
    )(dy, oa, ob, ga, gb)


def _ln_bwd(d, z, g, tm=512):
    t = z.shape[0]

    def body(d_ref, z_ref, g_ref, dz_ref, dzb_ref, dg_ref, db_ref):
        @pl.when(pl.program_id(0) == 0)
        def _():
            dg_ref[...] = jnp.zeros_like(dg_ref)
            db_ref[...] = jnp.zeros_like(db_ref)

        zz = z_ref[...]
        dd = d_ref[...]
        mu = jnp.mean(zz, axis=-1, keepdims=True)
        zc = zz - mu
        r = lax.rsqrt(jnp.mean(zc * zc, axis=-1, keepdims=True) + LN_EPS)
        xh = zc * r
        dg_ref[...] += jnp.sum(dd * xh, axis=0, keepdims=True)
        db_ref[...] += jnp.sum(dd, axis=0, keepdims=True)
        dxh = dd * g_ref[...]
        dz = r * (dxh - jnp.mean(dxh, axis=-1, keepdims=True) - xh * jnp.mean(dxh * xh, axis=-1, keepdims=True))
        dz_ref[...] = dz
        dzb_ref[...] = dz.astype(BF16)

    sd = jax.ShapeDtypeStruct
    return pl.pallas_call(
        body, grid=(t // tm,),
        in_specs=[_row_spec(tm, D_MODEL), _row_spec(tm, D_MODEL), _par_spec(D_MODEL)],
        out_specs=[_row_spec(tm, D_MODEL), _row_spec(tm, D_MODEL), _par_spec(D_MODEL), _par_spec(D_MODEL)],
        out_shape=[sd((t, D_MODEL), F32), sd((t, D_MODEL), BF16), sd((1, D_MODEL), F32), sd((1, D_MODEL), F32)],
        compiler_params=_cparams(("arbitrary",)), name="ln_bwd",
    )(d, z, g)


def _loss_grad(y, tgt, tm=512):
    t = y.shape[0]
    nsteps = t // tm

    def body(y_ref, t_ref, dy_ref, loss_ref, acc):
        i = pl.program_id(0)

        @pl.when(i == 0)
        def _():
            acc[...] = jnp.zeros_like(acc)

        e = y_ref[...] - t_ref[...]
        dy_ref[...] = e * (1.0 / D_MODEL)
        acc[...] += jnp.sum(e * e, axis=0, keepdims=True)

        @pl.when(i == nsteps - 1)
        def _():
            tot = jnp.sum(acc[...], axis=-1, keepdims=True) * (0.5 / D_MODEL)
            loss_ref[...] = jnp.broadcast_to(tot, loss_ref.shape)

    sd = jax.ShapeDtypeStruct
    return pl.pallas_call(
        body, grid=(nsteps,),
        in_specs=[_row_spec(tm, D_MODEL), _row_spec(tm, D_MODEL)],
        out_specs=[_row_spec(tm, D_MODEL), _par_spec(LANES)],
        out_shape=[sd((t, D_MODEL), F32), sd((1, LANES), F32)],
        scratch_shapes=[pltpu.VMEM((1, D_MODEL), F32)],
        compiler_params=_cparams(("arbitrary",)), name="loss_grad",
    )(y, tgt)


_GELU_C = math.sqrt(2.0 / math.pi)
_GELU_K = 0.044715
HALO = 16


def _gelu_parts(x):
    th = jnp.tanh(_GELU_C * (x + _GELU_K * x * x * x))
    return 0.5 * x * (1.0 + th), th


def _halo_specs(tm, t, shift=0):
    last = t // HALO - 1
    cur = pl.BlockSpec((None, tm, FF_SH), lambda j, i: (j + shift, i, 0))
    prev = pl.BlockSpec((None, HALO, FF_SH), lambda j, i: (j + shift, jnp.maximum(i * (tm // HALO) - 1, 0), 0))
    nxt = pl.BlockSpec((None, HALO, FF_SH), lambda j, i: (j + shift, jnp.minimum((i + 1) * (tm // HALO), last), 0))
    return [prev, cur, nxt]


def _ffn_mid_fwd(gu, cw, tm=512):
    t = gu.shape[1]
    nsteps = t // tm

    def body(gp_ref, g_ref, gn_ref, u_ref, cw_ref, h_ref):
        i = pl.program_id(1)
        gg = g_ref[...].astype(F32)
        row = lax.broadcasted_iota(jnp.int32, gg.shape, 0)
        prev = jnp.where(i == 0, 0.0, gp_ref[...].astype(F32)[HALO - 1:HALO, :])
        nxt = jnp.where(i == nsteps - 1, 0.0, gn_ref[...].astype(F32)[0:1, :])
        g_m1 = jnp.where(row == 0, prev, pltpu.roll(gg, 1, 0))
        g_p1 = jnp.where(row == tm - 1, nxt, pltpu.roll(gg, tm - 1, 0))
        gc = cw_ref[3:4, :] + g_m1 * cw_ref[0:1, :] + gg * cw_ref[1:2, :] + g_p1 * cw_ref[2:3, :]
        act, _ = _gelu_parts(gc)
        h_ref[...] = (act * u_ref[...].astype(F32)).astype(BF16)

    return pl.pallas_call(
        body, grid=(N_SHARD, nsteps),
        in_specs=_halo_specs(tm, t) + [pl.BlockSpec((None, tm, FF_SH), lambda j, i: (j + N_SHARD, i, 0)),
                                       pl.BlockSpec((None, 8, FF_SH), lambda j, i: (j, 0, 0))],
        out_specs=pl.BlockSpec((None, tm, FF_SH), lambda j, i: (j, i, 0)),
        out_shape=jax.ShapeDtypeStruct((N_SHARD, t, FF_SH), BF16),
        compiler_params=_cparams(("parallel", "parallel")), name="ffn_mid_fwd",
    )(gu, gu, gu, gu, cw)


def _ffn_mid_bwd(gu, dh, cw, tm=512):
    t = gu.shape[1]
    nsteps = t // tm
    te = tm + 2 * HALO

    def body(gp_ref, g_ref, gn_ref, up_ref, u_ref, un_ref, dp_ref, d_ref, dn_ref, cw_ref, dgu_ref, st_ref):
        i = pl.program_id(1)

        @pl.when(i == 0)
        def _():
            st_ref[...] = jnp.zeros_like(st_ref)

        e = lax.broadcasted_iota(jnp.int32, (te, FF_SH), 0)
        tg = i * tm - HALO + e
        valid = (tg >= 0) & (tg < t)

        def ext(p_ref, c_ref, n_ref):
            whole = jnp.concatenate([p_ref[...], c_ref[...], n_ref[...]], axis=0).astype(F32)
            return jnp.where(valid, whole, 0.0)

        eg = ext(gp_ref, g_ref, gn_ref)
        eu = ext(up_ref, u_ref, un_ref)
        ed = ext(dp_ref, d_ref, dn_ref)
        w0, w1, w2 = cw_ref[0:1, :], cw_ref[1:2, :], cw_ref[2:3, :]
        g_m1 = pltpu.roll(eg, 1, 0)
        g_p1 = pltpu.roll(eg, te - 1, 0)
        gc = cw_ref[3:4, :] + g_m1 * w0 + eg * w1 + g_p1 * w2
        act, th = _gelu_parts(gc)
        dact = 0.5 * (1.0 + th) + 0.5 * gc * (1.0 - th * th) * _GELU_C * (1.0 + 3.0 * _GELU_K * gc * gc)
        dgc = ed * eu * dact
        dg = pltpu.roll(dgc, te - 1, 0) * w0 + dgc * w1 + pltpu.roll(dgc, 1, 0) * w2
        mid = slice(HALO, HALO + tm)
        dgu_ref[0] = dg[mid].astype(BF16)
        dgu_ref[1] = (ed * act)[mid].astype(BF16)
        sel = dgc[mid]
        parts = [jnp.sum(sel, axis=0, keepdims=True),
                 jnp.sum(sel * g_m1[mid], axis=0, keepdims=True),
                 jnp.sum(sel * eg[mid], axis=0, keepdims=True),
                 jnp.sum(sel * g_p1[mid], axis=0, keepdims=True)]
        r8 = lax.broadcasted_iota(jnp.int32, (8, FF_SH), 0)
        upd = jnp.zeros((8, FF_SH), F32)
        for k, p in enumerate(parts):
            upd = upd + jnp.where(r8 == k, p, 0.0)
        st_ref[...] += upd

    sd = jax.ShapeDtypeStruct
    return pl.pallas_call(
        body, grid=(N_SHARD, nsteps),
        in_specs=_halo_specs(tm, t) + _halo_specs(tm, t, N_SHARD) + _halo_specs(tm, t)
        + [pl.BlockSpec((None, 8, FF_SH), lambda j, i: (j, 0, 0))],
        out_specs=[pl.BlockSpec((2, None, tm, FF_SH), lambda j, i: (0, j, i, 0)),
                   pl.BlockSpec((None, 8, FF_SH), lambda j, i: (j, 0, 0))],
        out_shape=[sd((2, N_SHARD, t, FF_SH), BF16), sd((N_SHARD, 8, FF_SH), F32)],
        compiler_params=_cparams(("parallel", "arbitrary")), name="ffn_mid_bwd",
    )(gu, gu, gu, gu, gu, gu, dh, dh, dh, cw)


def _stack_heads(src_ref, dst_ref, tq):
    lo = lax.broadcasted_iota(jnp.int32, (tq, LANES), 1) < HEAD_DIM
    for i in range(4):
        blk = src_ref[:, LANES * i:LANES * (i + 1)].astype(dst_ref.dtype)
        zero = jnp.zeros_like(blk)
        dst_ref[tq * i:tq * (i + 1), :] = jnp.where(lo, blk, zero)
        dst_ref[tq * (4 + i):tq * (5 + i), :] = jnp.where(lo, zero, blk)


def _gattn_fwd(q, k, v, gather=(), tq=128, tk=2048):
    t = q.shape[0]
    tk = min(tk, t)
    nq, nk, r = t // tq, t // tk, 8 * tq
    ng = len(gather)

    def body(*refs):
        q_ref, k_ref, v_ref = refs[:3]
        o_ref, lse_ref = refs[3 + ng:5 + ng]
        qst, m_s, l_s, acct = refs[5 + 2 * ng:9 + 2 * ng]
        if ng:
            ex = _gather_exchange(refs[3:3 + ng], refs[5 + ng:5 + 2 * ng], *refs[9 + 2 * ng:])
            pl.when(pl.program_id(0) == 0)(ex.start)
        lo_rows = lax.broadcasted_iota(jnp.int32, (LANES, tq), 0) < HEAD_DIM
        for i in range(4):
            bt = q_ref[:, LANES * i:LANES * (i + 1)].astype(F32).T
            qst[:, tq * i:tq * (i + 1)] = jnp.where(lo_rows, bt, 0.0).astype(BF16)
            qst[:, tq * (4 + i):tq * (5 + i)] = jnp.where(lo_rows, 0.0, bt).astype(BF16)
        m_s[...] = jnp.full_like(m_s, NEG)

        def max_step(j, carry):
            off = pl.multiple_of(j * tk, tk)
            st = _dot(k_ref[pl.ds(off, tk), :], qst[...], _NN)
            m_s[...] = jnp.maximum(m_s[...], jnp.max(st.reshape(tk // 8, 8, r), axis=0))
            return carry

        lax.fori_loop(0, nk, max_step, 0)
        m_row = jnp.max(m_s[...], axis=0, keepdims=True)
        l_s[...] = jnp.zeros_like(l_s)
        acct[...] = jnp.zeros_like(acct)

        def sum_step(j, carry):
            off = pl.multiple_of(j * tk, tk)
            st = _dot(k_ref[pl.ds(off, tk), :], qst[...], _NN)
            pt = jnp.exp(st - m_row)
            l_s[...] += jnp.sum(pt.reshape(tk // 8, 8, r), axis=0)
            for kv in range(2):
                rows, cols = pl.ds(HEAD_DIM * kv, HEAD_DIM), slice(4 * tq * kv, 4 * tq * (kv + 1))
                acct[rows, cols] += _dot(v_ref[j, rows, :], pt[:, cols], _NN)
            return carry

        lax.fori_loop(0, nk, sum_step, 0)
        l_row = jnp.sum(l_s[...], axis=0, keepdims=True)
        ot = acct[...] / l_row
        for i in range(4):
            pair_t = jnp.where(lo_rows, ot[:, tq * i:tq * (i + 1)], ot[:, tq * (4 + i):tq * (5 + i)])
            o_ref[:, LANES * i:LANES * (i + 1)] = pair_t.T
        lse_ref[...] = m_row + jnp.log(l_row)
        if ng:
            pl.when(pl.program_id(0) == nq - 1)(ex.wait)

    sd = jax.ShapeDtypeStruct
    vt3 = v.reshape(nk, tk, KV_W).transpose(0, 2, 1)
    return pl.pallas_call(
        body, grid=(nq,),
        in_specs=[_row_spec(tq, Q_W), _par_spec(KV_W, t), pl.BlockSpec((nk, KV_W, tk), lambda i: (0, 0, 0))]
        + [_ANY] * ng,
        out_specs=[_row_spec(tq, Q_W), pl.BlockSpec((None, 1, r), lambda i: (i, 0, 0))] + [_ANY] * ng,
        out_shape=[sd((t, Q_W), F32), sd((nq, 1, r), F32)] + _gathered_shapes(gather),
        scratch_shapes=[pltpu.VMEM((LANES, r), BF16), pltpu.VMEM((8, r), F32), pltpu.VMEM((8, r), F32),
                        pltpu.VMEM((LANES, r), F32)] + (_exchange_sems(ng) if ng else []),
        compiler_params=_cparams(("arbitrary",) if ng else ("parallel",)),
        name="gattn_fwd_gather" if ng else "gattn_fwd",
    )(q, k, vt3, *gather)


def _gattn_bwd(q, k, v, o, do, lse, scatter=None, tq=128, tk=1024):
    t = q.shape[0]
    tk = min(tk, t)
    nq, nk, r = t // tq, t // tk, 8 * tq
    items, sgrads = scatter if scatter else ((), ())
    ns = len(sgrads)
    slot_shapes = []
    for j, (o_idx, _) in enumerate(items):
        if o_idx == len(slot_shapes):
            slot_shapes += _slot_shapes([sgrads[j]])
    nslots = len(slot_shapes)

    n_in, n_scr = 7, 6
    kt3 = k.reshape(nk, tk, KV_W).transpose(0, 2, 1)

    def body(*refs):
        q_ref, k_ref, v_ref, kt_ref, o_ref, do_ref, lse_ref = refs[:n_in]
        dq_ref, dk_ref, dv_ref = refs[n_in + ns:n_in + 3 + ns]
        scr = n_in + 3 + ns + nslots
        qs, dos, qst, dost, dlt_row, dqt = refs[scr:scr + n_scr]
        if ns:
            ex = _scatter_exchange(items, refs[n_in:n_in + ns], refs[n_in + 3 + ns:scr], *refs[scr + n_scr:])
            pl.when(pl.program_id(0) == 0)(ex.start)

        @pl.when(pl.program_id(0) == 0)
        def _():
            dk_ref[...] = jnp.zeros_like(dk_ref)
            dv_ref[...] = jnp.zeros_like(dv_ref)

        _stack_heads(q_ref, qs, tq)
        _stack_heads(do_ref, dos, tq)
        lo_rows = lax.broadcasted_iota(jnp.int32, (LANES, tq), 0) < HEAD_DIM
        for i in range(4):
            lo, hi = slice(tq * i, tq * (i + 1)), slice(tq * (4 + i), tq * (5 + i))
            cols = slice(LANES * i, LANES * (i + 1))
            for src, dst in ((q_ref, qst), (do_ref, dost)):
                bt = src[:, cols].astype(F32).T
                dst[:, lo] = jnp.where(lo_rows, bt, 0.0).astype(BF16)
                dst[:, hi] = jnp.where(lo_rows, 0.0, bt).astype(BF16)
            prod_t = (do_ref[:, cols] * o_ref[:, cols]).T
            dlt_row[:, lo] = jnp.sum(prod_t[:HEAD_DIM], axis=0, keepdims=True)
            dlt_row[:, hi] = jnp.sum(prod_t[HEAD_DIM:], axis=0, keepdims=True)
        lse_row = lse_ref[...]
        dqt[...] = jnp.zeros_like(dqt)

        def step(j, carry):
            off = pl.multiple_of(j * tk, tk)
            kc = k_ref[pl.ds(off, tk), :]
            vc = v_ref[pl.ds(off, tk), :]
            p = jnp.exp(_dot(kc, qst[...], _NN) - lse_row)
            dp = _dot(vc, dost[...], _NN)
            ds = (p * (dp - dlt_row[...])).astype(BF16)
            dk_ref[pl.ds(off, tk), :] += _dot(ds, qs[...], _NN)
            dv_ref[pl.ds(off, tk), :] += _dot(p, dos[...], _NN)
            for kv in range(2):
                rows, cols = pl.ds(HEAD_DIM * kv, HEAD_DIM), slice(4 * tq * kv, 4 * tq * (kv + 1))
                dqt[rows, cols] += _dot(kt_ref[j, rows, :], ds[:, cols], _NN)
            return carry

        lax.fori_loop(0, nk, step, 0)
        for i in range(4):
            pair_t = jnp.where(lo_rows, dqt[:, tq * i:tq * (i + 1)], dqt[:, tq * (4 + i):tq * (5 + i)])
            dq_ref[:, LANES * i:LANES * (i + 1)] = pair_t.T
        if ns:
            pl.when(pl.program_id(0) == nq - 1)(ex.wait)

    sd = jax.ShapeDtypeStruct
    return pl.pallas_call(
        body, grid=(nq,),
        in_specs=[_row_spec(tq, Q_W), _par_spec(KV_W, t), _par_spec(KV_W, t),
                  pl.BlockSpec((nk, KV_W, tk), lambda i: (0, 0, 0)), _row_spec(tq, Q_W), _row_spec(tq, Q_W),
                  pl.BlockSpec((None, 1, r), lambda i: (i, 0, 0))] + [_ANY] * ns,
        out_specs=[_row_spec(tq, Q_W), _par_spec(KV_W, t), _par_spec(KV_W, t)] + [_ANY] * nslots,
        out_shape=[sd((t, Q_W), F32), sd((t, KV_W), F32), sd((t, KV_W), F32)] + slot_shapes,
        scratch_shapes=[pltpu.VMEM((r, LANES), BF16), pltpu.VMEM((r, LANES), BF16), pltpu.VMEM((LANES, r), BF16),
                        pltpu.VMEM((LANES, r), BF16), pltpu.VMEM((1, r), F32),
                        pltpu.VMEM((LANES, r), F32)] + (_exchange_sems(ns) if ns else []),
        compiler_params=_cparams(("arbitrary",)), name="gattn_bwd_scatter" if ns else "gattn_bwd",
    )(q, k, v, kt3, o, do, lse, *sgrads)


_WQ = Q_BLOCK
_WK = 3 * Q_BLOCK
_WR = 8 * _WQ


def _pairs_transposed(src_ref, dst, tq):
    lo_rows = lax.broadcasted_iota(jnp.int32, (LANES, tq), 0) < HEAD_DIM
    for i in range(4):
        bt = src_ref[:, LANES * i:LANES * (i + 1)].astype(F32).T
        dst[:, tq * i:tq * (i + 1)] = jnp.where(lo_rows, bt, 0.0).astype(BF16)
        dst[:, tq * (4 + i):tq * (5 + i)] = jnp.where(lo_rows, 0.0, bt).astype(BF16)


def _pairs_from_transposed(halves, dst_ref, tq):
    for i in range(4):
        pair_t = jnp.concatenate([h[:, tq * i:tq * (i + 1)] for h in halves], axis=0)
        dst_ref[:, LANES * i:LANES * (i + 1)] = pair_t.T.astype(dst_ref.dtype)


def _kv_quadrants(tq):
    return [(slice(HEAD_DIM * kv, HEAD_DIM * (kv + 1)), slice(4 * tq * kv, 4 * tq * (kv + 1))) for kv in range(2)]


def _wattn_scores_t(kw, qst, bias_ref, n, t):
    kabs = (n - 1) * _WQ + lax.broadcasted_iota(jnp.int32, (_WK, 1), 0)
    st = _dot(kw, qst[...], _NN) + bias_ref[...]
    return jnp.where((kabs >= 0) & (kabs < t), st, NEG)


def _window_t(ref3, n):
    return jnp.concatenate([ref3[n], ref3[n + 1], ref3[n + 2]], axis=1)


def _blocks_transposed(ap):
    return ap.reshape(ap.shape[0] // _WQ, _WQ, KV_W).transpose(0, 2, 1)


def _wattn_fwd(q, kp, vp, bias_t, sink):
    t = q.shape[0]
    nq = t // _WQ
    tp = t + 2 * _WQ
    vpt = _blocks_transposed(vp)

    def body(q_ref, k_ref, vt_ref, b_ref, sk_ref, o_ref, lse_ref, qst):
        n = pl.program_id(0)
        _pairs_transposed(q_ref, qst, _WQ)
        kw = k_ref[pl.ds(pl.multiple_of(n * _WQ, _WQ), _WK), :]
        st = _wattn_scores_t(kw, qst, b_ref, n, t)
        sk = sk_ref[...]
        m = jnp.maximum(jnp.max(st, axis=0, keepdims=True), sk)
        pt = jnp.exp(st - m)
        l = jnp.sum(pt, axis=0, keepdims=True) + jnp.exp(sk - m)
        vwt = _window_t(vt_ref, n)
        halves = [_dot(vwt[rows, :], pt[:, cols], _NN) / l[:, cols] for rows, cols in _kv_quadrants(_WQ)]
        _pairs_from_transposed(halves, o_ref, _WQ)
        lse_ref[...] = m + jnp.log(l)

    sd = jax.ShapeDtypeStruct
    return pl.pallas_call(
        body, grid=(nq,),
        in_specs=[_row_spec(_WQ, Q_W), _par_spec(KV_W, tp), pl.BlockSpec(vpt.shape, lambda i: (0, 0, 0)),
                  _par_spec(_WR, _WK), _par_spec(_WR)],
        out_specs=[_row_spec(_WQ, Q_W), pl.BlockSpec((None, 1, _WR), lambda i: (i, 0, 0))],
        out_shape=[sd((t, Q_W), F32), sd((nq, 1, _WR), F32)],
        scratch_shapes=[pltpu.VMEM((LANES, _WR), BF16)],
        compiler_params=_cparams(("parallel",)), name="wattn_fwd",
    )(q, kp, vpt, bias_t, sink)


def _wattn_bwd(q, kp, vp, bias_t, sink, o, do, lse):
    t = q.shape[0]
    nq = t // _WQ
    tp = t + 2 * _WQ
    kpt = _blocks_transposed(kp)

    def body(q_ref, k_ref, v_ref, kt_ref, b_ref, sk_ref, o_ref, do_ref, lse_ref, dq_ref, dk_ref, dv_ref, db_ref,
             dsk_ref, qs, dos, qst, dost):
        n = pl.program_id(0)

        @pl.when(n == 0)
        def _():
            dk_ref[...] = jnp.zeros_like(dk_ref)
            dv_ref[...] = jnp.zeros_like(dv_ref)
            db_ref[...] = jnp.zeros_like(db_ref)
            dsk_ref[...] = jnp.zeros_like(dsk_ref)

        _stack_heads(q_ref, qs, _WQ)
        _stack_heads(do_ref, dos, _WQ)
        _pairs_transposed(q_ref, qst, _WQ)
        _pairs_transposed(do_ref, dost, _WQ)
        delta = []
        for i in range(4):
            cols = slice(LANES * i, LANES * (i + 1))
            prod_t = (do_ref[:, cols] * o_ref[:, cols]).T
            delta.append((jnp.sum(prod_t[:HEAD_DIM], axis=0, keepdims=True),
                          jnp.sum(prod_t[HEAD_DIM:], axis=0, keepdims=True)))
        dlt = jnp.concatenate([d[0] for d in delta] + [d[1] for d in delta], axis=1)
        off = pl.multiple_of(n * _WQ, _WQ)
        kw = k_ref[pl.ds(off, _WK), :]
        vw = v_ref[pl.ds(off, _WK), :]
        lse_v = lse_ref[...]
        pt = jnp.exp(_wattn_scores_t(kw, qst, b_ref, n, t) - lse_v)
        dpt = _dot(vw, dost[...], _NN)
        ds = pt * (dpt - dlt)
        db_ref[...] += ds
        dsk_ref[...] -= jnp.exp(sk_ref[...] - lse_v) * dlt
        dsb = ds.astype(BF16)
        dk_ref[pl.ds(off, _WK), :] += _dot(dsb, qs[...], _NN)
        dv_ref[pl.ds(off, _WK), :] += _dot(pt, dos[...], _NN)
        kwt = _window_t(kt_ref, n)
        halves = [_dot(kwt[rows, :], dsb[:, cols], _NN) for rows, cols in _kv_quadrants(_WQ)]
        _pairs_from_transposed(halves, dq_ref, _WQ)

    sd = jax.ShapeDtypeStruct
    qb = _row_spec(_WQ, Q_W)
    return pl.pallas_call(
        body, grid=(nq,),
        in_specs=[qb, _par_spec(KV_W, tp), _par_spec(KV_W, tp), pl.BlockSpec(kpt.shape, lambda i: (0, 0, 0)),
                  _par_spec(_WR, _WK), _par_spec(_WR), qb, qb, pl.BlockSpec((None, 1, _WR), lambda i: (i, 0, 0))],
        out_specs=[qb, _par_spec(KV_W, tp), _par_spec(KV_W, tp), _par_spec(_WR, _WK), _par_spec(_WR)],
        out_shape=[sd((t, Q_W), F32), sd((tp, KV_W), F32), sd((tp, KV_W), F32), sd((_WK, _WR), F32), sd((1, _WR), F32)],
        scratch_shapes=[pltpu.VMEM((_WR, LANES), BF16), pltpu.VMEM((_WR, LANES), BF16), pltpu.VMEM((LANES, _WR), BF16),
                        pltpu.VMEM((LANES, _WR), BF16)],
        compiler_params=_cparams(("arbitrary",)), name="wattn_bwd",
    )(q, kp, vp, kpt, bias_t, sink, o, do, lse)


def _bias_bucket_reduce(db0, db1, bucket):
    def body(a_ref, b_ref, bk_ref, o_ref):
        d = a_ref[...] + b_ref[...]
        bk = bk_ref[...]
        lane = lax.broadcasted_iota(jnp.int32, (1, LANES), 1)
        out = jnp.zeros((1, LANES), F32)
        for b in range(N_BUCKETS):
            tot = jnp.sum(jnp.sum(jnp.where(bk == b, d, 0.0), axis=-1, keepdims=True), axis=0, keepdims=True)
            out = out + jnp.where(lane == b, tot, 0.0)
        o_ref[...] = out

    hb = pl.BlockSpec((None, _WQ, _WK), lambda h: (h, 0, 0))
    return pl.pallas_call(
        body, grid=(8,), in_specs=[hb, hb, pl.BlockSpec((_WQ, _WK), lambda h: (0, 0))],
        out_specs=pl.BlockSpec((None, 1, LANES), lambda h: (h, 0, 0)),
        out_shape=jax.ShapeDtypeStruct((8, 1, LANES), F32),
        compiler_params=_cparams(("parallel",)), name="bias_bucket_reduce",
    )(db0.reshape(8, _WQ, _WK), db1.reshape(8, _WQ, _WK), bucket)


def _rope_tables(t):
    rows_n = t // GRID_W
    row = jnp.repeat(jnp.arange(rows_n, dtype=F32), GRID_W)
    col = jnp.tile(jnp.arange(GRID_W, dtype=F32), rows_n)
    half = HEAD_DIM // 2
    inv_freq = ROPE_THETA ** (-jnp.arange(0, half, 2, dtype=F32) / half)
    ang = jnp.concatenate([row[:, None] * inv_freq, col[:, None] * inv_freq], axis=-1)
    cos, sin = jnp.cos(ang), jnp.sin(ang)
    c64 = jnp.repeat(cos, 2, axis=-1)
    s64 = jnp.stack([-sin, sin], axis=-1).reshape(t, HEAD_DIM)
    return jnp.tile(c64, (1, 2)), jnp.tile(s64, (1, 2))


def _t5_bucket(rel):
    half = N_BUCKETS // 2
    max_exact = half // 2
    bucket = jnp.where(rel > 0, half, 0)
    rp = jnp.abs(rel)
    rpf = jnp.maximum(rp, 1).astype(jnp.float32)
    large = max_exact + (jnp.log(rpf / max_exact) / math.log(MAX_DISTANCE / max_exact)
                         * (half - max_exact)).astype(jnp.int32)
    large = jnp.minimum(large, half - 1)
    return bucket + jnp.where(rp < max_exact, rp, large)


def _window_tables(rel_bias):
    qpos = jnp.arange(_WQ, dtype=jnp.int32)
    kpos = jnp.arange(_WK, dtype=jnp.int32) - _WQ
    rel = kpos[None, :] - qpos[:, None]
    bucket = _t5_bucket(rel)
    bias = jnp.zeros((8, _WQ, _WK), F32)
    for b in range(N_BUCKETS):
        bias = jnp.where((bucket == b)[None], rel_bias[b][:, None, None], bias)
    bias = jnp.where((jnp.abs(rel) <= WINDOW)[None], bias, NEG)
    return bias.reshape(_WR, _WK).T, bucket


def _pad_rows(a):
    return jnp.pad(a, ((_WQ, _WQ), (0, 0)))


def _layer_fwd(x, p, tabs, gather=None):
    cos_t, sin_t, bias = tabs
    h = _mm_nn(x, p["win"], F32, "in_proj")
    qa, ka, va, qb, kb, vb = _prep_fwd(h, cos_t, sin_t, p["qn"], p["kn"])
    if gather is None:
        oa, lse_a = _gattn_fwd(qa, ka, va)
    else:
        oa, lse_a, *gathered = _gattn_fwd(qa, ka, va, gather=gather[0])
        p = gather[1](gathered)
    kbp, vbp = _pad_rows(kb), _pad_rows(vb)
    ob, lse_b = _wattn_fwd(qb, kbp, vbp, bias, p["sink"])
    y = _outnorm_fwd(oa, ob, p["ga"], p["gb"])
    z1, x1, x1b = _mm_reduce(y[None], p["wout"][None], _NN, F32, "out_proj", res=x, res_scale=ALPHA,
                             ln=(p["ln1g"], p["ln1b"]))
    gu = _mm_expand(x1b, p["wgu"], _NN, BF16, "gate_up_proj")
    hdn = _ffn_mid_fwd(gu, p["cw"])
    z2, x2, _ = _mm_reduce(hdn, p["wd"], _NN, F32, "down_proj", res=x1, res_scale=ALPHA, ln=(p["ln2g"], p["ln2b"]))
    saved = dict(x=x, h=h, qa=qa, ka=ka, va=va, qb=qb, kbp=kbp, vbp=vbp, oa=oa, ob=ob, lse_a=lse_a, lse_b=lse_b,
                 y=y, z1=z1, x1b=x1b, gu=gu, hdn=hdn, z2=z2)
    return x2, saved


def _block_grads(g, names=("w_in", "w_out", "w_gate", "w_up", "w_down")):
    make = dict(
        w_in=lambda: _col_blocks(_in_cols_to_pairs(g["win"], _from_pairs), IN_SH),
        w_out=lambda: _mix_rows_to_pairs(g["wout"], _from_pairs).reshape(N_SHARD, OUT_SH, D_MODEL),
        w_gate=lambda: g["wg"], w_up=lambda: g["wu"], w_down=lambda: g["wd"])
    return [make[n]() for n in names]


def _layer_bwd(dx2, p, s, tabs, layer=0, pending=None):
    cos_t, sin_t, bias = tabs
    t = dx2.shape[0]
    dz2, dz2b, dln2g, dln2b = _ln_bwd(dx2, s["z2"], p["ln2g"])
    dhdn = _mm_expand(dz2b, p["wd"], _NT, BF16, "down_dx")
    dwd = _mm_tn_blocks(s["hdn"], dz2b, "down_dw")
    dgu, stats = _ffn_mid_bwd(s["gu"], dhdn, p["cw"])
    dgu = dgu.reshape(2 * N_SHARD, t, FF_SH)
    dx1 = _mm_reduce(dgu, p["wgu"], _NT, F32, "gate_up_dx", res=dz2, res_scale=ALPHA)
    dwg = _mm_tn_blocks(s["x1b"], dgu, "gate_dw", blk=0)
    dwu = _mm_tn_blocks(s["x1b"], dgu, "up_dw", blk=1)
    dz1, dz1b, dln1g, dln1b = _ln_bwd(dx1, s["z1"], p["ln1g"])
    dy = _mm_nt(dz1b, p["wout"], F32, "out_dx")
    dwout = _mm_tn(s["y"], dz1b, "out_dw")
    doa, dob, dga, dgb = _outnorm_bwd(dy, s["oa"], s["ob"], p["ga"], p["gb"])
    slots = None
    if pending is None:
        dqa, dka, dva = _gattn_bwd(s["qa"], s["ka"], s["va"], s["oa"], doa, s["lse_a"])
    else:
        mine = _block_grads(dict(wout=dwout, wg=dwg, wu=dwu, wd=dwd), ("w_out", "w_gate", "w_up", "w_down"))
        todo = list(pending) + [(o + 1, layer, g) for o, g in enumerate(mine)]
        dqa, dka, dva, *slots = _gattn_bwd(s["qa"], s["ka"], s["va"], s["oa"], doa, s["lse_a"],
                                           scatter=([(o, l) for o, l, _ in todo], [g for _, _, g in todo]))
    dqb, dkbp, dvbp, dbias, dsink = _wattn_bwd(s["qb"], s["kbp"], s["vbp"], bias, p["sink"], s["ob"], dob, s["lse_b"])
    dkb = lax.slice_in_dim(dkbp, _WQ, _WQ + t, axis=0)
    dvb = lax.slice_in_dim(dvbp, _WQ, _WQ + t, axis=0)
    dh, dqn, dkn = _prep_bwd(s["h"], cos_t, sin_t, p["qn"], p["kn"], dqa, dka, dva, dqb, dkb, dvb)
    dx = _mm_nt(dh, p["win"], F32, "in_dx", res=dz1, res_scale=ALPHA)
    dwin = _mm_tn(s["x"], dh, "in_dw")
    grads = dict(win=dwin, wout=dwout, wg=dwg, wu=dwu, wd=dwd, stats=stats, qn=dqn, kn=dkn, ga=dga, gb=dgb,
                 ln1g=dln1g, ln1b=dln1b, ln2g=dln2g, ln2b=dln2b, bias=dbias, sink=dsink, slots=slots)
    return dx, grads


def _prep_layer_params(l, win, wout, wg, wu, wd, cw, q_norm, k_norm, sink, out_norm_a, out_norm_b, conv_b,
                       ln1_g, ln1_b, ln2_g, ln2_b):
    win_full = win.transpose(1, 0, 2).reshape(D_MODEL, IN_COLS)
    row = lambda v: v.reshape(1, -1)
    late = {}
    if wout is not None:
        late = dict(
            wout=_mix_rows_to_pairs(wout.reshape(D_MODEL, D_MODEL)), wgu=jnp.concatenate([wg, wu], axis=0), wd=wd,
            cw=jnp.pad(cw, ((0, 0), (0, 5), (0, 0)))
            + jnp.pad(conv_b[l].reshape(N_SHARD, 1, FF_SH), ((0, 0), (3, 4), (0, 0))))
    return dict(
        late, win=_in_cols_to_pairs(win_full),
        qn=row(jnp.tile(q_norm[l], 2)), kn=row(jnp.tile(k_norm[l], 2)),
        ga=row(_to_pairs(out_norm_a[l], 0)), gb=row(_to_pairs(out_norm_b[l], 0)),
        ln1g=row(ln1_g[l]), ln1b=row(ln1_b[l]), ln2g=row(ln2_g[l]), ln2b=row(ln2_b[l]),
        sink=jnp.repeat(sink[l], _WQ).reshape(1, _WR))


def _local_step(x, tgt, params, rel_bias, gather=None, scatter=False):
    t = x.shape[0]
    cos_t, sin_t = _rope_tables(t)
    bias, bucket = _window_tables(rel_bias)
    tabs = (cos_t, sin_t, bias)
    saved = []
    for l in range(DEPTH):
        x, s = _layer_fwd(x, params[l], tabs, gather if l == 0 else None)
        saved.append(s)
    dx, loss = _loss_grad(x, tgt)
    grads = [None] * DEPTH
    for l in reversed(range(DEPTH)):
        pending = None
        if scatter and l == 0:
            pending = [(o, 1, g) for o, g in enumerate(_block_grads(grads[1]))]
        dx, grads[l] = _layer_bwd(dx, params[l], saved[l], tabs, l, pending)
    dbucket = _bias_bucket_reduce(grads[0]["bias"].T, grads[1]["bias"].T, bucket)
    return loss, dx, grads, dbucket


_ANY = pl.BlockSpec(memory_space=pl.ANY)
_MESH = pl.DeviceIdType.MESH


def _mesh_pos():
    return lax.axis_index("x"), lax.axis_index("y"), lax.axis_index("c")


def _other_chips(x, y):
    return [(1 - x, y), (x, 1 - y), (1 - x, 1 - y)]


class _Exchange:
    def __init__(self, local, sends, recvs):
        self.local, self.sends, self.recvs = local, sends, recvs

    def start(self):
        for cp in self.local + self.sends:
            cp.start()

    def wait(self):
        for cp in self.recvs:
            cp.wait_recv()
        for cp in self.sends:
            cp.wait_send()
        for cp in self.local:
            cp.wait()


def _exchange_sems(n):
    return [pltpu.SemaphoreType.DMA((n, 3)), pltpu.SemaphoreType.DMA((n, 3)), pltpu.SemaphoreType.DMA((n,))]


def _gather_exchange(ins, outs, send, recv, loc):
    x, y, c = _mesh_pos()
    me = 2 * x + y
    chips = _other_chips(x, y)

    def remote(i, k, block):
        px, py = chips[k]
        return pltpu.make_async_remote_copy(ins[i], outs[i].at[block], send.at[i, k], recv.at[i, k],
                                            device_id=(px, py, c), device_id_type=_MESH)

    n = len(ins)
    local = [pltpu.make_async_copy(ins[i], outs[i].at[me], loc.at[i]) for i in range(n)]
    sends = [remote(i, k, me) for i in range(n) for k in range(3)]
    recvs = [remote(i, k, 2 * chips[k][0] + chips[k][1]) for i in range(n) for k in range(3)]
    return _Exchange(local, sends, recvs)


def _scatter_exchange(items, ins, outs, send, recv, loc):
    x, y, c = _mesh_pos()
    me = 2 * x + y
    chips = _other_chips(x, y)

    def remote(j, k):
        o, l = items[j]
        px, py = chips[k]
        return pltpu.make_async_remote_copy(ins[j].at[2 * px + py], outs[o].at[k, l], send.at[j, k], recv.at[j, k],
                                            device_id=(px, py, c), device_id_type=_MESH)

    local = [pltpu.make_async_copy(ins[j].at[me], outs[o].at[3, l], loc.at[j]) for j, (o, l) in enumerate(items)]
    sends = [remote(j, k) for j in range(len(items)) for k in range(3)]
    return _Exchange(local, sends, sends)


def _gathered_shapes(shards):
    return [jax.ShapeDtypeStruct((N_SHARD,) + s.shape, s.dtype) for s in shards]


def _slot_shapes(blocks):
    return [jax.ShapeDtypeStruct((N_SHARD, DEPTH) + g.shape[1:], g.dtype) for g in blocks]


def _gather_shards(shards):
    n = len(shards)

    def body(*refs):
        ex = _gather_exchange(refs[:n], refs[n:2 * n], *refs[2 * n:])
        ex.start()
        ex.wait()

    return pl.pallas_call(
        body, in_specs=[_ANY] * n, out_specs=[_ANY] * n, out_shape=_gathered_shapes(shards),
        scratch_shapes=_exchange_sems(n), name="gather_weights",
    )(*shards)


def _scatter_into(items, grads, slots):
    n, ns = len(grads), len(slots)

    def body(*refs):
        ex = _scatter_exchange(items, refs[:n], refs[n + ns:n + 2 * ns], *refs[n + 2 * ns:])
        ex.start()
        ex.wait()

    return pl.pallas_call(
        body, in_specs=[_ANY] * (n + ns), out_specs=[_ANY] * ns,
        out_shape=[jax.ShapeDtypeStruct(s.shape, s.dtype) for s in slots],
        input_output_aliases={n + i: i for i in range(ns)},
        scratch_shapes=_exchange_sems(n), name="scatter_grads",
    )(*grads, *slots)


def _swap_with_sibling(parts):
    n = len(parts)

    def body(*refs):
        ins, outs = refs[:n], refs[n:2 * n]
        send, recv = refs[2 * n:]
        x, y, c = _mesh_pos()
        copies = [pltpu.make_async_remote_copy(ins[i], outs[i], send.at[i], recv.at[i], device_id=(x, y, 1 - c),
                                               device_id_type=_MESH) for i in range(n)]
        for cp in copies:
            cp.start()
        for cp in copies:
            cp.wait_recv()
        for cp in copies:
            cp.wait_send()

    return pl.pallas_call(
        body, in_specs=[_ANY] * n, out_specs=[_ANY] * n,
        out_shape=[jax.ShapeDtypeStruct(p.shape, p.dtype) for p in parts],
        scratch_shapes=[pltpu.SemaphoreType.DMA((n,)), pltpu.SemaphoreType.DMA((n,))],
        name="swap_sibling",
    )(*parts)


N_DEV = 8


def _allreduce_small(packed):
    rows = packed.shape[0]

    def body(in_ref, out_ref, buf, send, recv, loc):
        x, y, c = _mesh_pos()
        me = 4 * x + 2 * y + c
        own = pltpu.make_async_copy(in_ref, buf.at[me], loc)
        own.start()

        def remote(m, block):
            peer = (x ^ (m >> 2), y ^ ((m >> 1) & 1), c ^ (m & 1))
            return pltpu.make_async_remote_copy(in_ref, buf.at[block], send.at[m - 1], recv.at[m - 1],
                                                device_id=peer, device_id_type=_MESH)

        sends = [remote(m, me) for m in range(1, N_DEV)]
        for cp in sends:
            cp.start()
        for m in range(1, N_DEV):
            remote(m, me ^ m).wait_recv()
        for cp in sends:
            cp.wait_send()
        own.wait()
        tot = buf[0]
        for d in range(1, N_DEV):
            tot = tot + buf[d]
        out_ref[...] = tot

    vm = pl.BlockSpec(memory_space=pltpu.VMEM)
    return pl.pallas_call(
        body, in_specs=[vm], out_specs=vm, out_shape=jax.ShapeDtypeStruct((rows, LANES), F32),
        scratch_shapes=[pltpu.VMEM((N_DEV, rows, LANES), F32), pltpu.SemaphoreType.DMA((N_DEV - 1,)),
                        pltpu.SemaphoreType.DMA((N_DEV - 1,)), pltpu.SemaphoreType.DMA(())],
        name="allreduce_small",
    )(packed)


def _shard_rows(r):
    return r // 2 if r % 32 == 0 else r


def _sum_slots(slots):
    _, _, r, cdim = slots.shape
    tr = _shard_rows(r)

    def body(a_ref, b_ref, c_ref, d_ref, o_ref):
        up = lambda ref: ref[...].astype(F32)
        o_ref[...] = ((up(d_ref) + up(a_ref)) + up(b_ref)) + up(c_ref)

    def spec(k):
        return pl.BlockSpec((None, None, tr, cdim), lambda l, i: (k, l, i, 0))

    return pl.pallas_call(
        body, grid=(DEPTH, r // tr), in_specs=[spec(0), spec(1), spec(2), spec(3)],
        out_specs=pl.BlockSpec((None, tr, cdim), lambda l, i: (l, i, 0)),
        out_shape=jax.ShapeDtypeStruct((DEPTH, r, cdim), F32),
        compiler_params=_cparams(("parallel", "parallel")), name="sum_slots",
    )(slots, slots, slots, slots)


def _adamw_math(w, g, m, v):
    m = ADAM_B1 * m + (1.0 - ADAM_B1) * g
    v = ADAM_B2 * v + (1.0 - ADAM_B2) * (g * g)
    m_hat = m / (1.0 - ADAM_B1 ** ADAM_STEP)
    v_hat = v / (1.0 - ADAM_B2 ** ADAM_STEP)
    delta = -ADAM_LR * (m_hat / (jnp.sqrt(v_hat) + ADAM_EPS) + ADAM_WD * w)
    return delta, m, v


def _adamw_big(ga, gb, w, m, v):
    _, r, cdim = w.shape
    tr = _shard_rows(r)

    def body(ga_ref, gb_ref, w_ref, m_ref, v_ref, g_out, d_out, m_out, v_out):
        g = ga_ref[...] + gb_ref[...]
        d, mn, vn = _adamw_math(w_ref[...], g, m_ref[...], v_ref[...])
        g_out[...] = g
        d_out[...] = d
        m_out[...] = mn
        v_out[...] = vn

    spec = pl.BlockSpec((None, tr, cdim), lambda l, i: (l, i, 0))
    shp = jax.ShapeDtypeStruct(w.shape, F32)
    return pl.pallas_call(
        body, grid=(DEPTH, r // tr), in_specs=[spec] * 5, out_specs=[spec] * 4, out_shape=[shp] * 4,
        compiler_params=_cparams(("parallel", "parallel")), name="adamw_big",
    )(ga, gb, w, m, v)


def _adamw_small(ws, gs, ms, vs):
    n = len(ws)

    def body(*refs):
        w_r, g_r, m_r, v_r = (refs[k * n:(k + 1) * n] for k in range(4))
        d_o, m_o, v_o = (refs[(4 + k) * n:(5 + k) * n] for k in range(3))
        for i in range(n):
            d, mn, vn = _adamw_math(w_r[i][...], g_r[i][...], m_r[i][...], v_r[i][...])
            d_o[i][...] = d
            m_o[i][...] = mn
            v_o[i][...] = vn

    vm = pl.BlockSpec(memory_space=pltpu.VMEM)
    shp = [jax.ShapeDtypeStruct(w.shape, F32) for w in ws]
    outs = pl.pallas_call(
        body, in_specs=[vm] * (4 * n), out_specs=[vm] * (3 * n), out_shape=shp * 3, name="adamw_small",
    )(*ws, *gs, *ms, *vs)
    return outs[:n], outs[n:2 * n], outs[2 * n:]


def _tile_rows(a):
    a = a.reshape(-1, LANES)
    pad = (-a.shape[0]) % 8
    return jnp.pad(a, ((0, pad), (0, 0))) if pad else a


_SMALL_LAYER_PARTS = (("qn", 8), ("kn", 8), ("sink", 8), ("ga", 8), ("gb", 8), ("ln1g", 8), ("ln1b", 8),
                      ("ln2g", 8), ("ln2b", 8), ("stats", N_SHARD * 8 * FF_SH // LANES))
_SMALL_HEAD_ROWS = 16
_SMALL_LAYER_ROWS = sum(r for _, r in _SMALL_LAYER_PARTS)


def _pack_small(loss, dbucket, grads):
    parts = [_tile_rows(loss), _tile_rows(dbucket)]
    for l in range(DEPTH):
        parts += [_tile_rows(grads[l][name]) for name, _ in _SMALL_LAYER_PARTS]
    return jnp.concatenate(parts, axis=0)


def _unpack_small(tot, chip):
    out = dict(loss=tot[0, 0], rel_bias=tot[8:16, :N_BUCKETS].T)
    per = {name: [] for name, _ in _SMALL_LAYER_PARTS}
    for l in range(DEPTH):
        base = _SMALL_HEAD_ROWS + l * _SMALL_LAYER_ROWS
        for name, rows in _SMALL_LAYER_PARTS:
            per[name].append(tot[base:base + rows])
            base += rows
    fold = lambda v: v[0, :HEAD_DIM] + v[0, HEAD_DIM:]
    out["q_norm"] = jnp.stack([fold(v) for v in per["qn"]])
    out["k_norm"] = jnp.stack([fold(v) for v in per["kn"]])
    out["sink"] = jnp.stack([jnp.sum(v, axis=1) for v in per["sink"]])
    out["out_norm_a"] = jnp.stack([_from_pairs(v[:4].reshape(Q_W), 0) for v in per["ga"]])
    out["out_norm_b"] = jnp.stack([_from_pairs(v[:4].reshape(Q_W), 0) for v in per["gb"]])
    for name, key in (("ln1_g", "ln1g"), ("ln1_b", "ln1b"), ("ln2_g", "ln2g"), ("ln2_b", "ln2b")):
        out[name] = jnp.stack([v.reshape(D_MODEL) for v in per[key]])
    stats = [v.reshape(N_SHARD, 8, FF_SH) for v in per["stats"]]
    out["conv_b"] = jnp.stack([s[:, 0, :].reshape(D_FF) for s in stats])
    out["conv_w"] = jnp.stack([lax.dynamic_index_in_dim(s, chip, 0, keepdims=False)[1:4] for s in stats])
    return out


_WEIGHTS = ("rel_bias", "w_in", "q_norm", "k_norm", "sink", "out_norm_a", "out_norm_b", "w_out", "ln1_g", "ln1_b",
            "w_gate", "w_up", "conv_w", "conv_b", "w_down", "ln2_g", "ln2_b")
_BIG = ("w_in", "w_out", "w_gate", "w_up", "w_down")
_SMALL = tuple(n for n in _WEIGHTS if n not in _BIG)


def _col_blocks(g, n):
    return g.reshape(g.shape[0], N_SHARD, n).transpose(1, 0, 2)


def kernel(x, rel_bias, w_in, q_norm, k_norm, sink, out_norm_a, out_norm_b, w_out, ln1_g, ln1_b, w_gate, w_up, conv_w, conv_b, w_down, ln2_g, ln2_b, loss_target, m_rel_bias, m_w_in, m_q_norm, m_k_norm, m_sink, m_out_norm_a, m_out_norm_b, m_w_out, m_ln1_g, m_ln1_b, m_w_gate, m_w_up, m_conv_w, m_conv_b, m_w_down, m_ln2_g, m_ln2_b, v_rel_bias, v_w_in, v_q_norm, v_k_norm, v_sink, v_out_norm_a, v_out_norm_b, v_w_out, v_ln1_g, v_ln1_b, v_w_gate, v_w_up, v_conv_w, v_conv_b, v_w_down, v_ln2_g, v_ln2_b):
    w = dict(rel_bias=rel_bias, w_in=w_in, q_norm=q_norm, k_norm=k_norm, sink=sink, out_norm_a=out_norm_a,
             out_norm_b=out_norm_b, w_out=w_out, ln1_g=ln1_g, ln1_b=ln1_b, w_gate=w_gate, w_up=w_up, conv_w=conv_w,
             conv_b=conv_b, w_down=w_down, ln2_g=ln2_g, ln2_b=ln2_b)
    m = dict(rel_bias=m_rel_bias, w_in=m_w_in, q_norm=m_q_norm, k_norm=m_k_norm, sink=m_sink, out_norm_a=m_out_norm_a,
             out_norm_b=m_out_norm_b, w_out=m_w_out, ln1_g=m_ln1_g, ln1_b=m_ln1_b, w_gate=m_w_gate, w_up=m_w_up,
             conv_w=m_conv_w, conv_b=m_conv_b, w_down=m_w_down, ln2_g=m_ln2_g, ln2_b=m_ln2_b)
    v = dict(rel_bias=v_rel_bias, w_in=v_w_in, q_norm=v_q_norm, k_norm=v_k_norm, sink=v_sink, out_norm_a=v_out_norm_a,
             out_norm_b=v_out_norm_b, w_out=v_w_out, ln1_g=v_ln1_g, ln1_b=v_ln1_b, w_gate=v_w_gate, w_up=v_w_up,
             conv_w=v_conv_w, conv_b=v_conv_b, w_down=v_w_down, ln2_g=v_ln2_g, ln2_b=v_ln2_b)
    chip = 2 * lax.axis_index("x") + lax.axis_index("y")

    small_w = (q_norm, k_norm, sink, out_norm_a, out_norm_b, conv_b, ln1_g, ln1_b, ln2_g, ln2_b)
    (win0,) = _gather_shards([w_in[0].astype(BF16)])
    later = ([w[name][0].astype(BF16) for name in _BIG[1:]] + [w[name][1].astype(BF16) for name in _BIG] + [conv_w])
    params = [_prep_layer_params(0, win0, None, None, None, None, None, *small_w), None]

    def finish(g):
        wout0, wg0, wu0, wd0, win1, wout1, wg1, wu1, wd1, cw_all = g
        params[0] = _prep_layer_params(0, win0, wout0, wg0, wu0, wd0, cw_all[:, 0], *small_w)
        params[1] = _prep_layer_params(1, win1, wout1, wg1, wu1, wd1, cw_all[:, 1], *small_w)
        return params[0]

    loss, dx, grads, dbucket = _local_step(x[0], loss_target[0], params, rel_bias, gather=(later, finish),
                                           scatter=True)

    small = _unpack_small(_allreduce_small(_pack_small(loss, dbucket, grads)), chip)

    slots = list(grads[0]["slots"])
    slots[0] = _scatter_into([(0, 0)], _block_grads(grads[0], ("w_in",)), [slots[0]])[0]
    partial = [_sum_slots(s) for s in slots]
    other = _swap_with_sibling(partial)

    grad, delta, new_m, new_v = {}, {}, {}, {}
    for i, name in enumerate(_BIG):
        grad[name], delta[name], new_m[name], new_v[name] = _adamw_big(partial[i], other[i], w[name], m[name], v[name])
    flat2 = lambda a: a.reshape(-1, a.shape[-1])
    ds, ms, vs = _adamw_small([flat2(w[n]) for n in _SMALL], [flat2(small[n]) for n in _SMALL],
                              [flat2(m[n]) for n in _SMALL], [flat2(v[n]) for n in _SMALL])
    for i, name in enumerate(_SMALL):
        grad[name] = small[name]
        delta[name] = ds[i].reshape(w[name].shape)
        new_m[name] = ms[i].reshape(w[name].shape)
        new_v[name] = vs[i].reshape(w[name].shape)

    return (small["loss"], dx[None], *[grad[n] for n in _WEIGHTS], *[delta[n] for n in _WEIGHTS],
            *[new_m[n] for n in _WEIGHTS], *[new_v[n] for n in _WEIGHTS])
```

```python
import functools
import math

import numpy as np
import jax
import jax.numpy as jnp
from jax import lax
from jax.experimental import pallas as pl
from jax.experimental.pallas import tpu as pltpu

F32 = jnp.float32
BF16 = jnp.bfloat16

D_MODEL = 1024
DEPTH = 2
HEAD_DIM = 64
Q_W = 512
KV_W = 128
IN_COLS = 2 * (Q_W + 2 * KV_W)
N_SHARD = 4
IN_SH = IN_COLS // N_SHARD
OUT_SH = D_MODEL // N_SHARD
D_FF = 2816
FF_SH = D_FF // N_SHARD
Q_BLOCK = 128
WINDOW = 128
N_BUCKETS = 32
MAX_DISTANCE = 128
GRID_W = 64
ROPE_THETA = 10000.0
ALPHA = (2.0 * DEPTH) ** 0.25
RMS_EPS = 1e-6
LN_EPS = 1e-5
NEG = -1e30
LANES = 128
VMEM_LIMIT = 56 * 1024 * 1024

ADAM_LR = 0.001
ADAM_B1 = 0.9
ADAM_B2 = 0.999
ADAM_EPS = 1e-08
ADAM_WD = 0.01
ADAM_STEP = 10

_NN = (((1,), (0,)), ((), ()))
_NT = (((1,), (1,)), ((), ()))
_TN = (((0,), (0,)), ((), ()))


def _dot(a, b, dims):
    return lax.dot_general(a.astype(BF16), b.astype(BF16), dims, preferred_element_type=F32)


def _cparams(sem, vmem=VMEM_LIMIT):
    return pltpu.CompilerParams(dimension_semantics=sem, vmem_limit_bytes=vmem)


def _regroup(a, axis, n_outer, n_inner):
    shp = a.shape
    a = a.reshape(shp[:axis] + (n_outer, n_inner, HEAD_DIM) + shp[axis + 1:])
    return jnp.swapaxes(a, axis, axis + 1).reshape(shp)


def _to_pairs(a, axis):
    return _regroup(a, axis, 2, 4)


def _from_pairs(a, axis):
    return _regroup(a, axis, 4, 2)


def _in_cols_to_pairs(w, fn=_to_pairs):
    return jnp.concatenate([fn(w[..., :Q_W], w.ndim - 1), w[..., Q_W:Q_W + 2 * KV_W],
                            fn(w[..., Q_W + 2 * KV_W:2 * Q_W + 2 * KV_W], w.ndim - 1),
                            w[..., 2 * Q_W + 2 * KV_W:]], axis=-1)


def _mix_rows_to_pairs(w, fn=_to_pairs):
    return fn(w.reshape(2, Q_W, w.shape[-1]), 1).reshape(w.shape)


def _matmul(a, b, *, dims, grid, a_spec, b_spec, o_spec, out_shape, acc_shape, name, res=None,
            res_spec=None, res_scale=1.0):
    nk = grid[-1]
    kax = len(grid) - 1

    def body(*refs):
        if res is None:
            a_ref, b_ref, o_ref, acc = refs
            r_ref = None
        else:
            a_ref, b_ref, r_ref, o_ref, acc = refs
        k = pl.program_id(kax)

        @pl.when(k == 0)
        def _():
            acc[...] = jnp.zeros_like(acc)

        acc[...] += _dot(a_ref[...], b_ref[...], dims)

        @pl.when(k == nk - 1)
        def _():
            o = acc[...]
            if r_ref is not None:
                o = o + res_scale * r_ref[...]
            o_ref[...] = o.astype(o_ref.dtype)

    in_specs = [a_spec, b_spec] + ([res_spec] if res is not None else [])
    args = (a, b) + ((res,) if res is not None else ())
    sem = ("parallel",) * kax + ("arbitrary",)
    return pl.pallas_call(
        body, grid=grid, in_specs=in_specs, out_specs=o_spec, out_shape=out_shape,
        scratch_shapes=[pltpu.VMEM(acc_shape, F32)], compiler_params=_cparams(sem), name=name,
    )(*args)


def _mm_nn(a, b, out_dtype, name, tm=512, res=None, res_scale=1.0):
    m, kd = a.shape
    n = b.shape[1]
    return _matmul(
        a, b, dims=_NN, grid=(m // tm, 1),
        a_spec=pl.BlockSpec((tm, kd), lambda i, k: (i, 0)),
        b_spec=pl.BlockSpec((kd, n), lambda i, k: (0, 0)),
        o_spec=pl.BlockSpec((tm, n), lambda i, k: (i, 0)),
        out_shape=jax.ShapeDtypeStruct((m, n), out_dtype), acc_shape=(tm, n), name=name,
        res=res, res_spec=pl.BlockSpec((tm, n), lambda i, k: (i, 0)), res_scale=res_scale)


def _mm_nt(a, b, out_dtype, name, tm=512, res=None, res_scale=1.0):
    m, kd = a.shape
    n = b.shape[0]
    return _matmul(
        a, b, dims=_NT, grid=(m // tm, 1),
        a_spec=pl.BlockSpec((tm, kd), lambda i, k: (i, 0)),
        b_spec=pl.BlockSpec((n, kd), lambda i, k: (0, 0)),
        o_spec=pl.BlockSpec((tm, n), lambda i, k: (i, 0)),
        out_shape=jax.ShapeDtypeStruct((m, n), out_dtype), acc_shape=(tm, n), name=name,
        res=res, res_spec=pl.BlockSpec((tm, n), lambda i, k: (i, 0)), res_scale=res_scale)


def _mm_tn(a, b, name, tk=512, tn=None, out_dtype=BF16):
    t, m = a.shape
    n = b.shape[1]
    tn = n if tn is None else tn
    return _matmul(
        a, b, dims=_TN, grid=(n // tn, t // tk),
        a_spec=pl.BlockSpec((tk, m), lambda j, k: (k, 0)),
        b_spec=pl.BlockSpec((tk, tn), lambda j, k: (k, j)),
        o_spec=pl.BlockSpec((m, tn), lambda j, k: (0, j)),
        out_shape=jax.ShapeDtypeStruct((m, n), out_dtype), acc_shape=(m, tn), name=name)


def _blocked_n(w, dims):
    return w.shape[2] if dims == _NN else w.shape[1]


def _mm_expand(a, w, dims, out_dtype, name, tm=512):
    m, kd = a.shape
    nb, n = w.shape[0], _blocked_n(w, dims)

    def body(a_ref, w_ref, o_ref):
        av = a_ref[...]
        for j in range(nb):
            o_ref[j] = _dot(av, w_ref[j], dims).astype(o_ref.dtype)

    return pl.pallas_call(
        body, grid=(m // tm,),
        in_specs=[pl.BlockSpec((tm, kd), lambda i: (i, 0)), pl.BlockSpec(w.shape, lambda i: (0, 0, 0))],
        out_specs=pl.BlockSpec((nb, tm, n), lambda i: (0, i, 0)),
        out_shape=jax.ShapeDtypeStruct((nb, m, n), out_dtype),
        compiler_params=_cparams(("parallel",)), name=name,
    )(a, w)


def _mm_reduce(a, w, dims, out_dtype, name, tm=512, res=None, res_scale=1.0, ln=None):
    nb, m, kd = a.shape
    n = _blocked_n(w, dims)
    n_in = 2 + (res is not None) + (2 if ln else 0)

    def body(*refs):
        a_ref, w_ref = refs[0], refs[1]
        acc = _dot(a_ref[0], w_ref[0], dims)
        for j in range(1, nb):
            acc = acc + _dot(a_ref[j], w_ref[j], dims)
        if res is not None:
            acc = acc + res_scale * refs[2][...]
        refs[n_in][...] = acc.astype(out_dtype)
        if ln:
            g_ref, b_ref = refs[n_in - 2], refs[n_in - 1]
            zc = acc - jnp.mean(acc, axis=-1, keepdims=True)
            r = lax.rsqrt(jnp.mean(zc * zc, axis=-1, keepdims=True) + LN_EPS)
            y = zc * r * g_ref[...] + b_ref[...]
            refs[n_in + 1][...] = y
            refs[n_in + 2][...] = y.astype(BF16)

    row = pl.BlockSpec((tm, n), lambda i: (i, 0))
    par = pl.BlockSpec((1, n), lambda i: (0, 0))
    sd = jax.ShapeDtypeStruct
    out = pl.pallas_call(
        body, grid=(m // tm,),
        in_specs=[pl.BlockSpec((nb, tm, kd), lambda i: (0, i, 0)), pl.BlockSpec(w.shape, lambda i: (0, 0, 0))]
        + ([row] if res is not None else []) + ([par, par] if ln else []),
        out_specs=[row] * (3 if ln else 1),
        out_shape=[sd((m, n), out_dtype)] + ([sd((m, n), F32), sd((m, n), BF16)] if ln else []),
        compiler_params=_cparams(("parallel",)), name=name,
    )(a, w, *((res,) if res is not None else ()), *(ln or ()))
    return out if ln else out[0]


def _mm_tn_blocks(a, b, name, blk=0, nb=N_SHARD, tk=512, out_dtype=BF16):
    a3, b3 = a.ndim == 3, b.ndim == 3
    t, m, n = a.shape[-2], a.shape[-1], b.shape[-1]
    nsteps = t // tk

    def spec(blocked, width):
        if blocked:
            return pl.BlockSpec((nb, tk, width), lambda k: (blk, k, 0))
        return pl.BlockSpec((tk, width), lambda k: (k, 0))

    def body(a_ref, b_ref, o_ref, acc):
        k = pl.program_id(0)

        @pl.when(k == 0)
        def _():
            acc[...] = jnp.zeros_like(acc)

        for j in range(nb):
            acc[j] += _dot(a_ref[j] if a3 else a_ref[...], b_ref[j] if b3 else b_ref[...], _TN)

        @pl.when(k == nsteps - 1)
        def _():
            o_ref[...] = acc[...].astype(o_ref.dtype)

    return pl.pallas_call(
        body, grid=(nsteps,), in_specs=[spec(a3, m), spec(b3, n)],
        out_specs=pl.BlockSpec((nb, m, n), lambda k: (0, 0, 0)),
        out_shape=jax.ShapeDtypeStruct((nb, m, n), out_dtype),
        scratch_shapes=[pltpu.VMEM((nb, m, n), F32)],
        compiler_params=_cparams(("arbitrary",)), name=name,
    )(a, b)


def _row_spec(tm, n):
    return pl.BlockSpec((tm, n), lambda i: (i, 0))


def _par_spec(n, rows=1):
    return pl.BlockSpec((rows, n), lambda i: (0, 0))


def _swap_pairs(x):
    lane = lax.broadcasted_iota(jnp.int32, x.shape, 1)
    return jnp.where(lane % 2 == 0, pltpu.roll(x, LANES - 1, 1), pltpu.roll(x, 1, 1))


def _head_sums(v):
    lo = lax.broadcasted_iota(jnp.int32, v.shape, 1) < HEAD_DIM
    s_lo = jnp.sum(jnp.where(lo, v, 0.0), axis=-1, keepdims=True)
    s_hi = jnp.sum(jnp.where(lo, 0.0, v), axis=-1, keepdims=True)
    return jnp.where(lo, s_lo, s_hi)


def _qk_blocks():
    return [(128 * i, True) for i in range(4)] + [(Q_W, False)]


def _prep_fwd(h, cos_t, sin_t, qn, kn, tm=256):
    t = h.shape[0]
    scale = HEAD_DIM ** -0.5

    def body(h_ref, c_ref, s_ref, qn_ref, kn_ref, qa_ref, ka_ref, va_ref, qb_ref, kb_ref, vb_ref):
        c = c_ref[...]
        s = s_ref[...]
        for start, is_q in _qk_blocks():
            x = h_ref[:, start:start + LANES]
            r = lax.rsqrt(_head_sums(x * x) * (1.0 / HEAD_DIM) + RMS_EPS)
            y = x * r * (qn_ref[...] if is_q else kn_ref[...])
            y = y * c + _swap_pairs(y) * s
            if is_q:
                qa_ref[:, start:start + LANES] = (y * scale).astype(BF16)
            else:
                ka_ref[...] = y.astype(BF16)
        va_ref[...] = h_ref[:, 640:768].astype(BF16)
        qb_ref[...] = (h_ref[:, 768:1280] * scale).astype(BF16)
        kb_ref[...] = h_ref[:, 1280:1408].astype(BF16)
        vb_ref[...] = h_ref[:, 1408:1536].astype(BF16)

    sd = jax.ShapeDtypeStruct
    return pl.pallas_call(
        body, grid=(t // tm,),
        in_specs=[_row_spec(tm, IN_COLS), _row_spec(tm, LANES), _row_spec(tm, LANES), _par_spec(LANES), _par_spec(LANES)],
        out_specs=[_row_spec(tm, Q_W), _row_spec(tm, KV_W), _row_spec(tm, KV_W),
                   _row_spec(tm, Q_W), _row_spec(tm, KV_W), _row_spec(tm, KV_W)],
        out_shape=[sd((t, Q_W), BF16), sd((t, KV_W), BF16), sd((t, KV_W), BF16),
                   sd((t, Q_W), BF16), sd((t, KV_W), BF16), sd((t, KV_W), BF16)],
        compiler_params=_cparams(("parallel",)), name="prep_fwd",
    )(h, cos_t, sin_t, qn, kn)


def _prep_bwd(h, cos_t, sin_t, qn, kn, dqa, dka, dva, dqb, dkb, dvb, tm=256):
    t = h.shape[0]
    scale = HEAD_DIM ** -0.5

    def body(h_ref, c_ref, s_ref, qn_ref, kn_ref, dqa_ref, dka_ref, dva_ref, dqb_ref, dkb_ref, dvb_ref,
             dh_ref, dqn_ref, dkn_ref):
        @pl.when(pl.program_id(0) == 0)
        def _():
            dqn_ref[...] = jnp.zeros_like(dqn_ref)
            dkn_ref[...] = jnp.zeros_like(dkn_ref)

        c = c_ref[...]
        s = s_ref[...]
        for start, is_q in _qk_blocks():
            x = h_ref[:, start:start + LANES]
            gain = qn_ref[...] if is_q else kn_ref[...]
            d = dqa_ref[:, start:start + LANES] * scale if is_q else dka_ref[...]
            dy = d * c + _swap_pairs(d * s)
            r = lax.rsqrt(_head_sums(x * x) * (1.0 / HEAD_DIM) + RMS_EPS)
            xr = x * r
            gsum = jnp.sum(dy * xr, axis=0, keepdims=True)
            if is_q:
                dqn_ref[...] += gsum
            else:
                dkn_ref[...] += gsum
            gy = dy * gain
            dx = r * (gy - xr * (_head_sums(xr * gy) * (1.0 / HEAD_DIM)))
            dh_ref[:, start:start + LANES] = dx.astype(BF16)
        dh_ref[:, 640:768] = dva_ref[...].astype(BF16)
        dh_ref[:, 768:1280] = (dqb_ref[...] * scale).astype(BF16)
        dh_ref[:, 1280:1408] = dkb_ref[...].astype(BF16)
        dh_ref[:, 1408:1536] = dvb_ref[...].astype(BF16)

    sd = jax.ShapeDtypeStruct
    return pl.pallas_call(
        body, grid=(t // tm,),
        in_specs=[_row_spec(tm, IN_COLS), _row_spec(tm, LANES), _row_spec(tm, LANES), _par_spec(LANES), _par_spec(LANES),
                  _row_spec(tm, Q_W), _row_spec(tm, KV_W), _row_spec(tm, KV_W),
                  _row_spec(tm, Q_W), _row_spec(tm, KV_W), _row_spec(tm, KV_W)],
        out_specs=[_row_spec(tm, IN_COLS), _par_spec(LANES), _par_spec(LANES)],
        out_shape=[sd((t, IN_COLS), BF16), sd((1, LANES), F32), sd((1, LANES), F32)],
        compiler_params=_cparams(("arbitrary",)), name="prep_bwd",
    )(h, cos_t, sin_t, qn, kn, dqa, dka, dva, dqb, dkb, dvb)


def _outnorm_fwd(oa, ob, ga, gb, tm=512):
    t = oa.shape[0]

    def body(oa_ref, ob_ref, ga_ref, gb_ref, y_ref):
        for o_ref, g_ref, start in ((oa_ref, ga_ref, 0), (ob_ref, gb_ref, Q_W)):
            x = o_ref[...]
            r = lax.rsqrt(jnp.mean(x * x, axis=-1, keepdims=True) + RMS_EPS)
            y_ref[:, start:start + Q_W] = (x * r * g_ref[...]).astype(BF16)

    return pl.pallas_call(
        body, grid=(t // tm,),
        in_specs=[_row_spec(tm, Q_W), _row_spec(tm, Q_W), _par_spec(Q_W), _par_spec(Q_W)],
        out_specs=_row_spec(tm, D_MODEL), out_shape=jax.ShapeDtypeStruct((t, D_MODEL), BF16),
        compiler_params=_cparams(("parallel",)), name="outnorm_fwd",
    )(oa, ob, ga, gb)


def _outnorm_bwd(dy, oa, ob, ga, gb, tm=512):
    t = oa.shape[0]

    def body(dy_ref, oa_ref, ob_ref, ga_ref, gb_ref, doa_ref, dob_ref, dga_ref, dgb_ref):
        @pl.when(pl.program_id(0) == 0)
        def _():
            dga_ref[...] = jnp.zeros_like(dga_ref)
            dgb_ref[...] = jnp.zeros_like(dgb_ref)

        for o_ref, g_ref, do_ref, dg_ref, start in ((oa_ref, ga_ref, doa_ref, dga_ref, 0),
                                                    (ob_ref, gb_ref, dob_ref, dgb_ref, Q_W)):
            x = o_ref[...]
            d = dy_ref[:, start:start + Q_W]
            r = lax.rsqrt(jnp.mean(x * x, axis=-1, keepdims=True) + RMS_EPS)
            xr = x * r
            dg_ref[...] += jnp.sum(d * xr, axis=0, keepdims=True)
            gy = d * g_ref[...]
            do_ref[...] = r * (gy - xr * jnp.mean(xr * gy, axis=-1, keepdims=True))

    sd = jax.ShapeDtypeStruct
    return pl.pallas_call(
        body, grid=(t // tm,),
        in_specs=[_row_spec(tm, D_MODEL), _row_spec(tm, Q_W), _row_spec(tm, Q_W), _par_spec(Q_W), _par_spec(Q_W)],
        out_specs=[_row_spec(tm, Q_W), _row_spec(tm, Q_W), _par_spec(Q_W), _par_spec(Q_W)],
        out_shape=[sd((t, Q_W), F32), sd((t, Q_W), F32), sd((1, Q_W), F32), sd((1, Q_W), F32)],
        compiler_params=_cparams(("arbitrary",)), name="outnorm_bwd",
    )(dy, oa, ob, ga, gb)


def _ln_bwd(d, z, g, tm=512):
    t = z.shape[0]

    def body(d_ref, z_ref, g_ref, dz_ref, dzb_ref, dg_ref, db_ref):
        @pl.when(pl.program_id(0) == 0)
        def _():
            dg_ref[...] = jnp.zeros_like(dg_ref)
            db_ref[...] = jnp.zeros_like(db_ref)

        zz = z_ref[...]
        dd = d_ref[...]
        mu = jnp.mean(zz, axis=-1, keepdims=True)
        zc = zz - mu
        r = lax.rsqrt(jnp.mean(zc * zc, axis=-1, keepdims=True) + LN_EPS)
        xh = zc * r
        dg_ref[...] += jnp.sum(dd * xh, axis=0, keepdims=True)
        db_ref[...] += jnp.sum(dd, axis=0, keepdims=True)
        dxh = dd * g_ref[...]
        dz = r * (dxh - jnp.mean(dxh, axis=-1, keepdims=True) - xh * jnp.mean(dxh * xh, axis=-1, keepdims=True))
        dz_ref[...] = dz
        dzb_ref[...] = dz.astype(BF16)

    sd = jax.ShapeDtypeStruct
    return pl.pallas_call(
        body, grid=(t // tm,),
        in_specs=[_row_spec(tm, D_MODEL), _row_spec(tm, D_MODEL), _par_spec(D_MODEL)],
        out_specs=[_row_spec(tm, D_MODEL), _row_spec(tm, D_MODEL), _par_spec(D_MODEL), _par_spec(D_MODEL)],
        out_shape=[sd((t, D_MODEL), F32), sd((t, D_MODEL), BF16), sd((1, D_MODEL), F32), sd((1, D_MODEL), F32)],
        compiler_params=_cparams(("arbitrary",)), name="ln_bwd",
    )(d, z, g)


def _loss_grad(y, tgt, tm=512):
    t = y.shape[0]
    nsteps = t // tm

    def body(y_ref, t_ref, dy_ref, loss_ref, acc):
        i = pl.program_id(0)

        @pl.when(i == 0)
        def _():
            acc[...] = jnp.zeros_like(acc)

        e = y_ref[...] - t_ref[...]
        dy_ref[...] = e * (1.0 / D_MODEL)
        acc[...] += jnp.sum(e * e, axis=0, keepdims=True)

        @pl.when(i == nsteps - 1)
        def _():
            tot = jnp.sum(acc[...], axis=-1, keepdims=True) * (0.5 / D_MODEL)
            loss_ref[...] = jnp.broadcast_to(tot, loss_ref.shape)

    sd = jax.ShapeDtypeStruct
    return pl.pallas_call(
        body, grid=(nsteps,),
        in_specs=[_row_spec(tm, D_MODEL), _row_spec(tm, D_MODEL)],
        out_specs=[_row_spec(tm, D_MODEL), _par_spec(LANES)],
        out_shape=[sd((t, D_MODEL), F32), sd((1, LANES), F32)],
        scratch_shapes=[pltpu.VMEM((1, D_MODEL), F32)],
        compiler_params=_cparams(("arbitrary",)), name="loss_grad",
    )(y, tgt)


_GELU_C = math.sqrt(2.0 / math.pi)
_GELU_K = 0.044715
HALO = 16


def _gelu_parts(x):
    x2 = x * x
    th = jnp.tanh(x * (_GELU_C + (_GELU_C * _GELU_K) * x2))
    a = 0.5 + 0.5 * th
    dact = a + (0.5 * x) * (1.0 - th * th) * (_GELU_C + (3.0 * _GELU_C * _GELU_K) * x2)
    return x * a, dact


def _halo_specs(tm, t, shift=0):
    last = t // HALO - 1
    cur = pl.BlockSpec((None, tm, FF_SH), lambda j, i: (j + shift, i, 0))
    prev = pl.BlockSpec((None, HALO, FF_SH), lambda j, i: (j + shift, jnp.maximum(i * (tm // HALO) - 1, 0), 0))
    nxt = pl.BlockSpec((None, HALO, FF_SH), lambda j, i: (j + shift, jnp.minimum((i + 1) * (tm // HALO), last), 0))
    return [prev, cur, nxt]


def _ffn_mid_fwd(gu, cw, tm=512):
    t = gu.shape[1]
    nsteps = t // tm

    def body(gp_ref, g_ref, gn_ref, u_ref, cw_ref, h_ref):
        i = pl.program_id(1)
        gg = g_ref[...].astype(F32)
        row = lax.broadcasted_iota(jnp.int32, gg.shape, 0)
        prev = jnp.where(i == 0, 0.0, gp_ref[...].astype(F32)[HALO - 1:HALO, :])
        nxt = jnp.where(i == nsteps - 1, 0.0, gn_ref[...].astype(F32)[0:1, :])
        g_m1 = jnp.where(row == 0, prev, pltpu.roll(gg, 1, 0))
        g_p1 = jnp.where(row == tm - 1, nxt, pltpu.roll(gg, tm - 1, 0))
        gc =cw_ref[3:4, :] + g_m1 * cw_ref[0:1, :] + gg * cw_ref[1:2, :] + g_p1 * cw_ref[2:3, :]
        act, _ = _gelu_parts(gc)
        h_ref[...] = (act * u_ref[...].astype(F32)).astype(BF16)

    return pl.pallas_call(
        body, grid=(N_SHARD, nsteps),
        in_specs=_halo_specs(tm, t) + [pl.BlockSpec((None, tm, FF_SH), lambda j, i: (j + N_SHARD, i, 0)),
                                       pl.BlockSpec((None, 8, FF_SH), lambda j, i: (j, 0, 0))],
        out_specs=pl.BlockSpec((None, tm, FF_SH), lambda j, i: (j, i, 0)),
        out_shape=jax.ShapeDtypeStruct((N_SHARD, t, FF_SH), BF16),
        compiler_params=_cparams(("parallel", "parallel")), name="ffn_mid_fwd",
    )(gu, gu, gu, gu, cw)


def _ffn_mid_bwd(gu, dh, cw, tm=512):
    t = gu.shape[1]
    nsteps = t // tm
    te = tm + 2 * HALO

    def body(gp_ref, g_ref, gn_ref, up_ref, u_ref, un_ref, dp_ref, d_ref, dn_ref, cw_ref, dgu_ref, st_ref):
        i = pl.program_id(1)

        @pl.when(i == 0)
        def _():
            st_ref[...] = jnp.zeros_like(st_ref)

        def ext(p_ref, c_ref, n_ref):
            prev = jnp.where(i == 0, 0.0, p_ref[...].astype(F32))
            nxt = jnp.where(i == nsteps - 1, 0.0, n_ref[...].astype(F32))
            return jnp.concatenate([prev, c_ref[...].astype(F32), nxt], axis=0)

        eg = ext(gp_ref, g_ref, gn_ref)
        eu = ext(up_ref, u_ref, un_ref)
        ed = ext(dp_ref, d_ref, dn_ref)
        w0, w1, w2 = cw_ref[0:1, :], cw_ref[1:2, :], cw_ref[2:3, :]
        g_m1 = pltpu.roll(eg, 1, 0)
        g_p1 = pltpu.roll(eg, te - 1, 0)
        gc = cw_ref[3:4, :] + g_m1 * w0 + eg * w1 + g_p1 * w2
        act, dact = _gelu_parts(gc)
        dgc = ed * eu * dact
        dg = pltpu.roll(dgc, te - 1, 0) * w0 + dgc * w1 + pltpu.roll(dgc, 1, 0) * w2
        mid = slice(HALO, HALO + tm)
        dgu_ref[0] = dg[mid].astype(BF16)
        dgu_ref[1] = (ed * act)[mid].astype(BF16)
        sel = dgc[mid]
        parts = [jnp.sum(sel, axis=0, keepdims=True),
                 jnp.sum(sel * g_m1[mid], axis=0, keepdims=True),
                 jnp.sum(sel * eg[mid], axis=0, keepdims=True),
                 jnp.sum(sel * g_p1[mid], axis=0, keepdims=True)]
        r8 = lax.broadcasted_iota(jnp.int32, (8, FF_SH), 0)
        upd = jnp.zeros((8, FF_SH), F32)
        for k, p in enumerate(parts):
            upd = upd + jnp.where(r8 == k, p, 0.0)
        st_ref[...] += upd

    sd = jax.ShapeDtypeStruct
    return pl.pallas_call(
        body, grid=(N_SHARD, nsteps),
        in_specs=_halo_specs(tm, t) + _halo_specs(tm, t, N_SHARD) + _halo_specs(tm, t)
        + [pl.BlockSpec((None, 8, FF_SH), lambda j, i: (j, 0, 0))],
        out_specs=[pl.BlockSpec((2, None, tm, FF_SH), lambda j, i: (0, j, i, 0)),
                   pl.BlockSpec((None, 8, FF_SH), lambda j, i: (j, 0, 0))],
        out_shape=[sd((2, N_SHARD, t, FF_SH), BF16), sd((N_SHARD, 8, FF_SH), F32)],
        compiler_params=_cparams(("parallel", "arbitrary")), name="ffn_mid_bwd",
    )(gu, gu, gu, gu, gu, gu, dh, dh, dh, cw)


def _stack_heads(src_ref, dst_ref, tq):
    lo = lax.broadcasted_iota(jnp.int32, (tq, LANES), 1) < HEAD_DIM
    for i in range(4):
        blk = src_ref[:, LANES * i:LANES * (i + 1)].astype(dst_ref.dtype)
        zero = jnp.zeros_like(blk)
        dst_ref[tq * i:tq * (i + 1), :] = jnp.where(lo, blk, zero)
        dst_ref[tq * (4 + i):tq * (5 + i), :] = jnp.where(lo, zero, blk)


def _gattn_fwd(q, k, v, gather=(), tq=128, tk=2048):
    t = q.shape[0]
    tk = min(tk, t)
    nq, nk, r = t // tq, t // tk, 8 * tq
    ng = len(gather)

    def body(*refs):
        q_ref, k_ref, v_ref = refs[:3]
        o_ref, lse_ref = refs[3 + ng:5 + ng]
        qst, m_s, l_s, acct = refs[5 + 2 * ng:9 + 2 * ng]
        if ng:
            ex = _gather_exchange(refs[3:3 + ng], refs[5 + ng:5 + 2 * ng], *refs[9 + 2 * ng:])
            pl.when(pl.program_id(0) == 0)(ex.start)
        lo_rows = lax.broadcasted_iota(jnp.int32, (LANES, tq), 0) < HEAD_DIM
        for i in range(4):
            bt = q_ref[:, LANES * i:LANES * (i + 1)].astype(F32).T
            qst[:, tq * i:tq * (i + 1)] = jnp.where(lo_rows, bt, 0.0).astype(BF16)
            qst[:, tq * (4 + i):tq * (5 + i)] = jnp.where(lo_rows, 0.0, bt).astype(BF16)
        m_s[...] = jnp.full_like(m_s, NEG)

        def max_step(j, carry):
            off = pl.multiple_of(j * tk, tk)
            st = _dot(k_ref[pl.ds(off, tk), :], qst[...], _NN)
            m_s[...] = jnp.maximum(m_s[...], jnp.max(st.reshape(tk // 8, 8, r), axis=0))
            return carry

        lax.fori_loop(0, nk, max_step, 0)
        m_row = jnp.max(m_s[...], axis=0, keepdims=True)
        l_s[...] = jnp.zeros_like(l_s)
        acct[...] = jnp.zeros_like(acct)

        def sum_step(j, carry):
            off = pl.multiple_of(j * tk, tk)
            st = _dot(k_ref[pl.ds(off, tk), :], qst[...], _NN)
            pt = jnp.exp(st - m_row)
            l_s[...] += jnp.sum(pt.reshape(tk // 8, 8, r), axis=0)
            acct[...] += _dot(v_ref[j], pt, _NN)
            return carry

        lax.fori_loop(0, nk, sum_step, 0)
        l_row = jnp.sum(l_s[...], axis=0, keepdims=True)
        ot = acct[...] / l_row
        for i in range(4):
            pair_t = jnp.where(lo_rows, ot[:, tq * i:tq * (i + 1)], ot[:, tq * (4 + i):tq * (5 + i)])
            o_ref[:, LANES * i:LANES * (i + 1)] = pair_t.T
        lse_ref[...] = m_row + jnp.log(l_row)
        if ng:
            pl.when(pl.program_id(0) == nq - 1)(ex.wait)

    sd = jax.ShapeDtypeStruct
    vt3 = v.reshape(nk, tk, KV_W).transpose(0, 2, 1)
    return pl.pallas_call(
        body, grid=(nq,),
        in_specs=[_row_spec(tq, Q_W), _par_spec(KV_W, t), pl.BlockSpec((nk, KV_W, tk), lambda i: (0, 0, 0))]
        + [_ANY] * ng,
        out_specs=[_row_spec(tq, Q_W), pl.BlockSpec((None, 1, r), lambda i: (i, 0, 0))] + [_ANY] * ng,
        out_shape=[sd((t, Q_W), F32), sd((nq, 1, r), F32)] + _gathered_shapes(gather),
        scratch_shapes=[pltpu.VMEM((LANES, r), BF16), pltpu.VMEM((8, r), F32), pltpu.VMEM((8, r), F32),
                        pltpu.VMEM((LANES, r), F32)] + (_exchange_sems(ng) if ng else []),
        compiler_params=_cparams(("arbitrary",) if ng else ("parallel",)),
        name="gattn_fwd_gather" if ng else "gattn_fwd",
    )(q, k, vt3, *gather)


def _gattn_bwd(q, k, v, o, do, lse, scatter=None, tq=128, tk=1024):
    t = q.shape[0]
    tk = min(tk, t)
    nq, nk, r = t // tq, t // tk, 8 * tq
    items, sgrads = scatter if scatter else ((), ())
    ns = len(sgrads)
    slot_shapes = []
    for j, (o_idx, _) in enumerate(items):
        if o_idx == len(slot_shapes):
            slot_shapes += _slot_shapes([sgrads[j]])
    nslots = len(slot_shapes)

    n_in, n_scr = 7, 6
    kt3 = k.reshape(nk, tk, KV_W).transpose(0, 2, 1)

    def body(*refs):
        q_ref, k_ref, v_ref, kt_ref, o_ref, do_ref, lse_ref = refs[:n_in]
        dq_ref, dk_ref, dv_ref = refs[n_in + ns:n_in + 3 + ns]
        scr = n_in + 3 + ns + nslots
        qs, dos, qst, dost, dlt_row, dqt = refs[scr:scr + n_scr]
        if ns:
            ex = _scatter_exchange(items, refs[n_in:n_in + ns], refs[n_in + 3 + ns:scr], *refs[scr + n_scr:])
            pl.when(pl.program_id(0) == 0)(ex.start)

        @pl.when(pl.program_id(0) == 0)
        def _():
            dk_ref[...] = jnp.zeros_like(dk_ref)
            dv_ref[...] = jnp.zeros_like(dv_ref)

        _stack_heads(q_ref, qs, tq)
        _stack_heads(do_ref, dos, tq)
        lo_rows = lax.broadcasted_iota(jnp.int32, (LANES, tq), 0) < HEAD_DIM
        for i in range(4):
            lo, hi = slice(tq * i, tq * (i + 1)), slice(tq * (4 + i), tq * (5 + i))
            cols = slice(LANES * i, LANES * (i + 1))
            for src, dst in ((q_ref, qst), (do_ref, dost)):
                bt = src[:, cols].astype(F32).T
                dst[:, lo] = jnp.where(lo_rows, bt, 0.0).astype(BF16)
                dst[:, hi] = jnp.where(lo_rows, 0.0, bt).astype(BF16)
            prod_t = (do_ref[:, cols] * o_ref[:, cols]).T
            dlt_row[:, lo] = jnp.sum(prod_t[:HEAD_DIM], axis=0, keepdims=True)
            dlt_row[:, hi] = jnp.sum(prod_t[HEAD_DIM:], axis=0, keepdims=True)
        lse_row = lse_ref[...]
        dqt[...] = jnp.zeros_like(dqt)

        def step(j, carry):
            off = pl.multiple_of(j * tk, tk)
            kc = k_ref[pl.ds(off, tk), :]
            vc = v_ref[pl.ds(off, tk), :]
            p = jnp.exp(_dot(kc, qst[...], _NN) - lse_row)
            dp = _dot(vc, dost[...], _NN)
            ds = (p * (dp - dlt_row[...])).astype(BF16)
            dk_ref[pl.ds(off, tk), :] += _dot(ds, qs[...], _NN)
            dv_ref[pl.ds(off, tk), :] += _dot(p, dos[...], _NN)
            dqt[...] += _dot(kt_ref[j], ds, _NN)
            return carry

        lax.fori_loop(0, nk, step, 0)
        for i in range(4):
            pair_t = jnp.where(lo_rows, dqt[:, tq * i:tq * (i + 1)], dqt[:, tq * (4 + i):tq * (5 + i)])
            dq_ref[:, LANES * i:LANES * (i + 1)] = pair_t.T
        if ns:
            pl.when(pl.program_id(0) == nq - 1)(ex.wait)

    sd = jax.ShapeDtypeStruct
    return pl.pallas_call(
        body, grid=(nq,),
        in_specs=[_row_spec(tq, Q_W), _par_spec(KV_W, t), _par_spec(KV_W, t),
                  pl.BlockSpec((nk, KV_W, tk), lambda i: (0, 0, 0)), _row_spec(tq, Q_W), _row_spec(tq, Q_W),
                  pl.BlockSpec((None, 1, r), lambda i: (i, 0, 0))] + [_ANY] * ns,
        out_specs=[_row_spec(tq, Q_W), _par_spec(KV_W, t), _par_spec(KV_W, t)] + [_ANY] * nslots,
        out_shape=[sd((t, Q_W), F32), sd((t, KV_W), F32), sd((t, KV_W), F32)] + slot_shapes,
        scratch_shapes=[pltpu.VMEM((r, LANES), BF16), pltpu.VMEM((r, LANES), BF16), pltpu.VMEM((LANES, r), BF16),
                        pltpu.VMEM((LANES, r), BF16), pltpu.VMEM((1, r), F32),
                        pltpu.VMEM((LANES, r), F32)] + (_exchange_sems(ns) if ns else []),
        compiler_params=_cparams(("arbitrary",)), name="gattn_bwd_scatter" if ns else "gattn_bwd",
    )(q, k, v, kt3, o, do, lse, *sgrads)


_WQ = Q_BLOCK
_WK = 3 * Q_BLOCK
_WR = 8 * _WQ


def _pairs_transposed(src_ref, dst, tq):
    lo_rows = lax.broadcasted_iota(jnp.int32, (LANES, tq), 0) < HEAD_DIM
    for i in range(4):
        bt = src_ref[:, LANES * i:LANES * (i + 1)].astype(F32).T
        dst[:, tq * i:tq * (i + 1)] = jnp.where(lo_rows, bt, 0.0).astype(BF16)
        dst[:, tq * (4 + i):tq * (5 + i)] = jnp.where(lo_rows, 0.0, bt).astype(BF16)


def _pairs_from_transposed(halves, dst_ref, tq):
    for i in range(4):
        pair_t = jnp.concatenate([h[:, tq * i:tq * (i + 1)] for h in halves], axis=0)
        dst_ref[:, LANES * i:LANES * (i + 1)] = pair_t.T.astype(dst_ref.dtype)


def _kv_quadrants(tq):
    return [(slice(HEAD_DIM * kv, HEAD_DIM * (kv + 1)), slice(4 * tq * kv, 4 * tq * (kv + 1))) for kv in range(2)]


def _wattn_scores_t(kw, qst, bias_ref, n, t):
    kabs = (n - 1) * _WQ + lax.broadcasted_iota(jnp.int32, (_WK, 1), 0)
    st = _dot(kw, qst[...], _NN) + bias_ref[...]
    return jnp.where((kabs >= 0) & (kabs < t), st, NEG)


def _window_t(ref3, n):
    return jnp.concatenate([ref3[n], ref3[n + 1], ref3[n + 2]], axis=1)


def _blocks_transposed(ap):
    return ap.reshape(ap.shape[0] // _WQ, _WQ, KV_W).transpose(0, 2, 1)


def _wattn_fwd(q, kp, vp, bias_t, sink):
    t = q.shape[0]
    nq = t // _WQ
    tp = t + 2 * _WQ
    vpt = _blocks_transposed(vp)

    def body(q_ref, k_ref, vt_ref, b_ref, sk_ref, o_ref, lse_ref, qst):
        n = pl.program_id(0)
        _pairs_transposed(q_ref, qst, _WQ)
        kw = k_ref[pl.ds(pl.multiple_of(n * _WQ, _WQ), _WK), :]
        st = _wattn_scores_t(kw, qst, b_ref, n, t)
        sk = sk_ref[...]
        m = jnp.maximum(jnp.max(st, axis=0, keepdims=True), sk)
        pt = jnp.exp(st - m)
        l = jnp.sum(pt, axis=0, keepdims=True) + jnp.exp(sk - m)
        vwt = _window_t(vt_ref, n)
        halves = [_dot(vwt[rows, :], pt[:, cols], _NN) / l[:, cols] for rows, cols in _kv_quadrants(_WQ)]
        _pairs_from_transposed(halves, o_ref, _WQ)
        lse_ref[...] = m + jnp.log(l)

    sd = jax.ShapeDtypeStruct
    return pl.pallas_call(
        body, grid=(nq,),
        in_specs=[_row_spec(_WQ, Q_W), _par_spec(KV_W, tp), pl.BlockSpec(vpt.shape, lambda i: (0, 0, 0)),
                  _par_spec(_WR, _WK), _par_spec(_WR)],
        out_specs=[_row_spec(_WQ, Q_W), pl.BlockSpec((None, 1, _WR), lambda i: (i, 0, 0))],
        out_shape=[sd((t, Q_W), F32), sd((nq, 1, _WR), F32)],
        scratch_shapes=[pltpu.VMEM((LANES, _WR), BF16)],
        compiler_params=_cparams(("parallel",)), name="wattn_fwd",
    )(q, kp, vpt, bias_t, sink)


def _wattn_bwd(q, kp, vp, bias_t, sink, o, do, lse):
    t = q.shape[0]
    nq = t // _WQ
    tp = t + 2 * _WQ
    kpt = _blocks_transposed(kp)

    def body(q_ref, k_ref, v_ref, kt_ref, b_ref, sk_ref, o_ref, do_ref, lse_ref, dq_ref, dk_ref, dv_ref, db_ref,
             dsk_ref, qs, dos, qst, dost):
        n = pl.program_id(0)

        @pl.when(n == 0)
        def _():
            dk_ref[...] = jnp.zeros_like(dk_ref)
            dv_ref[...] = jnp.zeros_like(dv_ref)
            db_ref[...] = jnp.zeros_like(db_ref)
            dsk_ref[...] = jnp.zeros_like(dsk_ref)

        _stack_heads(q_ref, qs, _WQ)
        _stack_heads(do_ref, dos, _WQ)
        _pairs_transposed(q_ref, qst, _WQ)
        _pairs_transposed(do_ref, dost, _WQ)
        delta = []
        for i in range(4):
            cols = slice(LANES * i, LANES * (i + 1))
            prod_t = (do_ref[:, cols] * o_ref[:, cols]).T
            delta.append((jnp.sum(prod_t[:HEAD_DIM], axis=0, keepdims=True),
                          jnp.sum(prod_t[HEAD_DIM:], axis=0, keepdims=True)))
        dlt = jnp.concatenate([d[0] for d in delta] + [d[1] for d in delta], axis=1)
        off = pl.multiple_of(n * _WQ, _WQ)
        kw = k_ref[pl.ds(off, _WK), :]
        vw = v_ref[pl.ds(off, _WK), :]
        lse_v = lse_ref[...]
        pt = jnp.exp(_wattn_scores_t(kw, qst, b_ref, n, t) - lse_v)
        dpt = _dot(vw, dost[...], _NN)
        ds = pt * (dpt - dlt)
        db_ref[...] += ds
        dsk_ref[...] -= jnp.exp(sk_ref[...] - lse_v) * dlt
        dsb = ds.astype(BF16)
        dk_ref[pl.ds(off, _WK), :] += _dot(dsb, qs[...], _NN)
        dv_ref[pl.ds(off, _WK), :] += _dot(pt, dos[...], _NN)
        kwt = _window_t(kt_ref, n)
        halves = [_dot(kwt[rows, :], dsb[:, cols], _NN) for rows, cols in _kv_quadrants(_WQ)]
        _pairs_from_transposed(halves, dq_ref, _WQ)

    sd = jax.ShapeDtypeStruct
    qb = _row_spec(_WQ, Q_W)
    return pl.pallas_call(
        body, grid=(nq,),
        in_specs=[qb, _par_spec(KV_W, tp), _par_spec(KV_W, tp), pl.BlockSpec(kpt.shape, lambda i: (0, 0, 0)),
                  _par_spec(_WR, _WK), _par_spec(_WR), qb, qb, pl.BlockSpec((None, 1, _WR), lambda i: (i, 0, 0))],
        out_specs=[qb, _par_spec(KV_W, tp), _par_spec(KV_W, tp), _par_spec(_WR, _WK), _par_spec(_WR)],
        out_shape=[sd((t, Q_W), F32), sd((tp, KV_W), F32), sd((tp, KV_W), F32), sd((_WK, _WR), F32), sd((1, _WR), F32)],
        scratch_shapes=[pltpu.VMEM((_WR, LANES), BF16), pltpu.VMEM((_WR, LANES), BF16), pltpu.VMEM((LANES, _WR), BF16),
                        pltpu.VMEM((LANES, _WR), BF16)],
        compiler_params=_cparams(("arbitrary",)), name="wattn_bwd",
    )(q, kp, vp, kpt, bias_t, sink, o, do, lse)


def _bias_bucket_reduce(db0, db1, bucket):
    def body(a_ref, b_ref, bk_ref, o_ref):
        d = a_ref[...] + b_ref[...]
        bk = bk_ref[...]
        lane = lax.broadcasted_iota(jnp.int32, (1, LANES), 1)
        out = jnp.zeros((1, LANES), F32)
        for b in range(N_BUCKETS):
            tot = jnp.sum(jnp.sum(jnp.where(bk == b, d, 0.0), axis=-1, keepdims=True), axis=0, keepdims=True)
            out = out + jnp.where(lane == b, tot, 0.0)
        o_ref[...] = out

    hb = pl.BlockSpec((None, _WQ, _WK), lambda h: (h, 0, 0))
    return pl.pallas_call(
        body, grid=(8,), in_specs=[hb, hb, pl.BlockSpec((_WQ, _WK), lambda h: (0, 0))],
        out_specs=pl.BlockSpec((None, 1, LANES), lambda h: (h, 0, 0)),
        out_shape=jax.ShapeDtypeStruct((8, 1, LANES), F32),
        compiler_params=_cparams(("parallel",)), name="bias_bucket_reduce",
    )(db0.reshape(8, _WQ, _WK), db1.reshape(8, _WQ, _WK), bucket)


def _rope_tables(t):
    rows_n = t // GRID_W
    row = jnp.repeat(jnp.arange(rows_n, dtype=F32), GRID_W)
    col = jnp.tile(jnp.arange(GRID_W, dtype=F32), rows_n)
    half = HEAD_DIM // 2
    inv_freq = ROPE_THETA ** (-jnp.arange(0, half, 2, dtype=F32) / half)
    ang = jnp.concatenate([row[:, None] * inv_freq, col[:, None] * inv_freq], axis=-1)
    cos, sin = jnp.cos(ang), jnp.sin(ang)
    c64 = jnp.repeat(cos, 2, axis=-1)
    s64 = jnp.stack([-sin, sin], axis=-1).reshape(t, HEAD_DIM)
    return jnp.tile(c64, (1, 2)), jnp.tile(s64, (1, 2))


def _t5_bucket(rel):
    half = N_BUCKETS // 2
    max_exact = half // 2
    bucket = jnp.where(rel > 0, half, 0)
    rp = jnp.abs(rel)
    rpf = jnp.maximum(rp, 1).astype(jnp.float32)
    large = max_exact + (jnp.log(rpf / max_exact) / math.log(MAX_DISTANCE / max_exact)
                         * (half - max_exact)).astype(jnp.int32)
    large = jnp.minimum(large, half - 1)
    return bucket + jnp.where(rp < max_exact, rp, large)


def _window_tables(rel_bias):
    qpos = jnp.arange(_WQ, dtype=jnp.int32)
    kpos = jnp.arange(_WK, dtype=jnp.int32) - _WQ
    rel = kpos[None, :] - qpos[:, None]
    bucket = _t5_bucket(rel)
    bias = jnp.zeros((8, _WQ, _WK), F32)
    for b in range(N_BUCKETS):
        bias = jnp.where((bucket == b)[None], rel_bias[b][:, None, None], bias)
    bias = jnp.where((jnp.abs(rel) <= WINDOW)[None], bias, NEG)
    return bias.reshape(_WR, _WK).T, bucket


def _pad_rows(a):
    return jnp.pad(a, ((_WQ, _WQ), (0, 0)))


def _layer_fwd(x, p, tabs, gather=None):
    cos_t, sin_t, bias = tabs
    h = _mm_nn(x, p["win"], F32, "in_proj")
    qa, ka, va, qb, kb, vb = _prep_fwd(h, cos_t, sin_t, p["qn"], p["kn"])
    if gather is None:
        oa, lse_a = _gattn_fwd(qa, ka, va)
    else:
        oa, lse_a, *gathered = _gattn_fwd(qa, ka, va, gather=gather[0])
        p = gather[1](gathered)
    kbp, vbp = _pad_rows(kb), _pad_rows(vb)
    ob, lse_b = _wattn_fwd(qb, kbp, vbp, bias, p["sink"])
    y = _outnorm_fwd(oa, ob, p["ga"], p["gb"])
    z1, x1, x1b = _mm_reduce(y[None], p["wout"][None], _NN, F32, "out_proj", res=x, res_scale=ALPHA,
                             ln=(p["ln1g"], p["ln1b"]))
    gu = _mm_expand(x1b, p["wgu"], _NN, BF16, "gate_up_proj")
    hdn = _ffn_mid_fwd(gu, p["cw"])
    z2, x2, _ = _mm_reduce(hdn, p["wd"], _NN, F32, "down_proj", res=x1, res_scale=ALPHA, ln=(p["ln2g"], p["ln2b"]))
    saved = dict(x=x, h=h, qa=qa, ka=ka, va=va, qb=qb, kbp=kbp, vbp=vbp, oa=oa, ob=ob, lse_a=lse_a, lse_b=lse_b,
                 y=y, z1=z1, x1b=x1b, gu=gu, hdn=hdn, z2=z2)
    return x2, saved


def _block_grads(g, names=("w_in", "w_out", "w_gate", "w_up", "w_down")):
    make = dict(
        w_in=lambda: _col_blocks(_in_cols_to_pairs(g["win"], _from_pairs), IN_SH),
        w_out=lambda: _mix_rows_to_pairs(g["wout"], _from_pairs).reshape(N_SHARD, OUT_SH, D_MODEL),
        w_gate=lambda: g["wg"], w_up=lambda: g["wu"], w_down=lambda: g["wd"])
    return [make[n]() for n in names]


def _layer_bwd(dx2, p, s, tabs, layer=0, pending=None):
    cos_t, sin_t, bias = tabs
    t = dx2.shape[0]
    dz2, dz2b, dln2g, dln2b = _ln_bwd(dx2, s["z2"], p["ln2g"])
    dhdn = _mm_expand(dz2b, p["wd"], _NT, BF16, "down_dx")
    dwd = _mm_tn_blocks(s["hdn"], dz2b, "down_dw")
    dgu, stats = _ffn_mid_bwd(s["gu"], dhdn, p["cw"])
    dgu = dgu.reshape(2 * N_SHARD, t, FF_SH)
    dx1 = _mm_reduce(dgu, p["wgu"], _NT, F32, "gate_up_dx", res=dz2, res_scale=ALPHA)
    dwg = _mm_tn_blocks(s["x1b"], dgu, "gate_dw", blk=0)
    dwu = _mm_tn_blocks(s["x1b"], dgu, "up_dw", blk=1)
    dz1, dz1b, dln1g, dln1b = _ln_bwd(dx1, s["z1"], p["ln1g"])
    dy = _mm_nt(dz1b, p["wout"], F32, "out_dx")
    dwout = _mm_tn(s["y"], dz1b, "out_dw")
    doa, dob, dga, dgb = _outnorm_bwd(dy, s["oa"], s["ob"], p["ga"], p["gb"])
    slots = None
    if pending is None:
        dqa, dka, dva = _gattn_bwd(s["qa"], s["ka"], s["va"], s["oa"], doa, s["lse_a"])
    else:
        mine = _block_grads(dict(wout=dwout, wg=dwg, wu=dwu, wd=dwd), ("w_out", "w_gate", "w_up", "w_down"))
        todo = list(pending) + [(o + 1, layer, g) for o, g in enumerate(mine)]
        dqa, dka, dva, *slots = _gattn_bwd(s["qa"], s["ka"], s["va"], s["oa"], doa, s["lse_a"],
                                           scatter=([(o, l) for o, l, _ in todo], [g for _, _, g in todo]))
    dqb, dkbp, dvbp, dbias, dsink = _wattn_bwd(s["qb"], s["kbp"], s["vbp"], bias, p["sink"], s["ob"], dob, s["lse_b"])
    dkb = lax.slice_in_dim(dkbp, _WQ, _WQ + t, axis=0)
    dvb = lax.slice_in_dim(dvbp, _WQ, _WQ + t, axis=0)
    dh, dqn, dkn = _prep_bwd(s["h"], cos_t, sin_t, p["qn"], p["kn"], dqa, dka, dva, dqb, dkb, dvb)
    dx = _mm_nt(dh, p["win"], F32, "in_dx", res=dz1, res_scale=ALPHA)
    dwin = _mm_tn(s["x"], dh, "in_dw")
    grads = dict(win=dwin, wout=dwout, wg=dwg, wu=dwu, wd=dwd, stats=stats, qn=dqn, kn=dkn, ga=dga, gb=dgb,
                 ln1g=dln1g, ln1b=dln1b, ln2g=dln2g, ln2b=dln2b, bias=dbias, sink=dsink, slots=slots)
    return dx, grads


def _prep_layer_params(l, win, wout, wg, wu, wd, cw, q_norm, k_norm, sink, out_norm_a, out_norm_b, conv_b,
                       ln1_g, ln1_b, ln2_g, ln2_b):
    win_full = win.transpose(1, 0, 2).reshape(D_MODEL, IN_COLS)
    row = lambda v: v.reshape(1, -1)
    late = {}
    if wout is not None:
        late = dict(
            wout=_mix_rows_to_pairs(wout.reshape(D_MODEL, D_MODEL)), wgu=jnp.concatenate([wg, wu], axis=0), wd=wd,
            cw=jnp.pad(cw, ((0, 0), (0, 5), (0, 0)))
            + jnp.pad(conv_b[l].reshape(N_SHARD, 1, FF_SH), ((0, 0), (3, 4), (0, 0))))
    return dict(
        late, win=_in_cols_to_pairs(win_full),
        qn=row(jnp.tile(q_norm[l], 2)), kn=row(jnp.tile(k_norm[l], 2)),
        ga=row(_to_pairs(out_norm_a[l], 0)), gb=row(_to_pairs(out_norm_b[l], 0)),
        ln1g=row(ln1_g[l]), ln1b=row(ln1_b[l]), ln2g=row(ln2_g[l]), ln2b=row(ln2_b[l]),
        sink=jnp.repeat(sink[l], _WQ).reshape(1, _WR))


def _local_step(x, tgt, params, rel_bias, gather=None, scatter=False):
    t = x.shape[0]
    cos_t, sin_t = _rope_tables(t)
    bias, bucket = _window_tables(rel_bias)
    tabs = (cos_t, sin_t, bias)
    saved = []
    for l in range(DEPTH):
        x, s = _layer_fwd(x, params[l], tabs, gather if l == 0 else None)
        saved.append(s)
    dx, loss = _loss_grad(x, tgt)
    grads = [None] * DEPTH
    for l in reversed(range(DEPTH)):
        pending = None
        if scatter and l == 0:
            pending = [(o, 1, g) for o, g in enumerate(_block_grads(grads[1]))]
        dx, grads[l] = _layer_bwd(dx, params[l], saved[l], tabs, l, pending)
    dbucket = _bias_bucket_reduce(grads[0]["bias"].T, grads[1]["bias"].T, bucket)
    return loss, dx, grads, dbucket


_ANY = pl.BlockSpec(memory_space=pl.ANY)
_MESH = pl.DeviceIdType.MESH


def _mesh_pos():
    return lax.axis_index("x"), lax.axis_index("y"), lax.axis_index("c")


def _other_chips(x, y):
    return [(1 - x, y), (x, 1 - y), (1 - x, 1 - y)]


class _Exchange:
    def __init__(self, local, sends, recvs):
        self.local, self.sends, self.recvs = local, sends, recvs

    def start(self):
        for cp in self.local + self.sends:
            cp.start()

    def wait(self):
        for cp in self.recvs:
            cp.wait_recv()
        for cp in self.sends:
            cp.wait_send()
        for cp in self.local:
            cp.wait()


def _exchange_sems(n):
    return [pltpu.SemaphoreType.DMA((n, 3)), pltpu.SemaphoreType.DMA((n, 3)), pltpu.SemaphoreType.DMA((n,))]


def _gather_exchange(ins, outs, send, recv, loc):
    x, y, c = _mesh_pos()
    me = 2 * x + y
    chips = _other_chips(x, y)

    def remote(i, k, block):
        px, py = chips[k]
        return pltpu.make_async_remote_copy(ins[i], outs[i].at[block], send.at[i, k], recv.at[i, k],
                                            device_id=(px, py, c), device_id_type=_MESH)

    n = len(ins)
    local = [pltpu.make_async_copy(ins[i], outs[i].at[me], loc.at[i]) for i in range(n)]
    sends = [remote(i, k, me) for i in range(n) for k in range(3)]
    recvs = [remote(i, k, 2 * chips[k][0] + chips[k][1]) for i in range(n) for k in range(3)]
    return _Exchange(local, sends, recvs)


def _scatter_exchange(items, ins, outs, send, recv, loc):
    x, y, c = _mesh_pos()
    me = 2 * x + y
    chips = _other_chips(x, y)

    def remote(j, k):
        o, l = items[j]
        px, py = chips[k]
        return pltpu.make_async_remote_copy(ins[j].at[2 * px + py], outs[o].at[k, l], send.at[j, k], recv.at[j, k],
                                            device_id=(px, py, c), device_id_type=_MESH)

    local = [pltpu.make_async_copy(ins[j].at[me], outs[o].at[3, l], loc.at[j]) for j, (o, l) in enumerate(items)]
    sends = [remote(j, k) for j in range(len(items)) for k in range(3)]
    return _Exchange(local, sends, sends)


def _gathered_shapes(shards):
    return [jax.ShapeDtypeStruct((N_SHARD,) + s.shape, s.dtype) for s in shards]


def _slot_shapes(blocks):
    return [jax.ShapeDtypeStruct((N_SHARD, DEPTH) + g.shape[1:], g.dtype) for g in blocks]


def _gather_shards(shards):
    n = len(shards)

    def body(*refs):
        ex = _gather_exchange(refs[:n], refs[n:2 * n], *refs[2 * n:])
        ex.start()
        ex.wait()

    return pl.pallas_call(
        body, in_specs=[_ANY] * n, out_specs=[_ANY] * n, out_shape=_gathered_shapes(shards),
        scratch_shapes=_exchange_sems(n), name="gather_weights",
    )(*shards)


def _scatter_into(items, grads, slots):
    n, ns = len(grads), len(slots)

    def body(*refs):
        ex = _scatter_exchange(items, refs[:n], refs[n + ns:n + 2 * ns], *refs[n + 2 * ns:])
        ex.start()
        ex.wait()

    return pl.pallas_call(
        body, in_specs=[_ANY] * (n + ns), out_specs=[_ANY] * ns,
        out_shape=[jax.ShapeDtypeStruct(s.shape, s.dtype) for s in slots],
        input_output_aliases={n + i: i for i in range(ns)},
        scratch_shapes=_exchange_sems(n), name="scatter_grads",
    )(*grads, *slots)


def _swap_with_sibling(parts):
    n = len(parts)

    def body(*refs):
        ins, outs = refs[:n], refs[n:2 * n]
        send, recv = refs[2 * n:]
        x, y, c = _mesh_pos()
        copies = [pltpu.make_async_remote_copy(ins[i], outs[i], send.at[i], recv.at[i], device_id=(x, y, 1 - c),
                                               device_id_type=_MESH) for i in range(n)]
        for cp in copies:
            cp.start()
        for cp in copies:
            cp.wait_recv()
        for cp in copies:
            cp.wait_send()

    return pl.pallas_call(
        body, in_specs=[_ANY] * n, out_specs=[_ANY] * n,
        out_shape=[jax.ShapeDtypeStruct(p.shape, p.dtype) for p in parts],
        scratch_shapes=[pltpu.SemaphoreType.DMA((n,)), pltpu.SemaphoreType.DMA((n,))],
        name="swap_sibling",
    )(*parts)


N_DEV = 8


def _allreduce_small(packed):
    rows = packed.shape[0]

    def body(in_ref, out_ref, buf, send, recv, loc):
        x, y, c = _mesh_pos()
        me = 4 * x + 2 * y + c
        own = pltpu.make_async_copy(in_ref, buf.at[me], loc)
        own.start()

        def remote(m, block):
            peer = (x ^ (m >> 2), y ^ ((m >> 1) & 1), c ^ (m & 1))
            return pltpu.make_async_remote_copy(in_ref, buf.at[block], send.at[m - 1], recv.at[m - 1],
                                                device_id=peer, device_id_type=_MESH)

        sends = [remote(m, me) for m in range(1, N_DEV)]
        for cp in sends:
            cp.start()
        for m in range(1, N_DEV):
            remote(m, me ^ m).wait_recv()
        for cp in sends:
            cp.wait_send()
        own.wait()
        tot = buf[0]
        for d in range(1, N_DEV):
            tot = tot + buf[d]
        out_ref[...] = tot

    vm = pl.BlockSpec(memory_space=pltpu.VMEM)
    return pl.pallas_call(
        body, in_specs=[vm], out_specs=vm, out_shape=jax.ShapeDtypeStruct((rows, LANES), F32),
        scratch_shapes=[pltpu.VMEM((N_DEV, rows, LANES), F32), pltpu.SemaphoreType.DMA((N_DEV - 1,)),
                        pltpu.SemaphoreType.DMA((N_DEV - 1,)), pltpu.SemaphoreType.DMA(())],
        name="allreduce_small",
    )(packed)


def _shard_rows(r):
    return r // 2 if r % 32 == 0 else r


def _sum_slots(slots):
    _, _, r, cdim = slots.shape
    tr = _shard_rows(r)

    def body(a_ref, b_ref, c_ref, d_ref, o_ref):
        up = lambda ref: ref[...].astype(F32)
        o_ref[...] = ((up(d_ref) + up(a_ref)) + up(b_ref)) + up(c_ref)

    def spec(k):
        return pl.BlockSpec((None, None, tr, cdim), lambda l, i: (k, l, i, 0))

    return pl.pallas_call(
        body, grid=(DEPTH, r // tr), in_specs=[spec(0), spec(1), spec(2), spec(3)],
        out_specs=pl.BlockSpec((None, tr, cdim), lambda l, i: (l, i, 0)),
        out_shape=jax.ShapeDtypeStruct((DEPTH, r, cdim), F32),
        compiler_params=_cparams(("parallel", "parallel")), name="sum_slots",
    )(slots, slots, slots, slots)


def _adamw_math(w, g, m, v):
    m = ADAM_B1 * m + (1.0 - ADAM_B1) * g
    v = ADAM_B2 * v + (1.0 - ADAM_B2) * (g * g)
    m_hat = m / (1.0 - ADAM_B1 ** ADAM_STEP)
    v_hat = v / (1.0 - ADAM_B2 ** ADAM_STEP)
    delta = -ADAM_LR * (m_hat / (jnp.sqrt(v_hat) + ADAM_EPS) + ADAM_WD * w)
    return delta, m, v


def _adamw_big(ga, gb, w, m, v):
    _, r, cdim = w.shape
    tr = _shard_rows(r)

    def body(ga_ref, gb_ref, w_ref, m_ref, v_ref, g_out, d_out, m_out, v_out):
        g = ga_ref[...] + gb_ref[...]
        d, mn, vn = _adamw_math(w_ref[...], g, m_ref[...], v_ref[...])
        g_out[...] = g
        d_out[...] = d
        m_out[...] = mn
        v_out[...] = vn

    spec = pl.BlockSpec((None, tr, cdim), lambda l, i: (l, i, 0))
    shp = jax.ShapeDtypeStruct(w.shape, F32)
    return pl.pallas_call(
        body, grid=(DEPTH, r // tr), in_specs=[spec] * 5, out_specs=[spec] * 4, out_shape=[shp] * 4,
        compiler_params=_cparams(("parallel", "parallel")), name="adamw_big",
    )(ga, gb, w, m, v)


def _adamw_small(ws, gs, ms, vs):
    n = len(ws)

    def body(*refs):
        w_r, g_r, m_r, v_r = (refs[k * n:(k + 1) * n] for k in range(4))
        d_o, m_o, v_o = (refs[(4 + k) * n:(5 + k) * n] for k in range(3))
        for i in range(n):
            d, mn, vn = _adamw_math(w_r[i][...], g_r[i][...], m_r[i][...], v_r[i][...])
            d_o[i][...] = d
            m_o[i][...] = mn
            v_o[i][...] = vn

    vm = pl.BlockSpec(memory_space=pltpu.VMEM)
    shp = [jax.ShapeDtypeStruct(w.shape, F32) for w in ws]
    outs = pl.pallas_call(
        body, in_specs=[vm] * (4 * n), out_specs=[vm] * (3 * n), out_shape=shp * 3, name="adamw_small",
    )(*ws, *gs, *ms, *vs)
    return outs[:n], outs[n:2 * n], outs[2 * n:]


def _tile_rows(a):
    a = a.reshape(-1, LANES)
    pad = (-a.shape[0]) % 8
    return jnp.pad(a, ((0, pad), (0, 0))) if pad else a


_SMALL_LAYER_PARTS = (("qn", 8), ("kn", 8), ("sink", 8), ("ga", 8), ("gb", 8), ("ln1g", 8), ("ln1b", 8),
                      ("ln2g", 8), ("ln2b", 8), ("stats", N_SHARD * 8 * FF_SH // LANES))
_SMALL_HEAD_ROWS = 16
_SMALL_LAYER_ROWS = sum(r for _, r in _SMALL_LAYER_PARTS)


def _pack_small(loss, dbucket, grads):
    parts = [_tile_rows(loss), _tile_rows(dbucket)]
    for l in range(DEPTH):
        parts += [_tile_rows(grads[l][name]) for name, _ in _SMALL_LAYER_PARTS]
    return jnp.concatenate(parts, axis=0)


def _unpack_small(tot, chip):
    out = dict(loss=tot[0, 0], rel_bias=tot[8:16, :N_BUCKETS].T)
    per = {name: [] for name, _ in _SMALL_LAYER_PARTS}
    for l in range(DEPTH):
        base = _SMALL_HEAD_ROWS + l * _SMALL_LAYER_ROWS
        for name, rows in _SMALL_LAYER_PARTS:
            per[name].append(tot[base:base + rows])
            base += rows
    fold = lambda v: v[0, :HEAD_DIM] + v[0, HEAD_DIM:]
    out["q_norm"] = jnp.stack([fold(v) for v in per["qn"]])
    out["k_norm"] = jnp.stack([fold(v) for v in per["kn"]])
    out["sink"] = jnp.stack([jnp.sum(v, axis=1) for v in per["sink"]])
    out["out_norm_a"] = jnp.stack([_from_pairs(v[:4].reshape(Q_W), 0) for v in per["ga"]])
    out["out_norm_b"] = jnp.stack([_from_pairs(v[:4].reshape(Q_W), 0) for v in per["gb"]])
    for name, key in (("ln1_g", "ln1g"), ("ln1_b", "ln1b"), ("ln2_g", "ln2g"), ("ln2_b", "ln2b")):
        out[name] = jnp.stack([v.reshape(D_MODEL) for v in per[key]])
    stats = [v.reshape(N_SHARD, 8, FF_SH) for v in per["stats"]]
    out["conv_b"] = jnp.stack([s[:, 0, :].reshape(D_FF) for s in stats])
    out["conv_w"] = jnp.stack([lax.dynamic_index_in_dim(s, chip, 0, keepdims=False)[1:4] for s in stats])
    return out


_WEIGHTS = ("rel_bias", "w_in", "q_norm", "k_norm", "sink", "out_norm_a", "out_norm_b", "w_out", "ln1_g", "ln1_b",
            "w_gate", "w_up", "conv_w", "conv_b", "w_down", "ln2_g", "ln2_b")
_BIG = ("w_in", "w_out", "w_gate", "w_up", "w_down")
_SMALL = tuple(n for n in _WEIGHTS if n not in _BIG)


def _col_blocks(g, n):
    return g.reshape(g.shape[0], N_SHARD, n).transpose(1, 0, 2)


def kernel(x, rel_bias, w_in, q_norm, k_norm, sink, out_norm_a, out_norm_b, w_out, ln1_g, ln1_b, w_gate, w_up, conv_w, conv_b, w_down, ln2_g, ln2_b, loss_target, m_rel_bias, m_w_in, m_q_norm, m_k_norm, m_sink, m_out_norm_a, m_out_norm_b, m_w_out, m_ln1_g, m_ln1_b, m_w_gate, m_w_up, m_conv_w, m_conv_b, m_w_down, m_ln2_g, m_ln2_b, v_rel_bias, v_w_in, v_q_norm, v_k_norm, v_sink, v_out_norm_a, v_out_norm_b, v_w_out, v_ln1_g, v_ln1_b, v_w_gate, v_w_up, v_conv_w, v_conv_b, v_w_down, v_ln2_g, v_ln2_b):
    w = dict(rel_bias=rel_bias, w_in=w_in, q_norm=q_norm, k_norm=k_norm, sink=sink, out_norm_a=out_norm_a,
             out_norm_b=out_norm_b, w_out=w_out, ln1_g=ln1_g, ln1_b=ln1_b, w_gate=w_gate, w_up=w_up, conv_w=conv_w,
             conv_b=conv_b, w_down=w_down, ln2_g=ln2_g, ln2_b=ln2_b)
    m = dict(rel_bias=m_rel_bias, w_in=m_w_in, q_norm=m_q_norm, k_norm=m_k_norm, sink=m_sink, out_norm_a=m_out_norm_a,
             out_norm_b=m_out_norm_b, w_out=m_w_out, ln1_g=m_ln1_g, ln1_b=m_ln1_b, w_gate=m_w_gate, w_up=m_w_up,
             conv_w=m_conv_w, conv_b=m_conv_b, w_down=m_w_down, ln2_g=m_ln2_g, ln2_b=m_ln2_b)
    v = dict(rel_bias=v_rel_bias, w_in=v_w_in, q_norm=v_q_norm, k_norm=v_k_norm, sink=v_sink, out_norm_a=v_out_norm_a,
             out_norm_b=v_out_norm_b, w_out=v_w_out, ln1_g=v_ln1_g, ln1_b=v_ln1_b, w_gate=v_w_gate, w_up=v_w_up,
             conv_w=v_conv_w, conv_b=v_conv_b, w_down=v_w_down, ln2_g=v_ln2_g, ln2_b=v_ln2_b)
    chip = 2 * lax.axis_index("x") + lax.axis_index("y")

    small_w = (q_norm, k_norm, sink, out_norm_a, out_norm_b, conv_b, ln1_g, ln1_b, ln2_g, ln2_b)
    (win0,) = _gather_shards([w_in[0].astype(BF16)])
    later = ([w[name][0].astype(BF16) for name in _BIG[1:]] + [w[name][1].astype(BF16) for name in _BIG] + [conv_w])
    params = [_prep_layer_params(0, win0, None, None, None, None, None, *small_w), None]

    def finish(g):
        wout0, wg0, wu0, wd0, win1, wout1, wg1, wu1, wd1, cw_all = g
        params[0] = _prep_layer_params(0, win0, wout0, wg0, wu0, wd0, cw_all[:, 0], *small_w)
        params[1] = _prep_layer_params(1, win1, wout1, wg1, wu1, wd1, cw_all[:, 1], *small_w)
        return params[0]

    loss, dx, grads, dbucket = _local_step(x[0], loss_target[0], params, rel_bias, gather=(later, finish),
                                           scatter=True)

    small = _unpack_small(_allreduce_small(_pack_small(loss, dbucket, grads)), chip)

    slots = list(grads[0]["slots"])
    slots[0] = _scatter_into([(0, 0)], _block_grads(grads[0], ("w_in",)), [slots[0]])[0]
    partial = [_sum_slots(s) for s in slots]
    other = _swap_with_sibling(partial)

    grad, delta, new_m, new_v = {}, {}, {}, {}
    for i, name in enumerate(_BIG):
        grad[name], delta[name], new_m[name], new_v[name] = _adamw_big(partial[i], other[i], w[name], m[name], v[name])
    flat2 = lambda a: a.reshape(-1, a.shape[-1])
    ds, ms, vs = _adamw_small([flat2(w[n]) for n in _SMALL], [flat2(small[n]) for n in _SMALL],
                              [flat2(m[n]) for n in _SMALL], [flat2(v[n]) for n in _SMALL])
    for i, name in enumerate(_SMALL):
        grad[name] = small[name]
        delta[name] = ds[i].reshape(w[name].shape)
        new_m[name] = ms[i].reshape(w[name].shape)
        new_v[name] = vs[i].reshape(w[name].shape)

    return (small["loss"], dx[None], *[grad[n] for n in _WEIGHTS], *[delta[n] for n in _WEIGHTS],
            *[new_m[n] for n in _WEIGHTS], *[new_v[n] for n in _WEIGHTS])
```

```python
import math

import jax
import jax.numpy as jnp
from jax import lax
from jax.experimental import pallas as pl
from jax.experimental.pallas import tpu as pltpu

F32 = jnp.float32
BF16 = jnp.bfloat16

D_MODEL = 1024
DEPTH = 2
HEAD_DIM = 64
Q_W = 512
KV_W = 128
IN_COLS = 2 * (Q_W + 2 * KV_W)
N_SHARD = 4
IN_SH = IN_COLS // N_SHARD
OUT_SH = D_MODEL // N_SHARD
D_FF = 2816
FF_SH = D_FF // N_SHARD
Q_BLOCK = 128
WINDOW = 128
N_BUCKETS = 32
MAX_DISTANCE = 128
GRID_W = 64
ROPE_THETA = 10000.0
ALPHA = (2.0 * DEPTH) ** 0.25
RMS_EPS = 1e-6
LN_EPS = 1e-5
NEG = -1e30
LANES = 128
VMEM_LIMIT = 56 * 1024 * 1024

ADAM_LR = 0.001
ADAM_B1 = 0.9
ADAM_B2 = 0.999
ADAM_EPS = 1e-08
ADAM_WD = 0.01
ADAM_STEP = 10

_NN = (((1,), (0,)), ((), ()))
_NT = (((1,), (1,)), ((), ()))
_TN = (((0,), (0,)), ((), ()))


def _dot(a, b, dims):
    return lax.dot_general(a.astype(BF16), b.astype(BF16), dims, preferred_element_type=F32)


def _cparams(sem, vmem=VMEM_LIMIT):
    return pltpu.CompilerParams(dimension_semantics=sem, vmem_limit_bytes=vmem)


def _regroup(a, axis, n_outer, n_inner):
    shp = a.shape
    a = a.reshape(shp[:axis] + (n_outer, n_inner, HEAD_DIM) + shp[axis + 1:])
    return jnp.swapaxes(a, axis, axis + 1).reshape(shp)


def _to_pairs(a, axis):
    return _regroup(a, axis, 2, 4)


def _from_pairs(a, axis):
    return _regroup(a, axis, 4, 2)


def _in_cols_to_pairs(w, fn=_to_pairs):
    return jnp.concatenate([fn(w[..., :Q_W], w.ndim - 1), w[..., Q_W:Q_W + 2 * KV_W],
                            fn(w[..., Q_W + 2 * KV_W:2 * Q_W + 2 * KV_W], w.ndim - 1),
                            w[..., 2 * Q_W + 2 * KV_W:]], axis=-1)


def _mix_rows_to_pairs(w, fn=_to_pairs):
    return fn(w.reshape(2, Q_W, w.shape[-1]), 1).reshape(w.shape)


def _matmul(a, b, *, dims, grid, a_spec, b_spec, o_spec, out_shape, acc_shape, name, res=None,
            res_spec=None, res_scale=1.0):
    nk = grid[-1]
    kax = len(grid) - 1

    def body(*refs):
        if res is None:
            a_ref, b_ref, o_ref, acc = refs
            r_ref = None
        else:
            a_ref, b_ref, r_ref, o_ref, acc = refs
        k = pl.program_id(kax)

        @pl.when(k == 0)
        def _():
            acc[...] = jnp.zeros_like(acc)

        acc[...] += _dot(a_ref[...], b_ref[...], dims)

        @pl.when(k == nk - 1)
        def _():
            o = acc[...]
            if r_ref is not None:
                o = o + res_scale * r_ref[...]
            o_ref[...] = o.astype(o_ref.dtype)

    in_specs = [a_spec, b_spec] + ([res_spec] if res is not None else [])
    args = (a, b) + ((res,) if res is not None else ())
    sem = ("parallel",) * kax + ("arbitrary",)
    return pl.pallas_call(
        body, grid=grid, in_specs=in_specs, out_specs=o_spec, out_shape=out_shape,
        scratch_shapes=[pltpu.VMEM(acc_shape, F32)], compiler_params=_cparams(sem), name=name,
    )(*args)


def _mm_nn(a, b, out_dtype, name, tm=512, res=None, res_scale=1.0):
    m, kd = a.shape
    n = b.shape[1]
    return _matmul(
        a, b, dims=_NN, grid=(m // tm, 1),
        a_spec=pl.BlockSpec((tm, kd), lambda i, k: (i, 0)),
        b_spec=pl.BlockSpec((kd, n), lambda i, k: (0, 0)),
        o_spec=pl.BlockSpec((tm, n), lambda i, k: (i, 0)),
        out_shape=jax.ShapeDtypeStruct((m, n), out_dtype), acc_shape=(tm, n), name=name,
        res=res, res_spec=pl.BlockSpec((tm, n), lambda i, k: (i, 0)), res_scale=res_scale)


def _mm_nt(a, b, out_dtype, name, tm=512, res=None, res_scale=1.0):
    m, kd = a.shape
    n = b.shape[0]
    return _matmul(
        a, b, dims=_NT, grid=(m // tm, 1),
        a_spec=pl.BlockSpec((tm, kd), lambda i, k: (i, 0)),
        b_spec=pl.BlockSpec((n, kd), lambda i, k: (0, 0)),
        o_spec=pl.BlockSpec((tm, n), lambda i, k: (i, 0)),
        out_shape=jax.ShapeDtypeStruct((m, n), out_dtype), acc_shape=(tm, n), name=name,
        res=res, res_spec=pl.BlockSpec((tm, n), lambda i, k: (i, 0)), res_scale=res_scale)


def _mm_tn(a, b, name, tk=1024, tn=None, out_dtype=BF16):
    t, m = a.shape
    n = b.shape[1]
    tn = n if tn is None else tn
    tk = min(tk, t)
    return _matmul(
        a, b, dims=_TN, grid=(n // tn, t // tk),
        a_spec=pl.BlockSpec((tk, m), lambda j, k: (k, 0)),
        b_spec=pl.BlockSpec((tk, tn), lambda j, k: (k, j)),
        o_spec=pl.BlockSpec((m, tn), lambda j, k: (0, j)),
        out_shape=jax.ShapeDtypeStruct((m, n), out_dtype), acc_shape=(m, tn), name=name)


def _blocked_n(w, dims):
    return w.shape[2] if dims == _NN else w.shape[1]


def _mm_expand(a, w, dims, out_dtype, name, tm=512):
    m, kd = a.shape
    nb, n = w.shape[0], _blocked_n(w, dims)

    def body(a_ref, w_ref, o_ref):
        av = a_ref[...]
        for j in range(nb):
            o_ref[j] = _dot(av, w_ref[j], dims).astype(o_ref.dtype)

    return pl.pallas_call(
        body, grid=(m // tm,),
        in_specs=[pl.BlockSpec((tm, kd), lambda i: (i, 0)), pl.BlockSpec(w.shape, lambda i: (0, 0, 0))],
        out_specs=pl.BlockSpec((nb, tm, n), lambda i: (0, i, 0)),
        out_shape=jax.ShapeDtypeStruct((nb, m, n), out_dtype),
        compiler_params=_cparams(("parallel",)), name=name,
    )(a, w)


def _mm_reduce(a, w, dims, out_dtype, name, tm=512, res=None, res_scale=1.0, ln=None):
    nb, m, kd = a.shape
    n = _blocked_n(w, dims)
    n_in = 2 + (res is not None) + (2 if ln else 0)

    def body(*refs):
        a_ref, w_ref = refs[0], refs[1]
        acc = _dot(a_ref[0], w_ref[0], dims)
        for j in range(1, nb):
            acc = acc + _dot(a_ref[j], w_ref[j], dims)
        if res is not None:
            acc = acc + res_scale * refs[2][...]
        refs[n_in][...] = acc.astype(out_dtype)
        if ln:
            g_ref, b_ref = refs[n_in - 2], refs[n_in - 1]
            zc = acc - jnp.mean(acc, axis=-1, keepdims=True)
            r = lax.rsqrt(jnp.mean(zc * zc, axis=-1, keepdims=True) + LN_EPS)
            y = zc * r * g_ref[...] + b_ref[...]
            refs[n_in + 1][...] = y
            refs[n_in + 2][...] = y.astype(BF16)

    row = pl.BlockSpec((tm, n), lambda i: (i, 0))
    par = pl.BlockSpec((1, n), lambda i: (0, 0))
    sd = jax.ShapeDtypeStruct
    out = pl.pallas_call(
        body, grid=(m // tm,),
        in_specs=[pl.BlockSpec((nb, tm, kd), lambda i: (0, i, 0)), pl.BlockSpec(w.shape, lambda i: (0, 0, 0))]
        + ([row] if res is not None else []) + ([par, par] if ln else []),
        out_specs=[row] * (3 if ln else 1),
        out_shape=[sd((m, n), out_dtype)] + ([sd((m, n), F32), sd((m, n), BF16)] if ln else []),
        compiler_params=_cparams(("parallel",)), name=name,
    )(a, w, *((res,) if res is not None else ()), *(ln or ()))
    return out if ln else out[0]


def _mm_tn_blocks(a, b, name, blk=0, nb=N_SHARD, tk=1024, out_dtype=BF16):
    a3, b3 = a.ndim == 3, b.ndim == 3
    t, m, n = a.shape[-2], a.shape[-1], b.shape[-1]
    tk = min(tk, t)
    nsteps = t // tk

    def spec(blocked, width):
        if blocked:
            return pl.BlockSpec((nb, tk, width), lambda k: (blk, k, 0))
        return pl.BlockSpec((tk, width), lambda k: (k, 0))

    def body(a_ref, b_ref, o_ref, acc):
        k = pl.program_id(0)

        @pl.when(k == 0)
        def _():
            acc[...] = jnp.zeros_like(acc)

        for j in range(nb):
            acc[j] += _dot(a_ref[j] if a3 else a_ref[...], b_ref[j] if b3 else b_ref[...], _TN)

        @pl.when(k == nsteps - 1)
        def _():
            o_ref[...] = acc[...].astype(o_ref.dtype)

    return pl.pallas_call(
        body, grid=(nsteps,), in_specs=[spec(a3, m), spec(b3, n)],
        out_specs=pl.BlockSpec((nb, m, n), lambda k: (0, 0, 0)),
        out_shape=jax.ShapeDtypeStruct((nb, m, n), out_dtype),
        scratch_shapes=[pltpu.VMEM((nb, m, n), F32)],
        compiler_params=_cparams(("arbitrary",)), name=name,
    )(a, b)


def _row_spec(tm, n):
    return pl.BlockSpec((tm, n), lambda i: (i, 0))


def _par_spec(n, rows=1):
    return pl.BlockSpec((rows, n), lambda i: (0, 0))


def _swap_pairs(x):
    lane = lax.broadcasted_iota(jnp.int32, x.shape, 1)
    return jnp.where(lane % 2 == 0, pltpu.roll(x, LANES - 1, 1), pltpu.roll(x, 1, 1))


def _head_sums(v):
    lo = lax.broadcasted_iota(jnp.int32, v.shape, 1) < HEAD_DIM
    s_lo = jnp.sum(jnp.where(lo, v, 0.0), axis=-1, keepdims=True)
    s_hi = jnp.sum(jnp.where(lo, 0.0, v), axis=-1, keepdims=True)
    return jnp.where(lo, s_lo, s_hi)


def _qk_blocks():
    return [(128 * i, True) for i in range(4)] + [(Q_W, False)]


def _prep_fwd(h, cos_t, sin_t, qn, kn, tm=256):
    t = h.shape[0]
    scale = HEAD_DIM ** -0.5

    def body(h_ref, c_ref, s_ref, qn_ref, kn_ref, qa_ref, ka_ref, va_ref, qb_ref, kb_ref, vb_ref):
        c = c_ref[...]
        s = s_ref[...]
        for start, is_q in _qk_blocks():
            x = h_ref[:, start:start + LANES]
            r = lax.rsqrt(_head_sums(x * x) * (1.0 / HEAD_DIM) + RMS_EPS)
            y = x * r * (qn_ref[...] if is_q else kn_ref[...])
            y = y * c + _swap_pairs(y) * s
            if is_q:
                qa_ref[:, start:start + LANES] = (y * scale).astype(BF16)
            else:
                ka_ref[...] = y.astype(BF16)
        va_ref[...] = h_ref[:, 640:768].astype(BF16)
        qb_ref[...] = (h_ref[:, 768:1280] * scale).astype(BF16)
        kb_ref[...] = h_ref[:, 1280:1408].astype(BF16)
        vb_ref[...] = h_ref[:, 1408:1536].astype(BF16)

    sd = jax.ShapeDtypeStruct
    return pl.pallas_call(
        body, grid=(t // tm,),
        in_specs=[_row_spec(tm, IN_COLS), _row_spec(tm, LANES), _row_spec(tm, LANES), _par_spec(LANES), _par_spec(LANES)],
        out_specs=[_row_spec(tm, Q_W), _row_spec(tm, KV_W), _row_spec(tm, KV_W),
                   _row_spec(tm, Q_W), _row_spec(tm, KV_W), _row_spec(tm, KV_W)],
        out_shape=[sd((t, Q_W), BF16), sd((t, KV_W), BF16), sd((t, KV_W), BF16),
                   sd((t, Q_W), BF16), sd((t, KV_W), BF16), sd((t, KV_W), BF16)],
        compiler_params=_cparams(("parallel",)), name="prep_fwd",
    )(h, cos_t, sin_t, qn, kn)


def _prep_bwd(h, cos_t, sin_t, qn, kn, dqa, dka, dva, dqb, dkb, dvb, tm=256):
    t = h.shape[0]
    scale = HEAD_DIM ** -0.5

    def body(h_ref, c_ref, s_ref, qn_ref, kn_ref, dqa_ref, dka_ref, dva_ref, dqb_ref, dkb_ref, dvb_ref,
             dh_ref, dqn_ref, dkn_ref):
        @pl.when(pl.program_id(0) == 0)
        def _():
            dqn_ref[...] = jnp.zeros_like(dqn_ref)
            dkn_ref[...] = jnp.zeros_like(dkn_ref)

        c = c_ref[...]
        s = s_ref[...]
        for start, is_q in _qk_blocks():
            x = h_ref[:, start:start + LANES]
            gain = qn_ref[...] if is_q else kn_ref[...]
            d = dqa_ref[:, start:start + LANES] * scale if is_q else dka_ref[...]
            dy = d * c + _swap_pairs(d * s)
            r = lax.rsqrt(_head_sums(x * x) * (1.0 / HEAD_DIM) + RMS_EPS)
            xr = x * r
            gsum = jnp.sum(dy * xr, axis=0, keepdims=True)
            if is_q:
                dqn_ref[...] += gsum
            else:
                dkn_ref[...] += gsum
            gy = dy * gain
            dx = r * (gy - xr * (_head_sums(xr * gy) * (1.0 / HEAD_DIM)))
            dh_ref[:, start:start + LANES] = dx.astype(BF16)
        dh_ref[:, 640:768] = dva_ref[...].astype(BF16)
        dh_ref[:, 768:1280] = (dqb_ref[...] * scale).astype(BF16)
        dh_ref[:, 1280:1408] = dkb_ref[...].astype(BF16)
        dh_ref[:, 1408:1536] = dvb_ref[...].astype(BF16)

    sd = jax.ShapeDtypeStruct
    return pl.pallas_call(
        body, grid=(t // tm,),
        in_specs=[_row_spec(tm, IN_COLS), _row_spec(tm, LANES), _row_spec(tm, LANES), _par_spec(LANES), _par_spec(LANES),
                  _row_spec(tm, Q_W), _row_spec(tm, KV_W), _row_spec(tm, KV_W),
                  _row_spec(tm, Q_W), _row_spec(tm, KV_W), _row_spec(tm, KV_W)],
        out_specs=[_row_spec(tm, IN_COLS), _par_spec(LANES), _par_spec(LANES)],
        out_shape=[sd((t, IN_COLS), BF16), sd((1, LANES), F32), sd((1, LANES), F32)],
        compiler_params=_cparams(("arbitrary",)), name="prep_bwd",
    )(h, cos_t, sin_t, qn, kn, dqa, dka, dva, dqb, dkb, dvb)


def _outnorm_fwd(oa, ob, ga, gb, tm=512):
    t = oa.shape[0]

    def body(oa_ref, ob_ref, ga_ref, gb_ref, y_ref):
        for o_ref, g_ref, start in ((oa_ref, ga_ref, 0), (ob_ref, gb_ref, Q_W)):
            x = o_ref[...]
            r = lax.rsqrt(jnp.mean(x * x, axis=-1, keepdims=True) + RMS_EPS)
            y_ref[:, start:start + Q_W] = (x * r * g_ref[...]).astype(BF16)

    return pl.pallas_call(
        body, grid=(t // tm,),
        in_specs=[_row_spec(tm, Q_W), _row_spec(tm, Q_W), _par_spec(Q_W), _par_spec(Q_W)],
        out_specs=_row_spec(tm, D_MODEL), out_shape=jax.ShapeDtypeStruct((t, D_MODEL), BF16),
        compiler_params=_cparams(("parallel",)), name="outnorm_fwd",
    )(oa, ob, ga, gb)


def _outnorm_bwd(dy, oa, ob, ga, gb, tm=512):
    t = oa.shape[0]

    def body(dy_ref, oa_ref, ob_ref, ga_ref, gb_ref, doa_ref, dob_ref, dga_ref, dgb_ref):
        @pl.when(pl.program_id(0) == 0)
        def _():
            dga_ref[...] = jnp.zeros_like(dga_ref)
            dgb_ref[...] = jnp.zeros_like(dgb_ref)

        for o_ref, g_ref, do_ref, dg_ref, start in ((oa_ref, ga_ref, doa_ref, dga_ref, 0),
                                                    (ob_ref, gb_ref, dob_ref, dgb_ref, Q_W)):
            x = o_ref[...]
            d = dy_ref[:, start:start + Q_W]
            r = lax.rsqrt(jnp.mean(x * x, axis=-1, keepdims=True) + RMS_EPS)
            xr = x * r
            dg_ref[...] += jnp.sum(d * xr, axis=0, keepdims=True)
            gy = d * g_ref[...]
            do_ref[...] = r * (gy - xr * jnp.mean(xr * gy, axis=-1, keepdims=True))

    sd = jax.ShapeDtypeStruct
    return pl.pallas_call(
        body, grid=(t // tm,),
        in_specs=[_row_spec(tm, D_MODEL), _row_spec(tm, Q_W), _row_spec(tm, Q_W), _par_spec(Q_W), _par_spec(Q_W)],
        out_specs=[_row_spec(tm, Q_W), _row_spec(tm, Q_W), _par_spec(Q_W), _par_spec(Q_W)],
        out_shape=[sd((t, Q_W), F32), sd((t, Q_W), F32), sd((1, Q_W), F32), sd((1, Q_W), F32)],
        compiler_params=_cparams(("arbitrary",)), name="outnorm_bwd",
    )(dy, oa, ob, ga, gb)


def _ln_bwd(d, z, g, tm=512):
    t = z.shape[0]

    def body(d_ref, z_ref, g_ref, dz_ref, dzb_ref, dg_ref, db_ref):
        @pl.when(pl.program_id(0) == 0)
        def _():
            dg_ref[...] = jnp.zeros_like(dg_ref)
            db_ref[...] = jnp.zeros_like(db_ref)

        zz = z_ref[...]
        dd = d_ref[...]
        mu = jnp.mean(zz, axis=-1, keepdims=True)
        zc = zz - mu
        r = lax.rsqrt(jnp.mean(zc * zc, axis=-1, keepdims=True) + LN_EPS)
        xh = zc * r
        dg_ref[...] += jnp.sum(dd * xh, axis=0, keepdims=True)
        db_ref[...] += jnp.sum(dd, axis=0, keepdims=True)
        dxh = dd * g_ref[...]
        dz = r * (dxh - jnp.mean(dxh, axis=-1, keepdims=True) - xh * jnp.mean(dxh * xh, axis=-1, keepdims=True))
        dz_ref[...] = dz
        dzb_ref[...] = dz.astype(BF16)

    sd = jax.ShapeDtypeStruct
    return pl.pallas_call(
        body, grid=(t // tm,),
        in_specs=[_row_spec(tm, D_MODEL), _row_spec(tm, D_MODEL), _par_spec(D_MODEL)],
        out_specs=[_row_spec(tm, D_MODEL), _row_spec(tm, D_MODEL), _par_spec(D_MODEL), _par_spec(D_MODEL)],
        out_shape=[sd((t, D_MODEL), F32), sd((t, D_MODEL), BF16), sd((1, D_MODEL), F32), sd((1, D_MODEL), F32)],
        compiler_params=_cparams(("arbitrary",)), name="ln_bwd",
    )(d, z, g)


def _loss_grad(y, tgt, tm=512):
    t = y.shape[0]
    nsteps = t // tm

    def body(y_ref, t_ref, dy_ref, loss_ref, acc):
        i = pl.program_id(0)

        @pl.when(i == 0)
        def _():
            acc[...] = jnp.zeros_like(acc)

        e = y_ref[...] - t_ref[...]
        dy_ref[...] = e * (1.0 / D_MODEL)
        acc[...] += jnp.sum(e * e, axis=0, keepdims=True)

        @pl.when(i == nsteps - 1)
        def _():
            tot = jnp.sum(acc[...], axis=-1, keepdims=True) * (0.5 / D_MODEL)
            loss_ref[...] = jnp.broadcast_to(tot, loss_ref.shape)

    sd = jax.ShapeDtypeStruct
    return pl.pallas_call(
        body, grid=(nsteps,),
        in_specs=[_row_spec(tm, D_MODEL), _row_spec(tm, D_MODEL)],
        out_specs=[_row_spec(tm, D_MODEL), _par_spec(LANES)],
        out_shape=[sd((t, D_MODEL), F32), sd((1, LANES), F32)],
        scratch_shapes=[pltpu.VMEM((1, D_MODEL), F32)],
        compiler_params=_cparams(("arbitrary",)), name="loss_grad",
    )(y, tgt)


_GELU_C = math.sqrt(2.0 / math.pi)
_GELU_K = 0.044715
HALO = 16


def _gelu_parts(x):
    x2 = x * x
    th = jnp.tanh(x * (_GELU_C + (_GELU_C * _GELU_K) * x2))
    a = 0.5 + 0.5 * th
    dact = a + (0.5 * x) * (1.0 - th * th) * (_GELU_C + (3.0 * _GELU_C * _GELU_K) * x2)
    return x * a, dact


def _halo_specs(tm, t, shift=0):
    last = t // HALO - 1
    cur = pl.BlockSpec((None, tm, FF_SH), lambda j, i: (j + shift, i, 0))
    prev = pl.BlockSpec((None, HALO, FF_SH), lambda j, i: (j + shift, jnp.maximum(i * (tm // HALO) - 1, 0), 0))
    nxt = pl.BlockSpec((None, HALO, FF_SH), lambda j, i: (j + shift, jnp.minimum((i + 1) * (tm // HALO), last), 0))
    return [prev, cur, nxt]


def _ffn_mid_fwd(gu, cw, tm=512):
    t = gu.shape[1]
    nsteps = t // tm

    def body(gp_ref, g_ref, gn_ref, u_ref, cw_ref, h_ref):
        i = pl.program_id(1)
        gg = g_ref[...].astype(F32)
        row = lax.broadcasted_iota(jnp.int32, gg.shape, 0)
        prev = jnp.where(i == 0, 0.0, gp_ref[...].astype(F32)[HALO - 1:HALO, :])
        nxt = jnp.where(i == nsteps - 1, 0.0, gn_ref[...].astype(F32)[0:1, :])
        g_m1 = jnp.where(row == 0, prev, pltpu.roll(gg, 1, 0))
        g_p1 = jnp.where(row == tm - 1, nxt, pltpu.roll(gg, tm - 1, 0))
        gc =cw_ref[3:4, :] + g_m1 * cw_ref[0:1, :] + gg * cw_ref[1:2, :] + g_p1 * cw_ref[2:3, :]
        act, _ = _gelu_parts(gc)
        h_ref[...] = (act * u_ref[...].astype(F32)).astype(BF16)

    return pl.pallas_call(
        body, grid=(N_SHARD, nsteps),
        in_specs=_halo_specs(tm, t) + [pl.BlockSpec((None, tm, FF_SH), lambda j, i: (j + N_SHARD, i, 0)),
                                       pl.BlockSpec((None, 8, FF_SH), lambda j, i: (j, 0, 0))],
        out_specs=pl.BlockSpec((None, tm, FF_SH), lambda j, i: (j, i, 0)),
        out_shape=jax.ShapeDtypeStruct((N_SHARD, t, FF_SH), BF16),
        compiler_params=_cparams(("parallel", "parallel")), name="ffn_mid_fwd",
    )(gu, gu, gu, gu, cw)


def _ffn_mid_bwd(gu, dh, cw, tm=1024):
    t = gu.shape[1]
    tm = min(tm, t)
    nsteps = t // tm
    te = tm + 2 * HALO

    def body(gp_ref, g_ref, gn_ref, up_ref, u_ref, un_ref, dp_ref, d_ref, dn_ref, cw_ref, dgu_ref, st_ref):
        i = pl.program_id(1)

        @pl.when(i == 0)
        def _():
            st_ref[...] = jnp.zeros_like(st_ref)

        def ext(p_ref, c_ref, n_ref):
            prev = jnp.where(i == 0, 0.0, p_ref[...].astype(F32))
            nxt = jnp.where(i == nsteps - 1, 0.0, n_ref[...].astype(F32))
            return jnp.concatenate([prev, c_ref[...].astype(F32), nxt], axis=0)

        eg = ext(gp_ref, g_ref, gn_ref)
        eu = ext(up_ref, u_ref, un_ref)
        ed = ext(dp_ref, d_ref, dn_ref)
        w0, w1, w2 = cw_ref[0:1, :], cw_ref[1:2, :], cw_ref[2:3, :]
        g_m1 = pltpu.roll(eg, 1, 0)
        g_p1 = pltpu.roll(eg, te - 1, 0)
        gc = cw_ref[3:4, :] + g_m1 * w0 + eg * w1 + g_p1 * w2
        act, dact = _gelu_parts(gc)
        dgc = ed * eu * dact
        dg = pltpu.roll(dgc, te - 1, 0) * w0 + dgc * w1 + pltpu.roll(dgc, 1, 0) * w2
        mid = slice(HALO, HALO + tm)
        dgu_ref[0] = dg[mid].astype(BF16)
        dgu_ref[1] = (ed * act)[mid].astype(BF16)
        sel = dgc[mid]
        parts = [jnp.sum(sel, axis=0, keepdims=True),
                 jnp.sum(sel * g_m1[mid], axis=0, keepdims=True),
                 jnp.sum(sel * eg[mid], axis=0, keepdims=True),
                 jnp.sum(sel * g_p1[mid], axis=0, keepdims=True)]
        r8 = lax.broadcasted_iota(jnp.int32, (8, FF_SH), 0)
        upd = jnp.zeros((8, FF_SH), F32)
        for k, p in enumerate(parts):
            upd = upd + jnp.where(r8 == k, p, 0.0)
        st_ref[...] += upd

    sd = jax.ShapeDtypeStruct
    return pl.pallas_call(
        body, grid=(N_SHARD, nsteps),
        in_specs=_halo_specs(tm, t) + _halo_specs(tm, t, N_SHARD) + _halo_specs(tm, t)
        + [pl.BlockSpec((None, 8, FF_SH), lambda j, i: (j, 0, 0))],
        out_specs=[pl.BlockSpec((2, None, tm, FF_SH), lambda j, i: (0, j, i, 0)),
                   pl.BlockSpec((None, 8, FF_SH), lambda j, i: (j, 0, 0))],
        out_shape=[sd((2, N_SHARD, t, FF_SH), BF16), sd((N_SHARD, 8, FF_SH), F32)],
        compiler_params=_cparams(("parallel", "arbitrary")), name="ffn_mid_bwd",
    )(gu, gu, gu, gu, gu, gu, dh, dh, dh, cw)


def _stack_heads(src_ref, dst_ref, tq):
    lo = lax.broadcasted_iota(jnp.int32, (tq, LANES), 1) < HEAD_DIM
    for i in range(4):
        blk = src_ref[:, LANES * i:LANES * (i + 1)].astype(dst_ref.dtype)
        zero = jnp.zeros_like(blk)
        dst_ref[tq * i:tq * (i + 1), :] = jnp.where(lo, blk, zero)
        dst_ref[tq * (4 + i):tq * (5 + i), :] = jnp.where(lo, zero, blk)


def _gattn_fwd(q, k, v, gather=(), tq=128, tk=2048):
    t = q.shape[0]
    tk = min(tk, t)
    nq, nk, r = t // tq, t // tk, 8 * tq
    ng = len(gather)

    def body(*refs):
        q_ref, k_ref, v_ref = refs[:3]
        o_ref, lse_ref = refs[3 + ng:5 + ng]
        qst, m_s, l_s, acct = refs[5 + 2 * ng:9 + 2 * ng]
        if ng:
            ex = _gather_exchange(refs[3:3 + ng], refs[5 + ng:5 + 2 * ng], *refs[9 + 2 * ng:])
            pl.when(pl.program_id(0) == 0)(ex.start)
        lo_rows = lax.broadcasted_iota(jnp.int32, (LANES, tq), 0) < HEAD_DIM
        for i in range(4):
            bt = q_ref[:, LANES * i:LANES * (i + 1)].astype(F32).T
            qst[:, tq * i:tq * (i + 1)] = jnp.where(lo_rows, bt, 0.0).astype(BF16)
            qst[:, tq * (4 + i):tq * (5 + i)] = jnp.where(lo_rows, 0.0, bt).astype(BF16)
        m_s[...] = jnp.full_like(m_s, NEG)

        def max_step(j, carry):
            off = pl.multiple_of(j * tk, tk)
            st = _dot(k_ref[pl.ds(off, tk), :], qst[...], _NN)
            m_s[...] = jnp.maximum(m_s[...], jnp.max(st.reshape(tk // 8, 8, r), axis=0))
            return carry

        lax.fori_loop(0, nk, max_step, 0)
        m_row = jnp.max(m_s[...], axis=0, keepdims=True)
        l_s[...] = jnp.zeros_like(l_s)
        acct[...] = jnp.zeros_like(acct)

        def sum_step(j, carry):
            off = pl.multiple_of(j * tk, tk)
            st = _dot(k_ref[pl.ds(off, tk), :], qst[...], _NN)
            pt = jnp.exp(st - m_row)
            l_s[...] += jnp.sum(pt.reshape(tk // 8, 8, r), axis=0)
            acct[...] += _dot(v_ref[j], pt, _NN)
            return carry

        lax.fori_loop(0, nk, sum_step, 0)
        l_row = jnp.sum(l_s[...], axis=0, keepdims=True)
        ot = acct[...] / l_row
        for i in range(4):
            pair_t = jnp.where(lo_rows, ot[:, tq * i:tq * (i + 1)], ot[:, tq * (4 + i):tq * (5 + i)])
            o_ref[:, LANES * i:LANES * (i + 1)] = pair_t.T
        lse_ref[...] = m_row + jnp.log(l_row)
        if ng:
            pl.when(pl.program_id(0) == nq - 1)(ex.wait)

    sd = jax.ShapeDtypeStruct
    vt3 = v.reshape(nk, tk, KV_W).transpose(0, 2, 1)
    return pl.pallas_call(
        body, grid=(nq,),
        in_specs=[_row_spec(tq, Q_W), _par_spec(KV_W, t), pl.BlockSpec((nk, KV_W, tk), lambda i: (0, 0, 0))]
        + [_ANY] * ng,
        out_specs=[_row_spec(tq, Q_W), pl.BlockSpec((None, 1, r), lambda i: (i, 0, 0))] + [_ANY] * ng,
        out_shape=[sd((t, Q_W), F32), sd((nq, 1, r), F32)] + _gathered_shapes(gather),
        scratch_shapes=[pltpu.VMEM((LANES, r), BF16), pltpu.VMEM((8, r), F32), pltpu.VMEM((8, r), F32),
                        pltpu.VMEM((LANES, r), F32)] + (_exchange_sems(ng) if ng else []),
        compiler_params=_cparams(("arbitrary",) if ng else ("parallel",)),
        name="gattn_fwd_gather" if ng else "gattn_fwd",
    )(q, k, vt3, *gather)


def _gattn_bwd(q, k, v, o, do, lse, scatter=None, tq=128, tk=1024):
    t = q.shape[0]
    tk = min(tk, t)
    nq, nk, r = t // tq, t // tk, 8 * tq
    items, sgrads = scatter if scatter else ((), ())
    ns = len(sgrads)
    slot_shapes = []
    for j, (o_idx, _) in enumerate(items):
        if o_idx == len(slot_shapes):
            slot_shapes += _slot_shapes([sgrads[j]])
    nslots = len(slot_shapes)

    n_in, n_scr = 7, 6
    kt3 = k.reshape(nk, tk, KV_W).transpose(0, 2, 1)

    def body(*refs):
        q_ref, k_ref, v_ref, kt_ref, o_ref, do_ref, lse_ref = refs[:n_in]
        dq_ref, dk_ref, dv_ref = refs[n_in + ns:n_in + 3 + ns]
        scr = n_in + 3 + ns + nslots
        qs, dos, qst, dost, dlt_row, dqt = refs[scr:scr + n_scr]
        if ns:
            ex = _scatter_exchange(items, refs[n_in:n_in + ns], refs[n_in + 3 + ns:scr], *refs[scr + n_scr:])
            pl.when(pl.program_id(0) == 0)(ex.start)

        @pl.when(pl.program_id(0) == 0)
        def _():
            dk_ref[...] = jnp.zeros_like(dk_ref)
            dv_ref[...] = jnp.zeros_like(dv_ref)

        _stack_heads(q_ref, qs, tq)
        _stack_heads(do_ref, dos, tq)
        lo_rows = lax.broadcasted_iota(jnp.int32, (LANES, tq), 0) < HEAD_DIM
        for i in range(4):
            lo, hi = slice(tq * i, tq * (i + 1)), slice(tq * (4 + i), tq * (5 + i))
            cols = slice(LANES * i, LANES * (i + 1))
            for src, dst in ((q_ref, qst), (do_ref, dost)):
                bt = src[:, cols].astype(F32).T
                dst[:, lo] = jnp.where(lo_rows, bt, 0.0).astype(BF16)
                dst[:, hi] = jnp.where(lo_rows, 0.0, bt).astype(BF16)
            prod_t = (do_ref[:, cols] * o_ref[:, cols]).T
            dlt_row[:, lo] = jnp.sum(prod_t[:HEAD_DIM], axis=0, keepdims=True)
            dlt_row[:, hi] = jnp.sum(prod_t[HEAD_DIM:], axis=0, keepdims=True)
        lse_row = lse_ref[...]
        dqt[...] = jnp.zeros_like(dqt)

        def step(j, carry):
            off = pl.multiple_of(j * tk, tk)
            kc = k_ref[pl.ds(off, tk), :]
            vc = v_ref[pl.ds(off, tk), :]
            p = jnp.exp(_dot(kc, qst[...], _NN) - lse_row)
            dp = _dot(vc, dost[...], _NN)
            ds = (p * (dp - dlt_row[...])).astype(BF16)
            dk_ref[pl.ds(off, tk), :] += _dot(ds, qs[...], _NN)
            dv_ref[pl.ds(off, tk), :] += _dot(p, dos[...], _NN)
            dqt[...] += _dot(kt_ref[j], ds, _NN)
            return carry

        lax.fori_loop(0, nk, step, 0)
        for i in range(4):
            pair_t = jnp.where(lo_rows, dqt[:, tq * i:tq * (i + 1)], dqt[:, tq * (4 + i):tq * (5 + i)])
            dq_ref[:, LANES * i:LANES * (i + 1)] = pair_t.T
        if ns:
            pl.when(pl.program_id(0) == nq - 1)(ex.wait)

    sd = jax.ShapeDtypeStruct
    return pl.pallas_call(
        body, grid=(nq,),
        in_specs=[_row_spec(tq, Q_W), _par_spec(KV_W, t), _par_spec(KV_W, t),
                  pl.BlockSpec((nk, KV_W, tk), lambda i: (0, 0, 0)), _row_spec(tq, Q_W), _row_spec(tq, Q_W),
                  pl.BlockSpec((None, 1, r), lambda i: (i, 0, 0))] + [_ANY] * ns,
        out_specs=[_row_spec(tq, Q_W), _par_spec(KV_W, t), _par_spec(KV_W, t)] + [_ANY] * nslots,
        out_shape=[sd((t, Q_W), F32), sd((t, KV_W), F32), sd((t, KV_W), F32)] + slot_shapes,
        scratch_shapes=[pltpu.VMEM((r, LANES), BF16), pltpu.VMEM((r, LANES), BF16), pltpu.VMEM((LANES, r), BF16),
                        pltpu.VMEM((LANES, r), BF16), pltpu.VMEM((1, r), F32),
                        pltpu.VMEM((LANES, r), F32)] + (_exchange_sems(ns) if ns else []),
        compiler_params=_cparams(("arbitrary",)), name="gattn_bwd_scatter" if ns else "gattn_bwd",
    )(q, k, v, kt3, o, do, lse, *sgrads)


_WQ = Q_BLOCK
_WK = 3 * Q_BLOCK
_WR = 8 * _WQ


def _pairs_transposed(src_ref, dst, tq):
    lo_rows = lax.broadcasted_iota(jnp.int32, (LANES, tq), 0) < HEAD_DIM
    for i in range(4):
        bt = src_ref[:, LANES * i:LANES * (i + 1)].astype(F32).T
        dst[:, tq * i:tq * (i + 1)] = jnp.where(lo_rows, bt, 0.0).astype(BF16)
        dst[:, tq * (4 + i):tq * (5 + i)] = jnp.where(lo_rows, 0.0, bt).astype(BF16)


def _pairs_from_transposed(halves, dst_ref, tq):
    for i in range(4):
        pair_t = jnp.concatenate([h[:, tq * i:tq * (i + 1)] for h in halves], axis=0)
        dst_ref[:, LANES * i:LANES * (i + 1)] = pair_t.T.astype(dst_ref.dtype)


def _kv_quadrants(tq):
    return [(slice(HEAD_DIM * kv, HEAD_DIM * (kv + 1)), slice(4 * tq * kv, 4 * tq * (kv + 1))) for kv in range(2)]


def _wattn_scores_t(kw, qst, bias_ref, n, t):
    kabs = (n - 1) * _WQ + lax.broadcasted_iota(jnp.int32, (_WK, 1), 0)
    st = _dot(kw, qst[...], _NN) + bias_ref[...]
    return jnp.where((kabs >= 0) & (kabs < t), st, NEG)


def _window_t(ref3, n):
    return jnp.concatenate([ref3[n], ref3[n + 1], ref3[n + 2]], axis=1)


def _blocks_transposed(ap):
    return ap.reshape(ap.shape[0] // _WQ, _WQ, KV_W).transpose(0, 2, 1)


def _wattn_fwd(q, kp, vp, bias_t, sink):
    t = q.shape[0]
    nq = t // _WQ
    tp = t + 2 * _WQ
    vpt = _blocks_transposed(vp)

    def body(q_ref, k_ref, vt_ref, b_ref, sk_ref, o_ref, lse_ref, qst):
        n = pl.program_id(0)
        _pairs_transposed(q_ref, qst, _WQ)
        kw = k_ref[pl.ds(pl.multiple_of(n * _WQ, _WQ), _WK), :]
        st = _wattn_scores_t(kw, qst, b_ref, n, t)
        sk = sk_ref[...]
        m = jnp.maximum(jnp.max(st, axis=0, keepdims=True), sk)
        pt = jnp.exp(st - m)
        l = jnp.sum(pt, axis=0, keepdims=True) + jnp.exp(sk - m)
        vwt = _window_t(vt_ref, n)
        halves = [_dot(vwt[rows, :], pt[:, cols], _NN) / l[:, cols] for rows, cols in _kv_quadrants(_WQ)]
        _pairs_from_transposed(halves, o_ref, _WQ)
        lse_ref[...] = m + jnp.log(l)

    sd = jax.ShapeDtypeStruct
    return pl.pallas_call(
        body, grid=(nq,),
        in_specs=[_row_spec(_WQ, Q_W), _par_spec(KV_W, tp), pl.BlockSpec(vpt.shape, lambda i: (0, 0, 0)),
                  _par_spec(_WR, _WK), _par_spec(_WR)],
        out_specs=[_row_spec(_WQ, Q_W), pl.BlockSpec((None, 1, _WR), lambda i: (i, 0, 0))],
        out_shape=[sd((t, Q_W), F32), sd((nq, 1, _WR), F32)],
        scratch_shapes=[pltpu.VMEM((LANES, _WR), BF16)],
        compiler_params=_cparams(("parallel",)), name="wattn_fwd",
    )(q, kp, vpt, bias_t, sink)


def _wattn_bwd(q, kp, vp, bias_t, sink, o, do, lse):
    t = q.shape[0]
    nq = t // _WQ
    tp = t + 2 * _WQ
    kpt = _blocks_transposed(kp)

    def body(q_ref, k_ref, v_ref, kt_ref, b_ref, sk_ref, o_ref, do_ref, lse_ref, dq_ref, dk_ref, dv_ref, db_ref,
             dsk_ref, qs, dos, qst, dost):
        n = pl.program_id(0)

        @pl.when(n == 0)
        def _():
            dk_ref[...] = jnp.zeros_like(dk_ref)
            dv_ref[...] = jnp.zeros_like(dv_ref)
            db_ref[...] = jnp.zeros_like(db_ref)
            dsk_ref[...] = jnp.zeros_like(dsk_ref)

        _stack_heads(q_ref, qs, _WQ)
        _stack_heads(do_ref, dos, _WQ)
        _pairs_transposed(q_ref, qst, _WQ)
        _pairs_transposed(do_ref, dost, _WQ)
        delta = []
        for i in range(4):
            cols = slice(LANES * i, LANES * (i + 1))
            prod_t = (do_ref[:, cols] * o_ref[:, cols]).T
            delta.append((jnp.sum(prod_t[:HEAD_DIM], axis=0, keepdims=True),
                          jnp.sum(prod_t[HEAD_DIM:], axis=0, keepdims=True)))
        dlt = jnp.concatenate([d[0] for d in delta] + [d[1] for d in delta], axis=1)
        off = pl.multiple_of(n * _WQ, _WQ)
        kw = k_ref[pl.ds(off, _WK), :]
        vw = v_ref[pl.ds(off, _WK), :]
        lse_v = lse_ref[...]
        pt = jnp.exp(_wattn_scores_t(kw, qst, b_ref, n, t) - lse_v)
        dpt = _dot(vw, dost[...], _NN)
        ds = pt * (dpt - dlt)
        db_ref[...] += ds
        dsk_ref[...] -= jnp.exp(sk_ref[...] - lse_v) * dlt
        dsb = ds.astype(BF16)
        dk_ref[pl.ds(off, _WK), :] += _dot(dsb, qs[...], _NN)
        dv_ref[pl.ds(off, _WK), :] += _dot(pt, dos[...], _NN)
        kwt = _window_t(kt_ref, n)
        halves = [_dot(kwt[rows, :], dsb[:, cols], _NN) for rows, cols in _kv_quadrants(_WQ)]
        _pairs_from_transposed(halves, dq_ref, _WQ)

    sd = jax.ShapeDtypeStruct
    qb = _row_spec(_WQ, Q_W)
    return pl.pallas_call(
        body, grid=(nq,),
        in_specs=[qb, _par_spec(KV_W, tp), _par_spec(KV_W, tp), pl.BlockSpec(kpt.shape, lambda i: (0, 0, 0)),
                  _par_spec(_WR, _WK), _par_spec(_WR), qb, qb, pl.BlockSpec((None, 1, _WR), lambda i: (i, 0, 0))],
        out_specs=[qb, _par_spec(KV_W, tp), _par_spec(KV_W, tp), _par_spec(_WR, _WK), _par_spec(_WR)],
        out_shape=[sd((t, Q_W), F32), sd((tp, KV_W), F32), sd((tp, KV_W), F32), sd((_WK, _WR), F32), sd((1, _WR), F32)],
        scratch_shapes=[pltpu.VMEM((_WR, LANES), BF16), pltpu.VMEM((_WR, LANES), BF16), pltpu.VMEM((LANES, _WR), BF16),
                        pltpu.VMEM((LANES, _WR), BF16)],
        compiler_params=_cparams(("arbitrary",)), name="wattn_bwd",
    )(q, kp, vp, kpt, bias_t, sink, o, do, lse)


def _bias_bucket_reduce(db0, db1, bucket):
    def body(a_ref, b_ref, bk_ref, o_ref):
        d = a_ref[...] + b_ref[...]
        bk = bk_ref[...]
        lane = lax.broadcasted_iota(jnp.int32, (1, LANES), 1)
        out = jnp.zeros((1, LANES), F32)
        for b in range(N_BUCKETS):
            tot = jnp.sum(jnp.sum(jnp.where(bk == b, d, 0.0), axis=-1, keepdims=True), axis=0, keepdims=True)
            out = out + jnp.where(lane == b, tot, 0.0)
        o_ref[...] = out

    hb = pl.BlockSpec((None, _WQ, _WK), lambda h: (h, 0, 0))
    return pl.pallas_call(
        body, grid=(8,), in_specs=[hb, hb, pl.BlockSpec((_WQ, _WK), lambda h: (0, 0))],
        out_specs=pl.BlockSpec((None, 1, LANES), lambda h: (h, 0, 0)),
        out_shape=jax.ShapeDtypeStruct((8, 1, LANES), F32),
        compiler_params=_cparams(("parallel",)), name="bias_bucket_reduce",
    )(db0.reshape(8, _WQ, _WK), db1.reshape(8, _WQ, _WK), bucket)


def _rope_tables(t):
    rows_n = t // GRID_W
    row = jnp.repeat(jnp.arange(rows_n, dtype=F32), GRID_W)
    col = jnp.tile(jnp.arange(GRID_W, dtype=F32), rows_n)
    half = HEAD_DIM // 2
    inv_freq = ROPE_THETA ** (-jnp.arange(0, half, 2, dtype=F32) / half)
    ang = jnp.concatenate([row[:, None] * inv_freq, col[:, None] * inv_freq], axis=-1)
    cos, sin = jnp.cos(ang), jnp.sin(ang)
    c64 = jnp.repeat(cos, 2, axis=-1)
    s64 = jnp.stack([-sin, sin], axis=-1).reshape(t, HEAD_DIM)
    return jnp.tile(c64, (1, 2)), jnp.tile(s64, (1, 2))


def _t5_bucket(rel):
    half = N_BUCKETS // 2
    max_exact = half // 2
    bucket = jnp.where(rel > 0, half, 0)
    rp = jnp.abs(rel)
    rpf = jnp.maximum(rp, 1).astype(jnp.float32)
    large = max_exact + (jnp.log(rpf / max_exact) / math.log(MAX_DISTANCE / max_exact)
                         * (half - max_exact)).astype(jnp.int32)
    large = jnp.minimum(large, half - 1)
    return bucket + jnp.where(rp < max_exact, rp, large)


def _window_tables(rel_bias):
    qpos = jnp.arange(_WQ, dtype=jnp.int32)
    kpos = jnp.arange(_WK, dtype=jnp.int32) - _WQ
    rel = kpos[None, :] - qpos[:, None]
    bucket = _t5_bucket(rel)
    bias = jnp.zeros((8, _WQ, _WK), F32)
    for b in range(N_BUCKETS):
        bias = jnp.where((bucket == b)[None], rel_bias[b][:, None, None], bias)
    bias = jnp.where((jnp.abs(rel) <= WINDOW)[None], bias, NEG)
    return bias.reshape(_WR, _WK).T, bucket


def _pad_rows(a):
    return jnp.pad(a, ((_WQ, _WQ), (0, 0)))


def _layer_fwd(x, p, tabs, gather=None):
    cos_t, sin_t, bias = tabs
    h = _mm_nn(x, p["win"], F32, "in_proj")
    qa, ka, va, qb, kb, vb = _prep_fwd(h, cos_t, sin_t, p["qn"], p["kn"])
    if gather is None:
        oa, lse_a = _gattn_fwd(qa, ka, va)
    else:
        oa, lse_a, *gathered = _gattn_fwd(qa, ka, va, gather=gather[0])
        p = gather[1](gathered)
    kbp, vbp = _pad_rows(kb), _pad_rows(vb)
    ob, lse_b = _wattn_fwd(qb, kbp, vbp, bias, p["sink"])
    y = _outnorm_fwd(oa, ob, p["ga"], p["gb"])
    z1, x1, x1b = _mm_reduce(y[None], p["wout"][None], _NN, F32, "out_proj", res=x, res_scale=ALPHA,
                             ln=(p["ln1g"], p["ln1b"]))
    gu = _mm_expand(x1b, p["wgu"], _NN, BF16, "gate_up_proj")
    hdn = _ffn_mid_fwd(gu, p["cw"])
    z2, x2, _ = _mm_reduce(hdn, p["wd"], _NN, F32, "down_proj", res=x1, res_scale=ALPHA, ln=(p["ln2g"], p["ln2b"]))
    saved = dict(x=x, h=h, qa=qa, ka=ka, va=va, qb=qb, kbp=kbp, vbp=vbp, oa=oa, ob=ob, lse_a=lse_a, lse_b=lse_b,
                 y=y, z1=z1, x1b=x1b, gu=gu, hdn=hdn, z2=z2)
    return x2, saved


def _block_grads(g, names=("w_in", "w_out", "w_gate", "w_up", "w_down")):
    make = dict(
        w_in=lambda: _col_blocks(_in_cols_to_pairs(g["win"], _from_pairs), IN_SH),
        w_out=lambda: _mix_rows_to_pairs(g["wout"], _from_pairs).reshape(N_SHARD, OUT_SH, D_MODEL),
        w_gate=lambda: g["wg"], w_up=lambda: g["wu"], w_down=lambda: g["wd"])
    return [make[n]() for n in names]


def _layer_bwd(dx2, p, s, tabs, layer=0, pending=None):
    cos_t, sin_t, bias = tabs
    t = dx2.shape[0]
    dz2, dz2b, dln2g, dln2b = _ln_bwd(dx2, s["z2"], p["ln2g"])
    dhdn = _mm_expand(dz2b, p["wd"], _NT, BF16, "down_dx")
    dwd = _mm_tn_blocks(s["hdn"], dz2b, "down_dw")
    dgu, stats = _ffn_mid_bwd(s["gu"], dhdn, p["cw"])
    dgu = dgu.reshape(2 * N_SHARD, t, FF_SH)
    dx1 = _mm_reduce(dgu, p["wgu"], _NT, F32, "gate_up_dx", res=dz2, res_scale=ALPHA)
    dwg = _mm_tn_blocks(s["x1b"], dgu, "gate_dw", blk=0)
    dwu = _mm_tn_blocks(s["x1b"], dgu, "up_dw", blk=1)
    dz1, dz1b, dln1g, dln1b = _ln_bwd(dx1, s["z1"], p["ln1g"])
    dy = _mm_nt(dz1b, p["wout"], F32, "out_dx")
    dwout = _mm_tn(s["y"], dz1b, "out_dw")
    doa, dob, dga, dgb = _outnorm_bwd(dy, s["oa"], s["ob"], p["ga"], p["gb"])
    slots = None
    if pending is None:
        dqa, dka, dva = _gattn_bwd(s["qa"], s["ka"], s["va"], s["oa"], doa, s["lse_a"])
    else:
        mine = _block_grads(dict(wout=dwout, wg=dwg, wu=dwu, wd=dwd), ("w_out", "w_gate", "w_up", "w_down"))
        todo = list(pending) + [(o + 1, layer, g) for o, g in enumerate(mine)]
        dqa, dka, dva, *slots = _gattn_bwd(s["qa"], s["ka"], s["va"], s["oa"], doa, s["lse_a"],
                                           scatter=([(o, l) for o, l, _ in todo], [g for _, _, g in todo]))
    dqb, dkbp, dvbp, dbias, dsink = _wattn_bwd(s["qb"], s["kbp"], s["vbp"], bias, p["sink"], s["ob"], dob, s["lse_b"])
    dkb = lax.slice_in_dim(dkbp, _WQ, _WQ + t, axis=0)
    dvb = lax.slice_in_dim(dvbp, _WQ, _WQ + t, axis=0)
    dh, dqn, dkn = _prep_bwd(s["h"], cos_t, sin_t, p["qn"], p["kn"], dqa, dka, dva, dqb, dkb, dvb)
    dx = _mm_nt(dh, p["win"], F32, "in_dx", res=dz1, res_scale=ALPHA)
    dwin = _mm_tn(s["x"], dh, "in_dw")
    grads = dict(win=dwin, wout=dwout, wg=dwg, wu=dwu, wd=dwd, stats=stats, qn=dqn, kn=dkn, ga=dga, gb=dgb,
                 ln1g=dln1g, ln1b=dln1b, ln2g=dln2g, ln2b=dln2b, bias=dbias, sink=dsink, slots=slots)
    return dx, grads


def _prep_layer_params(l, win, wout, wg, wu, wd, cw, q_norm, k_norm, sink, out_norm_a, out_norm_b, conv_b,
                       ln1_g, ln1_b, ln2_g, ln2_b):
    win_full = win.transpose(1, 0, 2).reshape(D_MODEL, IN_COLS)
    row = lambda v: v.reshape(1, -1)
    late = {}
    if wout is not None:
        late = dict(
            wout=_mix_rows_to_pairs(wout.reshape(D_MODEL, D_MODEL)), wgu=jnp.concatenate([wg, wu], axis=0), wd=wd,
            cw=jnp.pad(cw, ((0, 0), (0, 5), (0, 0)))
            + jnp.pad(conv_b[l].reshape(N_SHARD, 1, FF_SH), ((0, 0), (3, 4), (0, 0))))
    return dict(
        late, win=_in_cols_to_pairs(win_full),
        qn=row(jnp.tile(q_norm[l], 2)), kn=row(jnp.tile(k_norm[l], 2)),
        ga=row(_to_pairs(out_norm_a[l], 0)), gb=row(_to_pairs(out_norm_b[l], 0)),
        ln1g=row(ln1_g[l]), ln1b=row(ln1_b[l]), ln2g=row(ln2_g[l]), ln2b=row(ln2_b[l]),
        sink=jnp.repeat(sink[l], _WQ).reshape(1, _WR))


def _local_step(x, tgt, params, rel_bias, gather=None, scatter=False):
    t = x.shape[0]
    cos_t, sin_t = _rope_tables(t)
    bias, bucket = _window_tables(rel_bias)
    tabs = (cos_t, sin_t, bias)
    saved = []
    for l in range(DEPTH):
        x, s = _layer_fwd(x, params[l], tabs, gather if l == 0 else None)
        saved.append(s)
    dx, loss = _loss_grad(x, tgt)
    grads = [None] * DEPTH
    for l in reversed(range(DEPTH)):
        pending = None
        if scatter and l == 0:
            pending = [(o, 1, g) for o, g in enumerate(_block_grads(grads[1]))]
        dx, grads[l] = _layer_bwd(dx, params[l], saved[l], tabs, l, pending)
    dbucket = _bias_bucket_reduce(grads[0]["bias"].T, grads[1]["bias"].T, bucket)
    return loss, dx, grads, dbucket


_ANY = pl.BlockSpec(memory_space=pl.ANY)
_MESH = pl.DeviceIdType.MESH


def _mesh_pos():
    return lax.axis_index("x"), lax.axis_index("y"), lax.axis_index("c")


def _other_chips(x, y):
    return [(1 - x, y), (x, 1 - y), (1 - x, 1 - y)]


class _Exchange:
    def __init__(self, local, sends, recvs):
        self.local, self.sends, self.recvs = local, sends, recvs

    def start(self):
        for cp in self.local + self.sends:
            cp.start()

    def wait(self):
        for cp in self.recvs:
            cp.wait_recv()
        for cp in self.sends:
            cp.wait_send()
        for cp in self.local:
            cp.wait()


def _exchange_sems(n):
    return [pltpu.SemaphoreType.DMA((n, 3)), pltpu.SemaphoreType.DMA((n, 3)), pltpu.SemaphoreType.DMA((n,))]


def _gather_exchange(ins, outs, send, recv, loc):
    x, y, c = _mesh_pos()
    me = 2 * x + y
    chips = _other_chips(x, y)

    def remote(i, k, block):
        px, py = chips[k]
        return pltpu.make_async_remote_copy(ins[i], outs[i].at[block], send.at[i, k], recv.at[i, k],
                                            device_id=(px, py, c), device_id_type=_MESH)

    n = len(ins)
    local = [pltpu.make_async_copy(ins[i], outs[i].at[me], loc.at[i]) for i in range(n)]
    sends = [remote(i, k, me) for i in range(n) for k in range(3)]
    recvs = [remote(i, k, 2 * chips[k][0] + chips[k][1]) for i in range(n) for k in range(3)]
    return _Exchange(local, sends, recvs)


def _scatter_exchange(items, ins, outs, send, recv, loc):
    x, y, c = _mesh_pos()
    me = 2 * x + y
    chips = _other_chips(x, y)

    def remote(j, k):
        o, l = items[j]
        px, py = chips[k]
        return pltpu.make_async_remote_copy(ins[j].at[2 * px + py], outs[o].at[k, l], send.at[j, k], recv.at[j, k],
                                            device_id=(px, py, c), device_id_type=_MESH)

    local = [pltpu.make_async_copy(ins[j].at[me], outs[o].at[3, l], loc.at[j]) for j, (o, l) in enumerate(items)]
    sends = [remote(j, k) for j in range(len(items)) for k in range(3)]
    return _Exchange(local, sends, sends)


def _gathered_shapes(shards):
    return [jax.ShapeDtypeStruct((N_SHARD,) + s.shape, s.dtype) for s in shards]


def _slot_shapes(blocks):
    return [jax.ShapeDtypeStruct((N_SHARD, DEPTH) + g.shape[1:], g.dtype) for g in blocks]


def _gather_shards(shards):
    n = len(shards)

    def body(*refs):
        ex = _gather_exchange(refs[:n], refs[n:2 * n], *refs[2 * n:])
        ex.start()
        ex.wait()

    return pl.pallas_call(
        body, in_specs=[_ANY] * n, out_specs=[_ANY] * n, out_shape=_gathered_shapes(shards),
        scratch_shapes=_exchange_sems(n), name="gather_weights",
    )(*shards)


def _scatter_into(items, grads, slots):
    n, ns = len(grads), len(slots)

    def body(*refs):
        ex = _scatter_exchange(items, refs[:n], refs[n + ns:n + 2 * ns], *refs[n + 2 * ns:])
        ex.start()
        ex.wait()

    return pl.pallas_call(
        body, in_specs=[_ANY] * (n + ns), out_specs=[_ANY] * ns,
        out_shape=[jax.ShapeDtypeStruct(s.shape, s.dtype) for s in slots],
        input_output_aliases={n + i: i for i in range(ns)},
        scratch_shapes=_exchange_sems(n), name="scatter_grads",
    )(*grads, *slots)


def _swap_with_sibling(parts):
    n = len(parts)

    def body(*refs):
        ins, outs = refs[:n], refs[n:2 * n]
        send, recv = refs[2 * n:]
        x, y, c = _mesh_pos()
        copies = [pltpu.make_async_remote_copy(ins[i], outs[i], send.at[i], recv.at[i], device_id=(x, y, 1 - c),
                                               device_id_type=_MESH) for i in range(n)]
        for cp in copies:
            cp.start()
        for cp in copies:
            cp.wait_recv()
        for cp in copies:
            cp.wait_send()

    return pl.pallas_call(
        body, in_specs=[_ANY] * n, out_specs=[_ANY] * n,
        out_shape=[jax.ShapeDtypeStruct(p.shape, p.dtype) for p in parts],
        scratch_shapes=[pltpu.SemaphoreType.DMA((n,)), pltpu.SemaphoreType.DMA((n,))],
        name="swap_sibling",
    )(*parts)


N_DEV = 8


def _allreduce_small(packed):
    rows = packed.shape[0]

    def body(in_ref, out_ref, buf, send, recv, loc):
        x, y, c = _mesh_pos()
        me = 4 * x + 2 * y + c
        own = pltpu.make_async_copy(in_ref, buf.at[me], loc)
        own.start()

        def remote(m, block):
            peer = (x ^ (m >> 2), y ^ ((m >> 1) & 1), c ^ (m & 1))
            return pltpu.make_async_remote_copy(in_ref, buf.at[block], send.at[m - 1], recv.at[m - 1],
                                                device_id=peer, device_id_type=_MESH)

        sends = [remote(m, me) for m in range(1, N_DEV)]
        for cp in sends:
            cp.start()
        for m in range(1, N_DEV):
            remote(m, me ^ m).wait_recv()
        for cp in sends:
            cp.wait_send()
        own.wait()
        tot = buf[0]
        for d in range(1, N_DEV):
            tot = tot + buf[d]
        out_ref[...] = tot

    vm = pl.BlockSpec(memory_space=pltpu.VMEM)
    return pl.pallas_call(
        body, in_specs=[vm], out_specs=vm, out_shape=jax.ShapeDtypeStruct((rows, LANES), F32),
        scratch_shapes=[pltpu.VMEM((N_DEV, rows, LANES), F32), pltpu.SemaphoreType.DMA((N_DEV - 1,)),
                        pltpu.SemaphoreType.DMA((N_DEV - 1,)), pltpu.SemaphoreType.DMA(())],
        name="allreduce_small",
    )(packed)


def _shard_rows(r):
    return r // 2 if r % 32 == 0 else r


def _sum_slots(slots):
    _, _, r, cdim = slots.shape
    tr = _shard_rows(r)

    def body(a_ref, b_ref, c_ref, d_ref, o_ref):
        up = lambda ref: ref[...].astype(F32)
        o_ref[...] = ((up(d_ref) + up(a_ref)) + up(b_ref)) + up(c_ref)

    def spec(k):
        return pl.BlockSpec((None, None, tr, cdim), lambda l, i: (k, l, i, 0))

    return pl.pallas_call(
        body, grid=(DEPTH, r // tr), in_specs=[spec(0), spec(1), spec(2), spec(3)],
        out_specs=pl.BlockSpec((None, tr, cdim), lambda l, i: (l, i, 0)),
        out_shape=jax.ShapeDtypeStruct((DEPTH, r, cdim), F32),
        compiler_params=_cparams(("parallel", "parallel")), name="sum_slots",
    )(slots, slots, slots, slots)


def _adamw_math(w, g, m, v):
    m = ADAM_B1 * m + (1.0 - ADAM_B1) * g
    v = ADAM_B2 * v + (1.0 - ADAM_B2) * (g * g)
    m_hat = m / (1.0 - ADAM_B1 ** ADAM_STEP)
    v_hat = v / (1.0 - ADAM_B2 ** ADAM_STEP)
    delta = -ADAM_LR * (m_hat / (jnp.sqrt(v_hat) + ADAM_EPS) + ADAM_WD * w)
    return delta, m, v


def _adamw_big(ga, gb, w, m, v):
    _, r, cdim = w.shape
    tr = _shard_rows(r)

    def body(ga_ref, gb_ref, w_ref, m_ref, v_ref, g_out, d_out, m_out, v_out):
        g = ga_ref[...] + gb_ref[...]
        d, mn, vn = _adamw_math(w_ref[...], g, m_ref[...], v_ref[...])
        g_out[...] = g
        d_out[...] = d
        m_out[...] = mn
        v_out[...] = vn

    spec = pl.BlockSpec((None, tr, cdim), lambda l, i: (l, i, 0))
    shp = jax.ShapeDtypeStruct(w.shape, F32)
    return pl.pallas_call(
        body, grid=(DEPTH, r // tr), in_specs=[spec] * 5, out_specs=[spec] * 4, out_shape=[shp] * 4,
        compiler_params=_cparams(("parallel", "parallel")), name="adamw_big",
    )(ga, gb, w, m, v)


def _adamw_small(ws, gs, ms, vs):
    n = len(ws)

    def body(*refs):
        w_r, g_r, m_r, v_r = (refs[k * n:(k + 1) * n] for k in range(4))
        d_o, m_o, v_o = (refs[(4 + k) * n:(5 + k) * n] for k in range(3))
        for i in range(n):
            d, mn, vn = _adamw_math(w_r[i][...], g_r[i][...], m_r[i][...], v_r[i][...])
            d_o[i][...] = d
            m_o[i][...] = mn
            v_o[i][...] = vn

    vm = pl.BlockSpec(memory_space=pltpu.VMEM)
    shp = [jax.ShapeDtypeStruct(w.shape, F32) for w in ws]
    outs = pl.pallas_call(
        body, in_specs=[vm] * (4 * n), out_specs=[vm] * (3 * n), out_shape=shp * 3, name="adamw_small",
    )(*ws, *gs, *ms, *vs)
    return outs[:n], outs[n:2 * n], outs[2 * n:]


def _tile_rows(a):
    a = a.reshape(-1, LANES)
    pad = (-a.shape[0]) % 8
    return jnp.pad(a, ((0, pad), (0, 0))) if pad else a


_SMALL_LAYER_PARTS = (("qn", 8), ("kn", 8), ("sink", 8), ("ga", 8), ("gb", 8), ("ln1g", 8), ("ln1b", 8),
                      ("ln2g", 8), ("ln2b", 8), ("stats", N_SHARD * 8 * FF_SH // LANES))
_SMALL_HEAD_ROWS = 16
_SMALL_LAYER_ROWS = sum(r for _, r in _SMALL_LAYER_PARTS)


def _pack_small(loss, dbucket, grads):
    parts = [_tile_rows(loss), _tile_rows(dbucket)]
    for l in range(DEPTH):
        parts += [_tile_rows(grads[l][name]) for name, _ in _SMALL_LAYER_PARTS]
    return jnp.concatenate(parts, axis=0)


def _unpack_small(tot, chip):
    out = dict(loss=tot[0, 0], rel_bias=tot[8:16, :N_BUCKETS].T)
    per = {name: [] for name, _ in _SMALL_LAYER_PARTS}
    for l in range(DEPTH):
        base = _SMALL_HEAD_ROWS + l * _SMALL_LAYER_ROWS
        for name, rows in _SMALL_LAYER_PARTS:
            per[name].append(tot[base:base + rows])
            base += rows
    fold = lambda v: v[0, :HEAD_DIM] + v[0, HEAD_DIM:]
    out["q_norm"] = jnp.stack([fold(v) for v in per["qn"]])
    out["k_norm"] = jnp.stack([fold(v) for v in per["kn"]])
    out["sink"] = jnp.stack([jnp.sum(v, axis=1) for v in per["sink"]])
    out["out_norm_a"] = jnp.stack([_from_pairs(v[:4].reshape(Q_W), 0) for v in per["ga"]])
    out["out_norm_b"] = jnp.stack([_from_pairs(v[:4].reshape(Q_W), 0) for v in per["gb"]])
    for name, key in (("ln1_g", "ln1g"), ("ln1_b", "ln1b"), ("ln2_g", "ln2g"), ("ln2_b", "ln2b")):
        out[name] = jnp.stack([v.reshape(D_MODEL) for v in per[key]])
    stats = [v.reshape(N_SHARD, 8, FF_SH) for v in per["stats"]]
    out["conv_b"] = jnp.stack([s[:, 0, :].reshape(D_FF) for s in stats])
    out["conv_w"] = jnp.stack([lax.dynamic_index_in_dim(s, chip, 0, keepdims=False)[1:4] for s in stats])
    return out


_WEIGHTS = ("rel_bias", "w_in", "q_norm", "k_norm", "sink", "out_norm_a", "out_norm_b", "w_out", "ln1_g", "ln1_b",
            "w_gate", "w_up", "conv_w", "conv_b", "w_down", "ln2_g", "ln2_b")
_BIG = ("w_in", "w_out", "w_gate", "w_up", "w_down")
_SMALL = tuple(n for n in _WEIGHTS if n not in _BIG)


def _col_blocks(g, n):
    return g.reshape(g.shape[0], N_SHARD, n).transpose(1, 0, 2)


def kernel(x, rel_bias, w_in, q_norm, k_norm, sink, out_norm_a, out_norm_b, w_out, ln1_g, ln1_b, w_gate, w_up, conv_w, conv_b, w_down, ln2_g, ln2_b, loss_target, m_rel_bias, m_w_in, m_q_norm, m_k_norm, m_sink, m_out_norm_a, m_out_norm_b, m_w_out, m_ln1_g, m_ln1_b, m_w_gate, m_w_up, m_conv_w, m_conv_b, m_w_down, m_ln2_g, m_ln2_b, v_rel_bias, v_w_in, v_q_norm, v_k_norm, v_sink, v_out_norm_a, v_out_norm_b, v_w_out, v_ln1_g, v_ln1_b, v_w_gate, v_w_up, v_conv_w, v_conv_b, v_w_down, v_ln2_g, v_ln2_b):
    w = dict(rel_bias=rel_bias, w_in=w_in, q_norm=q_norm, k_norm=k_norm, sink=sink, out_norm_a=out_norm_a,
             out_norm_b=out_norm_b, w_out=w_out, ln1_g=ln1_g, ln1_b=ln1_b, w_gate=w_gate, w_up=w_up, conv_w=conv_w,
             conv_b=conv_b, w_down=w_down, ln2_g=ln2_g, ln2_b=ln2_b)
    m = dict(rel_bias=m_rel_bias, w_in=m_w_in, q_norm=m_q_norm, k_norm=m_k_norm, sink=m_sink, out_norm_a=m_out_norm_a,
             out_norm_b=m_out_norm_b, w_out=m_w_out, ln1_g=m_ln1_g, ln1_b=m_ln1_b, w_gate=m_w_gate, w_up=m_w_up,
             conv_w=m_conv_w, conv_b=m_conv_b, w_down=m_w_down, ln2_g=m_ln2_g, ln2_b=m_ln2_b)
    v = dict(rel_bias=v_rel_bias, w_in=v_w_in, q_norm=v_q_norm, k_norm=v_k_norm, sink=v_sink, out_norm_a=v_out_norm_a,
             out_norm_b=v_out_norm_b, w_out=v_w_out, ln1_g=v_ln1_g, ln1_b=v_ln1_b, w_gate=v_w_gate, w_up=v_w_up,
             conv_w=v_conv_w, conv_b=v_conv_b, w_down=v_w_down, ln2_g=v_ln2_g, ln2_b=v_ln2_b)
    chip = 2 * lax.axis_index("x") + lax.axis_index("y")

    small_w = (q_norm, k_norm, sink, out_norm_a, out_norm_b, conv_b, ln1_g, ln1_b, ln2_g, ln2_b)
    (win0,) = _gather_shards([w_in[0].astype(BF16)])
    later = ([w[name][0].astype(BF16) for name in _BIG[1:]] + [w[name][1].astype(BF16) for name in _BIG] + [conv_w])
    params = [_prep_layer_params(0, win0, None, None, None, None, None, *small_w), None]

    def finish(g):
        wout0, wg0, wu0, wd0, win1, wout1, wg1, wu1, wd1, cw_all = g
        params[0] = _prep_layer_params(0, win0, wout0, wg0, wu0, wd0, cw_all[:, 0], *small_w)
        params[1] = _prep_layer_params(1, win1, wout1, wg1, wu1, wd1, cw_all[:, 1], *small_w)
        return params[0]

    loss, dx, grads, dbucket = _local_step(x[0], loss_target[0], params, rel_bias, gather=(later, finish),
                                           scatter=True)

    small = _unpack_small(_allreduce_small(_pack_small(loss, dbucket, grads)), chip)

    slots = list(grads[0]["slots"])
    slots[0] = _scatter_into([(0, 0)], _block_grads(grads[0], ("w_in",)), [slots[0]])[0]
    partial = [_sum_slots(s) for s in slots]
    other = _swap_with_sibling(partial)

    grad, delta, new_m, new_v = {}, {}, {}, {}
    for i, name in enumerate(_BIG):
        grad[name], delta[name], new_m[name], new_v[name] = _adamw_big(partial[i], other[i], w[name], m[name], v[name])
    flat2 = lambda a: a.reshape(-1, a.shape[-1])
    ds, ms, vs = _adamw_small([flat2(w[n]) for n in _SMALL], [flat2(small[n]) for n in _SMALL],
                              [flat2(m[n]) for n in _SMALL], [flat2(v[n]) for n in _SMALL])
    for i, name in enumerate(_SMALL):
        grad[name] = small[name]
        delta[name] = ds[i].reshape(w[name].shape)
        new_m[name] = ms[i].reshape(w[name].shape)
        new_v[name] = vs[i].reshape(w[name].shape)

    return (small["loss"], dx[None], *[grad[n] for n in _WEIGHTS], *[delta[n] for n in _WEIGHTS],
            *[new_m[n] for n in _WEIGHTS], *[new_v[n] for n in _WEIGHTS])
```

```python
import math

import jax
import jax.numpy as jnp
from jax import lax
from jax.experimental import pallas as pl
from jax.experimental.pallas import tpu as pltpu

F32 = jnp.float32
BF16 = jnp.bfloat16

D_MODEL = 1024
DEPTH = 2
HEAD_DIM = 64
Q_W = 512
KV_W = 128
IN_COLS = 2 * (Q_W + 2 * KV_W)
N_SHARD = 4
IN_SH = IN_COLS // N_SHARD
OUT_SH = D_MODEL // N_SHARD
D_FF = 2816
FF_SH = D_FF // N_SHARD
Q_BLOCK = 128
WINDOW = 128
N_BUCKETS = 32
MAX_DISTANCE = 128
GRID_W = 64
ROPE_THETA = 10000.0
ALPHA = (2.0 * DEPTH) ** 0.25
RMS_EPS = 1e-6
LN_EPS = 1e-5
NEG = -1e30
LANES = 128
VMEM_LIMIT = 56 * 1024 * 1024

ADAM_LR = 0.001
ADAM_B1 = 0.9
ADAM_B2 = 0.999
ADAM_EPS = 1e-08
ADAM_WD = 0.01
ADAM_STEP = 10

_NN = (((1,), (0,)), ((), ()))
_NT = (((1,), (1,)), ((), ()))
_TN = (((0,), (0,)), ((), ()))


def _dot(a, b, dims):
    return lax.dot_general(a.astype(BF16), b.astype(BF16), dims, preferred_element_type=F32)


def _cparams(sem, vmem=VMEM_LIMIT):
    return pltpu.CompilerParams(dimension_semantics=sem, vmem_limit_bytes=vmem)


def _regroup(a, axis, n_outer, n_inner):
    shp = a.shape
    a = a.reshape(shp[:axis] + (n_outer, n_inner, HEAD_DIM) + shp[axis + 1:])
    return jnp.swapaxes(a, axis, axis + 1).reshape(shp)


def _to_pairs(a, axis):
    return _regroup(a, axis, 2, 4)


def _from_pairs(a, axis):
    return _regroup(a, axis, 4, 2)


def _in_cols_to_pairs(w, fn=_to_pairs):
    return jnp.concatenate([fn(w[..., :Q_W], w.ndim - 1), w[..., Q_W:Q_W + 2 * KV_W],
                            fn(w[..., Q_W + 2 * KV_W:2 * Q_W + 2 * KV_W], w.ndim - 1),
                            w[..., 2 * Q_W + 2 * KV_W:]], axis=-1)


def _mix_rows_to_pairs(w, fn=_to_pairs):
    return fn(w.reshape(2, Q_W, w.shape[-1]), 1).reshape(w.shape)


def _matmul(a, b, *, dims, grid, a_spec, b_spec, o_spec, out_shape, acc_shape, name, res=None,
            res_spec=None, res_scale=1.0):
    nk = grid[-1]
    kax = len(grid) - 1

    def body(*refs):
        if res is None:
            a_ref, b_ref, o_ref, acc = refs
            r_ref = None
        else:
            a_ref, b_ref, r_ref, o_ref, acc = refs
        k = pl.program_id(kax)

        @pl.when(k == 0)
        def _():
            acc[...] = jnp.zeros_like(acc)

        acc[...] += _dot(a_ref[...], b_ref[...], dims)

        @pl.when(k == nk - 1)
        def _():
            o = acc[...]
            if r_ref is not None:
                o = o + res_scale * r_ref[...]
            o_ref[...] = o.astype(o_ref.dtype)

    in_specs = [a_spec, b_spec] + ([res_spec] if res is not None else [])
    args = (a, b) + ((res,) if res is not None else ())
    sem = ("parallel",) * kax + ("arbitrary",)
    return pl.pallas_call(
        body, grid=grid, in_specs=in_specs, out_specs=o_spec, out_shape=out_shape,
        scratch_shapes=[pltpu.VMEM(acc_shape, F32)], compiler_params=_cparams(sem), name=name,
    )(*args)


def _mm_nn(a, b, out_dtype, name, tm=512, res=None, res_scale=1.0):
    m, kd = a.shape
    n = b.shape[1]
    return _matmul(
        a, b, dims=_NN, grid=(m // tm, 1),
        a_spec=pl.BlockSpec((tm, kd), lambda i, k: (i, 0)),
        b_spec=pl.BlockSpec((kd, n), lambda i, k: (0, 0)),
        o_spec=pl.BlockSpec((tm, n), lambda i, k: (i, 0)),
        out_shape=jax.ShapeDtypeStruct((m, n), out_dtype), acc_shape=(tm, n), name=name,
        res=res, res_spec=pl.BlockSpec((tm, n), lambda i, k: (i, 0)), res_scale=res_scale)


def _mm_nt(a, b, out_dtype, name, tm=512, res=None, res_scale=1.0):
    m, kd = a.shape
    n = b.shape[0]
    return _matmul(
        a, b, dims=_NT, grid=(m // tm, 1),
        a_spec=pl.BlockSpec((tm, kd), lambda i, k: (i, 0)),
        b_spec=pl.BlockSpec((n, kd), lambda i, k: (0, 0)),
        o_spec=pl.BlockSpec((tm, n), lambda i, k: (i, 0)),
        out_shape=jax.ShapeDtypeStruct((m, n), out_dtype), acc_shape=(tm, n), name=name,
        res=res, res_spec=pl.BlockSpec((tm, n), lambda i, k: (i, 0)), res_scale=res_scale)


def _mm_tn(a, b, name, tk=1024, tn=None, out_dtype=BF16):
    t, m = a.shape
    n = b.shape[1]
    tn = n if tn is None else tn
    tk = min(tk, t)
    return _matmul(
        a, b, dims=_TN, grid=(n // tn, t // tk),
        a_spec=pl.BlockSpec((tk, m), lambda j, k: (k, 0)),
        b_spec=pl.BlockSpec((tk, tn), lambda j, k: (k, j)),
        o_spec=pl.BlockSpec((m, tn), lambda j, k: (0, j)),
        out_shape=jax.ShapeDtypeStruct((m, n), out_dtype), acc_shape=(m, tn), name=name)


def _blocked_n(w, dims):
    return w.shape[2] if dims == _NN else w.shape[1]


def _mm_expand(a, w, dims, out_dtype, name, tm=512):
    m, kd = a.shape
    nb, n = w.shape[0], _blocked_n(w, dims)

    def body(a_ref, w_ref, o_ref):
        av = a_ref[...]
        for j in range(nb):
            o_ref[j] = _dot(av, w_ref[j], dims).astype(o_ref.dtype)

    return pl.pallas_call(
        body, grid=(m // tm,),
        in_specs=[pl.BlockSpec((tm, kd), lambda i: (i, 0)), pl.BlockSpec(w.shape, lambda i: (0, 0, 0))],
        out_specs=pl.BlockSpec((nb, tm, n), lambda i: (0, i, 0)),
        out_shape=jax.ShapeDtypeStruct((nb, m, n), out_dtype),
        compiler_params=_cparams(("parallel",)), name=name,
    )(a, w)


def _mm_reduce(a, w, dims, out_dtype, name, tm=512, res=None, res_scale=1.0, ln=None):
    nb, m, kd = a.shape
    n = _blocked_n(w, dims)
    n_in = 2 + (res is not None) + (2 if ln else 0)

    def body(*refs):
        a_ref, w_ref = refs[0], refs[1]
        acc = _dot(a_ref[0], w_ref[0], dims)
        for j in range(1, nb):
            acc = acc + _dot(a_ref[j], w_ref[j], dims)
        if res is not None:
            acc = acc + res_scale * refs[2][...]
        refs[n_in][...] = acc.astype(out_dtype)
        if ln:
            g_ref, b_ref = refs[n_in - 2], refs[n_in - 1]
            zc = acc - jnp.mean(acc, axis=-1, keepdims=True)
            r = lax.rsqrt(jnp.mean(zc * zc, axis=-1, keepdims=True) + LN_EPS)
            y = zc * r * g_ref[...] + b_ref[...]
            refs[n_in + 1][...] = y
            refs[n_in + 2][...] = y.astype(BF16)

    row = pl.BlockSpec((tm, n), lambda i: (i, 0))
    par = pl.BlockSpec((1, n), lambda i: (0, 0))
    sd = jax.ShapeDtypeStruct
    out = pl.pallas_call(
        body, grid=(m // tm,),
        in_specs=[pl.BlockSpec((nb, tm, kd), lambda i: (0, i, 0)), pl.BlockSpec(w.shape, lambda i: (0, 0, 0))]
        + ([row] if res is not None else []) + ([par, par] if ln else []),
        out_specs=[row] * (3 if ln else 1),
        out_shape=[sd((m, n), out_dtype)] + ([sd((m, n), F32), sd((m, n), BF16)] if ln else []),
        compiler_params=_cparams(("parallel",)), name=name,
    )(a, w, *((res,) if res is not None else ()), *(ln or ()))
    return out if ln else out[0]


def _mm_tn_blocks(a, b, name, blk=0, nb=N_SHARD, tk=1024, out_dtype=BF16):
    a3, b3 = a.ndim == 3, b.ndim == 3
    t, m, n = a.shape[-2], a.shape[-1], b.shape[-1]
    tk = min(tk, t)
    nsteps = t // tk

    def spec(blocked, width):
        if blocked:
            return pl.BlockSpec((nb, tk, width), lambda k: (blk, k, 0))
        return pl.BlockSpec((tk, width), lambda k: (k, 0))

    def body(a_ref, b_ref, o_ref, acc):
        k = pl.program_id(0)

        @pl.when(k == 0)
        def _():
            acc[...] = jnp.zeros_like(acc)

        for j in range(nb):
            acc[j] += _dot(a_ref[j] if a3 else a_ref[...], b_ref[j] if b3 else b_ref[...], _TN)

        @pl.when(k == nsteps - 1)
        def _():
            o_ref[...] = acc[...].astype(o_ref.dtype)

    return pl.pallas_call(
        body, grid=(nsteps,), in_specs=[spec(a3, m), spec(b3, n)],
        out_specs=pl.BlockSpec((nb, m, n), lambda k: (0, 0, 0)),
        out_shape=jax.ShapeDtypeStruct((nb, m, n), out_dtype),
        scratch_shapes=[pltpu.VMEM((nb, m, n), F32)],
        compiler_params=_cparams(("arbitrary",)), name=name,
    )(a, b)


def _row_spec(tm, n):
    return pl.BlockSpec((tm, n), lambda i: (i, 0))


def _par_spec(n, rows=1):
    return pl.BlockSpec((rows, n), lambda i: (0, 0))


def _swap_pairs(x):
    lane = lax.broadcasted_iota(jnp.int32, x.shape, 1)
    return jnp.where(lane % 2 == 0, pltpu.roll(x, LANES - 1, 1), pltpu.roll(x, 1, 1))


def _head_sums(v):
    lo = lax.broadcasted_iota(jnp.int32, v.shape, 1) < HEAD_DIM
    s_lo = jnp.sum(jnp.where(lo, v, 0.0), axis=-1, keepdims=True)
    s_hi = jnp.sum(jnp.where(lo, 0.0, v), axis=-1, keepdims=True)
    return jnp.where(lo, s_lo, s_hi)


def _qk_blocks():
    return [(128 * i, True) for i in range(4)] + [(Q_W, False)]


def _prep_fwd(h, cos_t, sin_t, qn, kn, tm=256):
    t = h.shape[0]
    scale = HEAD_DIM ** -0.5

    def body(h_ref, c_ref, s_ref, qn_ref, kn_ref, qa_ref, ka_ref, va_ref, qb_ref, kb_ref, vb_ref):
        c = c_ref[...]
        s = s_ref[...]
        for start, is_q in _qk_blocks():
            x = h_ref[:, start:start + LANES]
            r = lax.rsqrt(_head_sums(x * x) * (1.0 / HEAD_DIM) + RMS_EPS)
            y = x * r * (qn_ref[...] if is_q else kn_ref[...])
            y = y * c + _swap_pairs(y) * s
            if is_q:
                qa_ref[:, start:start + LANES] = (y * scale).astype(BF16)
            else:
                ka_ref[...] = y.astype(BF16)
        va_ref[...] = h_ref[:, 640:768].astype(BF16)
        qb_ref[...] = (h_ref[:, 768:1280] * scale).astype(BF16)
        kb_ref[...] = h_ref[:, 1280:1408].astype(BF16)
        vb_ref[...] = h_ref[:, 1408:1536].astype(BF16)

    sd = jax.ShapeDtypeStruct
    return pl.pallas_call(
        body, grid=(t // tm,),
        in_specs=[_row_spec(tm, IN_COLS), _row_spec(tm, LANES), _row_spec(tm, LANES), _par_spec(LANES), _par_spec(LANES)],
        out_specs=[_row_spec(tm, Q_W), _row_spec(tm, KV_W), _row_spec(tm, KV_W),
                   _row_spec(tm, Q_W), _row_spec(tm, KV_W), _row_spec(tm, KV_W)],
        out_shape=[sd((t, Q_W), BF16), sd((t, KV_W), BF16), sd((t, KV_W), BF16),
                   sd((t, Q_W), BF16), sd((t, KV_W), BF16), sd((t, KV_W), BF16)],
        compiler_params=_cparams(("parallel",)), name="prep_fwd",
    )(h, cos_t, sin_t, qn, kn)


def _prep_bwd(h, cos_t, sin_t, qn, kn, dqa, dka, dva, dqb, dkb, dvb, tm=256):
    t = h.shape[0]
    scale = HEAD_DIM ** -0.5

    def body(h_ref, c_ref, s_ref, qn_ref, kn_ref, dqa_ref, dka_ref, dva_ref, dqb_ref, dkb_ref, dvb_ref,
             dh_ref, dqn_ref, dkn_ref):
        @pl.when(pl.program_id(0) == 0)
        def _():
            dqn_ref[...] = jnp.zeros_like(dqn_ref)
            dkn_ref[...] = jnp.zeros_like(dkn_ref)

        c = c_ref[...]
        s = s_ref[...]
        for start, is_q in _qk_blocks():
            x = h_ref[:, start:start + LANES]
            gain = qn_ref[...] if is_q else kn_ref[...]
            d = dqa_ref[:, start:start + LANES] * scale if is_q else dka_ref[...]
            dy = d * c + _swap_pairs(d * s)
            r = lax.rsqrt(_head_sums(x * x) * (1.0 / HEAD_DIM) + RMS_EPS)
            xr = x * r
            gsum = jnp.sum(dy * xr, axis=0, keepdims=True)
            if is_q:
                dqn_ref[...] += gsum
            else:
                dkn_ref[...] += gsum
            gy = dy * gain
            dx = r * (gy - xr * (_head_sums(xr * gy) * (1.0 / HEAD_DIM)))
            dh_ref[:, start:start + LANES] = dx.astype(BF16)
        dh_ref[:, 640:768] = dva_ref[...].astype(BF16)
        dh_ref[:, 768:1280] = (dqb_ref[...] * scale).astype(BF16)
        dh_ref[:, 1280:1408] = dkb_ref[...].astype(BF16)
        dh_ref[:, 1408:1536] = dvb_ref[...].astype(BF16)

    sd = jax.ShapeDtypeStruct
    return pl.pallas_call(
        body, grid=(t // tm,),
        in_specs=[_row_spec(tm, IN_COLS), _row_spec(tm, LANES), _row_spec(tm, LANES), _par_spec(LANES), _par_spec(LANES),
                  _row_spec(tm, Q_W), _row_spec(tm, KV_W), _row_spec(tm, KV_W),
                  _row_spec(tm, Q_W), _row_spec(tm, KV_W), _row_spec(tm, KV_W)],
        out_specs=[_row_spec(tm, IN_COLS), _par_spec(LANES), _par_spec(LANES)],
        out_shape=[sd((t, IN_COLS), BF16), sd((1, LANES), F32), sd((1, LANES), F32)],
        compiler_params=_cparams(("arbitrary",)), name="prep_bwd",
    )(h, cos_t, sin_t, qn, kn, dqa, dka, dva, dqb, dkb, dvb)


def _outnorm_fwd(oa, ob, ga, gb, tm=512):
    t = oa.shape[0]

    def body(oa_ref, ob_ref, ga_ref, gb_ref, y_ref):
        for o_ref, g_ref, start in ((oa_ref, ga_ref, 0), (ob_ref, gb_ref, Q_W)):
            x = o_ref[...]
            r = lax.rsqrt(jnp.mean(x * x, axis=-1, keepdims=True) + RMS_EPS)
            y_ref[:, start:start + Q_W] = (x * r * g_ref[...]).astype(BF16)

    return pl.pallas_call(
        body, grid=(t // tm,),
        in_specs=[_row_spec(tm, Q_W), _row_spec(tm, Q_W), _par_spec(Q_W), _par_spec(Q_W)],
        out_specs=_row_spec(tm, D_MODEL), out_shape=jax.ShapeDtypeStruct((t, D_MODEL), BF16),
        compiler_params=_cparams(("parallel",)), name="outnorm_fwd",
    )(oa, ob, ga, gb)


def _outnorm_bwd(dy, oa, ob, ga, gb, tm=512):
    t = oa.shape[0]

    def body(dy_ref, oa_ref, ob_ref, ga_ref, gb_ref, doa_ref, dob_ref, dga_ref, dgb_ref):
        @pl.when(pl.program_id(0) == 0)
        def _():
            dga_ref[...] = jnp.zeros_like(dga_ref)
            dgb_ref[...] = jnp.zeros_like(dgb_ref)

        for o_ref, g_ref, do_ref, dg_ref, start in ((oa_ref, ga_ref, doa_ref, dga_ref, 0),
                                                    (ob_ref, gb_ref, dob_ref, dgb_ref, Q_W)):
            x = o_ref[...]
            d = dy_ref[:, start:start + Q_W]
            r = lax.rsqrt(jnp.mean(x * x, axis=-1, keepdims=True) + RMS_EPS)
            xr = x * r
            dg_ref[...] += jnp.sum(d * xr, axis=0, keepdims=True)
            gy = d * g_ref[...]
            do_ref[...] = r * (gy - xr * jnp.mean(xr * gy, axis=-1, keepdims=True))

    sd = jax.ShapeDtypeStruct
    return pl.pallas_call(
        body, grid=(t // tm,),
        in_specs=[_row_spec(tm, D_MODEL), _row_spec(tm, Q_W), _row_spec(tm, Q_W), _par_spec(Q_W), _par_spec(Q_W)],
        out_specs=[_row_spec(tm, Q_W), _row_spec(tm, Q_W), _par_spec(Q_W), _par_spec(Q_W)],
        out_shape=[sd((t, Q_W), F32), sd((t, Q_W), F32), sd((1, Q_W), F32), sd((1, Q_W), F32)],
        compiler_params=_cparams(("arbitrary",)), name="outnorm_bwd",
    )(dy, oa, ob, ga, gb)


def _ln_bwd(d, z, g, tm=512):
    t = z.shape[0]

    def body(d_ref, z_ref, g_ref, dz_ref, dzb_ref, dg_ref, db_ref):
        @pl.when(pl.program_id(0) == 0)
        def _():
            dg_ref[...] = jnp.zeros_like(dg_ref)
            db_ref[...] = jnp.zeros_like(db_ref)

        zz = z_ref[...]
        dd = d_ref[...]
        mu = jnp.mean(zz, axis=-1, keepdims=True)
        zc = zz - mu
        r = lax.rsqrt(jnp.mean(zc * zc, axis=-1, keepdims=True) + LN_EPS)
        xh = zc * r
        dg_ref[...] += jnp.sum(dd * xh, axis=0, keepdims=True)
        db_ref[...] += jnp.sum(dd, axis=0, keepdims=True)
        dxh = dd * g_ref[...]
        dz = r * (dxh - jnp.mean(dxh, axis=-1, keepdims=True) - xh * jnp.mean(dxh * xh, axis=-1, keepdims=True))
        dz_ref[...] = dz
        dzb_ref[...] = dz.astype(BF16)

    sd = jax.ShapeDtypeStruct
    return pl.pallas_call(
        body, grid=(t // tm,),
        in_specs=[_row_spec(tm, D_MODEL), _row_spec(tm, D_MODEL), _par_spec(D_MODEL)],
        out_specs=[_row_spec(tm, D_MODEL), _row_spec(tm, D_MODEL), _par_spec(D_MODEL), _par_spec(D_MODEL)],
        out_shape=[sd((t, D_MODEL), F32), sd((t, D_MODEL), BF16), sd((1, D_MODEL), F32), sd((1, D_MODEL), F32)],
        compiler_params=_cparams(("arbitrary",)), name="ln_bwd",
    )(d, z, g)


def _loss_grad(y, tgt, tm=512):
    t = y.shape[0]
    nsteps = t // tm

    def body(y_ref, t_ref, dy_ref, loss_ref, acc):
        i = pl.program_id(0)

        @pl.when(i == 0)
        def _():
            acc[...] = jnp.zeros_like(acc)

        e = y_ref[...] - t_ref[...]
        dy_ref[...] = e * (1.0 / D_MODEL)
        acc[...] += jnp.sum(e * e, axis=0, keepdims=True)

        @pl.when(i == nsteps - 1)
        def _():
            tot = jnp.sum(acc[...], axis=-1, keepdims=True) * (0.5 / D_MODEL)
            loss_ref[...] = jnp.broadcast_to(tot, loss_ref.shape)

    sd = jax.ShapeDtypeStruct
    return pl.pallas_call(
        body, grid=(nsteps,),
        in_specs=[_row_spec(tm, D_MODEL), _row_spec(tm, D_MODEL)],
        out_specs=[_row_spec(tm, D_MODEL), _par_spec(LANES)],
        out_shape=[sd((t, D_MODEL), F32), sd((1, LANES), F32)],
        scratch_shapes=[pltpu.VMEM((1, D_MODEL), F32)],
        compiler_params=_cparams(("arbitrary",)), name="loss_grad",
    )(y, tgt)


_GELU_C = math.sqrt(2.0 / math.pi)
_GELU_K = 0.044715
HALO = 16


def _gelu_parts(x):
    x2 = x * x
    th = jnp.tanh(x * (_GELU_C + (_GELU_C * _GELU_K) * x2))
    a = 0.5 + 0.5 * th
    dact = a + (0.5 * x) * (1.0 - th * th) * (_GELU_C + (3.0 * _GELU_C * _GELU_K) * x2)
    return x * a, dact


def _halo_specs(tm, t, shift=0):
    last = t // HALO - 1
    cur = pl.BlockSpec((None, tm, FF_SH), lambda j, i: (j + shift, i, 0))
    prev = pl.BlockSpec((None, HALO, FF_SH), lambda j, i: (j + shift, jnp.maximum(i * (tm // HALO) - 1, 0), 0))
    nxt = pl.BlockSpec((None, HALO, FF_SH), lambda j, i: (j + shift, jnp.minimum((i + 1) * (tm // HALO), last), 0))
    return [prev, cur, nxt]


def _ffn_mid_fwd(gu, cw, tm=512):
    t = gu.shape[1]
    nsteps = t // tm

    def body(gp_ref, g_ref, gn_ref, u_ref, cw_ref, h_ref):
        i = pl.program_id(1)
        gg = g_ref[...].astype(F32)
        row = lax.broadcasted_iota(jnp.int32, gg.shape, 0)
        prev = jnp.where(i == 0, 0.0, gp_ref[...].astype(F32)[HALO - 1:HALO, :])
        nxt = jnp.where(i == nsteps - 1, 0.0, gn_ref[...].astype(F32)[0:1, :])
        g_m1 = jnp.where(row == 0, prev, pltpu.roll(gg, 1, 0))
        g_p1 = jnp.where(row == tm - 1, nxt, pltpu.roll(gg, tm - 1, 0))
        gc =cw_ref[3:4, :] + g_m1 * cw_ref[0:1, :] + gg * cw_ref[1:2, :] + g_p1 * cw_ref[2:3, :]
        act, _ = _gelu_parts(gc)
        h_ref[...] = (act * u_ref[...].astype(F32)).astype(BF16)

    return pl.pallas_call(
        body, grid=(N_SHARD, nsteps),
        in_specs=_halo_specs(tm, t) + [pl.BlockSpec((None, tm, FF_SH), lambda j, i: (j + N_SHARD, i, 0)),
                                       pl.BlockSpec((None, 8, FF_SH), lambda j, i: (j, 0, 0))],
        out_specs=pl.BlockSpec((None, tm, FF_SH), lambda j, i: (j, i, 0)),
        out_shape=jax.ShapeDtypeStruct((N_SHARD, t, FF_SH), BF16),
        compiler_params=_cparams(("parallel", "parallel")), name="ffn_mid_fwd",
    )(gu, gu, gu, gu, cw)


def _ffn_mid_bwd(gu, dh, cw, tm=1024):
    t = gu.shape[1]
    tm = min(tm, t)
    nsteps = t // tm
    te = tm + 2 * HALO

    def body(gp_ref, g_ref, gn_ref, up_ref, u_ref, un_ref, dp_ref, d_ref, dn_ref, cw_ref, dgu_ref, st_ref):
        i = pl.program_id(1)

        @pl.when(i == 0)
        def _():
            st_ref[...] = jnp.zeros_like(st_ref)

        def ext(p_ref, c_ref, n_ref):
            prev = jnp.where(i == 0, 0.0, p_ref[...].astype(F32))
            nxt = jnp.where(i == nsteps - 1, 0.0, n_ref[...].astype(F32))
            return jnp.concatenate([prev, c_ref[...].astype(F32), nxt], axis=0)

        eg = ext(gp_ref, g_ref, gn_ref)
        eu = ext(up_ref, u_ref, un_ref)
        ed = ext(dp_ref, d_ref, dn_ref)
        w0, w1, w2 = cw_ref[0:1, :], cw_ref[1:2, :], cw_ref[2:3, :]
        g_m1 = pltpu.roll(eg, 1, 0)
        g_p1 = pltpu.roll(eg, te - 1, 0)
        gc = cw_ref[3:4, :] + g_m1 * w0 + eg * w1 + g_p1 * w2
        act, dact = _gelu_parts(gc)
        dgc = ed * eu * dact
        dg = pltpu.roll(dgc, te - 1, 0) * w0 + dgc * w1 + pltpu.roll(dgc, 1, 0) * w2
        mid = slice(HALO, HALO + tm)
        dgu_ref[0] = dg[mid].astype(BF16)
        dgu_ref[1] = (ed * act)[mid].astype(BF16)
        sel = dgc[mid]
        parts = [jnp.sum(sel, axis=0, keepdims=True),
                 jnp.sum(sel * g_m1[mid], axis=0, keepdims=True),
                 jnp.sum(sel * eg[mid], axis=0, keepdims=True),
                 jnp.sum(sel * g_p1[mid], axis=0, keepdims=True)]
        r8 = lax.broadcasted_iota(jnp.int32, (8, FF_SH), 0)
        upd = jnp.zeros((8, FF_SH), F32)
        for k, p in enumerate(parts):
            upd = upd + jnp.where(r8 == k, p, 0.0)
        st_ref[...] += upd

    sd = jax.ShapeDtypeStruct
    return pl.pallas_call(
        body, grid=(N_SHARD, nsteps),
        in_specs=_halo_specs(tm, t) + _halo_specs(tm, t, N_SHARD) + _halo_specs(tm, t)
        + [pl.BlockSpec((None, 8, FF_SH), lambda j, i: (j, 0, 0))],
        out_specs=[pl.BlockSpec((2, None, tm, FF_SH), lambda j, i: (0, j, i, 0)),
                   pl.BlockSpec((None, 8, FF_SH), lambda j, i: (j, 0, 0))],
        out_shape=[sd((2, N_SHARD, t, FF_SH), BF16), sd((N_SHARD, 8, FF_SH), F32)],
        compiler_params=_cparams(("parallel", "arbitrary")), name="ffn_mid_bwd",
    )(gu, gu, gu, gu, gu, gu, dh, dh, dh, cw)


def _stack_heads(src_ref, dst_ref, tq):
    lo = lax.broadcasted_iota(jnp.int32, (tq, LANES), 1) < HEAD_DIM
    for i in range(4):
        blk = src_ref[:, LANES * i:LANES * (i + 1)].astype(dst_ref.dtype)
        zero = jnp.zeros_like(blk)
        dst_ref[tq * i:tq * (i + 1), :] = jnp.where(lo, blk, zero)
        dst_ref[tq * (4 + i):tq * (5 + i), :] = jnp.where(lo, zero, blk)


def _gattn_fwd(q, k, v, gather=(), tq=128, tk=2048):
    t = q.shape[0]
    tk = min(tk, t)
    nq, nk, r = t // tq, t // tk, 8 * tq
    ng = len(gather)

    def body(*refs):
        q_ref, k_ref, v_ref = refs[:3]
        o_ref, lse_ref = refs[3 + ng:5 + ng]
        qst, m_s, l_s, acct = refs[5 + 2 * ng:9 + 2 * ng]
        if ng:
            ex = _gather_exchange(refs[3:3 + ng], refs[5 + ng:5 + 2 * ng], *refs[9 + 2 * ng:])
            pl.when(pl.program_id(0) == 0)(ex.start)
        lo_rows = lax.broadcasted_iota(jnp.int32, (LANES, tq), 0) < HEAD_DIM
        for i in range(4):
            bt = q_ref[:, LANES * i:LANES * (i + 1)].astype(F32).T
            qst[:, tq * i:tq * (i + 1)] = jnp.where(lo_rows, bt, 0.0).astype(BF16)
            qst[:, tq * (4 + i):tq * (5 + i)] = jnp.where(lo_rows, 0.0, bt).astype(BF16)
        m_s[...] = jnp.full_like(m_s, NEG)

        def max_step(j, carry):
            off = pl.multiple_of(j * tk, tk)
            st = _dot(k_ref[pl.ds(off, tk), :], qst[...], _NN)
            m_s[...] = jnp.maximum(m_s[...], jnp.max(st.reshape(tk // 8, 8, r), axis=0))
            return carry

        lax.fori_loop(0, nk, max_step, 0)
        m_row = jnp.max(m_s[...], axis=0, keepdims=True)
        l_s[...] = jnp.zeros_like(l_s)
        acct[...] = jnp.zeros_like(acct)

        def sum_step(j, carry):
            off = pl.multiple_of(j * tk, tk)
            st = _dot(k_ref[pl.ds(off, tk), :], qst[...], _NN)
            pt = jnp.exp(st - m_row)
            l_s[...] += jnp.sum(pt.reshape(tk // 8, 8, r), axis=0)
            acct[...] += _dot(v_ref[j], pt, _NN)
            return carry

        lax.fori_loop(0, nk, sum_step, 0)
        l_row = jnp.sum(l_s[...], axis=0, keepdims=True)
        ot = acct[...] / l_row
        for i in range(4):
            pair_t = jnp.where(lo_rows, ot[:, tq * i:tq * (i + 1)], ot[:, tq * (4 + i):tq * (5 + i)])
            o_ref[:, LANES * i:LANES * (i + 1)] = pair_t.T
        lse_ref[...] = m_row + jnp.log(l_row)
        if ng:
            pl.when(pl.program_id(0) == nq - 1)(ex.wait)

    sd = jax.ShapeDtypeStruct
    vt3 = v.reshape(nk, tk, KV_W).transpose(0, 2, 1)
    return pl.pallas_call(
        body, grid=(nq,),
        in_specs=[_row_spec(tq, Q_W), _par_spec(KV_W, t), pl.BlockSpec((nk, KV_W, tk), lambda i: (0, 0, 0))]
        + [_ANY] * ng,
        out_specs=[_row_spec(tq, Q_W), pl.BlockSpec((None, 1, r), lambda i: (i, 0, 0))] + [_ANY] * ng,
        out_shape=[sd((t, Q_W), F32), sd((nq, 1, r), F32)] + _gathered_shapes(gather),
        scratch_shapes=[pltpu.VMEM((LANES, r), BF16), pltpu.VMEM((8, r), F32), pltpu.VMEM((8, r), F32),
                        pltpu.VMEM((LANES, r), F32)] + (_exchange_sems(ng) if ng else []),
        compiler_params=_cparams(("arbitrary",) if ng else ("parallel",)),
        name="gattn_fwd_gather" if ng else "gattn_fwd",
    )(q, k, vt3, *gather)


def _gattn_bwd(q, k, v, o, do, lse, scatter=None, tq=128, tk=1024):
    t = q.shape[0]
    tk = min(tk, t)
    nq, nk, r = t // tq, t // tk, 8 * tq
    items, sgrads = scatter if scatter else ((), ())
    ns = len(sgrads)
    slot_shapes = []
    for j, (o_idx, _) in enumerate(items):
        if o_idx == len(slot_shapes):
            slot_shapes += _slot_shapes([sgrads[j]])
    nslots = len(slot_shapes)

    n_in, n_scr = 7, 6
    kt3 = k.reshape(nk, tk, KV_W).transpose(0, 2, 1)

    def body(*refs):
        q_ref, k_ref, v_ref, kt_ref, o_ref, do_ref, lse_ref = refs[:n_in]
        dq_ref, dk_ref, dv_ref = refs[n_in + ns:n_in + 3 + ns]
        scr = n_in + 3 + ns + nslots
        qs, dos, qst, dost, dlt_row, dqt = refs[scr:scr + n_scr]
        if ns:
            ex = _scatter_exchange(items, refs[n_in:n_in + ns], refs[n_in + 3 + ns:scr], *refs[scr + n_scr:])
            pl.when(pl.program_id(0) == 0)(ex.start)

        @pl.when(pl.program_id(0) == 0)
        def _():
            dk_ref[...] = jnp.zeros_like(dk_ref)
            dv_ref[...] = jnp.zeros_like(dv_ref)

        _stack_heads(q_ref, qs, tq)
        _stack_heads(do_ref, dos, tq)
        lo_rows = lax.broadcasted_iota(jnp.int32, (LANES, tq), 0) < HEAD_DIM
        for i in range(4):
            lo, hi = slice(tq * i, tq * (i + 1)), slice(tq * (4 + i), tq * (5 + i))
            cols = slice(LANES * i, LANES * (i + 1))
            for src, dst in ((q_ref, qst), (do_ref, dost)):
                bt = src[:, cols].astype(F32).T
                dst[:, lo] = jnp.where(lo_rows, bt, 0.0).astype(BF16)
                dst[:, hi] = jnp.where(lo_rows, 0.0, bt).astype(BF16)
            prod_t = (do_ref[:, cols] * o_ref[:, cols]).T
            dlt_row[:, lo] = jnp.sum(prod_t[:HEAD_DIM], axis=0, keepdims=True)
            dlt_row[:, hi] = jnp.sum(prod_t[HEAD_DIM:], axis=0, keepdims=True)
        lse_row = lse_ref[...]
        dqt[...] = jnp.zeros_like(dqt)

        def step(j, carry):
            off = pl.multiple_of(j * tk, tk)
            kc = k_ref[pl.ds(off, tk), :]
            vc = v_ref[pl.ds(off, tk), :]
            p = jnp.exp(_dot(kc, qst[...], _NN) - lse_row)
            dp = _dot(vc, dost[...], _NN)
            ds = (p * (dp - dlt_row[...])).astype(BF16)
            dk_ref[pl.ds(off, tk), :] += _dot(ds, qs[...], _NN)
            dv_ref[pl.ds(off, tk), :] += _dot(p, dos[...], _NN)
            dqt[...] += _dot(kt_ref[j], ds, _NN)
            return carry

        lax.fori_loop(0, nk, step, 0)
        for i in range(4):
            pair_t = jnp.where(lo_rows, dqt[:, tq * i:tq * (i + 1)], dqt[:, tq * (4 + i):tq * (5 + i)])
            dq_ref[:, LANES * i:LANES * (i + 1)] = pair_t.T
        if ns:
            pl.when(pl.program_id(0) == nq - 1)(ex.wait)

    sd = jax.ShapeDtypeStruct
    return pl.pallas_call(
        body, grid=(nq,),
        in_specs=[_row_spec(tq, Q_W), _par_spec(KV_W, t), _par_spec(KV_W, t),
                  pl.BlockSpec((nk, KV_W, tk), lambda i: (0, 0, 0)), _row_spec(tq, Q_W), _row_spec(tq, Q_W),
                  pl.BlockSpec((None, 1, r), lambda i: (i, 0, 0))] + [_ANY] * ns,
        out_specs=[_row_spec(tq, Q_W), _par_spec(KV_W, t), _par_spec(KV_W, t)] + [_ANY] * nslots,
        out_shape=[sd((t, Q_W), F32), sd((t, KV_W), F32), sd((t, KV_W), F32)] + slot_shapes,
        scratch_shapes=[pltpu.VMEM((r, LANES), BF16), pltpu.VMEM((r, LANES), BF16), pltpu.VMEM((LANES, r), BF16),
                        pltpu.VMEM((LANES, r), BF16), pltpu.VMEM((1, r), F32),
                        pltpu.VMEM((LANES, r), F32)] + (_exchange_sems(ns) if ns else []),
        compiler_params=_cparams(("arbitrary",)), name="gattn_bwd_scatter" if ns else "gattn_bwd",
    )(q, k, v, kt3, o, do, lse, *sgrads)


_WQ = Q_BLOCK
_WK = 3 * Q_BLOCK
_WR = 8 * _WQ


def _pairs_transposed(src_ref, dst, tq):
    lo_rows = lax.broadcasted_iota(jnp.int32, (LANES, tq), 0) < HEAD_DIM
    for i in range(4):
        bt = src_ref[:, LANES * i:LANES * (i + 1)].astype(F32).T
        dst[:, tq * i:tq * (i + 1)] = jnp.where(lo_rows, bt, 0.0).astype(BF16)
        dst[:, tq * (4 + i):tq * (5 + i)] = jnp.where(lo_rows, 0.0, bt).astype(BF16)


def _pairs_from_transposed(halves, dst_ref, tq):
    for i in range(4):
        pair_t = jnp.concatenate([h[:, tq * i:tq * (i + 1)] for h in halves], axis=0)
        dst_ref[:, LANES * i:LANES * (i + 1)] = pair_t.T.astype(dst_ref.dtype)


def _kv_quadrants(tq):
    return [(slice(HEAD_DIM * kv, HEAD_DIM * (kv + 1)), slice(4 * tq * kv, 4 * tq * (kv + 1))) for kv in range(2)]


def _wattn_scores_t(kw, qst, bias_ref, n, t):
    kabs = (n - 1) * _WQ + lax.broadcasted_iota(jnp.int32, (_WK, 1), 0)
    st = _dot(kw, qst[...], _NN) + bias_ref[...]
    return jnp.where((kabs >= 0) & (kabs < t), st, NEG)


def _window_t(ref3, n):
    return jnp.concatenate([ref3[n], ref3[n + 1], ref3[n + 2]], axis=1)


def _blocks_transposed(ap):
    return ap.reshape(ap.shape[0] // _WQ, _WQ, KV_W).transpose(0, 2, 1)


def _wattn_fwd(q, kp, vp, bias_t, sink):
    t = q.shape[0]
    nq = t // _WQ
    tp = t + 2 * _WQ
    vpt = _blocks_transposed(vp)

    def body(q_ref, k_ref, vt_ref, b_ref, sk_ref, o_ref, lse_ref, qst):
        n = pl.program_id(0)
        _pairs_transposed(q_ref, qst, _WQ)
        kw = k_ref[pl.ds(pl.multiple_of(n * _WQ, _WQ), _WK), :]
        st = _wattn_scores_t(kw, qst, b_ref, n, t)
        sk = sk_ref[...]
        m = jnp.maximum(jnp.max(st, axis=0, keepdims=True), sk)
        pt = jnp.exp(st - m)
        l = jnp.sum(pt, axis=0, keepdims=True) + jnp.exp(sk - m)
        vwt = _window_t(vt_ref, n)
        halves = [_dot(vwt[rows, :], pt[:, cols], _NN) / l[:, cols] for rows, cols in _kv_quadrants(_WQ)]
        _pairs_from_transposed(halves, o_ref, _WQ)
        lse_ref[...] = m + jnp.log(l)

    sd = jax.ShapeDtypeStruct
    return pl.pallas_call(
        body, grid=(nq,),
        in_specs=[_row_spec(_WQ, Q_W), _par_spec(KV_W, tp), pl.BlockSpec(vpt.shape, lambda i: (0, 0, 0)),
                  _par_spec(_WR, _WK), _par_spec(_WR)],
        out_specs=[_row_spec(_WQ, Q_W), pl.BlockSpec((None, 1, _WR), lambda i: (i, 0, 0))],
        out_shape=[sd((t, Q_W), F32), sd((nq, 1, _WR), F32)],
        scratch_shapes=[pltpu.VMEM((LANES, _WR), BF16)],
        compiler_params=_cparams(("parallel",)), name="wattn_fwd",
    )(q, kp, vpt, bias_t, sink)


def _wattn_bwd(q, kp, vp, bias_t, sink, o, do, lse):
    t = q.shape[0]
    nq = t // _WQ
    tp = t + 2 * _WQ
    kpt = _blocks_transposed(kp)

    def body(q_ref, k_ref, v_ref, kt_ref, b_ref, sk_ref, o_ref, do_ref, lse_ref, dq_ref, dk_ref, dv_ref, db_ref,
             dsk_ref, qs, dos, qst, dost):
        n = pl.program_id(0)

        @pl.when(n == 0)
        def _():
            dk_ref[...] = jnp.zeros_like(dk_ref)
            dv_ref[...] = jnp.zeros_like(dv_ref)
            db_ref[...] = jnp.zeros_like(db_ref)
            dsk_ref[...] = jnp.zeros_like(dsk_ref)

        _stack_heads(q_ref, qs, _WQ)
        _stack_heads(do_ref, dos, _WQ)
        _pairs_transposed(q_ref, qst, _WQ)
        _pairs_transposed(do_ref, dost, _WQ)
        delta = []
        for i in range(4):
            cols = slice(LANES * i, LANES * (i + 1))
            prod_t = (do_ref[:, cols] * o_ref[:, cols]).T
            delta.append((jnp.sum(prod_t[:HEAD_DIM], axis=0, keepdims=True),
                          jnp.sum(prod_t[HEAD_DIM:], axis=0, keepdims=True)))
        dlt = jnp.concatenate([d[0] for d in delta] + [d[1] for d in delta], axis=1)
        off = pl.multiple_of(n * _WQ, _WQ)
        kw = k_ref[pl.ds(off, _WK), :]
        vw = v_ref[pl.ds(off, _WK), :]
        lse_v = lse_ref[...]
        pt = jnp.exp(_wattn_scores_t(kw, qst, b_ref, n, t) - lse_v)
        dpt = _dot(vw, dost[...], _NN)
        ds = pt * (dpt - dlt)
        db_ref[...] += ds
        dsk_ref[...] -= jnp.exp(sk_ref[...] - lse_v) * dlt
        dsb = ds.astype(BF16)
        dk_ref[pl.ds(off, _WK), :] += _dot(dsb, qs[...], _NN)
        dv_ref[pl.ds(off, _WK), :] += _dot(pt, dos[...], _NN)
        kwt = _window_t(kt_ref, n)
        halves = [_dot(kwt[rows, :], dsb[:, cols], _NN) for rows, cols in _kv_quadrants(_WQ)]
        _pairs_from_transposed(halves, dq_ref, _WQ)

    sd = jax.ShapeDtypeStruct
    qb = _row_spec(_WQ, Q_W)
    return pl.pallas_call(
        body, grid=(nq,),
        in_specs=[qb, _par_spec(KV_W, tp), _par_spec(KV_W, tp), pl.BlockSpec(kpt.shape, lambda i: (0, 0, 0)),
                  _par_spec(_WR, _WK), _par_spec(_WR), qb, qb, pl.BlockSpec((None, 1, _WR), lambda i: (i, 0, 0))],
        out_specs=[qb, _par_spec(KV_W, tp), _par_spec(KV_W, tp), _par_spec(_WR, _WK), _par_spec(_WR)],
        out_shape=[sd((t, Q_W), F32), sd((tp, KV_W), F32), sd((tp, KV_W), F32), sd((_WK, _WR), F32), sd((1, _WR), F32)],
        scratch_shapes=[pltpu.VMEM((_WR, LANES), BF16), pltpu.VMEM((_WR, LANES), BF16), pltpu.VMEM((LANES, _WR), BF16),
                        pltpu.VMEM((LANES, _WR), BF16)],
        compiler_params=_cparams(("arbitrary",)), name="wattn_bwd",
    )(q, kp, vp, kpt, bias_t, sink, o, do, lse)


def _bias_bucket_reduce(db0, db1, bucket):
    def body(a_ref, b_ref, bk_ref, o_ref):
        d = a_ref[...] + b_ref[...]
        bk = bk_ref[...]
        lane = lax.broadcasted_iota(jnp.int32, (1, LANES), 1)
        out = jnp.zeros((1, LANES), F32)
        for b in range(N_BUCKETS):
            tot = jnp.sum(jnp.sum(jnp.where(bk == b, d, 0.0), axis=-1, keepdims=True), axis=0, keepdims=True)
            out = out + jnp.where(lane == b, tot, 0.0)
        o_ref[...] = out

    hb = pl.BlockSpec((None, _WQ, _WK), lambda h: (h, 0, 0))
    return pl.pallas_call(
        body, grid=(8,), in_specs=[hb, hb, pl.BlockSpec((_WQ, _WK), lambda h: (0, 0))],
        out_specs=pl.BlockSpec((None, 1, LANES), lambda h: (h, 0, 0)),
        out_shape=jax.ShapeDtypeStruct((8, 1, LANES), F32),
        compiler_params=_cparams(("parallel",)), name="bias_bucket_reduce",
    )(db0.reshape(8, _WQ, _WK), db1.reshape(8, _WQ, _WK), bucket)


def _rope_tables(t):
    rows_n = t // GRID_W
    row = jnp.repeat(jnp.arange(rows_n, dtype=F32), GRID_W)
    col = jnp.tile(jnp.arange(GRID_W, dtype=F32), rows_n)
    half = HEAD_DIM // 2
    inv_freq = ROPE_THETA ** (-jnp.arange(0, half, 2, dtype=F32) / half)
    ang = jnp.concatenate([row[:, None] * inv_freq, col[:, None] * inv_freq], axis=-1)
    cos, sin = jnp.cos(ang), jnp.sin(ang)
    c64 = jnp.repeat(cos, 2, axis=-1)
    s64 = jnp.stack([-sin, sin], axis=-1).reshape(t, HEAD_DIM)
    return jnp.tile(c64, (1, 2)), jnp.tile(s64, (1, 2))


def _t5_bucket(rel):
    half = N_BUCKETS // 2
    max_exact = half // 2
    bucket = jnp.where(rel > 0, half, 0)
    rp = jnp.abs(rel)
    rpf = jnp.maximum(rp, 1).astype(jnp.float32)
    large = max_exact + (jnp.log(rpf / max_exact) / math.log(MAX_DISTANCE / max_exact)
                         * (half - max_exact)).astype(jnp.int32)
    large = jnp.minimum(large, half - 1)
    return bucket + jnp.where(rp < max_exact, rp, large)


def _window_tables(rel_bias):
    qpos = jnp.arange(_WQ, dtype=jnp.int32)
    kpos = jnp.arange(_WK, dtype=jnp.int32) - _WQ
    rel = kpos[None, :] - qpos[:, None]
    bucket = _t5_bucket(rel)
    bias = jnp.zeros((8, _WQ, _WK), F32)
    for b in range(N_BUCKETS):
        bias = jnp.where((bucket == b)[None], rel_bias[b][:, None, None], bias)
    bias = jnp.where((jnp.abs(rel) <= WINDOW)[None], bias, NEG)
    return bias.reshape(_WR, _WK).T, bucket


def _pad_rows(a):
    return jnp.pad(a, ((_WQ, _WQ), (0, 0)))


def _layer_fwd(x, p, tabs, gather=None):
    cos_t, sin_t, bias = tabs
    h = _mm_nn(x, p["win"], F32, "in_proj")
    qa, ka, va, qb, kb, vb = _prep_fwd(h, cos_t, sin_t, p["qn"], p["kn"])
    if gather is None:
        oa, lse_a = _gattn_fwd(qa, ka, va)
    else:
        oa, lse_a, *gathered = _gattn_fwd(qa, ka, va, gather=gather[0])
        p = gather[1](gathered)
    kbp, vbp = _pad_rows(kb), _pad_rows(vb)
    ob, lse_b = _wattn_fwd(qb, kbp, vbp, bias, p["sink"])
    y = _outnorm_fwd(oa, ob, p["ga"], p["gb"])
    z1, x1, x1b = _mm_reduce(y[None], p["wout"][None], _NN, F32, "out_proj", res=x, res_scale=ALPHA,
                             ln=(p["ln1g"], p["ln1b"]))
    gu = _mm_expand(x1b, p["wgu"], _NN, BF16, "gate_up_proj")
    hdn = _ffn_mid_fwd(gu, p["cw"])
    z2, x2, _ = _mm_reduce(hdn, p["wd"], _NN, F32, "down_proj", res=x1, res_scale=ALPHA, ln=(p["ln2g"], p["ln2b"]))
    saved = dict(x=x, h=h, qa=qa, ka=ka, va=va, qb=qb, kbp=kbp, vbp=vbp, oa=oa, ob=ob, lse_a=lse_a, lse_b=lse_b,
                 y=y, z1=z1, x1b=x1b, gu=gu, hdn=hdn, z2=z2)
    return x2, saved


def _block_grads(g, names=("w_in", "w_out", "w_gate", "w_up", "w_down")):
    make = dict(
        w_in=lambda: _col_blocks(_in_cols_to_pairs(g["win"], _from_pairs), IN_SH),
        w_out=lambda: _mix_rows_to_pairs(g["wout"], _from_pairs).reshape(N_SHARD, OUT_SH, D_MODEL),
        w_gate=lambda: g["wg"], w_up=lambda: g["wu"], w_down=lambda: g["wd"])
    return [make[n]() for n in names]


def _layer_bwd(dx2, p, s, tabs, layer=0, pending=None):
    cos_t, sin_t, bias = tabs
    t = dx2.shape[0]
    dz2, dz2b, dln2g, dln2b = _ln_bwd(dx2, s["z2"], p["ln2g"])
    dhdn = _mm_expand(dz2b, p["wd"], _NT, BF16, "down_dx")
    dwd = _mm_tn_blocks(s["hdn"], dz2b, "down_dw")
    dgu, stats = _ffn_mid_bwd(s["gu"], dhdn, p["cw"])
    dgu = dgu.reshape(2 * N_SHARD, t, FF_SH)
    dx1 = _mm_reduce(dgu, p["wgu"], _NT, F32, "gate_up_dx", res=dz2, res_scale=ALPHA)
    dwg = _mm_tn_blocks(dgu, s["x1b"], "gate_dw", blk=0)
    dwu = _mm_tn_blocks(dgu, s["x1b"], "up_dw", blk=1)
    dz1, dz1b, dln1g, dln1b = _ln_bwd(dx1, s["z1"], p["ln1g"])
    dy = _mm_nt(dz1b, p["wout"], F32, "out_dx")
    dwout = _mm_tn(s["y"], dz1b, "out_dw")
    doa, dob, dga, dgb = _outnorm_bwd(dy, s["oa"], s["ob"], p["ga"], p["gb"])
    slots = None
    if pending is None:
        dqa, dka, dva = _gattn_bwd(s["qa"], s["ka"], s["va"], s["oa"], doa, s["lse_a"])
    else:
        mine = _block_grads(dict(wout=dwout, wg=dwg, wu=dwu, wd=dwd), ("w_out", "w_gate", "w_up", "w_down"))
        todo = list(pending) + [(o + 1, layer, g) for o, g in enumerate(mine)]
        dqa, dka, dva, *slots = _gattn_bwd(s["qa"], s["ka"], s["va"], s["oa"], doa, s["lse_a"],
                                           scatter=([(o, l) for o, l, _ in todo], [g for _, _, g in todo]))
    dqb, dkbp, dvbp, dbias, dsink = _wattn_bwd(s["qb"], s["kbp"], s["vbp"], bias, p["sink"], s["ob"], dob, s["lse_b"])
    dkb = lax.slice_in_dim(dkbp, _WQ, _WQ + t, axis=0)
    dvb = lax.slice_in_dim(dvbp, _WQ, _WQ + t, axis=0)
    dh, dqn, dkn = _prep_bwd(s["h"], cos_t, sin_t, p["qn"], p["kn"], dqa, dka, dva, dqb, dkb, dvb)
    dx = _mm_nt(dh, p["win"], F32, "in_dx", res=dz1, res_scale=ALPHA)
    dwin = _mm_tn(s["x"], dh, "in_dw")
    grads = dict(win=dwin, wout=dwout, wg=dwg, wu=dwu, wd=dwd, stats=stats, qn=dqn, kn=dkn, ga=dga, gb=dgb,
                 ln1g=dln1g, ln1b=dln1b, ln2g=dln2g, ln2b=dln2b, bias=dbias, sink=dsink, slots=slots)
    return dx, grads


def _prep_layer_params(l, win, wout, wg, wu, wd, cw, q_norm, k_norm, sink, out_norm_a, out_norm_b, conv_b,
                       ln1_g, ln1_b, ln2_g, ln2_b):
    win_full = win.transpose(1, 0, 2).reshape(D_MODEL, IN_COLS)
    row = lambda v: v.reshape(1, -1)
    late = {}
    if wout is not None:
        late = dict(
            wout=_mix_rows_to_pairs(wout.reshape(D_MODEL, D_MODEL)), wgu=jnp.concatenate([wg, wu], axis=0), wd=wd,
            cw=jnp.pad(cw, ((0, 0), (0, 5), (0, 0)))
            + jnp.pad(conv_b[l].reshape(N_SHARD, 1, FF_SH), ((0, 0), (3, 4), (0, 0))))
    return dict(
        late, win=_in_cols_to_pairs(win_full),
        qn=row(jnp.tile(q_norm[l], 2)), kn=row(jnp.tile(k_norm[l], 2)),
        ga=row(_to_pairs(out_norm_a[l], 0)), gb=row(_to_pairs(out_norm_b[l], 0)),
        ln1g=row(ln1_g[l]), ln1b=row(ln1_b[l]), ln2g=row(ln2_g[l]), ln2b=row(ln2_b[l]),
        sink=jnp.repeat(sink[l], _WQ).reshape(1, _WR))


def _local_step(x, tgt, params, rel_bias, gather=None, scatter=False):
    t = x.shape[0]
    cos_t, sin_t = _rope_tables(t)
    bias, bucket = _window_tables(rel_bias)
    tabs = (cos_t, sin_t, bias)
    saved = []
    for l in range(DEPTH):
        x, s = _layer_fwd(x, params[l], tabs, gather if l == 0 else None)
        saved.append(s)
    dx, loss = _loss_grad(x, tgt)
    grads = [None] * DEPTH
    for l in reversed(range(DEPTH)):
        pending = None
        if scatter and l == 0:
            pending = [(o, 1, g) for o, g in enumerate(_block_grads(grads[1]))]
        dx, grads[l] = _layer_bwd(dx, params[l], saved[l], tabs, l, pending)
    dbucket = _bias_bucket_reduce(grads[0]["bias"].T, grads[1]["bias"].T, bucket)
    return loss, dx, grads, dbucket


_ANY = pl.BlockSpec(memory_space=pl.ANY)
_MESH = pl.DeviceIdType.MESH


def _mesh_pos():
    return lax.axis_index("x"), lax.axis_index("y"), lax.axis_index("c")


def _other_chips(x, y):
    return [(1 - x, y), (x, 1 - y), (1 - x, 1 - y)]


class _Exchange:
    def __init__(self, local, sends, recvs):
        self.local, self.sends, self.recvs = local, sends, recvs

    def start(self):
        for cp in self.local + self.sends:
            cp.start()

    def wait(self):
        for cp in self.recvs:
            cp.wait_recv()
        for cp in self.sends:
            cp.wait_send()
        for cp in self.local:
            cp.wait()


def _exchange_sems(n):
    return [pltpu.SemaphoreType.DMA((n, 3)), pltpu.SemaphoreType.DMA((n, 3)), pltpu.SemaphoreType.DMA((n,))]


def _gather_exchange(ins, outs, send, recv, loc):
    x, y, c = _mesh_pos()
    me = 2 * x + y
    chips = _other_chips(x, y)

    def remote(i, k, block):
        px, py = chips[k]
        return pltpu.make_async_remote_copy(ins[i], outs[i].at[block], send.at[i, k], recv.at[i, k],
                                            device_id=(px, py, c), device_id_type=_MESH)

    n = len(ins)
    local = [pltpu.make_async_copy(ins[i], outs[i].at[me], loc.at[i]) for i in range(n)]
    sends = [remote(i, k, me) for i in range(n) for k in range(3)]
    recvs = [remote(i, k, 2 * chips[k][0] + chips[k][1]) for i in range(n) for k in range(3)]
    return _Exchange(local, sends, recvs)


def _scatter_exchange(items, ins, outs, send, recv, loc):
    x, y, c = _mesh_pos()
    me = 2 * x + y
    chips = _other_chips(x, y)

    def remote(j, k):
        o, l = items[j]
        px, py = chips[k]
        return pltpu.make_async_remote_copy(ins[j].at[2 * px + py], outs[o].at[k, l], send.at[j, k], recv.at[j, k],
                                            device_id=(px, py, c), device_id_type=_MESH)

    local = [pltpu.make_async_copy(ins[j].at[me], outs[o].at[3, l], loc.at[j]) for j, (o, l) in enumerate(items)]
    sends = [remote(j, k) for j in range(len(items)) for k in range(3)]
    return _Exchange(local, sends, sends)


def _gathered_shapes(shards):
    return [jax.ShapeDtypeStruct((N_SHARD,) + s.shape, s.dtype) for s in shards]


def _slot_shapes(blocks):
    return [jax.ShapeDtypeStruct((N_SHARD, DEPTH) + g.shape[1:], g.dtype) for g in blocks]


def _gather_shards(shards):
    n = len(shards)

    def body(*refs):
        ex = _gather_exchange(refs[:n], refs[n:2 * n], *refs[2 * n:])
        ex.start()
        ex.wait()

    return pl.pallas_call(
        body, in_specs=[_ANY] * n, out_specs=[_ANY] * n, out_shape=_gathered_shapes(shards),
        scratch_shapes=_exchange_sems(n), name="gather_weights",
    )(*shards)


def _scatter_into(items, grads, slots):
    n, ns = len(grads), len(slots)

    def body(*refs):
        ex = _scatter_exchange(items, refs[:n], refs[n + ns:n + 2 * ns], *refs[n + 2 * ns:])
        ex.start()
        ex.wait()

    return pl.pallas_call(
        body, in_specs=[_ANY] * (n + ns), out_specs=[_ANY] * ns,
        out_shape=[jax.ShapeDtypeStruct(s.shape, s.dtype) for s in slots],
        input_output_aliases={n + i: i for i in range(ns)},
        scratch_shapes=_exchange_sems(n), name="scatter_grads",
    )(*grads, *slots)


def _swap_with_sibling(parts):
    n = len(parts)

    def body(*refs):
        ins, outs = refs[:n], refs[n:2 * n]
        send, recv = refs[2 * n:]
        x, y, c = _mesh_pos()
        copies = [pltpu.make_async_remote_copy(ins[i], outs[i], send.at[i], recv.at[i], device_id=(x, y, 1 - c),
                                               device_id_type=_MESH) for i in range(n)]
        for cp in copies:
            cp.start()
        for cp in copies:
            cp.wait_recv()
        for cp in copies:
            cp.wait_send()

    return pl.pallas_call(
        body, in_specs=[_ANY] * n, out_specs=[_ANY] * n,
        out_shape=[jax.ShapeDtypeStruct(p.shape, p.dtype) for p in parts],
        scratch_shapes=[pltpu.SemaphoreType.DMA((n,)), pltpu.SemaphoreType.DMA((n,))],
        name="swap_sibling",
    )(*parts)


N_DEV = 8


def _allreduce_small(packed):
    rows = packed.shape[0]

    def body(in_ref, out_ref, buf, send, recv, loc):
        x, y, c = _mesh_pos()
        me = 4 * x + 2 * y + c
        own = pltpu.make_async_copy(in_ref, buf.at[me], loc)
        own.start()

        def remote(m, block):
            peer = (x ^ (m >> 2), y ^ ((m >> 1) & 1), c ^ (m & 1))
            return pltpu.make_async_remote_copy(in_ref, buf.at[block], send.at[m - 1], recv.at[m - 1],
                                                device_id=peer, device_id_type=_MESH)

        sends = [remote(m, me) for m in range(1, N_DEV)]
        for cp in sends:
            cp.start()
        for m in range(1, N_DEV):
            remote(m, me ^ m).wait_recv()
        for cp in sends:
            cp.wait_send()
        own.wait()
        tot = buf[0]
        for d in range(1, N_DEV):
            tot = tot + buf[d]
        out_ref[...] = tot

    vm = pl.BlockSpec(memory_space=pltpu.VMEM)
    return pl.pallas_call(
        body, in_specs=[vm], out_specs=vm, out_shape=jax.ShapeDtypeStruct((rows, LANES), F32),
        scratch_shapes=[pltpu.VMEM((N_DEV, rows, LANES), F32), pltpu.SemaphoreType.DMA((N_DEV - 1,)),
                        pltpu.SemaphoreType.DMA((N_DEV - 1,)), pltpu.SemaphoreType.DMA(())],
        name="allreduce_small",
    )(packed)


def _shard_rows(r):
    return r // 2 if r % 32 == 0 else r


def _sum_slots(slots):
    _, _, r, cdim = slots.shape
    tr = _shard_rows(r)

    def body(a_ref, b_ref, c_ref, d_ref, o_ref):
        up = lambda ref: ref[...].astype(F32)
        o_ref[...] = ((up(d_ref) + up(a_ref)) + up(b_ref)) + up(c_ref)

    def spec(k):
        return pl.BlockSpec((None, None, tr, cdim), lambda l, i: (k, l, i, 0))

    return pl.pallas_call(
        body, grid=(DEPTH, r // tr), in_specs=[spec(0), spec(1), spec(2), spec(3)],
        out_specs=pl.BlockSpec((None, tr, cdim), lambda l, i: (l, i, 0)),
        out_shape=jax.ShapeDtypeStruct((DEPTH, r, cdim), F32),
        compiler_params=_cparams(("parallel", "parallel")), name="sum_slots",
    )(slots, slots, slots, slots)


def _adamw_math(w, g, m, v):
    m = ADAM_B1 * m + (1.0 - ADAM_B1) * g
    v = ADAM_B2 * v + (1.0 - ADAM_B2) * (g * g)
    m_hat = m / (1.0 - ADAM_B1 ** ADAM_STEP)
    v_hat = v / (1.0 - ADAM_B2 ** ADAM_STEP)
    delta = -ADAM_LR * (m_hat / (jnp.sqrt(v_hat) + ADAM_EPS) + ADAM_WD * w)
    return delta, m, v


def _adamw_big(ga, gb, w, m, v):
    _, r, cdim = w.shape
    tr = _shard_rows(r)

    def body(ga_ref, gb_ref, w_ref, m_ref, v_ref, g_out, d_out, m_out, v_out):
        g = ga_ref[...] + gb_ref[...]
        d, mn, vn = _adamw_math(w_ref[...], g, m_ref[...], v_ref[...])
        g_out[...] = g
        d_out[...] = d
        m_out[...] = mn
        v_out[...] = vn

    spec = pl.BlockSpec((None, tr, cdim), lambda l, i: (l, i, 0))
    shp = jax.ShapeDtypeStruct(w.shape, F32)
    return pl.pallas_call(
        body, grid=(DEPTH, r // tr), in_specs=[spec] * 5, out_specs=[spec] * 4, out_shape=[shp] * 4,
        compiler_params=_cparams(("parallel", "parallel")), name="adamw_big",
    )(ga, gb, w, m, v)


def _adamw_small(ws, gs, ms, vs):
    n = len(ws)

    def body(*refs):
        w_r, g_r, m_r, v_r = (refs[k * n:(k + 1) * n] for k in range(4))
        d_o, m_o, v_o = (refs[(4 + k) * n:(5 + k) * n] for k in range(3))
        for i in range(n):
            d, mn, vn = _adamw_math(w_r[i][...], g_r[i][...], m_r[i][...], v_r[i][...])
            d_o[i][...] = d
            m_o[i][...] = mn
            v_o[i][...] = vn

    vm = pl.BlockSpec(memory_space=pltpu.VMEM)
    shp = [jax.ShapeDtypeStruct(w.shape, F32) for w in ws]
    outs = pl.pallas_call(
        body, in_specs=[vm] * (4 * n), out_specs=[vm] * (3 * n), out_shape=shp * 3, name="adamw_small",
    )(*ws, *gs, *ms, *vs)
    return outs[:n], outs[n:2 * n], outs[2 * n:]


def _tile_rows(a):
    a = a.reshape(-1, LANES)
    pad = (-a.shape[0]) % 8
    return jnp.pad(a, ((0, pad), (0, 0))) if pad else a


_SMALL_LAYER_PARTS = (("qn", 8), ("kn", 8), ("sink", 8), ("ga", 8), ("gb", 8), ("ln1g", 8), ("ln1b", 8),
                      ("ln2g", 8), ("ln2b", 8), ("stats", N_SHARD * 8 * FF_SH // LANES))
_SMALL_HEAD_ROWS = 16
_SMALL_LAYER_ROWS = sum(r for _, r in _SMALL_LAYER_PARTS)


def _pack_small(loss, dbucket, grads):
    parts = [_tile_rows(loss), _tile_rows(dbucket)]
    for l in range(DEPTH):
        parts += [_tile_rows(grads[l][name]) for name, _ in _SMALL_LAYER_PARTS]
    return jnp.concatenate(parts, axis=0)


def _unpack_small(tot, chip):
    out = dict(loss=tot[0, 0], rel_bias=tot[8:16, :N_BUCKETS].T)
    per = {name: [] for name, _ in _SMALL_LAYER_PARTS}
    for l in range(DEPTH):
        base = _SMALL_HEAD_ROWS + l * _SMALL_LAYER_ROWS
        for name, rows in _SMALL_LAYER_PARTS:
            per[name].append(tot[base:base + rows])
            base += rows
    fold = lambda v: v[0, :HEAD_DIM] + v[0, HEAD_DIM:]
    out["q_norm"] = jnp.stack([fold(v) for v in per["qn"]])
    out["k_norm"] = jnp.stack([fold(v) for v in per["kn"]])
    out["sink"] = jnp.stack([jnp.sum(v, axis=1) for v in per["sink"]])
    out["out_norm_a"] = jnp.stack([_from_pairs(v[:4].reshape(Q_W), 0) for v in per["ga"]])
    out["out_norm_b"] = jnp.stack([_from_pairs(v[:4].reshape(Q_W), 0) for v in per["gb"]])
    for name, key in (("ln1_g", "ln1g"), ("ln1_b", "ln1b"), ("ln2_g", "ln2g"), ("ln2_b", "ln2b")):
        out[name] = jnp.stack([v.reshape(D_MODEL) for v in per[key]])
    stats = [v.reshape(N_SHARD, 8, FF_SH) for v in per["stats"]]
    out["conv_b"] = jnp.stack([s[:, 0, :].reshape(D_FF) for s in stats])
    out["conv_w"] = jnp.stack([lax.dynamic_index_in_dim(s, chip, 0, keepdims=False)[1:4] for s in stats])
    return out


_WEIGHTS = ("rel_bias", "w_in", "q_norm", "k_norm", "sink", "out_norm_a", "out_norm_b", "w_out", "ln1_g", "ln1_b",
            "w_gate", "w_up", "conv_w", "conv_b", "w_down", "ln2_g", "ln2_b")
_BIG = ("w_in", "w_out", "w_gate", "w_up", "w_down")
_SMALL = tuple(n for n in _WEIGHTS if n not in _BIG)


def _col_blocks(g, n):
    return g.reshape(g.shape[0], N_SHARD, n).transpose(1, 0, 2)


def kernel(x, rel_bias, w_in, q_norm, k_norm, sink, out_norm_a, out_norm_b, w_out, ln1_g, ln1_b, w_gate, w_up, conv_w, conv_b, w_down, ln2_g, ln2_b, loss_target, m_rel_bias, m_w_in, m_q_norm, m_k_norm, m_sink, m_out_norm_a, m_out_norm_b, m_w_out, m_ln1_g, m_ln1_b, m_w_gate, m_w_up, m_conv_w, m_conv_b, m_w_down, m_ln2_g, m_ln2_b, v_rel_bias, v_w_in, v_q_norm, v_k_norm, v_sink, v_out_norm_a, v_out_norm_b, v_w_out, v_ln1_g, v_ln1_b, v_w_gate, v_w_up, v_conv_w, v_conv_b, v_w_down, v_ln2_g, v_ln2_b):
    w = dict(rel_bias=rel_bias, w_in=w_in, q_norm=q_norm, k_norm=k_norm, sink=sink, out_norm_a=out_norm_a,
             out_norm_b=out_norm_b, w_out=w_out, ln1_g=ln1_g, ln1_b=ln1_b, w_gate=w_gate, w_up=w_up, conv_w=conv_w,
             conv_b=conv_b, w_down=w_down, ln2_g=ln2_g, ln2_b=ln2_b)
    m = dict(rel_bias=m_rel_bias, w_in=m_w_in, q_norm=m_q_norm, k_norm=m_k_norm, sink=m_sink, out_norm_a=m_out_norm_a,
             out_norm_b=m_out_norm_b, w_out=m_w_out, ln1_g=m_ln1_g, ln1_b=m_ln1_b, w_gate=m_w_gate, w_up=m_w_up,
             conv_w=m_conv_w, conv_b=m_conv_b, w_down=m_w_down, ln2_g=m_ln2_g, ln2_b=m_ln2_b)
    v = dict(rel_bias=v_rel_bias, w_in=v_w_in, q_norm=v_q_norm, k_norm=v_k_norm, sink=v_sink, out_norm_a=v_out_norm_a,
             out_norm_b=v_out_norm_b, w_out=v_w_out, ln1_g=v_ln1_g, ln1_b=v_ln1_b, w_gate=v_w_gate, w_up=v_w_up,
             conv_w=v_conv_w, conv_b=v_conv_b, w_down=v_w_down, ln2_g=v_ln2_g, ln2_b=v_ln2_b)
    chip = 2 * lax.axis_index("x") + lax.axis_index("y")

    small_w = (q_norm, k_norm, sink, out_norm_a, out_norm_b, conv_b, ln1_g, ln1_b, ln2_g, ln2_b)
    (win0,) = _gather_shards([w_in[0].astype(BF16)])
    later = ([w[name][0].astype(BF16) for name in _BIG[1:]] + [w[name][1].astype(BF16) for name in _BIG] + [conv_w])
    params = [_prep_layer_params(0, win0, None, None, None, None, None, *small_w), None]

    def finish(g):
        wout0, wg0, wu0, wd0, win1, wout1, wg1, wu1, wd1, cw_all = g
        params[0] = _prep_layer_params(0, win0, wout0, wg0, wu0, wd0, cw_all[:, 0], *small_w)
        params[1] = _prep_layer_params(1, win1, wout1, wg1, wu1, wd1, cw_all[:, 1], *small_w)
        return params[0]

    loss, dx, grads, dbucket = _local_step(x[0], loss_target[0], params, rel_bias, gather=(later, finish),
                                           scatter=True)

    small = _unpack_small(_allreduce_small(_pack_small(loss, dbucket, grads)), chip)

    slots = list(grads[0]["slots"])
    slots[0] = _scatter_into([(0, 0)], _block_grads(grads[0], ("w_in",)), [slots[0]])[0]
    partial = [_sum_slots(s) for s in slots]
    other = _swap_with_sibling(partial)

    grad, delta, new_m, new_v = {}, {}, {}, {}
    for i, name in enumerate(_BIG):
        fix = (lambda a: jnp.swapaxes(a, 1, 2)) if name in ("w_gate", "w_up") else (lambda a: a)
        outs = _adamw_big(partial[i], other[i], fix(w[name]), fix(m[name]), fix(v[name]))
        grad[name], delta[name], new_m[name], new_v[name] = [fix(o) for o in outs]
    flat2 = lambda a: a.reshape(-1, a.shape[-1])
    ds, ms, vs = _adamw_small([flat2(w[n]) for n in _SMALL], [flat2(small[n]) for n in _SMALL],
                              [flat2(m[n]) for n in _SMALL], [flat2(v[n]) for n in _SMALL])
    for i, name in enumerate(_SMALL):
        grad[name] = small[name]
        delta[name] = ds[i].reshape(w[name].shape)
        new_m[name] = ms[i].reshape(w[name].shape)
        new_v[name] = vs[i].reshape(w[name].shape)

    return (small["loss"], dx[None], *[grad[n] for n in _WEIGHTS], *[delta[n] for n in _WEIGHTS],
            *[new_m[n] for n in _WEIGHTS], *[new_v[n] for n in _WEIGHTS])
```

```python
import math

import jax
import jax.numpy as jnp
from jax import lax
from jax.experimental import pallas as pl
from jax.experimental.pallas import tpu as pltpu

F32 = jnp.float32
BF16 = jnp.bfloat16

D_MODEL = 1024
DEPTH = 2
HEAD_DIM = 64
Q_W = 512
KV_W = 128
IN_COLS = 2 * (Q_W + 2 * KV_W)
N_SHARD = 4
IN_SH = IN_COLS // N_SHARD
OUT_SH = D_MODEL // N_SHARD
D_FF = 2816
FF_SH = D_FF // N_SHARD
Q_BLOCK = 128
WINDOW = 128
N_BUCKETS = 32
MAX_DISTANCE = 128
GRID_W = 64
ROPE_THETA = 10000.0
ALPHA = (2.0 * DEPTH) ** 0.25
RMS_EPS = 1e-6
LN_EPS = 1e-5
NEG = -1e30
LANES = 128
VMEM_LIMIT = 56 * 1024 * 1024

ADAM_LR = 0.001
ADAM_B1 = 0.9
ADAM_B2 = 0.999
ADAM_EPS = 1e-08
ADAM_WD = 0.01
ADAM_STEP = 10

_NN = (((1,), (0,)), ((), ()))
_NT = (((1,), (1,)), ((), ()))
_TN = (((0,), (0,)), ((), ()))


def _dot(a, b, dims):
    return lax.dot_general(a.astype(BF16), b.astype(BF16), dims, preferred_element_type=F32)


def _cparams(sem, vmem=VMEM_LIMIT):
    return pltpu.CompilerParams(dimension_semantics=sem, vmem_limit_bytes=vmem)


def _regroup(a, axis, n_outer, n_inner):
    shp = a.shape
    a = a.reshape(shp[:axis] + (n_outer, n_inner, HEAD_DIM) + shp[axis + 1:])
    return jnp.swapaxes(a, axis, axis + 1).reshape(shp)


def _to_pairs(a, axis):
    return _regroup(a, axis, 2, 4)


def _from_pairs(a, axis):
    return _regroup(a, axis, 4, 2)


def _in_cols_to_pairs(w, fn=_to_pairs):
    return jnp.concatenate([fn(w[..., :Q_W], w.ndim - 1), w[..., Q_W:Q_W + 2 * KV_W],
                            fn(w[..., Q_W + 2 * KV_W:2 * Q_W + 2 * KV_W], w.ndim - 1),
                            w[..., 2 * Q_W + 2 * KV_W:]], axis=-1)


def _mix_rows_to_pairs(w, fn=_to_pairs):
    return fn(w.reshape(2, Q_W, w.shape[-1]), 1).reshape(w.shape)


def _matmul(a, b, *, dims, grid, a_spec, b_spec, o_spec, out_shape, acc_shape, name, res=None,
            res_spec=None, res_scale=1.0):
    nk = grid[-1]
    kax = len(grid) - 1

    def body(*refs):
        if res is None:
            a_ref, b_ref, o_ref, acc = refs
            r_ref = None
        else:
            a_ref, b_ref, r_ref, o_ref, acc = refs
        k = pl.program_id(kax)

        @pl.when(k == 0)
        def _():
            acc[...] = jnp.zeros_like(acc)

        acc[...] += _dot(a_ref[...], b_ref[...], dims)

        @pl.when(k == nk - 1)
        def _():
            o = acc[...]
            if r_ref is not None:
                o = o + res_scale * r_ref[...]
            o_ref[...] = o.astype(o_ref.dtype)

    in_specs = [a_spec, b_spec] + ([res_spec] if res is not None else [])
    args = (a, b) + ((res,) if res is not None else ())
    sem = ("parallel",) * kax + ("arbitrary",)
    return pl.pallas_call(
        body, grid=grid, in_specs=in_specs, out_specs=o_spec, out_shape=out_shape,
        scratch_shapes=[pltpu.VMEM(acc_shape, F32)], compiler_params=_cparams(sem), name=name,
    )(*args)


def _mm_nn(a, b, out_dtype, name, tm=512, res=None, res_scale=1.0):
    m, kd = a.shape
    n = b.shape[1]
    return _matmul(
        a, b, dims=_NN, grid=(m // tm, 1),
        a_spec=pl.BlockSpec((tm, kd), lambda i, k: (i, 0)),
        b_spec=pl.BlockSpec((kd, n), lambda i, k: (0, 0)),
        o_spec=pl.BlockSpec((tm, n), lambda i, k: (i, 0)),
        out_shape=jax.ShapeDtypeStruct((m, n), out_dtype), acc_shape=(tm, n), name=name,
        res=res, res_spec=pl.BlockSpec((tm, n), lambda i, k: (i, 0)), res_scale=res_scale)


def _mm_nt(a, b, out_dtype, name, tm=512, res=None, res_scale=1.0):
    m, kd = a.shape
    n = b.shape[0]
    return _matmul(
        a, b, dims=_NT, grid=(m // tm, 1),
        a_spec=pl.BlockSpec((tm, kd), lambda i, k: (i, 0)),
        b_spec=pl.BlockSpec((n, kd), lambda i, k: (0, 0)),
        o_spec=pl.BlockSpec((tm, n), lambda i, k: (i, 0)),
        out_shape=jax.ShapeDtypeStruct((m, n), out_dtype), acc_shape=(tm, n), name=name,
        res=res, res_spec=pl.BlockSpec((tm, n), lambda i, k: (i, 0)), res_scale=res_scale)


def _mm_tn(a, b, name, tk=1024, tn=None, out_dtype=BF16):
    t, m = a.shape
    n = b.shape[1]
    tn = n if tn is None else tn
    tk = min(tk, t)
    return _matmul(
        a, b, dims=_TN, grid=(n // tn, t // tk),
        a_spec=pl.BlockSpec((tk, m), lambda j, k: (k, 0)),
        b_spec=pl.BlockSpec((tk, tn), lambda j, k: (k, j)),
        o_spec=pl.BlockSpec((m, tn), lambda j, k: (0, j)),
        out_shape=jax.ShapeDtypeStruct((m, n), out_dtype), acc_shape=(m, tn), name=name)


def _blocked_n(w, dims):
    return w.shape[2] if dims == _NN else w.shape[1]


def _mm_expand(a, w, dims, out_dtype, name, tm=512):
    m, kd = a.shape
    nb, n = w.shape[0], _blocked_n(w, dims)

    def body(a_ref, w_ref, o_ref):
        av = a_ref[...]
        for j in range(nb):
            o_ref[j] = _dot(av, w_ref[j], dims).astype(o_ref.dtype)

    return pl.pallas_call(
        body, grid=(m // tm,),
        in_specs=[pl.BlockSpec((tm, kd), lambda i: (i, 0)), pl.BlockSpec(w.shape, lambda i: (0, 0, 0))],
        out_specs=pl.BlockSpec((nb, tm, n), lambda i: (0, i, 0)),
        out_shape=jax.ShapeDtypeStruct((nb, m, n), out_dtype),
        compiler_params=_cparams(("parallel",)), name=name,
    )(a, w)


def _mm_reduce(a, w, dims, out_dtype, name, tm=512, res=None, res_scale=1.0, ln=None):
    nb, m, kd = a.shape
    n = _blocked_n(w, dims)
    n_in = 2 + (res is not None) + (2 if ln else 0)

    def body(*refs):
        a_ref, w_ref = refs[0], refs[1]
        acc = _dot(a_ref[0], w_ref[0], dims)
        for j in range(1, nb):
            acc = acc + _dot(a_ref[j], w_ref[j], dims)
        if res is not None:
            acc = acc + res_scale * refs[2][...]
        refs[n_in][...] = acc.astype(out_dtype)
        if ln:
            g_ref, b_ref = refs[n_in - 2], refs[n_in - 1]
            zc = acc - jnp.mean(acc, axis=-1, keepdims=True)
            r = lax.rsqrt(jnp.mean(zc * zc, axis=-1, keepdims=True) + LN_EPS)
            y = zc * r * g_ref[...] + b_ref[...]
            refs[n_in + 1][...] = y
            refs[n_in + 2][...] = y.astype(BF16)

    row = pl.BlockSpec((tm, n), lambda i: (i, 0))
    par = pl.BlockSpec((1, n), lambda i: (0, 0))
    sd = jax.ShapeDtypeStruct
    out = pl.pallas_call(
        body, grid=(m // tm,),
        in_specs=[pl.BlockSpec((nb, tm, kd), lambda i: (0, i, 0)), pl.BlockSpec(w.shape, lambda i: (0, 0, 0))]
        + ([row] if res is not None else []) + ([par, par] if ln else []),
        out_specs=[row] * (3 if ln else 1),
        out_shape=[sd((m, n), out_dtype)] + ([sd((m, n), F32), sd((m, n), BF16)] if ln else []),
        compiler_params=_cparams(("parallel",)), name=name,
    )(a, w, *((res,) if res is not None else ()), *(ln or ()))
    return out if ln else out[0]


def _mm_tn_blocks(a, b, name, blk=0, nb=N_SHARD, tk=1024, out_dtype=BF16):
    a3, b3 = a.ndim == 3, b.ndim == 3
    t, m, n = a.shape[-2], a.shape[-1], b.shape[-1]
    tk = min(tk, t)
    nsteps = t // tk

    def spec(blocked, width):
        if blocked:
            return pl.BlockSpec((nb, tk, width), lambda k: (blk, k, 0))
        return pl.BlockSpec((tk, width), lambda k: (k, 0))

    def body(a_ref, b_ref, o_ref, acc):
        k = pl.program_id(0)

        @pl.when(k == 0)
        def _():
            acc[...] = jnp.zeros_like(acc)

        for j in range(nb):
            acc[j] += _dot(a_ref[j] if a3 else a_ref[...], b_ref[j] if b3 else b_ref[...], _TN)

        @pl.when(k == nsteps - 1)
        def _():
            o_ref[...] = acc[...].astype(o_ref.dtype)

    return pl.pallas_call(
        body, grid=(nsteps,), in_specs=[spec(a3, m), spec(b3, n)],
        out_specs=pl.BlockSpec((nb, m, n), lambda k: (0, 0, 0)),
        out_shape=jax.ShapeDtypeStruct((nb, m, n), out_dtype),
        scratch_shapes=[pltpu.VMEM((nb, m, n), F32)],
        compiler_params=_cparams(("arbitrary",)), name=name,
    )(a, b)


def _row_spec(tm, n):
    return pl.BlockSpec((tm, n), lambda i: (i, 0))


def _par_spec(n, rows=1):
    return pl.BlockSpec((rows, n), lambda i: (0, 0))


def _swap_pairs(x):
    lane = lax.broadcasted_iota(jnp.int32, x.shape, 1)
    return jnp.where(lane % 2 == 0, pltpu.roll(x, LANES - 1, 1), pltpu.roll(x, 1, 1))


def _head_sums(v):
    lo = lax.broadcasted_iota(jnp.int32, v.shape, 1) < HEAD_DIM
    s_lo = jnp.sum(jnp.where(lo, v, 0.0), axis=-1, keepdims=True)
    s_hi = jnp.sum(jnp.where(lo, 0.0, v), axis=-1, keepdims=True)
    return jnp.where(lo, s_lo, s_hi)


def _qk_blocks():
    return [(128 * i, True) for i in range(4)] + [(Q_W, False)]


def _prep_fwd(h, cos_t, sin_t, qn, kn, tm=256):
    t = h.shape[0]
    scale = HEAD_DIM ** -0.5

    def body(h_ref, c_ref, s_ref, qn_ref, kn_ref, qa_ref, ka_ref, va_ref, qb_ref, kb_ref, vb_ref):
        c = c_ref[...]
        s = s_ref[...]
        for start, is_q in _qk_blocks():
            x = h_ref[:, start:start + LANES]
            r = lax.rsqrt(_head_sums(x * x) * (1.0 / HEAD_DIM) + RMS_EPS)
            y = x * r * (qn_ref[...] if is_q else kn_ref[...])
            y = y * c + _swap_pairs(y) * s
            if is_q:
                qa_ref[:, start:start + LANES] = (y * scale).astype(BF16)
            else:
                ka_ref[...] = y.astype(BF16)
        va_ref[...] = h_ref[:, 640:768].astype(BF16)
        qb_ref[...] = (h_ref[:, 768:1280] * scale).astype(BF16)
        kb_ref[...] = h_ref[:, 1280:1408].astype(BF16)
        vb_ref[...] = h_ref[:, 1408:1536].astype(BF16)

    sd = jax.ShapeDtypeStruct
    return pl.pallas_call(
        body, grid=(t // tm,),
        in_specs=[_row_spec(tm, IN_COLS), _row_spec(tm, LANES), _row_spec(tm, LANES), _par_spec(LANES), _par_spec(LANES)],
        out_specs=[_row_spec(tm, Q_W), _row_spec(tm, KV_W), _row_spec(tm, KV_W),
                   _row_spec(tm, Q_W), _row_spec(tm, KV_W), _row_spec(tm, KV_W)],
        out_shape=[sd((t, Q_W), BF16), sd((t, KV_W), BF16), sd((t, KV_W), BF16),
                   sd((t, Q_W), BF16), sd((t, KV_W), BF16), sd((t, KV_W), BF16)],
        compiler_params=_cparams(("parallel",)), name="prep_fwd",
    )(h, cos_t, sin_t, qn, kn)


def _prep_bwd(h, cos_t, sin_t, qn, kn, dqa, dka, dva, dqb, dkb, dvb, tm=256):
    t = h.shape[0]
    scale = HEAD_DIM ** -0.5

    def body(h_ref, c_ref, s_ref, qn_ref, kn_ref, dqa_ref, dka_ref, dva_ref, dqb_ref, dkb_ref, dvb_ref,
             dh_ref, dqn_ref, dkn_ref):
        @pl.when(pl.program_id(0) == 0)
        def _():
            dqn_ref[...] = jnp.zeros_like(dqn_ref)
            dkn_ref[...] = jnp.zeros_like(dkn_ref)

        c = c_ref[...]
        s = s_ref[...]
        for start, is_q in _qk_blocks():
            x = h_ref[:, start:start + LANES]
            gain = qn_ref[...] if is_q else kn_ref[...]
            d = dqa_ref[:, start:start + LANES] * scale if is_q else dka_ref[...]
            dy = d * c + _swap_pairs(d * s)
            r = lax.rsqrt(_head_sums(x * x) * (1.0 / HEAD_DIM) + RMS_EPS)
            xr = x * r
            gsum = jnp.sum(dy * xr, axis=0, keepdims=True)
            if is_q:
                dqn_ref[...] += gsum
            else:
                dkn_ref[...] += gsum
            gy = dy * gain
            dx = r * (gy - xr * (_head_sums(xr * gy) * (1.0 / HEAD_DIM)))
            dh_ref[:, start:start + LANES] = dx.astype(BF16)
        dh_ref[:, 640:768] = dva_ref[...].astype(BF16)
        dh_ref[:, 768:1280] = (dqb_ref[...] * scale).astype(BF16)
        dh_ref[:, 1280:1408] = dkb_ref[...].astype(BF16)
        dh_ref[:, 1408:1536] = dvb_ref[...].astype(BF16)

    sd = jax.ShapeDtypeStruct
    return pl.pallas_call(
        body, grid=(t // tm,),
        in_specs=[_row_spec(tm, IN_COLS), _row_spec(tm, LANES), _row_spec(tm, LANES), _par_spec(LANES), _par_spec(LANES),
                  _row_spec(tm, Q_W), _row_spec(tm, KV_W), _row_spec(tm, KV_W),
                  _row_spec(tm, Q_W), _row_spec(tm, KV_W), _row_spec(tm, KV_W)],
        out_specs=[_row_spec(tm, IN_COLS), _par_spec(LANES), _par_spec(LANES)],
        out_shape=[sd((t, IN_COLS), BF16), sd((1, LANES), F32), sd((1, LANES), F32)],
        compiler_params=_cparams(("arbitrary",)), name="prep_bwd",
    )(h, cos_t, sin_t, qn, kn, dqa, dka, dva, dqb, dkb, dvb)


def _outnorm_fwd(oa, ob, ga, gb, tm=512):
    t = oa.shape[0]

    def body(oa_ref, ob_ref, ga_ref, gb_ref, y_ref):
        for o_ref, g_ref, start in ((oa_ref, ga_ref, 0), (ob_ref, gb_ref, Q_W)):
            x = o_ref[...]
            r = lax.rsqrt(jnp.mean(x * x, axis=-1, keepdims=True) + RMS_EPS)
            y_ref[:, start:start + Q_W] = (x * r * g_ref[...]).astype(BF16)

    return pl.pallas_call(
        body, grid=(t // tm,),
        in_specs=[_row_spec(tm, Q_W), _row_spec(tm, Q_W), _par_spec(Q_W), _par_spec(Q_W)],
        out_specs=_row_spec(tm, D_MODEL), out_shape=jax.ShapeDtypeStruct((t, D_MODEL), BF16),
        compiler_params=_cparams(("parallel",)), name="outnorm_fwd",
    )(oa, ob, ga, gb)


def _outnorm_bwd(dy, oa, ob, ga, gb, tm=512):
    t = oa.shape[0]

    def body(dy_ref, oa_ref, ob_ref, ga_ref, gb_ref, doa_ref, dob_ref, dga_ref, dgb_ref):
        @pl.when(pl.program_id(0) == 0)
        def _():
            dga_ref[...] = jnp.zeros_like(dga_ref)
            dgb_ref[...] = jnp.zeros_like(dgb_ref)

        for o_ref, g_ref, do_ref, dg_ref, start in ((oa_ref, ga_ref, doa_ref, dga_ref, 0),
                                                    (ob_ref, gb_ref, dob_ref, dgb_ref, Q_W)):
            x = o_ref[...]
            d = dy_ref[:, start:start + Q_W]
            r = lax.rsqrt(jnp.mean(x * x, axis=-1, keepdims=True) + RMS_EPS)
            xr = x * r
            dg_ref[...] += jnp.sum(d * xr, axis=0, keepdims=True)
            gy = d * g_ref[...]
            do_ref[...] = r * (gy - xr * jnp.mean(xr * gy, axis=-1, keepdims=True))

    sd = jax.ShapeDtypeStruct
    return pl.pallas_call(
        body, grid=(t // tm,),
        in_specs=[_row_spec(tm, D_MODEL), _row_spec(tm, Q_W), _row_spec(tm, Q_W), _par_spec(Q_W), _par_spec(Q_W)],
        out_specs=[_row_spec(tm, Q_W), _row_spec(tm, Q_W), _par_spec(Q_W), _par_spec(Q_W)],
        out_shape=[sd((t, Q_W), F32), sd((t, Q_W), F32), sd((1, Q_W), F32), sd((1, Q_W), F32)],
        compiler_params=_cparams(("arbitrary",)), name="outnorm_bwd",
    )(dy, oa, ob, ga, gb)


def _ln_bwd(d, z, g, tm=512):
    t = z.shape[0]

    def body(d_ref, z_ref, g_ref, dz_ref, dzb_ref, dg_ref, db_ref):
        @pl.when(pl.program_id(0) == 0)
        def _():
            dg_ref[...] = jnp.zeros_like(dg_ref)
            db_ref[...] = jnp.zeros_like(db_ref)

        zz = z_ref[...]
        dd = d_ref[...]
        mu = jnp.mean(zz, axis=-1, keepdims=True)
        zc = zz - mu
        r = lax.rsqrt(jnp.mean(zc * zc, axis=-1, keepdims=True) + LN_EPS)
        xh = zc * r
        dg_ref[...] += jnp.sum(dd * xh, axis=0, keepdims=True)
        db_ref[...] += jnp.sum(dd, axis=0, keepdims=True)
        dxh = dd * g_ref[...]
        dz = r * (dxh - jnp.mean(dxh, axis=-1, keepdims=True) - xh * jnp.mean(dxh * xh, axis=-1, keepdims=True))
        dz_ref[...] = dz
        dzb_ref[...] = dz.astype(BF16)

    sd = jax.ShapeDtypeStruct
    return pl.pallas_call(
        body, grid=(t // tm,),
        in_specs=[_row_spec(tm, D_MODEL), _row_spec(tm, D_MODEL), _par_spec(D_MODEL)],
        out_specs=[_row_spec(tm, D_MODEL), _row_spec(tm, D_MODEL), _par_spec(D_MODEL), _par_spec(D_MODEL)],
        out_shape=[sd((t, D_MODEL), F32), sd((t, D_MODEL), BF16), sd((1, D_MODEL), F32), sd((1, D_MODEL), F32)],
        compiler_params=_cparams(("arbitrary",)), name="ln_bwd",
    )(d, z, g)


def _loss_grad(y, tgt, tm=512):
    t = y.shape[0]
    nsteps = t // tm

    def body(y_ref, t_ref, dy_ref, loss_ref, acc):
        i = pl.program_id(0)

        @pl.when(i == 0)
        def _():
            acc[...] = jnp.zeros_like(acc)

        e = y_ref[...] - t_ref[...]
        dy_ref[...] = e * (1.0 / D_MODEL)
        acc[...] += jnp.sum(e * e, axis=0, keepdims=True)

        @pl.when(i == nsteps - 1)
        def _():
            tot = jnp.sum(acc[...], axis=-1, keepdims=True) * (0.5 / D_MODEL)
            loss_ref[...] = jnp.broadcast_to(tot, loss_ref.shape)

    sd = jax.ShapeDtypeStruct
    return pl.pallas_call(
        body, grid=(nsteps,),
        in_specs=[_row_spec(tm, D_MODEL), _row_spec(tm, D_MODEL)],
        out_specs=[_row_spec(tm, D_MODEL), _par_spec(LANES)],
        out_shape=[sd((t, D_MODEL), F32), sd((1, LANES), F32)],
        scratch_shapes=[pltpu.VMEM((1, D_MODEL), F32)],
        compiler_params=_cparams(("arbitrary",)), name="loss_grad",
    )(y, tgt)


_GELU_C = math.sqrt(2.0 / math.pi)
_GELU_K = 0.044715
HALO = 16


def _gelu_parts(x):
    x2 = x * x
    th = jnp.tanh(x * (_GELU_C + (_GELU_C * _GELU_K) * x2))
    a = 0.5 + 0.5 * th
    dact = a + (0.5 * x) * (1.0 - th * th) * (_GELU_C + (3.0 * _GELU_C * _GELU_K) * x2)
    return x * a, dact


def _halo_specs(tm, t, shift=0):
    last = t // HALO - 1
    cur = pl.BlockSpec((None, tm, FF_SH), lambda j, i: (j + shift, i, 0))
    prev = pl.BlockSpec((None, HALO, FF_SH), lambda j, i: (j + shift, jnp.maximum(i * (tm // HALO) - 1, 0), 0))
    nxt = pl.BlockSpec((None, HALO, FF_SH), lambda j, i: (j + shift, jnp.minimum((i + 1) * (tm // HALO), last), 0))
    return [prev, cur, nxt]


def _ffn_mid_fwd(gu, cw, tm=512):
    t = gu.shape[1]
    nsteps = t // tm

    def body(gp_ref, g_ref, gn_ref, u_ref, cw_ref, h_ref):
        i = pl.program_id(1)
        gg = g_ref[...].astype(F32)
        row = lax.broadcasted_iota(jnp.int32, gg.shape, 0)
        prev = jnp.where(i == 0, 0.0, gp_ref[...].astype(F32)[HALO - 1:HALO, :])
        nxt = jnp.where(i == nsteps - 1, 0.0, gn_ref[...].astype(F32)[0:1, :])
        g_m1 = jnp.where(row == 0, prev, pltpu.roll(gg, 1, 0))
        g_p1 = jnp.where(row == tm - 1, nxt, pltpu.roll(gg, tm - 1, 0))
        gc =cw_ref[3:4, :] + g_m1 * cw_ref[0:1, :] + gg * cw_ref[1:2, :] + g_p1 * cw_ref[2:3, :]
        act, _ = _gelu_parts(gc)
        h_ref[...] = (act * u_ref[...].astype(F32)).astype(BF16)

    return pl.pallas_call(
        body, grid=(N_SHARD, nsteps),
        in_specs=_halo_specs(tm, t) + [pl.BlockSpec((None, tm, FF_SH), lambda j, i: (j + N_SHARD, i, 0)),
                                       pl.BlockSpec((None, 8, FF_SH), lambda j, i: (j, 0, 0))],
        out_specs=pl.BlockSpec((None, tm, FF_SH), lambda j, i: (j, i, 0)),
        out_shape=jax.ShapeDtypeStruct((N_SHARD, t, FF_SH), BF16),
        compiler_params=_cparams(("parallel", "parallel")), name="ffn_mid_fwd",
    )(gu, gu, gu, gu, cw)


def _ffn_mid_bwd(gu, dh, cw, tm=1024):
    t = gu.shape[1]
    tm = min(tm, t)
    nsteps = t // tm
    te = tm + 2 * HALO

    def body(gp_ref, g_ref, gn_ref, up_ref, u_ref, un_ref, dp_ref, d_ref, dn_ref, cw_ref, dgu_ref, st_ref):
        i = pl.program_id(1)

        @pl.when(i == 0)
        def _():
            st_ref[...] = jnp.zeros_like(st_ref)

        def ext(p_ref, c_ref, n_ref):
            prev = jnp.where(i == 0, 0.0, p_ref[...].astype(F32))
            nxt = jnp.where(i == nsteps - 1, 0.0, n_ref[...].astype(F32))
            return jnp.concatenate([prev, c_ref[...].astype(F32), nxt], axis=0)

        eg = ext(gp_ref, g_ref, gn_ref)
        eu = ext(up_ref, u_ref, un_ref)
        ed = ext(dp_ref, d_ref, dn_ref)
        w0, w1, w2 = cw_ref[0:1, :], cw_ref[1:2, :], cw_ref[2:3, :]
        g_m1 = pltpu.roll(eg, 1, 0)
        g_p1 = pltpu.roll(eg, te - 1, 0)
        gc = cw_ref[3:4, :] + g_m1 * w0 + eg * w1 + g_p1 * w2
        act, dact = _gelu_parts(gc)
        dgc = ed * eu * dact
        dg = pltpu.roll(dgc, te - 1, 0) * w0 + dgc * w1 + pltpu.roll(dgc, 1, 0) * w2
        mid = slice(HALO, HALO + tm)
        dgu_ref[0] = dg[mid].astype(BF16)
        dgu_ref[1] = (ed * act)[mid].astype(BF16)
        sel = dgc[mid]
        parts = [jnp.sum(sel, axis=0, keepdims=True),
                 jnp.sum(sel * g_m1[mid], axis=0, keepdims=True),
                 jnp.sum(sel * eg[mid], axis=0, keepdims=True),
                 jnp.sum(sel * g_p1[mid], axis=0, keepdims=True)]
        r8 = lax.broadcasted_iota(jnp.int32, (8, FF_SH), 0)
        upd = jnp.zeros((8, FF_SH), F32)
        for k, p in enumerate(parts):
            upd = upd + jnp.where(r8 == k, p, 0.0)
        st_ref[...] += upd

    sd = jax.ShapeDtypeStruct
    return pl.pallas_call(
        body, grid=(N_SHARD, nsteps),
        in_specs=_halo_specs(tm, t) + _halo_specs(tm, t, N_SHARD) + _halo_specs(tm, t)
        + [pl.BlockSpec((None, 8, FF_SH), lambda j, i: (j, 0, 0))],
        out_specs=[pl.BlockSpec((2, None, tm, FF_SH), lambda j, i: (0, j, i, 0)),
                   pl.BlockSpec((None, 8, FF_SH), lambda j, i: (j, 0, 0))],
        out_shape=[sd((2, N_SHARD, t, FF_SH), BF16), sd((N_SHARD, 8, FF_SH), F32)],
        compiler_params=_cparams(("parallel", "arbitrary")), name="ffn_mid_bwd",
    )(gu, gu, gu, gu, gu, gu, dh, dh, dh, cw)


def _stack_heads(src_ref, dst_ref, tq):
    lo = lax.broadcasted_iota(jnp.int32, (tq, LANES), 1) < HEAD_DIM
    for i in range(4):
        blk = src_ref[:, LANES * i:LANES * (i + 1)].astype(dst_ref.dtype)
        zero = jnp.zeros_like(blk)
        dst_ref[tq * i:tq * (i + 1), :] = jnp.where(lo, blk, zero)
        dst_ref[tq * (4 + i):tq * (5 + i), :] = jnp.where(lo, zero, blk)


def _gattn_fwd(q, k, v, gather=(), tq=128, tk=2048):
    t = q.shape[0]
    tk = min(tk, t)
    nq, nk, r = t // tq, t // tk, 8 * tq
    ng = len(gather)

    def body(*refs):
        q_ref, k_ref, v_ref = refs[:3]
        o_ref, lse_ref = refs[3 + ng:5 + ng]
        qst, m_s, l_s, acct = refs[5 + 2 * ng:9 + 2 * ng]
        if ng:
            ex = _gather_exchange(refs[3:3 + ng], refs[5 + ng:5 + 2 * ng], *refs[9 + 2 * ng:])
            pl.when(pl.program_id(0) == 0)(ex.start)
        lo_rows = lax.broadcasted_iota(jnp.int32, (LANES, tq), 0) < HEAD_DIM
        for i in range(4):
            bt = q_ref[:, LANES * i:LANES * (i + 1)].astype(F32).T
            qst[:, tq * i:tq * (i + 1)] = jnp.where(lo_rows, bt, 0.0).astype(BF16)
            qst[:, tq * (4 + i):tq * (5 + i)] = jnp.where(lo_rows, 0.0, bt).astype(BF16)
        m_s[...] = jnp.full_like(m_s, NEG)

        def max_step(j, carry):
            off = pl.multiple_of(j * tk, tk)
            st = _dot(k_ref[pl.ds(off, tk), :], qst[...], _NN)
            m_s[...] = jnp.maximum(m_s[...], jnp.max(st.reshape(tk // 8, 8, r), axis=0))
            return carry

        lax.fori_loop(0, nk, max_step, 0, unroll=2)
        m_row = jnp.max(m_s[...], axis=0, keepdims=True)
        l_s[...] = jnp.zeros_like(l_s)
        acct[...] = jnp.zeros_like(acct)

        def sum_step(j, carry):
            off = pl.multiple_of(j * tk, tk)
            st = _dot(k_ref[pl.ds(off, tk), :], qst[...], _NN)
            pt = jnp.exp(st - m_row)
            l_s[...] += jnp.sum(pt.reshape(tk // 8, 8, r), axis=0)
            acct[...] += _dot(v_ref[j], pt, _NN)
            return carry

        lax.fori_loop(0, nk, sum_step, 0, unroll=2)
        l_row = jnp.sum(l_s[...], axis=0, keepdims=True)
        ot = acct[...] / l_row
        for i in range(4):
            pair_t = jnp.where(lo_rows, ot[:, tq * i:tq * (i + 1)], ot[:, tq * (4 + i):tq * (5 + i)])
            o_ref[:, LANES * i:LANES * (i + 1)] = pair_t.T
        lse_ref[...] = m_row + jnp.log(l_row)
        if ng:
            pl.when(pl.program_id(0) == nq - 1)(ex.wait)

    sd = jax.ShapeDtypeStruct
    vt3 = v.reshape(nk, tk, KV_W).transpose(0, 2, 1)
    return pl.pallas_call(
        body, grid=(nq,),
        in_specs=[_row_spec(tq, Q_W), _par_spec(KV_W, t), pl.BlockSpec((nk, KV_W, tk), lambda i: (0, 0, 0))]
        + [_ANY] * ng,
        out_specs=[_row_spec(tq, Q_W), pl.BlockSpec((None, 1, r), lambda i: (i, 0, 0))] + [_ANY] * ng,
        out_shape=[sd((t, Q_W), F32), sd((nq, 1, r), F32)] + _gathered_shapes(gather),
        scratch_shapes=[pltpu.VMEM((LANES, r), BF16), pltpu.VMEM((8, r), F32), pltpu.VMEM((8, r), F32),
                        pltpu.VMEM((LANES, r), F32)] + (_exchange_sems(ng) if ng else []),
        compiler_params=_cparams(("arbitrary",) if ng else ("parallel",)),
        name="gattn_fwd_gather" if ng else "gattn_fwd",
    )(q, k, vt3, *gather)


def _gattn_bwd(q, k, v, o, do, lse, scatter=None, tq=128, tk=512):
    t = q.shape[0]
    tk = min(tk, t)
    nq, nk, r = t // tq, t // tk, 8 * tq
    items, sgrads = scatter if scatter else ((), ())
    ns = len(sgrads)
    slot_shapes = []
    for j, (o_idx, _) in enumerate(items):
        if o_idx == len(slot_shapes):
            slot_shapes += _slot_shapes([sgrads[j]])
    nslots = len(slot_shapes)

    n_in, n_scr = 7, 6
    kt3 = k.reshape(nk, tk, KV_W).transpose(0, 2, 1)

    def body(*refs):
        q_ref, k_ref, v_ref, kt_ref, o_ref, do_ref, lse_ref = refs[:n_in]
        dq_ref, dk_ref, dv_ref = refs[n_in + ns:n_in + 3 + ns]
        scr = n_in + 3 + ns + nslots
        qs, dos, qst, dost, dlt_row, dqt = refs[scr:scr + n_scr]
        if ns:
            ex = _scatter_exchange(items, refs[n_in:n_in + ns], refs[n_in + 3 + ns:scr], *refs[scr + n_scr:])
            pl.when(pl.program_id(0) == 0)(ex.start)

        @pl.when(pl.program_id(0) == 0)
        def _():
            dk_ref[...] = jnp.zeros_like(dk_ref)
            dv_ref[...] = jnp.zeros_like(dv_ref)

        _stack_heads(q_ref, qs, tq)
        _stack_heads(do_ref, dos, tq)
        lo_rows = lax.broadcasted_iota(jnp.int32, (LANES, tq), 0) < HEAD_DIM
        for i in range(4):
            lo, hi = slice(tq * i, tq * (i + 1)), slice(tq * (4 + i), tq * (5 + i))
            cols = slice(LANES * i, LANES * (i + 1))
            for src, dst in ((q_ref, qst), (do_ref, dost)):
                bt = src[:, cols].astype(F32).T
                dst[:, lo] = jnp.where(lo_rows, bt, 0.0).astype(BF16)
                dst[:, hi] = jnp.where(lo_rows, 0.0, bt).astype(BF16)
            prod_t = (do_ref[:, cols] * o_ref[:, cols]).T
            dlt_row[:, lo] = jnp.sum(prod_t[:HEAD_DIM], axis=0, keepdims=True)
            dlt_row[:, hi] = jnp.sum(prod_t[HEAD_DIM:], axis=0, keepdims=True)
        lse_row = lse_ref[...]
        dqt[...] = jnp.zeros_like(dqt)

        def step(j, carry):
            off = pl.multiple_of(j * tk, tk)
            kc = k_ref[pl.ds(off, tk), :]
            vc = v_ref[pl.ds(off, tk), :]
            p = jnp.exp(_dot(kc, qst[...], _NN) - lse_row)
            dp = _dot(vc, dost[...], _NN)
            ds = (p * (dp - dlt_row[...])).astype(BF16)
            dk_ref[pl.ds(off, tk), :] += _dot(ds, qs[...], _NN)
            dv_ref[pl.ds(off, tk), :] += _dot(p, dos[...], _NN)
            dqt[...] += _dot(kt_ref[j], ds, _NN)
            return carry

        lax.fori_loop(0, nk, step, 0, unroll=4)
        for i in range(4):
            pair_t = jnp.where(lo_rows, dqt[:, tq * i:tq * (i + 1)], dqt[:, tq * (4 + i):tq * (5 + i)])
            dq_ref[:, LANES * i:LANES * (i + 1)] = pair_t.T
        if ns:
            pl.when(pl.program_id(0) == nq - 1)(ex.wait)

    sd = jax.ShapeDtypeStruct
    return pl.pallas_call(
        body, grid=(nq,),
        in_specs=[_row_spec(tq, Q_W), _par_spec(KV_W, t), _par_spec(KV_W, t),
                  pl.BlockSpec((nk, KV_W, tk), lambda i: (0, 0, 0)), _row_spec(tq, Q_W), _row_spec(tq, Q_W),
                  pl.BlockSpec((None, 1, r), lambda i: (i, 0, 0))] + [_ANY] * ns,
        out_specs=[_row_spec(tq, Q_W), _par_spec(KV_W, t), _par_spec(KV_W, t)] + [_ANY] * nslots,
        out_shape=[sd((t, Q_W), F32), sd((t, KV_W), F32), sd((t, KV_W), F32)] + slot_shapes,
        scratch_shapes=[pltpu.VMEM((r, LANES), BF16), pltpu.VMEM((r, LANES), BF16), pltpu.VMEM((LANES, r), BF16),
                        pltpu.VMEM((LANES, r), BF16), pltpu.VMEM((1, r), F32),
                        pltpu.VMEM((LANES, r), F32)] + (_exchange_sems(ns) if ns else []),
        compiler_params=_cparams(("arbitrary",)), name="gattn_bwd_scatter" if ns else "gattn_bwd",
    )(q, k, v, kt3, o, do, lse, *sgrads)


_WQ = Q_BLOCK
_WK = 3 * Q_BLOCK
_WR = 8 * _WQ


def _pairs_transposed(src_ref, dst, tq):
    lo_rows = lax.broadcasted_iota(jnp.int32, (LANES, tq), 0) < HEAD_DIM
    for i in range(4):
        bt = src_ref[:, LANES * i:LANES * (i + 1)].astype(F32).T
        dst[:, tq * i:tq * (i + 1)] = jnp.where(lo_rows, bt, 0.0).astype(BF16)
        dst[:, tq * (4 + i):tq * (5 + i)] = jnp.where(lo_rows, 0.0, bt).astype(BF16)


def _pairs_from_transposed(halves, dst_ref, tq):
    for i in range(4):
        pair_t = jnp.concatenate([h[:, tq * i:tq * (i + 1)] for h in halves], axis=0)
        dst_ref[:, LANES * i:LANES * (i + 1)] = pair_t.T.astype(dst_ref.dtype)


def _kv_quadrants(tq):
    return [(slice(HEAD_DIM * kv, HEAD_DIM * (kv + 1)), slice(4 * tq * kv, 4 * tq * (kv + 1))) for kv in range(2)]


def _wattn_scores_t(kw, qst, bias_ref, n, t):
    kabs = (n - 1) * _WQ + lax.broadcasted_iota(jnp.int32, (_WK, 1), 0)
    st = _dot(kw, qst[...], _NN) + bias_ref[...]
    return jnp.where((kabs >= 0) & (kabs < t), st, NEG)


def _window_t(ref3, n):
    return jnp.concatenate([ref3[n], ref3[n + 1], ref3[n + 2]], axis=1)


def _blocks_transposed(ap):
    return ap.reshape(ap.shape[0] // _WQ, _WQ, KV_W).transpose(0, 2, 1)


def _wattn_fwd(q, kp, vp, bias_t, sink):
    t = q.shape[0]
    nq = t // _WQ
    tp = t + 2 * _WQ
    vpt = _blocks_transposed(vp)

    def body(q_ref, k_ref, vt_ref, b_ref, sk_ref, o_ref, lse_ref, qst):
        n = pl.program_id(0)
        _pairs_transposed(q_ref, qst, _WQ)
        kw = k_ref[pl.ds(pl.multiple_of(n * _WQ, _WQ), _WK), :]
        st = _wattn_scores_t(kw, qst, b_ref, n, t)
        sk = sk_ref[...]
        m = jnp.maximum(jnp.max(st, axis=0, keepdims=True), sk)
        pt = jnp.exp(st - m)
        l = jnp.sum(pt, axis=0, keepdims=True) + jnp.exp(sk - m)
        vwt = _window_t(vt_ref, n)
        halves = [_dot(vwt[rows, :], pt[:, cols], _NN) / l[:, cols] for rows, cols in _kv_quadrants(_WQ)]
        _pairs_from_transposed(halves, o_ref, _WQ)
        lse_ref[...] = m + jnp.log(l)

    sd = jax.ShapeDtypeStruct
    return pl.pallas_call(
        body, grid=(nq,),
        in_specs=[_row_spec(_WQ, Q_W), _par_spec(KV_W, tp), pl.BlockSpec(vpt.shape, lambda i: (0, 0, 0)),
                  _par_spec(_WR, _WK), _par_spec(_WR)],
        out_specs=[_row_spec(_WQ, Q_W), pl.BlockSpec((None, 1, _WR), lambda i: (i, 0, 0))],
        out_shape=[sd((t, Q_W), F32), sd((nq, 1, _WR), F32)],
        scratch_shapes=[pltpu.VMEM((LANES, _WR), BF16)],
        compiler_params=_cparams(("parallel",)), name="wattn_fwd",
    )(q, kp, vpt, bias_t, sink)


def _wattn_bwd(q, kp, vp, bias_t, sink, o, do, lse):
    t = q.shape[0]
    nq = t // _WQ
    tp = t + 2 * _WQ
    kpt = _blocks_transposed(kp)

    def body(q_ref, k_ref, v_ref, kt_ref, b_ref, sk_ref, o_ref, do_ref, lse_ref, dq_ref, dk_ref, dv_ref, db_ref,
             dsk_ref, qs, dos, qst, dost):
        n = pl.program_id(0)

        @pl.when(n == 0)
        def _():
            dk_ref[...] = jnp.zeros_like(dk_ref)
            dv_ref[...] = jnp.zeros_like(dv_ref)
            db_ref[...] = jnp.zeros_like(db_ref)
            dsk_ref[...] = jnp.zeros_like(dsk_ref)

        _stack_heads(q_ref, qs, _WQ)
        _stack_heads(do_ref, dos, _WQ)
        _pairs_transposed(q_ref, qst, _WQ)
        _pairs_transposed(do_ref, dost, _WQ)
        delta = []
        for i in range(4):
            cols = slice(LANES * i, LANES * (i + 1))
            prod_t = (do_ref[:, cols] * o_ref[:, cols]).T
            delta.append((jnp.sum(prod_t[:HEAD_DIM], axis=0, keepdims=True),
                          jnp.sum(prod_t[HEAD_DIM:], axis=0, keepdims=True)))
        dlt = jnp.concatenate([d[0] for d in delta] + [d[1] for d in delta], axis=1)
        off = pl.multiple_of(n * _WQ, _WQ)
        kw = k_ref[pl.ds(off, _WK), :]
        vw = v_ref[pl.ds(off, _WK), :]
        lse_v = lse_ref[...]
        pt = jnp.exp(_wattn_scores_t(kw, qst, b_ref, n, t) - lse_v)
        dpt = _dot(vw, dost[...], _NN)
        ds = pt * (dpt - dlt)
        db_ref[...] += ds
        dsk_ref[...] -= jnp.exp(sk_ref[...] - lse_v) * dlt
        dsb = ds.astype(BF16)
        dk_ref[pl.ds(off, _WK), :] += _dot(dsb, qs[...], _NN)
        dv_ref[pl.ds(off, _WK), :] += _dot(pt, dos[...], _NN)
        kwt = _window_t(kt_ref, n)
        halves = [_dot(kwt[rows, :], dsb[:, cols], _NN) for rows, cols in _kv_quadrants(_WQ)]
        _pairs_from_transposed(halves, dq_ref, _WQ)

    sd = jax.ShapeDtypeStruct
    qb = _row_spec(_WQ, Q_W)
    return pl.pallas_call(
        body, grid=(nq,),
        in_specs=[qb, _par_spec(KV_W, tp), _par_spec(KV_W, tp), pl.BlockSpec(kpt.shape, lambda i: (0, 0, 0)),
                  _par_spec(_WR, _WK), _par_spec(_WR), qb, qb, pl.BlockSpec((None, 1, _WR), lambda i: (i, 0, 0))],
        out_specs=[qb, _par_spec(KV_W, tp), _par_spec(KV_W, tp), _par_spec(_WR, _WK), _par_spec(_WR)],
        out_shape=[sd((t, Q_W), F32), sd((tp, KV_W), F32), sd((tp, KV_W), F32), sd((_WK, _WR), F32), sd((1, _WR), F32)],
        scratch_shapes=[pltpu.VMEM((_WR, LANES), BF16), pltpu.VMEM((_WR, LANES), BF16), pltpu.VMEM((LANES, _WR), BF16),
                        pltpu.VMEM((LANES, _WR), BF16)],
        compiler_params=_cparams(("arbitrary",)), name="wattn_bwd",
    )(q, kp, vp, kpt, bias_t, sink, o, do, lse)


def _bias_bucket_reduce(db0, db1, bucket):
    def body(a_ref, b_ref, bk_ref, o_ref):
        d = a_ref[...] + b_ref[...]
        bk = bk_ref[...]
        lane = lax.broadcasted_iota(jnp.int32, (1, LANES), 1)
        out = jnp.zeros((1, LANES), F32)
        for b in range(N_BUCKETS):
            tot = jnp.sum(jnp.sum(jnp.where(bk == b, d, 0.0), axis=-1, keepdims=True), axis=0, keepdims=True)
            out = out + jnp.where(lane == b, tot, 0.0)
        o_ref[...] = out

    hb = pl.BlockSpec((None, _WQ, _WK), lambda h: (h, 0, 0))
    return pl.pallas_call(
        body, grid=(8,), in_specs=[hb, hb, pl.BlockSpec((_WQ, _WK), lambda h: (0, 0))],
        out_specs=pl.BlockSpec((None, 1, LANES), lambda h: (h, 0, 0)),
        out_shape=jax.ShapeDtypeStruct((8, 1, LANES), F32),
        compiler_params=_cparams(("parallel",)), name="bias_bucket_reduce",
    )(db0.reshape(8, _WQ, _WK), db1.reshape(8, _WQ, _WK), bucket)


def _rope_tables(t):
    rows_n = t // GRID_W
    row = jnp.repeat(jnp.arange(rows_n, dtype=F32), GRID_W)
    col = jnp.tile(jnp.arange(GRID_W, dtype=F32), rows_n)
    half = HEAD_DIM // 2
    inv_freq = ROPE_THETA ** (-jnp.arange(0, half, 2, dtype=F32) / half)
    ang = jnp.concatenate([row[:, None] * inv_freq, col[:, None] * inv_freq], axis=-1)
    cos, sin = jnp.cos(ang), jnp.sin(ang)
    c64 = jnp.repeat(cos, 2, axis=-1)
    s64 = jnp.stack([-sin, sin], axis=-1).reshape(t, HEAD_DIM)
    return jnp.tile(c64, (1, 2)), jnp.tile(s64, (1, 2))


def _t5_bucket(rel):
    half = N_BUCKETS // 2
    max_exact = half // 2
    bucket = jnp.where(rel > 0, half, 0)
    rp = jnp.abs(rel)
    rpf = jnp.maximum(rp, 1).astype(jnp.float32)
    large = max_exact + (jnp.log(rpf / max_exact) / math.log(MAX_DISTANCE / max_exact)
                         * (half - max_exact)).astype(jnp.int32)
    large = jnp.minimum(large, half - 1)
    return bucket + jnp.where(rp < max_exact, rp, large)


def _window_tables(rel_bias):
    qpos = jnp.arange(_WQ, dtype=jnp.int32)
    kpos = jnp.arange(_WK, dtype=jnp.int32) - _WQ
    rel = kpos[None, :] - qpos[:, None]
    bucket = _t5_bucket(rel)
    bias = jnp.zeros((8, _WQ, _WK), F32)
    for b in range(N_BUCKETS):
        bias = jnp.where((bucket == b)[None], rel_bias[b][:, None, None], bias)
    bias = jnp.where((jnp.abs(rel) <= WINDOW)[None], bias, NEG)
    return bias.reshape(_WR, _WK).T, bucket


def _pad_rows(a):
    return jnp.pad(a, ((_WQ, _WQ), (0, 0)))


def _layer_fwd(x, p, tabs, gather=None):
    cos_t, sin_t, bias = tabs
    h = _mm_nn(x, p["win"], F32, "in_proj")
    qa, ka, va, qb, kb, vb = _prep_fwd(h, cos_t, sin_t, p["qn"], p["kn"])
    if gather is None:
        oa, lse_a = _gattn_fwd(qa, ka, va)
    else:
        oa, lse_a, *gathered = _gattn_fwd(qa, ka, va, gather=gather[0])
        p = gather[1](gathered)
    kbp, vbp = _pad_rows(kb), _pad_rows(vb)
    ob, lse_b = _wattn_fwd(qb, kbp, vbp, bias, p["sink"])
    y = _outnorm_fwd(oa, ob, p["ga"], p["gb"])
    z1, x1, x1b = _mm_reduce(y[None], p["wout"][None], _NN, F32, "out_proj", res=x, res_scale=ALPHA,
                             ln=(p["ln1g"], p["ln1b"]))
    gu = _mm_expand(x1b, p["wgu"], _NN, BF16, "gate_up_proj")
    hdn = _ffn_mid_fwd(gu, p["cw"])
    z2, x2, _ = _mm_reduce(hdn, p["wd"], _NN, F32, "down_proj", res=x1, res_scale=ALPHA, ln=(p["ln2g"], p["ln2b"]))
    saved = dict(x=x, h=h, qa=qa, ka=ka, va=va, qb=qb, kbp=kbp, vbp=vbp, oa=oa, ob=ob, lse_a=lse_a, lse_b=lse_b,
                 y=y, z1=z1, x1b=x1b, gu=gu, hdn=hdn, z2=z2)
    return x2, saved


def _block_grads(g, names=("w_in", "w_out", "w_gate", "w_up", "w_down")):
    make = dict(
        w_in=lambda: _col_blocks(_in_cols_to_pairs(g["win"], _from_pairs), IN_SH),
        w_out=lambda: _mix_rows_to_pairs(g["wout"], _from_pairs).reshape(N_SHARD, OUT_SH, D_MODEL),
        w_gate=lambda: g["wg"], w_up=lambda: g["wu"], w_down=lambda: g["wd"])
    return [make[n]() for n in names]


def _layer_bwd(dx2, p, s, tabs, layer=0, pending=None):
    cos_t, sin_t, bias = tabs
    t = dx2.shape[0]
    dz2, dz2b, dln2g, dln2b = _ln_bwd(dx2, s["z2"], p["ln2g"])
    dhdn = _mm_expand(dz2b, p["wd"], _NT, BF16, "down_dx")
    dwd = _mm_tn_blocks(s["hdn"], dz2b, "down_dw")
    dgu, stats = _ffn_mid_bwd(s["gu"], dhdn, p["cw"])
    dgu = dgu.reshape(2 * N_SHARD, t, FF_SH)
    dx1 = _mm_reduce(dgu, p["wgu"], _NT, F32, "gate_up_dx", res=dz2, res_scale=ALPHA)
    dwg = _mm_tn_blocks(dgu, s["x1b"], "gate_dw", blk=0)
    dwu = _mm_tn_blocks(dgu, s["x1b"], "up_dw", blk=1)
    dz1, dz1b, dln1g, dln1b = _ln_bwd(dx1, s["z1"], p["ln1g"])
    dy = _mm_nt(dz1b, p["wout"], F32, "out_dx")
    dwout = _mm_tn(s["y"], dz1b, "out_dw")
    doa, dob, dga, dgb = _outnorm_bwd(dy, s["oa"], s["ob"], p["ga"], p["gb"])
    slots = None
    if pending is None:
        dqa, dka, dva = _gattn_bwd(s["qa"], s["ka"], s["va"], s["oa"], doa, s["lse_a"])
    else:
        mine = _block_grads(dict(wout=dwout, wg=dwg, wu=dwu, wd=dwd), ("w_out", "w_gate", "w_up", "w_down"))
        todo = list(pending) + [(o + 1, layer, g) for o, g in enumerate(mine)]
        dqa, dka, dva, *slots = _gattn_bwd(s["qa"], s["ka"], s["va"], s["oa"], doa, s["lse_a"],
                                           scatter=([(o, l) for o, l, _ in todo], [g for _, _, g in todo]))
    dqb, dkbp, dvbp, dbias, dsink = _wattn_bwd(s["qb"], s["kbp"], s["vbp"], bias, p["sink"], s["ob"], dob, s["lse_b"])
    dkb = lax.slice_in_dim(dkbp, _WQ, _WQ + t, axis=0)
    dvb = lax.slice_in_dim(dvbp, _WQ, _WQ + t, axis=0)
    dh, dqn, dkn = _prep_bwd(s["h"], cos_t, sin_t, p["qn"], p["kn"], dqa, dka, dva, dqb, dkb, dvb)
    dx = _mm_nt(dh, p["win"], F32, "in_dx", res=dz1, res_scale=ALPHA)
    dwin = _mm_tn(s["x"], dh, "in_dw")
    grads = dict(win=dwin, wout=dwout, wg=dwg, wu=dwu, wd=dwd, stats=stats, qn=dqn, kn=dkn, ga=dga, gb=dgb,
                 ln1g=dln1g, ln1b=dln1b, ln2g=dln2g, ln2b=dln2b, bias=dbias, sink=dsink, slots=slots)
    return dx, grads


def _prep_layer_params(l, win, wout, wg, wu, wd, cw, q_norm, k_norm, sink, out_norm_a, out_norm_b, conv_b,
                       ln1_g, ln1_b, ln2_g, ln2_b):
    win_full = win.transpose(1, 0, 2).reshape(D_MODEL, IN_COLS)
    row = lambda v: v.reshape(1, -1)
    late = {}
    if wout is not None:
        late = dict(
            wout=_mix_rows_to_pairs(wout.reshape(D_MODEL, D_MODEL)), wgu=jnp.concatenate([wg, wu], axis=0), wd=wd,
            cw=jnp.pad(cw, ((0, 0), (0, 5), (0, 0)))
            + jnp.pad(conv_b[l].reshape(N_SHARD, 1, FF_SH), ((0, 0), (3, 4), (0, 0))))
    return dict(
        late, win=_in_cols_to_pairs(win_full),
        qn=row(jnp.tile(q_norm[l], 2)), kn=row(jnp.tile(k_norm[l], 2)),
        ga=row(_to_pairs(out_norm_a[l], 0)), gb=row(_to_pairs(out_norm_b[l], 0)),
        ln1g=row(ln1_g[l]), ln1b=row(ln1_b[l]), ln2g=row(ln2_g[l]), ln2b=row(ln2_b[l]),
        sink=jnp.repeat(sink[l], _WQ).reshape(1, _WR))


def _local_step(x, tgt, params, rel_bias, gather=None, scatter=False):
    t = x.shape[0]
    cos_t, sin_t = _rope_tables(t)
    bias, bucket = _window_tables(rel_bias)
    tabs = (cos_t, sin_t, bias)
    saved = []
    for l in range(DEPTH):
        x, s = _layer_fwd(x, params[l], tabs, gather if l == 0 else None)
        saved.append(s)
    dx, loss = _loss_grad(x, tgt)
    grads = [None] * DEPTH
    for l in reversed(range(DEPTH)):
        pending = None
        if scatter and l == 0:
            pending = [(o, 1, g) for o, g in enumerate(_block_grads(grads[1]))]
        dx, grads[l] = _layer_bwd(dx, params[l], saved[l], tabs, l, pending)
    dbucket = _bias_bucket_reduce(grads[0]["bias"].T, grads[1]["bias"].T, bucket)
    return loss, dx, grads, dbucket


_ANY = pl.BlockSpec(memory_space=pl.ANY)
_MESH = pl.DeviceIdType.MESH


def _mesh_pos():
    return lax.axis_index("x"), lax.axis_index("y"), lax.axis_index("c")


def _other_chips(x, y):
    return [(1 - x, y), (x, 1 - y), (1 - x, 1 - y)]


class _Exchange:
    def __init__(self, local, sends, recvs):
        self.local, self.sends, self.recvs = local, sends, recvs

    def start(self):
        for cp in self.local + self.sends:
            cp.start()

    def wait(self):
        for cp in self.recvs:
            cp.wait_recv()
        for cp in self.sends:
            cp.wait_send()
        for cp in self.local:
            cp.wait()


def _exchange_sems(n):
    return [pltpu.SemaphoreType.DMA((n, 3)), pltpu.SemaphoreType.DMA((n, 3)), pltpu.SemaphoreType.DMA((n,))]


def _gather_exchange(ins, outs, send, recv, loc):
    x, y, c = _mesh_pos()
    me = 2 * x + y
    chips = _other_chips(x, y)

    def remote(i, k, block):
        px, py = chips[k]
        return pltpu.make_async_remote_copy(ins[i], outs[i].at[block], send.at[i, k], recv.at[i, k],
                                            device_id=(px, py, c), device_id_type=_MESH)

    n = len(ins)
    local = [pltpu.make_async_copy(ins[i], outs[i].at[me], loc.at[i]) for i in range(n)]
    sends = [remote(i, k, me) for i in range(n) for k in range(3)]
    recvs = [remote(i, k, 2 * chips[k][0] + chips[k][1]) for i in range(n) for k in range(3)]
    return _Exchange(local, sends, recvs)


def _scatter_exchange(items, ins, outs, send, recv, loc):
    x, y, c = _mesh_pos()
    me = 2 * x + y
    chips = _other_chips(x, y)

    def remote(j, k):
        o, l = items[j]
        px, py = chips[k]
        return pltpu.make_async_remote_copy(ins[j].at[2 * px + py], outs[o].at[k, l], send.at[j, k], recv.at[j, k],
                                            device_id=(px, py, c), device_id_type=_MESH)

    local = [pltpu.make_async_copy(ins[j].at[me], outs[o].at[3, l], loc.at[j]) for j, (o, l) in enumerate(items)]
    sends = [remote(j, k) for j in range(len(items)) for k in range(3)]
    return _Exchange(local, sends, sends)


def _gathered_shapes(shards):
    return [jax.ShapeDtypeStruct((N_SHARD,) + s.shape, s.dtype) for s in shards]


def _slot_shapes(blocks):
    return [jax.ShapeDtypeStruct((N_SHARD, DEPTH) + g.shape[1:], g.dtype) for g in blocks]


def _gather_shards(shards):
    n = len(shards)

    def body(*refs):
        ex = _gather_exchange(refs[:n], refs[n:2 * n], *refs[2 * n:])
        ex.start()
        ex.wait()

    return pl.pallas_call(
        body, in_specs=[_ANY] * n, out_specs=[_ANY] * n, out_shape=_gathered_shapes(shards),
        scratch_shapes=_exchange_sems(n), name="gather_weights",
    )(*shards)


def _scatter_into(items, grads, slots):
    n, ns = len(grads), len(slots)

    def body(*refs):
        ex = _scatter_exchange(items, refs[:n], refs[n + ns:n + 2 * ns], *refs[n + 2 * ns:])
        ex.start()
        ex.wait()

    return pl.pallas_call(
        body, in_specs=[_ANY] * (n + ns), out_specs=[_ANY] * ns,
        out_shape=[jax.ShapeDtypeStruct(s.shape, s.dtype) for s in slots],
        input_output_aliases={n + i: i for i in range(ns)},
        scratch_shapes=_exchange_sems(n), name="scatter_grads",
    )(*grads, *slots)


def _swap_with_sibling(parts):
    n = len(parts)

    def body(*refs):
        ins, outs = refs[:n], refs[n:2 * n]
        send, recv = refs[2 * n:]
        x, y, c = _mesh_pos()
        copies = [pltpu.make_async_remote_copy(ins[i], outs[i], send.at[i], recv.at[i], device_id=(x, y, 1 - c),
                                               device_id_type=_MESH) for i in range(n)]
        for cp in copies:
            cp.start()
        for cp in copies:
            cp.wait_recv()
        for cp in copies:
            cp.wait_send()

    return pl.pallas_call(
        body, in_specs=[_ANY] * n, out_specs=[_ANY] * n,
        out_shape=[jax.ShapeDtypeStruct(p.shape, p.dtype) for p in parts],
        scratch_shapes=[pltpu.SemaphoreType.DMA((n,)), pltpu.SemaphoreType.DMA((n,))],
        name="swap_sibling",
    )(*parts)


N_DEV = 8


def _allreduce_small(packed):
    rows = packed.shape[0]

    def body(in_ref, out_ref, buf, send, recv, loc):
        x, y, c = _mesh_pos()
        me = 4 * x + 2 * y + c
        own = pltpu.make_async_copy(in_ref, buf.at[me], loc)
        own.start()

        def remote(m, block):
            peer = (x ^ (m >> 2), y ^ ((m >> 1) & 1), c ^ (m & 1))
            return pltpu.make_async_remote_copy(in_ref, buf.at[block], send.at[m - 1], recv.at[m - 1],
                                                device_id=peer, device_id_type=_MESH)

        sends = [remote(m, me) for m in range(1, N_DEV)]
        for cp in sends:
            cp.start()
        for m in range(1, N_DEV):
            remote(m, me ^ m).wait_recv()
        for cp in sends:
            cp.wait_send()
        own.wait()
        tot = buf[0]
        for d in range(1, N_DEV):
            tot = tot + buf[d]
        out_ref[...] = tot

    vm = pl.BlockSpec(memory_space=pltpu.VMEM)
    return pl.pallas_call(
        body, in_specs=[vm], out_specs=vm, out_shape=jax.ShapeDtypeStruct((rows, LANES), F32),
        scratch_shapes=[pltpu.VMEM((N_DEV, rows, LANES), F32), pltpu.SemaphoreType.DMA((N_DEV - 1,)),
                        pltpu.SemaphoreType.DMA((N_DEV - 1,)), pltpu.SemaphoreType.DMA(())],
        name="allreduce_small",
    )(packed)


def _shard_rows(r):
    return r // 2 if r % 32 == 0 else r


def _sum_slots(slots):
    _, _, r, cdim = slots.shape
    tr = _shard_rows(r)

    def body(a_ref, b_ref, c_ref, d_ref, o_ref):
        up = lambda ref: ref[...].astype(F32)
        o_ref[...] = ((up(d_ref) + up(a_ref)) + up(b_ref)) + up(c_ref)

    def spec(k):
        return pl.BlockSpec((None, None, tr, cdim), lambda l, i: (k, l, i, 0))

    return pl.pallas_call(
        body, grid=(DEPTH, r // tr), in_specs=[spec(0), spec(1), spec(2), spec(3)],
        out_specs=pl.BlockSpec((None, tr, cdim), lambda l, i: (l, i, 0)),
        out_shape=jax.ShapeDtypeStruct((DEPTH, r, cdim), F32),
        compiler_params=_cparams(("parallel", "parallel")), name="sum_slots",
    )(slots, slots, slots, slots)


def _adamw_math(w, g, m, v):
    m = ADAM_B1 * m + (1.0 - ADAM_B1) * g
    v = ADAM_B2 * v + (1.0 - ADAM_B2) * (g * g)
    m_hat = m / (1.0 - ADAM_B1 ** ADAM_STEP)
    v_hat = v / (1.0 - ADAM_B2 ** ADAM_STEP)
    delta = -ADAM_LR * (m_hat / (jnp.sqrt(v_hat) + ADAM_EPS) + ADAM_WD * w)
    return delta, m, v


def _adamw_big(ga, gb, w, m, v):
    _, r, cdim = w.shape
    tr = _shard_rows(r)

    def body(ga_ref, gb_ref, w_ref, m_ref, v_ref, g_out, d_out, m_out, v_out):
        g = ga_ref[...] + gb_ref[...]
        d, mn, vn = _adamw_math(w_ref[...], g, m_ref[...], v_ref[...])
        g_out[...] = g
        d_out[...] = d
        m_out[...] = mn
        v_out[...] = vn

    spec = pl.BlockSpec((None, tr, cdim), lambda l, i: (l, i, 0))
    shp = jax.ShapeDtypeStruct(w.shape, F32)
    return pl.pallas_call(
        body, grid=(DEPTH, r // tr), in_specs=[spec] * 5, out_specs=[spec] * 4, out_shape=[shp] * 4,
        compiler_params=_cparams(("parallel", "parallel")), name="adamw_big",
    )(ga, gb, w, m, v)


def _adamw_small(ws, gs, ms, vs):
    n = len(ws)

    def body(*refs):
        w_r, g_r, m_r, v_r = (refs[k * n:(k + 1) * n] for k in range(4))
        d_o, m_o, v_o = (refs[(4 + k) * n:(5 + k) * n] for k in range(3))
        for i in range(n):
            d, mn, vn = _adamw_math(w_r[i][...], g_r[i][...], m_r[i][...], v_r[i][...])
            d_o[i][...] = d
            m_o[i][...] = mn
            v_o[i][...] = vn

    vm = pl.BlockSpec(memory_space=pltpu.VMEM)
    shp = [jax.ShapeDtypeStruct(w.shape, F32) for w in ws]
    outs = pl.pallas_call(
        body, in_specs=[vm] * (4 * n), out_specs=[vm] * (3 * n), out_shape=shp * 3, name="adamw_small",
    )(*ws, *gs, *ms, *vs)
    return outs[:n], outs[n:2 * n], outs[2 * n:]


def _tile_rows(a):
    a = a.reshape(-1, LANES)
    pad = (-a.shape[0]) % 8
    return jnp.pad(a, ((0, pad), (0, 0))) if pad else a


_SMALL_LAYER_PARTS = (("qn", 8), ("kn", 8), ("sink", 8), ("ga", 8), ("gb", 8), ("ln1g", 8), ("ln1b", 8),
                      ("ln2g", 8), ("ln2b", 8), ("stats", N_SHARD * 8 * FF_SH // LANES))
_SMALL_HEAD_ROWS = 16
_SMALL_LAYER_ROWS = sum(r for _, r in _SMALL_LAYER_PARTS)


def _pack_small(loss, dbucket, grads):
    parts = [_tile_rows(loss), _tile_rows(dbucket)]
    for l in range(DEPTH):
        parts += [_tile_rows(grads[l][name]) for name, _ in _SMALL_LAYER_PARTS]
    return jnp.concatenate(parts, axis=0)


def _unpack_small(tot, chip):
    out = dict(loss=tot[0, 0], rel_bias=tot[8:16, :N_BUCKETS].T)
    per = {name: [] for name, _ in _SMALL_LAYER_PARTS}
    for l in range(DEPTH):
        base = _SMALL_HEAD_ROWS + l * _SMALL_LAYER_ROWS
        for name, rows in _SMALL_LAYER_PARTS:
            per[name].append(tot[base:base + rows])
            base += rows
    fold = lambda v: v[0, :HEAD_DIM] + v[0, HEAD_DIM:]
    out["q_norm"] = jnp.stack([fold(v) for v in per["qn"]])
    out["k_norm"] = jnp.stack([fold(v) for v in per["kn"]])
    out["sink"] = jnp.stack([jnp.sum(v, axis=1) for v in per["sink"]])
    out["out_norm_a"] = jnp.stack([_from_pairs(v[:4].reshape(Q_W), 0) for v in per["ga"]])
    out["out_norm_b"] = jnp.stack([_from_pairs(v[:4].reshape(Q_W), 0) for v in per["gb"]])
    for name, key in (("ln1_g", "ln1g"), ("ln1_b", "ln1b"), ("ln2_g", "ln2g"), ("ln2_b", "ln2b")):
        out[name] = jnp.stack([v.reshape(D_MODEL) for v in per[key]])
    stats = [v.reshape(N_SHARD, 8, FF_SH) for v in per["stats"]]
    out["conv_b"] = jnp.stack([s[:, 0, :].reshape(D_FF) for s in stats])
    out["conv_w"] = jnp.stack([lax.dynamic_index_in_dim(s, chip, 0, keepdims=False)[1:4] for s in stats])
    return out


_WEIGHTS = ("rel_bias", "w_in", "q_norm", "k_norm", "sink", "out_norm_a", "out_norm_b", "w_out", "ln1_g", "ln1_b",
            "w_gate", "w_up", "conv_w", "conv_b", "w_down", "ln2_g", "ln2_b")
_BIG = ("w_in", "w_out", "w_gate", "w_up", "w_down")
_SMALL = tuple(n for n in _WEIGHTS if n not in _BIG)


def _col_blocks(g, n):
    return g.reshape(g.shape[0], N_SHARD, n).transpose(1, 0, 2)


def kernel(x, rel_bias, w_in, q_norm, k_norm, sink, out_norm_a, out_norm_b, w_out, ln1_g, ln1_b, w_gate, w_up, conv_w, conv_b, w_down, ln2_g, ln2_b, loss_target, m_rel_bias, m_w_in, m_q_norm, m_k_norm, m_sink, m_out_norm_a, m_out_norm_b, m_w_out, m_ln1_g, m_ln1_b, m_w_gate, m_w_up, m_conv_w, m_conv_b, m_w_down, m_ln2_g, m_ln2_b, v_rel_bias, v_w_in, v_q_norm, v_k_norm, v_sink, v_out_norm_a, v_out_norm_b, v_w_out, v_ln1_g, v_ln1_b, v_w_gate, v_w_up, v_conv_w, v_conv_b, v_w_down, v_ln2_g, v_ln2_b):
    w = dict(rel_bias=rel_bias, w_in=w_in, q_norm=q_norm, k_norm=k_norm, sink=sink, out_norm_a=out_norm_a,
             out_norm_b=out_norm_b, w_out=w_out, ln1_g=ln1_g, ln1_b=ln1_b, w_gate=w_gate, w_up=w_up, conv_w=conv_w,
             conv_b=conv_b, w_down=w_down, ln2_g=ln2_g, ln2_b=ln2_b)
    m = dict(rel_bias=m_rel_bias, w_in=m_w_in, q_norm=m_q_norm, k_norm=m_k_norm, sink=m_sink, out_norm_a=m_out_norm_a,
             out_norm_b=m_out_norm_b, w_out=m_w_out, ln1_g=m_ln1_g, ln1_b=m_ln1_b, w_gate=m_w_gate, w_up=m_w_up,
             conv_w=m_conv_w, conv_b=m_conv_b, w_down=m_w_down, ln2_g=m_ln2_g, ln2_b=m_ln2_b)
    v = dict(rel_bias=v_rel_bias, w_in=v_w_in, q_norm=v_q_norm, k_norm=v_k_norm, sink=v_sink, out_norm_a=v_out_norm_a,
             out_norm_b=v_out_norm_b, w_out=v_w_out, ln1_g=v_ln1_g, ln1_b=v_ln1_b, w_gate=v_w_gate, w_up=v_w_up,
             conv_w=v_conv_w, conv_b=v_conv_b, w_down=v_w_down, ln2_g=v_ln2_g, ln2_b=v_ln2_b)
    chip = 2 * lax.axis_index("x") + lax.axis_index("y")

    small_w = (q_norm, k_norm, sink, out_norm_a, out_norm_b, conv_b, ln1_g, ln1_b, ln2_g, ln2_b)
    (win0,) = _gather_shards([w_in[0].astype(BF16)])
    later = ([w[name][0].astype(BF16) for name in _BIG[1:]] + [w[name][1].astype(BF16) for name in _BIG] + [conv_w])
    params = [_prep_layer_params(0, win0, None, None, None, None, None, *small_w), None]

    def finish(g):
        wout0, wg0, wu0, wd0, win1, wout1, wg1, wu1, wd1, cw_all = g
        params[0] = _prep_layer_params(0, win0, wout0, wg0, wu0, wd0, cw_all[:, 0], *small_w)
        params[1] = _prep_layer_params(1, win1, wout1, wg1, wu1, wd1, cw_all[:, 1], *small_w)
        return params[0]

    loss, dx, grads, dbucket = _local_step(x[0], loss_target[0], params, rel_bias, gather=(later, finish),
                                           scatter=True)

    small = _unpack_small(_allreduce_small(_pack_small(loss, dbucket, grads)), chip)

    slots = list(grads[0]["slots"])
    slots[0] = _scatter_into([(0, 0)], _block_grads(grads[0], ("w_in",)), [slots[0]])[0]
    partial = [_sum_slots(s) for s in slots]
    other = _swap_with_sibling(partial)

    grad, delta, new_m, new_v = {}, {}, {}, {}
    for i, name in enumerate(_BIG):
        fix = (lambda a: jnp.swapaxes(a, 1, 2)) if name in ("w_gate", "w_up") else (lambda a: a)
        outs = _adamw_big(partial[i], other[i], fix(w[name]), fix(m[name]), fix(v[name]))
        grad[name], delta[name], new_m[name], new_v[name] = [fix(o) for o in outs]
    flat2 = lambda a: a.reshape(-1, a.shape[-1])
    ds, ms, vs = _adamw_small([flat2(w[n]) for n in _SMALL], [flat2(small[n]) for n in _SMALL],
                              [flat2(m[n]) for n in _SMALL], [flat2(v[n]) for n in _SMALL])
    for i, name in enumerate(_SMALL):
        grad[name] = small[name]
        delta[name] = ds[i].reshape(w[name].shape)
        new_m[name] = ms[i].reshape(w[name].shape)
        new_v[name] = vs[i].reshape(w[name].shape)

    return (small["loss"], dx[None], *[grad[n] for n in _WEIGHTS], *[delta[n] for n in _WEIGHTS],
            *[new_m[n] for n in _WEIGHTS], *[new_v[n] for n in _WEIGHTS])
```

```python
import math

import jax
import jax.numpy as jnp
from jax import lax
from jax.experimental import pallas as pl
from jax.experimental.pallas import tpu as pltpu

F32 = jnp.float32
BF16 = jnp.bfloat16

D_MODEL = 1024
DEPTH = 2
HEAD_DIM = 64
Q_W = 512
KV_W = 128
IN_COLS = 2 * (Q_W + 2 * KV_W)
N_SHARD = 4
IN_SH = IN_COLS // N_SHARD
OUT_SH = D_MODEL // N_SHARD
D_FF = 2816
FF_SH = D_FF // N_SHARD
Q_BLOCK = 128
WINDOW = 128
N_BUCKETS = 32
MAX_DISTANCE = 128
GRID_W = 64
ROPE_THETA = 10000.0
ALPHA = (2.0 * DEPTH) ** 0.25
RMS_EPS = 1e-6
LN_EPS = 1e-5
NEG = -1e30
LANES = 128
VMEM_LIMIT = 56 * 1024 * 1024

ADAM_LR = 0.001
ADAM_B1 = 0.9
ADAM_B2 = 0.999
ADAM_EPS = 1e-08
ADAM_WD = 0.01
ADAM_STEP = 10

_NN = (((1,), (0,)), ((), ()))
_NT = (((1,), (1,)), ((), ()))
_TN = (((0,), (0,)), ((), ()))


def _dot(a, b, dims):
    return lax.dot_general(a.astype(BF16), b.astype(BF16), dims, preferred_element_type=F32)


def _cparams(sem, vmem=VMEM_LIMIT):
    return pltpu.CompilerParams(dimension_semantics=sem, vmem_limit_bytes=vmem)


def _regroup(a, axis, n_outer, n_inner):
    shp = a.shape
    a = a.reshape(shp[:axis] + (n_outer, n_inner, HEAD_DIM) + shp[axis + 1:])
    return jnp.swapaxes(a, axis, axis + 1).reshape(shp)


def _to_pairs(a, axis):
    return _regroup(a, axis, 2, 4)


def _from_pairs(a, axis):
    return _regroup(a, axis, 4, 2)


def _in_cols_to_pairs(w, fn=_to_pairs):
    return jnp.concatenate([fn(w[..., :Q_W], w.ndim - 1), w[..., Q_W:Q_W + 2 * KV_W],
                            fn(w[..., Q_W + 2 * KV_W:2 * Q_W + 2 * KV_W], w.ndim - 1),
                            w[..., 2 * Q_W + 2 * KV_W:]], axis=-1)


def _mix_rows_to_pairs(w, fn=_to_pairs):
    return fn(w.reshape(2, Q_W, w.shape[-1]), 1).reshape(w.shape)


def _matmul(a, b, *, dims, grid, a_spec, b_spec, o_spec, out_shape, acc_shape, name, res=None,
            res_spec=None, res_scale=1.0):
    nk = grid[-1]
    kax = len(grid) - 1

    def body(*refs):
        if res is None:
            a_ref, b_ref, o_ref, acc = refs
            r_ref = None
        else:
            a_ref, b_ref, r_ref, o_ref, acc = refs
        k = pl.program_id(kax)

        @pl.when(k == 0)
        def _():
            acc[...] = jnp.zeros_like(acc)

        acc[...] += _dot(a_ref[...], b_ref[...], dims)

        @pl.when(k == nk - 1)
        def _():
            o = acc[...]
            if r_ref is not None:
                o = o + res_scale * r_ref[...]
            o_ref[...] = o.astype(o_ref.dtype)

    in_specs = [a_spec, b_spec] + ([res_spec] if res is not None else [])
    args = (a, b) + ((res,) if res is not None else ())
    sem = ("parallel",) * kax + ("arbitrary",)
    return pl.pallas_call(
        body, grid=grid, in_specs=in_specs, out_specs=o_spec, out_shape=out_shape,
        scratch_shapes=[pltpu.VMEM(acc_shape, F32)], compiler_params=_cparams(sem), name=name,
    )(*args)


def _mm_nn(a, b, out_dtype, name, tm=512, res=None, res_scale=1.0):
    m, kd = a.shape
    n = b.shape[1]
    return _matmul(
        a, b, dims=_NN, grid=(m // tm, 1),
        a_spec=pl.BlockSpec((tm, kd), lambda i, k: (i, 0)),
        b_spec=pl.BlockSpec((kd, n), lambda i, k: (0, 0)),
        o_spec=pl.BlockSpec((tm, n), lambda i, k: (i, 0)),
        out_shape=jax.ShapeDtypeStruct((m, n), out_dtype), acc_shape=(tm, n), name=name,
        res=res, res_spec=pl.BlockSpec((tm, n), lambda i, k: (i, 0)), res_scale=res_scale)


def _mm_nt(a, b, out_dtype, name, tm=512, res=None, res_scale=1.0):
    m, kd = a.shape
    n = b.shape[0]
    return _matmul(
        a, b, dims=_NT, grid=(m // tm, 1),
        a_spec=pl.BlockSpec((tm, kd), lambda i, k: (i, 0)),
        b_spec=pl.BlockSpec((n, kd), lambda i, k: (0, 0)),
        o_spec=pl.BlockSpec((tm, n), lambda i, k: (i, 0)),
        out_shape=jax.ShapeDtypeStruct((m, n), out_dtype), acc_shape=(tm, n), name=name,
        res=res, res_spec=pl.BlockSpec((tm, n), lambda i, k: (i, 0)), res_scale=res_scale)


def _mm_tn(a, b, name, tk=1024, tn=None, out_dtype=BF16):
    t, m = a.shape
    n = b.shape[1]
    tn = n if tn is None else tn
    tk = min(tk, t)
    return _matmul(
        a, b, dims=_TN, grid=(n // tn, t // tk),
        a_spec=pl.BlockSpec((tk, m), lambda j, k: (k, 0)),
        b_spec=pl.BlockSpec((tk, tn), lambda j, k: (k, j)),
        o_spec=pl.BlockSpec((m, tn), lambda j, k: (0, j)),
        out_shape=jax.ShapeDtypeStruct((m, n), out_dtype), acc_shape=(m, tn), name=name)


def _blocked_n(w, dims):
    return w.shape[2] if dims == _NN else w.shape[1]


def _mm_expand(a, w, dims, out_dtype, name, tm=512):
    m, kd = a.shape
    nb, n = w.shape[0], _blocked_n(w, dims)

    def body(a_ref, w_ref, o_ref):
        av = a_ref[...]
        for j in range(nb):
            o_ref[j] = _dot(av, w_ref[j], dims).astype(o_ref.dtype)

    return pl.pallas_call(
        body, grid=(m // tm,),
        in_specs=[pl.BlockSpec((tm, kd), lambda i: (i, 0)), pl.BlockSpec(w.shape, lambda i: (0, 0, 0))],
        out_specs=pl.BlockSpec((nb, tm, n), lambda i: (0, i, 0)),
        out_shape=jax.ShapeDtypeStruct((nb, m, n), out_dtype),
        compiler_params=_cparams(("parallel",)), name=name,
    )(a, w)


def _mm_reduce(a, w, dims, out_dtype, name, tm=512, res=None, res_scale=1.0, ln=None):
    nb, m, kd = a.shape
    n = _blocked_n(w, dims)
    n_in = 2 + (res is not None) + (2 if ln else 0)

    def body(*refs):
        a_ref, w_ref = refs[0], refs[1]
        acc = _dot(a_ref[0], w_ref[0], dims)
        for j in range(1, nb):
            acc = acc + _dot(a_ref[j], w_ref[j], dims)
        if res is not None:
            acc = acc + res_scale * refs[2][...]
        refs[n_in][...] = acc.astype(out_dtype)
        if ln:
            g_ref, b_ref = refs[n_in - 2], refs[n_in - 1]
            zc = acc - jnp.mean(acc, axis=-1, keepdims=True)
            r = lax.rsqrt(jnp.mean(zc * zc, axis=-1, keepdims=True) + LN_EPS)
            y = zc * r * g_ref[...] + b_ref[...]
            refs[n_in + 1][...] = y
            refs[n_in + 2][...] = y.astype(BF16)

    row = pl.BlockSpec((tm, n), lambda i: (i, 0))
    par = pl.BlockSpec((1, n), lambda i: (0, 0))
    sd = jax.ShapeDtypeStruct
    out = pl.pallas_call(
        body, grid=(m // tm,),
        in_specs=[pl.BlockSpec((nb, tm, kd), lambda i: (0, i, 0)), pl.BlockSpec(w.shape, lambda i: (0, 0, 0))]
        + ([row] if res is not None else []) + ([par, par] if ln else []),
        out_specs=[row] * (3 if ln else 1),
        out_shape=[sd((m, n), out_dtype)] + ([sd((m, n), F32), sd((m, n), BF16)] if ln else []),
        compiler_params=_cparams(("parallel",)), name=name,
    )(a, w, *((res,) if res is not None else ()), *(ln or ()))
    return out if ln else out[0]


def _mm_tn_blocks(a, b, name, blk=0, nb=N_SHARD, tk=1024, out_dtype=BF16):
    a3, b3 = a.ndim == 3, b.ndim == 3
    t, m, n = a.shape[-2], a.shape[-1], b.shape[-1]
    tk = min(tk, t)
    nsteps = t // tk

    def spec(blocked, width):
        if blocked:
            return pl.BlockSpec((nb, tk, width), lambda k: (blk, k, 0))
        return pl.BlockSpec((tk, width), lambda k: (k, 0))

    def body(a_ref, b_ref, o_ref, acc):
        k = pl.program_id(0)

        @pl.when(k == 0)
        def _():
            acc[...] = jnp.zeros_like(acc)

        for j in range(nb):
            acc[j] += _dot(a_ref[j] if a3 else a_ref[...], b_ref[j] if b3 else b_ref[...], _TN)

        @pl.when(k == nsteps - 1)
        def _():
            o_ref[...] = acc[...].astype(o_ref.dtype)

    return pl.pallas_call(
        body, grid=(nsteps,), in_specs=[spec(a3, m), spec(b3, n)],
        out_specs=pl.BlockSpec((nb, m, n), lambda k: (0, 0, 0)),
        out_shape=jax.ShapeDtypeStruct((nb, m, n), out_dtype),
        scratch_shapes=[pltpu.VMEM((nb, m, n), F32)],
        compiler_params=_cparams(("arbitrary",)), name=name,
    )(a, b)


def _row_spec(tm, n):
    return pl.BlockSpec((tm, n), lambda i: (i, 0))


def _par_spec(n, rows=1):
    return pl.BlockSpec((rows, n), lambda i: (0, 0))


def _swap_pairs(x):
    lane = lax.broadcasted_iota(jnp.int32, x.shape, 1)
    return jnp.where(lane % 2 == 0, pltpu.roll(x, LANES - 1, 1), pltpu.roll(x, 1, 1))


def _head_sums(v):
    lo = lax.broadcasted_iota(jnp.int32, v.shape, 1) < HEAD_DIM
    s_lo = jnp.sum(jnp.where(lo, v, 0.0), axis=-1, keepdims=True)
    s_hi = jnp.sum(jnp.where(lo, 0.0, v), axis=-1, keepdims=True)
    return jnp.where(lo, s_lo, s_hi)


def _qk_blocks():
    return [(128 * i, True) for i in range(4)] + [(Q_W, False)]


def _prep_fwd(h, cos_t, sin_t, qn, kn, tm=256):
    t = h.shape[0]
    scale = HEAD_DIM ** -0.5

    def body(h_ref, c_ref, s_ref, qn_ref, kn_ref, qa_ref, ka_ref, va_ref, qb_ref, kb_ref, vb_ref):
        c = c_ref[...]
        s = s_ref[...]
        for start, is_q in _qk_blocks():
            x = h_ref[:, start:start + LANES]
            r = lax.rsqrt(_head_sums(x * x) * (1.0 / HEAD_DIM) + RMS_EPS)
            y = x * r * (qn_ref[...] if is_q else kn_ref[...])
            y = y * c + _swap_pairs(y) * s
            if is_q:
                qa_ref[:, start:start + LANES] = (y * scale).astype(BF16)
            else:
                ka_ref[...] = y.astype(BF16)
        va_ref[...] = h_ref[:, 640:768].astype(BF16)
        qb_ref[...] = (h_ref[:, 768:1280] * scale).astype(BF16)
        kb_ref[...] = h_ref[:, 1280:1408].astype(BF16)
        vb_ref[...] = h_ref[:, 1408:1536].astype(BF16)

    sd = jax.ShapeDtypeStruct
    return pl.pallas_call(
        body, grid=(t // tm,),
        in_specs=[_row_spec(tm, IN_COLS), _row_spec(tm, LANES), _row_spec(tm, LANES), _par_spec(LANES), _par_spec(LANES)],
        out_specs=[_row_spec(tm, Q_W), _row_spec(tm, KV_W), _row_spec(tm, KV_W),
                   _row_spec(tm, Q_W), _row_spec(tm, KV_W), _row_spec(tm, KV_W)],
        out_shape=[sd((t, Q_W), BF16), sd((t, KV_W), BF16), sd((t, KV_W), BF16),
                   sd((t, Q_W), BF16), sd((t, KV_W), BF16), sd((t, KV_W), BF16)],
        compiler_params=_cparams(("parallel",)), name="prep_fwd",
    )(h, cos_t, sin_t, qn, kn)


def _prep_bwd(h, cos_t, sin_t, qn, kn, dqa, dka, dva, dqb, dkb, dvb, tm=256):
    t = h.shape[0]
    scale = HEAD_DIM ** -0.5

    def body(h_ref, c_ref, s_ref, qn_ref, kn_ref, dqa_ref, dka_ref, dva_ref, dqb_ref, dkb_ref, dvb_ref,
             dh_ref, dqn_ref, dkn_ref):
        @pl.when(pl.program_id(0) == 0)
        def _():
            dqn_ref[...] = jnp.zeros_like(dqn_ref)
            dkn_ref[...] = jnp.zeros_like(dkn_ref)

        c = c_ref[...]
        s = s_ref[...]
        for start, is_q in _qk_blocks():
            x = h_ref[:, start:start + LANES]
            gain = qn_ref[...] if is_q else kn_ref[...]
            d = dqa_ref[:, start:start + LANES] * scale if is_q else dka_ref[...]
            dy = d * c + _swap_pairs(d * s)
            r = lax.rsqrt(_head_sums(x * x) * (1.0 / HEAD_DIM) + RMS_EPS)
            xr = x * r
            gsum = jnp.sum(dy * xr, axis=0, keepdims=True)
            if is_q:
                dqn_ref[...] += gsum
            else:
                dkn_ref[...] += gsum
            gy = dy * gain
            dx = r * (gy - xr * (_head_sums(xr * gy) * (1.0 / HEAD_DIM)))
            dh_ref[:, start:start + LANES] = dx.astype(BF16)
        dh_ref[:, 640:768] = dva_ref[...].astype(BF16)
        dh_ref[:, 768:1280] = (dqb_ref[...] * scale).astype(BF16)
        dh_ref[:, 1280:1408] = dkb_ref[...].astype(BF16)
        dh_ref[:, 1408:1536] = dvb_ref[...].astype(BF16)

    sd = jax.ShapeDtypeStruct
    return pl.pallas_call(
        body, grid=(t // tm,),
        in_specs=[_row_spec(tm, IN_COLS), _row_spec(tm, LANES), _row_spec(tm, LANES), _par_spec(LANES), _par_spec(LANES),
                  _row_spec(tm, Q_W), _row_spec(tm, KV_W), _row_spec(tm, KV_W),
                  _row_spec(tm, Q_W), _row_spec(tm, KV_W), _row_spec(tm, KV_W)],
        out_specs=[_row_spec(tm, IN_COLS), _par_spec(LANES), _par_spec(LANES)],
        out_shape=[sd((t, IN_COLS), BF16), sd((1, LANES), F32), sd((1, LANES), F32)],
        compiler_params=_cparams(("arbitrary",)), name="prep_bwd",
    )(h, cos_t, sin_t, qn, kn, dqa, dka, dva, dqb, dkb, dvb)


def _outnorm_fwd(oa, ob, ga, gb, tm=512):
    t = oa.shape[0]

    def body(oa_ref, ob_ref, ga_ref, gb_ref, y_ref):
        for o_ref, g_ref, start in ((oa_ref, ga_ref, 0), (ob_ref, gb_ref, Q_W)):
            x = o_ref[...]
            r = lax.rsqrt(jnp.mean(x * x, axis=-1, keepdims=True) + RMS_EPS)
            y_ref[:, start:start + Q_W] = (x * r * g_ref[...]).astype(BF16)

    return pl.pallas_call(
        body, grid=(t // tm,),
        in_specs=[_row_spec(tm, Q_W), _row_spec(tm, Q_W), _par_spec(Q_W), _par_spec(Q_W)],
        out_specs=_row_spec(tm, D_MODEL), out_shape=jax.ShapeDtypeStruct((t, D_MODEL), BF16),
        compiler_params=_cparams(("parallel",)), name="outnorm_fwd",
    )(oa, ob, ga, gb)


def _outnorm_bwd(dy, oa, ob, ga, gb, tm=512):
    t = oa.shape[0]

    def body(dy_ref, oa_ref, ob_ref, ga_ref, gb_ref, doa_ref, dob_ref, dga_ref, dgb_ref):
        @pl.when(pl.program_id(0) == 0)
        def _():
            dga_ref[...] = jnp.zeros_like(dga_ref)
            dgb_ref[...] = jnp.zeros_like(dgb_ref)

        for o_ref, g_ref, do_ref, dg_ref, start in ((oa_ref, ga_ref, doa_ref, dga_ref, 0),
                                                    (ob_ref, gb_ref, dob_ref, dgb_ref, Q_W)):
            x = o_ref[...]
            d = dy_ref[:, start:start + Q_W]
            r = lax.rsqrt(jnp.mean(x * x, axis=-1, keepdims=True) + RMS_EPS)
            xr = x * r
            dg_ref[...] += jnp.sum(d * xr, axis=0, keepdims=True)
            gy = d * g_ref[...]
            do_ref[...] = r * (gy - xr * jnp.mean(xr * gy, axis=-1, keepdims=True))

    sd = jax.ShapeDtypeStruct
    return pl.pallas_call(
        body, grid=(t // tm,),
        in_specs=[_row_spec(tm, D_MODEL), _row_spec(tm, Q_W), _row_spec(tm, Q_W), _par_spec(Q_W), _par_spec(Q_W)],
        out_specs=[_row_spec(tm, Q_W), _row_spec(tm, Q_W), _par_spec(Q_W), _par_spec(Q_W)],
        out_shape=[sd((t, Q_W), F32), sd((t, Q_W), F32), sd((1, Q_W), F32), sd((1, Q_W), F32)],
        compiler_params=_cparams(("arbitrary",)), name="outnorm_bwd",
    )(dy, oa, ob, ga, gb)


def _ln_bwd(d, z, g, tm=512):
    t = z.shape[0]

    def body(d_ref, z_ref, g_ref, dz_ref, dzb_ref, dg_ref, db_ref):
        @pl.when(pl.program_id(0) == 0)
        def _():
            dg_ref[...] = jnp.zeros_like(dg_ref)
            db_ref[...] = jnp.zeros_like(db_ref)

        zz = z_ref[...]
        dd = d_ref[...]
        mu = jnp.mean(zz, axis=-1, keepdims=True)
        zc = zz - mu
        r = lax.rsqrt(jnp.mean(zc * zc, axis=-1, keepdims=True) + LN_EPS)
        xh = zc * r
        dg_ref[...] += jnp.sum(dd * xh, axis=0, keepdims=True)
        db_ref[...] += jnp.sum(dd, axis=0, keepdims=True)
        dxh = dd * g_ref[...]
        dz = r * (dxh - jnp.mean(dxh, axis=-1, keepdims=True) - xh * jnp.mean(dxh * xh, axis=-1, keepdims=True))
        dz_ref[...] = dz
        dzb_ref[...] = dz.astype(BF16)

    sd = jax.ShapeDtypeStruct
    return pl.pallas_call(
        body, grid=(t // tm,),
        in_specs=[_row_spec(tm, D_MODEL), _row_spec(tm, D_MODEL), _par_spec(D_MODEL)],
        out_specs=[_row_spec(tm, D_MODEL), _row_spec(tm, D_MODEL), _par_spec(D_MODEL), _par_spec(D_MODEL)],
        out_shape=[sd((t, D_MODEL), F32), sd((t, D_MODEL), BF16), sd((1, D_MODEL), F32), sd((1, D_MODEL), F32)],
        compiler_params=_cparams(("arbitrary",)), name="ln_bwd",
    )(d, z, g)


def _loss_grad(y, tgt, tm=512):
    t = y.shape[0]
    nsteps = t // tm

    def body(y_ref, t_ref, dy_ref, loss_ref, acc):
        i = pl.program_id(0)

        @pl.when(i == 0)
        def _():
            acc[...] = jnp.zeros_like(acc)

        e = y_ref[...] - t_ref[...]
        dy_ref[...] = e * (1.0 / D_MODEL)
        acc[...] += jnp.sum(e * e, axis=0, keepdims=True)

        @pl.when(i == nsteps - 1)
        def _():
            tot = jnp.sum(acc[...], axis=-1, keepdims=True) * (0.5 / D_MODEL)
            loss_ref[...] = jnp.broadcast_to(tot, loss_ref.shape)

    sd = jax.ShapeDtypeStruct
    return pl.pallas_call(
        body, grid=(nsteps,),
        in_specs=[_row_spec(tm, D_MODEL), _row_spec(tm, D_MODEL)],
        out_specs=[_row_spec(tm, D_MODEL), _par_spec(LANES)],
        out_shape=[sd((t, D_MODEL), F32), sd((1, LANES), F32)],
        scratch_shapes=[pltpu.VMEM((1, D_MODEL), F32)],
        compiler_params=_cparams(("arbitrary",)), name="loss_grad",
    )(y, tgt)


_GELU_C = math.sqrt(2.0 / math.pi)
_GELU_K = 0.044715
HALO = 16


def _gelu_parts(x):
    x2 = x * x
    th = jnp.tanh(x * (_GELU_C + (_GELU_C * _GELU_K) * x2))
    a = 0.5 + 0.5 * th
    dact = a + (0.5 * x) * (1.0 - th * th) * (_GELU_C + (3.0 * _GELU_C * _GELU_K) * x2)
    return x * a, dact


def _halo_specs(tm, t, shift=0):
    last = t // HALO - 1
    cur = pl.BlockSpec((None, tm, FF_SH), lambda j, i: (j + shift, i, 0))
    prev = pl.BlockSpec((None, HALO, FF_SH), lambda j, i: (j + shift, jnp.maximum(i * (tm // HALO) - 1, 0), 0))
    nxt = pl.BlockSpec((None, HALO, FF_SH), lambda j, i: (j + shift, jnp.minimum((i + 1) * (tm // HALO), last), 0))
    return [prev, cur, nxt]


def _gate_up_glu(x, wgu, cw, tm=512):
    t, kd = x.shape
    nsteps = t // tm
    last = t // HALO - 1

    def body(xp_ref, x_ref, xn_ref, w_ref, cw_ref, gu_ref, h_ref):
        i = pl.program_id(0)
        xc = x_ref[...]
        xp = jnp.where(i == 0, jnp.zeros_like(xp_ref[...]), xp_ref[...])
        xn = jnp.where(i == nsteps - 1, jnp.zeros_like(xn_ref[...]), xn_ref[...])
        xe = jnp.concatenate([xp, xc, xn], axis=0)
        te = tm + 2 * HALO
        mid = slice(HALO, HALO + tm)
        for j in range(N_SHARD):
            ge = _dot(xe, w_ref[j], _NN).astype(BF16)
            u = _dot(xc, w_ref[j + N_SHARD], _NN).astype(BF16)
            gu_ref[j] = ge[mid]
            gu_ref[j + N_SHARD] = u
            gf = ge.astype(F32)
            cwj = cw_ref[j]
            gc = (cwj[3:4, :] + pltpu.roll(gf, 1, 0) * cwj[0:1, :] + gf * cwj[1:2, :]
                  + pltpu.roll(gf, te - 1, 0) * cwj[2:3, :])
            act, _ = _gelu_parts(gc[mid])
            h_ref[j] = (act * u.astype(F32)).astype(BF16)

    sd = jax.ShapeDtypeStruct
    return pl.pallas_call(
        body, grid=(nsteps,),
        in_specs=[pl.BlockSpec((HALO, kd), lambda i: (jnp.maximum(i * (tm // HALO) - 1, 0), 0)),
                  pl.BlockSpec((tm, kd), lambda i: (i, 0)),
                  pl.BlockSpec((HALO, kd), lambda i: (jnp.minimum((i + 1) * (tm // HALO), last), 0)),
                  pl.BlockSpec(wgu.shape, lambda i: (0, 0, 0)), pl.BlockSpec(cw.shape, lambda i: (0, 0, 0))],
        out_specs=[pl.BlockSpec((2 * N_SHARD, tm, FF_SH), lambda i: (0, i, 0)),
                   pl.BlockSpec((N_SHARD, tm, FF_SH), lambda i: (0, i, 0))],
        out_shape=[sd((2 * N_SHARD, t, FF_SH), BF16), sd((N_SHARD, t, FF_SH), BF16)],
        compiler_params=_cparams(("parallel",)), name="gate_up_glu",
    )(x, x, x, wgu, cw)


def _ffn_mid_bwd(gu, dh, cw, tm=1024):
    t = gu.shape[1]
    tm = min(tm, t)
    nsteps = t // tm
    te = tm + 2 * HALO

    def body(gp_ref, g_ref, gn_ref, up_ref, u_ref, un_ref, dp_ref, d_ref, dn_ref, cw_ref, dgu_ref, st_ref):
        i = pl.program_id(1)

        @pl.when(i == 0)
        def _():
            st_ref[...] = jnp.zeros_like(st_ref)

        def ext(p_ref, c_ref, n_ref):
            prev = jnp.where(i == 0, 0.0, p_ref[...].astype(F32))
            nxt = jnp.where(i == nsteps - 1, 0.0, n_ref[...].astype(F32))
            return jnp.concatenate([prev, c_ref[...].astype(F32), nxt], axis=0)

        eg = ext(gp_ref, g_ref, gn_ref)
        eu = ext(up_ref, u_ref, un_ref)
        ed = ext(dp_ref, d_ref, dn_ref)
        w0, w1, w2 = cw_ref[0:1, :], cw_ref[1:2, :], cw_ref[2:3, :]
        g_m1 = pltpu.roll(eg, 1, 0)
        g_p1 = pltpu.roll(eg, te - 1, 0)
        gc = cw_ref[3:4, :] + g_m1 * w0 + eg * w1 + g_p1 * w2
        act, dact = _gelu_parts(gc)
        dgc = ed * eu * dact
        dg = pltpu.roll(dgc, te - 1, 0) * w0 + dgc * w1 + pltpu.roll(dgc, 1, 0) * w2
        mid = slice(HALO, HALO + tm)
        dgu_ref[0] = dg[mid].astype(BF16)
        dgu_ref[1] = (ed * act)[mid].astype(BF16)
        sel = dgc[mid]
        parts = [jnp.sum(sel, axis=0, keepdims=True),
                 jnp.sum(sel * g_m1[mid], axis=0, keepdims=True),
                 jnp.sum(sel * eg[mid], axis=0, keepdims=True),
                 jnp.sum(sel * g_p1[mid], axis=0, keepdims=True)]
        r8 = lax.broadcasted_iota(jnp.int32, (8, FF_SH), 0)
        upd = jnp.zeros((8, FF_SH), F32)
        for k, p in enumerate(parts):
            upd = upd + jnp.where(r8 == k, p, 0.0)
        st_ref[...] += upd

    sd = jax.ShapeDtypeStruct
    return pl.pallas_call(
        body, grid=(N_SHARD, nsteps),
        in_specs=_halo_specs(tm, t) + _halo_specs(tm, t, N_SHARD) + _halo_specs(tm, t)
        + [pl.BlockSpec((None, 8, FF_SH), lambda j, i: (j, 0, 0))],
        out_specs=[pl.BlockSpec((2, None, tm, FF_SH), lambda j, i: (0, j, i, 0)),
                   pl.BlockSpec((None, 8, FF_SH), lambda j, i: (j, 0, 0))],
        out_shape=[sd((2, N_SHARD, t, FF_SH), BF16), sd((N_SHARD, 8, FF_SH), F32)],
        compiler_params=_cparams(("parallel", "arbitrary")), name="ffn_mid_bwd",
    )(gu, gu, gu, gu, gu, gu, dh, dh, dh, cw)


def _stack_heads(src_ref, dst_ref, tq):
    lo = lax.broadcasted_iota(jnp.int32, (tq, LANES), 1) < HEAD_DIM
    for i in range(4):
        blk = src_ref[:, LANES * i:LANES * (i + 1)].astype(dst_ref.dtype)
        zero = jnp.zeros_like(blk)
        dst_ref[tq * i:tq * (i + 1), :] = jnp.where(lo, blk, zero)
        dst_ref[tq * (4 + i):tq * (5 + i), :] = jnp.where(lo, zero, blk)


def _gattn_fwd(q, k, v, gather=(), tq=128, tk=2048):
    t = q.shape[0]
    tk = min(tk, t)
    nq, nk, r = t // tq, t // tk, 8 * tq
    ng = len(gather)

    def body(*refs):
        q_ref, k_ref, v_ref = refs[:3]
        o_ref, lse_ref = refs[3 + ng:5 + ng]
        qst, m_s, l_s, acct = refs[5 + 2 * ng:9 + 2 * ng]
        if ng:
            ex = _gather_exchange(refs[3:3 + ng], refs[5 + ng:5 + 2 * ng], *refs[9 + 2 * ng:])
            pl.when(pl.program_id(0) == 0)(ex.start)
        lo_rows = lax.broadcasted_iota(jnp.int32, (LANES, tq), 0) < HEAD_DIM
        for i in range(4):
            bt = q_ref[:, LANES * i:LANES * (i + 1)].astype(F32).T
            qst[:, tq * i:tq * (i + 1)] = jnp.where(lo_rows, bt, 0.0).astype(BF16)
            qst[:, tq * (4 + i):tq * (5 + i)] = jnp.where(lo_rows, 0.0, bt).astype(BF16)
        m_s[...] = jnp.full_like(m_s, NEG)

        def max_step(j, carry):
            off = pl.multiple_of(j * tk, tk)
            st = _dot(k_ref[pl.ds(off, tk), :], qst[...], _NN)
            m_s[...] = jnp.maximum(m_s[...], jnp.max(st.reshape(tk // 8, 8, r), axis=0))
            return carry

        lax.fori_loop(0, nk, max_step, 0, unroll=2)
        m_row = jnp.max(m_s[...], axis=0, keepdims=True)
        l_s[...] = jnp.zeros_like(l_s)
        acct[...] = jnp.zeros_like(acct)

        def sum_step(j, carry):
            off = pl.multiple_of(j * tk, tk)
            st = _dot(k_ref[pl.ds(off, tk), :], qst[...], _NN)
            pt = jnp.exp(st - m_row)
            l_s[...] += jnp.sum(pt.reshape(tk // 8, 8, r), axis=0)
            acct[...] += _dot(v_ref[j], pt, _NN)
            return carry

        lax.fori_loop(0, nk, sum_step, 0, unroll=2)
        l_row = jnp.sum(l_s[...], axis=0, keepdims=True)
        ot = acct[...] / l_row
        for i in range(4):
            pair_t = jnp.where(lo_rows, ot[:, tq * i:tq * (i + 1)], ot[:, tq * (4 + i):tq * (5 + i)])
            o_ref[:, LANES * i:LANES * (i + 1)] = pair_t.T
        lse_ref[...] = m_row + jnp.log(l_row)
        if ng:
            pl.when(pl.program_id(0) == nq - 1)(ex.wait)

    sd = jax.ShapeDtypeStruct
    vt3 = v.reshape(nk, tk, KV_W).transpose(0, 2, 1)
    return pl.pallas_call(
        body, grid=(nq,),
        in_specs=[_row_spec(tq, Q_W), _par_spec(KV_W, t), pl.BlockSpec((nk, KV_W, tk), lambda i: (0, 0, 0))]
        + [_ANY] * ng,
        out_specs=[_row_spec(tq, Q_W), pl.BlockSpec((None, 1, r), lambda i: (i, 0, 0))] + [_ANY] * ng,
        out_shape=[sd((t, Q_W), F32), sd((nq, 1, r), F32)] + _gathered_shapes(gather),
        scratch_shapes=[pltpu.VMEM((LANES, r), BF16), pltpu.VMEM((8, r), F32), pltpu.VMEM((8, r), F32),
                        pltpu.VMEM((LANES, r), F32)] + (_exchange_sems(ng) if ng else []),
        compiler_params=_cparams(("arbitrary",) if ng else ("parallel",)),
        name="gattn_fwd_gather" if ng else "gattn_fwd",
    )(q, k, vt3, *gather)


def _gattn_bwd(q, k, v, o, do, lse, scatter=None, tq=128, tk=512):
    t = q.shape[0]
    tk = min(tk, t)
    nq, nk, r = t // tq, t // tk, 8 * tq
    items, sgrads = scatter if scatter else ((), ())
    ns = len(sgrads)
    slot_shapes = []
    for j, (o_idx, _) in enumerate(items):
        if o_idx == len(slot_shapes):
            slot_shapes += _slot_shapes([sgrads[j]])
    nslots = len(slot_shapes)

    n_in, n_scr = 7, 6
    kt3 = k.reshape(nk, tk, KV_W).transpose(0, 2, 1)

    def body(*refs):
        q_ref, k_ref, v_ref, kt_ref, o_ref, do_ref, lse_ref = refs[:n_in]
        dq_ref, dk_ref, dv_ref = refs[n_in + ns:n_in + 3 + ns]
        scr = n_in + 3 + ns + nslots
        qs, dos, qst, dost, dlt_row, dqt = refs[scr:scr + n_scr]
        if ns:
            ex = _scatter_exchange(items, refs[n_in:n_in + ns], refs[n_in + 3 + ns:scr], *refs[scr + n_scr:])
            pl.when(pl.program_id(0) == 0)(ex.start)

        @pl.when(pl.program_id(0) == 0)
        def _():
            dk_ref[...] = jnp.zeros_like(dk_ref)
            dv_ref[...] = jnp.zeros_like(dv_ref)

        _stack_heads(q_ref, qs, tq)
        _stack_heads(do_ref, dos, tq)
        lo_rows = lax.broadcasted_iota(jnp.int32, (LANES, tq), 0) < HEAD_DIM
        for i in range(4):
            lo, hi = slice(tq * i, tq * (i + 1)), slice(tq * (4 + i), tq * (5 + i))
            cols = slice(LANES * i, LANES * (i + 1))
            for src, dst in ((q_ref, qst), (do_ref, dost)):
                bt = src[:, cols].astype(F32).T
                dst[:, lo] = jnp.where(lo_rows, bt, 0.0).astype(BF16)
                dst[:, hi] = jnp.where(lo_rows, 0.0, bt).astype(BF16)
            prod_t = (do_ref[:, cols] * o_ref[:, cols]).T
            dlt_row[:, lo] = jnp.sum(prod_t[:HEAD_DIM], axis=0, keepdims=True)
            dlt_row[:, hi] = jnp.sum(prod_t[HEAD_DIM:], axis=0, keepdims=True)
        lse_row = lse_ref[...]
        dqt[...] = jnp.zeros_like(dqt)

        def step(j, carry):
            off = pl.multiple_of(j * tk, tk)
            kc = k_ref[pl.ds(off, tk), :]
            vc = v_ref[pl.ds(off, tk), :]
            p = jnp.exp(_dot(kc, qst[...], _NN) - lse_row)
            dp = _dot(vc, dost[...], _NN)
            ds = (p * (dp - dlt_row[...])).astype(BF16)
            dk_ref[pl.ds(off, tk), :] += _dot(ds, qs[...], _NN)
            dv_ref[pl.ds(off, tk), :] += _dot(p, dos[...], _NN)
            dqt[...] += _dot(kt_ref[j], ds, _NN)
            return carry

        lax.fori_loop(0, nk, step, 0, unroll=4)
        for i in range(4):
            pair_t = jnp.where(lo_rows, dqt[:, tq * i:tq * (i + 1)], dqt[:, tq * (4 + i):tq * (5 + i)])
            dq_ref[:, LANES * i:LANES * (i + 1)] = pair_t.T
        if ns:
            pl.when(pl.program_id(0) == nq - 1)(ex.wait)

    sd = jax.ShapeDtypeStruct
    return pl.pallas_call(
        body, grid=(nq,),
        in_specs=[_row_spec(tq, Q_W), _par_spec(KV_W, t), _par_spec(KV_W, t),
                  pl.BlockSpec((nk, KV_W, tk), lambda i: (0, 0, 0)), _row_spec(tq, Q_W), _row_spec(tq, Q_W),
                  pl.BlockSpec((None, 1, r), lambda i: (i, 0, 0))] + [_ANY] * ns,
        out_specs=[_row_spec(tq, Q_W), _par_spec(KV_W, t), _par_spec(KV_W, t)] + [_ANY] * nslots,
        out_shape=[sd((t, Q_W), F32), sd((t, KV_W), F32), sd((t, KV_W), F32)] + slot_shapes,
        scratch_shapes=[pltpu.VMEM((r, LANES), BF16), pltpu.VMEM((r, LANES), BF16), pltpu.VMEM((LANES, r), BF16),
                        pltpu.VMEM((LANES, r), BF16), pltpu.VMEM((1, r), F32),
                        pltpu.VMEM((LANES, r), F32)] + (_exchange_sems(ns) if ns else []),
        compiler_params=_cparams(("arbitrary",)), name="gattn_bwd_scatter" if ns else "gattn_bwd",
    )(q, k, v, kt3, o, do, lse, *sgrads)


_WQ = Q_BLOCK
_WK = 3 * Q_BLOCK
_WR = 8 * _WQ


def _pairs_transposed(src_ref, dst, tq):
    lo_rows = lax.broadcasted_iota(jnp.int32, (LANES, tq), 0) < HEAD_DIM
    for i in range(4):
        bt = src_ref[:, LANES * i:LANES * (i + 1)].astype(F32).T
        dst[:, tq * i:tq * (i + 1)] = jnp.where(lo_rows, bt, 0.0).astype(BF16)
        dst[:, tq * (4 + i):tq * (5 + i)] = jnp.where(lo_rows, 0.0, bt).astype(BF16)


def _pairs_from_transposed(halves, dst_ref, tq):
    for i in range(4):
        pair_t = jnp.concatenate([h[:, tq * i:tq * (i + 1)] for h in halves], axis=0)
        dst_ref[:, LANES * i:LANES * (i + 1)] = pair_t.T.astype(dst_ref.dtype)


def _kv_quadrants(tq):
    return [(slice(HEAD_DIM * kv, HEAD_DIM * (kv + 1)), slice(4 * tq * kv, 4 * tq * (kv + 1))) for kv in range(2)]


def _wattn_scores_t(kw, qst, bias_ref, n, t):
    kabs = (n - 1) * _WQ + lax.broadcasted_iota(jnp.int32, (_WK, 1), 0)
    st = _dot(kw, qst[...], _NN) + bias_ref[...]
    return jnp.where((kabs >= 0) & (kabs < t), st, NEG)


def _window_t(ref3, n):
    return jnp.concatenate([ref3[n], ref3[n + 1], ref3[n + 2]], axis=1)


def _blocks_transposed(ap):
    return ap.reshape(ap.shape[0] // _WQ, _WQ, KV_W).transpose(0, 2, 1)


def _wattn_fwd(q, kp, vp, bias_t, sink):
    t = q.shape[0]
    nq = t // _WQ
    tp = t + 2 * _WQ
    vpt = _blocks_transposed(vp)

    def body(q_ref, k_ref, vt_ref, b_ref, sk_ref, o_ref, lse_ref, qst):
        n = pl.program_id(0)
        _pairs_transposed(q_ref, qst, _WQ)
        kw = k_ref[pl.ds(pl.multiple_of(n * _WQ, _WQ), _WK), :]
        st = _wattn_scores_t(kw, qst, b_ref, n, t)
        sk = sk_ref[...]
        m = jnp.maximum(jnp.max(st, axis=0, keepdims=True), sk)
        pt = jnp.exp(st - m)
        l = jnp.sum(pt, axis=0, keepdims=True) + jnp.exp(sk - m)
        vwt = _window_t(vt_ref, n)
        halves = [_dot(vwt[rows, :], pt[:, cols], _NN) / l[:, cols] for rows, cols in _kv_quadrants(_WQ)]
        _pairs_from_transposed(halves, o_ref, _WQ)
        lse_ref[...] = m + jnp.log(l)

    sd = jax.ShapeDtypeStruct
    return pl.pallas_call(
        body, grid=(nq,),
        in_specs=[_row_spec(_WQ, Q_W), _par_spec(KV_W, tp), pl.BlockSpec(vpt.shape, lambda i: (0, 0, 0)),
                  _par_spec(_WR, _WK), _par_spec(_WR)],
        out_specs=[_row_spec(_WQ, Q_W), pl.BlockSpec((None, 1, _WR), lambda i: (i, 0, 0))],
        out_shape=[sd((t, Q_W), F32), sd((nq, 1, _WR), F32)],
        scratch_shapes=[pltpu.VMEM((LANES, _WR), BF16)],
        compiler_params=_cparams(("parallel",)), name="wattn_fwd",
    )(q, kp, vpt, bias_t, sink)


def _wattn_bwd(q, kp, vp, bias_t, sink, o, do, lse):
    t = q.shape[0]
    nq = t // _WQ
    tp = t + 2 * _WQ
    kpt = _blocks_transposed(kp)

    def body(q_ref, k_ref, v_ref, kt_ref, b_ref, sk_ref, o_ref, do_ref, lse_ref, dq_ref, dk_ref, dv_ref, db_ref,
             dsk_ref, qs, dos, qst, dost):
        n = pl.program_id(0)

        @pl.when(n == 0)
        def _():
            dk_ref[...] = jnp.zeros_like(dk_ref)
            dv_ref[...] = jnp.zeros_like(dv_ref)
            db_ref[...] = jnp.zeros_like(db_ref)
            dsk_ref[...] = jnp.zeros_like(dsk_ref)

        _stack_heads(q_ref, qs, _WQ)
        _stack_heads(do_ref, dos, _WQ)
        _pairs_transposed(q_ref, qst, _WQ)
        _pairs_transposed(do_ref, dost, _WQ)
        delta = []
        for i in range(4):
            cols = slice(LANES * i, LANES * (i + 1))
            prod_t = (do_ref[:, cols] * o_ref[:, cols]).T
            delta.append((jnp.sum(prod_t[:HEAD_DIM], axis=0, keepdims=True),
                          jnp.sum(prod_t[HEAD_DIM:], axis=0, keepdims=True)))
        dlt = jnp.concatenate([d[0] for d in delta] + [d[1] for d in delta], axis=1)
        off = pl.multiple_of(n * _WQ, _WQ)
        kw = k_ref[pl.ds(off, _WK), :]
        vw = v_ref[pl.ds(off, _WK), :]
        lse_v = lse_ref[...]
        pt = jnp.exp(_wattn_scores_t(kw, qst, b_ref, n, t) - lse_v)
        dpt = _dot(vw, dost[...], _NN)
        ds = pt * (dpt - dlt)
        db_ref[...] += ds
        dsk_ref[...] -= jnp.exp(sk_ref[...] - lse_v) * dlt
        dsb = ds.astype(BF16)
        dk_ref[pl.ds(off, _WK), :] += _dot(dsb, qs[...], _NN)
        dv_ref[pl.ds(off, _WK), :] += _dot(pt, dos[...], _NN)
        kwt = _window_t(kt_ref, n)
        halves = [_dot(kwt[rows, :], dsb[:, cols], _NN) for rows, cols in _kv_quadrants(_WQ)]
        _pairs_from_transposed(halves, dq_ref, _WQ)

    sd = jax.ShapeDtypeStruct
    qb = _row_spec(_WQ, Q_W)
    return pl.pallas_call(
        body, grid=(nq,),
        in_specs=[qb, _par_spec(KV_W, tp), _par_spec(KV_W, tp), pl.BlockSpec(kpt.shape, lambda i: (0, 0, 0)),
                  _par_spec(_WR, _WK), _par_spec(_WR), qb, qb, pl.BlockSpec((None, 1, _WR), lambda i: (i, 0, 0))],
        out_specs=[qb, _par_spec(KV_W, tp), _par_spec(KV_W, tp), _par_spec(_WR, _WK), _par_spec(_WR)],
        out_shape=[sd((t, Q_W), F32), sd((tp, KV_W), F32), sd((tp, KV_W), F32), sd((_WK, _WR), F32), sd((1, _WR), F32)],
        scratch_shapes=[pltpu.VMEM((_WR, LANES), BF16), pltpu.VMEM((_WR, LANES), BF16), pltpu.VMEM((LANES, _WR), BF16),
                        pltpu.VMEM((LANES, _WR), BF16)],
        compiler_params=_cparams(("arbitrary",)), name="wattn_bwd",
    )(q, kp, vp, kpt, bias_t, sink, o, do, lse)


def _bias_bucket_reduce(db0, db1, bucket):
    def body(a_ref, b_ref, bk_ref, o_ref):
        d = a_ref[...] + b_ref[...]
        bk = bk_ref[...]
        lane = lax.broadcasted_iota(jnp.int32, (1, LANES), 1)
        out = jnp.zeros((1, LANES), F32)
        for b in range(N_BUCKETS):
            tot = jnp.sum(jnp.sum(jnp.where(bk == b, d, 0.0), axis=-1, keepdims=True), axis=0, keepdims=True)
            out = out + jnp.where(lane == b, tot, 0.0)
        o_ref[...] = out

    hb = pl.BlockSpec((None, _WQ, _WK), lambda h: (h, 0, 0))
    return pl.pallas_call(
        body, grid=(8,), in_specs=[hb, hb, pl.BlockSpec((_WQ, _WK), lambda h: (0, 0))],
        out_specs=pl.BlockSpec((None, 1, LANES), lambda h: (h, 0, 0)),
        out_shape=jax.ShapeDtypeStruct((8, 1, LANES), F32),
        compiler_params=_cparams(("parallel",)), name="bias_bucket_reduce",
    )(db0.reshape(8, _WQ, _WK), db1.reshape(8, _WQ, _WK), bucket)


def _rope_tables(t):
    rows_n = t // GRID_W
    row = jnp.repeat(jnp.arange(rows_n, dtype=F32), GRID_W)
    col = jnp.tile(jnp.arange(GRID_W, dtype=F32), rows_n)
    half = HEAD_DIM // 2
    inv_freq = ROPE_THETA ** (-jnp.arange(0, half, 2, dtype=F32) / half)
    ang = jnp.concatenate([row[:, None] * inv_freq, col[:, None] * inv_freq], axis=-1)
    cos, sin = jnp.cos(ang), jnp.sin(ang)
    c64 = jnp.repeat(cos, 2, axis=-1)
    s64 = jnp.stack([-sin, sin], axis=-1).reshape(t, HEAD_DIM)
    return jnp.tile(c64, (1, 2)), jnp.tile(s64, (1, 2))


def _t5_bucket(rel):
    half = N_BUCKETS // 2
    max_exact = half // 2
    bucket = jnp.where(rel > 0, half, 0)
    rp = jnp.abs(rel)
    rpf = jnp.maximum(rp, 1).astype(jnp.float32)
    large = max_exact + (jnp.log(rpf / max_exact) / math.log(MAX_DISTANCE / max_exact)
                         * (half - max_exact)).astype(jnp.int32)
    large = jnp.minimum(large, half - 1)
    return bucket + jnp.where(rp < max_exact, rp, large)


def _window_tables(rel_bias):
    qpos = jnp.arange(_WQ, dtype=jnp.int32)
    kpos = jnp.arange(_WK, dtype=jnp.int32) - _WQ
    rel = kpos[None, :] - qpos[:, None]
    bucket = _t5_bucket(rel)
    bias = jnp.zeros((8, _WQ, _WK), F32)
    for b in range(N_BUCKETS):
        bias = jnp.where((bucket == b)[None], rel_bias[b][:, None, None], bias)
    bias = jnp.where((jnp.abs(rel) <= WINDOW)[None], bias, NEG)
    return bias.reshape(_WR, _WK).T, bucket


def _pad_rows(a):
    return jnp.pad(a, ((_WQ, _WQ), (0, 0)))


def _layer_fwd(x, p, tabs, gather=None):
    cos_t, sin_t, bias = tabs
    h = _mm_nn(x, p["win"], F32, "in_proj")
    qa, ka, va, qb, kb, vb = _prep_fwd(h, cos_t, sin_t, p["qn"], p["kn"])
    if gather is None:
        oa, lse_a = _gattn_fwd(qa, ka, va)
    else:
        oa, lse_a, *gathered = _gattn_fwd(qa, ka, va, gather=gather[0])
        p = gather[1](gathered)
    kbp, vbp = _pad_rows(kb), _pad_rows(vb)
    ob, lse_b = _wattn_fwd(qb, kbp, vbp, bias, p["sink"])
    y = _outnorm_fwd(oa, ob, p["ga"], p["gb"])
    z1, x1, x1b = _mm_reduce(y[None], p["wout"][None], _NN, F32, "out_proj", res=x, res_scale=ALPHA,
                             ln=(p["ln1g"], p["ln1b"]))
    gu, hdn = _gate_up_glu(x1b, p["wgu"], p["cw"])
    z2, x2, _ = _mm_reduce(hdn, p["wd"], _NN, F32, "down_proj", res=x1, res_scale=ALPHA, ln=(p["ln2g"], p["ln2b"]))
    saved = dict(x=x, h=h, qa=qa, ka=ka, va=va, qb=qb, kbp=kbp, vbp=vbp, oa=oa, ob=ob, lse_a=lse_a, lse_b=lse_b,
                 y=y, z1=z1, x1b=x1b, gu=gu, hdn=hdn, z2=z2)
    return x2, saved


def _block_grads(g, names=("w_in", "w_out", "w_gate", "w_up", "w_down")):
    make = dict(
        w_in=lambda: _col_blocks(_in_cols_to_pairs(g["win"], _from_pairs), IN_SH),
        w_out=lambda: _mix_rows_to_pairs(g["wout"], _from_pairs).reshape(N_SHARD, OUT_SH, D_MODEL),
        w_gate=lambda: g["wg"], w_up=lambda: g["wu"], w_down=lambda: g["wd"])
    return [make[n]() for n in names]


def _layer_bwd(dx2, p, s, tabs, layer=0, pending=None):
    cos_t, sin_t, bias = tabs
    t = dx2.shape[0]
    dz2, dz2b, dln2g, dln2b = _ln_bwd(dx2, s["z2"], p["ln2g"])
    dhdn = _mm_expand(dz2b, p["wd"], _NT, BF16, "down_dx")
    dwd = _mm_tn_blocks(s["hdn"], dz2b, "down_dw")
    dgu, stats = _ffn_mid_bwd(s["gu"], dhdn, p["cw"])
    dgu = dgu.reshape(2 * N_SHARD, t, FF_SH)
    dx1 = _mm_reduce(dgu, p["wgu"], _NT, F32, "gate_up_dx", res=dz2, res_scale=ALPHA)
    dwg = _mm_tn_blocks(dgu, s["x1b"], "gate_dw", blk=0)
    dwu = _mm_tn_blocks(dgu, s["x1b"], "up_dw", blk=1)
    dz1, dz1b, dln1g, dln1b = _ln_bwd(dx1, s["z1"], p["ln1g"])
    dy = _mm_nt(dz1b, p["wout"], F32, "out_dx")
    dwout = _mm_tn(s["y"], dz1b, "out_dw")
    doa, dob, dga, dgb = _outnorm_bwd(dy, s["oa"], s["ob"], p["ga"], p["gb"])
    slots = None
    if pending is None:
        dqa, dka, dva = _gattn_bwd(s["qa"], s["ka"], s["va"], s["oa"], doa, s["lse_a"])
    else:
        mine = _block_grads(dict(wout=dwout, wg=dwg, wu=dwu, wd=dwd), ("w_out", "w_gate", "w_up", "w_down"))
        todo = list(pending) + [(o + 1, layer, g) for o, g in enumerate(mine)]
        dqa, dka, dva, *slots = _gattn_bwd(s["qa"], s["ka"], s["va"], s["oa"], doa, s["lse_a"],
                                           scatter=([(o, l) for o, l, _ in todo], [g for _, _, g in todo]))
    dqb, dkbp, dvbp, dbias, dsink = _wattn_bwd(s["qb"], s["kbp"], s["vbp"], bias, p["sink"], s["ob"], dob, s["lse_b"])
    dkb = lax.slice_in_dim(dkbp, _WQ, _WQ + t, axis=0)
    dvb = lax.slice_in_dim(dvbp, _WQ, _WQ + t, axis=0)
    dh, dqn, dkn = _prep_bwd(s["h"], cos_t, sin_t, p["qn"], p["kn"], dqa, dka, dva, dqb, dkb, dvb)
    dx = _mm_nt(dh, p["win"], F32, "in_dx", res=dz1, res_scale=ALPHA)
    dwin = _mm_tn(s["x"], dh, "in_dw")
    grads = dict(win=dwin, wout=dwout, wg=dwg, wu=dwu, wd=dwd, stats=stats, qn=dqn, kn=dkn, ga=dga, gb=dgb,
                 ln1g=dln1g, ln1b=dln1b, ln2g=dln2g, ln2b=dln2b, bias=dbias, sink=dsink, slots=slots)
    return dx, grads


def _prep_layer_params(l, win, wout, wg, wu, wd, cw, q_norm, k_norm, sink, out_norm_a, out_norm_b, conv_b,
                       ln1_g, ln1_b, ln2_g, ln2_b):
    win_full = win.transpose(1, 0, 2).reshape(D_MODEL, IN_COLS)
    row = lambda v: v.reshape(1, -1)
    late = {}
    if wout is not None:
        late = dict(
            wout=_mix_rows_to_pairs(wout.reshape(D_MODEL, D_MODEL)), wgu=jnp.concatenate([wg, wu], axis=0), wd=wd,
            cw=jnp.pad(cw, ((0, 0), (0, 5), (0, 0)))
            + jnp.pad(conv_b[l].reshape(N_SHARD, 1, FF_SH), ((0, 0), (3, 4), (0, 0))))
    return dict(
        late, win=_in_cols_to_pairs(win_full),
        qn=row(jnp.tile(q_norm[l], 2)), kn=row(jnp.tile(k_norm[l], 2)),
        ga=row(_to_pairs(out_norm_a[l], 0)), gb=row(_to_pairs(out_norm_b[l], 0)),
        ln1g=row(ln1_g[l]), ln1b=row(ln1_b[l]), ln2g=row(ln2_g[l]), ln2b=row(ln2_b[l]),
        sink=jnp.repeat(sink[l], _WQ).reshape(1, _WR))


def _local_step(x, tgt, params, rel_bias, gather=None, scatter=False):
    t = x.shape[0]
    cos_t, sin_t = _rope_tables(t)
    bias, bucket = _window_tables(rel_bias)
    tabs = (cos_t, sin_t, bias)
    saved = []
    for l in range(DEPTH):
        x, s = _layer_fwd(x, params[l], tabs, gather if l == 0 else None)
        saved.append(s)
    dx, loss = _loss_grad(x, tgt)
    grads = [None] * DEPTH
    for l in reversed(range(DEPTH)):
        pending = None
        if scatter and l == 0:
            pending = [(o, 1, g) for o, g in enumerate(_block_grads(grads[1]))]
        dx, grads[l] = _layer_bwd(dx, params[l], saved[l], tabs, l, pending)
    dbucket = _bias_bucket_reduce(grads[0]["bias"].T, grads[1]["bias"].T, bucket)
    return loss, dx, grads, dbucket


_ANY = pl.BlockSpec(memory_space=pl.ANY)
_MESH = pl.DeviceIdType.MESH


def _mesh_pos():
    return lax.axis_index("x"), lax.axis_index("y"), lax.axis_index("c")


def _other_chips(x, y):
    return [(1 - x, y), (x, 1 - y), (1 - x, 1 - y)]


class _Exchange:
    def __init__(self, local, sends, recvs):
        self.local, self.sends, self.recvs = local, sends, recvs

    def start(self):
        for cp in self.local + self.sends:
            cp.start()

    def wait(self):
        for cp in self.recvs:
            cp.wait_recv()
        for cp in self.sends:
            cp.wait_send()
        for cp in self.local:
            cp.wait()


def _exchange_sems(n):
    return [pltpu.SemaphoreType.DMA((n, 3)), pltpu.SemaphoreType.DMA((n, 3)), pltpu.SemaphoreType.DMA((n,))]


def _gather_exchange(ins, outs, send, recv, loc):
    x, y, c = _mesh_pos()
    me = 2 * x + y
    chips = _other_chips(x, y)

    def remote(i, k, block):
        px, py = chips[k]
        return pltpu.make_async_remote_copy(ins[i], outs[i].at[block], send.at[i, k], recv.at[i, k],
                                            device_id=(px, py, c), device_id_type=_MESH)

    n = len(ins)
    local = [pltpu.make_async_copy(ins[i], outs[i].at[me], loc.at[i]) for i in range(n)]
    sends = [remote(i, k, me) for i in range(n) for k in range(3)]
    recvs = [remote(i, k, 2 * chips[k][0] + chips[k][1]) for i in range(n) for k in range(3)]
    return _Exchange(local, sends, recvs)


def _scatter_exchange(items, ins, outs, send, recv, loc):
    x, y, c = _mesh_pos()
    me = 2 * x + y
    chips = _other_chips(x, y)

    def remote(j, k):
        o, l = items[j]
        px, py = chips[k]
        return pltpu.make_async_remote_copy(ins[j].at[2 * px + py], outs[o].at[k, l], send.at[j, k], recv.at[j, k],
                                            device_id=(px, py, c), device_id_type=_MESH)

    local = [pltpu.make_async_copy(ins[j].at[me], outs[o].at[3, l], loc.at[j]) for j, (o, l) in enumerate(items)]
    sends = [remote(j, k) for j in range(len(items)) for k in range(3)]
    return _Exchange(local, sends, sends)


def _gathered_shapes(shards):
    return [jax.ShapeDtypeStruct((N_SHARD,) + s.shape, s.dtype) for s in shards]


def _slot_shapes(blocks):
    return [jax.ShapeDtypeStruct((N_SHARD, DEPTH) + g.shape[1:], g.dtype) for g in blocks]


def _gather_shards(shards):
    n = len(shards)

    def body(*refs):
        ex = _gather_exchange(refs[:n], refs[n:2 * n], *refs[2 * n:])
        ex.start()
        ex.wait()

    return pl.pallas_call(
        body, in_specs=[_ANY] * n, out_specs=[_ANY] * n, out_shape=_gathered_shapes(shards),
        scratch_shapes=_exchange_sems(n), name="gather_weights",
    )(*shards)


def _scatter_into(items, grads, slots):
    n, ns = len(grads), len(slots)

    def body(*refs):
        ex = _scatter_exchange(items, refs[:n], refs[n + ns:n + 2 * ns], *refs[n + 2 * ns:])
        ex.start()
        ex.wait()

    return pl.pallas_call(
        body, in_specs=[_ANY] * (n + ns), out_specs=[_ANY] * ns,
        out_shape=[jax.ShapeDtypeStruct(s.shape, s.dtype) for s in slots],
        input_output_aliases={n + i: i for i in range(ns)},
        scratch_shapes=_exchange_sems(n), name="scatter_grads",
    )(*grads, *slots)


def _swap_with_sibling(parts):
    n = len(parts)

    def body(*refs):
        ins, outs = refs[:n], refs[n:2 * n]
        send, recv = refs[2 * n:]
        x, y, c = _mesh_pos()
        copies = [pltpu.make_async_remote_copy(ins[i], outs[i], send.at[i], recv.at[i], device_id=(x, y, 1 - c),
                                               device_id_type=_MESH) for i in range(n)]
        for cp in copies:
            cp.start()
        for cp in copies:
            cp.wait_recv()
        for cp in copies:
            cp.wait_send()

    return pl.pallas_call(
        body, in_specs=[_ANY] * n, out_specs=[_ANY] * n,
        out_shape=[jax.ShapeDtypeStruct(p.shape, p.dtype) for p in parts],
        scratch_shapes=[pltpu.SemaphoreType.DMA((n,)), pltpu.SemaphoreType.DMA((n,))],
        name="swap_sibling",
    )(*parts)


N_DEV = 8


def _allreduce_small(packed):
    rows = packed.shape[0]

    def body(in_ref, out_ref, buf, send, recv, loc):
        x, y, c = _mesh_pos()
        me = 4 * x + 2 * y + c
        own = pltpu.make_async_copy(in_ref, buf.at[me], loc)
        own.start()

        def remote(m, block):
            peer = (x ^ (m >> 2), y ^ ((m >> 1) & 1), c ^ (m & 1))
            return pltpu.make_async_remote_copy(in_ref, buf.at[block], send.at[m - 1], recv.at[m - 1],
                                                device_id=peer, device_id_type=_MESH)

        sends = [remote(m, me) for m in range(1, N_DEV)]
        for cp in sends:
            cp.start()
        for m in range(1, N_DEV):
            remote(m, me ^ m).wait_recv()
        for cp in sends:
            cp.wait_send()
        own.wait()
        tot = buf[0]
        for d in range(1, N_DEV):
            tot = tot + buf[d]
        out_ref[...] = tot

    vm = pl.BlockSpec(memory_space=pltpu.VMEM)
    return pl.pallas_call(
        body, in_specs=[vm], out_specs=vm, out_shape=jax.ShapeDtypeStruct((rows, LANES), F32),
        scratch_shapes=[pltpu.VMEM((N_DEV, rows, LANES), F32), pltpu.SemaphoreType.DMA((N_DEV - 1,)),
                        pltpu.SemaphoreType.DMA((N_DEV - 1,)), pltpu.SemaphoreType.DMA(())],
        name="allreduce_small",
    )(packed)


def _shard_rows(r):
    return r // 2 if r % 32 == 0 else r


def _sum_slots(slots):
    _, _, r, cdim = slots.shape
    tr = _shard_rows(r)

    def body(a_ref, b_ref, c_ref, d_ref, o_ref):
        up = lambda ref: ref[...].astype(F32)
        o_ref[...] = ((up(d_ref) + up(a_ref)) + up(b_ref)) + up(c_ref)

    def spec(k):
        return pl.BlockSpec((None, None, tr, cdim), lambda l, i: (k, l, i, 0))

    return pl.pallas_call(
        body, grid=(DEPTH, r // tr), in_specs=[spec(0), spec(1), spec(2), spec(3)],
        out_specs=pl.BlockSpec((None, tr, cdim), lambda l, i: (l, i, 0)),
        out_shape=jax.ShapeDtypeStruct((DEPTH, r, cdim), F32),
        compiler_params=_cparams(("parallel", "parallel")), name="sum_slots",
    )(slots, slots, slots, slots)


def _adamw_math(w, g, m, v):
    m = ADAM_B1 * m + (1.0 - ADAM_B1) * g
    v = ADAM_B2 * v + (1.0 - ADAM_B2) * (g * g)
    m_hat = m / (1.0 - ADAM_B1 ** ADAM_STEP)
    v_hat = v / (1.0 - ADAM_B2 ** ADAM_STEP)
    delta = -ADAM_LR * (m_hat / (jnp.sqrt(v_hat) + ADAM_EPS) + ADAM_WD * w)
    return delta, m, v


def _adamw_big(ga, gb, w, m, v):
    _, r, cdim = w.shape
    tr = _shard_rows(r)

    def body(ga_ref, gb_ref, w_ref, m_ref, v_ref, g_out, d_out, m_out, v_out):
        g = ga_ref[...] + gb_ref[...]
        d, mn, vn = _adamw_math(w_ref[...], g, m_ref[...], v_ref[...])
        g_out[...] = g
        d_out[...] = d
        m_out[...] = mn
        v_out[...] = vn

    spec = pl.BlockSpec((None, tr, cdim), lambda l, i: (l, i, 0))
    shp = jax.ShapeDtypeStruct(w.shape, F32)
    return pl.pallas_call(
        body, grid=(DEPTH, r // tr), in_specs=[spec] * 5, out_specs=[spec] * 4, out_shape=[shp] * 4,
        compiler_params=_cparams(("parallel", "parallel")), name="adamw_big",
    )(ga, gb, w, m, v)


def _adamw_small(ws, gs, ms, vs):
    n = len(ws)

    def body(*refs):
        w_r, g_r, m_r, v_r = (refs[k * n:(k + 1) * n] for k in range(4))
        d_o, m_o, v_o = (refs[(4 + k) * n:(5 + k) * n] for k in range(3))
        for i in range(n):
            d, mn, vn = _adamw_math(w_r[i][...], g_r[i][...], m_r[i][...], v_r[i][...])
            d_o[i][...] = d
            m_o[i][...] = mn
            v_o[i][...] = vn

    vm = pl.BlockSpec(memory_space=pltpu.VMEM)
    shp = [jax.ShapeDtypeStruct(w.shape, F32) for w in ws]
    outs = pl.pallas_call(
        body, in_specs=[vm] * (4 * n), out_specs=[vm] * (3 * n), out_shape=shp * 3, name="adamw_small",
    )(*ws, *gs, *ms, *vs)
    return outs[:n], outs[n:2 * n], outs[2 * n:]


def _tile_rows(a):
    a = a.reshape(-1, LANES)
    pad = (-a.shape[0]) % 8
    return jnp.pad(a, ((0, pad), (0, 0))) if pad else a


_SMALL_LAYER_PARTS = (("qn", 8), ("kn", 8), ("sink", 8), ("ga", 8), ("gb", 8), ("ln1g", 8), ("ln1b", 8),
                      ("ln2g", 8), ("ln2b", 8), ("stats", N_SHARD * 8 * FF_SH // LANES))
_SMALL_HEAD_ROWS = 16
_SMALL_LAYER_ROWS = sum(r for _, r in _SMALL_LAYER_PARTS)


def _pack_small(loss, dbucket, grads):
    parts = [_tile_rows(loss), _tile_rows(dbucket)]
    for l in range(DEPTH):
        parts += [_tile_rows(grads[l][name]) for name, _ in _SMALL_LAYER_PARTS]
    return jnp.concatenate(parts, axis=0)


def _unpack_small(tot, chip):
    out = dict(loss=tot[0, 0], rel_bias=tot[8:16, :N_BUCKETS].T)
    per = {name: [] for name, _ in _SMALL_LAYER_PARTS}
    for l in range(DEPTH):
        base = _SMALL_HEAD_ROWS + l * _SMALL_LAYER_ROWS
        for name, rows in _SMALL_LAYER_PARTS:
            per[name].append(tot[base:base + rows])
            base += rows
    fold = lambda v: v[0, :HEAD_DIM] + v[0, HEAD_DIM:]
    out["q_norm"] = jnp.stack([fold(v) for v in per["qn"]])
    out["k_norm"] = jnp.stack([fold(v) for v in per["kn"]])
    out["sink"] = jnp.stack([jnp.sum(v, axis=1) for v in per["sink"]])
    out["out_norm_a"] = jnp.stack([_from_pairs(v[:4].reshape(Q_W), 0) for v in per["ga"]])
    out["out_norm_b"] = jnp.stack([_from_pairs(v[:4].reshape(Q_W), 0) for v in per["gb"]])
    for name, key in (("ln1_g", "ln1g"), ("ln1_b", "ln1b"), ("ln2_g", "ln2g"), ("ln2_b", "ln2b")):
        out[name] = jnp.stack([v.reshape(D_MODEL) for v in per[key]])
    stats = [v.reshape(N_SHARD, 8, FF_SH) for v in per["stats"]]
    out["conv_b"] = jnp.stack([s[:, 0, :].reshape(D_FF) for s in stats])
    out["conv_w"] = jnp.stack([lax.dynamic_index_in_dim(s, chip, 0, keepdims=False)[1:4] for s in stats])
    return out


_WEIGHTS = ("rel_bias", "w_in", "q_norm", "k_norm", "sink", "out_norm_a", "out_norm_b", "w_out", "ln1_g", "ln1_b",
            "w_gate", "w_up", "conv_w", "conv_b", "w_down", "ln2_g", "ln2_b")
_BIG = ("w_in", "w_out", "w_gate", "w_up", "w_down")
_SMALL = tuple(n for n in _WEIGHTS if n not in _BIG)


def _col_blocks(g, n):
    return g.reshape(g.shape[0], N_SHARD, n).transpose(1, 0, 2)


def kernel(x, rel_bias, w_in, q_norm, k_norm, sink, out_norm_a, out_norm_b, w_out, ln1_g, ln1_b, w_gate, w_up, conv_w, conv_b, w_down, ln2_g, ln2_b, loss_target, m_rel_bias, m_w_in, m_q_norm, m_k_norm, m_sink, m_out_norm_a, m_out_norm_b, m_w_out, m_ln1_g, m_ln1_b, m_w_gate, m_w_up, m_conv_w, m_conv_b, m_w_down, m_ln2_g, m_ln2_b, v_rel_bias, v_w_in, v_q_norm, v_k_norm, v_sink, v_out_norm_a, v_out_norm_b, v_w_out, v_ln1_g, v_ln1_b, v_w_gate, v_w_up, v_conv_w, v_conv_b, v_w_down, v_ln2_g, v_ln2_b):
    w = dict(rel_bias=rel_bias, w_in=w_in, q_norm=q_norm, k_norm=k_norm, sink=sink, out_norm_a=out_norm_a,
             out_norm_b=out_norm_b, w_out=w_out, ln1_g=ln1_g, ln1_b=ln1_b, w_gate=w_gate, w_up=w_up, conv_w=conv_w,
             conv_b=conv_b, w_down=w_down, ln2_g=ln2_g, ln2_b=ln2_b)
    m = dict(rel_bias=m_rel_bias, w_in=m_w_in, q_norm=m_q_norm, k_norm=m_k_norm, sink=m_sink, out_norm_a=m_out_norm_a,
             out_norm_b=m_out_norm_b, w_out=m_w_out, ln1_g=m_ln1_g, ln1_b=m_ln1_b, w_gate=m_w_gate, w_up=m_w_up,
             conv_w=m_conv_w, conv_b=m_conv_b, w_down=m_w_down, ln2_g=m_ln2_g, ln2_b=m_ln2_b)
    v = dict(rel_bias=v_rel_bias, w_in=v_w_in, q_norm=v_q_norm, k_norm=v_k_norm, sink=v_sink, out_norm_a=v_out_norm_a,
             out_norm_b=v_out_norm_b, w_out=v_w_out, ln1_g=v_ln1_g, ln1_b=v_ln1_b, w_gate=v_w_gate, w_up=v_w_up,
             conv_w=v_conv_w, conv_b=v_conv_b, w_down=v_w_down, ln2_g=v_ln2_g, ln2_b=v_ln2_b)
    chip = 2 * lax.axis_index("x") + lax.axis_index("y")

    small_w = (q_norm, k_norm, sink, out_norm_a, out_norm_b, conv_b, ln1_g, ln1_b, ln2_g, ln2_b)
    (win0,) = _gather_shards([w_in[0].astype(BF16)])
    later = ([w[name][0].astype(BF16) for name in _BIG[1:]] + [w[name][1].astype(BF16) for name in _BIG] + [conv_w])
    params = [_prep_layer_params(0, win0, None, None, None, None, None, *small_w), None]

    def finish(g):
        wout0, wg0, wu0, wd0, win1, wout1, wg1, wu1, wd1, cw_all = g
        params[0] = _prep_layer_params(0, win0, wout0, wg0, wu0, wd0, cw_all[:, 0], *small_w)
        params[1] = _prep_layer_params(1, win1, wout1, wg1, wu1, wd1, cw_all[:, 1], *small_w)
        return params[0]

    loss, dx, grads, dbucket = _local_step(x[0], loss_target[0], params, rel_bias, gather=(later, finish),
                                           scatter=True)

    small = _unpack_small(_allreduce_small(_pack_small(loss, dbucket, grads)), chip)

    slots = list(grads[0]["slots"])
    slots[0] = _scatter_into([(0, 0)], _block_grads(grads[0], ("w_in",)), [slots[0]])[0]
    partial = [_sum_slots(s) for s in slots]
    other = _swap_with_sibling(partial)

    grad, delta, new_m, new_v = {}, {}, {}, {}
    for i, name in enumerate(_BIG):
        fix = (lambda a: jnp.swapaxes(a, 1, 2)) if name in ("w_gate", "w_up") else (lambda a: a)
        outs = _adamw_big(partial[i], other[i], fix(w[name]), fix(m[name]), fix(v[name]))
        grad[name], delta[name], new_m[name], new_v[name] = [fix(o) for o in outs]
    flat2 = lambda a: a.reshape(-1, a.shape[-1])
    ds, ms, vs = _adamw_small([flat2(w[n]) for n in _SMALL], [flat2(small[n]) for n in _SMALL],
                              [flat2(m[n]) for n in _SMALL], [flat2(v[n]) for n in _SMALL])
    for i, name in enumerate(_SMALL):
        grad[name] = small[name]
        delta[name] = ds[i].reshape(w[name].shape)
        new_m[name] = ms[i].reshape(w[name].shape)
        new_v[name] = vs[i].reshape(w[name].shape)

    return (small["loss"], dx[None], *[grad[n] for n in _WEIGHTS], *[delta[n] for n in _WEIGHTS],
            *[new_m[n] for n in _WEIGHTS], *[new_v[n] for n in _WEIGHTS])
```

```python
import math

import jax
import jax.numpy as jnp
from jax import lax
from jax.experimental import pallas as pl
from jax.experimental.pallas import tpu as pltpu

F32 = jnp.float32
BF16 = jnp.bfloat16

D_MODEL = 1024
DEPTH = 2
HEAD_DIM = 64
Q_W = 512
KV_W = 128
IN_COLS = 2 * (Q_W + 2 * KV_W)
N_SHARD = 4
IN_SH = IN_COLS // N_SHARD
OUT_SH = D_MODEL // N_SHARD
D_FF = 2816
FF_SH = D_FF // N_SHARD
Q_BLOCK = 128
WINDOW = 128
N_BUCKETS = 32
MAX_DISTANCE = 128
GRID_W = 64
ROPE_THETA = 10000.0
ALPHA = (2.0 * DEPTH) ** 0.25
RMS_EPS = 1e-6
LN_EPS = 1e-5
NEG = -1e30
LANES = 128
VMEM_LIMIT = 56 * 1024 * 1024

ADAM_LR = 0.001
ADAM_B1 = 0.9
ADAM_B2 = 0.999
ADAM_EPS = 1e-08
ADAM_WD = 0.01
ADAM_STEP = 10

_NN = (((1,), (0,)), ((), ()))
_NT = (((1,), (1,)), ((), ()))
_TN = (((0,), (0,)), ((), ()))


def _dot(a, b, dims):
    return lax.dot_general(a.astype(BF16), b.astype(BF16), dims, preferred_element_type=F32)


def _cparams(sem, vmem=VMEM_LIMIT):
    return pltpu.CompilerParams(dimension_semantics=sem, vmem_limit_bytes=vmem)


def _regroup(a, axis, n_outer, n_inner):
    shp = a.shape
    a = a.reshape(shp[:axis] + (n_outer, n_inner, HEAD_DIM) + shp[axis + 1:])
    return jnp.swapaxes(a, axis, axis + 1).reshape(shp)


def _to_pairs(a, axis):
    return _regroup(a, axis, 2, 4)


def _from_pairs(a, axis):
    return _regroup(a, axis, 4, 2)


def _in_cols_to_pairs(w, fn=_to_pairs):
    return jnp.concatenate([fn(w[..., :Q_W], w.ndim - 1), w[..., Q_W:Q_W + 2 * KV_W],
                            fn(w[..., Q_W + 2 * KV_W:2 * Q_W + 2 * KV_W], w.ndim - 1),
                            w[..., 2 * Q_W + 2 * KV_W:]], axis=-1)


def _mix_rows_to_pairs(w, fn=_to_pairs):
    return fn(w.reshape(2, Q_W, w.shape[-1]), 1).reshape(w.shape)


def _matmul(a, b, *, dims, grid, a_spec, b_spec, o_spec, out_shape, acc_shape, name, res=None,
            res_spec=None, res_scale=1.0):
    nk = grid[-1]
    kax = len(grid) - 1

    def body(*refs):
        if res is None:
            a_ref, b_ref, o_ref, acc = refs
            r_ref = None
        else:
            a_ref, b_ref, r_ref, o_ref, acc = refs
        k = pl.program_id(kax)

        @pl.when(k == 0)
        def _():
            acc[...] = jnp.zeros_like(acc)

        acc[...] += _dot(a_ref[...], b_ref[...], dims)

        @pl.when(k == nk - 1)
        def _():
            o = acc[...]
            if r_ref is not None:
                o = o + res_scale * r_ref[...]
            o_ref[...] = o.astype(o_ref.dtype)

    in_specs = [a_spec, b_spec] + ([res_spec] if res is not None else [])
    args = (a, b) + ((res,) if res is not None else ())
    sem = ("parallel",) * kax + ("arbitrary",)
    return pl.pallas_call(
        body, grid=grid, in_specs=in_specs, out_specs=o_spec, out_shape=out_shape,
        scratch_shapes=[pltpu.VMEM(acc_shape, F32)], compiler_params=_cparams(sem), name=name,
    )(*args)


def _mm_nn(a, b, out_dtype, name, tm=512, res=None, res_scale=1.0):
    m, kd = a.shape
    n = b.shape[1]
    return _matmul(
        a, b, dims=_NN, grid=(m // tm, 1),
        a_spec=pl.BlockSpec((tm, kd), lambda i, k: (i, 0)),
        b_spec=pl.BlockSpec((kd, n), lambda i, k: (0, 0)),
        o_spec=pl.BlockSpec((tm, n), lambda i, k: (i, 0)),
        out_shape=jax.ShapeDtypeStruct((m, n), out_dtype), acc_shape=(tm, n), name=name,
        res=res, res_spec=pl.BlockSpec((tm, n), lambda i, k: (i, 0)), res_scale=res_scale)


def _mm_nt(a, b, out_dtype, name, tm=512, res=None, res_scale=1.0):
    m, kd = a.shape
    n = b.shape[0]
    return _matmul(
        a, b, dims=_NT, grid=(m // tm, 1),
        a_spec=pl.BlockSpec((tm, kd), lambda i, k: (i, 0)),
        b_spec=pl.BlockSpec((n, kd), lambda i, k: (0, 0)),
        o_spec=pl.BlockSpec((tm, n), lambda i, k: (i, 0)),
        out_shape=jax.ShapeDtypeStruct((m, n), out_dtype), acc_shape=(tm, n), name=name,
        res=res, res_spec=pl.BlockSpec((tm, n), lambda i, k: (i, 0)), res_scale=res_scale)


def _mm_tn(a, b, name, tk=1024, tn=None, out_dtype=BF16):
    t, m = a.shape
    n = b.shape[1]
    tn = n if tn is None else tn
    tk = min(tk, t)
    return _matmul(
        a, b, dims=_TN, grid=(n // tn, t // tk),
        a_spec=pl.BlockSpec((tk, m), lambda j, k: (k, 0)),
        b_spec=pl.BlockSpec((tk, tn), lambda j, k: (k, j)),
        o_spec=pl.BlockSpec((m, tn), lambda j, k: (0, j)),
        out_shape=jax.ShapeDtypeStruct((m, n), out_dtype), acc_shape=(m, tn), name=name)


def _blocked_n(w, dims):
    return w.shape[2] if dims == _NN else w.shape[1]


def _mm_reduce(a, w, dims, out_dtype, name, tm=512, res=None, res_scale=1.0, ln=None):
    nb, m, kd = a.shape
    n = _blocked_n(w, dims)
    n_in = 2 + (res is not None) + (2 if ln else 0)

    def body(*refs):
        a_ref, w_ref = refs[0], refs[1]
        acc = _dot(a_ref[0], w_ref[0], dims)
        for j in range(1, nb):
            acc = acc + _dot(a_ref[j], w_ref[j], dims)
        if res is not None:
            acc = acc + res_scale * refs[2][...]
        refs[n_in][...] = acc.astype(out_dtype)
        if ln:
            g_ref, b_ref = refs[n_in - 2], refs[n_in - 1]
            zc = acc - jnp.mean(acc, axis=-1, keepdims=True)
            r = lax.rsqrt(jnp.mean(zc * zc, axis=-1, keepdims=True) + LN_EPS)
            y = zc * r * g_ref[...] + b_ref[...]
            refs[n_in + 1][...] = y
            refs[n_in + 2][...] = y.astype(BF16)

    row = pl.BlockSpec((tm, n), lambda i: (i, 0))
    par = pl.BlockSpec((1, n), lambda i: (0, 0))
    sd = jax.ShapeDtypeStruct
    out = pl.pallas_call(
        body, grid=(m // tm,),
        in_specs=[pl.BlockSpec((nb, tm, kd), lambda i: (0, i, 0)), pl.BlockSpec(w.shape, lambda i: (0, 0, 0))]
        + ([row] if res is not None else []) + ([par, par] if ln else []),
        out_specs=[row] * (3 if ln else 1),
        out_shape=[sd((m, n), out_dtype)] + ([sd((m, n), F32), sd((m, n), BF16)] if ln else []),
        compiler_params=_cparams(("parallel",)), name=name,
    )(a, w, *((res,) if res is not None else ()), *(ln or ()))
    return out if ln else out[0]


def _mm_tn_blocks(a, b, name, blk=0, nb=N_SHARD, tk=1024, out_dtype=BF16):
    a3, b3 = a.ndim == 3, b.ndim == 3
    t, m, n = a.shape[-2], a.shape[-1], b.shape[-1]
    tk = min(tk, t)
    nsteps = t // tk

    def spec(blocked, width):
        if blocked:
            return pl.BlockSpec((nb, tk, width), lambda k: (blk, k, 0))
        return pl.BlockSpec((tk, width), lambda k: (k, 0))

    def body(a_ref, b_ref, o_ref, acc):
        k = pl.program_id(0)

        @pl.when(k == 0)
        def _():
            acc[...] = jnp.zeros_like(acc)

        for j in range(nb):
            acc[j] += _dot(a_ref[j] if a3 else a_ref[...], b_ref[j] if b3 else b_ref[...], _TN)

        @pl.when(k == nsteps - 1)
        def _():
            o_ref[...] = acc[...].astype(o_ref.dtype)

    return pl.pallas_call(
        body, grid=(nsteps,), in_specs=[spec(a3, m), spec(b3, n)],
        out_specs=pl.BlockSpec((nb, m, n), lambda k: (0, 0, 0)),
        out_shape=jax.ShapeDtypeStruct((nb, m, n), out_dtype),
        scratch_shapes=[pltpu.VMEM((nb, m, n), F32)],
        compiler_params=_cparams(("arbitrary",)), name=name,
    )(a, b)


def _row_spec(tm, n):
    return pl.BlockSpec((tm, n), lambda i: (i, 0))


def _par_spec(n, rows=1):
    return pl.BlockSpec((rows, n), lambda i: (0, 0))


def _swap_pairs(x):
    lane = lax.broadcasted_iota(jnp.int32, x.shape, 1)
    return jnp.where(lane % 2 == 0, pltpu.roll(x, LANES - 1, 1), pltpu.roll(x, 1, 1))


def _head_sums(v):
    lo = lax.broadcasted_iota(jnp.int32, v.shape, 1) < HEAD_DIM
    s_lo = jnp.sum(jnp.where(lo, v, 0.0), axis=-1, keepdims=True)
    s_hi = jnp.sum(jnp.where(lo, 0.0, v), axis=-1, keepdims=True)
    return jnp.where(lo, s_lo, s_hi)


def _qk_blocks():
    return [(128 * i, True) for i in range(4)] + [(Q_W, False)]


def _prep_fwd(h, cos_t, sin_t, qn, kn, tm=256):
    t = h.shape[0]
    scale = HEAD_DIM ** -0.5

    def body(h_ref, c_ref, s_ref, qn_ref, kn_ref, qa_ref, ka_ref, va_ref, qb_ref, kb_ref, vb_ref):
        c = c_ref[...]
        s = s_ref[...]
        for start, is_q in _qk_blocks():
            x = h_ref[:, start:start + LANES]
            r = lax.rsqrt(_head_sums(x * x) * (1.0 / HEAD_DIM) + RMS_EPS)
            y = x * r * (qn_ref[...] if is_q else kn_ref[...])
            y = y * c + _swap_pairs(y) * s
            if is_q:
                qa_ref[:, start:start + LANES] = (y * scale).astype(BF16)
            else:
                ka_ref[...] = y.astype(BF16)
        va_ref[...] = h_ref[:, 640:768].astype(BF16)
        qb_ref[...] = (h_ref[:, 768:1280] * scale).astype(BF16)
        kb_ref[...] = h_ref[:, 1280:1408].astype(BF16)
        vb_ref[...] = h_ref[:, 1408:1536].astype(BF16)

    sd = jax.ShapeDtypeStruct
    return pl.pallas_call(
        body, grid=(t // tm,),
        in_specs=[_row_spec(tm, IN_COLS), _row_spec(tm, LANES), _row_spec(tm, LANES), _par_spec(LANES), _par_spec(LANES)],
        out_specs=[_row_spec(tm, Q_W), _row_spec(tm, KV_W), _row_spec(tm, KV_W),
                   _row_spec(tm, Q_W), _row_spec(tm, KV_W), _row_spec(tm, KV_W)],
        out_shape=[sd((t, Q_W), BF16), sd((t, KV_W), BF16), sd((t, KV_W), BF16),
                   sd((t, Q_W), BF16), sd((t, KV_W), BF16), sd((t, KV_W), BF16)],
        compiler_params=_cparams(("parallel",)), name="prep_fwd",
    )(h, cos_t, sin_t, qn, kn)


def _prep_bwd(h, cos_t, sin_t, qn, kn, dqa, dka, dva, dqb, dkb, dvb, tm=256):
    t = h.shape[0]
    scale = HEAD_DIM ** -0.5

    def body(h_ref, c_ref, s_ref, qn_ref, kn_ref, dqa_ref, dka_ref, dva_ref, dqb_ref, dkb_ref, dvb_ref,
             dh_ref, dqn_ref, dkn_ref):
        @pl.when(pl.program_id(0) == 0)
        def _():
            dqn_ref[...] = jnp.zeros_like(dqn_ref)
            dkn_ref[...] = jnp.zeros_like(dkn_ref)

        c = c_ref[...]
        s = s_ref[...]
        for start, is_q in _qk_blocks():
            x = h_ref[:, start:start + LANES]
            gain = qn_ref[...] if is_q else kn_ref[...]
            d = dqa_ref[:, start:start + LANES] * scale if is_q else dka_ref[...]
            dy = d * c + _swap_pairs(d * s)
            r = lax.rsqrt(_head_sums(x * x) * (1.0 / HEAD_DIM) + RMS_EPS)
            xr = x * r
            gsum = jnp.sum(dy * xr, axis=0, keepdims=True)
            if is_q:
                dqn_ref[...] += gsum
            else:
                dkn_ref[...] += gsum
            gy = dy * gain
            dx = r * (gy - xr * (_head_sums(xr * gy) * (1.0 / HEAD_DIM)))
            dh_ref[:, start:start + LANES] = dx.astype(BF16)
        dh_ref[:, 640:768] = dva_ref[...].astype(BF16)
        dh_ref[:, 768:1280] = (dqb_ref[...] * scale).astype(BF16)
        dh_ref[:, 1280:1408] = dkb_ref[...].astype(BF16)
        dh_ref[:, 1408:1536] = dvb_ref[...].astype(BF16)

    sd = jax.ShapeDtypeStruct
    return pl.pallas_call(
        body, grid=(t // tm,),
        in_specs=[_row_spec(tm, IN_COLS), _row_spec(tm, LANES), _row_spec(tm, LANES), _par_spec(LANES), _par_spec(LANES),
                  _row_spec(tm, Q_W), _row_spec(tm, KV_W), _row_spec(tm, KV_W),
                  _row_spec(tm, Q_W), _row_spec(tm, KV_W), _row_spec(tm, KV_W)],
        out_specs=[_row_spec(tm, IN_COLS), _par_spec(LANES), _par_spec(LANES)],
        out_shape=[sd((t, IN_COLS), BF16), sd((1, LANES), F32), sd((1, LANES), F32)],
        compiler_params=_cparams(("arbitrary",)), name="prep_bwd",
    )(h, cos_t, sin_t, qn, kn, dqa, dka, dva, dqb, dkb, dvb)


def _outnorm_fwd(oa, ob, ga, gb, tm=512):
    t = oa.shape[0]

    def body(oa_ref, ob_ref, ga_ref, gb_ref, y_ref):
        for o_ref, g_ref, start in ((oa_ref, ga_ref, 0), (ob_ref, gb_ref, Q_W)):
            x = o_ref[...]
            r = lax.rsqrt(jnp.mean(x * x, axis=-1, keepdims=True) + RMS_EPS)
            y_ref[:, start:start + Q_W] = (x * r * g_ref[...]).astype(BF16)

    return pl.pallas_call(
        body, grid=(t // tm,),
        in_specs=[_row_spec(tm, Q_W), _row_spec(tm, Q_W), _par_spec(Q_W), _par_spec(Q_W)],
        out_specs=_row_spec(tm, D_MODEL), out_shape=jax.ShapeDtypeStruct((t, D_MODEL), BF16),
        compiler_params=_cparams(("parallel",)), name="outnorm_fwd",
    )(oa, ob, ga, gb)


def _outnorm_bwd(dy, oa, ob, ga, gb, tm=512):
    t = oa.shape[0]

    def body(dy_ref, oa_ref, ob_ref, ga_ref, gb_ref, doa_ref, dob_ref, dga_ref, dgb_ref):
        @pl.when(pl.program_id(0) == 0)
        def _():
            dga_ref[...] = jnp.zeros_like(dga_ref)
            dgb_ref[...] = jnp.zeros_like(dgb_ref)

        for o_ref, g_ref, do_ref, dg_ref, start in ((oa_ref, ga_ref, doa_ref, dga_ref, 0),
                                                    (ob_ref, gb_ref, dob_ref, dgb_ref, Q_W)):
            x = o_ref[...]
            d = dy_ref[:, start:start + Q_W]
            r = lax.rsqrt(jnp.mean(x * x, axis=-1, keepdims=True) + RMS_EPS)
            xr = x * r
            dg_ref[...] += jnp.sum(d * xr, axis=0, keepdims=True)
            gy = d * g_ref[...]
            do_ref[...] = r * (gy - xr * jnp.mean(xr * gy, axis=-1, keepdims=True))

    sd = jax.ShapeDtypeStruct
    return pl.pallas_call(
        body, grid=(t // tm,),
        in_specs=[_row_spec(tm, D_MODEL), _row_spec(tm, Q_W), _row_spec(tm, Q_W), _par_spec(Q_W), _par_spec(Q_W)],
        out_specs=[_row_spec(tm, Q_W), _row_spec(tm, Q_W), _par_spec(Q_W), _par_spec(Q_W)],
        out_shape=[sd((t, Q_W), F32), sd((t, Q_W), F32), sd((1, Q_W), F32), sd((1, Q_W), F32)],
        compiler_params=_cparams(("arbitrary",)), name="outnorm_bwd",
    )(dy, oa, ob, ga, gb)


def _ln_bwd(d, z, g, tm=512):
    t = z.shape[0]

    def body(d_ref, z_ref, g_ref, dz_ref, dzb_ref, dg_ref, db_ref):
        @pl.when(pl.program_id(0) == 0)
        def _():
            dg_ref[...] = jnp.zeros_like(dg_ref)
            db_ref[...] = jnp.zeros_like(db_ref)

        zz = z_ref[...]
        dd = d_ref[...]
        mu = jnp.mean(zz, axis=-1, keepdims=True)
        zc = zz - mu
        r = lax.rsqrt(jnp.mean(zc * zc, axis=-1, keepdims=True) + LN_EPS)
        xh = zc * r
        dg_ref[...] += jnp.sum(dd * xh, axis=0, keepdims=True)
        db_ref[...] += jnp.sum(dd, axis=0, keepdims=True)
        dxh = dd * g_ref[...]
        dz = r * (dxh - jnp.mean(dxh, axis=-1, keepdims=True) - xh * jnp.mean(dxh * xh, axis=-1, keepdims=True))
        dz_ref[...] = dz
        dzb_ref[...] = dz.astype(BF16)

    sd = jax.ShapeDtypeStruct
    return pl.pallas_call(
        body, grid=(t // tm,),
        in_specs=[_row_spec(tm, D_MODEL), _row_spec(tm, D_MODEL), _par_spec(D_MODEL)],
        out_specs=[_row_spec(tm, D_MODEL), _row_spec(tm, D_MODEL), _par_spec(D_MODEL), _par_spec(D_MODEL)],
        out_shape=[sd((t, D_MODEL), F32), sd((t, D_MODEL), BF16), sd((1, D_MODEL), F32), sd((1, D_MODEL), F32)],
        compiler_params=_cparams(("arbitrary",)), name="ln_bwd",
    )(d, z, g)


def _loss_grad(y, tgt, tm=512):
    t = y.shape[0]
    nsteps = t // tm

    def body(y_ref, t_ref, dy_ref, loss_ref, acc):
        i = pl.program_id(0)

        @pl.when(i == 0)
        def _():
            acc[...] = jnp.zeros_like(acc)

        e = y_ref[...] - t_ref[...]
        dy_ref[...] = e * (1.0 / D_MODEL)
        acc[...] += jnp.sum(e * e, axis=0, keepdims=True)

        @pl.when(i == nsteps - 1)
        def _():
            tot = jnp.sum(acc[...], axis=-1, keepdims=True) * (0.5 / D_MODEL)
            loss_ref[...] = jnp.broadcast_to(tot, loss_ref.shape)

    sd = jax.ShapeDtypeStruct
    return pl.pallas_call(
        body, grid=(nsteps,),
        in_specs=[_row_spec(tm, D_MODEL), _row_spec(tm, D_MODEL)],
        out_specs=[_row_spec(tm, D_MODEL), _par_spec(LANES)],
        out_shape=[sd((t, D_MODEL), F32), sd((1, LANES), F32)],
        scratch_shapes=[pltpu.VMEM((1, D_MODEL), F32)],
        compiler_params=_cparams(("arbitrary",)), name="loss_grad",
    )(y, tgt)


_GELU_C = math.sqrt(2.0 / math.pi)
_GELU_K = 0.044715
HALO = 16


def _gelu_parts(x):
    x2 = x * x
    th = jnp.tanh(x * (_GELU_C + (_GELU_C * _GELU_K) * x2))
    a = 0.5 + 0.5 * th
    dact = a + (0.5 * x) * (1.0 - th * th) * (_GELU_C + (3.0 * _GELU_C * _GELU_K) * x2)
    return x * a, dact


def _halo_specs(tm, t, shift=0):
    last = t // HALO - 1
    cur = pl.BlockSpec((None, tm, FF_SH), lambda j, i: (j + shift, i, 0))
    prev = pl.BlockSpec((None, HALO, FF_SH), lambda j, i: (j + shift, jnp.maximum(i * (tm // HALO) - 1, 0), 0))
    nxt = pl.BlockSpec((None, HALO, FF_SH), lambda j, i: (j + shift, jnp.minimum((i + 1) * (tm // HALO), last), 0))
    return [prev, cur, nxt]


def _gate_up_glu(x, wgu, cw, tm=512):
    t, kd = x.shape
    nsteps = t // tm
    last = t // HALO - 1

    def body(xp_ref, x_ref, xn_ref, w_ref, cw_ref, gu_ref, h_ref):
        i = pl.program_id(0)
        xc = x_ref[...]
        xp = jnp.where(i == 0, jnp.zeros_like(xp_ref[...]), xp_ref[...])
        xn = jnp.where(i == nsteps - 1, jnp.zeros_like(xn_ref[...]), xn_ref[...])
        xe = jnp.concatenate([xp, xc, xn], axis=0)
        te = tm + 2 * HALO
        mid = slice(HALO, HALO + tm)
        for j in range(N_SHARD):
            ge = _dot(xe, w_ref[j], _NN).astype(BF16)
            u = _dot(xc, w_ref[j + N_SHARD], _NN).astype(BF16)
            gu_ref[j] = ge[mid]
            gu_ref[j + N_SHARD] = u
            gf = ge.astype(F32)
            cwj = cw_ref[j]
            gc = (cwj[3:4, :] + pltpu.roll(gf, 1, 0) * cwj[0:1, :] + gf * cwj[1:2, :]
                  + pltpu.roll(gf, te - 1, 0) * cwj[2:3, :])
            act, _ = _gelu_parts(gc[mid])
            h_ref[j] = (act * u.astype(F32)).astype(BF16)

    sd = jax.ShapeDtypeStruct
    return pl.pallas_call(
        body, grid=(nsteps,),
        in_specs=[pl.BlockSpec((HALO, kd), lambda i: (jnp.maximum(i * (tm // HALO) - 1, 0), 0)),
                  pl.BlockSpec((tm, kd), lambda i: (i, 0)),
                  pl.BlockSpec((HALO, kd), lambda i: (jnp.minimum((i + 1) * (tm // HALO), last), 0)),
                  pl.BlockSpec(wgu.shape, lambda i: (0, 0, 0)), pl.BlockSpec(cw.shape, lambda i: (0, 0, 0))],
        out_specs=[pl.BlockSpec((2 * N_SHARD, tm, FF_SH), lambda i: (0, i, 0)),
                   pl.BlockSpec((N_SHARD, tm, FF_SH), lambda i: (0, i, 0))],
        out_shape=[sd((2 * N_SHARD, t, FF_SH), BF16), sd((N_SHARD, t, FF_SH), BF16)],
        compiler_params=_cparams(("parallel",)), name="gate_up_glu",
    )(x, x, x, wgu, cw)


def _ffn_mid_bwd(gu, dz, wd, cw, tm=1024):
    t = gu.shape[1]
    tm = min(tm, t)
    nsteps = t // tm
    te = tm + 2 * HALO
    kd = dz.shape[1]

    def body(gp_ref, g_ref, gn_ref, up_ref, u_ref, un_ref, dp_ref, d_ref, dn_ref, wd_ref, cw_ref, dgu_ref, st_ref):
        i = pl.program_id(1)

        @pl.when(i == 0)
        def _():
            st_ref[...] = jnp.zeros_like(st_ref)

        def ext(p_ref, c_ref, n_ref, dtype=F32):
            prev = jnp.where(i == 0, jnp.zeros_like(p_ref[...]), p_ref[...]).astype(dtype)
            nxt = jnp.where(i == nsteps - 1, jnp.zeros_like(n_ref[...]), n_ref[...]).astype(dtype)
            return jnp.concatenate([prev, c_ref[...].astype(dtype), nxt], axis=0)

        eg = ext(gp_ref, g_ref, gn_ref)
        eu = ext(up_ref, u_ref, un_ref)
        ed = _dot(ext(dp_ref, d_ref, dn_ref, BF16), wd_ref[...], _NT)
        w0, w1, w2 = cw_ref[0:1, :], cw_ref[1:2, :], cw_ref[2:3, :]
        g_m1 = pltpu.roll(eg, 1, 0)
        g_p1 = pltpu.roll(eg, te - 1, 0)
        gc = cw_ref[3:4, :] + g_m1 * w0 + eg * w1 + g_p1 * w2
        act, dact = _gelu_parts(gc)
        dgc = ed * eu * dact
        dg = pltpu.roll(dgc, te - 1, 0) * w0 + dgc * w1 + pltpu.roll(dgc, 1, 0) * w2
        mid = slice(HALO, HALO + tm)
        dgu_ref[0] = dg[mid].astype(BF16)
        dgu_ref[1] = (ed * act)[mid].astype(BF16)
        sel = dgc[mid]
        parts = [jnp.sum(sel, axis=0, keepdims=True),
                 jnp.sum(sel * g_m1[mid], axis=0, keepdims=True),
                 jnp.sum(sel * eg[mid], axis=0, keepdims=True),
                 jnp.sum(sel * g_p1[mid], axis=0, keepdims=True)]
        r8 = lax.broadcasted_iota(jnp.int32, (8, FF_SH), 0)
        upd = jnp.zeros((8, FF_SH), F32)
        for k, p in enumerate(parts):
            upd = upd + jnp.where(r8 == k, p, 0.0)
        st_ref[...] += upd

    sd = jax.ShapeDtypeStruct
    last = t // HALO - 1
    dz_specs = [pl.BlockSpec((HALO, kd), lambda j, i: (jnp.maximum(i * (tm // HALO) - 1, 0), 0)),
                pl.BlockSpec((tm, kd), lambda j, i: (i, 0)),
                pl.BlockSpec((HALO, kd), lambda j, i: (jnp.minimum((i + 1) * (tm // HALO), last), 0))]
    return pl.pallas_call(
        body, grid=(N_SHARD, nsteps),
        in_specs=_halo_specs(tm, t) + _halo_specs(tm, t, N_SHARD) + dz_specs
        + [pl.BlockSpec((None, FF_SH, kd), lambda j, i: (j, 0, 0)), pl.BlockSpec((None, 8, FF_SH), lambda j, i: (j, 0, 0))],
        out_specs=[pl.BlockSpec((2, None, tm, FF_SH), lambda j, i: (0, j, i, 0)),
                   pl.BlockSpec((None, 8, FF_SH), lambda j, i: (j, 0, 0))],
        out_shape=[sd((2, N_SHARD, t, FF_SH), BF16), sd((N_SHARD, 8, FF_SH), F32)],
        compiler_params=_cparams(("parallel", "arbitrary")), name="ffn_mid_bwd",
    )(gu, gu, gu, gu, gu, gu, dz, dz, dz, wd, cw)


def _stack_heads(src_ref, dst_ref, tq):
    lo = lax.broadcasted_iota(jnp.int32, (tq, LANES), 1) < HEAD_DIM
    for i in range(4):
        blk = src_ref[:, LANES * i:LANES * (i + 1)].astype(dst_ref.dtype)
        zero = jnp.zeros_like(blk)
        dst_ref[tq * i:tq * (i + 1), :] = jnp.where(lo, blk, zero)
        dst_ref[tq * (4 + i):tq * (5 + i), :] = jnp.where(lo, zero, blk)


def _gattn_fwd(q, k, v, gather=(), tq=128, tk=2048):
    t = q.shape[0]
    tk = min(tk, t)
    nq, nk, r = t // tq, t // tk, 8 * tq
    ng = len(gather)

    def body(*refs):
        q_ref, k_ref, v_ref = refs[:3]
        o_ref, lse_ref = refs[3 + ng:5 + ng]
        qst, m_s, l_s, acct = refs[5 + 2 * ng:9 + 2 * ng]
        if ng:
            ex = _gather_exchange(refs[3:3 + ng], refs[5 + ng:5 + 2 * ng], *refs[9 + 2 * ng:])
            pl.when(pl.program_id(0) == 0)(ex.start)
        lo_rows = lax.broadcasted_iota(jnp.int32, (LANES, tq), 0) < HEAD_DIM
        for i in range(4):
            bt = q_ref[:, LANES * i:LANES * (i + 1)].astype(F32).T
            qst[:, tq * i:tq * (i + 1)] = jnp.where(lo_rows, bt, 0.0).astype(BF16)
            qst[:, tq * (4 + i):tq * (5 + i)] = jnp.where(lo_rows, 0.0, bt).astype(BF16)
        m_s[...] = jnp.full_like(m_s, NEG)

        def max_step(j, carry):
            off = pl.multiple_of(j * tk, tk)
            st = _dot(k_ref[pl.ds(off, tk), :], qst[...], _NN)
            m_s[...] = jnp.maximum(m_s[...], jnp.max(st.reshape(tk // 8, 8, r), axis=0))
            return carry

        lax.fori_loop(0, nk, max_step, 0, unroll=2)
        m_row = jnp.max(m_s[...], axis=0, keepdims=True)
        l_s[...] = jnp.zeros_like(l_s)
        acct[...] = jnp.zeros_like(acct)

        def sum_step(j, carry):
            off = pl.multiple_of(j * tk, tk)
            st = _dot(k_ref[pl.ds(off, tk), :], qst[...], _NN)
            pt = jnp.exp(st - m_row)
            l_s[...] += jnp.sum(pt.reshape(tk // 8, 8, r), axis=0)
            acct[...] += _dot(v_ref[j], pt, _NN)
            return carry

        lax.fori_loop(0, nk, sum_step, 0, unroll=2)
        l_row = jnp.sum(l_s[...], axis=0, keepdims=True)
        ot = acct[...] / l_row
        for i in range(4):
            pair_t = jnp.where(lo_rows, ot[:, tq * i:tq * (i + 1)], ot[:, tq * (4 + i):tq * (5 + i)])
            o_ref[:, LANES * i:LANES * (i + 1)] = pair_t.T
        lse_ref[...] = m_row + jnp.log(l_row)
        if ng:
            pl.when(pl.program_id(0) == nq - 1)(ex.wait)

    sd = jax.ShapeDtypeStruct
    vt3 = v.reshape(nk, tk, KV_W).transpose(0, 2, 1)
    return pl.pallas_call(
        body, grid=(nq,),
        in_specs=[_row_spec(tq, Q_W), _par_spec(KV_W, t), pl.BlockSpec((nk, KV_W, tk), lambda i: (0, 0, 0))]
        + [_ANY] * ng,
        out_specs=[_row_spec(tq, Q_W), pl.BlockSpec((None, 1, r), lambda i: (i, 0, 0))] + [_ANY] * ng,
        out_shape=[sd((t, Q_W), F32), sd((nq, 1, r), F32)] + _gathered_shapes(gather),
        scratch_shapes=[pltpu.VMEM((LANES, r), BF16), pltpu.VMEM((8, r), F32), pltpu.VMEM((8, r), F32),
                        pltpu.VMEM((LANES, r), F32)] + (_exchange_sems(ng) if ng else []),
        compiler_params=_cparams(("arbitrary",) if ng else ("parallel",)),
        name="gattn_fwd_gather" if ng else "gattn_fwd",
    )(q, k, vt3, *gather)


def _gattn_bwd(q, k, v, o, do, lse, scatter=None, tq=128, tk=512):
    t = q.shape[0]
    tk = min(tk, t)
    nq, nk, r = t // tq, t // tk, 8 * tq
    items, sgrads = scatter if scatter else ((), ())
    ns = len(sgrads)
    slot_shapes = []
    for j, (o_idx, _) in enumerate(items):
        if o_idx == len(slot_shapes):
            slot_shapes += _slot_shapes([sgrads[j]])
    nslots = len(slot_shapes)

    n_in, n_scr = 7, 6
    kt3 = k.reshape(nk, tk, KV_W).transpose(0, 2, 1)

    def body(*refs):
        q_ref, k_ref, v_ref, kt_ref, o_ref, do_ref, lse_ref = refs[:n_in]
        dq_ref, dk_ref, dv_ref = refs[n_in + ns:n_in + 3 + ns]
        scr = n_in + 3 + ns + nslots
        qs, dos, qst, dost, dlt_row, dqt = refs[scr:scr + n_scr]
        if ns:
            ex = _scatter_exchange(items, refs[n_in:n_in + ns], refs[n_in + 3 + ns:scr], *refs[scr + n_scr:])
            pl.when(pl.program_id(0) == 0)(ex.start)

        @pl.when(pl.program_id(0) == 0)
        def _():
            dk_ref[...] = jnp.zeros_like(dk_ref)
            dv_ref[...] = jnp.zeros_like(dv_ref)

        _stack_heads(q_ref, qs, tq)
        _stack_heads(do_ref, dos, tq)
        lo_rows = lax.broadcasted_iota(jnp.int32, (LANES, tq), 0) < HEAD_DIM
        for i in range(4):
            lo, hi = slice(tq * i, tq * (i + 1)), slice(tq * (4 + i), tq * (5 + i))
            cols = slice(LANES * i, LANES * (i + 1))
            for src, dst in ((q_ref, qst), (do_ref, dost)):
                bt = src[:, cols].astype(F32).T
                dst[:, lo] = jnp.where(lo_rows, bt, 0.0).astype(BF16)
                dst[:, hi] = jnp.where(lo_rows, 0.0, bt).astype(BF16)
            prod_t = (do_ref[:, cols] * o_ref[:, cols]).T
            dlt_row[:, lo] = jnp.sum(prod_t[:HEAD_DIM], axis=0, keepdims=True)
            dlt_row[:, hi] = jnp.sum(prod_t[HEAD_DIM:], axis=0, keepdims=True)
        lse_row = lse_ref[...]
        dqt[...] = jnp.zeros_like(dqt)

        def step(j, carry):
            off = pl.multiple_of(j * tk, tk)
            kc = k_ref[pl.ds(off, tk), :]
            vc = v_ref[pl.ds(off, tk), :]
            p = jnp.exp(_dot(kc, qst[...], _NN) - lse_row)
            dp = _dot(vc, dost[...], _NN)
            ds = (p * (dp - dlt_row[...])).astype(BF16)
            dk_ref[pl.ds(off, tk), :] += _dot(ds, qs[...], _NN)
            dv_ref[pl.ds(off, tk), :] += _dot(p, dos[...], _NN)
            dqt[...] += _dot(kt_ref[j], ds, _NN)
            return carry

        lax.fori_loop(0, nk, step, 0, unroll=4)
        for i in range(4):
            pair_t = jnp.where(lo_rows, dqt[:, tq * i:tq * (i + 1)], dqt[:, tq * (4 + i):tq * (5 + i)])
            dq_ref[:, LANES * i:LANES * (i + 1)] = pair_t.T
        if ns:
            pl.when(pl.program_id(0) == nq - 1)(ex.wait)

    sd = jax.ShapeDtypeStruct
    return pl.pallas_call(
        body, grid=(nq,),
        in_specs=[_row_spec(tq, Q_W), _par_spec(KV_W, t), _par_spec(KV_W, t),
                  pl.BlockSpec((nk, KV_W, tk), lambda i: (0, 0, 0)), _row_spec(tq, Q_W), _row_spec(tq, Q_W),
                  pl.BlockSpec((None, 1, r), lambda i: (i, 0, 0))] + [_ANY] * ns,
        out_specs=[_row_spec(tq, Q_W), _par_spec(KV_W, t), _par_spec(KV_W, t)] + [_ANY] * nslots,
        out_shape=[sd((t, Q_W), F32), sd((t, KV_W), F32), sd((t, KV_W), F32)] + slot_shapes,
        scratch_shapes=[pltpu.VMEM((r, LANES), BF16), pltpu.VMEM((r, LANES), BF16), pltpu.VMEM((LANES, r), BF16),
                        pltpu.VMEM((LANES, r), BF16), pltpu.VMEM((1, r), F32),
                        pltpu.VMEM((LANES, r), F32)] + (_exchange_sems(ns) if ns else []),
        compiler_params=_cparams(("arbitrary",)), name="gattn_bwd_scatter" if ns else "gattn_bwd",
    )(q, k, v, kt3, o, do, lse, *sgrads)


_WQ = Q_BLOCK
_WK = 3 * Q_BLOCK
_WR = 8 * _WQ


def _pairs_transposed(src_ref, dst, tq):
    lo_rows = lax.broadcasted_iota(jnp.int32, (LANES, tq), 0) < HEAD_DIM
    for i in range(4):
        bt = src_ref[:, LANES * i:LANES * (i + 1)].astype(F32).T
        dst[:, tq * i:tq * (i + 1)] = jnp.where(lo_rows, bt, 0.0).astype(BF16)
        dst[:, tq * (4 + i):tq * (5 + i)] = jnp.where(lo_rows, 0.0, bt).astype(BF16)


def _pairs_from_transposed(halves, dst_ref, tq):
    for i in range(4):
        pair_t = jnp.concatenate([h[:, tq * i:tq * (i + 1)] for h in halves], axis=0)
        dst_ref[:, LANES * i:LANES * (i + 1)] = pair_t.T.astype(dst_ref.dtype)


def _kv_quadrants(tq):
    return [(slice(HEAD_DIM * kv, HEAD_DIM * (kv + 1)), slice(4 * tq * kv, 4 * tq * (kv + 1))) for kv in range(2)]


def _wattn_scores_t(kw, qst, bias_ref, n, t):
    kabs = (n - 1) * _WQ + lax.broadcasted_iota(jnp.int32, (_WK, 1), 0)
    st = _dot(kw, qst[...], _NN) + bias_ref[...]
    return jnp.where((kabs >= 0) & (kabs < t), st, NEG)


def _window_t(ref3, n):
    return jnp.concatenate([ref3[n], ref3[n + 1], ref3[n + 2]], axis=1)


def _blocks_transposed(ap):
    return ap.reshape(ap.shape[0] // _WQ, _WQ, KV_W).transpose(0, 2, 1)


def _wattn_fwd(q, kp, vp, bias_t, sink):
    t = q.shape[0]
    nq = t // _WQ
    tp = t + 2 * _WQ
    vpt = _blocks_transposed(vp)

    def body(q_ref, k_ref, vt_ref, b_ref, sk_ref, o_ref, lse_ref, qst):
        n = pl.program_id(0)
        _pairs_transposed(q_ref, qst, _WQ)
        kw = k_ref[pl.ds(pl.multiple_of(n * _WQ, _WQ), _WK), :]
        st = _wattn_scores_t(kw, qst, b_ref, n, t)
        sk = sk_ref[...]
        m = jnp.maximum(jnp.max(st, axis=0, keepdims=True), sk)
        pt = jnp.exp(st - m)
        l = jnp.sum(pt, axis=0, keepdims=True) + jnp.exp(sk - m)
        vwt = _window_t(vt_ref, n)
        halves = [_dot(vwt[rows, :], pt[:, cols], _NN) / l[:, cols] for rows, cols in _kv_quadrants(_WQ)]
        _pairs_from_transposed(halves, o_ref, _WQ)
        lse_ref[...] = m + jnp.log(l)

    sd = jax.ShapeDtypeStruct
    return pl.pallas_call(
        body, grid=(nq,),
        in_specs=[_row_spec(_WQ, Q_W), _par_spec(KV_W, tp), pl.BlockSpec(vpt.shape, lambda i: (0, 0, 0)),
                  _par_spec(_WR, _WK), _par_spec(_WR)],
        out_specs=[_row_spec(_WQ, Q_W), pl.BlockSpec((None, 1, _WR), lambda i: (i, 0, 0))],
        out_shape=[sd((t, Q_W), F32), sd((nq, 1, _WR), F32)],
        scratch_shapes=[pltpu.VMEM((LANES, _WR), BF16)],
        compiler_params=_cparams(("parallel",)), name="wattn_fwd",
    )(q, kp, vpt, bias_t, sink)


def _wattn_bwd(q, kp, vp, bias_t, sink, o, do, lse):
    t = q.shape[0]
    nq = t // _WQ
    tp = t + 2 * _WQ
    kpt = _blocks_transposed(kp)

    def body(q_ref, k_ref, v_ref, kt_ref, b_ref, sk_ref, o_ref, do_ref, lse_ref, dq_ref, dk_ref, dv_ref, db_ref,
             dsk_ref, qs, dos, qst, dost):
        n = pl.program_id(0)

        @pl.when(n == 0)
        def _():
            dk_ref[...] = jnp.zeros_like(dk_ref)
            dv_ref[...] = jnp.zeros_like(dv_ref)
            db_ref[...] = jnp.zeros_like(db_ref)
            dsk_ref[...] = jnp.zeros_like(dsk_ref)

        _stack_heads(q_ref, qs, _WQ)
        _stack_heads(do_ref, dos, _WQ)
        _pairs_transposed(q_ref, qst, _WQ)
        _pairs_transposed(do_ref, dost, _WQ)
        delta = []
        for i in range(4):
            cols = slice(LANES * i, LANES * (i + 1))
            prod_t = (do_ref[:, cols] * o_ref[:, cols]).T
            delta.append((jnp.sum(prod_t[:HEAD_DIM], axis=0, keepdims=True),
                          jnp.sum(prod_t[HEAD_DIM:], axis=0, keepdims=True)))
        dlt = jnp.concatenate([d[0] for d in delta] + [d[1] for d in delta], axis=1)
        off = pl.multiple_of(n * _WQ, _WQ)
        kw = k_ref[pl.ds(off, _WK), :]
        vw = v_ref[pl.ds(off, _WK), :]
        lse_v = lse_ref[...]
        pt = jnp.exp(_wattn_scores_t(kw, qst, b_ref, n, t) - lse_v)
        dpt = _dot(vw, dost[...], _NN)
        ds = pt * (dpt - dlt)
        db_ref[...] += ds
        dsk_ref[...] -= jnp.exp(sk_ref[...] - lse_v) * dlt
        dsb = ds.astype(BF16)
        dk_ref[pl.ds(off, _WK), :] += _dot(dsb, qs[...], _NN)
        dv_ref[pl.ds(off, _WK), :] += _dot(pt, dos[...], _NN)
        kwt = _window_t(kt_ref, n)
        halves = [_dot(kwt[rows, :], dsb[:, cols], _NN) for rows, cols in _kv_quadrants(_WQ)]
        _pairs_from_transposed(halves, dq_ref, _WQ)

    sd = jax.ShapeDtypeStruct
    qb = _row_spec(_WQ, Q_W)
    return pl.pallas_call(
        body, grid=(nq,),
        in_specs=[qb, _par_spec(KV_W, tp), _par_spec(KV_W, tp), pl.BlockSpec(kpt.shape, lambda i: (0, 0, 0)),
                  _par_spec(_WR, _WK), _par_spec(_WR), qb, qb, pl.BlockSpec((None, 1, _WR), lambda i: (i, 0, 0))],
        out_specs=[qb, _par_spec(KV_W, tp), _par_spec(KV_W, tp), _par_spec(_WR, _WK), _par_spec(_WR)],
        out_shape=[sd((t, Q_W), F32), sd((tp, KV_W), F32), sd((tp, KV_W), F32), sd((_WK, _WR), F32), sd((1, _WR), F32)],
        scratch_shapes=[pltpu.VMEM((_WR, LANES), BF16), pltpu.VMEM((_WR, LANES), BF16), pltpu.VMEM((LANES, _WR), BF16),
                        pltpu.VMEM((LANES, _WR), BF16)],
        compiler_params=_cparams(("arbitrary",)), name="wattn_bwd",
    )(q, kp, vp, kpt, bias_t, sink, o, do, lse)


def _bias_bucket_reduce(db0, db1, bucket):
    def body(a_ref, b_ref, bk_ref, o_ref):
        d = a_ref[...] + b_ref[...]
        bk = bk_ref[...]
        lane = lax.broadcasted_iota(jnp.int32, (1, LANES), 1)
        out = jnp.zeros((1, LANES), F32)
        for b in range(N_BUCKETS):
            tot = jnp.sum(jnp.sum(jnp.where(bk == b, d, 0.0), axis=-1, keepdims=True), axis=0, keepdims=True)
            out = out + jnp.where(lane == b, tot, 0.0)
        o_ref[...] = out

    hb = pl.BlockSpec((None, _WQ, _WK), lambda h: (h, 0, 0))
    return pl.pallas_call(
        body, grid=(8,), in_specs=[hb, hb, pl.BlockSpec((_WQ, _WK), lambda h: (0, 0))],
        out_specs=pl.BlockSpec((None, 1, LANES), lambda h: (h, 0, 0)),
        out_shape=jax.ShapeDtypeStruct((8, 1, LANES), F32),
        compiler_params=_cparams(("parallel",)), name="bias_bucket_reduce",
    )(db0.reshape(8, _WQ, _WK), db1.reshape(8, _WQ, _WK), bucket)


def _rope_tables(t):
    rows_n = t // GRID_W
    row = jnp.repeat(jnp.arange(rows_n, dtype=F32), GRID_W)
    col = jnp.tile(jnp.arange(GRID_W, dtype=F32), rows_n)
    half = HEAD_DIM // 2
    inv_freq = ROPE_THETA ** (-jnp.arange(0, half, 2, dtype=F32) / half)
    ang = jnp.concatenate([row[:, None] * inv_freq, col[:, None] * inv_freq], axis=-1)
    cos, sin = jnp.cos(ang), jnp.sin(ang)
    c64 = jnp.repeat(cos, 2, axis=-1)
    s64 = jnp.stack([-sin, sin], axis=-1).reshape(t, HEAD_DIM)
    return jnp.tile(c64, (1, 2)), jnp.tile(s64, (1, 2))


def _t5_bucket(rel):
    half = N_BUCKETS // 2
    max_exact = half // 2
    bucket = jnp.where(rel > 0, half, 0)
    rp = jnp.abs(rel)
    rpf = jnp.maximum(rp, 1).astype(jnp.float32)
    large = max_exact + (jnp.log(rpf / max_exact) / math.log(MAX_DISTANCE / max_exact)
                         * (half - max_exact)).astype(jnp.int32)
    large = jnp.minimum(large, half - 1)
    return bucket + jnp.where(rp < max_exact, rp, large)


def _window_tables(rel_bias):
    qpos = jnp.arange(_WQ, dtype=jnp.int32)
    kpos = jnp.arange(_WK, dtype=jnp.int32) - _WQ
    rel = kpos[None, :] - qpos[:, None]
    bucket = _t5_bucket(rel)
    bias = jnp.zeros((8, _WQ, _WK), F32)
    for b in range(N_BUCKETS):
        bias = jnp.where((bucket == b)[None], rel_bias[b][:, None, None], bias)
    bias = jnp.where((jnp.abs(rel) <= WINDOW)[None], bias, NEG)
    return bias.reshape(_WR, _WK).T, bucket


def _pad_rows(a):
    return jnp.pad(a, ((_WQ, _WQ), (0, 0)))


def _layer_fwd(x, p, tabs, gather=None):
    cos_t, sin_t, bias = tabs
    h = _mm_nn(x, p["win"], F32, "in_proj")
    qa, ka, va, qb, kb, vb = _prep_fwd(h, cos_t, sin_t, p["qn"], p["kn"])
    if gather is None:
        oa, lse_a = _gattn_fwd(qa, ka, va)
    else:
        oa, lse_a, *gathered = _gattn_fwd(qa, ka, va, gather=gather[0])
        p = gather[1](gathered)
    kbp, vbp = _pad_rows(kb), _pad_rows(vb)
    ob, lse_b = _wattn_fwd(qb, kbp, vbp, bias, p["sink"])
    y = _outnorm_fwd(oa, ob, p["ga"], p["gb"])
    z1, x1, x1b = _mm_reduce(y[None], p["wout"][None], _NN, F32, "out_proj", res=x, res_scale=ALPHA,
                             ln=(p["ln1g"], p["ln1b"]))
    gu, hdn = _gate_up_glu(x1b, p["wgu"], p["cw"])
    z2, x2, _ = _mm_reduce(hdn, p["wd"], _NN, F32, "down_proj", res=x1, res_scale=ALPHA, ln=(p["ln2g"], p["ln2b"]))
    saved = dict(x=x, h=h, qa=qa, ka=ka, va=va, qb=qb, kbp=kbp, vbp=vbp, oa=oa, ob=ob, lse_a=lse_a, lse_b=lse_b,
                 y=y, z1=z1, x1b=x1b, gu=gu, hdn=hdn, z2=z2)
    return x2, saved


def _block_grads(g, names=("w_in", "w_out", "w_gate", "w_up", "w_down")):
    make = dict(
        w_in=lambda: _col_blocks(_in_cols_to_pairs(g["win"], _from_pairs), IN_SH),
        w_out=lambda: _mix_rows_to_pairs(g["wout"], _from_pairs).reshape(N_SHARD, OUT_SH, D_MODEL),
        w_gate=lambda: g["wg"], w_up=lambda: g["wu"], w_down=lambda: g["wd"])
    return [make[n]() for n in names]


def _layer_bwd(dx2, p, s, tabs, layer=0, pending=None):
    cos_t, sin_t, bias = tabs
    t = dx2.shape[0]
    dz2, dz2b, dln2g, dln2b = _ln_bwd(dx2, s["z2"], p["ln2g"])
    dwd = _mm_tn_blocks(s["hdn"], dz2b, "down_dw")
    dgu, stats = _ffn_mid_bwd(s["gu"], dz2b, p["wd"], p["cw"])
    dgu = dgu.reshape(2 * N_SHARD, t, FF_SH)
    dx1 = _mm_reduce(dgu, p["wgu"], _NT, F32, "gate_up_dx", res=dz2, res_scale=ALPHA)
    dwg = _mm_tn_blocks(dgu, s["x1b"], "gate_dw", blk=0)
    dwu = _mm_tn_blocks(dgu, s["x1b"], "up_dw", blk=1)
    dz1, dz1b, dln1g, dln1b = _ln_bwd(dx1, s["z1"], p["ln1g"])
    dy = _mm_nt(dz1b, p["wout"], F32, "out_dx")
    dwout = _mm_tn(s["y"], dz1b, "out_dw")
    doa, dob, dga, dgb = _outnorm_bwd(dy, s["oa"], s["ob"], p["ga"], p["gb"])
    slots = None
    if pending is None:
        dqa, dka, dva = _gattn_bwd(s["qa"], s["ka"], s["va"], s["oa"], doa, s["lse_a"])
    else:
        mine = _block_grads(dict(wout=dwout, wg=dwg, wu=dwu, wd=dwd), ("w_out", "w_gate", "w_up", "w_down"))
        todo = list(pending) + [(o + 1, layer, g) for o, g in enumerate(mine)]
        dqa, dka, dva, *slots = _gattn_bwd(s["qa"], s["ka"], s["va"], s["oa"], doa, s["lse_a"],
                                           scatter=([(o, l) for o, l, _ in todo], [g for _, _, g in todo]))
    dqb, dkbp, dvbp, dbias, dsink = _wattn_bwd(s["qb"], s["kbp"], s["vbp"], bias, p["sink"], s["ob"], dob, s["lse_b"])
    dkb = lax.slice_in_dim(dkbp, _WQ, _WQ + t, axis=0)
    dvb = lax.slice_in_dim(dvbp, _WQ, _WQ + t, axis=0)
    dh, dqn, dkn = _prep_bwd(s["h"], cos_t, sin_t, p["qn"], p["kn"], dqa, dka, dva, dqb, dkb, dvb)
    dx = _mm_nt(dh, p["win"], F32, "in_dx", res=dz1, res_scale=ALPHA)
    dwin = _mm_tn(s["x"], dh, "in_dw")
    grads = dict(win=dwin, wout=dwout, wg=dwg, wu=dwu, wd=dwd, stats=stats, qn=dqn, kn=dkn, ga=dga, gb=dgb,
                 ln1g=dln1g, ln1b=dln1b, ln2g=dln2g, ln2b=dln2b, bias=dbias, sink=dsink, slots=slots)
    return dx, grads


def _prep_layer_params(l, win, wout, wg, wu, wd, cw, q_norm, k_norm, sink, out_norm_a, out_norm_b, conv_b,
                       ln1_g, ln1_b, ln2_g, ln2_b):
    win_full = win.transpose(1, 0, 2).reshape(D_MODEL, IN_COLS)
    row = lambda v: v.reshape(1, -1)
    late = {}
    if wout is not None:
        late = dict(
            wout=_mix_rows_to_pairs(wout.reshape(D_MODEL, D_MODEL)), wgu=jnp.concatenate([wg, wu], axis=0), wd=wd,
            cw=jnp.pad(cw, ((0, 0), (0, 5), (0, 0)))
            + jnp.pad(conv_b[l].reshape(N_SHARD, 1, FF_SH), ((0, 0), (3, 4), (0, 0))))
    return dict(
        late, win=_in_cols_to_pairs(win_full),
        qn=row(jnp.tile(q_norm[l], 2)), kn=row(jnp.tile(k_norm[l], 2)),
        ga=row(_to_pairs(out_norm_a[l], 0)), gb=row(_to_pairs(out_norm_b[l], 0)),
        ln1g=row(ln1_g[l]), ln1b=row(ln1_b[l]), ln2g=row(ln2_g[l]), ln2b=row(ln2_b[l]),
        sink=jnp.repeat(sink[l], _WQ).reshape(1, _WR))


def _local_step(x, tgt, params, rel_bias, gather=None, scatter=False):
    t = x.shape[0]
    cos_t, sin_t = _rope_tables(t)
    bias, bucket = _window_tables(rel_bias)
    tabs = (cos_t, sin_t, bias)
    saved = []
    for l in range(DEPTH):
        x, s = _layer_fwd(x, params[l], tabs, gather if l == 0 else None)
        saved.append(s)
    dx, loss = _loss_grad(x, tgt)
    grads = [None] * DEPTH
    for l in reversed(range(DEPTH)):
        pending = None
        if scatter and l == 0:
            pending = [(o, 1, g) for o, g in enumerate(_block_grads(grads[1]))]
        dx, grads[l] = _layer_bwd(dx, params[l], saved[l], tabs, l, pending)
    dbucket = _bias_bucket_reduce(grads[0]["bias"].T, grads[1]["bias"].T, bucket)
    return loss, dx, grads, dbucket


_ANY = pl.BlockSpec(memory_space=pl.ANY)
_MESH = pl.DeviceIdType.MESH


def _mesh_pos():
    return lax.axis_index("x"), lax.axis_index("y"), lax.axis_index("c")


def _other_chips(x, y):
    return [(1 - x, y), (x, 1 - y), (1 - x, 1 - y)]


class _Exchange:
    def __init__(self, local, sends, recvs):
        self.local, self.sends, self.recvs = local, sends, recvs

    def start(self):
        for cp in self.local + self.sends:
            cp.start()

    def wait(self):
        for cp in self.recvs:
            cp.wait_recv()
        for cp in self.sends:
            cp.wait_send()
        for cp in self.local:
            cp.wait()


def _exchange_sems(n):
    return [pltpu.SemaphoreType.DMA((n, 3)), pltpu.SemaphoreType.DMA((n, 3)), pltpu.SemaphoreType.DMA((n,))]


def _gather_exchange(ins, outs, send, recv, loc):
    x, y, c = _mesh_pos()
    me = 2 * x + y
    chips = _other_chips(x, y)

    def remote(i, k, block):
        px, py = chips[k]
        return pltpu.make_async_remote_copy(ins[i], outs[i].at[block], send.at[i, k], recv.at[i, k],
                                            device_id=(px, py, c), device_id_type=_MESH)

    n = len(ins)
    local = [pltpu.make_async_copy(ins[i], outs[i].at[me], loc.at[i]) for i in range(n)]
    sends = [remote(i, k, me) for i in range(n) for k in range(3)]
    recvs = [remote(i, k, 2 * chips[k][0] + chips[k][1]) for i in range(n) for k in range(3)]
    return _Exchange(local, sends, recvs)


def _scatter_exchange(items, ins, outs, send, recv, loc):
    x, y, c = _mesh_pos()
    me = 2 * x + y
    chips = _other_chips(x, y)

    def remote(j, k):
        o, l = items[j]
        px, py = chips[k]
        return pltpu.make_async_remote_copy(ins[j].at[2 * px + py], outs[o].at[k, l], send.at[j, k], recv.at[j, k],
                                            device_id=(px, py, c), device_id_type=_MESH)

    local = [pltpu.make_async_copy(ins[j].at[me], outs[o].at[3, l], loc.at[j]) for j, (o, l) in enumerate(items)]
    sends = [remote(j, k) for j in range(len(items)) for k in range(3)]
    return _Exchange(local, sends, sends)


def _gathered_shapes(shards):
    return [jax.ShapeDtypeStruct((N_SHARD,) + s.shape, s.dtype) for s in shards]


def _slot_shapes(blocks):
    return [jax.ShapeDtypeStruct((N_SHARD, DEPTH) + g.shape[1:], g.dtype) for g in blocks]


def _gather_shards(shards):
    n = len(shards)

    def body(*refs):
        ex = _gather_exchange(refs[:n], refs[n:2 * n], *refs[2 * n:])
        ex.start()
        ex.wait()

    return pl.pallas_call(
        body, in_specs=[_ANY] * n, out_specs=[_ANY] * n, out_shape=_gathered_shapes(shards),
        scratch_shapes=_exchange_sems(n), name="gather_weights",
    )(*shards)


def _scatter_into(items, grads, slots):
    n, ns = len(grads), len(slots)

    def body(*refs):
        ex = _scatter_exchange(items, refs[:n], refs[n + ns:n + 2 * ns], *refs[n + 2 * ns:])
        ex.start()
        ex.wait()

    return pl.pallas_call(
        body, in_specs=[_ANY] * (n + ns), out_specs=[_ANY] * ns,
        out_shape=[jax.ShapeDtypeStruct(s.shape, s.dtype) for s in slots],
        input_output_aliases={n + i: i for i in range(ns)},
        scratch_shapes=_exchange_sems(n), name="scatter_grads",
    )(*grads, *slots)


def _swap_with_sibling(parts):
    n = len(parts)

    def body(*refs):
        ins, outs = refs[:n], refs[n:2 * n]
        send, recv = refs[2 * n:]
        x, y, c = _mesh_pos()
        copies = [pltpu.make_async_remote_copy(ins[i], outs[i], send.at[i], recv.at[i], device_id=(x, y, 1 - c),
                                               device_id_type=_MESH) for i in range(n)]
        for cp in copies:
            cp.start()
        for cp in copies:
            cp.wait_recv()
        for cp in copies:
            cp.wait_send()

    return pl.pallas_call(
        body, in_specs=[_ANY] * n, out_specs=[_ANY] * n,
        out_shape=[jax.ShapeDtypeStruct(p.shape, p.dtype) for p in parts],
        scratch_shapes=[pltpu.SemaphoreType.DMA((n,)), pltpu.SemaphoreType.DMA((n,))],
        name="swap_sibling",
    )(*parts)


N_DEV = 8


def _allreduce_small(packed):
    rows = packed.shape[0]

    def body(in_ref, out_ref, buf, send, recv, loc):
        x, y, c = _mesh_pos()
        me = 4 * x + 2 * y + c
        own = pltpu.make_async_copy(in_ref, buf.at[me], loc)
        own.start()

        def remote(m, block):
            peer = (x ^ (m >> 2), y ^ ((m >> 1) & 1), c ^ (m & 1))
            return pltpu.make_async_remote_copy(in_ref, buf.at[block], send.at[m - 1], recv.at[m - 1],
                                                device_id=peer, device_id_type=_MESH)

        sends = [remote(m, me) for m in range(1, N_DEV)]
        for cp in sends:
            cp.start()
        for m in range(1, N_DEV):
            remote(m, me ^ m).wait_recv()
        for cp in sends:
            cp.wait_send()
        own.wait()
        tot = buf[0]
        for d in range(1, N_DEV):
            tot = tot + buf[d]
        out_ref[...] = tot

    vm = pl.BlockSpec(memory_space=pltpu.VMEM)
    return pl.pallas_call(
        body, in_specs=[vm], out_specs=vm, out_shape=jax.ShapeDtypeStruct((rows, LANES), F32),
        scratch_shapes=[pltpu.VMEM((N_DEV, rows, LANES), F32), pltpu.SemaphoreType.DMA((N_DEV - 1,)),
                        pltpu.SemaphoreType.DMA((N_DEV - 1,)), pltpu.SemaphoreType.DMA(())],
        name="allreduce_small",
    )(packed)


def _shard_rows(r):
    return r // 2 if r % 32 == 0 else r


def _sum_slots(slots):
    _, _, r, cdim = slots.shape
    tr = _shard_rows(r)

    def body(a_ref, b_ref, c_ref, d_ref, o_ref):
        up = lambda ref: ref[...].astype(F32)
        o_ref[...] = ((up(d_ref) + up(a_ref)) + up(b_ref)) + up(c_ref)

    def spec(k):
        return pl.BlockSpec((None, None, tr, cdim), lambda l, i: (k, l, i, 0))

    return pl.pallas_call(
        body, grid=(DEPTH, r // tr), in_specs=[spec(0), spec(1), spec(2), spec(3)],
        out_specs=pl.BlockSpec((None, tr, cdim), lambda l, i: (l, i, 0)),
        out_shape=jax.ShapeDtypeStruct((DEPTH, r, cdim), F32),
        compiler_params=_cparams(("parallel", "parallel")), name="sum_slots",
    )(slots, slots, slots, slots)


def _adamw_math(w, g, m, v):
    m = ADAM_B1 * m + (1.0 - ADAM_B1) * g
    v = ADAM_B2 * v + (1.0 - ADAM_B2) * (g * g)
    m_hat = m / (1.0 - ADAM_B1 ** ADAM_STEP)
    v_hat = v / (1.0 - ADAM_B2 ** ADAM_STEP)
    delta = -ADAM_LR * (m_hat / (jnp.sqrt(v_hat) + ADAM_EPS) + ADAM_WD * w)
    return delta, m, v


def _adamw_big(ga, gb, w, m, v):
    _, r, cdim = w.shape
    tr = _shard_rows(r)

    def body(ga_ref, gb_ref, w_ref, m_ref, v_ref, g_out, d_out, m_out, v_out):
        g = ga_ref[...] + gb_ref[...]
        d, mn, vn = _adamw_math(w_ref[...], g, m_ref[...], v_ref[...])
        g_out[...] = g
        d_out[...] = d
        m_out[...] = mn
        v_out[...] = vn

    spec = pl.BlockSpec((None, tr, cdim), lambda l, i: (l, i, 0))
    shp = jax.ShapeDtypeStruct(w.shape, F32)
    return pl.pallas_call(
        body, grid=(DEPTH, r // tr), in_specs=[spec] * 5, out_specs=[spec] * 4, out_shape=[shp] * 4,
        compiler_params=_cparams(("parallel", "parallel")), name="adamw_big",
    )(ga, gb, w, m, v)


def _adamw_small(ws, gs, ms, vs):
    n = len(ws)

    def body(*refs):
        w_r, g_r, m_r, v_r = (refs[k * n:(k + 1) * n] for k in range(4))
        d_o, m_o, v_o = (refs[(4 + k) * n:(5 + k) * n] for k in range(3))
        for i in range(n):
            d, mn, vn = _adamw_math(w_r[i][...], g_r[i][...], m_r[i][...], v_r[i][...])
            d_o[i][...] = d
            m_o[i][...] = mn
            v_o[i][...] = vn

    vm = pl.BlockSpec(memory_space=pltpu.VMEM)
    shp = [jax.ShapeDtypeStruct(w.shape, F32) for w in ws]
    outs = pl.pallas_call(
        body, in_specs=[vm] * (4 * n), out_specs=[vm] * (3 * n), out_shape=shp * 3, name="adamw_small",
    )(*ws, *gs, *ms, *vs)
    return outs[:n], outs[n:2 * n], outs[2 * n:]


def _tile_rows(a):
    a = a.reshape(-1, LANES)
    pad = (-a.shape[0]) % 8
    return jnp.pad(a, ((0, pad), (0, 0))) if pad else a


_SMALL_LAYER_PARTS = (("qn", 8), ("kn", 8), ("sink", 8), ("ga", 8), ("gb", 8), ("ln1g", 8), ("ln1b", 8),
                      ("ln2g", 8), ("ln2b", 8), ("stats", N_SHARD * 8 * FF_SH // LANES))
_SMALL_HEAD_ROWS = 16
_SMALL_LAYER_ROWS = sum(r for _, r in _SMALL_LAYER_PARTS)


def _pack_small(loss, dbucket, grads):
    parts = [_tile_rows(loss), _tile_rows(dbucket)]
    for l in range(DEPTH):
        parts += [_tile_rows(grads[l][name]) for name, _ in _SMALL_LAYER_PARTS]
    return jnp.concatenate(parts, axis=0)


def _unpack_small(tot, chip):
    out = dict(loss=tot[0, 0], rel_bias=tot[8:16, :N_BUCKETS].T)
    per = {name: [] for name, _ in _SMALL_LAYER_PARTS}
    for l in range(DEPTH):
        base = _SMALL_HEAD_ROWS + l * _SMALL_LAYER_ROWS
        for name, rows in _SMALL_LAYER_PARTS:
            per[name].append(tot[base:base + rows])
            base += rows
    fold = lambda v: v[0, :HEAD_DIM] + v[0, HEAD_DIM:]
    out["q_norm"] = jnp.stack([fold(v) for v in per["qn"]])
    out["k_norm"] = jnp.stack([fold(v) for v in per["kn"]])
    out["sink"] = jnp.stack([jnp.sum(v, axis=1) for v in per["sink"]])
    out["out_norm_a"] = jnp.stack([_from_pairs(v[:4].reshape(Q_W), 0) for v in per["ga"]])
    out["out_norm_b"] = jnp.stack([_from_pairs(v[:4].reshape(Q_W), 0) for v in per["gb"]])
    for name, key in (("ln1_g", "ln1g"), ("ln1_b", "ln1b"), ("ln2_g", "ln2g"), ("ln2_b", "ln2b")):
        out[name] = jnp.stack([v.reshape(D_MODEL) for v in per[key]])
    stats = [v.reshape(N_SHARD, 8, FF_SH) for v in per["stats"]]
    out["conv_b"] = jnp.stack([s[:, 0, :].reshape(D_FF) for s in stats])
    out["conv_w"] = jnp.stack([lax.dynamic_index_in_dim(s, chip, 0, keepdims=False)[1:4] for s in stats])
    return out


_WEIGHTS = ("rel_bias", "w_in", "q_norm", "k_norm", "sink", "out_norm_a", "out_norm_b", "w_out", "ln1_g", "ln1_b",
            "w_gate", "w_up", "conv_w", "conv_b", "w_down", "ln2_g", "ln2_b")
_BIG = ("w_in", "w_out", "w_gate", "w_up", "w_down")
_SMALL = tuple(n for n in _WEIGHTS if n not in _BIG)


def _col_blocks(g, n):
    return g.reshape(g.shape[0], N_SHARD, n).transpose(1, 0, 2)


def kernel(x, rel_bias, w_in, q_norm, k_norm, sink, out_norm_a, out_norm_b, w_out, ln1_g, ln1_b, w_gate, w_up, conv_w, conv_b, w_down, ln2_g, ln2_b, loss_target, m_rel_bias, m_w_in, m_q_norm, m_k_norm, m_sink, m_out_norm_a, m_out_norm_b, m_w_out, m_ln1_g, m_ln1_b, m_w_gate, m_w_up, m_conv_w, m_conv_b, m_w_down, m_ln2_g, m_ln2_b, v_rel_bias, v_w_in, v_q_norm, v_k_norm, v_sink, v_out_norm_a, v_out_norm_b, v_w_out, v_ln1_g, v_ln1_b, v_w_gate, v_w_up, v_conv_w, v_conv_b, v_w_down, v_ln2_g, v_ln2_b):
    w = dict(rel_bias=rel_bias, w_in=w_in, q_norm=q_norm, k_norm=k_norm, sink=sink, out_norm_a=out_norm_a,
             out_norm_b=out_norm_b, w_out=w_out, ln1_g=ln1_g, ln1_b=ln1_b, w_gate=w_gate, w_up=w_up, conv_w=conv_w,
             conv_b=conv_b, w_down=w_down, ln2_g=ln2_g, ln2_b=ln2_b)
    m = dict(rel_bias=m_rel_bias, w_in=m_w_in, q_norm=m_q_norm, k_norm=m_k_norm, sink=m_sink, out_norm_a=m_out_norm_a,
             out_norm_b=m_out_norm_b, w_out=m_w_out, ln1_g=m_ln1_g, ln1_b=m_ln1_b, w_gate=m_w_gate, w_up=m_w_up,
             conv_w=m_conv_w, conv_b=m_conv_b, w_down=m_w_down, ln2_g=m_ln2_g, ln2_b=m_ln2_b)
    v = dict(rel_bias=v_rel_bias, w_in=v_w_in, q_norm=v_q_norm, k_norm=v_k_norm, sink=v_sink, out_norm_a=v_out_norm_a,
             out_norm_b=v_out_norm_b, w_out=v_w_out, ln1_g=v_ln1_g, ln1_b=v_ln1_b, w_gate=v_w_gate, w_up=v_w_up,
             conv_w=v_conv_w, conv_b=v_conv_b, w_down=v_w_down, ln2_g=v_ln2_g, ln2_b=v_ln2_b)
    chip = 2 * lax.axis_index("x") + lax.axis_index("y")

    small_w = (q_norm, k_norm, sink, out_norm_a, out_norm_b, conv_b, ln1_g, ln1_b, ln2_g, ln2_b)
    (win0,) = _gather_shards([w_in[0].astype(BF16)])
    later = ([w[name][0].astype(BF16) for name in _BIG[1:]] + [w[name][1].astype(BF16) for name in _BIG] + [conv_w])
    params = [_prep_layer_params(0, win0, None, None, None, None, None, *small_w), None]

    def finish(g):
        wout0, wg0, wu0, wd0, win1, wout1, wg1, wu1, wd1, cw_all = g
        params[0] = _prep_layer_params(0, win0, wout0, wg0, wu0, wd0, cw_all[:, 0], *small_w)
        params[1] = _prep_layer_params(1, win1, wout1, wg1, wu1, wd1, cw_all[:, 1], *small_w)
        return params[0]

    loss, dx, grads, dbucket = _local_step(x[0], loss_target[0], params, rel_bias, gather=(later, finish),
                                           scatter=True)

    small = _unpack_small(_allreduce_small(_pack_small(loss, dbucket, grads)), chip)

    slots = list(grads[0]["slots"])
    slots[0] = _scatter_into([(0, 0)], _block_grads(grads[0], ("w_in",)), [slots[0]])[0]
    partial = [_sum_slots(s) for s in slots]
    other = _swap_with_sibling(partial)

    grad, delta, new_m, new_v = {}, {}, {}, {}
    for i, name in enumerate(_BIG):
        fix = (lambda a: jnp.swapaxes(a, 1, 2)) if name in ("w_gate", "w_up") else (lambda a: a)
        outs = _adamw_big(partial[i], other[i], fix(w[name]), fix(m[name]), fix(v[name]))
        grad[name], delta[name], new_m[name], new_v[name] = [fix(o) for o in outs]
    flat2 = lambda a: a.reshape(-1, a.shape[-1])
    ds, ms, vs = _adamw_small([flat2(w[n]) for n in _SMALL], [flat2(small[n]) for n in _SMALL],
                              [flat2(m[n]) for n in _SMALL], [flat2(v[n]) for n in _SMALL])
    for i, name in enumerate(_SMALL):
        grad[name] = small[name]
        delta[name] = ds[i].reshape(w[name].shape)
        new_m[name] = ms[i].reshape(w[name].shape)
        new_v[name] = vs[i].reshape(w[name].shape)

    return (small["loss"], dx[None], *[grad[n] for n in _WEIGHTS], *[delta[n] for n in _WEIGHTS],
            *[new_m[n] for n in _WEIGHTS], *[new_v[n] for n in _WEIGHTS])
```

```python
import math

import jax
import jax.numpy as jnp
from jax import lax
from jax.experimental import pallas as pl
from jax.experimental.pallas import tpu as pltpu

F32 = jnp.float32
BF16 = jnp.bfloat16

D_MODEL = 1024
DEPTH = 2
HEAD_DIM = 64
Q_W = 512
KV_W = 128
IN_COLS = 2 * (Q_W + 2 * KV_W)
N_SHARD = 4
IN_SH = IN_COLS // N_SHARD
OUT_SH = D_MODEL // N_SHARD
D_FF = 2816
FF_SH = D_FF // N_SHARD
Q_BLOCK = 128
WINDOW = 128
N_BUCKETS = 32
MAX_DISTANCE = 128
GRID_W = 64
ROPE_THETA = 10000.0
ALPHA = (2.0 * DEPTH) ** 0.25
RMS_EPS = 1e-6
LN_EPS = 1e-5
NEG = -1e30
LANES = 128
VMEM_LIMIT = 56 * 1024 * 1024

ADAM_LR = 0.001
ADAM_B1 = 0.9
ADAM_B2 = 0.999
ADAM_EPS = 1e-08
ADAM_WD = 0.01
ADAM_STEP = 10

_NN = (((1,), (0,)), ((), ()))
_NT = (((1,), (1,)), ((), ()))
_TN = (((0,), (0,)), ((), ()))


def _dot(a, b, dims):
    return lax.dot_general(a.astype(BF16), b.astype(BF16), dims, preferred_element_type=F32)


def _cparams(sem, vmem=VMEM_LIMIT):
    return pltpu.CompilerParams(dimension_semantics=sem, vmem_limit_bytes=vmem)


def _regroup(a, axis, n_outer, n_inner):
    shp = a.shape
    a = a.reshape(shp[:axis] + (n_outer, n_inner, HEAD_DIM) + shp[axis + 1:])
    return jnp.swapaxes(a, axis, axis + 1).reshape(shp)


def _to_pairs(a, axis):
    return _regroup(a, axis, 2, 4)


def _from_pairs(a, axis):
    return _regroup(a, axis, 4, 2)


def _in_cols_to_pairs(w, fn=_to_pairs):
    return jnp.concatenate([fn(w[..., :Q_W], w.ndim - 1), w[..., Q_W:Q_W + 2 * KV_W],
                            fn(w[..., Q_W + 2 * KV_W:2 * Q_W + 2 * KV_W], w.ndim - 1),
                            w[..., 2 * Q_W + 2 * KV_W:]], axis=-1)


def _mix_rows_to_pairs(w, fn=_to_pairs):
    return fn(w.reshape(2, Q_W, w.shape[-1]), 1).reshape(w.shape)


def _matmul(a, b, *, dims, grid, a_spec, b_spec, o_spec, out_shape, acc_shape, name, res=None,
            res_spec=None, res_scale=1.0):
    nk = grid[-1]
    kax = len(grid) - 1

    def body(*refs):
        if res is None:
            a_ref, b_ref, o_ref, acc = refs
            r_ref = None
        else:
            a_ref, b_ref, r_ref, o_ref, acc = refs
        k = pl.program_id(kax)

        @pl.when(k == 0)
        def _():
            acc[...] = jnp.zeros_like(acc)

        acc[...] += _dot(a_ref[...], b_ref[...], dims)

        @pl.when(k == nk - 1)
        def _():
            o = acc[...]
            if r_ref is not None:
                o = o + res_scale * r_ref[...]
            o_ref[...] = o.astype(o_ref.dtype)

    in_specs = [a_spec, b_spec] + ([res_spec] if res is not None else [])
    args = (a, b) + ((res,) if res is not None else ())
    sem = ("parallel",) * kax + ("arbitrary",)
    return pl.pallas_call(
        body, grid=grid, in_specs=in_specs, out_specs=o_spec, out_shape=out_shape,
        scratch_shapes=[pltpu.VMEM(acc_shape, F32)], compiler_params=_cparams(sem), name=name,
    )(*args)


def _mm_nt(a, b, out_dtype, name, tm=512, res=None, res_scale=1.0):
    m, kd = a.shape
    n = b.shape[0]
    return _matmul(
        a, b, dims=_NT, grid=(m // tm, 1),
        a_spec=pl.BlockSpec((tm, kd), lambda i, k: (i, 0)),
        b_spec=pl.BlockSpec((n, kd), lambda i, k: (0, 0)),
        o_spec=pl.BlockSpec((tm, n), lambda i, k: (i, 0)),
        out_shape=jax.ShapeDtypeStruct((m, n), out_dtype), acc_shape=(tm, n), name=name,
        res=res, res_spec=pl.BlockSpec((tm, n), lambda i, k: (i, 0)), res_scale=res_scale)


def _mm_tn(a, b, name, tk=1024, tn=None, out_dtype=BF16):
    t, m = a.shape
    n = b.shape[1]
    tn = n if tn is None else tn
    tk = min(tk, t)
    return _matmul(
        a, b, dims=_TN, grid=(n // tn, t // tk),
        a_spec=pl.BlockSpec((tk, m), lambda j, k: (k, 0)),
        b_spec=pl.BlockSpec((tk, tn), lambda j, k: (k, j)),
        o_spec=pl.BlockSpec((m, tn), lambda j, k: (0, j)),
        out_shape=jax.ShapeDtypeStruct((m, n), out_dtype), acc_shape=(m, tn), name=name)


def _blocked_n(w, dims):
    return w.shape[2] if dims == _NN else w.shape[1]


def _mm_reduce(a, w, dims, out_dtype, name, tm=512, res=None, res_scale=1.0, ln=None):
    nb, m, kd = a.shape
    n = _blocked_n(w, dims)
    n_in = 2 + (res is not None) + (2 if ln else 0)

    def body(*refs):
        a_ref, w_ref = refs[0], refs[1]
        acc = _dot(a_ref[0], w_ref[0], dims)
        for j in range(1, nb):
            acc = acc + _dot(a_ref[j], w_ref[j], dims)
        if res is not None:
            acc = acc + res_scale * refs[2][...]
        refs[n_in][...] = acc.astype(out_dtype)
        if ln:
            g_ref, b_ref = refs[n_in - 2], refs[n_in - 1]
            zc = acc - jnp.mean(acc, axis=-1, keepdims=True)
            r = lax.rsqrt(jnp.mean(zc * zc, axis=-1, keepdims=True) + LN_EPS)
            y = zc * r * g_ref[...] + b_ref[...]
            refs[n_in + 1][...] = y
            refs[n_in + 2][...] = y.astype(BF16)

    row = pl.BlockSpec((tm, n), lambda i: (i, 0))
    par = pl.BlockSpec((1, n), lambda i: (0, 0))
    sd = jax.ShapeDtypeStruct
    out = pl.pallas_call(
        body, grid=(m // tm,),
        in_specs=[pl.BlockSpec((nb, tm, kd), lambda i: (0, i, 0)), pl.BlockSpec(w.shape, lambda i: (0, 0, 0))]
        + ([row] if res is not None else []) + ([par, par] if ln else []),
        out_specs=[row] * (3 if ln else 1),
        out_shape=[sd((m, n), out_dtype)] + ([sd((m, n), F32), sd((m, n), BF16)] if ln else []),
        compiler_params=_cparams(("parallel",)), name=name,
    )(a, w, *((res,) if res is not None else ()), *(ln or ()))
    return out if ln else out[0]


def _mm_tn_blocks(a, b, name, blk=0, nb=N_SHARD, tk=1024, out_dtype=BF16):
    a3, b3 = a.ndim == 3, b.ndim == 3
    t, m, n = a.shape[-2], a.shape[-1], b.shape[-1]
    tk = min(tk, t)
    nsteps = t // tk

    def spec(blocked, width):
        if blocked:
            return pl.BlockSpec((nb, tk, width), lambda k: (blk, k, 0))
        return pl.BlockSpec((tk, width), lambda k: (k, 0))

    def body(a_ref, b_ref, o_ref, acc):
        k = pl.program_id(0)

        @pl.when(k == 0)
        def _():
            acc[...] = jnp.zeros_like(acc)

        for j in range(nb):
            acc[j] += _dot(a_ref[j] if a3 else a_ref[...], b_ref[j] if b3 else b_ref[...], _TN)

        @pl.when(k == nsteps - 1)
        def _():
            o_ref[...] = acc[...].astype(o_ref.dtype)

    return pl.pallas_call(
        body, grid=(nsteps,), in_specs=[spec(a3, m), spec(b3, n)],
        out_specs=pl.BlockSpec((nb, m, n), lambda k: (0, 0, 0)),
        out_shape=jax.ShapeDtypeStruct((nb, m, n), out_dtype),
        scratch_shapes=[pltpu.VMEM((nb, m, n), F32)],
        compiler_params=_cparams(("arbitrary",)), name=name,
    )(a, b)


def _row_spec(tm, n):
    return pl.BlockSpec((tm, n), lambda i: (i, 0))


def _par_spec(n, rows=1):
    return pl.BlockSpec((rows, n), lambda i: (0, 0))


def _swap_pairs(x):
    lane = lax.broadcasted_iota(jnp.int32, x.shape, 1)
    return jnp.where(lane % 2 == 0, pltpu.roll(x, LANES - 1, 1), pltpu.roll(x, 1, 1))


def _head_sums(v):
    lo = lax.broadcasted_iota(jnp.int32, v.shape, 1) < HEAD_DIM
    s_lo = jnp.sum(jnp.where(lo, v, 0.0), axis=-1, keepdims=True)
    s_hi = jnp.sum(jnp.where(lo, 0.0, v), axis=-1, keepdims=True)
    return jnp.where(lo, s_lo, s_hi)


def _qk_blocks():
    return [(128 * i, True) for i in range(4)] + [(Q_W, False)]


def _in_proj_prep(x, win, cos_t, sin_t, qn, kn, tm=512):
    t, kd = x.shape
    scale = HEAD_DIM ** -0.5
    wide = 2 * LANES

    def body(x_ref, w_ref, c_ref, s_ref, qn_ref, kn_ref, h_ref, qa_ref, ka_ref, va_ref, qb_ref, kb_ref, vb_ref):
        xb = x_ref[...].astype(BF16)
        c = c_ref[...]
        s = s_ref[...]

        def normed(v, gain):
            r = lax.rsqrt(_head_sums(v * v) * (1.0 / HEAD_DIM) + RMS_EPS)
            y = v * r * gain
            return y * c + _swap_pairs(y) * s

        for b in range(IN_COLS // wide):
            hb = _dot(xb, w_ref[:, wide * b:wide * (b + 1)], _NN)
            h_ref[:, wide * b:wide * (b + 1)] = hb
            lo, hi = hb[:, :LANES], hb[:, LANES:]
            if b < 2:
                qa_ref[:, wide * b:wide * b + LANES] = (normed(lo, qn_ref[...]) * scale).astype(BF16)
                qa_ref[:, wide * b + LANES:wide * (b + 1)] = (normed(hi, qn_ref[...]) * scale).astype(BF16)
            elif b == 2:
                ka_ref[...] = normed(lo, kn_ref[...]).astype(BF16)
                va_ref[...] = hi.astype(BF16)
            elif b < 5:
                qb_ref[:, wide * (b - 3):wide * (b - 2)] = (hb * scale).astype(BF16)
            else:
                kb_ref[...] = lo.astype(BF16)
                vb_ref[...] = hi.astype(BF16)

    sd = jax.ShapeDtypeStruct
    return pl.pallas_call(
        body, grid=(t // tm,),
        in_specs=[_row_spec(tm, kd), pl.BlockSpec(win.shape, lambda i: (0, 0)), _row_spec(tm, LANES),
                  _row_spec(tm, LANES), _par_spec(LANES), _par_spec(LANES)],
        out_specs=[_row_spec(tm, IN_COLS), _row_spec(tm, Q_W), _row_spec(tm, KV_W), _row_spec(tm, KV_W),
                   _row_spec(tm, Q_W), _row_spec(tm, KV_W), _row_spec(tm, KV_W)],
        out_shape=[sd((t, IN_COLS), F32), sd((t, Q_W), BF16), sd((t, KV_W), BF16), sd((t, KV_W), BF16),
                   sd((t, Q_W), BF16), sd((t, KV_W), BF16), sd((t, KV_W), BF16)],
        compiler_params=_cparams(("parallel",)), name="in_proj_prep",
    )(x, win, cos_t, sin_t, qn, kn)


def _prep_bwd(h, cos_t, sin_t, qn, kn, dqa, dka, dva, dqb, dkb, dvb, tm=256):
    t = h.shape[0]
    scale = HEAD_DIM ** -0.5

    def body(h_ref, c_ref, s_ref, qn_ref, kn_ref, dqa_ref, dka_ref, dva_ref, dqb_ref, dkb_ref, dvb_ref,
             dh_ref, dqn_ref, dkn_ref):
        @pl.when(pl.program_id(0) == 0)
        def _():
            dqn_ref[...] = jnp.zeros_like(dqn_ref)
            dkn_ref[...] = jnp.zeros_like(dkn_ref)

        c = c_ref[...]
        s = s_ref[...]
        for start, is_q in _qk_blocks():
            x = h_ref[:, start:start + LANES]
            gain = qn_ref[...] if is_q else kn_ref[...]
            d = dqa_ref[:, start:start + LANES] * scale if is_q else dka_ref[...]
            dy = d * c + _swap_pairs(d * s)
            r = lax.rsqrt(_head_sums(x * x) * (1.0 / HEAD_DIM) + RMS_EPS)
            xr = x * r
            gsum = jnp.sum(dy * xr, axis=0, keepdims=True)
            if is_q:
                dqn_ref[...] += gsum
            else:
                dkn_ref[...] += gsum
            gy = dy * gain
            dx = r * (gy - xr * (_head_sums(xr * gy) * (1.0 / HEAD_DIM)))
            dh_ref[:, start:start + LANES] = dx.astype(BF16)
        dh_ref[:, 640:768] = dva_ref[...].astype(BF16)
        dh_ref[:, 768:1280] = (dqb_ref[...] * scale).astype(BF16)
        dh_ref[:, 1280:1408] = dkb_ref[...].astype(BF16)
        dh_ref[:, 1408:1536] = dvb_ref[...].astype(BF16)

    sd = jax.ShapeDtypeStruct
    return pl.pallas_call(
        body, grid=(t // tm,),
        in_specs=[_row_spec(tm, IN_COLS), _row_spec(tm, LANES), _row_spec(tm, LANES), _par_spec(LANES), _par_spec(LANES),
                  _row_spec(tm, Q_W), _row_spec(tm, KV_W), _row_spec(tm, KV_W),
                  _row_spec(tm, Q_W), _row_spec(tm, KV_W), _row_spec(tm, KV_W)],
        out_specs=[_row_spec(tm, IN_COLS), _par_spec(LANES), _par_spec(LANES)],
        out_shape=[sd((t, IN_COLS), BF16), sd((1, LANES), F32), sd((1, LANES), F32)],
        compiler_params=_cparams(("arbitrary",)), name="prep_bwd",
    )(h, cos_t, sin_t, qn, kn, dqa, dka, dva, dqb, dkb, dvb)


def _outnorm_fwd(oa, ob, ga, gb, tm=512):
    t = oa.shape[0]

    def body(oa_ref, ob_ref, ga_ref, gb_ref, y_ref):
        for o_ref, g_ref, start in ((oa_ref, ga_ref, 0), (ob_ref, gb_ref, Q_W)):
            x = o_ref[...]
            r = lax.rsqrt(jnp.mean(x * x, axis=-1, keepdims=True) + RMS_EPS)
            y_ref[:, start:start + Q_W] = (x * r * g_ref[...]).astype(BF16)

    return pl.pallas_call(
        body, grid=(t // tm,),
        in_specs=[_row_spec(tm, Q_W), _row_spec(tm, Q_W), _par_spec(Q_W), _par_spec(Q_W)],
        out_specs=_row_spec(tm, D_MODEL), out_shape=jax.ShapeDtypeStruct((t, D_MODEL), BF16),
        compiler_params=_cparams(("parallel",)), name="outnorm_fwd",
    )(oa, ob, ga, gb)


def _outnorm_bwd(dy, oa, ob, ga, gb, tm=512):
    t = oa.shape[0]

    def body(dy_ref, oa_ref, ob_ref, ga_ref, gb_ref, doa_ref, dob_ref, dga_ref, dgb_ref):
        @pl.when(pl.program_id(0) == 0)
        def _():
            dga_ref[...] = jnp.zeros_like(dga_ref)
            dgb_ref[...] = jnp.zeros_like(dgb_ref)

        for o_ref, g_ref, do_ref, dg_ref, start in ((oa_ref, ga_ref, doa_ref, dga_ref, 0),
                                                    (ob_ref, gb_ref, dob_ref, dgb_ref, Q_W)):
            x = o_ref[...]
            d = dy_ref[:, start:start + Q_W]
            r = lax.rsqrt(jnp.mean(x * x, axis=-1, keepdims=True) + RMS_EPS)
            xr = x * r
            dg_ref[...] += jnp.sum(d * xr, axis=0, keepdims=True)
            gy = d * g_ref[...]
            do_ref[...] = r * (gy - xr * jnp.mean(xr * gy, axis=-1, keepdims=True))

    sd = jax.ShapeDtypeStruct
    return pl.pallas_call(
        body, grid=(t // tm,),
        in_specs=[_row_spec(tm, D_MODEL), _row_spec(tm, Q_W), _row_spec(tm, Q_W), _par_spec(Q_W), _par_spec(Q_W)],
        out_specs=[_row_spec(tm, Q_W), _row_spec(tm, Q_W), _par_spec(Q_W), _par_spec(Q_W)],
        out_shape=[sd((t, Q_W), F32), sd((t, Q_W), F32), sd((1, Q_W), F32), sd((1, Q_W), F32)],
        compiler_params=_cparams(("arbitrary",)), name="outnorm_bwd",
    )(dy, oa, ob, ga, gb)


def _ln_bwd(d, z, g, tm=512):
    t = z.shape[0]

    def body(d_ref, z_ref, g_ref, dz_ref, dzb_ref, dg_ref, db_ref):
        @pl.when(pl.program_id(0) == 0)
        def _():
            dg_ref[...] = jnp.zeros_like(dg_ref)
            db_ref[...] = jnp.zeros_like(db_ref)

        zz = z_ref[...]
        dd = d_ref[...]
        mu = jnp.mean(zz, axis=-1, keepdims=True)
        zc = zz - mu
        r = lax.rsqrt(jnp.mean(zc * zc, axis=-1, keepdims=True) + LN_EPS)
        xh = zc * r
        dg_ref[...] += jnp.sum(dd * xh, axis=0, keepdims=True)
        db_ref[...] += jnp.sum(dd, axis=0, keepdims=True)
        dxh = dd * g_ref[...]
        dz = r * (dxh - jnp.mean(dxh, axis=-1, keepdims=True) - xh * jnp.mean(dxh * xh, axis=-1, keepdims=True))
        dz_ref[...] = dz
        dzb_ref[...] = dz.astype(BF16)

    sd = jax.ShapeDtypeStruct
    return pl.pallas_call(
        body, grid=(t // tm,),
        in_specs=[_row_spec(tm, D_MODEL), _row_spec(tm, D_MODEL), _par_spec(D_MODEL)],
        out_specs=[_row_spec(tm, D_MODEL), _row_spec(tm, D_MODEL), _par_spec(D_MODEL), _par_spec(D_MODEL)],
        out_shape=[sd((t, D_MODEL), F32), sd((t, D_MODEL), BF16), sd((1, D_MODEL), F32), sd((1, D_MODEL), F32)],
        compiler_params=_cparams(("arbitrary",)), name="ln_bwd",
    )(d, z, g)


def _loss_grad(y, tgt, tm=512):
    t = y.shape[0]
    nsteps = t // tm

    def body(y_ref, t_ref, dy_ref, loss_ref, acc):
        i = pl.program_id(0)

        @pl.when(i == 0)
        def _():
            acc[...] = jnp.zeros_like(acc)

        e = y_ref[...] - t_ref[...]
        dy_ref[...] = e * (1.0 / D_MODEL)
        acc[...] += jnp.sum(e * e, axis=0, keepdims=True)

        @pl.when(i == nsteps - 1)
        def _():
            tot = jnp.sum(acc[...], axis=-1, keepdims=True) * (0.5 / D_MODEL)
            loss_ref[...] = jnp.broadcast_to(tot, loss_ref.shape)

    sd = jax.ShapeDtypeStruct
    return pl.pallas_call(
        body, grid=(nsteps,),
        in_specs=[_row_spec(tm, D_MODEL), _row_spec(tm, D_MODEL)],
        out_specs=[_row_spec(tm, D_MODEL), _par_spec(LANES)],
        out_shape=[sd((t, D_MODEL), F32), sd((1, LANES), F32)],
        scratch_shapes=[pltpu.VMEM((1, D_MODEL), F32)],
        compiler_params=_cparams(("arbitrary",)), name="loss_grad",
    )(y, tgt)


_GELU_C = math.sqrt(2.0 / math.pi)
_GELU_K = 0.044715
HALO = 16


def _gelu_parts(x):
    x2 = x * x
    th = jnp.tanh(x * (_GELU_C + (_GELU_C * _GELU_K) * x2))
    a = 0.5 + 0.5 * th
    dact = a + (0.5 * x) * (1.0 - th * th) * (_GELU_C + (3.0 * _GELU_C * _GELU_K) * x2)
    return x * a, dact


def _halo_specs(tm, t, shift=0):
    last = t // HALO - 1
    cur = pl.BlockSpec((None, tm, FF_SH), lambda j, i: (j + shift, i, 0))
    prev = pl.BlockSpec((None, HALO, FF_SH), lambda j, i: (j + shift, jnp.maximum(i * (tm // HALO) - 1, 0), 0))
    nxt = pl.BlockSpec((None, HALO, FF_SH), lambda j, i: (j + shift, jnp.minimum((i + 1) * (tm // HALO), last), 0))
    return [prev, cur, nxt]


def _gate_up_glu(x, wgu, cw, tm=512):
    t, kd = x.shape
    nsteps = t // tm
    last = t // HALO - 1

    def body(xp_ref, x_ref, xn_ref, w_ref, cw_ref, gu_ref, h_ref):
        i = pl.program_id(0)
        xc = x_ref[...]
        xp = jnp.where(i == 0, jnp.zeros_like(xp_ref[...]), xp_ref[...])
        xn = jnp.where(i == nsteps - 1, jnp.zeros_like(xn_ref[...]), xn_ref[...])
        xe = jnp.concatenate([xp, xc, xn], axis=0)
        te = tm + 2 * HALO
        mid = slice(HALO, HALO + tm)
        for j in range(N_SHARD):
            ge = _dot(xe, w_ref[j], _NN).astype(BF16)
            u = _dot(xc, w_ref[j + N_SHARD], _NN).astype(BF16)
            gu_ref[j] = ge[mid]
            gu_ref[j + N_SHARD] = u
            gf = ge.astype(F32)
            cwj = cw_ref[j]
            gc = (cwj[3:4, :] + pltpu.roll(gf, 1, 0) * cwj[0:1, :] + gf * cwj[1:2, :]
                  + pltpu.roll(gf, te - 1, 0) * cwj[2:3, :])
            act, _ = _gelu_parts(gc[mid])
            h_ref[j] = (act * u.astype(F32)).astype(BF16)

    sd = jax.ShapeDtypeStruct
    return pl.pallas_call(
        body, grid=(nsteps,),
        in_specs=[pl.BlockSpec((HALO, kd), lambda i: (jnp.maximum(i * (tm // HALO) - 1, 0), 0)),
                  pl.BlockSpec((tm, kd), lambda i: (i, 0)),
                  pl.BlockSpec((HALO, kd), lambda i: (jnp.minimum((i + 1) * (tm // HALO), last), 0)),
                  pl.BlockSpec(wgu.shape, lambda i: (0, 0, 0)), pl.BlockSpec(cw.shape, lambda i: (0, 0, 0))],
        out_specs=[pl.BlockSpec((2 * N_SHARD, tm, FF_SH), lambda i: (0, i, 0)),
                   pl.BlockSpec((N_SHARD, tm, FF_SH), lambda i: (0, i, 0))],
        out_shape=[sd((2 * N_SHARD, t, FF_SH), BF16), sd((N_SHARD, t, FF_SH), BF16)],
        compiler_params=_cparams(("parallel",)), name="gate_up_glu",
    )(x, x, x, wgu, cw)


def _ffn_mid_bwd(gu, dz, wd, cw, tm=1024):
    t = gu.shape[1]
    tm = min(tm, t)
    nsteps = t // tm
    te = tm + 2 * HALO
    kd = dz.shape[1]

    def body(gp_ref, g_ref, gn_ref, up_ref, u_ref, un_ref, dp_ref, d_ref, dn_ref, wd_ref, cw_ref, dgu_ref, st_ref):
        i = pl.program_id(1)

        @pl.when(i == 0)
        def _():
            st_ref[...] = jnp.zeros_like(st_ref)

        def ext(p_ref, c_ref, n_ref, dtype=F32):
            prev = jnp.where(i == 0, jnp.zeros_like(p_ref[...]), p_ref[...]).astype(dtype)
            nxt = jnp.where(i == nsteps - 1, jnp.zeros_like(n_ref[...]), n_ref[...]).astype(dtype)
            return jnp.concatenate([prev, c_ref[...].astype(dtype), nxt], axis=0)

        eg = ext(gp_ref, g_ref, gn_ref)
        eu = ext(up_ref, u_ref, un_ref)
        ed = _dot(ext(dp_ref, d_ref, dn_ref, BF16), wd_ref[...], _NT)
        w0, w1, w2 = cw_ref[0:1, :], cw_ref[1:2, :], cw_ref[2:3, :]
        g_m1 = pltpu.roll(eg, 1, 0)
        g_p1 = pltpu.roll(eg, te - 1, 0)
        gc = cw_ref[3:4, :] + g_m1 * w0 + eg * w1 + g_p1 * w2
        act, dact = _gelu_parts(gc)
        dgc = ed * eu * dact
        dg = pltpu.roll(dgc, te - 1, 0) * w0 + dgc * w1 + pltpu.roll(dgc, 1, 0) * w2
        mid = slice(HALO, HALO + tm)
        dgu_ref[0] = dg[mid].astype(BF16)
        dgu_ref[1] = (ed * act)[mid].astype(BF16)
        sel = dgc[mid]
        parts = [jnp.sum(sel, axis=0, keepdims=True),
                 jnp.sum(sel * g_m1[mid], axis=0, keepdims=True),
                 jnp.sum(sel * eg[mid], axis=0, keepdims=True),
                 jnp.sum(sel * g_p1[mid], axis=0, keepdims=True)]
        r8 = lax.broadcasted_iota(jnp.int32, (8, FF_SH), 0)
        upd = jnp.zeros((8, FF_SH), F32)
        for k, p in enumerate(parts):
            upd = upd + jnp.where(r8 == k, p, 0.0)
        st_ref[...] += upd

    sd = jax.ShapeDtypeStruct
    last = t // HALO - 1
    dz_specs = [pl.BlockSpec((HALO, kd), lambda j, i: (jnp.maximum(i * (tm // HALO) - 1, 0), 0)),
                pl.BlockSpec((tm, kd), lambda j, i: (i, 0)),
                pl.BlockSpec((HALO, kd), lambda j, i: (jnp.minimum((i + 1) * (tm // HALO), last), 0))]
    return pl.pallas_call(
        body, grid=(N_SHARD, nsteps),
        in_specs=_halo_specs(tm, t) + _halo_specs(tm, t, N_SHARD) + dz_specs
        + [pl.BlockSpec((None, FF_SH, kd), lambda j, i: (j, 0, 0)), pl.BlockSpec((None, 8, FF_SH), lambda j, i: (j, 0, 0))],
        out_specs=[pl.BlockSpec((2, None, tm, FF_SH), lambda j, i: (0, j, i, 0)),
                   pl.BlockSpec((None, 8, FF_SH), lambda j, i: (j, 0, 0))],
        out_shape=[sd((2, N_SHARD, t, FF_SH), BF16), sd((N_SHARD, 8, FF_SH), F32)],
        compiler_params=_cparams(("parallel", "arbitrary")), name="ffn_mid_bwd",
    )(gu, gu, gu, gu, gu, gu, dz, dz, dz, wd, cw)


def _stack_heads(src_ref, dst_ref, tq):
    lo = lax.broadcasted_iota(jnp.int32, (tq, LANES), 1) < HEAD_DIM
    for i in range(4):
        blk = src_ref[:, LANES * i:LANES * (i + 1)].astype(dst_ref.dtype)
        zero = jnp.zeros_like(blk)
        dst_ref[tq * i:tq * (i + 1), :] = jnp.where(lo, blk, zero)
        dst_ref[tq * (4 + i):tq * (5 + i), :] = jnp.where(lo, zero, blk)


def _gattn_fwd(q, k, v, gather=(), tq=128, tk=2048):
    t = q.shape[0]
    tk = min(tk, t)
    nq, nk, r = t // tq, t // tk, 8 * tq
    ng = len(gather)

    def body(*refs):
        q_ref, k_ref, v_ref = refs[:3]
        o_ref, lse_ref = refs[3 + ng:5 + ng]
        qst, m_s, l_s, acct = refs[5 + 2 * ng:9 + 2 * ng]
        if ng:
            ex = _gather_exchange(refs[3:3 + ng], refs[5 + ng:5 + 2 * ng], *refs[9 + 2 * ng:])
            pl.when(pl.program_id(0) == 0)(ex.start)
        lo_rows = lax.broadcasted_iota(jnp.int32, (LANES, tq), 0) < HEAD_DIM
        for i in range(4):
            bt = q_ref[:, LANES * i:LANES * (i + 1)].astype(F32).T
            qst[:, tq * i:tq * (i + 1)] = jnp.where(lo_rows, bt, 0.0).astype(BF16)
            qst[:, tq * (4 + i):tq * (5 + i)] = jnp.where(lo_rows, 0.0, bt).astype(BF16)
        m_s[...] = jnp.full_like(m_s, NEG)

        def max_step(j, carry):
            off = pl.multiple_of(j * tk, tk)
            st = _dot(k_ref[pl.ds(off, tk), :], qst[...], _NN)
            m_s[...] = jnp.maximum(m_s[...], jnp.max(st.reshape(tk // 8, 8, r), axis=0))
            return carry

        lax.fori_loop(0, nk, max_step, 0, unroll=2)
        m_row = jnp.max(m_s[...], axis=0, keepdims=True)
        l_s[...] = jnp.zeros_like(l_s)
        acct[...] = jnp.zeros_like(acct)

        def sum_step(j, carry):
            off = pl.multiple_of(j * tk, tk)
            st = _dot(k_ref[pl.ds(off, tk), :], qst[...], _NN)
            pt = jnp.exp(st - m_row)
            l_s[...] += jnp.sum(pt.reshape(tk // 8, 8, r), axis=0)
            acct[...] += _dot(v_ref[j], pt, _NN)
            return carry

        lax.fori_loop(0, nk, sum_step, 0, unroll=2)
        l_row = jnp.sum(l_s[...], axis=0, keepdims=True)
        ot = acct[...] / l_row
        for i in range(4):
            pair_t = jnp.where(lo_rows, ot[:, tq * i:tq * (i + 1)], ot[:, tq * (4 + i):tq * (5 + i)])
            o_ref[:, LANES * i:LANES * (i + 1)] = pair_t.T
        lse_ref[...] = m_row + jnp.log(l_row)
        if ng:
            pl.when(pl.program_id(0) == nq - 1)(ex.wait)

    sd = jax.ShapeDtypeStruct
    vt3 = v.reshape(nk, tk, KV_W).transpose(0, 2, 1)
    return pl.pallas_call(
        body, grid=(nq,),
        in_specs=[_row_spec(tq, Q_W), _par_spec(KV_W, t), pl.BlockSpec((nk, KV_W, tk), lambda i: (0, 0, 0))]
        + [_ANY] * ng,
        out_specs=[_row_spec(tq, Q_W), pl.BlockSpec((None, 1, r), lambda i: (i, 0, 0))] + [_ANY] * ng,
        out_shape=[sd((t, Q_W), F32), sd((nq, 1, r), F32)] + _gathered_shapes(gather),
        scratch_shapes=[pltpu.VMEM((LANES, r), BF16), pltpu.VMEM((8, r), F32), pltpu.VMEM((8, r), F32),
                        pltpu.VMEM((LANES, r), F32)] + (_exchange_sems(ng) if ng else []),
        compiler_params=_cparams(("arbitrary",) if ng else ("parallel",)),
        name="gattn_fwd_gather" if ng else "gattn_fwd",
    )(q, k, vt3, *gather)


def _gattn_bwd(q, k, v, o, do, lse, scatter=None, tq=128, tk=512):
    t = q.shape[0]
    tk = min(tk, t)
    nq, nk, r = t // tq, t // tk, 8 * tq
    items, sgrads = scatter if scatter else ((), ())
    ns = len(sgrads)
    slot_shapes = []
    for j, (o_idx, _) in enumerate(items):
        if o_idx == len(slot_shapes):
            slot_shapes += _slot_shapes([sgrads[j]])
    nslots = len(slot_shapes)

    n_in, n_scr = 7, 6
    kt3 = k.reshape(nk, tk, KV_W).transpose(0, 2, 1)

    def body(*refs):
        q_ref, k_ref, v_ref, kt_ref, o_ref, do_ref, lse_ref = refs[:n_in]
        dq_ref, dk_ref, dv_ref = refs[n_in + ns:n_in + 3 + ns]
        scr = n_in + 3 + ns + nslots
        qs, dos, qst, dost, dlt_row, dqt = refs[scr:scr + n_scr]
        if ns:
            ex = _scatter_exchange(items, refs[n_in:n_in + ns], refs[n_in + 3 + ns:scr], *refs[scr + n_scr:])
            pl.when(pl.program_id(0) == 0)(ex.start)

        @pl.when(pl.program_id(0) == 0)
        def _():
            dk_ref[...] = jnp.zeros_like(dk_ref)
            dv_ref[...] = jnp.zeros_like(dv_ref)

        _stack_heads(q_ref, qs, tq)
        _stack_heads(do_ref, dos, tq)
        lo_rows = lax.broadcasted_iota(jnp.int32, (LANES, tq), 0) < HEAD_DIM
        for i in range(4):
            lo, hi = slice(tq * i, tq * (i + 1)), slice(tq * (4 + i), tq * (5 + i))
            cols = slice(LANES * i, LANES * (i + 1))
            for src, dst in ((q_ref, qst), (do_ref, dost)):
                bt = src[:, cols].astype(F32).T
                dst[:, lo] = jnp.where(lo_rows, bt, 0.0).astype(BF16)
                dst[:, hi] = jnp.where(lo_rows, 0.0, bt).astype(BF16)
            prod_t = (do_ref[:, cols] * o_ref[:, cols]).T
            dlt_row[:, lo] = jnp.sum(prod_t[:HEAD_DIM], axis=0, keepdims=True)
            dlt_row[:, hi] = jnp.sum(prod_t[HEAD_DIM:], axis=0, keepdims=True)
        lse_row = lse_ref[...]
        dqt[...] = jnp.zeros_like(dqt)

        def step(j, carry):
            off = pl.multiple_of(j * tk, tk)
            kc = k_ref[pl.ds(off, tk), :]
            vc = v_ref[pl.ds(off, tk), :]
            p = jnp.exp(_dot(kc, qst[...], _NN) - lse_row)
            dp = _dot(vc, dost[...], _NN)
            ds = (p * (dp - dlt_row[...])).astype(BF16)
            dk_ref[pl.ds(off, tk), :] += _dot(ds, qs[...], _NN)
            dv_ref[pl.ds(off, tk), :] += _dot(p, dos[...], _NN)
            dqt[...] += _dot(kt_ref[j], ds, _NN)
            return carry

        lax.fori_loop(0, nk, step, 0, unroll=4)
        for i in range(4):
            pair_t = jnp.where(lo_rows, dqt[:, tq * i:tq * (i + 1)], dqt[:, tq * (4 + i):tq * (5 + i)])
            dq_ref[:, LANES * i:LANES * (i + 1)] = pair_t.T
        if ns:
            pl.when(pl.program_id(0) == nq - 1)(ex.wait)

    sd = jax.ShapeDtypeStruct
    return pl.pallas_call(
        body, grid=(nq,),
        in_specs=[_row_spec(tq, Q_W), _par_spec(KV_W, t), _par_spec(KV_W, t),
                  pl.BlockSpec((nk, KV_W, tk), lambda i: (0, 0, 0)), _row_spec(tq, Q_W), _row_spec(tq, Q_W),
                  pl.BlockSpec((None, 1, r), lambda i: (i, 0, 0))] + [_ANY] * ns,
        out_specs=[_row_spec(tq, Q_W), _par_spec(KV_W, t), _par_spec(KV_W, t)] + [_ANY] * nslots,
        out_shape=[sd((t, Q_W), F32), sd((t, KV_W), F32), sd((t, KV_W), F32)] + slot_shapes,
        scratch_shapes=[pltpu.VMEM((r, LANES), BF16), pltpu.VMEM((r, LANES), BF16), pltpu.VMEM((LANES, r), BF16),
                        pltpu.VMEM((LANES, r), BF16), pltpu.VMEM((1, r), F32),
                        pltpu.VMEM((LANES, r), F32)] + (_exchange_sems(ns) if ns else []),
        compiler_params=_cparams(("arbitrary",)), name="gattn_bwd_scatter" if ns else "gattn_bwd",
    )(q, k, v, kt3, o, do, lse, *sgrads)


_WQ = Q_BLOCK
_WK = 3 * Q_BLOCK
_WR = 8 * _WQ


def _pairs_transposed(src_ref, dst, tq):
    lo_rows = lax.broadcasted_iota(jnp.int32, (LANES, tq), 0) < HEAD_DIM
    for i in range(4):
        bt = src_ref[:, LANES * i:LANES * (i + 1)].astype(F32).T
        dst[:, tq * i:tq * (i + 1)] = jnp.where(lo_rows, bt, 0.0).astype(BF16)
        dst[:, tq * (4 + i):tq * (5 + i)] = jnp.where(lo_rows, 0.0, bt).astype(BF16)


def _pairs_from_transposed(halves, dst_ref, tq):
    for i in range(4):
        pair_t = jnp.concatenate([h[:, tq * i:tq * (i + 1)] for h in halves], axis=0)
        dst_ref[:, LANES * i:LANES * (i + 1)] = pair_t.T.astype(dst_ref.dtype)


def _kv_quadrants(tq):
    return [(slice(HEAD_DIM * kv, HEAD_DIM * (kv + 1)), slice(4 * tq * kv, 4 * tq * (kv + 1))) for kv in range(2)]


def _wattn_scores_t(kw, qst, bias_ref, n, t):
    kabs = (n - 1) * _WQ + lax.broadcasted_iota(jnp.int32, (_WK, 1), 0)
    st = _dot(kw, qst[...], _NN) + bias_ref[...]
    return jnp.where((kabs >= 0) & (kabs < t), st, NEG)


def _window_t(ref3, n):
    return jnp.concatenate([ref3[n], ref3[n + 1], ref3[n + 2]], axis=1)


def _blocks_transposed(ap):
    return ap.reshape(ap.shape[0] // _WQ, _WQ, KV_W).transpose(0, 2, 1)


def _wattn_fwd(q, kp, vp, bias_t, sink):
    t = q.shape[0]
    nq = t // _WQ
    tp = t + 2 * _WQ
    vpt = _blocks_transposed(vp)

    def body(q_ref, k_ref, vt_ref, b_ref, sk_ref, o_ref, lse_ref, qst):
        n = pl.program_id(0)
        _pairs_transposed(q_ref, qst, _WQ)
        kw = k_ref[pl.ds(pl.multiple_of(n * _WQ, _WQ), _WK), :]
        st = _wattn_scores_t(kw, qst, b_ref, n, t)
        sk = sk_ref[...]
        m = jnp.maximum(jnp.max(st, axis=0, keepdims=True), sk)
        pt = jnp.exp(st - m)
        l = jnp.sum(pt, axis=0, keepdims=True) + jnp.exp(sk - m)
        vwt = _window_t(vt_ref, n)
        halves = [_dot(vwt[rows, :], pt[:, cols], _NN) / l[:, cols] for rows, cols in _kv_quadrants(_WQ)]
        _pairs_from_transposed(halves, o_ref, _WQ)
        lse_ref[...] = m + jnp.log(l)

    sd = jax.ShapeDtypeStruct
    return pl.pallas_call(
        body, grid=(nq,),
        in_specs=[_row_spec(_WQ, Q_W), _par_spec(KV_W, tp), pl.BlockSpec(vpt.shape, lambda i: (0, 0, 0)),
                  _par_spec(_WR, _WK), _par_spec(_WR)],
        out_specs=[_row_spec(_WQ, Q_W), pl.BlockSpec((None, 1, _WR), lambda i: (i, 0, 0))],
        out_shape=[sd((t, Q_W), F32), sd((nq, 1, _WR), F32)],
        scratch_shapes=[pltpu.VMEM((LANES, _WR), BF16)],
        compiler_params=_cparams(("parallel",)), name="wattn_fwd",
    )(q, kp, vpt, bias_t, sink)


def _wattn_bwd(q, kp, vp, bias_t, sink, o, do, lse):
    t = q.shape[0]
    nq = t // _WQ
    tp = t + 2 * _WQ
    kpt = _blocks_transposed(kp)

    def body(q_ref, k_ref, v_ref, kt_ref, b_ref, sk_ref, o_ref, do_ref, lse_ref, dq_ref, dk_ref, dv_ref, db_ref,
             dsk_ref, qs, dos, qst, dost):
        n = pl.program_id(0)

        @pl.when(n == 0)
        def _():
            dk_ref[...] = jnp.zeros_like(dk_ref)
            dv_ref[...] = jnp.zeros_like(dv_ref)
            db_ref[...] = jnp.zeros_like(db_ref)
            dsk_ref[...] = jnp.zeros_like(dsk_ref)

        _stack_heads(q_ref, qs, _WQ)
        _stack_heads(do_ref, dos, _WQ)
        _pairs_transposed(q_ref, qst, _WQ)
        _pairs_transposed(do_ref, dost, _WQ)
        delta = []
        for i in range(4):
            cols = slice(LANES * i, LANES * (i + 1))
            prod_t = (do_ref[:, cols] * o_ref[:, cols]).T
            delta.append((jnp.sum(prod_t[:HEAD_DIM], axis=0, keepdims=True),
                          jnp.sum(prod_t[HEAD_DIM:], axis=0, keepdims=True)))
        dlt = jnp.concatenate([d[0] for d in delta] + [d[1] for d in delta], axis=1)
        off = pl.multiple_of(n * _WQ, _WQ)
        kw = k_ref[pl.ds(off, _WK), :]
        vw = v_ref[pl.ds(off, _WK), :]
        lse_v = lse_ref[...]
        pt = jnp.exp(_wattn_scores_t(kw, qst, b_ref, n, t) - lse_v)
        dpt = _dot(vw, dost[...], _NN)
        ds = pt * (dpt - dlt)
        db_ref[...] += ds
        dsk_ref[...] -= jnp.exp(sk_ref[...] - lse_v) * dlt
        dsb = ds.astype(BF16)
        dk_ref[pl.ds(off, _WK), :] += _dot(dsb, qs[...], _NN)
        dv_ref[pl.ds(off, _WK), :] += _dot(pt, dos[...], _NN)
        kwt = _window_t(kt_ref, n)
        halves = [_dot(kwt[rows, :], dsb[:, cols], _NN) for rows, cols in _kv_quadrants(_WQ)]
        _pairs_from_transposed(halves, dq_ref, _WQ)

    sd = jax.ShapeDtypeStruct
    qb = _row_spec(_WQ, Q_W)
    return pl.pallas_call(
        body, grid=(nq,),
        in_specs=[qb, _par_spec(KV_W, tp), _par_spec(KV_W, tp), pl.BlockSpec(kpt.shape, lambda i: (0, 0, 0)),
                  _par_spec(_WR, _WK), _par_spec(_WR), qb, qb, pl.BlockSpec((None, 1, _WR), lambda i: (i, 0, 0))],
        out_specs=[qb, _par_spec(KV_W, tp), _par_spec(KV_W, tp), _par_spec(_WR, _WK), _par_spec(_WR)],
        out_shape=[sd((t, Q_W), F32), sd((tp, KV_W), F32), sd((tp, KV_W), F32), sd((_WK, _WR), F32), sd((1, _WR), F32)],
        scratch_shapes=[pltpu.VMEM((_WR, LANES), BF16), pltpu.VMEM((_WR, LANES), BF16), pltpu.VMEM((LANES, _WR), BF16),
                        pltpu.VMEM((LANES, _WR), BF16)],
        compiler_params=_cparams(("arbitrary",)), name="wattn_bwd",
    )(q, kp, vp, kpt, bias_t, sink, o, do, lse)


def _bias_bucket_reduce(db0, db1, bucket):
    def body(a_ref, b_ref, bk_ref, o_ref):
        d = a_ref[...] + b_ref[...]
        bk = bk_ref[...]
        lane = lax.broadcasted_iota(jnp.int32, (1, LANES), 1)
        out = jnp.zeros((1, LANES), F32)
        for b in range(N_BUCKETS):
            tot = jnp.sum(jnp.sum(jnp.where(bk == b, d, 0.0), axis=-1, keepdims=True), axis=0, keepdims=True)
            out = out + jnp.where(lane == b, tot, 0.0)
        o_ref[...] = out

    hb = pl.BlockSpec((None, _WQ, _WK), lambda h: (h, 0, 0))
    return pl.pallas_call(
        body, grid=(8,), in_specs=[hb, hb, pl.BlockSpec((_WQ, _WK), lambda h: (0, 0))],
        out_specs=pl.BlockSpec((None, 1, LANES), lambda h: (h, 0, 0)),
        out_shape=jax.ShapeDtypeStruct((8, 1, LANES), F32),
        compiler_params=_cparams(("parallel",)), name="bias_bucket_reduce",
    )(db0.reshape(8, _WQ, _WK), db1.reshape(8, _WQ, _WK), bucket)


def _rope_tables(t):
    rows_n = t // GRID_W
    row = jnp.repeat(jnp.arange(rows_n, dtype=F32), GRID_W)
    col = jnp.tile(jnp.arange(GRID_W, dtype=F32), rows_n)
    half = HEAD_DIM // 2
    inv_freq = ROPE_THETA ** (-jnp.arange(0, half, 2, dtype=F32) / half)
    ang = jnp.concatenate([row[:, None] * inv_freq, col[:, None] * inv_freq], axis=-1)
    cos, sin = jnp.cos(ang), jnp.sin(ang)
    c64 = jnp.repeat(cos, 2, axis=-1)
    s64 = jnp.stack([-sin, sin], axis=-1).reshape(t, HEAD_DIM)
    return jnp.tile(c64, (1, 2)), jnp.tile(s64, (1, 2))


def _t5_bucket(rel):
    half = N_BUCKETS // 2
    max_exact = half // 2
    bucket = jnp.where(rel > 0, half, 0)
    rp = jnp.abs(rel)
    rpf = jnp.maximum(rp, 1).astype(jnp.float32)
    large = max_exact + (jnp.log(rpf / max_exact) / math.log(MAX_DISTANCE / max_exact)
                         * (half - max_exact)).astype(jnp.int32)
    large = jnp.minimum(large, half - 1)
    return bucket + jnp.where(rp < max_exact, rp, large)


def _window_tables(rel_bias):
    qpos = jnp.arange(_WQ, dtype=jnp.int32)
    kpos = jnp.arange(_WK, dtype=jnp.int32) - _WQ
    rel = kpos[None, :] - qpos[:, None]
    bucket = _t5_bucket(rel)
    bias = jnp.zeros((8, _WQ, _WK), F32)
    for b in range(N_BUCKETS):
        bias = jnp.where((bucket == b)[None], rel_bias[b][:, None, None], bias)
    bias = jnp.where((jnp.abs(rel) <= WINDOW)[None], bias, NEG)
    return bias.reshape(_WR, _WK).T, bucket


def _pad_rows(a):
    return jnp.pad(a, ((_WQ, _WQ), (0, 0)))


def _layer_fwd(x, p, tabs, gather=None):
    cos_t, sin_t, bias = tabs
    h, qa, ka, va, qb, kb, vb = _in_proj_prep(x, p["win"], cos_t, sin_t, p["qn"], p["kn"])
    if gather is None:
        oa, lse_a = _gattn_fwd(qa, ka, va)
    else:
        oa, lse_a, *gathered = _gattn_fwd(qa, ka, va, gather=gather[0])
        p = gather[1](gathered)
    kbp, vbp = _pad_rows(kb), _pad_rows(vb)
    ob, lse_b = _wattn_fwd(qb, kbp, vbp, bias, p["sink"])
    y = _outnorm_fwd(oa, ob, p["ga"], p["gb"])
    z1, x1, x1b = _mm_reduce(y[None], p["wout"][None], _NN, F32, "out_proj", res=x, res_scale=ALPHA,
                             ln=(p["ln1g"], p["ln1b"]))
    gu, hdn = _gate_up_glu(x1b, p["wgu"], p["cw"])
    z2, x2, _ = _mm_reduce(hdn, p["wd"], _NN, F32, "down_proj", res=x1, res_scale=ALPHA, ln=(p["ln2g"], p["ln2b"]))
    saved = dict(x=x, h=h, qa=qa, ka=ka, va=va, qb=qb, kbp=kbp, vbp=vbp, oa=oa, ob=ob, lse_a=lse_a, lse_b=lse_b,
                 y=y, z1=z1, x1b=x1b, gu=gu, hdn=hdn, z2=z2)
    return x2, saved


def _block_grads(g, names=("w_in", "w_out", "w_gate", "w_up", "w_down")):
    make = dict(
        w_in=lambda: _col_blocks(_in_cols_to_pairs(g["win"], _from_pairs), IN_SH),
        w_out=lambda: _mix_rows_to_pairs(g["wout"], _from_pairs).reshape(N_SHARD, OUT_SH, D_MODEL),
        w_gate=lambda: g["wg"], w_up=lambda: g["wu"], w_down=lambda: g["wd"])
    return [make[n]() for n in names]


def _layer_bwd(dx2, p, s, tabs, layer=0, pending=None):
    cos_t, sin_t, bias = tabs
    t = dx2.shape[0]
    dz2, dz2b, dln2g, dln2b = _ln_bwd(dx2, s["z2"], p["ln2g"])
    dwd = _mm_tn_blocks(s["hdn"], dz2b, "down_dw")
    dgu, stats = _ffn_mid_bwd(s["gu"], dz2b, p["wd"], p["cw"])
    dgu = dgu.reshape(2 * N_SHARD, t, FF_SH)
    dx1 = _mm_reduce(dgu, p["wgu"], _NT, F32, "gate_up_dx", res=dz2, res_scale=ALPHA)
    dwg = _mm_tn_blocks(dgu, s["x1b"], "gate_dw", blk=0)
    dwu = _mm_tn_blocks(dgu, s["x1b"], "up_dw", blk=1)
    dz1, dz1b, dln1g, dln1b = _ln_bwd(dx1, s["z1"], p["ln1g"])
    dy = _mm_nt(dz1b, p["wout"], F32, "out_dx")
    dwout = _mm_tn(s["y"], dz1b, "out_dw")
    doa, dob, dga, dgb = _outnorm_bwd(dy, s["oa"], s["ob"], p["ga"], p["gb"])
    slots = None
    if pending is None:
        dqa, dka, dva = _gattn_bwd(s["qa"], s["ka"], s["va"], s["oa"], doa, s["lse_a"])
    else:
        mine = _block_grads(dict(wout=dwout, wg=dwg, wu=dwu, wd=dwd), ("w_out", "w_gate", "w_up", "w_down"))
        todo = list(pending) + [(o + 1, layer, g) for o, g in enumerate(mine)]
        dqa, dka, dva, *slots = _gattn_bwd(s["qa"], s["ka"], s["va"], s["oa"], doa, s["lse_a"],
                                           scatter=([(o, l) for o, l, _ in todo], [g for _, _, g in todo]))
    dqb, dkbp, dvbp, dbias, dsink = _wattn_bwd(s["qb"], s["kbp"], s["vbp"], bias, p["sink"], s["ob"], dob, s["lse_b"])
    dkb = lax.slice_in_dim(dkbp, _WQ, _WQ + t, axis=0)
    dvb = lax.slice_in_dim(dvbp, _WQ, _WQ + t, axis=0)
    dh, dqn, dkn = _prep_bwd(s["h"], cos_t, sin_t, p["qn"], p["kn"], dqa, dka, dva, dqb, dkb, dvb)
    dx = _mm_nt(dh, p["win"], F32, "in_dx", res=dz1, res_scale=ALPHA)
    dwin = _mm_tn(s["x"], dh, "in_dw")
    grads = dict(win=dwin, wout=dwout, wg=dwg, wu=dwu, wd=dwd, stats=stats, qn=dqn, kn=dkn, ga=dga, gb=dgb,
                 ln1g=dln1g, ln1b=dln1b, ln2g=dln2g, ln2b=dln2b, bias=dbias, sink=dsink, slots=slots)
    return dx, grads


def _prep_layer_params(l, win, wout, wg, wu, wd, cw, q_norm, k_norm, sink, out_norm_a, out_norm_b, conv_b,
                       ln1_g, ln1_b, ln2_g, ln2_b):
    win_full = win.transpose(1, 0, 2).reshape(D_MODEL, IN_COLS)
    row = lambda v: v.reshape(1, -1)
    late = {}
    if wout is not None:
        late = dict(
            wout=_mix_rows_to_pairs(wout.reshape(D_MODEL, D_MODEL)), wgu=jnp.concatenate([wg, wu], axis=0), wd=wd,
            cw=jnp.pad(cw, ((0, 0), (0, 5), (0, 0)))
            + jnp.pad(conv_b[l].reshape(N_SHARD, 1, FF_SH), ((0, 0), (3, 4), (0, 0))))
    return dict(
        late, win=_in_cols_to_pairs(win_full),
        qn=row(jnp.tile(q_norm[l], 2)), kn=row(jnp.tile(k_norm[l], 2)),
        ga=row(_to_pairs(out_norm_a[l], 0)), gb=row(_to_pairs(out_norm_b[l], 0)),
        ln1g=row(ln1_g[l]), ln1b=row(ln1_b[l]), ln2g=row(ln2_g[l]), ln2b=row(ln2_b[l]),
        sink=jnp.repeat(sink[l], _WQ).reshape(1, _WR))


def _local_step(x, tgt, params, rel_bias, gather=None, scatter=False):
    t = x.shape[0]
    cos_t, sin_t = _rope_tables(t)
    bias, bucket = _window_tables(rel_bias)
    tabs = (cos_t, sin_t, bias)
    saved = []
    for l in range(DEPTH):
        x, s = _layer_fwd(x, params[l], tabs, gather if l == 0 else None)
        saved.append(s)
    dx, loss = _loss_grad(x, tgt)
    grads = [None] * DEPTH
    for l in reversed(range(DEPTH)):
        pending = None
        if scatter and l == 0:
            pending = [(o, 1, g) for o, g in enumerate(_block_grads(grads[1]))]
        dx, grads[l] = _layer_bwd(dx, params[l], saved[l], tabs, l, pending)
    dbucket = _bias_bucket_reduce(grads[0]["bias"].T, grads[1]["bias"].T, bucket)
    return loss, dx, grads, dbucket


_ANY = pl.BlockSpec(memory_space=pl.ANY)
_MESH = pl.DeviceIdType.MESH


def _mesh_pos():
    return lax.axis_index("x"), lax.axis_index("y"), lax.axis_index("c")


def _other_chips(x, y):
    return [(1 - x, y), (x, 1 - y), (1 - x, 1 - y)]


class _Exchange:
    def __init__(self, local, sends, recvs):
        self.local, self.sends, self.recvs = local, sends, recvs

    def start(self):
        for cp in self.local + self.sends:
            cp.start()

    def wait(self):
        for cp in self.recvs:
            cp.wait_recv()
        for cp in self.sends:
            cp.wait_send()
        for cp in self.local:
            cp.wait()


def _exchange_sems(n):
    return [pltpu.SemaphoreType.DMA((n, 3)), pltpu.SemaphoreType.DMA((n, 3)), pltpu.SemaphoreType.DMA((n,))]


def _gather_exchange(ins, outs, send, recv, loc):
    x, y, c = _mesh_pos()
    me = 2 * x + y
    chips = _other_chips(x, y)

    def remote(i, k, block):
        px, py = chips[k]
        return pltpu.make_async_remote_copy(ins[i], outs[i].at[block], send.at[i, k], recv.at[i, k],
                                            device_id=(px, py, c), device_id_type=_MESH)

    n = len(ins)
    local = [pltpu.make_async_copy(ins[i], outs[i].at[me], loc.at[i]) for i in range(n)]
    sends = [remote(i, k, me) for i in range(n) for k in range(3)]
    recvs = [remote(i, k, 2 * chips[k][0] + chips[k][1]) for i in range(n) for k in range(3)]
    return _Exchange(local, sends, recvs)


def _scatter_exchange(items, ins, outs, send, recv, loc):
    x, y, c = _mesh_pos()
    me = 2 * x + y
    chips = _other_chips(x, y)

    def remote(j, k):
        o, l = items[j]
        px, py = chips[k]
        return pltpu.make_async_remote_copy(ins[j].at[2 * px + py], outs[o].at[k, l], send.at[j, k], recv.at[j, k],
                                            device_id=(px, py, c), device_id_type=_MESH)

    local = [pltpu.make_async_copy(ins[j].at[me], outs[o].at[3, l], loc.at[j]) for j, (o, l) in enumerate(items)]
    sends = [remote(j, k) for j in range(len(items)) for k in range(3)]
    return _Exchange(local, sends, sends)


def _gathered_shapes(shards):
    return [jax.ShapeDtypeStruct((N_SHARD,) + s.shape, s.dtype) for s in shards]


def _slot_shapes(blocks):
    return [jax.ShapeDtypeStruct((N_SHARD, DEPTH) + g.shape[1:], g.dtype) for g in blocks]


def _gather_shards(shards):
    n = len(shards)

    def body(*refs):
        ex = _gather_exchange(refs[:n], refs[n:2 * n], *refs[2 * n:])
        ex.start()
        ex.wait()

    return pl.pallas_call(
        body, in_specs=[_ANY] * n, out_specs=[_ANY] * n, out_shape=_gathered_shapes(shards),
        scratch_shapes=_exchange_sems(n), name="gather_weights",
    )(*shards)


def _scatter_into(items, grads, slots):
    n, ns = len(grads), len(slots)

    def body(*refs):
        ex = _scatter_exchange(items, refs[:n], refs[n + ns:n + 2 * ns], *refs[n + 2 * ns:])
        ex.start()
        ex.wait()

    return pl.pallas_call(
        body, in_specs=[_ANY] * (n + ns), out_specs=[_ANY] * ns,
        out_shape=[jax.ShapeDtypeStruct(s.shape, s.dtype) for s in slots],
        input_output_aliases={n + i: i for i in range(ns)},
        scratch_shapes=_exchange_sems(n), name="scatter_grads",
    )(*grads, *slots)


def _swap_with_sibling(parts):
    n = len(parts)

    def body(*refs):
        ins, outs = refs[:n], refs[n:2 * n]
        send, recv = refs[2 * n:]
        x, y, c = _mesh_pos()
        copies = [pltpu.make_async_remote_copy(ins[i], outs[i], send.at[i], recv.at[i], device_id=(x, y, 1 - c),
                                               device_id_type=_MESH) for i in range(n)]
        for cp in copies:
            cp.start()
        for cp in copies:
            cp.wait_recv()
        for cp in copies:
            cp.wait_send()

    return pl.pallas_call(
        body, in_specs=[_ANY] * n, out_specs=[_ANY] * n,
        out_shape=[jax.ShapeDtypeStruct(p.shape, p.dtype) for p in parts],
        scratch_shapes=[pltpu.SemaphoreType.DMA((n,)), pltpu.SemaphoreType.DMA((n,))],
        name="swap_sibling",
    )(*parts)


N_DEV = 8


def _allreduce_small(packed):
    rows = packed.shape[0]

    def body(in_ref, out_ref, buf, send, recv, loc):
        x, y, c = _mesh_pos()
        me = 4 * x + 2 * y + c
        own = pltpu.make_async_copy(in_ref, buf.at[me], loc)
        own.start()

        def remote(m, block):
            peer = (x ^ (m >> 2), y ^ ((m >> 1) & 1), c ^ (m & 1))
            return pltpu.make_async_remote_copy(in_ref, buf.at[block], send.at[m - 1], recv.at[m - 1],
                                                device_id=peer, device_id_type=_MESH)

        sends = [remote(m, me) for m in range(1, N_DEV)]
        for cp in sends:
            cp.start()
        for m in range(1, N_DEV):
            remote(m, me ^ m).wait_recv()
        for cp in sends:
            cp.wait_send()
        own.wait()
        tot = buf[0]
        for d in range(1, N_DEV):
            tot = tot + buf[d]
        out_ref[...] = tot

    vm = pl.BlockSpec(memory_space=pltpu.VMEM)
    return pl.pallas_call(
        body, in_specs=[vm], out_specs=vm, out_shape=jax.ShapeDtypeStruct((rows, LANES), F32),
        scratch_shapes=[pltpu.VMEM((N_DEV, rows, LANES), F32), pltpu.SemaphoreType.DMA((N_DEV - 1,)),
                        pltpu.SemaphoreType.DMA((N_DEV - 1,)), pltpu.SemaphoreType.DMA(())],
        name="allreduce_small",
    )(packed)


def _shard_rows(r):
    return r // 2 if r % 32 == 0 else r


def _sum_slots(slots):
    _, _, r, cdim = slots.shape
    tr = _shard_rows(r)

    def body(a_ref, b_ref, c_ref, d_ref, o_ref):
        up = lambda ref: ref[...].astype(F32)
        o_ref[...] = ((up(d_ref) + up(a_ref)) + up(b_ref)) + up(c_ref)

    def spec(k):
        return pl.BlockSpec((None, None, tr, cdim), lambda l, i: (k, l, i, 0))

    return pl.pallas_call(
        body, grid=(DEPTH, r // tr), in_specs=[spec(0), spec(1), spec(2), spec(3)],
        out_specs=pl.BlockSpec((None, tr, cdim), lambda l, i: (l, i, 0)),
        out_shape=jax.ShapeDtypeStruct((DEPTH, r, cdim), F32),
        compiler_params=_cparams(("parallel", "parallel")), name="sum_slots",
    )(slots, slots, slots, slots)


def _adamw_math(w, g, m, v):
    m = ADAM_B1 * m + (1.0 - ADAM_B1) * g
    v = ADAM_B2 * v + (1.0 - ADAM_B2) * (g * g)
    m_hat = m / (1.0 - ADAM_B1 ** ADAM_STEP)
    v_hat = v / (1.0 - ADAM_B2 ** ADAM_STEP)
    delta = -ADAM_LR * (m_hat / (jnp.sqrt(v_hat) + ADAM_EPS) + ADAM_WD * w)
    return delta, m, v


def _adamw_big(ga, gb, w, m, v):
    _, r, cdim = w.shape
    tr = _shard_rows(r)

    def body(ga_ref, gb_ref, w_ref, m_ref, v_ref, g_out, d_out, m_out, v_out):
        g = ga_ref[...] + gb_ref[...]
        d, mn, vn = _adamw_math(w_ref[...], g, m_ref[...], v_ref[...])
        g_out[...] = g
        d_out[...] = d
        m_out[...] = mn
        v_out[...] = vn

    spec = pl.BlockSpec((None, tr, cdim), lambda l, i: (l, i, 0))
    shp = jax.ShapeDtypeStruct(w.shape, F32)
    return pl.pallas_call(
        body, grid=(DEPTH, r // tr), in_specs=[spec] * 5, out_specs=[spec] * 4, out_shape=[shp] * 4,
        compiler_params=_cparams(("parallel", "parallel")), name="adamw_big",
    )(ga, gb, w, m, v)


def _adamw_small(ws, gs, ms, vs):
    n = len(ws)

    def body(*refs):
        w_r, g_r, m_r, v_r = (refs[k * n:(k + 1) * n] for k in range(4))
        d_o, m_o, v_o = (refs[(4 + k) * n:(5 + k) * n] for k in range(3))
        for i in range(n):
            d, mn, vn = _adamw_math(w_r[i][...], g_r[i][...], m_r[i][...], v_r[i][...])
            d_o[i][...] = d
            m_o[i][...] = mn
            v_o[i][...] = vn

    vm = pl.BlockSpec(memory_space=pltpu.VMEM)
    shp = [jax.ShapeDtypeStruct(w.shape, F32) for w in ws]
    outs = pl.pallas_call(
        body, in_specs=[vm] * (4 * n), out_specs=[vm] * (3 * n), out_shape=shp * 3, name="adamw_small",
    )(*ws, *gs, *ms, *vs)
    return outs[:n], outs[n:2 * n], outs[2 * n:]


def _tile_rows(a):
    a = a.reshape(-1, LANES)
    pad = (-a.shape[0]) % 8
    return jnp.pad(a, ((0, pad), (0, 0))) if pad else a


_SMALL_LAYER_PARTS = (("qn", 8), ("kn", 8), ("sink", 8), ("ga", 8), ("gb", 8), ("ln1g", 8), ("ln1b", 8),
                      ("ln2g", 8), ("ln2b", 8), ("stats", N_SHARD * 8 * FF_SH // LANES))
_SMALL_HEAD_ROWS = 16
_SMALL_LAYER_ROWS = sum(r for _, r in _SMALL_LAYER_PARTS)


def _pack_small(loss, dbucket, grads):
    parts = [_tile_rows(loss), _tile_rows(dbucket)]
    for l in range(DEPTH):
        parts += [_tile_rows(grads[l][name]) for name, _ in _SMALL_LAYER_PARTS]
    return jnp.concatenate(parts, axis=0)


def _unpack_small(tot, chip):
    out = dict(loss=tot[0, 0], rel_bias=tot[8:16, :N_BUCKETS].T)
    per = {name: [] for name, _ in _SMALL_LAYER_PARTS}
    for l in range(DEPTH):
        base = _SMALL_HEAD_ROWS + l * _SMALL_LAYER_ROWS
        for name, rows in _SMALL_LAYER_PARTS:
            per[name].append(tot[base:base + rows])
            base += rows
    fold = lambda v: v[0, :HEAD_DIM] + v[0, HEAD_DIM:]
    out["q_norm"] = jnp.stack([fold(v) for v in per["qn"]])
    out["k_norm"] = jnp.stack([fold(v) for v in per["kn"]])
    out["sink"] = jnp.stack([jnp.sum(v, axis=1) for v in per["sink"]])
    out["out_norm_a"] = jnp.stack([_from_pairs(v[:4].reshape(Q_W), 0) for v in per["ga"]])
    out["out_norm_b"] = jnp.stack([_from_pairs(v[:4].reshape(Q_W), 0) for v in per["gb"]])
    for name, key in (("ln1_g", "ln1g"), ("ln1_b", "ln1b"), ("ln2_g", "ln2g"), ("ln2_b", "ln2b")):
        out[name] = jnp.stack([v.reshape(D_MODEL) for v in per[key]])
    stats = [v.reshape(N_SHARD, 8, FF_SH) for v in per["stats"]]
    out["conv_b"] = jnp.stack([s[:, 0, :].reshape(D_FF) for s in stats])
    out["conv_w"] = jnp.stack([lax.dynamic_index_in_dim(s, chip, 0, keepdims=False)[1:4] for s in stats])
    return out


_WEIGHTS = ("rel_bias", "w_in", "q_norm", "k_norm", "sink", "out_norm_a", "out_norm_b", "w_out", "ln1_g", "ln1_b",
            "w_gate", "w_up", "conv_w", "conv_b", "w_down", "ln2_g", "ln2_b")
_BIG = ("w_in", "w_out", "w_gate", "w_up", "w_down")
_SMALL = tuple(n for n in _WEIGHTS if n not in _BIG)


def _col_blocks(g, n):
    return g.reshape(g.shape[0], N_SHARD, n).transpose(1, 0, 2)


def kernel(x, rel_bias, w_in, q_norm, k_norm, sink, out_norm_a, out_norm_b, w_out, ln1_g, ln1_b, w_gate, w_up, conv_w, conv_b, w_down, ln2_g, ln2_b, loss_target, m_rel_bias, m_w_in, m_q_norm, m_k_norm, m_sink, m_out_norm_a, m_out_norm_b, m_w_out, m_ln1_g, m_ln1_b, m_w_gate, m_w_up, m_conv_w, m_conv_b, m_w_down, m_ln2_g, m_ln2_b, v_rel_bias, v_w_in, v_q_norm, v_k_norm, v_sink, v_out_norm_a, v_out_norm_b, v_w_out, v_ln1_g, v_ln1_b, v_w_gate, v_w_up, v_conv_w, v_conv_b, v_w_down, v_ln2_g, v_ln2_b):
    w = dict(rel_bias=rel_bias, w_in=w_in, q_norm=q_norm, k_norm=k_norm, sink=sink, out_norm_a=out_norm_a,
             out_norm_b=out_norm_b, w_out=w_out, ln1_g=ln1_g, ln1_b=ln1_b, w_gate=w_gate, w_up=w_up, conv_w=conv_w,
             conv_b=conv_b, w_down=w_down, ln2_g=ln2_g, ln2_b=ln2_b)
    m = dict(rel_bias=m_rel_bias, w_in=m_w_in, q_norm=m_q_norm, k_norm=m_k_norm, sink=m_sink, out_norm_a=m_out_norm_a,
             out_norm_b=m_out_norm_b, w_out=m_w_out, ln1_g=m_ln1_g, ln1_b=m_ln1_b, w_gate=m_w_gate, w_up=m_w_up,
             conv_w=m_conv_w, conv_b=m_conv_b, w_down=m_w_down, ln2_g=m_ln2_g, ln2_b=m_ln2_b)
    v = dict(rel_bias=v_rel_bias, w_in=v_w_in, q_norm=v_q_norm, k_norm=v_k_norm, sink=v_sink, out_norm_a=v_out_norm_a,
             out_norm_b=v_out_norm_b, w_out=v_w_out, ln1_g=v_ln1_g, ln1_b=v_ln1_b, w_gate=v_w_gate, w_up=v_w_up,
             conv_w=v_conv_w, conv_b=v_conv_b, w_down=v_w_down, ln2_g=v_ln2_g, ln2_b=v_ln2_b)
    chip = 2 * lax.axis_index("x") + lax.axis_index("y")

    small_w = (q_norm, k_norm, sink, out_norm_a, out_norm_b, conv_b, ln1_g, ln1_b, ln2_g, ln2_b)
    (win0,) = _gather_shards([w_in[0].astype(BF16)])
    later = ([w[name][0].astype(BF16) for name in _BIG[1:]] + [w[name][1].astype(BF16) for name in _BIG] + [conv_w])
    params = [_prep_layer_params(0, win0, None, None, None, None, None, *small_w), None]

    def finish(g):
        wout0, wg0, wu0, wd0, win1, wout1, wg1, wu1, wd1, cw_all = g
        params[0] = _prep_layer_params(0, win0, wout0, wg0, wu0, wd0, cw_all[:, 0], *small_w)
        params[1] = _prep_layer_params(1, win1, wout1, wg1, wu1, wd1, cw_all[:, 1], *small_w)
        return params[0]

    loss, dx, grads, dbucket = _local_step(x[0], loss_target[0], params, rel_bias, gather=(later, finish),
                                           scatter=True)

    small = _unpack_small(_allreduce_small(_pack_small(loss, dbucket, grads)), chip)

    slots = list(grads[0]["slots"])
    slots[0] = _scatter_into([(0, 0)], _block_grads(grads[0], ("w_in",)), [slots[0]])[0]
    partial = [_sum_slots(s) for s in slots]
    other = _swap_with_sibling(partial)

    grad, delta, new_m, new_v = {}, {}, {}, {}
    for i, name in enumerate(_BIG):
        fix = (lambda a: jnp.swapaxes(a, 1, 2)) if name in ("w_gate", "w_up") else (lambda a: a)
        outs = _adamw_big(partial[i], other[i], fix(w[name]), fix(m[name]), fix(v[name]))
        grad[name], delta[name], new_m[name], new_v[name] = [fix(o) for o in outs]
    flat2 = lambda a: a.reshape(-1, a.shape[-1])
    ds, ms, vs = _adamw_small([flat2(w[n]) for n in _SMALL], [flat2(small[n]) for n in _SMALL],
                              [flat2(m[n]) for n in _SMALL], [flat2(v[n]) for n in _SMALL])
    for i, name in enumerate(_SMALL):
        grad[name] = small[name]
        delta[name] = ds[i].reshape(w[name].shape)
        new_m[name] = ms[i].reshape(w[name].shape)
        new_v[name] = vs[i].reshape(w[name].shape)

    return (small["loss"], dx[None], *[grad[n] for n in _WEIGHTS], *[delta[n] for n in _WEIGHTS],
            *[new_m[n] for n in _WEIGHTS], *[new_v[n] for n in _WEIGHTS])
```

```python
import math

import jax
import jax.numpy as jnp
from jax import lax
from jax.experimental import pallas as pl
from jax.experimental.pallas import tpu as pltpu

F32 = jnp.float32
BF16 = jnp.bfloat16

D_MODEL = 1024
DEPTH = 2
HEAD_DIM = 64
Q_W = 512
KV_W = 128
IN_COLS = 2 * (Q_W + 2 * KV_W)
N_SHARD = 4
IN_SH = IN_COLS // N_SHARD
OUT_SH = D_MODEL // N_SHARD
D_FF = 2816
FF_SH = D_FF // N_SHARD
Q_BLOCK = 128
WINDOW = 128
N_BUCKETS = 32
MAX_DISTANCE = 128
GRID_W = 64
ROPE_THETA = 10000.0
ALPHA = (2.0 * DEPTH) ** 0.25
RMS_EPS = 1e-6
LN_EPS = 1e-5
NEG = -1e30
LANES = 128
VMEM_LIMIT = 56 * 1024 * 1024

ADAM_LR = 0.001
ADAM_B1 = 0.9
ADAM_B2 = 0.999
ADAM_EPS = 1e-08
ADAM_WD = 0.01
ADAM_STEP = 10

_NN = (((1,), (0,)), ((), ()))
_NT = (((1,), (1,)), ((), ()))
_TN = (((0,), (0,)), ((), ()))


def _dot(a, b, dims):
    return lax.dot_general(a.astype(BF16), b.astype(BF16), dims, preferred_element_type=F32)


def _cparams(sem, vmem=VMEM_LIMIT):
    return pltpu.CompilerParams(dimension_semantics=sem, vmem_limit_bytes=vmem)


def _regroup(a, axis, n_outer, n_inner):
    shp = a.shape
    a = a.reshape(shp[:axis] + (n_outer, n_inner, HEAD_DIM) + shp[axis + 1:])
    return jnp.swapaxes(a, axis, axis + 1).reshape(shp)


def _to_pairs(a, axis):
    return _regroup(a, axis, 2, 4)


def _from_pairs(a, axis):
    return _regroup(a, axis, 4, 2)


def _in_cols_to_pairs(w, fn=_to_pairs):
    return jnp.concatenate([fn(w[..., :Q_W], w.ndim - 1), w[..., Q_W:Q_W + 2 * KV_W],
                            fn(w[..., Q_W + 2 * KV_W:2 * Q_W + 2 * KV_W], w.ndim - 1),
                            w[..., 2 * Q_W + 2 * KV_W:]], axis=-1)


def _mix_rows_to_pairs(w, fn=_to_pairs):
    return fn(w.reshape(2, Q_W, w.shape[-1]), 1).reshape(w.shape)


def _matmul(a, b, *, dims, grid, a_spec, b_spec, o_spec, out_shape, acc_shape, name, res=None,
            res_spec=None, res_scale=1.0):
    nk = grid[-1]
    kax = len(grid) - 1

    def body(*refs):
        if res is None:
            a_ref, b_ref, o_ref, acc = refs
            r_ref = None
        else:
            a_ref, b_ref, r_ref, o_ref, acc = refs
        k = pl.program_id(kax)

        @pl.when(k == 0)
        def _():
            acc[...] = jnp.zeros_like(acc)

        acc[...] += _dot(a_ref[...], b_ref[...], dims)

        @pl.when(k == nk - 1)
        def _():
            o = acc[...]
            if r_ref is not None:
                o = o + res_scale * r_ref[...]
            o_ref[...] = o.astype(o_ref.dtype)

    in_specs = [a_spec, b_spec] + ([res_spec] if res is not None else [])
    args = (a, b) + ((res,) if res is not None else ())
    sem = ("parallel",) * kax + ("arbitrary",)
    return pl.pallas_call(
        body, grid=grid, in_specs=in_specs, out_specs=o_spec, out_shape=out_shape,
        scratch_shapes=[pltpu.VMEM(acc_shape, F32)], compiler_params=_cparams(sem), name=name,
    )(*args)


def _mm_nt(a, b, out_dtype, name, tm=512, res=None, res_scale=1.0):
    m, kd = a.shape
    n = b.shape[0]
    return _matmul(
        a, b, dims=_NT, grid=(m // tm, 1),
        a_spec=pl.BlockSpec((tm, kd), lambda i, k: (i, 0)),
        b_spec=pl.BlockSpec((n, kd), lambda i, k: (0, 0)),
        o_spec=pl.BlockSpec((tm, n), lambda i, k: (i, 0)),
        out_shape=jax.ShapeDtypeStruct((m, n), out_dtype), acc_shape=(tm, n), name=name,
        res=res, res_spec=pl.BlockSpec((tm, n), lambda i, k: (i, 0)), res_scale=res_scale)


def _mm_tn(a, b, name, tk=1024, tn=None, out_dtype=BF16):
    t, m = a.shape
    n = b.shape[1]
    tn = n if tn is None else tn
    tk = min(tk, t)
    return _matmul(
        a, b, dims=_TN, grid=(n // tn, t // tk),
        a_spec=pl.BlockSpec((tk, m), lambda j, k: (k, 0)),
        b_spec=pl.BlockSpec((tk, tn), lambda j, k: (k, j)),
        o_spec=pl.BlockSpec((m, tn), lambda j, k: (0, j)),
        out_shape=jax.ShapeDtypeStruct((m, n), out_dtype), acc_shape=(m, tn), name=name)


def _blocked_n(w, dims):
    return w.shape[2] if dims == _NN else w.shape[1]


def _mm_reduce(a, w, dims, out_dtype, name, tm=512, res=None, res_scale=1.0, ln=None):
    nb, m, kd = a.shape
    n = _blocked_n(w, dims)
    n_in = 2 + (res is not None) + (2 if ln else 0)

    def body(*refs):
        a_ref, w_ref = refs[0], refs[1]
        acc = _dot(a_ref[0], w_ref[0], dims)
        for j in range(1, nb):
            acc = acc + _dot(a_ref[j], w_ref[j], dims)
        if res is not None:
            acc = acc + res_scale * refs[2][...]
        refs[n_in][...] = acc.astype(out_dtype)
        if ln:
            g_ref, b_ref = refs[n_in - 2], refs[n_in - 1]
            zc = acc - jnp.mean(acc, axis=-1, keepdims=True)
            r = lax.rsqrt(jnp.mean(zc * zc, axis=-1, keepdims=True) + LN_EPS)
            y = zc * r * g_ref[...] + b_ref[...]
            refs[n_in + 1][...] = y
            refs[n_in + 2][...] = y.astype(BF16)

    row = pl.BlockSpec((tm, n), lambda i: (i, 0))
    par = pl.BlockSpec((1, n), lambda i: (0, 0))
    sd = jax.ShapeDtypeStruct
    out = pl.pallas_call(
        body, grid=(m // tm,),
        in_specs=[pl.BlockSpec((nb, tm, kd), lambda i: (0, i, 0)), pl.BlockSpec(w.shape, lambda i: (0, 0, 0))]
        + ([row] if res is not None else []) + ([par, par] if ln else []),
        out_specs=[row] * (3 if ln else 1),
        out_shape=[sd((m, n), out_dtype)] + ([sd((m, n), F32), sd((m, n), BF16)] if ln else []),
        compiler_params=_cparams(("parallel",)), name=name,
    )(a, w, *((res,) if res is not None else ()), *(ln or ()))
    return out if ln else out[0]


def _mm_tn_blocks(a, b, name, blk=0, nb=N_SHARD, tk=1024, out_dtype=BF16):
    a3, b3 = a.ndim == 3, b.ndim == 3
    t, m, n = a.shape[-2], a.shape[-1], b.shape[-1]
    tk = min(tk, t)
    nsteps = t // tk

    def spec(blocked, width):
        if blocked:
            return pl.BlockSpec((nb, tk, width), lambda k: (blk, k, 0))
        return pl.BlockSpec((tk, width), lambda k: (k, 0))

    def body(a_ref, b_ref, o_ref, acc):
        k = pl.program_id(0)

        @pl.when(k == 0)
        def _():
            acc[...] = jnp.zeros_like(acc)

        for j in range(nb):
            acc[j] += _dot(a_ref[j] if a3 else a_ref[...], b_ref[j] if b3 else b_ref[...], _TN)

        @pl.when(k == nsteps - 1)
        def _():
            o_ref[...] = acc[...].astype(o_ref.dtype)

    return pl.pallas_call(
        body, grid=(nsteps,), in_specs=[spec(a3, m), spec(b3, n)],
        out_specs=pl.BlockSpec((nb, m, n), lambda k: (0, 0, 0)),
        out_shape=jax.ShapeDtypeStruct((nb, m, n), out_dtype),
        scratch_shapes=[pltpu.VMEM((nb, m, n), F32)],
        compiler_params=_cparams(("arbitrary",)), name=name,
    )(a, b)


def _row_spec(tm, n):
    return pl.BlockSpec((tm, n), lambda i: (i, 0))


def _par_spec(n, rows=1):
    return pl.BlockSpec((rows, n), lambda i: (0, 0))


def _swap_pairs(x):
    lane = lax.broadcasted_iota(jnp.int32, x.shape, 1)
    return jnp.where(lane % 2 == 0, pltpu.roll(x, LANES - 1, 1), pltpu.roll(x, 1, 1))


def _head_sums(v):
    lo = lax.broadcasted_iota(jnp.int32, v.shape, 1) < HEAD_DIM
    s_lo = jnp.sum(jnp.where(lo, v, 0.0), axis=-1, keepdims=True)
    s_hi = jnp.sum(jnp.where(lo, 0.0, v), axis=-1, keepdims=True)
    return jnp.where(lo, s_lo, s_hi)


def _qk_blocks():
    return [(128 * i, True) for i in range(4)] + [(Q_W, False)]


def _in_proj_prep(x, win, cos_t, sin_t, qn, kn, tm=512):
    t, kd = x.shape
    scale = HEAD_DIM ** -0.5
    wide = 2 * LANES

    def body(x_ref, w_ref, c_ref, s_ref, qn_ref, kn_ref, h_ref, qa_ref, ka_ref, va_ref, qb_ref, kb_ref, vb_ref):
        xb = x_ref[...].astype(BF16)
        c = c_ref[...]
        s = s_ref[...]

        def normed(v, gain):
            r = lax.rsqrt(_head_sums(v * v) * (1.0 / HEAD_DIM) + RMS_EPS)
            y = v * r * gain
            return y * c + _swap_pairs(y) * s

        for b in range(IN_COLS // wide):
            hb = _dot(xb, w_ref[:, wide * b:wide * (b + 1)], _NN)
            h_ref[:, wide * b:wide * (b + 1)] = hb
            lo, hi = hb[:, :LANES], hb[:, LANES:]
            if b < 2:
                qa_ref[:, wide * b:wide * b + LANES] = (normed(lo, qn_ref[...]) * scale).astype(BF16)
                qa_ref[:, wide * b + LANES:wide * (b + 1)] = (normed(hi, qn_ref[...]) * scale).astype(BF16)
            elif b == 2:
                ka_ref[...] = normed(lo, kn_ref[...]).astype(BF16)
                va_ref[...] = hi.astype(BF16)
            elif b < 5:
                qb_ref[:, wide * (b - 3):wide * (b - 2)] = (hb * scale).astype(BF16)
            else:
                kb_ref[...] = lo.astype(BF16)
                vb_ref[...] = hi.astype(BF16)

    sd = jax.ShapeDtypeStruct
    return pl.pallas_call(
        body, grid=(t // tm,),
        in_specs=[_row_spec(tm, kd), pl.BlockSpec(win.shape, lambda i: (0, 0)), _row_spec(tm, LANES),
                  _row_spec(tm, LANES), _par_spec(LANES), _par_spec(LANES)],
        out_specs=[_row_spec(tm, IN_COLS), _row_spec(tm, Q_W), _row_spec(tm, KV_W), _row_spec(tm, KV_W),
                   _row_spec(tm, Q_W), _row_spec(tm, KV_W), _row_spec(tm, KV_W)],
        out_shape=[sd((t, IN_COLS), F32), sd((t, Q_W), BF16), sd((t, KV_W), BF16), sd((t, KV_W), BF16),
                   sd((t, Q_W), BF16), sd((t, KV_W), BF16), sd((t, KV_W), BF16)],
        compiler_params=_cparams(("parallel",)), name="in_proj_prep",
    )(x, win, cos_t, sin_t, qn, kn)


def _prep_bwd(h, cos_t, sin_t, qn, kn, dqa, dka, dva, dqb, dkb, dvb, win, res, res_scale, tm=256):
    t = h.shape[0]
    scale = HEAD_DIM ** -0.5
    wide = 2 * LANES
    win = win.T

    def body(h_ref, c_ref, s_ref, qn_ref, kn_ref, dqa_ref, dka_ref, dva_ref, dqb_ref, dkb_ref, dvb_ref, w_ref, r_ref,
             dh_ref, dqn_ref, dkn_ref, dx_ref):
        @pl.when(pl.program_id(0) == 0)
        def _():
            dqn_ref[...] = jnp.zeros_like(dqn_ref)
            dkn_ref[...] = jnp.zeros_like(dkn_ref)

        c = c_ref[...]
        s = s_ref[...]

        def normed_bwd(x, d, gain, acc_ref):
            dy = d * c + _swap_pairs(d * s)
            r = lax.rsqrt(_head_sums(x * x) * (1.0 / HEAD_DIM) + RMS_EPS)
            xr = x * r
            acc_ref[...] += jnp.sum(dy * xr, axis=0, keepdims=True)
            gy = dy * gain
            return r * (gy - xr * (_head_sums(xr * gy) * (1.0 / HEAD_DIM)))

        def halves(b):
            lo, hi = slice(wide * b, wide * b + LANES), slice(wide * b + LANES, wide * (b + 1))
            if b < 2:
                return [normed_bwd(h_ref[:, sl], dqa_ref[:, sl] * scale, qn_ref[...], dqn_ref) for sl in (lo, hi)]
            if b == 2:
                return [normed_bwd(h_ref[:, lo], dka_ref[...], kn_ref[...], dkn_ref), dva_ref[...]]
            if b < 5:
                return [dqb_ref[:, wide * (b - 3):wide * (b - 2)] * scale]
            return [dkb_ref[...], dvb_ref[...]]

        acc = res_scale * r_ref[...]
        for b in range(IN_COLS // wide):
            blk = jnp.concatenate(halves(b), axis=1).astype(BF16)
            dh_ref[:, wide * b:wide * (b + 1)] = blk
            acc = acc + _dot(blk, w_ref[wide * b:wide * (b + 1), :], _NN)
        dx_ref[...] = acc

    sd = jax.ShapeDtypeStruct
    return pl.pallas_call(
        body, grid=(t // tm,),
        in_specs=[_row_spec(tm, IN_COLS), _row_spec(tm, LANES), _row_spec(tm, LANES), _par_spec(LANES), _par_spec(LANES),
                  _row_spec(tm, Q_W), _row_spec(tm, KV_W), _row_spec(tm, KV_W),
                  _row_spec(tm, Q_W), _row_spec(tm, KV_W), _row_spec(tm, KV_W),
                  pl.BlockSpec(win.shape, lambda i: (0, 0)), _row_spec(tm, D_MODEL)],
        out_specs=[_row_spec(tm, IN_COLS), _par_spec(LANES), _par_spec(LANES), _row_spec(tm, D_MODEL)],
        out_shape=[sd((t, IN_COLS), BF16), sd((1, LANES), F32), sd((1, LANES), F32), sd((t, D_MODEL), F32)],
        compiler_params=_cparams(("arbitrary",)), name="prep_bwd_in_dx",
    )(h, cos_t, sin_t, qn, kn, dqa, dka, dva, dqb, dkb, dvb, win, res)


def _outnorm_fwd(oa, ob, ga, gb, tm=512):
    t = oa.shape[0]

    def body(oa_ref, ob_ref, ga_ref, gb_ref, y_ref):
        for o_ref, g_ref, start in ((oa_ref, ga_ref, 0), (ob_ref, gb_ref, Q_W)):
            x = o_ref[...]
            r = lax.rsqrt(jnp.mean(x * x, axis=-1, keepdims=True) + RMS_EPS)
            y_ref[:, start:start + Q_W] = (x * r * g_ref[...]).astype(BF16)

    return pl.pallas_call(
        body, grid=(t // tm,),
        in_specs=[_row_spec(tm, Q_W), _row_spec(tm, Q_W), _par_spec(Q_W), _par_spec(Q_W)],
        out_specs=_row_spec(tm, D_MODEL), out_shape=jax.ShapeDtypeStruct((t, D_MODEL), BF16),
        compiler_params=_cparams(("parallel",)), name="outnorm_fwd",
    )(oa, ob, ga, gb)


def _outnorm_bwd(dy, oa, ob, ga, gb, tm=512):
    t = oa.shape[0]

    def body(dy_ref, oa_ref, ob_ref, ga_ref, gb_ref, doa_ref, dob_ref, dga_ref, dgb_ref):
        @pl.when(pl.program_id(0) == 0)
        def _():
            dga_ref[...] = jnp.zeros_like(dga_ref)
            dgb_ref[...] = jnp.zeros_like(dgb_ref)

        for o_ref, g_ref, do_ref, dg_ref, start in ((oa_ref, ga_ref, doa_ref, dga_ref, 0),
                                                    (ob_ref, gb_ref, dob_ref, dgb_ref, Q_W)):
            x = o_ref[...]
            d = dy_ref[:, start:start + Q_W]
            r = lax.rsqrt(jnp.mean(x * x, axis=-1, keepdims=True) + RMS_EPS)
            xr = x * r
            dg_ref[...] += jnp.sum(d * xr, axis=0, keepdims=True)
            gy = d * g_ref[...]
            do_ref[...] = r * (gy - xr * jnp.mean(xr * gy, axis=-1, keepdims=True))

    sd = jax.ShapeDtypeStruct
    return pl.pallas_call(
        body, grid=(t // tm,),
        in_specs=[_row_spec(tm, D_MODEL), _row_spec(tm, Q_W), _row_spec(tm, Q_W), _par_spec(Q_W), _par_spec(Q_W)],
        out_specs=[_row_spec(tm, Q_W), _row_spec(tm, Q_W), _par_spec(Q_W), _par_spec(Q_W)],
        out_shape=[sd((t, Q_W), F32), sd((t, Q_W), F32), sd((1, Q_W), F32), sd((1, Q_W), F32)],
        compiler_params=_cparams(("arbitrary",)), name="outnorm_bwd",
    )(dy, oa, ob, ga, gb)


def _ln_bwd(d, z, g, tm=512):
    t = z.shape[0]

    def body(d_ref, z_ref, g_ref, dz_ref, dzb_ref, dg_ref, db_ref):
        @pl.when(pl.program_id(0) == 0)
        def _():
            dg_ref[...] = jnp.zeros_like(dg_ref)
            db_ref[...] = jnp.zeros_like(db_ref)

        zz = z_ref[...]
        dd = d_ref[...]
        mu = jnp.mean(zz, axis=-1, keepdims=True)
        zc = zz - mu
        r = lax.rsqrt(jnp.mean(zc * zc, axis=-1, keepdims=True) + LN_EPS)
        xh = zc * r
        dg_ref[...] += jnp.sum(dd * xh, axis=0, keepdims=True)
        db_ref[...] += jnp.sum(dd, axis=0, keepdims=True)
        dxh = dd * g_ref[...]
        dz = r * (dxh - jnp.mean(dxh, axis=-1, keepdims=True) - xh * jnp.mean(dxh * xh, axis=-1, keepdims=True))
        dz_ref[...] = dz
        dzb_ref[...] = dz.astype(BF16)

    sd = jax.ShapeDtypeStruct
    return pl.pallas_call(
        body, grid=(t // tm,),
        in_specs=[_row_spec(tm, D_MODEL), _row_spec(tm, D_MODEL), _par_spec(D_MODEL)],
        out_specs=[_row_spec(tm, D_MODEL), _row_spec(tm, D_MODEL), _par_spec(D_MODEL), _par_spec(D_MODEL)],
        out_shape=[sd((t, D_MODEL), F32), sd((t, D_MODEL), BF16), sd((1, D_MODEL), F32), sd((1, D_MODEL), F32)],
        compiler_params=_cparams(("arbitrary",)), name="ln_bwd",
    )(d, z, g)


def _loss_grad(y, tgt, tm=512):
    t = y.shape[0]
    nsteps = t // tm

    def body(y_ref, t_ref, dy_ref, loss_ref, acc):
        i = pl.program_id(0)

        @pl.when(i == 0)
        def _():
            acc[...] = jnp.zeros_like(acc)

        e = y_ref[...] - t_ref[...]
        dy_ref[...] = e * (1.0 / D_MODEL)
        acc[...] += jnp.sum(e * e, axis=0, keepdims=True)

        @pl.when(i == nsteps - 1)
        def _():
            tot = jnp.sum(acc[...], axis=-1, keepdims=True) * (0.5 / D_MODEL)
            loss_ref[...] = jnp.broadcast_to(tot, loss_ref.shape)

    sd = jax.ShapeDtypeStruct
    return pl.pallas_call(
        body, grid=(nsteps,),
        in_specs=[_row_spec(tm, D_MODEL), _row_spec(tm, D_MODEL)],
        out_specs=[_row_spec(tm, D_MODEL), _par_spec(LANES)],
        out_shape=[sd((t, D_MODEL), F32), sd((1, LANES), F32)],
        scratch_shapes=[pltpu.VMEM((1, D_MODEL), F32)],
        compiler_params=_cparams(("arbitrary",)), name="loss_grad",
    )(y, tgt)


_GELU_C = math.sqrt(2.0 / math.pi)
_GELU_K = 0.044715
HALO = 16


def _gelu_parts(x):
    x2 = x * x
    th = jnp.tanh(x * (_GELU_C + (_GELU_C * _GELU_K) * x2))
    a = 0.5 + 0.5 * th
    dact = a + (0.5 * x) * (1.0 - th * th) * (_GELU_C + (3.0 * _GELU_C * _GELU_K) * x2)
    return x * a, dact


def _halo_specs(tm, t, shift=0):
    last = t // HALO - 1
    cur = pl.BlockSpec((None, tm, FF_SH), lambda j, i: (j + shift, i, 0))
    prev = pl.BlockSpec((None, HALO, FF_SH), lambda j, i: (j + shift, jnp.maximum(i * (tm // HALO) - 1, 0), 0))
    nxt = pl.BlockSpec((None, HALO, FF_SH), lambda j, i: (j + shift, jnp.minimum((i + 1) * (tm // HALO), last), 0))
    return [prev, cur, nxt]


def _gate_up_glu(x, wgu, cw, tm=512):
    t, kd = x.shape
    nsteps = t // tm
    last = t // HALO - 1

    def body(xp_ref, x_ref, xn_ref, w_ref, cw_ref, gu_ref, h_ref):
        i = pl.program_id(0)
        xc = x_ref[...]
        xp = jnp.where(i == 0, jnp.zeros_like(xp_ref[...]), xp_ref[...])
        xn = jnp.where(i == nsteps - 1, jnp.zeros_like(xn_ref[...]), xn_ref[...])
        xe = jnp.concatenate([xp, xc, xn], axis=0)
        te = tm + 2 * HALO
        mid = slice(HALO, HALO + tm)
        for j in range(N_SHARD):
            ge = _dot(xe, w_ref[j], _NN).astype(BF16)
            u = _dot(xc, w_ref[j + N_SHARD], _NN).astype(BF16)
            gu_ref[j] = ge[mid]
            gu_ref[j + N_SHARD] = u
            gf = ge.astype(F32)
            cwj = cw_ref[j]
            gc = (cwj[3:4, :] + pltpu.roll(gf, 1, 0) * cwj[0:1, :] + gf * cwj[1:2, :]
                  + pltpu.roll(gf, te - 1, 0) * cwj[2:3, :])
            act, _ = _gelu_parts(gc[mid])
            h_ref[j] = (act * u.astype(F32)).astype(BF16)

    sd = jax.ShapeDtypeStruct
    return pl.pallas_call(
        body, grid=(nsteps,),
        in_specs=[pl.BlockSpec((HALO, kd), lambda i: (jnp.maximum(i * (tm // HALO) - 1, 0), 0)),
                  pl.BlockSpec((tm, kd), lambda i: (i, 0)),
                  pl.BlockSpec((HALO, kd), lambda i: (jnp.minimum((i + 1) * (tm // HALO), last), 0)),
                  pl.BlockSpec(wgu.shape, lambda i: (0, 0, 0)), pl.BlockSpec(cw.shape, lambda i: (0, 0, 0))],
        out_specs=[pl.BlockSpec((2 * N_SHARD, tm, FF_SH), lambda i: (0, i, 0)),
                   pl.BlockSpec((N_SHARD, tm, FF_SH), lambda i: (0, i, 0))],
        out_shape=[sd((2 * N_SHARD, t, FF_SH), BF16), sd((N_SHARD, t, FF_SH), BF16)],
        compiler_params=_cparams(("parallel",)), name="gate_up_glu",
    )(x, x, x, wgu, cw)


def _ffn_mid_bwd(gu, dz, wd, cw, tm=1024):
    t = gu.shape[1]
    tm = min(tm, t)
    nsteps = t // tm
    te = tm + 2 * HALO
    kd = dz.shape[1]

    def body(gp_ref, g_ref, gn_ref, up_ref, u_ref, un_ref, dp_ref, d_ref, dn_ref, wd_ref, cw_ref, dgu_ref, st_ref):
        i = pl.program_id(1)

        @pl.when(i == 0)
        def _():
            st_ref[...] = jnp.zeros_like(st_ref)

        def ext(p_ref, c_ref, n_ref, dtype=F32):
            prev = jnp.where(i == 0, jnp.zeros_like(p_ref[...]), p_ref[...]).astype(dtype)
            nxt = jnp.where(i == nsteps - 1, jnp.zeros_like(n_ref[...]), n_ref[...]).astype(dtype)
            return jnp.concatenate([prev, c_ref[...].astype(dtype), nxt], axis=0)

        eg = ext(gp_ref, g_ref, gn_ref)
        eu = ext(up_ref, u_ref, un_ref)
        ed = _dot(ext(dp_ref, d_ref, dn_ref, BF16), wd_ref[...], _NT)
        w0, w1, w2 = cw_ref[0:1, :], cw_ref[1:2, :], cw_ref[2:3, :]
        g_m1 = pltpu.roll(eg, 1, 0)
        g_p1 = pltpu.roll(eg, te - 1, 0)
        gc = cw_ref[3:4, :] + g_m1 * w0 + eg * w1 + g_p1 * w2
        act, dact = _gelu_parts(gc)
        dgc = ed * eu * dact
        dg = pltpu.roll(dgc, te - 1, 0) * w0 + dgc * w1 + pltpu.roll(dgc, 1, 0) * w2
        mid = slice(HALO, HALO + tm)
        dgu_ref[0] = dg[mid].astype(BF16)
        dgu_ref[1] = (ed * act)[mid].astype(BF16)
        sel = dgc[mid]
        parts = [jnp.sum(sel, axis=0, keepdims=True),
                 jnp.sum(sel * g_m1[mid], axis=0, keepdims=True),
                 jnp.sum(sel * eg[mid], axis=0, keepdims=True),
                 jnp.sum(sel * g_p1[mid], axis=0, keepdims=True)]
        r8 = lax.broadcasted_iota(jnp.int32, (8, FF_SH), 0)
        upd = jnp.zeros((8, FF_SH), F32)
        for k, p in enumerate(parts):
            upd = upd + jnp.where(r8 == k, p, 0.0)
        st_ref[...] += upd

    sd = jax.ShapeDtypeStruct
    last = t // HALO - 1
    dz_specs = [pl.BlockSpec((HALO, kd), lambda j, i: (jnp.maximum(i * (tm // HALO) - 1, 0), 0)),
                pl.BlockSpec((tm, kd), lambda j, i: (i, 0)),
                pl.BlockSpec((HALO, kd), lambda j, i: (jnp.minimum((i + 1) * (tm // HALO), last), 0))]
    return pl.pallas_call(
        body, grid=(N_SHARD, nsteps),
        in_specs=_halo_specs(tm, t) + _halo_specs(tm, t, N_SHARD) + dz_specs
        + [pl.BlockSpec((None, FF_SH, kd), lambda j, i: (j, 0, 0)), pl.BlockSpec((None, 8, FF_SH), lambda j, i: (j, 0, 0))],
        out_specs=[pl.BlockSpec((2, None, tm, FF_SH), lambda j, i: (0, j, i, 0)),
                   pl.BlockSpec((None, 8, FF_SH), lambda j, i: (j, 0, 0))],
        out_shape=[sd((2, N_SHARD, t, FF_SH), BF16), sd((N_SHARD, 8, FF_SH), F32)],
        compiler_params=_cparams(("parallel", "arbitrary")), name="ffn_mid_bwd",
    )(gu, gu, gu, gu, gu, gu, dz, dz, dz, wd, cw)


def _stack_heads(src_ref, dst_ref, tq):
    lo = lax.broadcasted_iota(jnp.int32, (tq, LANES), 1) < HEAD_DIM
    for i in range(4):
        blk = src_ref[:, LANES * i:LANES * (i + 1)].astype(dst_ref.dtype)
        zero = jnp.zeros_like(blk)
        dst_ref[tq * i:tq * (i + 1), :] = jnp.where(lo, blk, zero)
        dst_ref[tq * (4 + i):tq * (5 + i), :] = jnp.where(lo, zero, blk)


def _gattn_fwd(q, k, v, gather=(), tq=128, tk=2048):
    t = q.shape[0]
    tk = min(tk, t)
    nq, nk, r = t // tq, t // tk, 8 * tq
    ng = len(gather)

    def body(*refs):
        q_ref, k_ref, v_ref = refs[:3]
        o_ref, lse_ref = refs[3 + ng:5 + ng]
        qst, m_s, l_s, acct = refs[5 + 2 * ng:9 + 2 * ng]
        if ng:
            ex = _gather_exchange(refs[3:3 + ng], refs[5 + ng:5 + 2 * ng], *refs[9 + 2 * ng:])
            pl.when(pl.program_id(0) == 0)(ex.start)
        lo_rows = lax.broadcasted_iota(jnp.int32, (LANES, tq), 0) < HEAD_DIM
        for i in range(4):
            bt = q_ref[:, LANES * i:LANES * (i + 1)].astype(F32).T
            qst[:, tq * i:tq * (i + 1)] = jnp.where(lo_rows, bt, 0.0).astype(BF16)
            qst[:, tq * (4 + i):tq * (5 + i)] = jnp.where(lo_rows, 0.0, bt).astype(BF16)
        m_s[...] = jnp.full_like(m_s, NEG)

        def max_step(j, carry):
            off = pl.multiple_of(j * tk, tk)
            st = _dot(k_ref[pl.ds(off, tk), :], qst[...], _NN)
            m_s[...] = jnp.maximum(m_s[...], jnp.max(st.reshape(tk // 8, 8, r), axis=0))
            return carry

        lax.fori_loop(0, nk, max_step, 0, unroll=2)
        m_row = jnp.max(m_s[...], axis=0, keepdims=True)
        l_s[...] = jnp.zeros_like(l_s)
        acct[...] = jnp.zeros_like(acct)

        def sum_step(j, carry):
            off = pl.multiple_of(j * tk, tk)
            st = _dot(k_ref[pl.ds(off, tk), :], qst[...], _NN)
            pt = jnp.exp(st - m_row)
            l_s[...] += jnp.sum(pt.reshape(tk // 8, 8, r), axis=0)
            acct[...] += _dot(v_ref[j], pt, _NN)
            return carry

        lax.fori_loop(0, nk, sum_step, 0, unroll=2)
        l_row = jnp.sum(l_s[...], axis=0, keepdims=True)
        ot = acct[...] / l_row
        for i in range(4):
            pair_t = jnp.where(lo_rows, ot[:, tq * i:tq * (i + 1)], ot[:, tq * (4 + i):tq * (5 + i)])
            o_ref[:, LANES * i:LANES * (i + 1)] = pair_t.T
        lse_ref[...] = m_row + jnp.log(l_row)
        if ng:
            pl.when(pl.program_id(0) == nq - 1)(ex.wait)

    sd = jax.ShapeDtypeStruct
    vt3 = v.reshape(nk, tk, KV_W).transpose(0, 2, 1)
    return pl.pallas_call(
        body, grid=(nq,),
        in_specs=[_row_spec(tq, Q_W), _par_spec(KV_W, t), pl.BlockSpec((nk, KV_W, tk), lambda i: (0, 0, 0))]
        + [_ANY] * ng,
        out_specs=[_row_spec(tq, Q_W), pl.BlockSpec((None, 1, r), lambda i: (i, 0, 0))] + [_ANY] * ng,
        out_shape=[sd((t, Q_W), F32), sd((nq, 1, r), F32)] + _gathered_shapes(gather),
        scratch_shapes=[pltpu.VMEM((LANES, r), BF16), pltpu.VMEM((8, r), F32), pltpu.VMEM((8, r), F32),
                        pltpu.VMEM((LANES, r), F32)] + (_exchange_sems(ng) if ng else []),
        compiler_params=_cparams(("arbitrary",) if ng else ("parallel",)),
        name="gattn_fwd_gather" if ng else "gattn_fwd",
    )(q, k, vt3, *gather)


def _gattn_bwd(q, k, v, o, do, lse, scatter=None, tq=128, tk=512):
    t = q.shape[0]
    tk = min(tk, t)
    nq, nk, r = t // tq, t // tk, 8 * tq
    items, sgrads = scatter if scatter else ((), ())
    ns = len(sgrads)
    slot_shapes = []
    for j, (o_idx, _) in enumerate(items):
        if o_idx == len(slot_shapes):
            slot_shapes += _slot_shapes([sgrads[j]])
    nslots = len(slot_shapes)

    n_in, n_scr = 7, 6
    kt3 = k.reshape(nk, tk, KV_W).transpose(0, 2, 1)

    def body(*refs):
        q_ref, k_ref, v_ref, kt_ref, o_ref, do_ref, lse_ref = refs[:n_in]
        dq_ref, dk_ref, dv_ref = refs[n_in + ns:n_in + 3 + ns]
        scr = n_in + 3 + ns + nslots
        qs, dos, qst, dost, dlt_row, dqt = refs[scr:scr + n_scr]
        if ns:
            ex = _scatter_exchange(items, refs[n_in:n_in + ns], refs[n_in + 3 + ns:scr], *refs[scr + n_scr:])
            pl.when(pl.program_id(0) == 0)(ex.start)

        @pl.when(pl.program_id(0) == 0)
        def _():
            dk_ref[...] = jnp.zeros_like(dk_ref)
            dv_ref[...] = jnp.zeros_like(dv_ref)

        _stack_heads(q_ref, qs, tq)
        _stack_heads(do_ref, dos, tq)
        lo_rows = lax.broadcasted_iota(jnp.int32, (LANES, tq), 0) < HEAD_DIM
        for i in range(4):
            lo, hi = slice(tq * i, tq * (i + 1)), slice(tq * (4 + i), tq * (5 + i))
            cols = slice(LANES * i, LANES * (i + 1))
            for src, dst in ((q_ref, qst), (do_ref, dost)):
                bt = src[:, cols].astype(F32).T
                dst[:, lo] = jnp.where(lo_rows, bt, 0.0).astype(BF16)
                dst[:, hi] = jnp.where(lo_rows, 0.0, bt).astype(BF16)
            prod_t = (do_ref[:, cols] * o_ref[:, cols]).T
            dlt_row[:, lo] = jnp.sum(prod_t[:HEAD_DIM], axis=0, keepdims=True)
            dlt_row[:, hi] = jnp.sum(prod_t[HEAD_DIM:], axis=0, keepdims=True)
        lse_row = lse_ref[...]
        dqt[...] = jnp.zeros_like(dqt)

        def step(j, carry):
            off = pl.multiple_of(j * tk, tk)
            kc = k_ref[pl.ds(off, tk), :]
            vc = v_ref[pl.ds(off, tk), :]
            p = jnp.exp(_dot(kc, qst[...], _NN) - lse_row)
            dp = _dot(vc, dost[...], _NN)
            ds = (p * (dp - dlt_row[...])).astype(BF16)
            dk_ref[pl.ds(off, tk), :] += _dot(ds, qs[...], _NN)
            dv_ref[pl.ds(off, tk), :] += _dot(p, dos[...], _NN)
            dqt[...] += _dot(kt_ref[j], ds, _NN)
            return carry

        lax.fori_loop(0, nk, step, 0, unroll=4)
        for i in range(4):
            pair_t = jnp.where(lo_rows, dqt[:, tq * i:tq * (i + 1)], dqt[:, tq * (4 + i):tq * (5 + i)])
            dq_ref[:, LANES * i:LANES * (i + 1)] = pair_t.T
        if ns:
            pl.when(pl.program_id(0) == nq - 1)(ex.wait)

    sd = jax.ShapeDtypeStruct
    return pl.pallas_call(
        body, grid=(nq,),
        in_specs=[_row_spec(tq, Q_W), _par_spec(KV_W, t), _par_spec(KV_W, t),
                  pl.BlockSpec((nk, KV_W, tk), lambda i: (0, 0, 0)), _row_spec(tq, Q_W), _row_spec(tq, Q_W),
                  pl.BlockSpec((None, 1, r), lambda i: (i, 0, 0))] + [_ANY] * ns,
        out_specs=[_row_spec(tq, Q_W), _par_spec(KV_W, t), _par_spec(KV_W, t)] + [_ANY] * nslots,
        out_shape=[sd((t, Q_W), F32), sd((t, KV_W), F32), sd((t, KV_W), F32)] + slot_shapes,
        scratch_shapes=[pltpu.VMEM((r, LANES), BF16), pltpu.VMEM((r, LANES), BF16), pltpu.VMEM((LANES, r), BF16),
                        pltpu.VMEM((LANES, r), BF16), pltpu.VMEM((1, r), F32),
                        pltpu.VMEM((LANES, r), F32)] + (_exchange_sems(ns) if ns else []),
        compiler_params=_cparams(("arbitrary",)), name="gattn_bwd_scatter" if ns else "gattn_bwd",
    )(q, k, v, kt3, o, do, lse, *sgrads)


_WQ = Q_BLOCK
_WK = 3 * Q_BLOCK
_WR = 8 * _WQ


def _pairs_transposed(src_ref, dst, tq):
    lo_rows = lax.broadcasted_iota(jnp.int32, (LANES, tq), 0) < HEAD_DIM
    for i in range(4):
        bt = src_ref[:, LANES * i:LANES * (i + 1)].astype(F32).T
        dst[:, tq * i:tq * (i + 1)] = jnp.where(lo_rows, bt, 0.0).astype(BF16)
        dst[:, tq * (4 + i):tq * (5 + i)] = jnp.where(lo_rows, 0.0, bt).astype(BF16)


def _pairs_from_transposed(halves, dst_ref, tq):
    for i in range(4):
        pair_t = jnp.concatenate([h[:, tq * i:tq * (i + 1)] for h in halves], axis=0)
        dst_ref[:, LANES * i:LANES * (i + 1)] = pair_t.T.astype(dst_ref.dtype)


def _kv_quadrants(tq):
    return [(slice(HEAD_DIM * kv, HEAD_DIM * (kv + 1)), slice(4 * tq * kv, 4 * tq * (kv + 1))) for kv in range(2)]


def _wattn_scores_t(kw, qst, bias_ref, n, t):
    kabs = (n - 1) * _WQ + lax.broadcasted_iota(jnp.int32, (_WK, 1), 0)
    st = _dot(kw, qst[...], _NN) + bias_ref[...]
    return jnp.where((kabs >= 0) & (kabs < t), st, NEG)


def _window_t(ref3, n):
    return jnp.concatenate([ref3[n], ref3[n + 1], ref3[n + 2]], axis=1)


def _blocks_transposed(ap):
    return ap.reshape(ap.shape[0] // _WQ, _WQ, KV_W).transpose(0, 2, 1)


def _wattn_fwd(q, kp, vp, bias_t, sink):
    t = q.shape[0]
    nq = t // _WQ
    tp = t + 2 * _WQ
    vpt = _blocks_transposed(vp)

    def body(q_ref, k_ref, vt_ref, b_ref, sk_ref, o_ref, lse_ref, qst):
        n = pl.program_id(0)
        _pairs_transposed(q_ref, qst, _WQ)
        kw = k_ref[pl.ds(pl.multiple_of(n * _WQ, _WQ), _WK), :]
        st = _wattn_scores_t(kw, qst, b_ref, n, t)
        sk = sk_ref[...]
        m = jnp.maximum(jnp.max(st, axis=0, keepdims=True), sk)
        pt = jnp.exp(st - m)
        l = jnp.sum(pt, axis=0, keepdims=True) + jnp.exp(sk - m)
        vwt = _window_t(vt_ref, n)
        halves = [_dot(vwt[rows, :], pt[:, cols], _NN) / l[:, cols] for rows, cols in _kv_quadrants(_WQ)]
        _pairs_from_transposed(halves, o_ref, _WQ)
        lse_ref[...] = m + jnp.log(l)

    sd = jax.ShapeDtypeStruct
    return pl.pallas_call(
        body, grid=(nq,),
        in_specs=[_row_spec(_WQ, Q_W), _par_spec(KV_W, tp), pl.BlockSpec(vpt.shape, lambda i: (0, 0, 0)),
                  _par_spec(_WR, _WK), _par_spec(_WR)],
        out_specs=[_row_spec(_WQ, Q_W), pl.BlockSpec((None, 1, _WR), lambda i: (i, 0, 0))],
        out_shape=[sd((t, Q_W), F32), sd((nq, 1, _WR), F32)],
        scratch_shapes=[pltpu.VMEM((LANES, _WR), BF16)],
        compiler_params=_cparams(("parallel",)), name="wattn_fwd",
    )(q, kp, vpt, bias_t, sink)


def _wattn_bwd(q, kp, vp, bias_t, sink, o, do, lse):
    t = q.shape[0]
    nq = t // _WQ
    tp = t + 2 * _WQ
    kpt = _blocks_transposed(kp)

    def body(q_ref, k_ref, v_ref, kt_ref, b_ref, sk_ref, o_ref, do_ref, lse_ref, dq_ref, dk_ref, dv_ref, db_ref,
             dsk_ref, qs, dos, qst, dost):
        n = pl.program_id(0)

        @pl.when(n == 0)
        def _():
            dk_ref[...] = jnp.zeros_like(dk_ref)
            dv_ref[...] = jnp.zeros_like(dv_ref)
            db_ref[...] = jnp.zeros_like(db_ref)
            dsk_ref[...] = jnp.zeros_like(dsk_ref)

        _stack_heads(q_ref, qs, _WQ)
        _stack_heads(do_ref, dos, _WQ)
        _pairs_transposed(q_ref, qst, _WQ)
        _pairs_transposed(do_ref, dost, _WQ)
        delta = []
        for i in range(4):
            cols = slice(LANES * i, LANES * (i + 1))
            prod_t = (do_ref[:, cols] * o_ref[:, cols]).T
            delta.append((jnp.sum(prod_t[:HEAD_DIM], axis=0, keepdims=True),
                          jnp.sum(prod_t[HEAD_DIM:], axis=0, keepdims=True)))
        dlt = jnp.concatenate([d[0] for d in delta] + [d[1] for d in delta], axis=1)
        off = pl.multiple_of(n * _WQ, _WQ)
        kw = k_ref[pl.ds(off, _WK), :]
        vw = v_ref[pl.ds(off, _WK), :]
        lse_v = lse_ref[...]
        pt = jnp.exp(_wattn_scores_t(kw, qst, b_ref, n, t) - lse_v)
        dpt = _dot(vw, dost[...], _NN)
        ds = pt * (dpt - dlt)
        db_ref[...] += ds
        dsk_ref[...] -= jnp.exp(sk_ref[...] - lse_v) * dlt
        dsb = ds.astype(BF16)
        dk_ref[pl.ds(off, _WK), :] += _dot(dsb, qs[...], _NN)
        dv_ref[pl.ds(off, _WK), :] += _dot(pt, dos[...], _NN)
        kwt = _window_t(kt_ref, n)
        halves = [_dot(kwt[rows, :], dsb[:, cols], _NN) for rows, cols in _kv_quadrants(_WQ)]
        _pairs_from_transposed(halves, dq_ref, _WQ)

    sd = jax.ShapeDtypeStruct
    qb = _row_spec(_WQ, Q_W)
    return pl.pallas_call(
        body, grid=(nq,),
        in_specs=[qb, _par_spec(KV_W, tp), _par_spec(KV_W, tp), pl.BlockSpec(kpt.shape, lambda i: (0, 0, 0)),
                  _par_spec(_WR, _WK), _par_spec(_WR), qb, qb, pl.BlockSpec((None, 1, _WR), lambda i: (i, 0, 0))],
        out_specs=[qb, _par_spec(KV_W, tp), _par_spec(KV_W, tp), _par_spec(_WR, _WK), _par_spec(_WR)],
        out_shape=[sd((t, Q_W), F32), sd((tp, KV_W), F32), sd((tp, KV_W), F32), sd((_WK, _WR), F32), sd((1, _WR), F32)],
        scratch_shapes=[pltpu.VMEM((_WR, LANES), BF16), pltpu.VMEM((_WR, LANES), BF16), pltpu.VMEM((LANES, _WR), BF16),
                        pltpu.VMEM((LANES, _WR), BF16)],
        compiler_params=_cparams(("arbitrary",)), name="wattn_bwd",
    )(q, kp, vp, kpt, bias_t, sink, o, do, lse)


def _bias_bucket_reduce(db0, db1, bucket):
    def body(a_ref, b_ref, bk_ref, o_ref):
        d = a_ref[...] + b_ref[...]
        bk = bk_ref[...]
        lane = lax.broadcasted_iota(jnp.int32, (1, LANES), 1)
        out = jnp.zeros((1, LANES), F32)
        for b in range(N_BUCKETS):
            tot = jnp.sum(jnp.sum(jnp.where(bk == b, d, 0.0), axis=-1, keepdims=True), axis=0, keepdims=True)
            out = out + jnp.where(lane == b, tot, 0.0)
        o_ref[...] = out

    hb = pl.BlockSpec((None, _WQ, _WK), lambda h: (h, 0, 0))
    return pl.pallas_call(
        body, grid=(8,), in_specs=[hb, hb, pl.BlockSpec((_WQ, _WK), lambda h: (0, 0))],
        out_specs=pl.BlockSpec((None, 1, LANES), lambda h: (h, 0, 0)),
        out_shape=jax.ShapeDtypeStruct((8, 1, LANES), F32),
        compiler_params=_cparams(("parallel",)), name="bias_bucket_reduce",
    )(db0.reshape(8, _WQ, _WK), db1.reshape(8, _WQ, _WK), bucket)


def _rope_tables(t):
    rows_n = t // GRID_W
    row = jnp.repeat(jnp.arange(rows_n, dtype=F32), GRID_W)
    col = jnp.tile(jnp.arange(GRID_W, dtype=F32), rows_n)
    half = HEAD_DIM // 2
    inv_freq = ROPE_THETA ** (-jnp.arange(0, half, 2, dtype=F32) / half)
    ang = jnp.concatenate([row[:, None] * inv_freq, col[:, None] * inv_freq], axis=-1)
    cos, sin = jnp.cos(ang), jnp.sin(ang)
    c64 = jnp.repeat(cos, 2, axis=-1)
    s64 = jnp.stack([-sin, sin], axis=-1).reshape(t, HEAD_DIM)
    return jnp.tile(c64, (1, 2)), jnp.tile(s64, (1, 2))


def _t5_bucket(rel):
    half = N_BUCKETS // 2
    max_exact = half // 2
    bucket = jnp.where(rel > 0, half, 0)
    rp = jnp.abs(rel)
    rpf = jnp.maximum(rp, 1).astype(jnp.float32)
    large = max_exact + (jnp.log(rpf / max_exact) / math.log(MAX_DISTANCE / max_exact)
                         * (half - max_exact)).astype(jnp.int32)
    large = jnp.minimum(large, half - 1)
    return bucket + jnp.where(rp < max_exact, rp, large)


def _window_tables(rel_bias):
    qpos = jnp.arange(_WQ, dtype=jnp.int32)
    kpos = jnp.arange(_WK, dtype=jnp.int32) - _WQ
    rel = kpos[None, :] - qpos[:, None]
    bucket = _t5_bucket(rel)
    bias = jnp.zeros((8, _WQ, _WK), F32)
    for b in range(N_BUCKETS):
        bias = jnp.where((bucket == b)[None], rel_bias[b][:, None, None], bias)
    bias = jnp.where((jnp.abs(rel) <= WINDOW)[None], bias, NEG)
    return bias.reshape(_WR, _WK).T, bucket


def _pad_rows(a):
    return jnp.pad(a, ((_WQ, _WQ), (0, 0)))


def _layer_fwd(x, p, tabs, gather=None):
    cos_t, sin_t, bias = tabs
    h, qa, ka, va, qb, kb, vb = _in_proj_prep(x, p["win"], cos_t, sin_t, p["qn"], p["kn"])
    if gather is None:
        oa, lse_a = _gattn_fwd(qa, ka, va)
    else:
        oa, lse_a, *gathered = _gattn_fwd(qa, ka, va, gather=gather[0])
        p = gather[1](gathered)
    kbp, vbp = _pad_rows(kb), _pad_rows(vb)
    ob, lse_b = _wattn_fwd(qb, kbp, vbp, bias, p["sink"])
    y = _outnorm_fwd(oa, ob, p["ga"], p["gb"])
    z1, x1, x1b = _mm_reduce(y[None], p["wout"][None], _NN, F32, "out_proj", res=x, res_scale=ALPHA,
                             ln=(p["ln1g"], p["ln1b"]))
    gu, hdn = _gate_up_glu(x1b, p["wgu"], p["cw"])
    z2, x2, _ = _mm_reduce(hdn, p["wd"], _NN, F32, "down_proj", res=x1, res_scale=ALPHA, ln=(p["ln2g"], p["ln2b"]))
    saved = dict(x=x, h=h, qa=qa, ka=ka, va=va, qb=qb, kbp=kbp, vbp=vbp, oa=oa, ob=ob, lse_a=lse_a, lse_b=lse_b,
                 y=y, z1=z1, x1b=x1b, gu=gu, hdn=hdn, z2=z2)
    return x2, saved


def _block_grads(g, names=("w_in", "w_out", "w_gate", "w_up", "w_down")):
    make = dict(
        w_in=lambda: _col_blocks(_in_cols_to_pairs(g["win"], _from_pairs), IN_SH),
        w_out=lambda: _mix_rows_to_pairs(g["wout"], _from_pairs).reshape(N_SHARD, OUT_SH, D_MODEL),
        w_gate=lambda: g["wg"], w_up=lambda: g["wu"], w_down=lambda: g["wd"])
    return [make[n]() for n in names]


def _layer_bwd(dx2, p, s, tabs, layer=0, pending=None):
    cos_t, sin_t, bias = tabs
    t = dx2.shape[0]
    dz2, dz2b, dln2g, dln2b = _ln_bwd(dx2, s["z2"], p["ln2g"])
    dwd = _mm_tn_blocks(s["hdn"], dz2b, "down_dw")
    dgu, stats = _ffn_mid_bwd(s["gu"], dz2b, p["wd"], p["cw"])
    dgu = dgu.reshape(2 * N_SHARD, t, FF_SH)
    dx1 = _mm_reduce(dgu, p["wgu"], _NT, F32, "gate_up_dx", res=dz2, res_scale=ALPHA)
    dwg = _mm_tn_blocks(dgu, s["x1b"], "gate_dw", blk=0)
    dwu = _mm_tn_blocks(dgu, s["x1b"], "up_dw", blk=1)
    dz1, dz1b, dln1g, dln1b = _ln_bwd(dx1, s["z1"], p["ln1g"])
    dy = _mm_nt(dz1b, p["wout"], F32, "out_dx")
    dwout = _mm_tn(s["y"], dz1b, "out_dw")
    doa, dob, dga, dgb = _outnorm_bwd(dy, s["oa"], s["ob"], p["ga"], p["gb"])
    slots = None
    if pending is None:
        dqa, dka, dva = _gattn_bwd(s["qa"], s["ka"], s["va"], s["oa"], doa, s["lse_a"])
    else:
        mine = _block_grads(dict(wout=dwout, wg=dwg, wu=dwu, wd=dwd), ("w_out", "w_gate", "w_up", "w_down"))
        todo = list(pending) + [(o + 1, layer, g) for o, g in enumerate(mine)]
        dqa, dka, dva, *slots = _gattn_bwd(s["qa"], s["ka"], s["va"], s["oa"], doa, s["lse_a"],
                                           scatter=([(o, l) for o, l, _ in todo], [g for _, _, g in todo]))
    dqb, dkbp, dvbp, dbias, dsink = _wattn_bwd(s["qb"], s["kbp"], s["vbp"], bias, p["sink"], s["ob"], dob, s["lse_b"])
    dkb = lax.slice_in_dim(dkbp, _WQ, _WQ + t, axis=0)
    dvb = lax.slice_in_dim(dvbp, _WQ, _WQ + t, axis=0)
    dh, dqn, dkn, dx = _prep_bwd(s["h"], cos_t, sin_t, p["qn"], p["kn"], dqa, dka, dva, dqb, dkb, dvb,
                                 p["win"], dz1, ALPHA)
    dwin = _mm_tn(s["x"], dh, "in_dw")
    grads = dict(win=dwin, wout=dwout, wg=dwg, wu=dwu, wd=dwd, stats=stats, qn=dqn, kn=dkn, ga=dga, gb=dgb,
                 ln1g=dln1g, ln1b=dln1b, ln2g=dln2g, ln2b=dln2b, bias=dbias, sink=dsink, slots=slots)
    return dx, grads


def _prep_layer_params(l, win, wout, wg, wu, wd, cw, q_norm, k_norm, sink, out_norm_a, out_norm_b, conv_b,
                       ln1_g, ln1_b, ln2_g, ln2_b):
    win_full = win.transpose(1, 0, 2).reshape(D_MODEL, IN_COLS)
    row = lambda v: v.reshape(1, -1)
    late = {}
    if wout is not None:
        late = dict(
            wout=_mix_rows_to_pairs(wout.reshape(D_MODEL, D_MODEL)), wgu=jnp.concatenate([wg, wu], axis=0), wd=wd,
            cw=jnp.pad(cw, ((0, 0), (0, 5), (0, 0)))
            + jnp.pad(conv_b[l].reshape(N_SHARD, 1, FF_SH), ((0, 0), (3, 4), (0, 0))))
    return dict(
        late, win=_in_cols_to_pairs(win_full),
        qn=row(jnp.tile(q_norm[l], 2)), kn=row(jnp.tile(k_norm[l], 2)),
        ga=row(_to_pairs(out_norm_a[l], 0)), gb=row(_to_pairs(out_norm_b[l], 0)),
        ln1g=row(ln1_g[l]), ln1b=row(ln1_b[l]), ln2g=row(ln2_g[l]), ln2b=row(ln2_b[l]),
        sink=jnp.repeat(sink[l], _WQ).reshape(1, _WR))


def _local_step(x, tgt, params, rel_bias, gather=None, scatter=False):
    t = x.shape[0]
    cos_t, sin_t = _rope_tables(t)
    bias, bucket = _window_tables(rel_bias)
    tabs = (cos_t, sin_t, bias)
    saved = []
    for l in range(DEPTH):
        x, s = _layer_fwd(x, params[l], tabs, gather if l == 0 else None)
        saved.append(s)
    dx, loss = _loss_grad(x, tgt)
    grads = [None] * DEPTH
    for l in reversed(range(DEPTH)):
        pending = None
        if scatter and l == 0:
            pending = [(o, 1, g) for o, g in enumerate(_block_grads(grads[1]))]
        dx, grads[l] = _layer_bwd(dx, params[l], saved[l], tabs, l, pending)
    dbucket = _bias_bucket_reduce(grads[0]["bias"].T, grads[1]["bias"].T, bucket)
    return loss, dx, grads, dbucket


_ANY = pl.BlockSpec(memory_space=pl.ANY)
_MESH = pl.DeviceIdType.MESH


def _mesh_pos():
    return lax.axis_index("x"), lax.axis_index("y"), lax.axis_index("c")


def _other_chips(x, y):
    return [(1 - x, y), (x, 1 - y), (1 - x, 1 - y)]


class _Exchange:
    def __init__(self, local, sends, recvs):
        self.local, self.sends, self.recvs = local, sends, recvs

    def start(self):
        for cp in self.local + self.sends:
            cp.start()

    def wait(self):
        for cp in self.recvs:
            cp.wait_recv()
        for cp in self.sends:
            cp.wait_send()
        for cp in self.local:
            cp.wait()


def _exchange_sems(n):
    return [pltpu.SemaphoreType.DMA((n, 3)), pltpu.SemaphoreType.DMA((n, 3)), pltpu.SemaphoreType.DMA((n,))]


def _gather_exchange(ins, outs, send, recv, loc):
    x, y, c = _mesh_pos()
    me = 2 * x + y
    chips = _other_chips(x, y)

    def remote(i, k, block):
        px, py = chips[k]
        return pltpu.make_async_remote_copy(ins[i], outs[i].at[block], send.at[i, k], recv.at[i, k],
                                            device_id=(px, py, c), device_id_type=_MESH)

    n = len(ins)
    local = [pltpu.make_async_copy(ins[i], outs[i].at[me], loc.at[i]) for i in range(n)]
    sends = [remote(i, k, me) for i in range(n) for k in range(3)]
    recvs = [remote(i, k, 2 * chips[k][0] + chips[k][1]) for i in range(n) for k in range(3)]
    return _Exchange(local, sends, recvs)


def _scatter_exchange(items, ins, outs, send, recv, loc):
    x, y, c = _mesh_pos()
    me = 2 * x + y
    chips = _other_chips(x, y)

    def remote(j, k):
        o, l = items[j]
        px, py = chips[k]
        return pltpu.make_async_remote_copy(ins[j].at[2 * px + py], outs[o].at[k, l], send.at[j, k], recv.at[j, k],
                                            device_id=(px, py, c), device_id_type=_MESH)

    local = [pltpu.make_async_copy(ins[j].at[me], outs[o].at[3, l], loc.at[j]) for j, (o, l) in enumerate(items)]
    sends = [remote(j, k) for j in range(len(items)) for k in range(3)]
    return _Exchange(local, sends, sends)


def _gathered_shapes(shards):
    return [jax.ShapeDtypeStruct((N_SHARD,) + s.shape, s.dtype) for s in shards]


def _slot_shapes(blocks):
    return [jax.ShapeDtypeStruct((N_SHARD, DEPTH) + g.shape[1:], g.dtype) for g in blocks]


def _gather_shards(shards):
    n = len(shards)

    def body(*refs):
        ex = _gather_exchange(refs[:n], refs[n:2 * n], *refs[2 * n:])
        ex.start()
        ex.wait()

    return pl.pallas_call(
        body, in_specs=[_ANY] * n, out_specs=[_ANY] * n, out_shape=_gathered_shapes(shards),
        scratch_shapes=_exchange_sems(n), name="gather_weights",
    )(*shards)


def _scatter_into(items, grads, slots):
    n, ns = len(grads), len(slots)

    def body(*refs):
        ex = _scatter_exchange(items, refs[:n], refs[n + ns:n + 2 * ns], *refs[n + 2 * ns:])
        ex.start()
        ex.wait()

    return pl.pallas_call(
        body, in_specs=[_ANY] * (n + ns), out_specs=[_ANY] * ns,
        out_shape=[jax.ShapeDtypeStruct(s.shape, s.dtype) for s in slots],
        input_output_aliases={n + i: i for i in range(ns)},
        scratch_shapes=_exchange_sems(n), name="scatter_grads",
    )(*grads, *slots)


def _swap_with_sibling(parts):
    n = len(parts)

    def body(*refs):
        ins, outs = refs[:n], refs[n:2 * n]
        send, recv = refs[2 * n:]
        x, y, c = _mesh_pos()
        copies = [pltpu.make_async_remote_copy(ins[i], outs[i], send.at[i], recv.at[i], device_id=(x, y, 1 - c),
                                               device_id_type=_MESH) for i in range(n)]
        for cp in copies:
            cp.start()
        for cp in copies:
            cp.wait_recv()
        for cp in copies:
            cp.wait_send()

    return pl.pallas_call(
        body, in_specs=[_ANY] * n, out_specs=[_ANY] * n,
        out_shape=[jax.ShapeDtypeStruct(p.shape, p.dtype) for p in parts],
        scratch_shapes=[pltpu.SemaphoreType.DMA((n,)), pltpu.SemaphoreType.DMA((n,))],
        name="swap_sibling",
    )(*parts)


N_DEV = 8


def _allreduce_small(packed):
    rows = packed.shape[0]

    def body(in_ref, out_ref, buf, send, recv, loc):
        x, y, c = _mesh_pos()
        me = 4 * x + 2 * y + c
        own = pltpu.make_async_copy(in_ref, buf.at[me], loc)
        own.start()

        def remote(m, block):
            peer = (x ^ (m >> 2), y ^ ((m >> 1) & 1), c ^ (m & 1))
            return pltpu.make_async_remote_copy(in_ref, buf.at[block], send.at[m - 1], recv.at[m - 1],
                                                device_id=peer, device_id_type=_MESH)

        sends = [remote(m, me) for m in range(1, N_DEV)]
        for cp in sends:
            cp.start()
        for m in range(1, N_DEV):
            remote(m, me ^ m).wait_recv()
        for cp in sends:
            cp.wait_send()
        own.wait()
        tot = buf[0]
        for d in range(1, N_DEV):
            tot = tot + buf[d]
        out_ref[...] = tot

    vm = pl.BlockSpec(memory_space=pltpu.VMEM)
    return pl.pallas_call(
        body, in_specs=[vm], out_specs=vm, out_shape=jax.ShapeDtypeStruct((rows, LANES), F32),
        scratch_shapes=[pltpu.VMEM((N_DEV, rows, LANES), F32), pltpu.SemaphoreType.DMA((N_DEV - 1,)),
                        pltpu.SemaphoreType.DMA((N_DEV - 1,)), pltpu.SemaphoreType.DMA(())],
        name="allreduce_small",
    )(packed)


def _shard_rows(r):
    return r // 2 if r % 32 == 0 else r


def _sum_slots(slots):
    _, _, r, cdim = slots.shape
    tr = _shard_rows(r)

    def body(a_ref, b_ref, c_ref, d_ref, o_ref):
        up = lambda ref: ref[...].astype(F32)
        o_ref[...] = ((up(d_ref) + up(a_ref)) + up(b_ref)) + up(c_ref)

    def spec(k):
        return pl.BlockSpec((None, None, tr, cdim), lambda l, i: (k, l, i, 0))

    return pl.pallas_call(
        body, grid=(DEPTH, r // tr), in_specs=[spec(0), spec(1), spec(2), spec(3)],
        out_specs=pl.BlockSpec((None, tr, cdim), lambda l, i: (l, i, 0)),
        out_shape=jax.ShapeDtypeStruct((DEPTH, r, cdim), F32),
        compiler_params=_cparams(("parallel", "parallel")), name="sum_slots",
    )(slots, slots, slots, slots)


def _adamw_math(w, g, m, v):
    m = ADAM_B1 * m + (1.0 - ADAM_B1) * g
    v = ADAM_B2 * v + (1.0 - ADAM_B2) * (g * g)
    m_hat = m / (1.0 - ADAM_B1 ** ADAM_STEP)
    v_hat = v / (1.0 - ADAM_B2 ** ADAM_STEP)
    delta = -ADAM_LR * (m_hat / (jnp.sqrt(v_hat) + ADAM_EPS) + ADAM_WD * w)
    return delta, m, v


def _adamw_big(ga, gb, w, m, v):
    _, r, cdim = w.shape
    tr = _shard_rows(r)

    def body(ga_ref, gb_ref, w_ref, m_ref, v_ref, g_out, d_out, m_out, v_out):
        g = ga_ref[...] + gb_ref[...]
        d, mn, vn = _adamw_math(w_ref[...], g, m_ref[...], v_ref[...])
        g_out[...] = g
        d_out[...] = d
        m_out[...] = mn
        v_out[...] = vn

    spec = pl.BlockSpec((None, tr, cdim), lambda l, i: (l, i, 0))
    shp = jax.ShapeDtypeStruct(w.shape, F32)
    return pl.pallas_call(
        body, grid=(DEPTH, r // tr), in_specs=[spec] * 5, out_specs=[spec] * 4, out_shape=[shp] * 4,
        compiler_params=_cparams(("parallel", "parallel")), name="adamw_big",
    )(ga, gb, w, m, v)


def _adamw_small(ws, gs, ms, vs):
    n = len(ws)

    def body(*refs):
        w_r, g_r, m_r, v_r = (refs[k * n:(k + 1) * n] for k in range(4))
        d_o, m_o, v_o = (refs[(4 + k) * n:(5 + k) * n] for k in range(3))
        for i in range(n):
            d, mn, vn = _adamw_math(w_r[i][...], g_r[i][...], m_r[i][...], v_r[i][...])
            d_o[i][...] = d
            m_o[i][...] = mn
            v_o[i][...] = vn

    vm = pl.BlockSpec(memory_space=pltpu.VMEM)
    shp = [jax.ShapeDtypeStruct(w.shape, F32) for w in ws]
    outs = pl.pallas_call(
        body, in_specs=[vm] * (4 * n), out_specs=[vm] * (3 * n), out_shape=shp * 3, name="adamw_small",
    )(*ws, *gs, *ms, *vs)
    return outs[:n], outs[n:2 * n], outs[2 * n:]


def _tile_rows(a):
    a = a.reshape(-1, LANES)
    pad = (-a.shape[0]) % 8
    return jnp.pad(a, ((0, pad), (0, 0))) if pad else a


_SMALL_LAYER_PARTS = (("qn", 8), ("kn", 8), ("sink", 8), ("ga", 8), ("gb", 8), ("ln1g", 8), ("ln1b", 8),
                      ("ln2g", 8), ("ln2b", 8), ("stats", N_SHARD * 8 * FF_SH // LANES))
_SMALL_HEAD_ROWS = 16
_SMALL_LAYER_ROWS = sum(r for _, r in _SMALL_LAYER_PARTS)


def _pack_small(loss, dbucket, grads):
    parts = [_tile_rows(loss), _tile_rows(dbucket)]
    for l in range(DEPTH):
        parts += [_tile_rows(grads[l][name]) for name, _ in _SMALL_LAYER_PARTS]
    return jnp.concatenate(parts, axis=0)


def _unpack_small(tot, chip):
    out = dict(loss=tot[0, 0], rel_bias=tot[8:16, :N_BUCKETS].T)
    per = {name: [] for name, _ in _SMALL_LAYER_PARTS}
    for l in range(DEPTH):
        base = _SMALL_HEAD_ROWS + l * _SMALL_LAYER_ROWS
        for name, rows in _SMALL_LAYER_PARTS:
            per[name].append(tot[base:base + rows])
            base += rows
    fold = lambda v: v[0, :HEAD_DIM] + v[0, HEAD_DIM:]
    out["q_norm"] = jnp.stack([fold(v) for v in per["qn"]])
    out["k_norm"] = jnp.stack([fold(v) for v in per["kn"]])
    out["sink"] = jnp.stack([jnp.sum(v, axis=1) for v in per["sink"]])
    out["out_norm_a"] = jnp.stack([_from_pairs(v[:4].reshape(Q_W), 0) for v in per["ga"]])
    out["out_norm_b"] = jnp.stack([_from_pairs(v[:4].reshape(Q_W), 0) for v in per["gb"]])
    for name, key in (("ln1_g", "ln1g"), ("ln1_b", "ln1b"), ("ln2_g", "ln2g"), ("ln2_b", "ln2b")):
        out[name] = jnp.stack([v.reshape(D_MODEL) for v in per[key]])
    stats = [v.reshape(N_SHARD, 8, FF_SH) for v in per["stats"]]
    out["conv_b"] = jnp.stack([s[:, 0, :].reshape(D_FF) for s in stats])
    out["conv_w"] = jnp.stack([lax.dynamic_index_in_dim(s, chip, 0, keepdims=False)[1:4] for s in stats])
    return out


_WEIGHTS = ("rel_bias", "w_in", "q_norm", "k_norm", "sink", "out_norm_a", "out_norm_b", "w_out", "ln1_g", "ln1_b",
            "w_gate", "w_up", "conv_w", "conv_b", "w_down", "ln2_g", "ln2_b")
_BIG = ("w_in", "w_out", "w_gate", "w_up", "w_down")
_SMALL = tuple(n for n in _WEIGHTS if n not in _BIG)


def _col_blocks(g, n):
    return g.reshape(g.shape[0], N_SHARD, n).transpose(1, 0, 2)


def kernel(x, rel_bias, w_in, q_norm, k_norm, sink, out_norm_a, out_norm_b, w_out, ln1_g, ln1_b, w_gate, w_up, conv_w, conv_b, w_down, ln2_g, ln2_b, loss_target, m_rel_bias, m_w_in, m_q_norm, m_k_norm, m_sink, m_out_norm_a, m_out_norm_b, m_w_out, m_ln1_g, m_ln1_b, m_w_gate, m_w_up, m_conv_w, m_conv_b, m_w_down, m_ln2_g, m_ln2_b, v_rel_bias, v_w_in, v_q_norm, v_k_norm, v_sink, v_out_norm_a, v_out_norm_b, v_w_out, v_ln1_g, v_ln1_b, v_w_gate, v_w_up, v_conv_w, v_conv_b, v_w_down, v_ln2_g, v_ln2_b):
    w = dict(rel_bias=rel_bias, w_in=w_in, q_norm=q_norm, k_norm=k_norm, sink=sink, out_norm_a=out_norm_a,
             out_norm_b=out_norm_b, w_out=w_out, ln1_g=ln1_g, ln1_b=ln1_b, w_gate=w_gate, w_up=w_up, conv_w=conv_w,
             conv_b=conv_b, w_down=w_down, ln2_g=ln2_g, ln2_b=ln2_b)
    m = dict(rel_bias=m_rel_bias, w_in=m_w_in, q_norm=m_q_norm, k_norm=m_k_norm, sink=m_sink, out_norm_a=m_out_norm_a,
             out_norm_b=m_out_norm_b, w_out=m_w_out, ln1_g=m_ln1_g, ln1_b=m_ln1_b, w_gate=m_w_gate, w_up=m_w_up,
             conv_w=m_conv_w, conv_b=m_conv_b, w_down=m_w_down, ln2_g=m_ln2_g, ln2_b=m_ln2_b)
    v = dict(rel_bias=v_rel_bias, w_in=v_w_in, q_norm=v_q_norm, k_norm=v_k_norm, sink=v_sink, out_norm_a=v_out_norm_a,
             out_norm_b=v_out_norm_b, w_out=v_w_out, ln1_g=v_ln1_g, ln1_b=v_ln1_b, w_gate=v_w_gate, w_up=v_w_up,
             conv_w=v_conv_w, conv_b=v_conv_b, w_down=v_w_down, ln2_g=v_ln2_g, ln2_b=v_ln2_b)
    chip = 2 * lax.axis_index("x") + lax.axis_index("y")

    small_w = (q_norm, k_norm, sink, out_norm_a, out_norm_b, conv_b, ln1_g, ln1_b, ln2_g, ln2_b)
    (win0,) = _gather_shards([w_in[0].astype(BF16)])
    later = ([w[name][0].astype(BF16) for name in _BIG[1:]] + [w[name][1].astype(BF16) for name in _BIG] + [conv_w])
    params = [_prep_layer_params(0, win0, None, None, None, None, None, *small_w), None]

    def finish(g):
        wout0, wg0, wu0, wd0, win1, wout1, wg1, wu1, wd1, cw_all = g
        params[0] = _prep_layer_params(0, win0, wout0, wg0, wu0, wd0, cw_all[:, 0], *small_w)
        params[1] = _prep_layer_params(1, win1, wout1, wg1, wu1, wd1, cw_all[:, 1], *small_w)
        return params[0]

    loss, dx, grads, dbucket = _local_step(x[0], loss_target[0], params, rel_bias, gather=(later, finish),
                                           scatter=True)

    small = _unpack_small(_allreduce_small(_pack_small(loss, dbucket, grads)), chip)

    slots = list(grads[0]["slots"])
    slots[0] = _scatter_into([(0, 0)], _block_grads(grads[0], ("w_in",)), [slots[0]])[0]
    partial = [_sum_slots(s) for s in slots]
    other = _swap_with_sibling(partial)

    grad, delta, new_m, new_v = {}, {}, {}, {}
    for i, name in enumerate(_BIG):
        fix = (lambda a: jnp.swapaxes(a, 1, 2)) if name in ("w_gate", "w_up") else (lambda a: a)
        outs = _adamw_big(partial[i], other[i], fix(w[name]), fix(m[name]), fix(v[name]))
        grad[name], delta[name], new_m[name], new_v[name] = [fix(o) for o in outs]
    flat2 = lambda a: a.reshape(-1, a.shape[-1])
    ds, ms, vs = _adamw_small([flat2(w[n]) for n in _SMALL], [flat2(small[n]) for n in _SMALL],
                              [flat2(m[n]) for n in _SMALL], [flat2(v[n]) for n in _SMALL])
    for i, name in enumerate(_SMALL):
        grad[name] = small[name]
        delta[name] = ds[i].reshape(w[name].shape)
        new_m[name] = ms[i].reshape(w[name].shape)
        new_v[name] = vs[i].reshape(w[name].shape)

    return (small["loss"], dx[None], *[grad[n] for n in _WEIGHTS], *[delta[n] for n in _WEIGHTS],
            *[new_m[n] for n in _WEIGHTS], *[new_v[n] for n in _WEIGHTS])
```

```python
import math

import jax
import jax.numpy as jnp
from jax import lax
from jax.experimental import pallas as pl
from jax.experimental.pallas import tpu as pltpu

F32 = jnp.float32
BF16 = jnp.bfloat16

D_MODEL = 1024
DEPTH = 2
HEAD_DIM = 64
Q_W = 512
KV_W = 128
IN_COLS = 2 * (Q_W + 2 * KV_W)
N_SHARD = 4
IN_SH = IN_COLS // N_SHARD
OUT_SH = D_MODEL // N_SHARD
D_FF = 2816
FF_SH = D_FF // N_SHARD
Q_BLOCK = 128
WINDOW = 128
N_BUCKETS = 32
MAX_DISTANCE = 128
GRID_W = 64
ROPE_THETA = 10000.0
ALPHA = (2.0 * DEPTH) ** 0.25
RMS_EPS = 1e-6
LN_EPS = 1e-5
NEG = -1e30
LANES = 128
VMEM_LIMIT = 56 * 1024 * 1024

ADAM_LR = 0.001
ADAM_B1 = 0.9
ADAM_B2 = 0.999
ADAM_EPS = 1e-08
ADAM_WD = 0.01
ADAM_STEP = 10

_NN = (((1,), (0,)), ((), ()))
_NT = (((1,), (1,)), ((), ()))
_TN = (((0,), (0,)), ((), ()))


def _dot(a, b, dims):
    return lax.dot_general(a.astype(BF16), b.astype(BF16), dims, preferred_element_type=F32)


def _cparams(sem, vmem=VMEM_LIMIT):
    return pltpu.CompilerParams(dimension_semantics=sem, vmem_limit_bytes=vmem)


def _regroup(a, axis, n_outer, n_inner):
    shp = a.shape
    a = a.reshape(shp[:axis] + (n_outer, n_inner, HEAD_DIM) + shp[axis + 1:])
    return jnp.swapaxes(a, axis, axis + 1).reshape(shp)


def _to_pairs(a, axis):
    return _regroup(a, axis, 2, 4)


def _from_pairs(a, axis):
    return _regroup(a, axis, 4, 2)


def _in_cols_to_pairs(w, fn=_to_pairs):
    return jnp.concatenate([fn(w[..., :Q_W], w.ndim - 1), w[..., Q_W:Q_W + 2 * KV_W],
                            fn(w[..., Q_W + 2 * KV_W:2 * Q_W + 2 * KV_W], w.ndim - 1),
                            w[..., 2 * Q_W + 2 * KV_W:]], axis=-1)


def _mix_rows_to_pairs(w, fn=_to_pairs):
    return fn(w.reshape(2, Q_W, w.shape[-1]), 1).reshape(w.shape)


def _matmul(a, b, *, dims, grid, a_spec, b_spec, o_spec, out_shape, acc_shape, name, res=None,
            res_spec=None, res_scale=1.0):
    nk = grid[-1]
    kax = len(grid) - 1

    def body(*refs):
        if res is None:
            a_ref, b_ref, o_ref, acc = refs
            r_ref = None
        else:
            a_ref, b_ref, r_ref, o_ref, acc = refs
        k = pl.program_id(kax)

        @pl.when(k == 0)
        def _():
            acc[...] = jnp.zeros_like(acc)

        acc[...] += _dot(a_ref[...], b_ref[...], dims)

        @pl.when(k == nk - 1)
        def _():
            o = acc[...]
            if r_ref is not None:
                o = o + res_scale * r_ref[...]
            o_ref[...] = o.astype(o_ref.dtype)

    in_specs = [a_spec, b_spec] + ([res_spec] if res is not None else [])
    args = (a, b) + ((res,) if res is not None else ())
    sem = ("parallel",) * kax + ("arbitrary",)
    return pl.pallas_call(
        body, grid=grid, in_specs=in_specs, out_specs=o_spec, out_shape=out_shape,
        scratch_shapes=[pltpu.VMEM(acc_shape, F32)], compiler_params=_cparams(sem), name=name,
    )(*args)


def _mm_nt(a, b, out_dtype, name, tm=512, res=None, res_scale=1.0):
    m, kd = a.shape
    n = b.shape[0]
    return _matmul(
        a, b, dims=_NT, grid=(m // tm, 1),
        a_spec=pl.BlockSpec((tm, kd), lambda i, k: (i, 0)),
        b_spec=pl.BlockSpec((n, kd), lambda i, k: (0, 0)),
        o_spec=pl.BlockSpec((tm, n), lambda i, k: (i, 0)),
        out_shape=jax.ShapeDtypeStruct((m, n), out_dtype), acc_shape=(tm, n), name=name,
        res=res, res_spec=pl.BlockSpec((tm, n), lambda i, k: (i, 0)), res_scale=res_scale)


def _mm_tn(a, b, name, tk=1024, tn=None, out_dtype=BF16):
    t, m = a.shape
    n = b.shape[1]
    tn = n if tn is None else tn
    tk = min(tk, t)
    return _matmul(
        a, b, dims=_TN, grid=(n // tn, t // tk),
        a_spec=pl.BlockSpec((tk, m), lambda j, k: (k, 0)),
        b_spec=pl.BlockSpec((tk, tn), lambda j, k: (k, j)),
        o_spec=pl.BlockSpec((m, tn), lambda j, k: (0, j)),
        out_shape=jax.ShapeDtypeStruct((m, n), out_dtype), acc_shape=(m, tn), name=name)


def _blocked_n(w, dims):
    return w.shape[2] if dims == _NN else w.shape[1]


def _mm_reduce(a, w, dims, out_dtype, name, tm=512, res=None, res_scale=1.0, ln=None):
    nb, m, kd = a.shape
    n = _blocked_n(w, dims)
    n_in = 2 + (res is not None) + (2 if ln else 0)

    def body(*refs):
        a_ref, w_ref = refs[0], refs[1]
        acc = _dot(a_ref[0], w_ref[0], dims)
        for j in range(1, nb):
            acc = acc + _dot(a_ref[j], w_ref[j], dims)
        if res is not None:
            acc = acc + res_scale * refs[2][...]
        refs[n_in][...] = acc.astype(out_dtype)
        if ln:
            g_ref, b_ref = refs[n_in - 2], refs[n_in - 1]
            zc = acc - jnp.mean(acc, axis=-1, keepdims=True)
            r = lax.rsqrt(jnp.mean(zc * zc, axis=-1, keepdims=True) + LN_EPS)
            y = zc * r * g_ref[...] + b_ref[...]
            refs[n_in + 1][...] = y
            refs[n_in + 2][...] = y.astype(BF16)

    row = pl.BlockSpec((tm, n), lambda i: (i, 0))
    par = pl.BlockSpec((1, n), lambda i: (0, 0))
    sd = jax.ShapeDtypeStruct
    out = pl.pallas_call(
        body, grid=(m // tm,),
        in_specs=[pl.BlockSpec((nb, tm, kd), lambda i: (0, i, 0)), pl.BlockSpec(w.shape, lambda i: (0, 0, 0))]
        + ([row] if res is not None else []) + ([par, par] if ln else []),
        out_specs=[row] * (3 if ln else 1),
        out_shape=[sd((m, n), out_dtype)] + ([sd((m, n), F32), sd((m, n), BF16)] if ln else []),
        compiler_params=_cparams(("parallel",)), name=name,
    )(a, w, *((res,) if res is not None else ()), *(ln or ()))
    return out if ln else out[0]


def _mm_tn_blocks(a, b, name, blk=0, nb=N_SHARD, tk=1024, out_dtype=BF16):
    a3, b3 = a.ndim == 3, b.ndim == 3
    t, m, n = a.shape[-2], a.shape[-1], b.shape[-1]
    tk = min(tk, t)
    nsteps = t // tk

    def spec(blocked, width):
        if blocked:
            return pl.BlockSpec((nb, tk, width), lambda k: (blk, k, 0))
        return pl.BlockSpec((tk, width), lambda k: (k, 0))

    def body(a_ref, b_ref, o_ref, acc):
        k = pl.program_id(0)

        @pl.when(k == 0)
        def _():
            acc[...] = jnp.zeros_like(acc)

        for j in range(nb):
            acc[j] += _dot(a_ref[j] if a3 else a_ref[...], b_ref[j] if b3 else b_ref[...], _TN)

        @pl.when(k == nsteps - 1)
        def _():
            o_ref[...] = acc[...].astype(o_ref.dtype)

    return pl.pallas_call(
        body, grid=(nsteps,), in_specs=[spec(a3, m), spec(b3, n)],
        out_specs=pl.BlockSpec((nb, m, n), lambda k: (0, 0, 0)),
        out_shape=jax.ShapeDtypeStruct((nb, m, n), out_dtype),
        scratch_shapes=[pltpu.VMEM((nb, m, n), F32)],
        compiler_params=_cparams(("arbitrary",)), name=name,
    )(a, b)


def _row_spec(tm, n):
    return pl.BlockSpec((tm, n), lambda i: (i, 0))


def _par_spec(n, rows=1):
    return pl.BlockSpec((rows, n), lambda i: (0, 0))


def _swap_pairs(x):
    lane = lax.broadcasted_iota(jnp.int32, x.shape, 1)
    return jnp.where(lane % 2 == 0, pltpu.roll(x, LANES - 1, 1), pltpu.roll(x, 1, 1))


def _head_sums(v):
    lo = lax.broadcasted_iota(jnp.int32, v.shape, 1) < HEAD_DIM
    s_lo = jnp.sum(jnp.where(lo, v, 0.0), axis=-1, keepdims=True)
    s_hi = jnp.sum(jnp.where(lo, 0.0, v), axis=-1, keepdims=True)
    return jnp.where(lo, s_lo, s_hi)


def _qk_blocks():
    return [(128 * i, True) for i in range(4)] + [(Q_W, False)]


def _in_proj_prep(x, win, cos_t, sin_t, qn, kn, tm=512):
    t, kd = x.shape
    scale = HEAD_DIM ** -0.5
    wide = 2 * LANES

    def body(x_ref, w_ref, c_ref, s_ref, qn_ref, kn_ref, h_ref, qa_ref, ka_ref, va_ref, qb_ref, kb_ref, vb_ref):
        xb = x_ref[...].astype(BF16)
        c = c_ref[...]
        s = s_ref[...]

        def normed(v, gain):
            r = lax.rsqrt(_head_sums(v * v) * (1.0 / HEAD_DIM) + RMS_EPS)
            y = v * r * gain
            return y * c + _swap_pairs(y) * s

        for b in range(IN_COLS // wide):
            hb = _dot(xb, w_ref[:, wide * b:wide * (b + 1)], _NN)
            h_ref[:, wide * b:wide * (b + 1)] = hb
            lo, hi = hb[:, :LANES], hb[:, LANES:]
            if b < 2:
                qa_ref[:, wide * b:wide * b + LANES] = (normed(lo, qn_ref[...]) * scale).astype(BF16)
                qa_ref[:, wide * b + LANES:wide * (b + 1)] = (normed(hi, qn_ref[...]) * scale).astype(BF16)
            elif b == 2:
                ka_ref[...] = normed(lo, kn_ref[...]).astype(BF16)
                va_ref[...] = hi.astype(BF16)
            elif b < 5:
                qb_ref[:, wide * (b - 3):wide * (b - 2)] = (hb * scale).astype(BF16)
            else:
                kb_ref[...] = lo.astype(BF16)
                vb_ref[...] = hi.astype(BF16)

    sd = jax.ShapeDtypeStruct
    return pl.pallas_call(
        body, grid=(t // tm,),
        in_specs=[_row_spec(tm, kd), pl.BlockSpec(win.shape, lambda i: (0, 0)), _row_spec(tm, LANES),
                  _row_spec(tm, LANES), _par_spec(LANES), _par_spec(LANES)],
        out_specs=[_row_spec(tm, IN_COLS), _row_spec(tm, Q_W), _row_spec(tm, KV_W), _row_spec(tm, KV_W),
                   _row_spec(tm, Q_W), _row_spec(tm, KV_W), _row_spec(tm, KV_W)],
        out_shape=[sd((t, IN_COLS), F32), sd((t, Q_W), BF16), sd((t, KV_W), BF16), sd((t, KV_W), BF16),
                   sd((t, Q_W), BF16), sd((t, KV_W), BF16), sd((t, KV_W), BF16)],
        compiler_params=_cparams(("parallel",)), name="in_proj_prep",
    )(x, win, cos_t, sin_t, qn, kn)


def _prep_bwd(h, cos_t, sin_t, qn, kn, dqa, dka, dva, dqb, dkb, dvb, tm=256):
    t = h.shape[0]
    scale = HEAD_DIM ** -0.5

    def body(h_ref, c_ref, s_ref, qn_ref, kn_ref, dqa_ref, dka_ref, dva_ref, dqb_ref, dkb_ref, dvb_ref,
             dh_ref, dqn_ref, dkn_ref):
        @pl.when(pl.program_id(0) == 0)
        def _():
            dqn_ref[...] = jnp.zeros_like(dqn_ref)
            dkn_ref[...] = jnp.zeros_like(dkn_ref)

        c = c_ref[...]
        s = s_ref[...]
        for start, is_q in _qk_blocks():
            x = h_ref[:, start:start + LANES]
            gain = qn_ref[...] if is_q else kn_ref[...]
            d = dqa_ref[:, start:start + LANES] * scale if is_q else dka_ref[...]
            dy = d * c + _swap_pairs(d * s)
            r = lax.rsqrt(_head_sums(x * x) * (1.0 / HEAD_DIM) + RMS_EPS)
            xr = x * r
            gsum = jnp.sum(dy * xr, axis=0, keepdims=True)
            if is_q:
                dqn_ref[...] += gsum
            else:
                dkn_ref[...] += gsum
            gy = dy * gain
            dx = r * (gy - xr * (_head_sums(xr * gy) * (1.0 / HEAD_DIM)))
            dh_ref[:, start:start + LANES] = dx.astype(BF16)
        dh_ref[:, 640:768] = dva_ref[...].astype(BF16)
        dh_ref[:, 768:1280] = (dqb_ref[...] * scale).astype(BF16)
        dh_ref[:, 1280:1408] = dkb_ref[...].astype(BF16)
        dh_ref[:, 1408:1536] = dvb_ref[...].astype(BF16)

    sd = jax.ShapeDtypeStruct
    return pl.pallas_call(
        body, grid=(t // tm,),
        in_specs=[_row_spec(tm, IN_COLS), _row_spec(tm, LANES), _row_spec(tm, LANES), _par_spec(LANES), _par_spec(LANES),
                  _row_spec(tm, Q_W), _row_spec(tm, KV_W), _row_spec(tm, KV_W),
                  _row_spec(tm, Q_W), _row_spec(tm, KV_W), _row_spec(tm, KV_W)],
        out_specs=[_row_spec(tm, IN_COLS), _par_spec(LANES), _par_spec(LANES)],
        out_shape=[sd((t, IN_COLS), BF16), sd((1, LANES), F32), sd((1, LANES), F32)],
        compiler_params=_cparams(("arbitrary",)), name="prep_bwd",
    )(h, cos_t, sin_t, qn, kn, dqa, dka, dva, dqb, dkb, dvb)


def _outnorm_fwd(oa, ob, ga, gb, tm=512):
    t = oa.shape[0]

    def body(oa_ref, ob_ref, ga_ref, gb_ref, y_ref):
        for o_ref, g_ref, start in ((oa_ref, ga_ref, 0), (ob_ref, gb_ref, Q_W)):
            x = o_ref[...]
            r = lax.rsqrt(jnp.mean(x * x, axis=-1, keepdims=True) + RMS_EPS)
            y_ref[:, start:start + Q_W] = (x * r * g_ref[...]).astype(BF16)

    return pl.pallas_call(
        body, grid=(t // tm,),
        in_specs=[_row_spec(tm, Q_W), _row_spec(tm, Q_W), _par_spec(Q_W), _par_spec(Q_W)],
        out_specs=_row_spec(tm, D_MODEL), out_shape=jax.ShapeDtypeStruct((t, D_MODEL), BF16),
        compiler_params=_cparams(("parallel",)), name="outnorm_fwd",
    )(oa, ob, ga, gb)


def _outnorm_bwd(dy, oa, ob, ga, gb, tm=512):
    t = oa.shape[0]

    def body(dy_ref, oa_ref, ob_ref, ga_ref, gb_ref, doa_ref, dob_ref, dga_ref, dgb_ref):
        @pl.when(pl.program_id(0) == 0)
        def _():
            dga_ref[...] = jnp.zeros_like(dga_ref)
            dgb_ref[...] = jnp.zeros_like(dgb_ref)

        for o_ref, g_ref, do_ref, dg_ref, start in ((oa_ref, ga_ref, doa_ref, dga_ref, 0),
                                                    (ob_ref, gb_ref, dob_ref, dgb_ref, Q_W)):
            x = o_ref[...]
            d = dy_ref[:, start:start + Q_W]
            r = lax.rsqrt(jnp.mean(x * x, axis=-1, keepdims=True) + RMS_EPS)
            xr = x * r
            dg_ref[...] += jnp.sum(d * xr, axis=0, keepdims=True)
            gy = d * g_ref[...]
            do_ref[...] = r * (gy - xr * jnp.mean(xr * gy, axis=-1, keepdims=True))

    sd = jax.ShapeDtypeStruct
    return pl.pallas_call(
        body, grid=(t // tm,),
        in_specs=[_row_spec(tm, D_MODEL), _row_spec(tm, Q_W), _row_spec(tm, Q_W), _par_spec(Q_W), _par_spec(Q_W)],
        out_specs=[_row_spec(tm, Q_W), _row_spec(tm, Q_W), _par_spec(Q_W), _par_spec(Q_W)],
        out_shape=[sd((t, Q_W), F32), sd((t, Q_W), F32), sd((1, Q_W), F32), sd((1, Q_W), F32)],
        compiler_params=_cparams(("arbitrary",)), name="outnorm_bwd",
    )(dy, oa, ob, ga, gb)


def _ln_bwd(d, z, g, tm=512):
    t = z.shape[0]

    def body(d_ref, z_ref, g_ref, dz_ref, dzb_ref, dg_ref, db_ref):
        @pl.when(pl.program_id(0) == 0)
        def _():
            dg_ref[...] = jnp.zeros_like(dg_ref)
            db_ref[...] = jnp.zeros_like(db_ref)

        zz = z_ref[...]
        dd = d_ref[...]
        mu = jnp.mean(zz, axis=-1, keepdims=True)
        zc = zz - mu
        r = lax.rsqrt(jnp.mean(zc * zc, axis=-1, keepdims=True) + LN_EPS)
        xh = zc * r
        dg_ref[...] += jnp.sum(dd * xh, axis=0, keepdims=True)
        db_ref[...] += jnp.sum(dd, axis=0, keepdims=True)
        dxh = dd * g_ref[...]
        dz = r * (dxh - jnp.mean(dxh, axis=-1, keepdims=True) - xh * jnp.mean(dxh * xh, axis=-1, keepdims=True))
        dz_ref[...] = dz
        dzb_ref[...] = dz.astype(BF16)

    sd = jax.ShapeDtypeStruct
    return pl.pallas_call(
        body, grid=(t // tm,),
        in_specs=[_row_spec(tm, D_MODEL), _row_spec(tm, D_MODEL), _par_spec(D_MODEL)],
        out_specs=[_row_spec(tm, D_MODEL), _row_spec(tm, D_MODEL), _par_spec(D_MODEL), _par_spec(D_MODEL)],
        out_shape=[sd((t, D_MODEL), F32), sd((t, D_MODEL), BF16), sd((1, D_MODEL), F32), sd((1, D_MODEL), F32)],
        compiler_params=_cparams(("arbitrary",)), name="ln_bwd",
    )(d, z, g)


def _loss_grad(y, tgt, tm=512):
    t = y.shape[0]
    nsteps = t // tm

    def body(y_ref, t_ref, dy_ref, loss_ref, acc):
        i = pl.program_id(0)

        @pl.when(i == 0)
        def _():
            acc[...] = jnp.zeros_like(acc)

        e = y_ref[...] - t_ref[...]
        dy_ref[...] = e * (1.0 / D_MODEL)
        acc[...] += jnp.sum(e * e, axis=0, keepdims=True)

        @pl.when(i == nsteps - 1)
        def _():
            tot = jnp.sum(acc[...], axis=-1, keepdims=True) * (0.5 / D_MODEL)
            loss_ref[...] = jnp.broadcast_to(tot, loss_ref.shape)

    sd = jax.ShapeDtypeStruct
    return pl.pallas_call(
        body, grid=(nsteps,),
        in_specs=[_row_spec(tm, D_MODEL), _row_spec(tm, D_MODEL)],
        out_specs=[_row_spec(tm, D_MODEL), _par_spec(LANES)],
        out_shape=[sd((t, D_MODEL), F32), sd((1, LANES), F32)],
        scratch_shapes=[pltpu.VMEM((1, D_MODEL), F32)],
        compiler_params=_cparams(("arbitrary",)), name="loss_grad",
    )(y, tgt)


_GELU_C = math.sqrt(2.0 / math.pi)
_GELU_K = 0.044715
HALO = 16


def _gelu_parts(x):
    x2 = x * x
    th = jnp.tanh(x * (_GELU_C + (_GELU_C * _GELU_K) * x2))
    a = 0.5 + 0.5 * th
    dact = a + (0.5 * x) * (1.0 - th * th) * (_GELU_C + (3.0 * _GELU_C * _GELU_K) * x2)
    return x * a, dact


def _halo_specs(tm, t, shift=0):
    last = t // HALO - 1
    cur = pl.BlockSpec((None, tm, FF_SH), lambda j, i: (j + shift, i, 0))
    prev = pl.BlockSpec((None, HALO, FF_SH), lambda j, i: (j + shift, jnp.maximum(i * (tm // HALO) - 1, 0), 0))
    nxt = pl.BlockSpec((None, HALO, FF_SH), lambda j, i: (j + shift, jnp.minimum((i + 1) * (tm // HALO), last), 0))
    return [prev, cur, nxt]


def _gate_up_glu(x, wgu, cw, tm=512):
    t, kd = x.shape
    nsteps = t // tm
    last = t // HALO - 1

    def body(xp_ref, x_ref, xn_ref, w_ref, cw_ref, gu_ref, h_ref):
        i = pl.program_id(0)
        xc = x_ref[...]
        xp = jnp.where(i == 0, jnp.zeros_like(xp_ref[...]), xp_ref[...])
        xn = jnp.where(i == nsteps - 1, jnp.zeros_like(xn_ref[...]), xn_ref[...])
        xe = jnp.concatenate([xp, xc, xn], axis=0)
        te = tm + 2 * HALO
        mid = slice(HALO, HALO + tm)
        for j in range(N_SHARD):
            ge = _dot(xe, w_ref[j], _NN).astype(BF16)
            u = _dot(xc, w_ref[j + N_SHARD], _NN).astype(BF16)
            gu_ref[j] = ge[mid]
            gu_ref[j + N_SHARD] = u
            gf = ge.astype(F32)
            cwj = cw_ref[j]
            gc = (cwj[3:4, :] + pltpu.roll(gf, 1, 0) * cwj[0:1, :] + gf * cwj[1:2, :]
                  + pltpu.roll(gf, te - 1, 0) * cwj[2:3, :])
            act, _ = _gelu_parts(gc[mid])
            h_ref[j] = (act * u.astype(F32)).astype(BF16)

    sd = jax.ShapeDtypeStruct
    return pl.pallas_call(
        body, grid=(nsteps,),
        in_specs=[pl.BlockSpec((HALO, kd), lambda i: (jnp.maximum(i * (tm // HALO) - 1, 0), 0)),
                  pl.BlockSpec((tm, kd), lambda i: (i, 0)),
                  pl.BlockSpec((HALO, kd), lambda i: (jnp.minimum((i + 1) * (tm // HALO), last), 0)),
                  pl.BlockSpec(wgu.shape, lambda i: (0, 0, 0)), pl.BlockSpec(cw.shape, lambda i: (0, 0, 0))],
        out_specs=[pl.BlockSpec((2 * N_SHARD, tm, FF_SH), lambda i: (0, i, 0)),
                   pl.BlockSpec((N_SHARD, tm, FF_SH), lambda i: (0, i, 0))],
        out_shape=[sd((2 * N_SHARD, t, FF_SH), BF16), sd((N_SHARD, t, FF_SH), BF16)],
        compiler_params=_cparams(("parallel",)), name="gate_up_glu",
    )(x, x, x, wgu, cw)


def _ffn_mid_bwd(gu, dz, wd, cw, tm=1024):
    t = gu.shape[1]
    tm = min(tm, t)
    nsteps = t // tm
    te = tm + 2 * HALO
    kd = dz.shape[1]

    def body(gp_ref, g_ref, gn_ref, up_ref, u_ref, un_ref, dp_ref, d_ref, dn_ref, wd_ref, cw_ref, dgu_ref, st_ref):
        i = pl.program_id(1)

        @pl.when(i == 0)
        def _():
            st_ref[...] = jnp.zeros_like(st_ref)

        def ext(p_ref, c_ref, n_ref, dtype=F32):
            prev = jnp.where(i == 0, jnp.zeros_like(p_ref[...]), p_ref[...]).astype(dtype)
            nxt = jnp.where(i == nsteps - 1, jnp.zeros_like(n_ref[...]), n_ref[...]).astype(dtype)
            return jnp.concatenate([prev, c_ref[...].astype(dtype), nxt], axis=0)

        eg = ext(gp_ref, g_ref, gn_ref)
        eu = ext(up_ref, u_ref, un_ref)
        ed = _dot(ext(dp_ref, d_ref, dn_ref, BF16), wd_ref[...], _NT)
        w0, w1, w2 = cw_ref[0:1, :], cw_ref[1:2, :], cw_ref[2:3, :]
        g_m1 = pltpu.roll(eg, 1, 0)
        g_p1 = pltpu.roll(eg, te - 1, 0)
        gc = cw_ref[3:4, :] + g_m1 * w0 + eg * w1 + g_p1 * w2
        act, dact = _gelu_parts(gc)
        dgc = ed * eu * dact
        dg = pltpu.roll(dgc, te - 1, 0) * w0 + dgc * w1 + pltpu.roll(dgc, 1, 0) * w2
        mid = slice(HALO, HALO + tm)
        dgu_ref[0] = dg[mid].astype(BF16)
        dgu_ref[1] = (ed * act)[mid].astype(BF16)
        sel = dgc[mid]
        parts = [jnp.sum(sel, axis=0, keepdims=True),
                 jnp.sum(sel * g_m1[mid], axis=0, keepdims=True),
                 jnp.sum(sel * eg[mid], axis=0, keepdims=True),
                 jnp.sum(sel * g_p1[mid], axis=0, keepdims=True)]
        r8 = lax.broadcasted_iota(jnp.int32, (8, FF_SH), 0)
        upd = jnp.zeros((8, FF_SH), F32)
        for k, p in enumerate(parts):
            upd = upd + jnp.where(r8 == k, p, 0.0)
        st_ref[...] += upd

    sd = jax.ShapeDtypeStruct
    last = t // HALO - 1
    dz_specs = [pl.BlockSpec((HALO, kd), lambda j, i: (jnp.maximum(i * (tm // HALO) - 1, 0), 0)),
                pl.BlockSpec((tm, kd), lambda j, i: (i, 0)),
                pl.BlockSpec((HALO, kd), lambda j, i: (jnp.minimum((i + 1) * (tm // HALO), last), 0))]
    return pl.pallas_call(
        body, grid=(N_SHARD, nsteps),
        in_specs=_halo_specs(tm, t) + _halo_specs(tm, t, N_SHARD) + dz_specs
        + [pl.BlockSpec((None, FF_SH, kd), lambda j, i: (j, 0, 0)), pl.BlockSpec((None, 8, FF_SH), lambda j, i: (j, 0, 0))],
        out_specs=[pl.BlockSpec((2, None, tm, FF_SH), lambda j, i: (0, j, i, 0)),
                   pl.BlockSpec((None, 8, FF_SH), lambda j, i: (j, 0, 0))],
        out_shape=[sd((2, N_SHARD, t, FF_SH), BF16), sd((N_SHARD, 8, FF_SH), F32)],
        compiler_params=_cparams(("parallel", "arbitrary")), name="ffn_mid_bwd",
    )(gu, gu, gu, gu, gu, gu, dz, dz, dz, wd, cw)


def _stack_heads(src_ref, dst_ref, tq):
    lo = lax.broadcasted_iota(jnp.int32, (tq, LANES), 1) < HEAD_DIM
    for i in range(4):
        blk = src_ref[:, LANES * i:LANES * (i + 1)].astype(dst_ref.dtype)
        zero = jnp.zeros_like(blk)
        dst_ref[tq * i:tq * (i + 1), :] = jnp.where(lo, blk, zero)
        dst_ref[tq * (4 + i):tq * (5 + i), :] = jnp.where(lo, zero, blk)


def _gattn_fwd(q, k, v, gather=(), tq=128, tk=2048):
    t = q.shape[0]
    tk = min(tk, t)
    nq, nk, r = t // tq, t // tk, 8 * tq
    ng = len(gather)

    def body(*refs):
        q_ref, k_ref, v_ref = refs[:3]
        o_ref, lse_ref = refs[3 + ng:5 + ng]
        qst, m_s, l_s, acct = refs[5 + 2 * ng:9 + 2 * ng]
        if ng:
            ex = _gather_exchange(refs[3:3 + ng], refs[5 + ng:5 + 2 * ng], *refs[9 + 2 * ng:])
            pl.when(pl.program_id(0) == 0)(ex.start)
        lo_rows = lax.broadcasted_iota(jnp.int32, (LANES, tq), 0) < HEAD_DIM
        for i in range(4):
            bt = q_ref[LANES * i:LANES * (i + 1), :]
            qst[:, tq * i:tq * (i + 1)] = jnp.where(lo_rows, bt, jnp.zeros_like(bt))
            qst[:, tq * (4 + i):tq * (5 + i)] = jnp.where(lo_rows, jnp.zeros_like(bt), bt)
        m_s[...] = jnp.full_like(m_s, NEG)

        def max_step(j, carry):
            off = pl.multiple_of(j * tk, tk)
            st = _dot(k_ref[pl.ds(off, tk), :], qst[...], _NN)
            m_s[...] = jnp.maximum(m_s[...], jnp.max(st.reshape(tk // 8, 8, r), axis=0))
            return carry

        lax.fori_loop(0, nk, max_step, 0, unroll=2)
        m_row = jnp.max(m_s[...], axis=0, keepdims=True)
        l_s[...] = jnp.zeros_like(l_s)
        acct[...] = jnp.zeros_like(acct)

        def sum_step(j, carry):
            off = pl.multiple_of(j * tk, tk)
            st = _dot(k_ref[pl.ds(off, tk), :], qst[...], _NN)
            pt = jnp.exp(st - m_row)
            l_s[...] += jnp.sum(pt.reshape(tk // 8, 8, r), axis=0)
            acct[...] += _dot(v_ref[j], pt, _NN)
            return carry

        lax.fori_loop(0, nk, sum_step, 0, unroll=2)
        l_row = jnp.sum(l_s[...], axis=0, keepdims=True)
        ot = acct[...] / l_row
        for i in range(4):
            pair_t = jnp.where(lo_rows, ot[:, tq * i:tq * (i + 1)], ot[:, tq * (4 + i):tq * (5 + i)])
            o_ref[:, LANES * i:LANES * (i + 1)] = pair_t.T
        lse_ref[...] = m_row + jnp.log(l_row)
        if ng:
            pl.when(pl.program_id(0) == nq - 1)(ex.wait)

    sd = jax.ShapeDtypeStruct
    vt3 = v.reshape(nk, tk, KV_W).transpose(0, 2, 1)
    return pl.pallas_call(
        body, grid=(nq,),
        in_specs=[pl.BlockSpec((Q_W, tq), lambda i: (0, i)), _par_spec(KV_W, t),
                  pl.BlockSpec((nk, KV_W, tk), lambda i: (0, 0, 0))] + [_ANY] * ng,
        out_specs=[_row_spec(tq, Q_W), pl.BlockSpec((None, 1, r), lambda i: (i, 0, 0))] + [_ANY] * ng,
        out_shape=[sd((t, Q_W), F32), sd((nq, 1, r), F32)] + _gathered_shapes(gather),
        scratch_shapes=[pltpu.VMEM((LANES, r), BF16), pltpu.VMEM((8, r), F32), pltpu.VMEM((8, r), F32),
                        pltpu.VMEM((LANES, r), F32)] + (_exchange_sems(ng) if ng else []),
        compiler_params=_cparams(("arbitrary",) if ng else ("parallel",)),
        name="gattn_fwd_gather" if ng else "gattn_fwd",
    )(q.T, k, vt3, *gather)


def _gattn_bwd(q, k, v, o, do, lse, scatter=None, tq=128, tk=512):
    t = q.shape[0]
    tk = min(tk, t)
    nq, nk, r = t // tq, t // tk, 8 * tq
    items, sgrads = scatter if scatter else ((), ())
    ns = len(sgrads)
    slot_shapes = []
    for j, (o_idx, _) in enumerate(items):
        if o_idx == len(slot_shapes):
            slot_shapes += _slot_shapes([sgrads[j]])
    nslots = len(slot_shapes)

    n_in, n_scr = 8, 6
    kt3 = k.reshape(nk, tk, KV_W).transpose(0, 2, 1)

    def body(*refs):
        q_ref, k_ref, v_ref, kt_ref, o_ref, do_ref, lse_ref, qt_ref = refs[:n_in]
        dq_ref, dk_ref, dv_ref = refs[n_in + ns:n_in + 3 + ns]
        scr = n_in + 3 + ns + nslots
        qs, dos, qst, dost, dlt_row, dqt = refs[scr:scr + n_scr]
        if ns:
            ex = _scatter_exchange(items, refs[n_in:n_in + ns], refs[n_in + 3 + ns:scr], *refs[scr + n_scr:])
            pl.when(pl.program_id(0) == 0)(ex.start)

        @pl.when(pl.program_id(0) == 0)
        def _():
            dk_ref[...] = jnp.zeros_like(dk_ref)
            dv_ref[...] = jnp.zeros_like(dv_ref)

        _stack_heads(q_ref, qs, tq)
        _stack_heads(do_ref, dos, tq)
        lo_rows = lax.broadcasted_iota(jnp.int32, (LANES, tq), 0) < HEAD_DIM
        for i in range(4):
            lo, hi = slice(tq * i, tq * (i + 1)), slice(tq * (4 + i), tq * (5 + i))
            cols = slice(LANES * i, LANES * (i + 1))
            qt = qt_ref[cols, :]
            qst[:, lo] = jnp.where(lo_rows, qt, jnp.zeros_like(qt))
            qst[:, hi] = jnp.where(lo_rows, jnp.zeros_like(qt), qt)
            bt = do_ref[:, cols].T
            dost[:, lo] = jnp.where(lo_rows, bt, 0.0).astype(BF16)
            dost[:, hi] = jnp.where(lo_rows, 0.0, bt).astype(BF16)
            prod_t = (do_ref[:, cols] * o_ref[:, cols]).T
            dlt_row[:, lo] = jnp.sum(prod_t[:HEAD_DIM], axis=0, keepdims=True)
            dlt_row[:, hi] = jnp.sum(prod_t[HEAD_DIM:], axis=0, keepdims=True)
        lse_row = lse_ref[...]
        dqt[...] = jnp.zeros_like(dqt)

        def step(j, carry):
            off = pl.multiple_of(j * tk, tk)
            kc = k_ref[pl.ds(off, tk), :]
            vc = v_ref[pl.ds(off, tk), :]
            p = jnp.exp(_dot(kc, qst[...], _NN) - lse_row)
            dp = _dot(vc, dost[...], _NN)
            ds = (p * (dp - dlt_row[...])).astype(BF16)
            dk_ref[pl.ds(off, tk), :] += _dot(ds, qs[...], _NN)
            dv_ref[pl.ds(off, tk), :] += _dot(p, dos[...], _NN)
            dqt[...] += _dot(kt_ref[j], ds, _NN)
            return carry

        lax.fori_loop(0, nk, step, 0, unroll=4)
        for i in range(4):
            pair_t = jnp.where(lo_rows, dqt[:, tq * i:tq * (i + 1)], dqt[:, tq * (4 + i):tq * (5 + i)])
            dq_ref[:, LANES * i:LANES * (i + 1)] = pair_t.T
        if ns:
            pl.when(pl.program_id(0) == nq - 1)(ex.wait)

    sd = jax.ShapeDtypeStruct
    return pl.pallas_call(
        body, grid=(nq,),
        in_specs=[_row_spec(tq, Q_W), _par_spec(KV_W, t), _par_spec(KV_W, t),
                  pl.BlockSpec((nk, KV_W, tk), lambda i: (0, 0, 0)), _row_spec(tq, Q_W), _row_spec(tq, Q_W),
                  pl.BlockSpec((None, 1, r), lambda i: (i, 0, 0)), pl.BlockSpec((Q_W, tq), lambda i: (0, i))]
        + [_ANY] * ns,
        out_specs=[_row_spec(tq, Q_W), _par_spec(KV_W, t), _par_spec(KV_W, t)] + [_ANY] * nslots,
        out_shape=[sd((t, Q_W), F32), sd((t, KV_W), F32), sd((t, KV_W), F32)] + slot_shapes,
        scratch_shapes=[pltpu.VMEM((r, LANES), BF16), pltpu.VMEM((r, LANES), BF16), pltpu.VMEM((LANES, r), BF16),
                        pltpu.VMEM((LANES, r), BF16), pltpu.VMEM((1, r), F32),
                        pltpu.VMEM((LANES, r), F32)] + (_exchange_sems(ns) if ns else []),
        compiler_params=_cparams(("arbitrary",)), name="gattn_bwd_scatter" if ns else "gattn_bwd",
    )(q, k, v, kt3, o, do, lse, q.T, *sgrads)


_WQ = Q_BLOCK
_WK = 3 * Q_BLOCK
_WR = 8 * _WQ


def _pairs_transposed(src_ref, dst, tq):
    lo_rows = lax.broadcasted_iota(jnp.int32, (LANES, tq), 0) < HEAD_DIM
    for i in range(4):
        bt = src_ref[:, LANES * i:LANES * (i + 1)].astype(F32).T
        dst[:, tq * i:tq * (i + 1)] = jnp.where(lo_rows, bt, 0.0).astype(BF16)
        dst[:, tq * (4 + i):tq * (5 + i)] = jnp.where(lo_rows, 0.0, bt).astype(BF16)


def _pairs_from_transposed(halves, dst_ref, tq):
    for i in range(4):
        pair_t = jnp.concatenate([h[:, tq * i:tq * (i + 1)] for h in halves], axis=0)
        dst_ref[:, LANES * i:LANES * (i + 1)] = pair_t.T.astype(dst_ref.dtype)


def _kv_quadrants(tq):
    return [(slice(HEAD_DIM * kv, HEAD_DIM * (kv + 1)), slice(4 * tq * kv, 4 * tq * (kv + 1))) for kv in range(2)]


def _wattn_scores_t(kw, qst, bias_ref, n, t):
    kabs = (n - 1) * _WQ + lax.broadcasted_iota(jnp.int32, (_WK, 1), 0)
    st = _dot(kw, qst[...], _NN) + bias_ref[...]
    return jnp.where((kabs >= 0) & (kabs < t), st, NEG)


def _window_t(ref3, n):
    return jnp.concatenate([ref3[n], ref3[n + 1], ref3[n + 2]], axis=1)


def _blocks_transposed(ap):
    return ap.reshape(ap.shape[0] // _WQ, _WQ, KV_W).transpose(0, 2, 1)


def _wattn_fwd(q, kp, vp, bias_t, sink):
    t = q.shape[0]
    nq = t // _WQ
    tp = t + 2 * _WQ
    vpt = _blocks_transposed(vp)

    def body(q_ref, k_ref, vt_ref, b_ref, sk_ref, o_ref, lse_ref, qst):
        n = pl.program_id(0)
        _pairs_transposed(q_ref, qst, _WQ)
        kw = k_ref[pl.ds(pl.multiple_of(n * _WQ, _WQ), _WK), :]
        st = _wattn_scores_t(kw, qst, b_ref, n, t)
        sk = sk_ref[...]
        m = jnp.maximum(jnp.max(st, axis=0, keepdims=True), sk)
        pt = jnp.exp(st - m)
        l = jnp.sum(pt, axis=0, keepdims=True) + jnp.exp(sk - m)
        vwt = _window_t(vt_ref, n)
        halves = [_dot(vwt[rows, :], pt[:, cols], _NN) / l[:, cols] for rows, cols in _kv_quadrants(_WQ)]
        _pairs_from_transposed(halves, o_ref, _WQ)
        lse_ref[...] = m + jnp.log(l)

    sd = jax.ShapeDtypeStruct
    return pl.pallas_call(
        body, grid=(nq,),
        in_specs=[_row_spec(_WQ, Q_W), _par_spec(KV_W, tp), pl.BlockSpec(vpt.shape, lambda i: (0, 0, 0)),
                  _par_spec(_WR, _WK), _par_spec(_WR)],
        out_specs=[_row_spec(_WQ, Q_W), pl.BlockSpec((None, 1, _WR), lambda i: (i, 0, 0))],
        out_shape=[sd((t, Q_W), F32), sd((nq, 1, _WR), F32)],
        scratch_shapes=[pltpu.VMEM((LANES, _WR), BF16)],
        compiler_params=_cparams(("parallel",)), name="wattn_fwd",
    )(q, kp, vpt, bias_t, sink)


def _wattn_bwd(q, kp, vp, bias_t, sink, o, do, lse):
    t = q.shape[0]
    nq = t // _WQ
    tp = t + 2 * _WQ
    kpt = _blocks_transposed(kp)

    def body(q_ref, k_ref, v_ref, kt_ref, b_ref, sk_ref, o_ref, do_ref, lse_ref, dq_ref, dk_ref, dv_ref, db_ref,
             dsk_ref, qs, dos, qst, dost):
        n = pl.program_id(0)

        @pl.when(n == 0)
        def _():
            dk_ref[...] = jnp.zeros_like(dk_ref)
            dv_ref[...] = jnp.zeros_like(dv_ref)
            db_ref[...] = jnp.zeros_like(db_ref)
            dsk_ref[...] = jnp.zeros_like(dsk_ref)

        _stack_heads(q_ref, qs, _WQ)
        _stack_heads(do_ref, dos, _WQ)
        _pairs_transposed(q_ref, qst, _WQ)
        _pairs_transposed(do_ref, dost, _WQ)
        delta = []
        for i in range(4):
            cols = slice(LANES * i, LANES * (i + 1))
            prod_t = (do_ref[:, cols] * o_ref[:, cols]).T
            delta.append((jnp.sum(prod_t[:HEAD_DIM], axis=0, keepdims=True),
                          jnp.sum(prod_t[HEAD_DIM:], axis=0, keepdims=True)))
        dlt = jnp.concatenate([d[0] for d in delta] + [d[1] for d in delta], axis=1)
        off = pl.multiple_of(n * _WQ, _WQ)
        kw = k_ref[pl.ds(off, _WK), :]
        vw = v_ref[pl.ds(off, _WK), :]
        lse_v = lse_ref[...]
        pt = jnp.exp(_wattn_scores_t(kw, qst, b_ref, n, t) - lse_v)
        dpt = _dot(vw, dost[...], _NN)
        ds = pt * (dpt - dlt)
        db_ref[...] += ds
        dsk_ref[...] -= jnp.exp(sk_ref[...] - lse_v) * dlt
        dsb = ds.astype(BF16)
        dk_ref[pl.ds(off, _WK), :] += _dot(dsb, qs[...], _NN)
        dv_ref[pl.ds(off, _WK), :] += _dot(pt, dos[...], _NN)
        kwt = _window_t(kt_ref, n)
        halves = [_dot(kwt[rows, :], dsb[:, cols], _NN) for rows, cols in _kv_quadrants(_WQ)]
        _pairs_from_transposed(halves, dq_ref, _WQ)

    sd = jax.ShapeDtypeStruct
    qb = _row_spec(_WQ, Q_W)
    return pl.pallas_call(
        body, grid=(nq,),
        in_specs=[qb, _par_spec(KV_W, tp), _par_spec(KV_W, tp), pl.BlockSpec(kpt.shape, lambda i: (0, 0, 0)),
                  _par_spec(_WR, _WK), _par_spec(_WR), qb, qb, pl.BlockSpec((None, 1, _WR), lambda i: (i, 0, 0))],
        out_specs=[qb, _par_spec(KV_W, tp), _par_spec(KV_W, tp), _par_spec(_WR, _WK), _par_spec(_WR)],
        out_shape=[sd((t, Q_W), F32), sd((tp, KV_W), F32), sd((tp, KV_W), F32), sd((_WK, _WR), F32), sd((1, _WR), F32)],
        scratch_shapes=[pltpu.VMEM((_WR, LANES), BF16), pltpu.VMEM((_WR, LANES), BF16), pltpu.VMEM((LANES, _WR), BF16),
                        pltpu.VMEM((LANES, _WR), BF16)],
        compiler_params=_cparams(("arbitrary",)), name="wattn_bwd",
    )(q, kp, vp, kpt, bias_t, sink, o, do, lse)


def _bias_bucket_reduce(db0, db1, bucket):
    def body(a_ref, b_ref, bk_ref, o_ref):
        d = a_ref[...] + b_ref[...]
        bk = bk_ref[...]
        lane = lax.broadcasted_iota(jnp.int32, (1, LANES), 1)
        out = jnp.zeros((1, LANES), F32)
        for b in range(N_BUCKETS):
            tot = jnp.sum(jnp.sum(jnp.where(bk == b, d, 0.0), axis=-1, keepdims=True), axis=0, keepdims=True)
            out = out + jnp.where(lane == b, tot, 0.0)
        o_ref[...] = out

    hb = pl.BlockSpec((None, _WQ, _WK), lambda h: (h, 0, 0))
    return pl.pallas_call(
        body, grid=(8,), in_specs=[hb, hb, pl.BlockSpec((_WQ, _WK), lambda h: (0, 0))],
        out_specs=pl.BlockSpec((None, 1, LANES), lambda h: (h, 0, 0)),
        out_shape=jax.ShapeDtypeStruct((8, 1, LANES), F32),
        compiler_params=_cparams(("parallel",)), name="bias_bucket_reduce",
    )(db0.reshape(8, _WQ, _WK), db1.reshape(8, _WQ, _WK), bucket)


def _rope_tables(t):
    rows_n = t // GRID_W
    row = jnp.repeat(jnp.arange(rows_n, dtype=F32), GRID_W)
    col = jnp.tile(jnp.arange(GRID_W, dtype=F32), rows_n)
    half = HEAD_DIM // 2
    inv_freq = ROPE_THETA ** (-jnp.arange(0, half, 2, dtype=F32) / half)
    ang = jnp.concatenate([row[:, None] * inv_freq, col[:, None] * inv_freq], axis=-1)
    cos, sin = jnp.cos(ang), jnp.sin(ang)
    c64 = jnp.repeat(cos, 2, axis=-1)
    s64 = jnp.stack([-sin, sin], axis=-1).reshape(t, HEAD_DIM)
    return jnp.tile(c64, (1, 2)), jnp.tile(s64, (1, 2))


def _t5_bucket(rel):
    half = N_BUCKETS // 2
    max_exact = half // 2
    bucket = jnp.where(rel > 0, half, 0)
    rp = jnp.abs(rel)
    rpf = jnp.maximum(rp, 1).astype(jnp.float32)
    large = max_exact + (jnp.log(rpf / max_exact) / math.log(MAX_DISTANCE / max_exact)
                         * (half - max_exact)).astype(jnp.int32)
    large = jnp.minimum(large, half - 1)
    return bucket + jnp.where(rp < max_exact, rp, large)


def _window_tables(rel_bias):
    qpos = jnp.arange(_WQ, dtype=jnp.int32)
    kpos = jnp.arange(_WK, dtype=jnp.int32) - _WQ
    rel = kpos[None, :] - qpos[:, None]
    bucket = _t5_bucket(rel)
    bias = jnp.zeros((8, _WQ, _WK), F32)
    for b in range(N_BUCKETS):
        bias = jnp.where((bucket == b)[None], rel_bias[b][:, None, None], bias)
    bias = jnp.where((jnp.abs(rel) <= WINDOW)[None], bias, NEG)
    return bias.reshape(_WR, _WK).T, bucket


def _pad_rows(a):
    return jnp.pad(a, ((_WQ, _WQ), (0, 0)))


def _layer_fwd(x, p, tabs, gather=None):
    cos_t, sin_t, bias = tabs
    h, qa, ka, va, qb, kb, vb = _in_proj_prep(x, p["win"], cos_t, sin_t, p["qn"], p["kn"])
    if gather is None:
        oa, lse_a = _gattn_fwd(qa, ka, va)
    else:
        oa, lse_a, *gathered = _gattn_fwd(qa, ka, va, gather=gather[0])
        p = gather[1](gathered)
    kbp, vbp = _pad_rows(kb), _pad_rows(vb)
    ob, lse_b = _wattn_fwd(qb, kbp, vbp, bias, p["sink"])
    y = _outnorm_fwd(oa, ob, p["ga"], p["gb"])
    z1, x1, x1b = _mm_reduce(y[None], p["wout"][None], _NN, F32, "out_proj", res=x, res_scale=ALPHA,
                             ln=(p["ln1g"], p["ln1b"]))
    gu, hdn = _gate_up_glu(x1b, p["wgu"], p["cw"])
    z2, x2, _ = _mm_reduce(hdn, p["wd"], _NN, F32, "down_proj", res=x1, res_scale=ALPHA, ln=(p["ln2g"], p["ln2b"]))
    saved = dict(x=x, h=h, qa=qa, ka=ka, va=va, qb=qb, kbp=kbp, vbp=vbp, oa=oa, ob=ob, lse_a=lse_a, lse_b=lse_b,
                 y=y, z1=z1, x1b=x1b, gu=gu, hdn=hdn, z2=z2)
    return x2, saved


def _block_grads(g, names=("w_in", "w_out", "w_gate", "w_up", "w_down")):
    make = dict(
        w_in=lambda: _col_blocks(_in_cols_to_pairs(g["win"], _from_pairs), IN_SH),
        w_out=lambda: _mix_rows_to_pairs(g["wout"], _from_pairs).reshape(N_SHARD, OUT_SH, D_MODEL),
        w_gate=lambda: g["wg"], w_up=lambda: g["wu"], w_down=lambda: g["wd"])
    return [make[n]() for n in names]


def _layer_bwd(dx2, p, s, tabs, layer=0, pending=None):
    cos_t, sin_t, bias = tabs
    t = dx2.shape[0]
    dz2, dz2b, dln2g, dln2b = _ln_bwd(dx2, s["z2"], p["ln2g"])
    dwd = _mm_tn_blocks(s["hdn"], dz2b, "down_dw")
    dgu, stats = _ffn_mid_bwd(s["gu"], dz2b, p["wd"], p["cw"])
    dgu = dgu.reshape(2 * N_SHARD, t, FF_SH)
    dx1 = _mm_reduce(dgu, p["wgu"], _NT, F32, "gate_up_dx", res=dz2, res_scale=ALPHA)
    dwg = _mm_tn_blocks(dgu, s["x1b"], "gate_dw", blk=0)
    dwu = _mm_tn_blocks(dgu, s["x1b"], "up_dw", blk=1)
    dz1, dz1b, dln1g, dln1b = _ln_bwd(dx1, s["z1"], p["ln1g"])
    dy = _mm_nt(dz1b, p["wout"], F32, "out_dx")
    dwout = _mm_tn(s["y"], dz1b, "out_dw")
    doa, dob, dga, dgb = _outnorm_bwd(dy, s["oa"], s["ob"], p["ga"], p["gb"])
    slots = None
    if pending is None:
        dqa, dka, dva = _gattn_bwd(s["qa"], s["ka"], s["va"], s["oa"], doa, s["lse_a"])
    else:
        mine = _block_grads(dict(wout=dwout, wg=dwg, wu=dwu, wd=dwd), ("w_out", "w_gate", "w_up", "w_down"))
        todo = list(pending) + [(o + 1, layer, g) for o, g in enumerate(mine)]
        dqa, dka, dva, *slots = _gattn_bwd(s["qa"], s["ka"], s["va"], s["oa"], doa, s["lse_a"],
                                           scatter=([(o, l) for o, l, _ in todo], [g for _, _, g in todo]))
    dqb, dkbp, dvbp, dbias, dsink = _wattn_bwd(s["qb"], s["kbp"], s["vbp"], bias, p["sink"], s["ob"], dob, s["lse_b"])
    dkb = lax.slice_in_dim(dkbp, _WQ, _WQ + t, axis=0)
    dvb = lax.slice_in_dim(dvbp, _WQ, _WQ + t, axis=0)
    dh, dqn, dkn = _prep_bwd(s["h"], cos_t, sin_t, p["qn"], p["kn"], dqa, dka, dva, dqb, dkb, dvb)
    dx = _mm_nt(dh, p["win"], F32, "in_dx", res=dz1, res_scale=ALPHA)
    dwin = _mm_tn(s["x"], dh, "in_dw")
    grads = dict(win=dwin, wout=dwout, wg=dwg, wu=dwu, wd=dwd, stats=stats, qn=dqn, kn=dkn, ga=dga, gb=dgb,
                 ln1g=dln1g, ln1b=dln1b, ln2g=dln2g, ln2b=dln2b, bias=dbias, sink=dsink, slots=slots)
    return dx, grads


def _prep_layer_params(l, win, wout, wg, wu, wd, cw, q_norm, k_norm, sink, out_norm_a, out_norm_b, conv_b,
                       ln1_g, ln1_b, ln2_g, ln2_b):
    win_full = win.transpose(1, 0, 2).reshape(D_MODEL, IN_COLS)
    row = lambda v: v.reshape(1, -1)
    late = {}
    if wout is not None:
        late = dict(
            wout=_mix_rows_to_pairs(wout.reshape(D_MODEL, D_MODEL)), wgu=jnp.concatenate([wg, wu], axis=0), wd=wd,
            cw=jnp.pad(cw, ((0, 0), (0, 5), (0, 0)))
            + jnp.pad(conv_b[l].reshape(N_SHARD, 1, FF_SH), ((0, 0), (3, 4), (0, 0))))
    return dict(
        late, win=_in_cols_to_pairs(win_full),
        qn=row(jnp.tile(q_norm[l], 2)), kn=row(jnp.tile(k_norm[l], 2)),
        ga=row(_to_pairs(out_norm_a[l], 0)), gb=row(_to_pairs(out_norm_b[l], 0)),
        ln1g=row(ln1_g[l]), ln1b=row(ln1_b[l]), ln2g=row(ln2_g[l]), ln2b=row(ln2_b[l]),
        sink=jnp.repeat(sink[l], _WQ).reshape(1, _WR))


def _local_step(x, tgt, params, rel_bias, gather=None, scatter=False):
    t = x.shape[0]
    cos_t, sin_t = _rope_tables(t)
    bias, bucket = _window_tables(rel_bias)
    tabs = (cos_t, sin_t, bias)
    saved = []
    for l in range(DEPTH):
        x, s = _layer_fwd(x, params[l], tabs, gather if l == 0 else None)
        saved.append(s)
    dx, loss = _loss_grad(x, tgt)
    grads = [None] * DEPTH
    for l in reversed(range(DEPTH)):
        pending = None
        if scatter and l == 0:
            pending = [(o, 1, g) for o, g in enumerate(_block_grads(grads[1]))]
        dx, grads[l] = _layer_bwd(dx, params[l], saved[l], tabs, l, pending)
    dbucket = _bias_bucket_reduce(grads[0]["bias"].T, grads[1]["bias"].T, bucket)
    return loss, dx, grads, dbucket


_ANY = pl.BlockSpec(memory_space=pl.ANY)
_MESH = pl.DeviceIdType.MESH


def _mesh_pos():
    return lax.axis_index("x"), lax.axis_index("y"), lax.axis_index("c")


def _other_chips(x, y):
    return [(1 - x, y), (x, 1 - y), (1 - x, 1 - y)]


class _Exchange:
    def __init__(self, local, sends, recvs):
        self.local, self.sends, self.recvs = local, sends, recvs

    def start(self):
        for cp in self.local + self.sends:
            cp.start()

    def wait(self):
        for cp in self.recvs:
            cp.wait_recv()
        for cp in self.sends:
            cp.wait_send()
        for cp in self.local:
            cp.wait()


def _exchange_sems(n):
    return [pltpu.SemaphoreType.DMA((n, 3)), pltpu.SemaphoreType.DMA((n, 3)), pltpu.SemaphoreType.DMA((n,))]


def _gather_exchange(ins, outs, send, recv, loc):
    x, y, c = _mesh_pos()
    me = 2 * x + y
    chips = _other_chips(x, y)

    def remote(i, k, block):
        px, py = chips[k]
        return pltpu.make_async_remote_copy(ins[i], outs[i].at[block], send.at[i, k], recv.at[i, k],
                                            device_id=(px, py, c), device_id_type=_MESH)

    n = len(ins)
    local = [pltpu.make_async_copy(ins[i], outs[i].at[me], loc.at[i]) for i in range(n)]
    sends = [remote(i, k, me) for i in range(n) for k in range(3)]
    recvs = [remote(i, k, 2 * chips[k][0] + chips[k][1]) for i in range(n) for k in range(3)]
    return _Exchange(local, sends, recvs)


def _scatter_exchange(items, ins, outs, send, recv, loc):
    x, y, c = _mesh_pos()
    me = 2 * x + y
    chips = _other_chips(x, y)

    def remote(j, k):
        o, l = items[j]
        px, py = chips[k]
        return pltpu.make_async_remote_copy(ins[j].at[2 * px + py], outs[o].at[k, l], send.at[j, k], recv.at[j, k],
                                            device_id=(px, py, c), device_id_type=_MESH)

    local = [pltpu.make_async_copy(ins[j].at[me], outs[o].at[3, l], loc.at[j]) for j, (o, l) in enumerate(items)]
    sends = [remote(j, k) for j in range(len(items)) for k in range(3)]
    return _Exchange(local, sends, sends)


def _gathered_shapes(shards):
    return [jax.ShapeDtypeStruct((N_SHARD,) + s.shape, s.dtype) for s in shards]


def _slot_shapes(blocks):
    return [jax.ShapeDtypeStruct((N_SHARD, DEPTH) + g.shape[1:], g.dtype) for g in blocks]


def _gather_shards(shards):
    n = len(shards)

    def body(*refs):
        ex = _gather_exchange(refs[:n], refs[n:2 * n], *refs[2 * n:])
        ex.start()
        ex.wait()

    return pl.pallas_call(
        body, in_specs=[_ANY] * n, out_specs=[_ANY] * n, out_shape=_gathered_shapes(shards),
        scratch_shapes=_exchange_sems(n), name="gather_weights",
    )(*shards)


def _scatter_into(items, grads, slots):
    n, ns = len(grads), len(slots)

    def body(*refs):
        ex = _scatter_exchange(items, refs[:n], refs[n + ns:n + 2 * ns], *refs[n + 2 * ns:])
        ex.start()
        ex.wait()

    return pl.pallas_call(
        body, in_specs=[_ANY] * (n + ns), out_specs=[_ANY] * ns,
        out_shape=[jax.ShapeDtypeStruct(s.shape, s.dtype) for s in slots],
        input_output_aliases={n + i: i for i in range(ns)},
        scratch_shapes=_exchange_sems(n), name="scatter_grads",
    )(*grads, *slots)


def _swap_with_sibling(parts):
    n = len(parts)

    def body(*refs):
        ins, outs = refs[:n], refs[n:2 * n]
        send, recv = refs[2 * n:]
        x, y, c = _mesh_pos()
        copies = [pltpu.make_async_remote_copy(ins[i], outs[i], send.at[i], recv.at[i], device_id=(x, y, 1 - c),
                                               device_id_type=_MESH) for i in range(n)]
        for cp in copies:
            cp.start()
        for cp in copies:
            cp.wait_recv()
        for cp in copies:
            cp.wait_send()

    return pl.pallas_call(
        body, in_specs=[_ANY] * n, out_specs=[_ANY] * n,
        out_shape=[jax.ShapeDtypeStruct(p.shape, p.dtype) for p in parts],
        scratch_shapes=[pltpu.SemaphoreType.DMA((n,)), pltpu.SemaphoreType.DMA((n,))],
        name="swap_sibling",
    )(*parts)


N_DEV = 8


def _allreduce_small(packed):
    rows = packed.shape[0]

    def body(in_ref, out_ref, buf, send, recv, loc):
        x, y, c = _mesh_pos()
        me = 4 * x + 2 * y + c
        own = pltpu.make_async_copy(in_ref, buf.at[me], loc)
        own.start()

        def remote(m, block):
            peer = (x ^ (m >> 2), y ^ ((m >> 1) & 1), c ^ (m & 1))
            return pltpu.make_async_remote_copy(in_ref, buf.at[block], send.at[m - 1], recv.at[m - 1],
                                                device_id=peer, device_id_type=_MESH)

        sends = [remote(m, me) for m in range(1, N_DEV)]
        for cp in sends:
            cp.start()
        for m in range(1, N_DEV):
            remote(m, me ^ m).wait_recv()
        for cp in sends:
            cp.wait_send()
        own.wait()
        tot = buf[0]
        for d in range(1, N_DEV):
            tot = tot + buf[d]
        out_ref[...] = tot

    vm = pl.BlockSpec(memory_space=pltpu.VMEM)
    return pl.pallas_call(
        body, in_specs=[vm], out_specs=vm, out_shape=jax.ShapeDtypeStruct((rows, LANES), F32),
        scratch_shapes=[pltpu.VMEM((N_DEV, rows, LANES), F32), pltpu.SemaphoreType.DMA((N_DEV - 1,)),
                        pltpu.SemaphoreType.DMA((N_DEV - 1,)), pltpu.SemaphoreType.DMA(())],
        name="allreduce_small",
    )(packed)


def _shard_rows(r):
    return r // 2 if r % 32 == 0 else r


def _sum_slots(slots):
    _, _, r, cdim = slots.shape
    tr = _shard_rows(r)

    def body(a_ref, b_ref, c_ref, d_ref, o_ref):
        up = lambda ref: ref[...].astype(F32)
        o_ref[...] = ((up(d_ref) + up(a_ref)) + up(b_ref)) + up(c_ref)

    def spec(k):
        return pl.BlockSpec((None, None, tr, cdim), lambda l, i: (k, l, i, 0))

    return pl.pallas_call(
        body, grid=(DEPTH, r // tr), in_specs=[spec(0), spec(1), spec(2), spec(3)],
        out_specs=pl.BlockSpec((None, tr, cdim), lambda l, i: (l, i, 0)),
        out_shape=jax.ShapeDtypeStruct((DEPTH, r, cdim), F32),
        compiler_params=_cparams(("parallel", "parallel")), name="sum_slots",
    )(slots, slots, slots, slots)


def _adamw_math(w, g, m, v):
    m = ADAM_B1 * m + (1.0 - ADAM_B1) * g
    v = ADAM_B2 * v + (1.0 - ADAM_B2) * (g * g)
    m_hat = m / (1.0 - ADAM_B1 ** ADAM_STEP)
    v_hat = v / (1.0 - ADAM_B2 ** ADAM_STEP)
    delta = -ADAM_LR * (m_hat / (jnp.sqrt(v_hat) + ADAM_EPS) + ADAM_WD * w)
    return delta, m, v


def _adamw_big(ga, gb, w, m, v):
    _, r, cdim = w.shape
    tr = _shard_rows(r)

    def body(ga_ref, gb_ref, w_ref, m_ref, v_ref, g_out, d_out, m_out, v_out):
        g = ga_ref[...] + gb_ref[...]
        d, mn, vn = _adamw_math(w_ref[...], g, m_ref[...], v_ref[...])
        g_out[...] = g
        d_out[...] = d
        m_out[...] = mn
        v_out[...] = vn

    spec = pl.BlockSpec((None, tr, cdim), lambda l, i: (l, i, 0))
    shp = jax.ShapeDtypeStruct(w.shape, F32)
    return pl.pallas_call(
        body, grid=(DEPTH, r // tr), in_specs=[spec] * 5, out_specs=[spec] * 4, out_shape=[shp] * 4,
        compiler_params=_cparams(("parallel", "parallel")), name="adamw_big",
    )(ga, gb, w, m, v)


def _adamw_small(ws, gs, ms, vs):
    n = len(ws)

    def body(*refs):
        w_r, g_r, m_r, v_r = (refs[k * n:(k + 1) * n] for k in range(4))
        d_o, m_o, v_o = (refs[(4 + k) * n:(5 + k) * n] for k in range(3))
        for i in range(n):
            d, mn, vn = _adamw_math(w_r[i][...], g_r[i][...], m_r[i][...], v_r[i][...])
            d_o[i][...] = d
            m_o[i][...] = mn
            v_o[i][...] = vn

    vm = pl.BlockSpec(memory_space=pltpu.VMEM)
    shp = [jax.ShapeDtypeStruct(w.shape, F32) for w in ws]
    outs = pl.pallas_call(
        body, in_specs=[vm] * (4 * n), out_specs=[vm] * (3 * n), out_shape=shp * 3, name="adamw_small",
    )(*ws, *gs, *ms, *vs)
    return outs[:n], outs[n:2 * n], outs[2 * n:]


def _tile_rows(a):
    a = a.reshape(-1, LANES)
    pad = (-a.shape[0]) % 8
    return jnp.pad(a, ((0, pad), (0, 0))) if pad else a


_SMALL_LAYER_PARTS = (("qn", 8), ("kn", 8), ("sink", 8), ("ga", 8), ("gb", 8), ("ln1g", 8), ("ln1b", 8),
                      ("ln2g", 8), ("ln2b", 8), ("stats", N_SHARD * 8 * FF_SH // LANES))
_SMALL_HEAD_ROWS = 16
_SMALL_LAYER_ROWS = sum(r for _, r in _SMALL_LAYER_PARTS)


def _pack_small(loss, dbucket, grads):
    parts = [_tile_rows(loss), _tile_rows(dbucket)]
    for l in range(DEPTH):
        parts += [_tile_rows(grads[l][name]) for name, _ in _SMALL_LAYER_PARTS]
    return jnp.concatenate(parts, axis=0)


def _unpack_small(tot, chip):
    out = dict(loss=tot[0, 0], rel_bias=tot[8:16, :N_BUCKETS].T)
    per = {name: [] for name, _ in _SMALL_LAYER_PARTS}
    for l in range(DEPTH):
        base = _SMALL_HEAD_ROWS + l * _SMALL_LAYER_ROWS
        for name, rows in _SMALL_LAYER_PARTS:
            per[name].append(tot[base:base + rows])
            base += rows
    fold = lambda v: v[0, :HEAD_DIM] + v[0, HEAD_DIM:]
    out["q_norm"] = jnp.stack([fold(v) for v in per["qn"]])
    out["k_norm"] = jnp.stack([fold(v) for v in per["kn"]])
    out["sink"] = jnp.stack([jnp.sum(v, axis=1) for v in per["sink"]])
    out["out_norm_a"] = jnp.stack([_from_pairs(v[:4].reshape(Q_W), 0) for v in per["ga"]])
    out["out_norm_b"] = jnp.stack([_from_pairs(v[:4].reshape(Q_W), 0) for v in per["gb"]])
    for name, key in (("ln1_g", "ln1g"), ("ln1_b", "ln1b"), ("ln2_g", "ln2g"), ("ln2_b", "ln2b")):
        out[name] = jnp.stack([v.reshape(D_MODEL) for v in per[key]])
    stats = [v.reshape(N_SHARD, 8, FF_SH) for v in per["stats"]]
    out["conv_b"] = jnp.stack([s[:, 0, :].reshape(D_FF) for s in stats])
    out["conv_w"] = jnp.stack([lax.dynamic_index_in_dim(s, chip, 0, keepdims=False)[1:4] for s in stats])
    return out


_WEIGHTS = ("rel_bias", "w_in", "q_norm", "k_norm", "sink", "out_norm_a", "out_norm_b", "w_out", "ln1_g", "ln1_b",
            "w_gate", "w_up", "conv_w", "conv_b", "w_down", "ln2_g", "ln2_b")
_BIG = ("w_in", "w_out", "w_gate", "w_up", "w_down")
_SMALL = tuple(n for n in _WEIGHTS if n not in _BIG)


def _col_blocks(g, n):
    return g.reshape(g.shape[0], N_SHARD, n).transpose(1, 0, 2)


def kernel(x, rel_bias, w_in, q_norm, k_norm, sink, out_norm_a, out_norm_b, w_out, ln1_g, ln1_b, w_gate, w_up, conv_w, conv_b, w_down, ln2_g, ln2_b, loss_target, m_rel_bias, m_w_in, m_q_norm, m_k_norm, m_sink, m_out_norm_a, m_out_norm_b, m_w_out, m_ln1_g, m_ln1_b, m_w_gate, m_w_up, m_conv_w, m_conv_b, m_w_down, m_ln2_g, m_ln2_b, v_rel_bias, v_w_in, v_q_norm, v_k_norm, v_sink, v_out_norm_a, v_out_norm_b, v_w_out, v_ln1_g, v_ln1_b, v_w_gate, v_w_up, v_conv_w, v_conv_b, v_w_down, v_ln2_g, v_ln2_b):
    w = dict(rel_bias=rel_bias, w_in=w_in, q_norm=q_norm, k_norm=k_norm, sink=sink, out_norm_a=out_norm_a,
             out_norm_b=out_norm_b, w_out=w_out, ln1_g=ln1_g, ln1_b=ln1_b, w_gate=w_gate, w_up=w_up, conv_w=conv_w,
             conv_b=conv_b, w_down=w_down, ln2_g=ln2_g, ln2_b=ln2_b)
    m = dict(rel_bias=m_rel_bias, w_in=m_w_in, q_norm=m_q_norm, k_norm=m_k_norm, sink=m_sink, out_norm_a=m_out_norm_a,
             out_norm_b=m_out_norm_b, w_out=m_w_out, ln1_g=m_ln1_g, ln1_b=m_ln1_b, w_gate=m_w_gate, w_up=m_w_up,
             conv_w=m_conv_w, conv_b=m_conv_b, w_down=m_w_down, ln2_g=m_ln2_g, ln2_b=m_ln2_b)
    v = dict(rel_bias=v_rel_bias, w_in=v_w_in, q_norm=v_q_norm, k_norm=v_k_norm, sink=v_sink, out_norm_a=v_out_norm_a,
             out_norm_b=v_out_norm_b, w_out=v_w_out, ln1_g=v_ln1_g, ln1_b=v_ln1_b, w_gate=v_w_gate, w_up=v_w_up,
             conv_w=v_conv_w, conv_b=v_conv_b, w_down=v_w_down, ln2_g=v_ln2_g, ln2_b=v_ln2_b)
    chip = 2 * lax.axis_index("x") + lax.axis_index("y")

    small_w = (q_norm, k_norm, sink, out_norm_a, out_norm_b, conv_b, ln1_g, ln1_b, ln2_g, ln2_b)
    (win0,) = _gather_shards([w_in[0].astype(BF16)])
    later = ([w[name][0].astype(BF16) for name in _BIG[1:]] + [w[name][1].astype(BF16) for name in _BIG] + [conv_w])
    params = [_prep_layer_params(0, win0, None, None, None, None, None, *small_w), None]

    def finish(g):
        wout0, wg0, wu0, wd0, win1, wout1, wg1, wu1, wd1, cw_all = g
        params[0] = _prep_layer_params(0, win0, wout0, wg0, wu0, wd0, cw_all[:, 0], *small_w)
        params[1] = _prep_layer_params(1, win1, wout1, wg1, wu1, wd1, cw_all[:, 1], *small_w)
        return params[0]

    loss, dx, grads, dbucket = _local_step(x[0], loss_target[0], params, rel_bias, gather=(later, finish),
                                           scatter=True)

    small = _unpack_small(_allreduce_small(_pack_small(loss, dbucket, grads)), chip)

    slots = list(grads[0]["slots"])
    slots[0] = _scatter_into([(0, 0)], _block_grads(grads[0], ("w_in",)), [slots[0]])[0]
    partial = [_sum_slots(s) for s in slots]
    other = _swap_with_sibling(partial)

    grad, delta, new_m, new_v = {}, {}, {}, {}
    for i, name in enumerate(_BIG):
        fix = (lambda a: jnp.swapaxes(a, 1, 2)) if name in ("w_gate", "w_up") else (lambda a: a)
        outs = _adamw_big(partial[i], other[i], fix(w[name]), fix(m[name]), fix(v[name]))
        grad[name], delta[name], new_m[name], new_v[name] = [fix(o) for o in outs]
    flat2 = lambda a: a.reshape(-1, a.shape[-1])
    ds, ms, vs = _adamw_small([flat2(w[n]) for n in _SMALL], [flat2(small[n]) for n in _SMALL],
                              [flat2(m[n]) for n in _SMALL], [flat2(v[n]) for n in _SMALL])
    for i, name in enumerate(_SMALL):
        grad[name] = small[name]
        delta[name] = ds[i].reshape(w[name].shape)
        new_m[name] = ms[i].reshape(w[name].shape)
        new_v[name] = vs[i].reshape(w[name].shape)

    return (small["loss"], dx[None], *[grad[n] for n in _WEIGHTS], *[delta[n] for n in _WEIGHTS],
            *[new_m[n] for n in _WEIGHTS], *[new_v[n] for n in _WEIGHTS])
```

```python
import math

import jax
import jax.numpy as jnp
from jax import lax
from jax.experimental import pallas as pl
from jax.experimental.pallas import tpu as pltpu

F32 = jnp.float32
BF16 = jnp.bfloat16

D_MODEL = 1024
DEPTH = 2
HEAD_DIM = 64
Q_W = 512
KV_W = 128
IN_COLS = 2 * (Q_W + 2 * KV_W)
N_SHARD = 4
IN_SH = IN_COLS // N_SHARD
OUT_SH = D_MODEL // N_SHARD
D_FF = 2816
FF_SH = D_FF // N_SHARD
Q_BLOCK = 128
WINDOW = 128
N_BUCKETS = 32
MAX_DISTANCE = 128
GRID_W = 64
ROPE_THETA = 10000.0
ALPHA = (2.0 * DEPTH) ** 0.25
RMS_EPS = 1e-6
LN_EPS = 1e-5
NEG = -1e30
LANES = 128
VMEM_LIMIT = 56 * 1024 * 1024

ADAM_LR = 0.001
ADAM_B1 = 0.9
ADAM_B2 = 0.999
ADAM_EPS = 1e-08
ADAM_WD = 0.01
ADAM_STEP = 10

_NN = (((1,), (0,)), ((), ()))
_NT = (((1,), (1,)), ((), ()))
_TN = (((0,), (0,)), ((), ()))


def _dot(a, b, dims):
    return lax.dot_general(a.astype(BF16), b.astype(BF16), dims, preferred_element_type=F32)


def _cparams(sem, vmem=VMEM_LIMIT):
    return pltpu.CompilerParams(dimension_semantics=sem, vmem_limit_bytes=vmem)


def _regroup(a, axis, n_outer, n_inner):
    shp = a.shape
    a = a.reshape(shp[:axis] + (n_outer, n_inner, HEAD_DIM) + shp[axis + 1:])
    return jnp.swapaxes(a, axis, axis + 1).reshape(shp)


def _to_pairs(a, axis):
    return _regroup(a, axis, 2, 4)


def _from_pairs(a, axis):
    return _regroup(a, axis, 4, 2)


def _in_cols_to_pairs(w, fn=_to_pairs):
    return jnp.concatenate([fn(w[..., :Q_W], w.ndim - 1), w[..., Q_W:Q_W + 2 * KV_W],
                            fn(w[..., Q_W + 2 * KV_W:2 * Q_W + 2 * KV_W], w.ndim - 1),
                            w[..., 2 * Q_W + 2 * KV_W:]], axis=-1)


def _mix_rows_to_pairs(w, fn=_to_pairs):
    return fn(w.reshape(2, Q_W, w.shape[-1]), 1).reshape(w.shape)


def _matmul(a, b, *, dims, grid, a_spec, b_spec, o_spec, out_shape, acc_shape, name, res=None,
            res_spec=None, res_scale=1.0):
    nk = grid[-1]
    kax = len(grid) - 1

    def body(*refs):
        if res is None:
            a_ref, b_ref, o_ref, acc = refs
            r_ref = None
        else:
            a_ref, b_ref, r_ref, o_ref, acc = refs
        k = pl.program_id(kax)

        @pl.when(k == 0)
        def _():
            acc[...] = jnp.zeros_like(acc)

        acc[...] += _dot(a_ref[...], b_ref[...], dims)

        @pl.when(k == nk - 1)
        def _():
            o = acc[...]
            if r_ref is not None:
                o = o + res_scale * r_ref[...]
            o_ref[...] = o.astype(o_ref.dtype)

    in_specs = [a_spec, b_spec] + ([res_spec] if res is not None else [])
    args = (a, b) + ((res,) if res is not None else ())
    sem = ("parallel",) * kax + ("arbitrary",)
    return pl.pallas_call(
        body, grid=grid, in_specs=in_specs, out_specs=o_spec, out_shape=out_shape,
        scratch_shapes=[pltpu.VMEM(acc_shape, F32)], compiler_params=_cparams(sem), name=name,
    )(*args)


def _mm_nt(a, b, out_dtype, name, tm=512, res=None, res_scale=1.0):
    m, kd = a.shape
    n = b.shape[0]
    return _matmul(
        a, b, dims=_NT, grid=(m // tm, 1),
        a_spec=pl.BlockSpec((tm, kd), lambda i, k: (i, 0)),
        b_spec=pl.BlockSpec((n, kd), lambda i, k: (0, 0)),
        o_spec=pl.BlockSpec((tm, n), lambda i, k: (i, 0)),
        out_shape=jax.ShapeDtypeStruct((m, n), out_dtype), acc_shape=(tm, n), name=name,
        res=res, res_spec=pl.BlockSpec((tm, n), lambda i, k: (i, 0)), res_scale=res_scale)


def _mm_tn(a, b, name, tk=1024, tn=None, out_dtype=BF16):
    t, m = a.shape
    n = b.shape[1]
    tn = n if tn is None else tn
    tk = min(tk, t)
    return _matmul(
        a, b, dims=_TN, grid=(n // tn, t // tk),
        a_spec=pl.BlockSpec((tk, m), lambda j, k: (k, 0)),
        b_spec=pl.BlockSpec((tk, tn), lambda j, k: (k, j)),
        o_spec=pl.BlockSpec((m, tn), lambda j, k: (0, j)),
        out_shape=jax.ShapeDtypeStruct((m, n), out_dtype), acc_shape=(m, tn), name=name)


def _blocked_n(w, dims):
    return w.shape[2] if dims == _NN else w.shape[1]


def _mm_reduce(a, w, dims, out_dtype, name, tm=512, res=None, res_scale=1.0, ln=None):
    nb, m, kd = a.shape
    n = _blocked_n(w, dims)
    n_in = 2 + (res is not None) + (2 if ln else 0)

    def body(*refs):
        a_ref, w_ref = refs[0], refs[1]
        acc = _dot(a_ref[0], w_ref[0], dims)
        for j in range(1, nb):
            acc = acc + _dot(a_ref[j], w_ref[j], dims)
        if res is not None:
            acc = acc + res_scale * refs[2][...]
        refs[n_in][...] = acc.astype(out_dtype)
        if ln:
            g_ref, b_ref = refs[n_in - 2], refs[n_in - 1]
            zc = acc - jnp.mean(acc, axis=-1, keepdims=True)
            r = lax.rsqrt(jnp.mean(zc * zc, axis=-1, keepdims=True) + LN_EPS)
            y = zc * r * g_ref[...] + b_ref[...]
            refs[n_in + 1][...] = y
            refs[n_in + 2][...] = y.astype(BF16)

    row = pl.BlockSpec((tm, n), lambda i: (i, 0))
    par = pl.BlockSpec((1, n), lambda i: (0, 0))
    sd = jax.ShapeDtypeStruct
    out = pl.pallas_call(
        body, grid=(m // tm,),
        in_specs=[pl.BlockSpec((nb, tm, kd), lambda i: (0, i, 0)), pl.BlockSpec(w.shape, lambda i: (0, 0, 0))]
        + ([row] if res is not None else []) + ([par, par] if ln else []),
        out_specs=[row] * (3 if ln else 1),
        out_shape=[sd((m, n), out_dtype)] + ([sd((m, n), F32), sd((m, n), BF16)] if ln else []),
        compiler_params=_cparams(("parallel",)), name=name,
    )(a, w, *((res,) if res is not None else ()), *(ln or ()))
    return out if ln else out[0]


def _mm_tn_blocks(a, b, name, blk=0, nb=N_SHARD, tk=1024, out_dtype=BF16):
    a3, b3 = a.ndim == 3, b.ndim == 3
    t, m, n = a.shape[-2], a.shape[-1], b.shape[-1]
    tk = min(tk, t)
    nsteps = t // tk

    def spec(blocked, width):
        if blocked:
            return pl.BlockSpec((nb, tk, width), lambda k: (blk, k, 0))
        return pl.BlockSpec((tk, width), lambda k: (k, 0))

    def body(a_ref, b_ref, o_ref, acc):
        k = pl.program_id(0)

        @pl.when(k == 0)
        def _():
            acc[...] = jnp.zeros_like(acc)

        for j in range(nb):
            acc[j] += _dot(a_ref[j] if a3 else a_ref[...], b_ref[j] if b3 else b_ref[...], _TN)

        @pl.when(k == nsteps - 1)
        def _():
            o_ref[...] = acc[...].astype(o_ref.dtype)

    return pl.pallas_call(
        body, grid=(nsteps,), in_specs=[spec(a3, m), spec(b3, n)],
        out_specs=pl.BlockSpec((nb, m, n), lambda k: (0, 0, 0)),
        out_shape=jax.ShapeDtypeStruct((nb, m, n), out_dtype),
        scratch_shapes=[pltpu.VMEM((nb, m, n), F32)],
        compiler_params=_cparams(("arbitrary",)), name=name,
    )(a, b)


def _row_spec(tm, n):
    return pl.BlockSpec((tm, n), lambda i: (i, 0))


def _par_spec(n, rows=1):
    return pl.BlockSpec((rows, n), lambda i: (0, 0))


def _swap_pairs(x):
    lane = lax.broadcasted_iota(jnp.int32, x.shape, 1)
    return jnp.where(lane % 2 == 0, pltpu.roll(x, LANES - 1, 1), pltpu.roll(x, 1, 1))


def _head_sums(v):
    lo = lax.broadcasted_iota(jnp.int32, v.shape, 1) < HEAD_DIM
    s_lo = jnp.sum(jnp.where(lo, v, 0.0), axis=-1, keepdims=True)
    s_hi = jnp.sum(jnp.where(lo, 0.0, v), axis=-1, keepdims=True)
    return jnp.where(lo, s_lo, s_hi)


def _qk_blocks():
    return [(128 * i, True) for i in range(4)] + [(Q_W, False)]


def _in_proj_prep(x, win, cos_t, sin_t, qn, kn, tm=512):
    t, kd = x.shape
    scale = HEAD_DIM ** -0.5
    wide = 2 * LANES

    def body(x_ref, w_ref, c_ref, s_ref, qn_ref, kn_ref, h_ref, qa_ref, ka_ref, va_ref, qb_ref, kb_ref, vb_ref):
        xb = x_ref[...].astype(BF16)
        c = c_ref[...]
        s = s_ref[...]

        def normed(v, gain):
            r = lax.rsqrt(_head_sums(v * v) * (1.0 / HEAD_DIM) + RMS_EPS)
            y = v * r * gain
            return y * c + _swap_pairs(y) * s

        for b in range(IN_COLS // wide):
            hb = _dot(xb, w_ref[:, wide * b:wide * (b + 1)], _NN)
            h_ref[:, wide * b:wide * (b + 1)] = hb
            lo, hi = hb[:, :LANES], hb[:, LANES:]
            if b < 2:
                qa_ref[:, wide * b:wide * b + LANES] = (normed(lo, qn_ref[...]) * scale).astype(BF16)
                qa_ref[:, wide * b + LANES:wide * (b + 1)] = (normed(hi, qn_ref[...]) * scale).astype(BF16)
            elif b == 2:
                ka_ref[...] = normed(lo, kn_ref[...]).astype(BF16)
                va_ref[...] = hi.astype(BF16)
            elif b < 5:
                qb_ref[:, wide * (b - 3):wide * (b - 2)] = (hb * scale).astype(BF16)
            else:
                kb_ref[...] = lo.astype(BF16)
                vb_ref[...] = hi.astype(BF16)

    sd = jax.ShapeDtypeStruct
    return pl.pallas_call(
        body, grid=(t // tm,),
        in_specs=[_row_spec(tm, kd), pl.BlockSpec(win.shape, lambda i: (0, 0)), _row_spec(tm, LANES),
                  _row_spec(tm, LANES), _par_spec(LANES), _par_spec(LANES)],
        out_specs=[_row_spec(tm, IN_COLS), _row_spec(tm, Q_W), _row_spec(tm, KV_W), _row_spec(tm, KV_W),
                   _row_spec(tm, Q_W), _row_spec(tm, KV_W), _row_spec(tm, KV_W)],
        out_shape=[sd((t, IN_COLS), F32), sd((t, Q_W), BF16), sd((t, KV_W), BF16), sd((t, KV_W), BF16),
                   sd((t, Q_W), BF16), sd((t, KV_W), BF16), sd((t, KV_W), BF16)],
        compiler_params=_cparams(("parallel",)), name="in_proj_prep",
    )(x, win, cos_t, sin_t, qn, kn)


def _prep_bwd(h, cos_t, sin_t, qn, kn, dqa, dka, dva, dqb, dkb, dvb, tm=512):
    t = h.shape[0]
    scale = HEAD_DIM ** -0.5

    def body(h_ref, c_ref, s_ref, qn_ref, kn_ref, dqa_ref, dka_ref, dva_ref, dqb_ref, dkb_ref, dvb_ref,
             dh_ref, dqn_ref, dkn_ref):
        @pl.when(pl.program_id(0) == 0)
        def _():
            dqn_ref[...] = jnp.zeros_like(dqn_ref)
            dkn_ref[...] = jnp.zeros_like(dkn_ref)

        c = c_ref[...]
        s = s_ref[...]
        for start, is_q in _qk_blocks():
            x = h_ref[:, start:start + LANES]
            gain = qn_ref[...] if is_q else kn_ref[...]
            d = dqa_ref[:, start:start + LANES] * scale if is_q else dka_ref[...]
            dy = d * c + _swap_pairs(d * s)
            r = lax.rsqrt(_head_sums(x * x) * (1.0 / HEAD_DIM) + RMS_EPS)
            xr = x * r
            gsum = jnp.sum(dy * xr, axis=0, keepdims=True)
            if is_q:
                dqn_ref[...] += gsum
            else:
                dkn_ref[...] += gsum
            gy = dy * gain
            dx = r * (gy - xr * (_head_sums(xr * gy) * (1.0 / HEAD_DIM)))
            dh_ref[:, start:start + LANES] = dx.astype(BF16)
        dh_ref[:, 640:768] = dva_ref[...].astype(BF16)
        dh_ref[:, 768:1280] = (dqb_ref[...] * scale).astype(BF16)
        dh_ref[:, 1280:1408] = dkb_ref[...].astype(BF16)
        dh_ref[:, 1408:1536] = dvb_ref[...].astype(BF16)

    sd = jax.ShapeDtypeStruct
    return pl.pallas_call(
        body, grid=(t // tm,),
        in_specs=[_row_spec(tm, IN_COLS), _row_spec(tm, LANES), _row_spec(tm, LANES), _par_spec(LANES), _par_spec(LANES),
                  _row_spec(tm, Q_W), _row_spec(tm, KV_W), _row_spec(tm, KV_W),
                  _row_spec(tm, Q_W), _row_spec(tm, KV_W), _row_spec(tm, KV_W)],
        out_specs=[_row_spec(tm, IN_COLS), _par_spec(LANES), _par_spec(LANES)],
        out_shape=[sd((t, IN_COLS), BF16), sd((1, LANES), F32), sd((1, LANES), F32)],
        compiler_params=_cparams(("arbitrary",)), name="prep_bwd",
    )(h, cos_t, sin_t, qn, kn, dqa, dka, dva, dqb, dkb, dvb)


def _outnorm_fwd(oa, ob, ga, gb, tm=512):
    t = oa.shape[0]

    def body(oa_ref, ob_ref, ga_ref, gb_ref, y_ref):
        for o_ref, g_ref, start in ((oa_ref, ga_ref, 0), (ob_ref, gb_ref, Q_W)):
            x = o_ref[...]
            r = lax.rsqrt(jnp.mean(x * x, axis=-1, keepdims=True) + RMS_EPS)
            y_ref[:, start:start + Q_W] = (x * r * g_ref[...]).astype(BF16)

    return pl.pallas_call(
        body, grid=(t // tm,),
        in_specs=[_row_spec(tm, Q_W), _row_spec(tm, Q_W), _par_spec(Q_W), _par_spec(Q_W)],
        out_specs=_row_spec(tm, D_MODEL), out_shape=jax.ShapeDtypeStruct((t, D_MODEL), BF16),
        compiler_params=_cparams(("parallel",)), name="outnorm_fwd",
    )(oa, ob, ga, gb)


def _outnorm_bwd(dy, oa, ob, ga, gb, tm=512):
    t = oa.shape[0]

    def body(dy_ref, oa_ref, ob_ref, ga_ref, gb_ref, doa_ref, dob_ref, dga_ref, dgb_ref):
        @pl.when(pl.program_id(0) == 0)
        def _():
            dga_ref[...] = jnp.zeros_like(dga_ref)
            dgb_ref[...] = jnp.zeros_like(dgb_ref)

        for o_ref, g_ref, do_ref, dg_ref, start in ((oa_ref, ga_ref, doa_ref, dga_ref, 0),
                                                    (ob_ref, gb_ref, dob_ref, dgb_ref, Q_W)):
            x = o_ref[...]
            d = dy_ref[:, start:start + Q_W]
            r = lax.rsqrt(jnp.mean(x * x, axis=-1, keepdims=True) + RMS_EPS)
            xr = x * r
            dg_ref[...] += jnp.sum(d * xr, axis=0, keepdims=True)
            gy = d * g_ref[...]
            do_ref[...] = r * (gy - xr * jnp.mean(xr * gy, axis=-1, keepdims=True))

    sd = jax.ShapeDtypeStruct
    return pl.pallas_call(
        body, grid=(t // tm,),
        in_specs=[_row_spec(tm, D_MODEL), _row_spec(tm, Q_W), _row_spec(tm, Q_W), _par_spec(Q_W), _par_spec(Q_W)],
        out_specs=[_row_spec(tm, Q_W), _row_spec(tm, Q_W), _par_spec(Q_W), _par_spec(Q_W)],
        out_shape=[sd((t, Q_W), F32), sd((t, Q_W), F32), sd((1, Q_W), F32), sd((1, Q_W), F32)],
        compiler_params=_cparams(("arbitrary",)), name="outnorm_bwd",
    )(dy, oa, ob, ga, gb)


def _ln_bwd(d, z, g, tm=1024):
    t = z.shape[0]
    tm = min(tm, t)

    def body(d_ref, z_ref, g_ref, dz_ref, dzb_ref, dg_ref, db_ref):
        @pl.when(pl.program_id(0) == 0)
        def _():
            dg_ref[...] = jnp.zeros_like(dg_ref)
            db_ref[...] = jnp.zeros_like(db_ref)

        zz = z_ref[...]
        dd = d_ref[...]
        mu = jnp.mean(zz, axis=-1, keepdims=True)
        zc = zz - mu
        r = lax.rsqrt(jnp.mean(zc * zc, axis=-1, keepdims=True) + LN_EPS)
        xh = zc * r
        dg_ref[...] += jnp.sum(dd * xh, axis=0, keepdims=True)
        db_ref[...] += jnp.sum(dd, axis=0, keepdims=True)
        dxh = dd * g_ref[...]
        dz = r * (dxh - jnp.mean(dxh, axis=-1, keepdims=True) - xh * jnp.mean(dxh * xh, axis=-1, keepdims=True))
        dz_ref[...] = dz
        dzb_ref[...] = dz.astype(BF16)

    sd = jax.ShapeDtypeStruct
    return pl.pallas_call(
        body, grid=(t // tm,),
        in_specs=[_row_spec(tm, D_MODEL), _row_spec(tm, D_MODEL), _par_spec(D_MODEL)],
        out_specs=[_row_spec(tm, D_MODEL), _row_spec(tm, D_MODEL), _par_spec(D_MODEL), _par_spec(D_MODEL)],
        out_shape=[sd((t, D_MODEL), F32), sd((t, D_MODEL), BF16), sd((1, D_MODEL), F32), sd((1, D_MODEL), F32)],
        compiler_params=_cparams(("arbitrary",)), name="ln_bwd",
    )(d, z, g)


def _loss_grad(y, tgt, tm=512):
    t = y.shape[0]
    nsteps = t // tm

    def body(y_ref, t_ref, dy_ref, loss_ref, acc):
        i = pl.program_id(0)

        @pl.when(i == 0)
        def _():
            acc[...] = jnp.zeros_like(acc)

        e = y_ref[...] - t_ref[...]
        dy_ref[...] = e * (1.0 / D_MODEL)
        acc[...] += jnp.sum(e * e, axis=0, keepdims=True)

        @pl.when(i == nsteps - 1)
        def _():
            tot = jnp.sum(acc[...], axis=-1, keepdims=True) * (0.5 / D_MODEL)
            loss_ref[...] = jnp.broadcast_to(tot, loss_ref.shape)

    sd = jax.ShapeDtypeStruct
    return pl.pallas_call(
        body, grid=(nsteps,),
        in_specs=[_row_spec(tm, D_MODEL), _row_spec(tm, D_MODEL)],
        out_specs=[_row_spec(tm, D_MODEL), _par_spec(LANES)],
        out_shape=[sd((t, D_MODEL), F32), sd((1, LANES), F32)],
        scratch_shapes=[pltpu.VMEM((1, D_MODEL), F32)],
        compiler_params=_cparams(("arbitrary",)), name="loss_grad",
    )(y, tgt)


_GELU_C = math.sqrt(2.0 / math.pi)
_GELU_K = 0.044715
HALO = 16


def _gelu_parts(x):
    x2 = x * x
    th = jnp.tanh(x * (_GELU_C + (_GELU_C * _GELU_K) * x2))
    a = 0.5 + 0.5 * th
    dact = a + (0.5 * x) * (1.0 - th * th) * (_GELU_C + (3.0 * _GELU_C * _GELU_K) * x2)
    return x * a, dact


def _halo_specs(tm, t, shift=0):
    last = t // HALO - 1
    cur = pl.BlockSpec((None, tm, FF_SH), lambda j, i: (j + shift, i, 0))
    prev = pl.BlockSpec((None, HALO, FF_SH), lambda j, i: (j + shift, jnp.maximum(i * (tm // HALO) - 1, 0), 0))
    nxt = pl.BlockSpec((None, HALO, FF_SH), lambda j, i: (j + shift, jnp.minimum((i + 1) * (tm // HALO), last), 0))
    return [prev, cur, nxt]


def _gate_up_glu(x, wgu, cw, tm=512):
    t, kd = x.shape
    nsteps = t // tm
    last = t // HALO - 1

    def body(xp_ref, x_ref, xn_ref, w_ref, cw_ref, gu_ref, h_ref):
        i = pl.program_id(0)
        xc = x_ref[...]
        xp = jnp.where(i == 0, jnp.zeros_like(xp_ref[...]), xp_ref[...])
        xn = jnp.where(i == nsteps - 1, jnp.zeros_like(xn_ref[...]), xn_ref[...])
        xe = jnp.concatenate([xp, xc, xn], axis=0)
        te = tm + 2 * HALO
        mid = slice(HALO, HALO + tm)
        for j in range(N_SHARD):
            ge = _dot(xe, w_ref[j], _NN).astype(BF16)
            u = _dot(xc, w_ref[j + N_SHARD], _NN).astype(BF16)
            gu_ref[j] = ge[mid]
            gu_ref[j + N_SHARD] = u
            gf = ge.astype(F32)
            cwj = cw_ref[j]
            gc = (cwj[3:4, :] + pltpu.roll(gf, 1, 0) * cwj[0:1, :] + gf * cwj[1:2, :]
                  + pltpu.roll(gf, te - 1, 0) * cwj[2:3, :])
            act, _ = _gelu_parts(gc[mid])
            h_ref[j] = (act * u.astype(F32)).astype(BF16)

    sd = jax.ShapeDtypeStruct
    return pl.pallas_call(
        body, grid=(nsteps,),
        in_specs=[pl.BlockSpec((HALO, kd), lambda i: (jnp.maximum(i * (tm // HALO) - 1, 0), 0)),
                  pl.BlockSpec((tm, kd), lambda i: (i, 0)),
                  pl.BlockSpec((HALO, kd), lambda i: (jnp.minimum((i + 1) * (tm // HALO), last), 0)),
                  pl.BlockSpec(wgu.shape, lambda i: (0, 0, 0)), pl.BlockSpec(cw.shape, lambda i: (0, 0, 0))],
        out_specs=[pl.BlockSpec((2 * N_SHARD, tm, FF_SH), lambda i: (0, i, 0)),
                   pl.BlockSpec((N_SHARD, tm, FF_SH), lambda i: (0, i, 0))],
        out_shape=[sd((2 * N_SHARD, t, FF_SH), BF16), sd((N_SHARD, t, FF_SH), BF16)],
        compiler_params=_cparams(("parallel",)), name="gate_up_glu",
    )(x, x, x, wgu, cw)


def _ffn_mid_bwd(gu, dz, wd, cw, tm=1024):
    t = gu.shape[1]
    tm = min(tm, t)
    nsteps = t // tm
    te = tm + 2 * HALO
    kd = dz.shape[1]

    def body(gp_ref, g_ref, gn_ref, up_ref, u_ref, un_ref, dp_ref, d_ref, dn_ref, wd_ref, cw_ref, dgu_ref, st_ref):
        i = pl.program_id(1)

        @pl.when(i == 0)
        def _():
            st_ref[...] = jnp.zeros_like(st_ref)

        def ext(p_ref, c_ref, n_ref, dtype=F32):
            prev = jnp.where(i == 0, jnp.zeros_like(p_ref[...]), p_ref[...]).astype(dtype)
            nxt = jnp.where(i == nsteps - 1, jnp.zeros_like(n_ref[...]), n_ref[...]).astype(dtype)
            return jnp.concatenate([prev, c_ref[...].astype(dtype), nxt], axis=0)

        eg = ext(gp_ref, g_ref, gn_ref)
        eu = ext(up_ref, u_ref, un_ref)
        ed = _dot(ext(dp_ref, d_ref, dn_ref, BF16), wd_ref[...], _NT)
        w0, w1, w2 = cw_ref[0:1, :], cw_ref[1:2, :], cw_ref[2:3, :]
        g_m1 = pltpu.roll(eg, 1, 0)
        g_p1 = pltpu.roll(eg, te - 1, 0)
        gc = cw_ref[3:4, :] + g_m1 * w0 + eg * w1 + g_p1 * w2
        act, dact = _gelu_parts(gc)
        dgc = ed * eu * dact
        dg = pltpu.roll(dgc, te - 1, 0) * w0 + dgc * w1 + pltpu.roll(dgc, 1, 0) * w2
        mid = slice(HALO, HALO + tm)
        dgu_ref[0] = dg[mid].astype(BF16)
        dgu_ref[1] = (ed * act)[mid].astype(BF16)
        sel = dgc[mid]
        parts = [jnp.sum(sel, axis=0, keepdims=True),
                 jnp.sum(sel * g_m1[mid], axis=0, keepdims=True),
                 jnp.sum(sel * eg[mid], axis=0, keepdims=True),
                 jnp.sum(sel * g_p1[mid], axis=0, keepdims=True)]
        r8 = lax.broadcasted_iota(jnp.int32, (8, FF_SH), 0)
        upd = jnp.zeros((8, FF_SH), F32)
        for k, p in enumerate(parts):
            upd = upd + jnp.where(r8 == k, p, 0.0)
        st_ref[...] += upd

    sd = jax.ShapeDtypeStruct
    last = t // HALO - 1
    dz_specs = [pl.BlockSpec((HALO, kd), lambda j, i: (jnp.maximum(i * (tm // HALO) - 1, 0), 0)),
                pl.BlockSpec((tm, kd), lambda j, i: (i, 0)),
                pl.BlockSpec((HALO, kd), lambda j, i: (jnp.minimum((i + 1) * (tm // HALO), last), 0))]
    return pl.pallas_call(
        body, grid=(N_SHARD, nsteps),
        in_specs=_halo_specs(tm, t) + _halo_specs(tm, t, N_SHARD) + dz_specs
        + [pl.BlockSpec((None, FF_SH, kd), lambda j, i: (j, 0, 0)), pl.BlockSpec((None, 8, FF_SH), lambda j, i: (j, 0, 0))],
        out_specs=[pl.BlockSpec((2, None, tm, FF_SH), lambda j, i: (0, j, i, 0)),
                   pl.BlockSpec((None, 8, FF_SH), lambda j, i: (j, 0, 0))],
        out_shape=[sd((2, N_SHARD, t, FF_SH), BF16), sd((N_SHARD, 8, FF_SH), F32)],
        compiler_params=_cparams(("parallel", "arbitrary")), name="ffn_mid_bwd",
    )(gu, gu, gu, gu, gu, gu, dz, dz, dz, wd, cw)


def _stack_heads(src_ref, dst_ref, tq):
    lo = lax.broadcasted_iota(jnp.int32, (tq, LANES), 1) < HEAD_DIM
    for i in range(4):
        blk = src_ref[:, LANES * i:LANES * (i + 1)].astype(dst_ref.dtype)
        zero = jnp.zeros_like(blk)
        dst_ref[tq * i:tq * (i + 1), :] = jnp.where(lo, blk, zero)
        dst_ref[tq * (4 + i):tq * (5 + i), :] = jnp.where(lo, zero, blk)


def _gattn_fwd(q, k, v, gather=(), tq=128, tk=2048):
    t = q.shape[0]
    tk = min(tk, t)
    nq, nk, r = t // tq, t // tk, 8 * tq
    ng = len(gather)

    def body(*refs):
        q_ref, k_ref, v_ref = refs[:3]
        o_ref, lse_ref = refs[3 + ng:5 + ng]
        qst, m_s, l_s, acct = refs[5 + 2 * ng:9 + 2 * ng]
        if ng:
            ex = _gather_exchange(refs[3:3 + ng], refs[5 + ng:5 + 2 * ng], *refs[9 + 2 * ng:])
            pl.when(pl.program_id(0) == 0)(ex.start)
        lo_rows = lax.broadcasted_iota(jnp.int32, (LANES, tq), 0) < HEAD_DIM
        for i in range(4):
            bt = q_ref[:, LANES * i:LANES * (i + 1)].astype(F32).T
            qst[:, tq * i:tq * (i + 1)] = jnp.where(lo_rows, bt, 0.0).astype(BF16)
            qst[:, tq * (4 + i):tq * (5 + i)] = jnp.where(lo_rows, 0.0, bt).astype(BF16)
        m_s[...] = jnp.full_like(m_s, NEG)

        def max_step(j, carry):
            off = pl.multiple_of(j * tk, tk)
            st = _dot(k_ref[pl.ds(off, tk), :], qst[...], _NN)
            m_s[...] = jnp.maximum(m_s[...], jnp.max(st.reshape(tk // 8, 8, r), axis=0))
            return carry

        lax.fori_loop(0, nk, max_step, 0, unroll=2)
        m_row = jnp.max(m_s[...], axis=0, keepdims=True)
        l_s[...] = jnp.zeros_like(l_s)
        acct[...] = jnp.zeros_like(acct)

        def sum_step(j, carry):
            off = pl.multiple_of(j * tk, tk)
            st = _dot(k_ref[pl.ds(off, tk), :], qst[...], _NN)
            pt = jnp.exp(st - m_row)
            l_s[...] += jnp.sum(pt.reshape(tk // 8, 8, r), axis=0)
            acct[...] += _dot(v_ref[j], pt, _NN)
            return carry

        lax.fori_loop(0, nk, sum_step, 0, unroll=2)
        l_row = jnp.sum(l_s[...], axis=0, keepdims=True)
        ot = acct[...] / l_row
        for i in range(4):
            pair_t = jnp.where(lo_rows, ot[:, tq * i:tq * (i + 1)], ot[:, tq * (4 + i):tq * (5 + i)])
            o_ref[:, LANES * i:LANES * (i + 1)] = pair_t.T
        lse_ref[...] = m_row + jnp.log(l_row)
        if ng:
            pl.when(pl.program_id(0) == nq - 1)(ex.wait)

    sd = jax.ShapeDtypeStruct
    vt3 = v.reshape(nk, tk, KV_W).transpose(0, 2, 1)
    return pl.pallas_call(
        body, grid=(nq,),
        in_specs=[_row_spec(tq, Q_W), _par_spec(KV_W, t), pl.BlockSpec((nk, KV_W, tk), lambda i: (0, 0, 0))]
        + [_ANY] * ng,
        out_specs=[_row_spec(tq, Q_W), pl.BlockSpec((None, 1, r), lambda i: (i, 0, 0))] + [_ANY] * ng,
        out_shape=[sd((t, Q_W), F32), sd((nq, 1, r), F32)] + _gathered_shapes(gather),
        scratch_shapes=[pltpu.VMEM((LANES, r), BF16), pltpu.VMEM((8, r), F32), pltpu.VMEM((8, r), F32),
                        pltpu.VMEM((LANES, r), F32)] + (_exchange_sems(ng) if ng else []),
        compiler_params=_cparams(("arbitrary",) if ng else ("parallel",)),
        name="gattn_fwd_gather" if ng else "gattn_fwd",
    )(q, k, vt3, *gather)


def _gattn_bwd(q, k, v, o, do, lse, scatter=None, tq=128, tk=512):
    t = q.shape[0]
    tk = min(tk, t)
    nq, nk, r = t // tq, t // tk, 8 * tq
    items, sgrads = scatter if scatter else ((), ())
    ns = len(sgrads)
    slot_shapes = []
    for j, (o_idx, _) in enumerate(items):
        if o_idx == len(slot_shapes):
            slot_shapes += _slot_shapes([sgrads[j]])
    nslots = len(slot_shapes)

    n_in, n_scr = 7, 6
    kt3 = k.reshape(nk, tk, KV_W).transpose(0, 2, 1)

    def body(*refs):
        q_ref, k_ref, v_ref, kt_ref, o_ref, do_ref, lse_ref = refs[:n_in]
        dq_ref, dk_ref, dv_ref = refs[n_in + ns:n_in + 3 + ns]
        scr = n_in + 3 + ns + nslots
        qs, dos, qst, dost, dlt_row, dqt = refs[scr:scr + n_scr]
        if ns:
            ex = _scatter_exchange(items, refs[n_in:n_in + ns], refs[n_in + 3 + ns:scr], *refs[scr + n_scr:])
            pl.when(pl.program_id(0) == 0)(ex.start)

        @pl.when(pl.program_id(0) == 0)
        def _():
            dk_ref[...] = jnp.zeros_like(dk_ref)
            dv_ref[...] = jnp.zeros_like(dv_ref)

        _stack_heads(q_ref, qs, tq)
        _stack_heads(do_ref, dos, tq)
        lo_rows = lax.broadcasted_iota(jnp.int32, (LANES, tq), 0) < HEAD_DIM
        for i in range(4):
            lo, hi = slice(tq * i, tq * (i + 1)), slice(tq * (4 + i), tq * (5 + i))
            cols = slice(LANES * i, LANES * (i + 1))
            for src, dst in ((q_ref, qst), (do_ref, dost)):
                bt = src[:, cols].astype(F32).T
                dst[:, lo] = jnp.where(lo_rows, bt, 0.0).astype(BF16)
                dst[:, hi] = jnp.where(lo_rows, 0.0, bt).astype(BF16)
            prod_t = (do_ref[:, cols] * o_ref[:, cols]).T
            dlt_row[:, lo] = jnp.sum(prod_t[:HEAD_DIM], axis=0, keepdims=True)
            dlt_row[:, hi] = jnp.sum(prod_t[HEAD_DIM:], axis=0, keepdims=True)
        lse_row = lse_ref[...]
        dqt[...] = jnp.zeros_like(dqt)

        def step(j, carry):
            off = pl.multiple_of(j * tk, tk)
            kc = k_ref[pl.ds(off, tk), :]
            vc = v_ref[pl.ds(off, tk), :]
            p = jnp.exp(_dot(kc, qst[...], _NN) - lse_row)
            dp = _dot(vc, dost[...], _NN)
            ds = (p * (dp - dlt_row[...])).astype(BF16)
            dk_ref[pl.ds(off, tk), :] += _dot(ds, qs[...], _NN)
            dv_ref[pl.ds(off, tk), :] += _dot(p, dos[...], _NN)
            dqt[...] += _dot(kt_ref[j], ds, _NN)
            return carry

        lax.fori_loop(0, nk, step, 0, unroll=4)
        for i in range(4):
            pair_t = jnp.where(lo_rows, dqt[:, tq * i:tq * (i + 1)], dqt[:, tq * (4 + i):tq * (5 + i)])
            dq_ref[:, LANES * i:LANES * (i + 1)] = pair_t.T
        if ns:
            pl.when(pl.program_id(0) == nq - 1)(ex.wait)

    sd = jax.ShapeDtypeStruct
    return pl.pallas_call(
        body, grid=(nq,),
        in_specs=[_row_spec(tq, Q_W), _par_spec(KV_W, t), _par_spec(KV_W, t),
                  pl.BlockSpec((nk, KV_W, tk), lambda i: (0, 0, 0)), _row_spec(tq, Q_W), _row_spec(tq, Q_W),
                  pl.BlockSpec((None, 1, r), lambda i: (i, 0, 0))] + [_ANY] * ns,
        out_specs=[_row_spec(tq, Q_W), _par_spec(KV_W, t), _par_spec(KV_W, t)] + [_ANY] * nslots,
        out_shape=[sd((t, Q_W), F32), sd((t, KV_W), F32), sd((t, KV_W), F32)] + slot_shapes,
        scratch_shapes=[pltpu.VMEM((r, LANES), BF16), pltpu.VMEM((r, LANES), BF16), pltpu.VMEM((LANES, r), BF16),
                        pltpu.VMEM((LANES, r), BF16), pltpu.VMEM((1, r), F32),
                        pltpu.VMEM((LANES, r), F32)] + (_exchange_sems(ns) if ns else []),
        compiler_params=_cparams(("arbitrary",)), name="gattn_bwd_scatter" if ns else "gattn_bwd",
    )(q, k, v, kt3, o, do, lse, *sgrads)


_WQ = Q_BLOCK
_WK = 3 * Q_BLOCK
_WR = 8 * _WQ


def _pairs_transposed(src_ref, dst, tq):
    lo_rows = lax.broadcasted_iota(jnp.int32, (LANES, tq), 0) < HEAD_DIM
    for i in range(4):
        bt = src_ref[:, LANES * i:LANES * (i + 1)].astype(F32).T
        dst[:, tq * i:tq * (i + 1)] = jnp.where(lo_rows, bt, 0.0).astype(BF16)
        dst[:, tq * (4 + i):tq * (5 + i)] = jnp.where(lo_rows, 0.0, bt).astype(BF16)


def _pairs_from_transposed(halves, dst_ref, tq):
    for i in range(4):
        pair_t = jnp.concatenate([h[:, tq * i:tq * (i + 1)] for h in halves], axis=0)
        dst_ref[:, LANES * i:LANES * (i + 1)] = pair_t.T.astype(dst_ref.dtype)


def _kv_quadrants(tq):
    return [(slice(HEAD_DIM * kv, HEAD_DIM * (kv + 1)), slice(4 * tq * kv, 4 * tq * (kv + 1))) for kv in range(2)]


def _wattn_scores_t(kw, qst, bias_ref, n, t):
    kabs = (n - 1) * _WQ + lax.broadcasted_iota(jnp.int32, (_WK, 1), 0)
    st = _dot(kw, qst[...], _NN) + bias_ref[...]
    return jnp.where((kabs >= 0) & (kabs < t), st, NEG)


def _window_t(ref3, n):
    return jnp.concatenate([ref3[n], ref3[n + 1], ref3[n + 2]], axis=1)


def _blocks_transposed(ap):
    return ap.reshape(ap.shape[0] // _WQ, _WQ, KV_W).transpose(0, 2, 1)


def _wattn_fwd(q, kp, vp, bias_t, sink):
    t = q.shape[0]
    nq = t // _WQ
    tp = t + 2 * _WQ
    vpt = _blocks_transposed(vp)

    def body(q_ref, k_ref, vt_ref, b_ref, sk_ref, o_ref, lse_ref, qst):
        n = pl.program_id(0)
        _pairs_transposed(q_ref, qst, _WQ)
        kw = k_ref[pl.ds(pl.multiple_of(n * _WQ, _WQ), _WK), :]
        st = _wattn_scores_t(kw, qst, b_ref, n, t)
        sk = sk_ref[...]
        m = jnp.maximum(jnp.max(st, axis=0, keepdims=True), sk)
        pt = jnp.exp(st - m)
        l = jnp.sum(pt, axis=0, keepdims=True) + jnp.exp(sk - m)
        vwt = _window_t(vt_ref, n)
        halves = [_dot(vwt[rows, :], pt[:, cols], _NN) / l[:, cols] for rows, cols in _kv_quadrants(_WQ)]
        _pairs_from_transposed(halves, o_ref, _WQ)
        lse_ref[...] = m + jnp.log(l)

    sd = jax.ShapeDtypeStruct
    return pl.pallas_call(
        body, grid=(nq,),
        in_specs=[_row_spec(_WQ, Q_W), _par_spec(KV_W, tp), pl.BlockSpec(vpt.shape, lambda i: (0, 0, 0)),
                  _par_spec(_WR, _WK), _par_spec(_WR)],
        out_specs=[_row_spec(_WQ, Q_W), pl.BlockSpec((None, 1, _WR), lambda i: (i, 0, 0))],
        out_shape=[sd((t, Q_W), F32), sd((nq, 1, _WR), F32)],
        scratch_shapes=[pltpu.VMEM((LANES, _WR), BF16)],
        compiler_params=_cparams(("parallel",)), name="wattn_fwd",
    )(q, kp, vpt, bias_t, sink)


def _wattn_bwd(q, kp, vp, bias_t, sink, o, do, lse):
    t = q.shape[0]
    nq = t // _WQ
    tp = t + 2 * _WQ
    kpt = _blocks_transposed(kp)

    def body(q_ref, k_ref, v_ref, kt_ref, b_ref, sk_ref, o_ref, do_ref, lse_ref, dq_ref, dk_ref, dv_ref, db_ref,
             dsk_ref, qs, dos, qst, dost):
        n = pl.program_id(0)

        @pl.when(n == 0)
        def _():
            dk_ref[...] = jnp.zeros_like(dk_ref)
            dv_ref[...] = jnp.zeros_like(dv_ref)
            db_ref[...] = jnp.zeros_like(db_ref)
            dsk_ref[...] = jnp.zeros_like(dsk_ref)

        _stack_heads(q_ref, qs, _WQ)
        _stack_heads(do_ref, dos, _WQ)
        _pairs_transposed(q_ref, qst, _WQ)
        _pairs_transposed(do_ref, dost, _WQ)
        delta = []
        for i in range(4):
            cols = slice(LANES * i, LANES * (i + 1))
            prod_t = (do_ref[:, cols] * o_ref[:, cols]).T
            delta.append((jnp.sum(prod_t[:HEAD_DIM], axis=0, keepdims=True),
                          jnp.sum(prod_t[HEAD_DIM:], axis=0, keepdims=True)))
        dlt = jnp.concatenate([d[0] for d in delta] + [d[1] for d in delta], axis=1)
        off = pl.multiple_of(n * _WQ, _WQ)
        kw = k_ref[pl.ds(off, _WK), :]
        vw = v_ref[pl.ds(off, _WK), :]
        lse_v = lse_ref[...]
        pt = jnp.exp(_wattn_scores_t(kw, qst, b_ref, n, t) - lse_v)
        dpt = _dot(vw, dost[...], _NN)
        ds = pt * (dpt - dlt)
        db_ref[...] += ds
        dsk_ref[...] -= jnp.exp(sk_ref[...] - lse_v) * dlt
        dsb = ds.astype(BF16)
        dk_ref[pl.ds(off, _WK), :] += _dot(dsb, qs[...], _NN)
        dv_ref[pl.ds(off, _WK), :] += _dot(pt, dos[...], _NN)
        kwt = _window_t(kt_ref, n)
        halves = [_dot(kwt[rows, :], dsb[:, cols], _NN) for rows, cols in _kv_quadrants(_WQ)]
        _pairs_from_transposed(halves, dq_ref, _WQ)

    sd = jax.ShapeDtypeStruct
    qb = _row_spec(_WQ, Q_W)
    return pl.pallas_call(
        body, grid=(nq,),
        in_specs=[qb, _par_spec(KV_W, tp), _par_spec(KV_W, tp), pl.BlockSpec(kpt.shape, lambda i: (0, 0, 0)),
                  _par_spec(_WR, _WK), _par_spec(_WR), qb, qb, pl.BlockSpec((None, 1, _WR), lambda i: (i, 0, 0))],
        out_specs=[qb, _par_spec(KV_W, tp), _par_spec(KV_W, tp), _par_spec(_WR, _WK), _par_spec(_WR)],
        out_shape=[sd((t, Q_W), F32), sd((tp, KV_W), F32), sd((tp, KV_W), F32), sd((_WK, _WR), F32), sd((1, _WR), F32)],
        scratch_shapes=[pltpu.VMEM((_WR, LANES), BF16), pltpu.VMEM((_WR, LANES), BF16), pltpu.VMEM((LANES, _WR), BF16),
                        pltpu.VMEM((LANES, _WR), BF16)],
        compiler_params=_cparams(("arbitrary",)), name="wattn_bwd",
    )(q, kp, vp, kpt, bias_t, sink, o, do, lse)


def _bias_bucket_reduce(db0, db1, bucket):
    def body(a_ref, b_ref, bk_ref, o_ref):
        d = a_ref[...] + b_ref[...]
        bk = bk_ref[...]
        lane = lax.broadcasted_iota(jnp.int32, (1, LANES), 1)
        out = jnp.zeros((1, LANES), F32)
        for b in range(N_BUCKETS):
            tot = jnp.sum(jnp.sum(jnp.where(bk == b, d, 0.0), axis=-1, keepdims=True), axis=0, keepdims=True)
            out = out + jnp.where(lane == b, tot, 0.0)
        o_ref[...] = out

    hb = pl.BlockSpec((None, _WQ, _WK), lambda h: (h, 0, 0))
    return pl.pallas_call(
        body, grid=(8,), in_specs=[hb, hb, pl.BlockSpec((_WQ, _WK), lambda h: (0, 0))],
        out_specs=pl.BlockSpec((None, 1, LANES), lambda h: (h, 0, 0)),
        out_shape=jax.ShapeDtypeStruct((8, 1, LANES), F32),
        compiler_params=_cparams(("parallel",)), name="bias_bucket_reduce",
    )(db0.reshape(8, _WQ, _WK), db1.reshape(8, _WQ, _WK), bucket)


def _rope_tables(t):
    rows_n = t // GRID_W
    row = jnp.repeat(jnp.arange(rows_n, dtype=F32), GRID_W)
    col = jnp.tile(jnp.arange(GRID_W, dtype=F32), rows_n)
    half = HEAD_DIM // 2
    inv_freq = ROPE_THETA ** (-jnp.arange(0, half, 2, dtype=F32) / half)
    ang = jnp.concatenate([row[:, None] * inv_freq, col[:, None] * inv_freq], axis=-1)
    cos, sin = jnp.cos(ang), jnp.sin(ang)
    c64 = jnp.repeat(cos, 2, axis=-1)
    s64 = jnp.stack([-sin, sin], axis=-1).reshape(t, HEAD_DIM)
    return jnp.tile(c64, (1, 2)), jnp.tile(s64, (1, 2))


def _t5_bucket(rel):
    half = N_BUCKETS // 2
    max_exact = half // 2
    bucket = jnp.where(rel > 0, half, 0)
    rp = jnp.abs(rel)
    rpf = jnp.maximum(rp, 1).astype(jnp.float32)
    large = max_exact + (jnp.log(rpf / max_exact) / math.log(MAX_DISTANCE / max_exact)
                         * (half - max_exact)).astype(jnp.int32)
    large = jnp.minimum(large, half - 1)
    return bucket + jnp.where(rp < max_exact, rp, large)


def _window_tables(rel_bias):
    qpos = jnp.arange(_WQ, dtype=jnp.int32)
    kpos = jnp.arange(_WK, dtype=jnp.int32) - _WQ
    rel = kpos[None, :] - qpos[:, None]
    bucket = _t5_bucket(rel)
    bias = jnp.zeros((8, _WQ, _WK), F32)
    for b in range(N_BUCKETS):
        bias = jnp.where((bucket == b)[None], rel_bias[b][:, None, None], bias)
    bias = jnp.where((jnp.abs(rel) <= WINDOW)[None], bias, NEG)
    return bias.reshape(_WR, _WK).T, bucket


def _pad_rows(a):
    return jnp.pad(a, ((_WQ, _WQ), (0, 0)))


def _layer_fwd(x, p, tabs, gather=None):
    cos_t, sin_t, bias = tabs
    h, qa, ka, va, qb, kb, vb = _in_proj_prep(x, p["win"], cos_t, sin_t, p["qn"], p["kn"])
    if gather is None:
        oa, lse_a = _gattn_fwd(qa, ka, va)
    else:
        oa, lse_a, *gathered = _gattn_fwd(qa, ka, va, gather=gather[0])
        p = gather[1](gathered)
    kbp, vbp = _pad_rows(kb), _pad_rows(vb)
    ob, lse_b = _wattn_fwd(qb, kbp, vbp, bias, p["sink"])
    y = _outnorm_fwd(oa, ob, p["ga"], p["gb"])
    z1, x1, x1b = _mm_reduce(y[None], p["wout"][None], _NN, F32, "out_proj", res=x, res_scale=ALPHA,
                             ln=(p["ln1g"], p["ln1b"]))
    gu, hdn = _gate_up_glu(x1b, p["wgu"], p["cw"])
    z2, x2, _ = _mm_reduce(hdn, p["wd"], _NN, F32, "down_proj", res=x1, res_scale=ALPHA, ln=(p["ln2g"], p["ln2b"]))
    saved = dict(x=x, h=h, qa=qa, ka=ka, va=va, qb=qb, kbp=kbp, vbp=vbp, oa=oa, ob=ob, lse_a=lse_a, lse_b=lse_b,
                 y=y, z1=z1, x1b=x1b, gu=gu, hdn=hdn, z2=z2)
    return x2, saved


def _block_grads(g, names=("w_in", "w_out", "w_gate", "w_up", "w_down")):
    make = dict(
        w_in=lambda: _col_blocks(_in_cols_to_pairs(g["win"], _from_pairs), IN_SH),
        w_out=lambda: _mix_rows_to_pairs(g["wout"], _from_pairs).reshape(N_SHARD, OUT_SH, D_MODEL),
        w_gate=lambda: g["wg"], w_up=lambda: g["wu"], w_down=lambda: g["wd"])
    return [make[n]() for n in names]


def _layer_bwd(dx2, p, s, tabs, layer=0, pending=None):
    cos_t, sin_t, bias = tabs
    t = dx2.shape[0]
    dz2, dz2b, dln2g, dln2b = _ln_bwd(dx2, s["z2"], p["ln2g"])
    dwd = _mm_tn_blocks(s["hdn"], dz2b, "down_dw")
    dgu, stats = _ffn_mid_bwd(s["gu"], dz2b, p["wd"], p["cw"])
    dgu = dgu.reshape(2 * N_SHARD, t, FF_SH)
    dx1 = _mm_reduce(dgu, p["wgu"], _NT, F32, "gate_up_dx", res=dz2, res_scale=ALPHA)
    dwg = _mm_tn_blocks(dgu, s["x1b"], "gate_dw", blk=0)
    dwu = _mm_tn_blocks(dgu, s["x1b"], "up_dw", blk=1)
    dz1, dz1b, dln1g, dln1b = _ln_bwd(dx1, s["z1"], p["ln1g"])
    dy = _mm_nt(dz1b, p["wout"], F32, "out_dx")
    dwout = _mm_tn(s["y"], dz1b, "out_dw")
    doa, dob, dga, dgb = _outnorm_bwd(dy, s["oa"], s["ob"], p["ga"], p["gb"])
    slots = None
    if pending is None:
        dqa, dka, dva = _gattn_bwd(s["qa"], s["ka"], s["va"], s["oa"], doa, s["lse_a"])
    else:
        mine = _block_grads(dict(wout=dwout, wg=dwg, wu=dwu, wd=dwd), ("w_out", "w_gate", "w_up", "w_down"))
        todo = list(pending) + [(o + 1, layer, g) for o, g in enumerate(mine)]
        dqa, dka, dva, *slots = _gattn_bwd(s["qa"], s["ka"], s["va"], s["oa"], doa, s["lse_a"],
                                           scatter=([(o, l) for o, l, _ in todo], [g for _, _, g in todo]))
    dqb, dkbp, dvbp, dbias, dsink = _wattn_bwd(s["qb"], s["kbp"], s["vbp"], bias, p["sink"], s["ob"], dob, s["lse_b"])
    dkb = lax.slice_in_dim(dkbp, _WQ, _WQ + t, axis=0)
    dvb = lax.slice_in_dim(dvbp, _WQ, _WQ + t, axis=0)
    dh, dqn, dkn = _prep_bwd(s["h"], cos_t, sin_t, p["qn"], p["kn"], dqa, dka, dva, dqb, dkb, dvb)
    dx = _mm_nt(dh, p["win"], F32, "in_dx", res=dz1, res_scale=ALPHA)
    dwin = _mm_tn(s["x"], dh, "in_dw")
    grads = dict(win=dwin, wout=dwout, wg=dwg, wu=dwu, wd=dwd, stats=stats, qn=dqn, kn=dkn, ga=dga, gb=dgb,
                 ln1g=dln1g, ln1b=dln1b, ln2g=dln2g, ln2b=dln2b, bias=dbias, sink=dsink, slots=slots)
    return dx, grads


def _prep_layer_params(l, win, wout, wg, wu, wd, cw, q_norm, k_norm, sink, out_norm_a, out_norm_b, conv_b,
                       ln1_g, ln1_b, ln2_g, ln2_b):
    win_full = win.transpose(1, 0, 2).reshape(D_MODEL, IN_COLS)
    row = lambda v: v.reshape(1, -1)
    late = {}
    if wout is not None:
        late = dict(
            wout=_mix_rows_to_pairs(wout.reshape(D_MODEL, D_MODEL)), wgu=jnp.concatenate([wg, wu], axis=0), wd=wd,
            cw=jnp.pad(cw, ((0, 0), (0, 5), (0, 0)))
            + jnp.pad(conv_b[l].reshape(N_SHARD, 1, FF_SH), ((0, 0), (3, 4), (0, 0))))
    return dict(
        late, win=_in_cols_to_pairs(win_full),
        qn=row(jnp.tile(q_norm[l], 2)), kn=row(jnp.tile(k_norm[l], 2)),
        ga=row(_to_pairs(out_norm_a[l], 0)), gb=row(_to_pairs(out_norm_b[l], 0)),
        ln1g=row(ln1_g[l]), ln1b=row(ln1_b[l]), ln2g=row(ln2_g[l]), ln2b=row(ln2_b[l]),
        sink=jnp.repeat(sink[l], _WQ).reshape(1, _WR))


def _local_step(x, tgt, params, rel_bias, gather=None, scatter=False):
    t = x.shape[0]
    cos_t, sin_t = _rope_tables(t)
    bias, bucket = _window_tables(rel_bias)
    tabs = (cos_t, sin_t, bias)
    saved = []
    for l in range(DEPTH):
        x, s = _layer_fwd(x, params[l], tabs, gather if l == 0 else None)
        saved.append(s)
    dx, loss = _loss_grad(x, tgt)
    grads = [None] * DEPTH
    for l in reversed(range(DEPTH)):
        pending = None
        if scatter and l == 0:
            pending = [(o, 1, g) for o, g in enumerate(_block_grads(grads[1]))]
        dx, grads[l] = _layer_bwd(dx, params[l], saved[l], tabs, l, pending)
    dbucket = _bias_bucket_reduce(grads[0]["bias"].T, grads[1]["bias"].T, bucket)
    return loss, dx, grads, dbucket


_ANY = pl.BlockSpec(memory_space=pl.ANY)
_MESH = pl.DeviceIdType.MESH


def _mesh_pos():
    return lax.axis_index("x"), lax.axis_index("y"), lax.axis_index("c")


def _other_chips(x, y):
    return [(1 - x, y), (x, 1 - y), (1 - x, 1 - y)]


class _Exchange:
    def __init__(self, local, sends, recvs):
        self.local, self.sends, self.recvs = local, sends, recvs

    def start(self):
        for cp in self.local + self.sends:
            cp.start()

    def wait(self):
        for cp in self.recvs:
            cp.wait_recv()
        for cp in self.sends:
            cp.wait_send()
        for cp in self.local:
            cp.wait()


def _exchange_sems(n):
    return [pltpu.SemaphoreType.DMA((n, 3)), pltpu.SemaphoreType.DMA((n, 3)), pltpu.SemaphoreType.DMA((n,))]


def _gather_exchange(ins, outs, send, recv, loc):
    x, y, c = _mesh_pos()
    me = 2 * x + y
    chips = _other_chips(x, y)

    def remote(i, k, block):
        px, py = chips[k]
        return pltpu.make_async_remote_copy(ins[i], outs[i].at[block], send.at[i, k], recv.at[i, k],
                                            device_id=(px, py, c), device_id_type=_MESH)

    n = len(ins)
    local = [pltpu.make_async_copy(ins[i], outs[i].at[me], loc.at[i]) for i in range(n)]
    sends = [remote(i, k, me) for i in range(n) for k in range(3)]
    recvs = [remote(i, k, 2 * chips[k][0] + chips[k][1]) for i in range(n) for k in range(3)]
    return _Exchange(local, sends, recvs)


def _scatter_exchange(items, ins, outs, send, recv, loc):
    x, y, c = _mesh_pos()
    me = 2 * x + y
    chips = _other_chips(x, y)

    def remote(j, k):
        o, l = items[j]
        px, py = chips[k]
        return pltpu.make_async_remote_copy(ins[j].at[2 * px + py], outs[o].at[k, l], send.at[j, k], recv.at[j, k],
                                            device_id=(px, py, c), device_id_type=_MESH)

    local = [pltpu.make_async_copy(ins[j].at[me], outs[o].at[3, l], loc.at[j]) for j, (o, l) in enumerate(items)]
    sends = [remote(j, k) for j in range(len(items)) for k in range(3)]
    return _Exchange(local, sends, sends)


def _gathered_shapes(shards):
    return [jax.ShapeDtypeStruct((N_SHARD,) + s.shape, s.dtype) for s in shards]


def _slot_shapes(blocks):
    return [jax.ShapeDtypeStruct((N_SHARD, DEPTH) + g.shape[1:], g.dtype) for g in blocks]


def _gather_shards(shards):
    n = len(shards)

    def body(*refs):
        ex = _gather_exchange(refs[:n], refs[n:2 * n], *refs[2 * n:])
        ex.start()
        ex.wait()

    return pl.pallas_call(
        body, in_specs=[_ANY] * n, out_specs=[_ANY] * n, out_shape=_gathered_shapes(shards),
        scratch_shapes=_exchange_sems(n), name="gather_weights",
    )(*shards)


def _scatter_into(items, grads, slots):
    n, ns = len(grads), len(slots)

    def body(*refs):
        ex = _scatter_exchange(items, refs[:n], refs[n + ns:n + 2 * ns], *refs[n + 2 * ns:])
        ex.start()
        ex.wait()

    return pl.pallas_call(
        body, in_specs=[_ANY] * (n + ns), out_specs=[_ANY] * ns,
        out_shape=[jax.ShapeDtypeStruct(s.shape, s.dtype) for s in slots],
        input_output_aliases={n + i: i for i in range(ns)},
        scratch_shapes=_exchange_sems(n), name="scatter_grads",
    )(*grads, *slots)


def _swap_with_sibling(parts):
    n = len(parts)

    def body(*refs):
        ins, outs = refs[:n], refs[n:2 * n]
        send, recv = refs[2 * n:]
        x, y, c = _mesh_pos()
        copies = [pltpu.make_async_remote_copy(ins[i], outs[i], send.at[i], recv.at[i], device_id=(x, y, 1 - c),
                                               device_id_type=_MESH) for i in range(n)]
        for cp in copies:
            cp.start()
        for cp in copies:
            cp.wait_recv()
        for cp in copies:
            cp.wait_send()

    return pl.pallas_call(
        body, in_specs=[_ANY] * n, out_specs=[_ANY] * n,
        out_shape=[jax.ShapeDtypeStruct(p.shape, p.dtype) for p in parts],
        scratch_shapes=[pltpu.SemaphoreType.DMA((n,)), pltpu.SemaphoreType.DMA((n,))],
        name="swap_sibling",
    )(*parts)


N_DEV = 8


def _allreduce_small(packed):
    rows = packed.shape[0]

    def body(in_ref, out_ref, buf, send, recv, loc):
        x, y, c = _mesh_pos()
        me = 4 * x + 2 * y + c
        own = pltpu.make_async_copy(in_ref, buf.at[me], loc)
        own.start()

        def remote(m, block):
            peer = (x ^ (m >> 2), y ^ ((m >> 1) & 1), c ^ (m & 1))
            return pltpu.make_async_remote_copy(in_ref, buf.at[block], send.at[m - 1], recv.at[m - 1],
                                                device_id=peer, device_id_type=_MESH)

        sends = [remote(m, me) for m in range(1, N_DEV)]
        for cp in sends:
            cp.start()
        for m in range(1, N_DEV):
            remote(m, me ^ m).wait_recv()
        for cp in sends:
            cp.wait_send()
        own.wait()
        tot = buf[0]
        for d in range(1, N_DEV):
            tot = tot + buf[d]
        out_ref[...] = tot

    vm = pl.BlockSpec(memory_space=pltpu.VMEM)
    return pl.pallas_call(
        body, in_specs=[vm], out_specs=vm, out_shape=jax.ShapeDtypeStruct((rows, LANES), F32),
        scratch_shapes=[pltpu.VMEM((N_DEV, rows, LANES), F32), pltpu.SemaphoreType.DMA((N_DEV - 1,)),
                        pltpu.SemaphoreType.DMA((N_DEV - 1,)), pltpu.SemaphoreType.DMA(())],
        name="allreduce_small",
    )(packed)


def _shard_rows(r):
    return r // 2 if r % 32 == 0 else r


def _sum_slots(slots):
    _, _, r, cdim = slots.shape
    tr = _shard_rows(r)

    def body(a_ref, b_ref, c_ref, d_ref, o_ref):
        up = lambda ref: ref[...].astype(F32)
        o_ref[...] = ((up(d_ref) + up(a_ref)) + up(b_ref)) + up(c_ref)

    def spec(k):
        return pl.BlockSpec((None, None, tr, cdim), lambda l, i: (k, l, i, 0))

    return pl.pallas_call(
        body, grid=(DEPTH, r // tr), in_specs=[spec(0), spec(1), spec(2), spec(3)],
        out_specs=pl.BlockSpec((None, tr, cdim), lambda l, i: (l, i, 0)),
        out_shape=jax.ShapeDtypeStruct((DEPTH, r, cdim), F32),
        compiler_params=_cparams(("parallel", "parallel")), name="sum_slots",
    )(slots, slots, slots, slots)


def _adamw_math(w, g, m, v):
    m = ADAM_B1 * m + (1.0 - ADAM_B1) * g
    v = ADAM_B2 * v + (1.0 - ADAM_B2) * (g * g)
    m_hat = m / (1.0 - ADAM_B1 ** ADAM_STEP)
    v_hat = v / (1.0 - ADAM_B2 ** ADAM_STEP)
    delta = -ADAM_LR * (m_hat / (jnp.sqrt(v_hat) + ADAM_EPS) + ADAM_WD * w)
    return delta, m, v


def _adamw_big(ga, gb, w, m, v):
    _, r, cdim = w.shape
    tr = _shard_rows(r)

    def body(ga_ref, gb_ref, w_ref, m_ref, v_ref, g_out, d_out, m_out, v_out):
        g = ga_ref[...] + gb_ref[...]
        d, mn, vn = _adamw_math(w_ref[...], g, m_ref[...], v_ref[...])
        g_out[...] = g
        d_out[...] = d
        m_out[...] = mn
        v_out[...] = vn

    spec = pl.BlockSpec((None, tr, cdim), lambda l, i: (l, i, 0))
    shp = jax.ShapeDtypeStruct(w.shape, F32)
    return pl.pallas_call(
        body, grid=(DEPTH, r // tr), in_specs=[spec] * 5, out_specs=[spec] * 4, out_shape=[shp] * 4,
        compiler_params=_cparams(("parallel", "parallel")), name="adamw_big",
    )(ga, gb, w, m, v)


def _adamw_small(ws, gs, ms, vs):
    n = len(ws)

    def body(*refs):
        w_r, g_r, m_r, v_r = (refs[k * n:(k + 1) * n] for k in range(4))
        d_o, m_o, v_o = (refs[(4 + k) * n:(5 + k) * n] for k in range(3))
        for i in range(n):
            d, mn, vn = _adamw_math(w_r[i][...], g_r[i][...], m_r[i][...], v_r[i][...])
            d_o[i][...] = d
            m_o[i][...] = mn
            v_o[i][...] = vn

    vm = pl.BlockSpec(memory_space=pltpu.VMEM)
    shp = [jax.ShapeDtypeStruct(w.shape, F32) for w in ws]
    outs = pl.pallas_call(
        body, in_specs=[vm] * (4 * n), out_specs=[vm] * (3 * n), out_shape=shp * 3, name="adamw_small",
    )(*ws, *gs, *ms, *vs)
    return outs[:n], outs[n:2 * n], outs[2 * n:]


def _tile_rows(a):
    a = a.reshape(-1, LANES)
    pad = (-a.shape[0]) % 8
    return jnp.pad(a, ((0, pad), (0, 0))) if pad else a


_SMALL_LAYER_PARTS = (("qn", 8), ("kn", 8), ("sink", 8), ("ga", 8), ("gb", 8), ("ln1g", 8), ("ln1b", 8),
                      ("ln2g", 8), ("ln2b", 8), ("stats", N_SHARD * 8 * FF_SH // LANES))
_SMALL_HEAD_ROWS = 16
_SMALL_LAYER_ROWS = sum(r for _, r in _SMALL_LAYER_PARTS)


def _pack_small(loss, dbucket, grads):
    parts = [_tile_rows(loss), _tile_rows(dbucket)]
    for l in range(DEPTH):
        parts += [_tile_rows(grads[l][name]) for name, _ in _SMALL_LAYER_PARTS]
    return jnp.concatenate(parts, axis=0)


def _unpack_small(tot, chip):
    out = dict(loss=tot[0, 0], rel_bias=tot[8:16, :N_BUCKETS].T)
    per = {name: [] for name, _ in _SMALL_LAYER_PARTS}
    for l in range(DEPTH):
        base = _SMALL_HEAD_ROWS + l * _SMALL_LAYER_ROWS
        for name, rows in _SMALL_LAYER_PARTS:
            per[name].append(tot[base:base + rows])
            base += rows
    fold = lambda v: v[0, :HEAD_DIM] + v[0, HEAD_DIM:]
    out["q_norm"] = jnp.stack([fold(v) for v in per["qn"]])
    out["k_norm"] = jnp.stack([fold(v) for v in per["kn"]])
    out["sink"] = jnp.stack([jnp.sum(v, axis=1) for v in per["sink"]])
    out["out_norm_a"] = jnp.stack([_from_pairs(v[:4].reshape(Q_W), 0) for v in per["ga"]])
    out["out_norm_b"] = jnp.stack([_from_pairs(v[:4].reshape(Q_W), 0) for v in per["gb"]])
    for name, key in (("ln1_g", "ln1g"), ("ln1_b", "ln1b"), ("ln2_g", "ln2g"), ("ln2_b", "ln2b")):
        out[name] = jnp.stack([v.reshape(D_MODEL) for v in per[key]])
    stats = [v.reshape(N_SHARD, 8, FF_SH) for v in per["stats"]]
    out["conv_b"] = jnp.stack([s[:, 0, :].reshape(D_FF) for s in stats])
    out["conv_w"] = jnp.stack([lax.dynamic_index_in_dim(s, chip, 0, keepdims=False)[1:4] for s in stats])
    return out


_WEIGHTS = ("rel_bias", "w_in", "q_norm", "k_norm", "sink", "out_norm_a", "out_norm_b", "w_out", "ln1_g", "ln1_b",
            "w_gate", "w_up", "conv_w", "conv_b", "w_down", "ln2_g", "ln2_b")
_BIG = ("w_in", "w_out", "w_gate", "w_up", "w_down")
_SMALL = tuple(n for n in _WEIGHTS if n not in _BIG)


def _col_blocks(g, n):
    return g.reshape(g.shape[0], N_SHARD, n).transpose(1, 0, 2)


def kernel(x, rel_bias, w_in, q_norm, k_norm, sink, out_norm_a, out_norm_b, w_out, ln1_g, ln1_b, w_gate, w_up, conv_w, conv_b, w_down, ln2_g, ln2_b, loss_target, m_rel_bias, m_w_in, m_q_norm, m_k_norm, m_sink, m_out_norm_a, m_out_norm_b, m_w_out, m_ln1_g, m_ln1_b, m_w_gate, m_w_up, m_conv_w, m_conv_b, m_w_down, m_ln2_g, m_ln2_b, v_rel_bias, v_w_in, v_q_norm, v_k_norm, v_sink, v_out_norm_a, v_out_norm_b, v_w_out, v_ln1_g, v_ln1_b, v_w_gate, v_w_up, v_conv_w, v_conv_b, v_w_down, v_ln2_g, v_ln2_b):
    w = dict(rel_bias=rel_bias, w_in=w_in, q_norm=q_norm, k_norm=k_norm, sink=sink, out_norm_a=out_norm_a,
             out_norm_b=out_norm_b, w_out=w_out, ln1_g=ln1_g, ln1_b=ln1_b, w_gate=w_gate, w_up=w_up, conv_w=conv_w,
             conv_b=conv_b, w_down=w_down, ln2_g=ln2_g, ln2_b=ln2_b)
    m = dict(rel_bias=m_rel_bias, w_in=m_w_in, q_norm=m_q_norm, k_norm=m_k_norm, sink=m_sink, out_norm_a=m_out_norm_a,
             out_norm_b=m_out_norm_b, w_out=m_w_out, ln1_g=m_ln1_g, ln1_b=m_ln1_b, w_gate=m_w_gate, w_up=m_w_up,
             conv_w=m_conv_w, conv_b=m_conv_b, w_down=m_w_down, ln2_g=m_ln2_g, ln2_b=m_ln2_b)
    v = dict(rel_bias=v_rel_bias, w_in=v_w_in, q_norm=v_q_norm, k_norm=v_k_norm, sink=v_sink, out_norm_a=v_out_norm_a,
             out_norm_b=v_out_norm_b, w_out=v_w_out, ln1_g=v_ln1_g, ln1_b=v_ln1_b, w_gate=v_w_gate, w_up=v_w_up,
             conv_w=v_conv_w, conv_b=v_conv_b, w_down=v_w_down, ln2_g=v_ln2_g, ln2_b=v_ln2_b)
    chip = 2 * lax.axis_index("x") + lax.axis_index("y")

    small_w = (q_norm, k_norm, sink, out_norm_a, out_norm_b, conv_b, ln1_g, ln1_b, ln2_g, ln2_b)
    (win0,) = _gather_shards([w_in[0].astype(BF16)])
    later = ([w[name][0].astype(BF16) for name in _BIG[1:]] + [w[name][1].astype(BF16) for name in _BIG] + [conv_w])
    params = [_prep_layer_params(0, win0, None, None, None, None, None, *small_w), None]

    def finish(g):
        wout0, wg0, wu0, wd0, win1, wout1, wg1, wu1, wd1, cw_all = g
        params[0] = _prep_layer_params(0, win0, wout0, wg0, wu0, wd0, cw_all[:, 0], *small_w)
        params[1] = _prep_layer_params(1, win1, wout1, wg1, wu1, wd1, cw_all[:, 1], *small_w)
        return params[0]

    loss, dx, grads, dbucket = _local_step(x[0], loss_target[0], params, rel_bias, gather=(later, finish),
                                           scatter=True)

    small = _unpack_small(_allreduce_small(_pack_small(loss, dbucket, grads)), chip)

    slots = list(grads[0]["slots"])
    slots[0] = _scatter_into([(0, 0)], _block_grads(grads[0], ("w_in",)), [slots[0]])[0]
    partial = [_sum_slots(s) for s in slots]
    other = _swap_with_sibling(partial)

    grad, delta, new_m, new_v = {}, {}, {}, {}
    for i, name in enumerate(_BIG):
        fix = (lambda a: jnp.swapaxes(a, 1, 2)) if name in ("w_gate", "w_up") else (lambda a: a)
        outs = _adamw_big(partial[i], other[i], fix(w[name]), fix(m[name]), fix(v[name]))
        grad[name], delta[name], new_m[name], new_v[name] = [fix(o) for o in outs]
    flat2 = lambda a: a.reshape(-1, a.shape[-1])
    ds, ms, vs = _adamw_small([flat2(w[n]) for n in _SMALL], [flat2(small[n]) for n in _SMALL],
                              [flat2(m[n]) for n in _SMALL], [flat2(v[n]) for n in _SMALL])
    for i, name in enumerate(_SMALL):
        grad[name] = small[name]
        delta[name] = ds[i].reshape(w[name].shape)
        new_m[name] = ms[i].reshape(w[name].shape)
        new_v[name] = vs[i].reshape(w[name].shape)

    return (small["loss"], dx[None], *[grad[n] for n in _WEIGHTS], *[delta[n] for n in _WEIGHTS],
            *[new_m[n] for n in _WEIGHTS], *[new_v[n] for n in _WEIGHTS])
```
